```python
import math
import jax, jax.numpy as jnp
from jax import lax
import numpy as np

D_MODEL = 1024
BATCH = 8
SEQ = 16384
DEPTH = 2

HEAD_DIM = 64
CONV_CH = 256
CONV_GROUPS = CONV_CH // HEAD_DIM
CONV_WIDTH = 31
ATTN_HEADS = 8
ATTN_CH = ATTN_HEADS * HEAD_DIM
DILATED_PATTERNS = ((128, 1), (512, 4), (2048, 16))
ATTN_BLOCK = 128
N_BUCKETS = 32
MAX_DISTANCE = 2048
GMLP_CH = 256
GMLP_GROUPS = 4
GMLP_GROUP_DIM = GMLP_CH // GMLP_GROUPS
CHUNK = 128
MIX_CH = CONV_CH + ATTN_CH + GMLP_CH
IN_CH = 2 * CONV_CH + 3 * ATTN_CH + 2 * GMLP_CH
D_FF = 2816
FFN_CONV_WIDTH = 3
LN_EPS = 1e-5
ALPHA = (2.0 * DEPTH) ** 0.25
BETA = (8.0 * DEPTH) ** -0.25

kernel_name = "hymba_style_conv_dilattn_gmlp_convffn_deepnorm"


def _layernorm(x, g, b):
    xf = x.astype(jnp.float32)
    mu = jnp.mean(xf, axis=-1, keepdims=True)
    var = jnp.mean(jnp.square(xf - mu), axis=-1, keepdims=True)
    y = (xf - mu) * lax.rsqrt(var + LN_EPS)
    return (y * g.astype(jnp.float32) + b.astype(jnp.float32)).astype(x.dtype)


def _causal_dwconv(x, w, b):
    K, C = w.shape
    y = lax.conv_general_dilated(x, w[:, None, :].astype(x.dtype), window_strides=(1,),
                                 padding=((K - 1, 0),), dimension_numbers=("NWC", "WIO", "NWC"),
                                 feature_group_count=C)
    return y + b.astype(x.dtype)


def _t5_bucket(dist):
    max_exact = N_BUCKETS // 2
    d = np.maximum(dist, 1).astype(np.float64)
    large = max_exact + (np.log(d / max_exact) / math.log(MAX_DISTANCE / max_exact)
                         * (N_BUCKETS - max_exact)).astype(np.int32)
    large = np.minimum(large, N_BUCKETS - 1)
    return np.where(dist < max_exact, dist, large).astype(np.int32)


def _conv_module(a_in, dw_w, dw_b, ln_g, ln_b):
    a, gate = jnp.split(a_in, 2, axis=-1)
    h = a * jax.nn.sigmoid(gate)
    h = _causal_dwconv(h, dw_w, dw_b)
    h = _layernorm(h, ln_g, ln_b)
    return jax.nn.silu(h)


def _dilated_branch(q, k, v, rel_table, window, dilation):
    B, S, H, Dh = q.shape
    L = S // dilation
    n_win = window // dilation
    nb = -(-L // ATTN_BLOCK)
    Lp = nb * ATTN_BLOCK

    def to_blocks(t):
        t = t.reshape(B, L, dilation, H, Dh).transpose(0, 2, 1, 3, 4)
        t = jnp.pad(t, ((0, 0), (0, 0), (0, Lp - L), (0, 0), (0, 0)))
        return t.reshape(B, dilation, nb, ATTN_BLOCK, H, Dh)

    def with_prev(t):
        prev = jnp.pad(t, ((0, 0), (0, 0), (1, 0), (0, 0), (0, 0), (0, 0)))[:, :, :-1]
        return jnp.concatenate([prev, t], axis=3)

    qb = to_blocks(q)
    kk = with_prev(to_blocks(k))
    vv = with_prev(to_blocks(v))

    qi = np.arange(ATTN_BLOCK)[:, None]
    kj = np.arange(2 * ATTN_BLOCK)[None, :]
    dist_sub = qi + ATTN_BLOCK - kj
    kpos = np.arange(nb)[:, None, None] * ATTN_BLOCK + kj[None] - ATTN_BLOCK
    valid = (dist_sub >= 0)[None] & (dist_sub <= n_win)[None] & (kpos >= 0)
    bucket = _t5_bucket(np.clip(dist_sub, 0, None) * dilation)
    bias = jnp.transpose(rel_table[bucket].astype(jnp.float32), (2, 0, 1))

    logits = jnp.einsum("brnqhd,brnkhd->brnhqk", qb, kk).astype(jnp.float32) + bias
    logits = jnp.where(jnp.asarray(valid)[None, None, :, None], logits, -jnp.inf)
    m = jnp.max(logits, axis=-1)
    p = jnp.exp(logits - m[..., None])
    s = jnp.sum(p, axis=-1)
    o = jnp.einsum("brnhqk,brnkhd->brnqhd", p, vv.astype(jnp.float32))

    def from_blocks(t):
        rest = t.shape[4:]
        t = t.reshape((B, dilation, Lp) + rest)[:, :, :L]
        t = jnp.moveaxis(t, 1, 2)
        return t.reshape((B, S) + rest)

    m = from_blocks(jnp.swapaxes(m, 3, 4))
    s = from_blocks(jnp.swapaxes(s, 3, 4))
    o = from_blocks(o)
    return m, s, o


def _dilated_attention(q, k, v, rel_table):
    q = q * (HEAD_DIM ** -0.5)
    outs = [_dilated_branch(q, k, v, rel_table, w, d) for (w, d) in DILATED_PATTERNS]
    ms = jnp.stack([r[0] for r in outs])
    big_m = jnp.max(ms, axis=0)
    wts = jnp.exp(ms - big_m)
    den = sum(wts[i] * outs[i][1] for i in range(len(outs)))
    num = sum(wts[i][..., None] * outs[i][2] for i in range(len(outs)))
    return (num / den[..., None]).astype(q.dtype)


def _spatial_gating(c_in, ln_g, ln_b, w_s, b_s):
    B, S, _ = c_in.shape
    u, v = jnp.split(c_in, 2, axis=-1)
    v = _layernorm(v, ln_g, ln_b)
    vc = v.reshape(B, S // CHUNK, CHUNK, GMLP_GROUPS, GMLP_GROUP_DIM)
    w = jnp.tril(w_s).astype(v.dtype)
    mixed = jnp.einsum("gts,bcsgd->bctgd", w, vc) + b_s.T.astype(v.dtype)[None, None, :, :, None]
    return u * mixed.reshape(B, S, GMLP_CH)


def _conv_ffn(x, w_up, b_up, conv_w, conv_b, w_down, b_down):
    h = x @ w_up + b_up
    h = _causal_dwconv(h, conv_w, conv_b)
    g, val = jnp.split(h, 2, axis=-1)
    return (jax.nn.silu(g) * val) @ w_down + b_down


def _fwd_setup_inputs(seed: int = 0) -> dict:
    key = jax.random.key(seed)
    ks = jax.random.split(key, 24)
    f32 = jnp.float32

    def nrm(k, shape, scale):
        return jax.random.normal(k, shape, f32) * scale

    x = nrm(ks[0], (BATCH, SEQ, D_MODEL), 1.0)
    w_in = nrm(ks[1], (DEPTH, D_MODEL, IN_CH), D_MODEL ** -0.5)
    v_lo = 2 * CONV_CH + 2 * ATTN_CH
    v_scale = jnp.ones((IN_CH,), f32).at[v_lo:v_lo + ATTN_CH].set(BETA)
    w_in = w_in * v_scale
    b_in = nrm(ks[2], (DEPTH, IN_CH), 0.02)
    conv_dw_w = nrm(ks[3], (DEPTH, CONV_WIDTH, CONV_CH), CONV_WIDTH ** -0.5)
    conv_dw_b = nrm(ks[4], (DEPTH, CONV_CH), 0.02)
    conv_ln_g = 1.0 + nrm(ks[5], (DEPTH, CONV_CH), 0.02)
    conv_ln_b = nrm(ks[6], (DEPTH, CONV_CH), 0.02)
    rel_bias_table = nrm(ks[7], (N_BUCKETS, ATTN_HEADS), 0.5)
    gmlp_ln_g = 1.0 + nrm(ks[8], (DEPTH, GMLP_CH), 0.02)
    gmlp_ln_b = nrm(ks[9], (DEPTH, GMLP_CH), 0.02)
    gmlp_w_s = nrm(ks[10], (DEPTH, GMLP_GROUPS, CHUNK, CHUNK), CHUNK ** -0.5)
    gmlp_b_s = 1.0 + nrm(ks[11], (DEPTH, GMLP_GROUPS, CHUNK), 0.02)
    w_out = nrm(ks[12], (DEPTH, MIX_CH, D_MODEL), MIX_CH ** -0.5 * BETA)
    b_out = nrm(ks[13], (DEPTH, D_MODEL), 0.02)
    ln1_g = 1.0 + nrm(ks[14], (DEPTH, D_MODEL), 0.02)
    ln1_b = nrm(ks[15], (DEPTH, D_MODEL), 0.02)
    ffn_w_up = nrm(ks[16], (DEPTH, D_MODEL, 2 * D_FF), D_MODEL ** -0.5)
    ffn_b_up = nrm(ks[17], (DEPTH, 2 * D_FF), 0.02)
    ffn_conv_w = nrm(ks[18], (DEPTH, FFN_CONV_WIDTH, 2 * D_FF), FFN_CONV_WIDTH ** -0.5)
    ffn_conv_b = nrm(ks[19], (DEPTH, 2 * D_FF), 0.02)
    ffn_w_down = nrm(ks[20], (DEPTH, D_FF, D_MODEL), D_FF ** -0.5 * BETA)
    ffn_b_down = nrm(ks[21], (DEPTH, D_MODEL), 0.02)
    ln2_g = 1.0 + nrm(ks[22], (DEPTH, D_MODEL), 0.02)
    ln2_b = nrm(ks[23], (DEPTH, D_MODEL), 0.02)
    return {"x": x, "w_in": w_in, "b_in": b_in, "conv_dw_w": conv_dw_w, "conv_dw_b": conv_dw_b,
            "conv_ln_g": conv_ln_g, "conv_ln_b": conv_ln_b, "rel_bias_table": rel_bias_table,
            "gmlp_ln_g": gmlp_ln_g, "gmlp_ln_b": gmlp_ln_b, "gmlp_w_s": gmlp_w_s, "gmlp_b_s": gmlp_b_s,
            "w_out": w_out, "b_out": b_out, "ln1_g": ln1_g, "ln1_b": ln1_b,
            "ffn_w_up": ffn_w_up, "ffn_b_up": ffn_b_up, "ffn_conv_w": ffn_conv_w, "ffn_conv_b": ffn_conv_b,
            "ffn_w_down": ffn_w_down, "ffn_b_down": ffn_b_down, "ln2_g": ln2_g, "ln2_b": ln2_b}


def _fwd_reference(x, w_in, b_in, conv_dw_w, conv_dw_b, conv_ln_g, conv_ln_b, rel_bias_table,
              gmlp_ln_g, gmlp_ln_b, gmlp_w_s, gmlp_b_s, w_out, b_out, ln1_g, ln1_b,
              ffn_w_up, ffn_b_up, ffn_conv_w, ffn_conv_b, ffn_w_down, ffn_b_down, ln2_g, ln2_b):
    B, S, _ = x.shape
    split_pts = [2 * CONV_CH, 2 * CONV_CH + ATTN_CH, 2 * CONV_CH + 2 * ATTN_CH,
                 2 * CONV_CH + 3 * ATTN_CH]
    for l in range(DEPTH):
        h = x @ w_in[l] + b_in[l]
        a_in, q, k, v, c_in = jnp.split(h, split_pts, axis=-1)
        conv_out = _conv_module(a_in, conv_dw_w[l], conv_dw_b[l], conv_ln_g[l], conv_ln_b[l])
        attn_out = _dilated_attention(q.reshape(B, S, ATTN_HEADS, HEAD_DIM),
                                      k.reshape(B, S, ATTN_HEADS, HEAD_DIM),
                                      v.reshape(B, S, ATTN_HEADS, HEAD_DIM),
                                      rel_bias_table).reshape(B, S, ATTN_CH)
        gm_out = _spatial_gating(c_in, gmlp_ln_g[l], gmlp_ln_b[l], gmlp_w_s[l], gmlp_b_s[l])
        mix = jnp.concatenate([conv_out, attn_out, gm_out], axis=-1) @ w_out[l] + b_out[l]
        x = _layernorm(ALPHA * x + mix, ln1_g[l], ln1_b[l])
        ffn = _conv_ffn(x, ffn_w_up[l], ffn_b_up[l], ffn_conv_w[l], ffn_conv_b[l],
                        ffn_w_down[l], ffn_b_down[l])
        x = _layernorm(ALPHA * x + ffn, ln2_g[l], ln2_b[l])
    return x


import jax as _jax
import jax.numpy as _jnp

TWIN_FORMAT = 'train_step'
FWD_PARAMS = ['x', 'w_in', 'b_in', 'conv_dw_w', 'conv_dw_b', 'conv_ln_g', 'conv_ln_b', 'rel_bias_table', 'gmlp_ln_g', 'gmlp_ln_b', 'gmlp_w_s', 'gmlp_b_s', 'w_out', 'b_out', 'ln1_g', 'ln1_b', 'ffn_w_up', 'ffn_b_up', 'ffn_conv_w', 'ffn_conv_b', 'ffn_w_down', 'ffn_b_down', 'ln2_g', 'ln2_b']
TWIN_WEIGHTS = ['w_in', 'b_in', 'conv_dw_w', 'conv_dw_b', 'conv_ln_g', 'conv_ln_b', 'rel_bias_table', 'gmlp_ln_g', 'gmlp_ln_b', 'gmlp_w_s', 'gmlp_b_s', 'w_out', 'b_out', 'ln1_g', 'ln1_b', 'ffn_w_up', 'ffn_b_up', 'ffn_conv_w', 'ffn_conv_b', 'ffn_w_down', 'ffn_b_down', 'ln2_g', 'ln2_b']
TWIN_DIFF_INPUT = 'x'
TWIN_INPUTS = ['x', 'w_in', 'b_in', 'conv_dw_w', 'conv_dw_b', 'conv_ln_g', 'conv_ln_b', 'rel_bias_table', 'gmlp_ln_g', 'gmlp_ln_b', 'gmlp_w_s', 'gmlp_b_s', 'w_out', 'b_out', 'ln1_g', 'ln1_b', 'ffn_w_up', 'ffn_b_up', 'ffn_conv_w', 'ffn_conv_b', 'ffn_w_down', 'ffn_b_down', 'ln2_g', 'ln2_b', 'loss_target', 'm_w_in', 'm_b_in', 'm_conv_dw_w', 'm_conv_dw_b', 'm_conv_ln_g', 'm_conv_ln_b', 'm_rel_bias_table', 'm_gmlp_ln_g', 'm_gmlp_ln_b', 'm_gmlp_w_s', 'm_gmlp_b_s', 'm_w_out', 'm_b_out', 'm_ln1_g', 'm_ln1_b', 'm_ffn_w_up', 'm_ffn_b_up', 'm_ffn_conv_w', 'm_ffn_conv_b', 'm_ffn_w_down', 'm_ffn_b_down', 'm_ln2_g', 'm_ln2_b', 'v_w_in', 'v_b_in', 'v_conv_dw_w', 'v_conv_dw_b', 'v_conv_ln_g', 'v_conv_ln_b', 'v_rel_bias_table', 'v_gmlp_ln_g', 'v_gmlp_ln_b', 'v_gmlp_w_s', 'v_gmlp_b_s', 'v_w_out', 'v_b_out', 'v_ln1_g', 'v_ln1_b', 'v_ffn_w_up', 'v_ffn_b_up', 'v_ffn_conv_w', 'v_ffn_conv_b', 'v_ffn_w_down', 'v_ffn_b_down', 'v_ln2_g', 'v_ln2_b']
TWIN_OUTPUTS = ['loss', 'grad_x', 'grad_w_in', 'grad_b_in', 'grad_conv_dw_w', 'grad_conv_dw_b', 'grad_conv_ln_g', 'grad_conv_ln_b', 'grad_rel_bias_table', 'grad_gmlp_ln_g', 'grad_gmlp_ln_b', 'grad_gmlp_w_s', 'grad_gmlp_b_s', 'grad_w_out', 'grad_b_out', 'grad_ln1_g', 'grad_ln1_b', 'grad_ffn_w_up', 'grad_ffn_b_up', 'grad_ffn_conv_w', 'grad_ffn_conv_b', 'grad_ffn_w_down', 'grad_ffn_b_down', 'grad_ln2_g', 'grad_ln2_b', 'delta_w_in', 'delta_b_in', 'delta_conv_dw_w', 'delta_conv_dw_b', 'delta_conv_ln_g', 'delta_conv_ln_b', 'delta_rel_bias_table', 'delta_gmlp_ln_g', 'delta_gmlp_ln_b', 'delta_gmlp_w_s', 'delta_gmlp_b_s', 'delta_w_out', 'delta_b_out', 'delta_ln1_g', 'delta_ln1_b', 'delta_ffn_w_up', 'delta_ffn_b_up', 'delta_ffn_conv_w', 'delta_ffn_conv_b', 'delta_ffn_w_down', 'delta_ffn_b_down', 'delta_ln2_g', 'delta_ln2_b', 'new_m_w_in', 'new_m_b_in', 'new_m_conv_dw_w', 'new_m_conv_dw_b', 'new_m_conv_ln_g', 'new_m_conv_ln_b', 'new_m_rel_bias_table', 'new_m_gmlp_ln_g', 'new_m_gmlp_ln_b', 'new_m_gmlp_w_s', 'new_m_gmlp_b_s', 'new_m_w_out', 'new_m_b_out', 'new_m_ln1_g', 'new_m_ln1_b', 'new_m_ffn_w_up', 'new_m_ffn_b_up', 'new_m_ffn_conv_w', 'new_m_ffn_conv_b', 'new_m_ffn_w_down', 'new_m_ffn_b_down', 'new_m_ln2_g', 'new_m_ln2_b', 'new_v_w_in', 'new_v_b_in', 'new_v_conv_dw_w', 'new_v_conv_dw_b', 'new_v_conv_ln_g', 'new_v_conv_ln_b', 'new_v_rel_bias_table', 'new_v_gmlp_ln_g', 'new_v_gmlp_ln_b', 'new_v_gmlp_w_s', 'new_v_gmlp_b_s', 'new_v_w_out', 'new_v_b_out', 'new_v_ln1_g', 'new_v_ln1_b', 'new_v_ffn_w_up', 'new_v_ffn_b_up', 'new_v_ffn_conv_w', 'new_v_ffn_conv_b', 'new_v_ffn_w_down', 'new_v_ffn_b_down', 'new_v_ln2_g', 'new_v_ln2_b']
TWIN_LEAF_KINDS = {'loss': 'loss', 'grad_x': 'grad_x', 'grad_w_in': 'grad_w', 'grad_b_in': 'grad_w', 'grad_conv_dw_w': 'grad_w', 'grad_conv_dw_b': 'grad_w', 'grad_conv_ln_g': 'grad_w', 'grad_conv_ln_b': 'grad_w', 'grad_rel_bias_table': 'grad_w', 'grad_gmlp_ln_g': 'grad_w', 'grad_gmlp_ln_b': 'grad_w', 'grad_gmlp_w_s': 'grad_w', 'grad_gmlp_b_s': 'grad_w', 'grad_w_out': 'grad_w', 'grad_b_out': 'grad_w', 'grad_ln1_g': 'grad_w', 'grad_ln1_b': 'grad_w', 'grad_ffn_w_up': 'grad_w', 'grad_ffn_b_up': 'grad_w', 'grad_ffn_conv_w': 'grad_w', 'grad_ffn_conv_b': 'grad_w', 'grad_ffn_w_down': 'grad_w', 'grad_ffn_b_down': 'grad_w', 'grad_ln2_g': 'grad_w', 'grad_ln2_b': 'grad_w', 'delta_w_in': 'delta_w', 'delta_b_in': 'delta_w', 'delta_conv_dw_w': 'delta_w', 'delta_conv_dw_b': 'delta_w', 'delta_conv_ln_g': 'delta_w', 'delta_conv_ln_b': 'delta_w', 'delta_rel_bias_table': 'delta_w', 'delta_gmlp_ln_g': 'delta_w', 'delta_gmlp_ln_b': 'delta_w', 'delta_gmlp_w_s': 'delta_w', 'delta_gmlp_b_s': 'delta_w', 'delta_w_out': 'delta_w', 'delta_b_out': 'delta_w', 'delta_ln1_g': 'delta_w', 'delta_ln1_b': 'delta_w', 'delta_ffn_w_up': 'delta_w', 'delta_ffn_b_up': 'delta_w', 'delta_ffn_conv_w': 'delta_w', 'delta_ffn_conv_b': 'delta_w', 'delta_ffn_w_down': 'delta_w', 'delta_ffn_b_down': 'delta_w', 'delta_ln2_g': 'delta_w', 'delta_ln2_b': 'delta_w', 'new_m_w_in': 'new_m', 'new_m_b_in': 'new_m', 'new_m_conv_dw_w': 'new_m', 'new_m_conv_dw_b': 'new_m', 'new_m_conv_ln_g': 'new_m', 'new_m_conv_ln_b': 'new_m', 'new_m_rel_bias_table': 'new_m', 'new_m_gmlp_ln_g': 'new_m', 'new_m_gmlp_ln_b': 'new_m', 'new_m_gmlp_w_s': 'new_m', 'new_m_gmlp_b_s': 'new_m', 'new_m_w_out': 'new_m', 'new_m_b_out': 'new_m', 'new_m_ln1_g': 'new_m', 'new_m_ln1_b': 'new_m', 'new_m_ffn_w_up': 'new_m', 'new_m_ffn_b_up': 'new_m', 'new_m_ffn_conv_w': 'new_m', 'new_m_ffn_conv_b': 'new_m', 'new_m_ffn_w_down': 'new_m', 'new_m_ffn_b_down': 'new_m', 'new_m_ln2_g': 'new_m', 'new_m_ln2_b': 'new_m', 'new_v_w_in': 'new_v', 'new_v_b_in': 'new_v', 'new_v_conv_dw_w': 'new_v', 'new_v_conv_dw_b': 'new_v', 'new_v_conv_ln_g': 'new_v', 'new_v_conv_ln_b': 'new_v', 'new_v_rel_bias_table': 'new_v', 'new_v_gmlp_ln_g': 'new_v', 'new_v_gmlp_ln_b': 'new_v', 'new_v_gmlp_w_s': 'new_v', 'new_v_gmlp_b_s': 'new_v', 'new_v_w_out': 'new_v', 'new_v_b_out': 'new_v', 'new_v_ln1_g': 'new_v', 'new_v_ln1_b': 'new_v', 'new_v_ffn_w_up': 'new_v', 'new_v_ffn_b_up': 'new_v', 'new_v_ffn_conv_w': 'new_v', 'new_v_ffn_conv_b': 'new_v', 'new_v_ffn_w_down': 'new_v', 'new_v_ffn_b_down': 'new_v', 'new_v_ln2_g': 'new_v', 'new_v_ln2_b': 'new_v'}


def _forward(args):
    return _fwd_reference(*[args[k] for k in FWD_PARAMS])


def _output_shape():
    def fwd():
        inp = _fwd_setup_inputs(0)
        return _fwd_reference(*[inp[k] for k in FWD_PARAMS])
    out = _jax.eval_shape(fwd)
    return out.shape, out.dtype

N_MICROBATCH = 1
ADAM_LR = 0.001
ADAM_B1 = 0.9
ADAM_B2 = 0.999
ADAM_EPS = 1e-08
ADAM_WD = 0.01
ADAM_STEP = 10
PER_EXAMPLE_BATCH_AXIS = {'x': 0, 'loss_target': 0}
SHARED_INPUTS = []
_WEIGHT_DTYPES = {'w_in': _jnp.float32, 'b_in': _jnp.float32, 'conv_dw_w': _jnp.float32, 'conv_dw_b': _jnp.float32, 'conv_ln_g': _jnp.float32, 'conv_ln_b': _jnp.float32, 'rel_bias_table': _jnp.float32, 'gmlp_ln_g': _jnp.float32, 'gmlp_ln_b': _jnp.float32, 'gmlp_w_s': _jnp.float32, 'gmlp_b_s': _jnp.float32, 'w_out': _jnp.float32, 'b_out': _jnp.float32, 'ln1_g': _jnp.float32, 'ln1_b': _jnp.float32, 'ffn_w_up': _jnp.float32, 'ffn_b_up': _jnp.float32, 'ffn_conv_w': _jnp.float32, 'ffn_conv_b': _jnp.float32, 'ffn_w_down': _jnp.float32, 'ffn_b_down': _jnp.float32, 'ln2_g': _jnp.float32, 'ln2_b': _jnp.float32}
MOMENT_SCALE = {'w_in': 6.453704e-02, 'b_in': 2.735689e-01, 'conv_dw_w': 8.144414e-02, 'conv_dw_b': 3.642641e-01, 'conv_ln_g': 1.378856e-01, 'conv_ln_b': 2.227286e-01, 'rel_bias_table': 2.498979e-02, 'gmlp_ln_g': 9.144903e-02, 'gmlp_ln_b': 8.925271e-02, 'gmlp_w_s': 6.317937e-02, 'gmlp_b_s': 9.015095e-02, 'w_out': 1.890685e-01, 'b_out': 1.104501e+00, 'ln1_g': 3.877277e+00, 'ln1_b': 1.606630e+00, 'ffn_w_up': 4.618630e-02, 'ffn_b_up': 5.724284e-02, 'ffn_conv_w': 4.788944e-02, 'ffn_conv_b': 5.795732e-02, 'ffn_w_down': 1.514325e-01, 'ffn_b_down': 1.149361e+00, 'ln2_g': 9.067664e+01, 'ln2_b': 5.680153e+00}


def _to_microbatches(a, axis):
    t = _jnp.moveaxis(a, axis, 0)
    t = t.reshape((N_MICROBATCH, t.shape[0] // N_MICROBATCH) + t.shape[1:])
    return _jnp.moveaxis(t, 1, axis + 1)


def setup_inputs(seed: int = 0) -> dict:
    inp = _fwd_setup_inputs(seed)
    key = _jax.random.fold_in(_jax.random.key(seed), 7919)
    shape, _ = _output_shape()
    out = dict(inp)
    out["loss_target"] = _jax.random.normal(_jax.random.fold_in(key, 0), shape, _jnp.float32)
    for i, name in enumerate(TWIN_WEIGHTS):
        w = inp[name].astype(_jnp.float32)
        if MOMENT_SCALE is None:
            s = _jnp.sqrt(_jnp.mean(_jnp.square(w)) + 1e-30)
        else:
            s = MOMENT_SCALE[name]
        km, kv = _jax.random.split(_jax.random.fold_in(key, i + 1))
        out[name] = w
        out["m_" + name] = s * _jax.random.normal(km, w.shape, _jnp.float32)
        out["v_" + name] = (s * s) * _jax.random.uniform(kv, w.shape, _jnp.float32, 0.5, 1.5)
    if N_MICROBATCH > 1:
        for name, axis in PER_EXAMPLE_BATCH_AXIS.items():
            out[name] = _to_microbatches(out[name], axis)
    return {'x': out['x'], 'w_in': out['w_in'], 'b_in': out['b_in'], 'conv_dw_w': out['conv_dw_w'], 'conv_dw_b': out['conv_dw_b'], 'conv_ln_g': out['conv_ln_g'], 'conv_ln_b': out['conv_ln_b'], 'rel_bias_table': out['rel_bias_table'], 'gmlp_ln_g': out['gmlp_ln_g'], 'gmlp_ln_b': out['gmlp_ln_b'], 'gmlp_w_s': out['gmlp_w_s'], 'gmlp_b_s': out['gmlp_b_s'], 'w_out': out['w_out'], 'b_out': out['b_out'], 'ln1_g': out['ln1_g'], 'ln1_b': out['ln1_b'], 'ffn_w_up': out['ffn_w_up'], 'ffn_b_up': out['ffn_b_up'], 'ffn_conv_w': out['ffn_conv_w'], 'ffn_conv_b': out['ffn_conv_b'], 'ffn_w_down': out['ffn_w_down'], 'ffn_b_down': out['ffn_b_down'], 'ln2_g': out['ln2_g'], 'ln2_b': out['ln2_b'], 'loss_target': out['loss_target'], 'm_w_in': out['m_w_in'], 'm_b_in': out['m_b_in'], 'm_conv_dw_w': out['m_conv_dw_w'], 'm_conv_dw_b': out['m_conv_dw_b'], 'm_conv_ln_g': out['m_conv_ln_g'], 'm_conv_ln_b': out['m_conv_ln_b'], 'm_rel_bias_table': out['m_rel_bias_table'], 'm_gmlp_ln_g': out['m_gmlp_ln_g'], 'm_gmlp_ln_b': out['m_gmlp_ln_b'], 'm_gmlp_w_s': out['m_gmlp_w_s'], 'm_gmlp_b_s': out['m_gmlp_b_s'], 'm_w_out': out['m_w_out'], 'm_b_out': out['m_b_out'], 'm_ln1_g': out['m_ln1_g'], 'm_ln1_b': out['m_ln1_b'], 'm_ffn_w_up': out['m_ffn_w_up'], 'm_ffn_b_up': out['m_ffn_b_up'], 'm_ffn_conv_w': out['m_ffn_conv_w'], 'm_ffn_conv_b': out['m_ffn_conv_b'], 'm_ffn_w_down': out['m_ffn_w_down'], 'm_ffn_b_down': out['m_ffn_b_down'], 'm_ln2_g': out['m_ln2_g'], 'm_ln2_b': out['m_ln2_b'], 'v_w_in': out['v_w_in'], 'v_b_in': out['v_b_in'], 'v_conv_dw_w': out['v_conv_dw_w'], 'v_conv_dw_b': out['v_conv_dw_b'], 'v_conv_ln_g': out['v_conv_ln_g'], 'v_conv_ln_b': out['v_conv_ln_b'], 'v_rel_bias_table': out['v_rel_bias_table'], 'v_gmlp_ln_g': out['v_gmlp_ln_g'], 'v_gmlp_ln_b': out['v_gmlp_ln_b'], 'v_gmlp_w_s': out['v_gmlp_w_s'], 'v_gmlp_b_s': out['v_gmlp_b_s'], 'v_w_out': out['v_w_out'], 'v_b_out': out['v_b_out'], 'v_ln1_g': out['v_ln1_g'], 'v_ln1_b': out['v_ln1_b'], 'v_ffn_w_up': out['v_ffn_w_up'], 'v_ffn_b_up': out['v_ffn_b_up'], 'v_ffn_conv_w': out['v_ffn_conv_w'], 'v_ffn_conv_b': out['v_ffn_conv_b'], 'v_ffn_w_down': out['v_ffn_w_down'], 'v_ffn_b_down': out['v_ffn_b_down'], 'v_ln2_g': out['v_ln2_g'], 'v_ln2_b': out['v_ln2_b']}


def _loss(weights, diff, rest, loss_target):
    with _jax.named_scope("forward"):
        args = {**rest, TWIN_DIFF_INPUT: diff, **{k: w.astype(_WEIGHT_DTYPES[k]) for k, w in weights.items()}}
        y = _forward(args)
    with _jax.named_scope("loss_head"):
        err = _jnp.square(y.astype(_jnp.float32) - loss_target)
        return 0.5 * _jnp.sum(_jnp.mean(err, axis=-1)) if err.ndim else 0.5 * err


def _adamw(w, g, m, v):
    m = ADAM_B1 * m + (1.0 - ADAM_B1) * g
    v = ADAM_B2 * v + (1.0 - ADAM_B2) * _jnp.square(g)
    m_hat = m / (1.0 - ADAM_B1 ** ADAM_STEP)
    v_hat = v / (1.0 - ADAM_B2 ** ADAM_STEP)
    delta = -ADAM_LR * (m_hat / (_jnp.sqrt(v_hat) + ADAM_EPS) + ADAM_WD * w)
    return delta, m, v


def reference(x, w_in, b_in, conv_dw_w, conv_dw_b, conv_ln_g, conv_ln_b, rel_bias_table, gmlp_ln_g, gmlp_ln_b, gmlp_w_s, gmlp_b_s, w_out, b_out, ln1_g, ln1_b, ffn_w_up, ffn_b_up, ffn_conv_w, ffn_conv_b, ffn_w_down, ffn_b_down, ln2_g, ln2_b, loss_target, m_w_in, m_b_in, m_conv_dw_w, m_conv_dw_b, m_conv_ln_g, m_conv_ln_b, m_rel_bias_table, m_gmlp_ln_g, m_gmlp_ln_b, m_gmlp_w_s, m_gmlp_b_s, m_w_out, m_b_out, m_ln1_g, m_ln1_b, m_ffn_w_up, m_ffn_b_up, m_ffn_conv_w, m_ffn_conv_b, m_ffn_w_down, m_ffn_b_down, m_ln2_g, m_ln2_b, v_w_in, v_b_in, v_conv_dw_w, v_conv_dw_b, v_conv_ln_g, v_conv_ln_b, v_rel_bias_table, v_gmlp_ln_g, v_gmlp_ln_b, v_gmlp_w_s, v_gmlp_b_s, v_w_out, v_b_out, v_ln1_g, v_ln1_b, v_ffn_w_up, v_ffn_b_up, v_ffn_conv_w, v_ffn_conv_b, v_ffn_w_down, v_ffn_b_down, v_ln2_g, v_ln2_b):
    given = dict(x=x, w_in=w_in, b_in=b_in, conv_dw_w=conv_dw_w, conv_dw_b=conv_dw_b, conv_ln_g=conv_ln_g, conv_ln_b=conv_ln_b, rel_bias_table=rel_bias_table, gmlp_ln_g=gmlp_ln_g, gmlp_ln_b=gmlp_ln_b, gmlp_w_s=gmlp_w_s, gmlp_b_s=gmlp_b_s, w_out=w_out, b_out=b_out, ln1_g=ln1_g, ln1_b=ln1_b, ffn_w_up=ffn_w_up, ffn_b_up=ffn_b_up, ffn_conv_w=ffn_conv_w, ffn_conv_b=ffn_conv_b, ffn_w_down=ffn_w_down, ffn_b_down=ffn_b_down, ln2_g=ln2_g, ln2_b=ln2_b, loss_target=loss_target, m_w_in=m_w_in, m_b_in=m_b_in, m_conv_dw_w=m_conv_dw_w, m_conv_dw_b=m_conv_dw_b, m_conv_ln_g=m_conv_ln_g, m_conv_ln_b=m_conv_ln_b, m_rel_bias_table=m_rel_bias_table, m_gmlp_ln_g=m_gmlp_ln_g, m_gmlp_ln_b=m_gmlp_ln_b, m_gmlp_w_s=m_gmlp_w_s, m_gmlp_b_s=m_gmlp_b_s, m_w_out=m_w_out, m_b_out=m_b_out, m_ln1_g=m_ln1_g, m_ln1_b=m_ln1_b, m_ffn_w_up=m_ffn_w_up, m_ffn_b_up=m_ffn_b_up, m_ffn_conv_w=m_ffn_conv_w, m_ffn_conv_b=m_ffn_conv_b, m_ffn_w_down=m_ffn_w_down, m_ffn_b_down=m_ffn_b_down, m_ln2_g=m_ln2_g, m_ln2_b=m_ln2_b, v_w_in=v_w_in, v_b_in=v_b_in, v_conv_dw_w=v_conv_dw_w, v_conv_dw_b=v_conv_dw_b, v_conv_ln_g=v_conv_ln_g, v_conv_ln_b=v_conv_ln_b, v_rel_bias_table=v_rel_bias_table, v_gmlp_ln_g=v_gmlp_ln_g, v_gmlp_ln_b=v_gmlp_ln_b, v_gmlp_w_s=v_gmlp_w_s, v_gmlp_b_s=v_gmlp_b_s, v_w_out=v_w_out, v_b_out=v_b_out, v_ln1_g=v_ln1_g, v_ln1_b=v_ln1_b, v_ffn_w_up=v_ffn_w_up, v_ffn_b_up=v_ffn_b_up, v_ffn_conv_w=v_ffn_conv_w, v_ffn_conv_b=v_ffn_conv_b, v_ffn_w_down=v_ffn_w_down, v_ffn_b_down=v_ffn_b_down, v_ln2_g=v_ln2_g, v_ln2_b=v_ln2_b)
    weights = {n: given[n] for n in TWIN_WEIGHTS}
    shared = {n: given[n] for n in SHARED_INPUTS}
    per_example = {n: given[n] for n in ['x']}
    grad_fn = _jax.value_and_grad(_loss, argnums=(0, 1))

    def one_microbatch(ex, loss_target):
        ex = dict(ex)
        diff = ex.pop(TWIN_DIFF_INPUT)
        return grad_fn(weights, diff, {**shared, **ex}, loss_target)

    if N_MICROBATCH == 1:
        loss, (grad_w, grad_x) = one_microbatch(per_example, given["loss_target"])
    else:
        def body(carry, xs):
            loss_sum, grad_sum = carry
            l_k, (gw_k, gx_k) = one_microbatch(xs[0], xs[1])
            with _jax.named_scope("update"):
                return (loss_sum + l_k, _jax.tree.map(_jnp.add, grad_sum, gw_k)), gx_k

        init = (_jnp.zeros((), _jnp.float32), _jax.tree.map(_jnp.zeros_like, weights))
        (loss, grad_w), grad_x = _jax.lax.scan(body, init, (per_example, given["loss_target"]))
    with _jax.named_scope("update"):
        delta_w, new_m, new_v = {}, {}, {}
        for n in TWIN_WEIGHTS:
            delta_w[n], new_m[n], new_v[n] = _adamw(weights[n], grad_w[n], given["m_" + n], given["v_" + n])
    return (loss, grad_x, *[grad_w[n] for n in TWIN_WEIGHTS], *[delta_w[n] for n in TWIN_WEIGHTS],
            *[new_m[n] for n in TWIN_WEIGHTS], *[new_v[n] for n in TWIN_WEIGHTS])
```

```python
import math

import numpy as np
import jax
import jax.numpy as jnp
from jax import lax
from jax.experimental import pallas as pl
from jax.experimental.pallas import tpu as pltpu

F32 = jnp.float32
BF16 = jnp.bfloat16

DEPTH = 2
HEAD_DIM = 64
CONV_CH = 256
CONV_WIDTH = 31
ATTN_HEADS = 8
ATTN_CH = ATTN_HEADS * HEAD_DIM
PATTERNS = ((128, 1), (512, 4), (2048, 16))
ATTN_BLOCK = 128
N_BUCKETS = 32
MAX_DISTANCE = 2048
GMLP_CH = 256
GMLP_GROUPS = 4
GMLP_GROUP_DIM = GMLP_CH // GMLP_GROUPS
CHUNK = 128
FFN_CONV_WIDTH = 3
LN_EPS = 1e-5
ALPHA = (2.0 * DEPTH) ** 0.25
ADAM_LR = 0.001
ADAM_B1 = 0.9
ADAM_B2 = 0.999
ADAM_EPS = 1e-08
ADAM_WD = 0.01
ADAM_STEP = 10
NEG = -1e30
N_DEV = 8
LANES = 128
CONV_HALO = 32
FFN_HALO = 8
PACK_W = 512
MESH = pl.DeviceIdType.MESH

WEIGHTS = ['w_in', 'b_in', 'conv_dw_w', 'conv_dw_b', 'conv_ln_g', 'conv_ln_b', 'rel_bias_table', 'gmlp_ln_g',
           'gmlp_ln_b', 'gmlp_w_s', 'gmlp_b_s', 'w_out', 'b_out', 'ln1_g', 'ln1_b', 'ffn_w_up', 'ffn_b_up',
           'ffn_conv_w', 'ffn_conv_b', 'ffn_w_down', 'ffn_b_down', 'ln2_g', 'ln2_b']
SHARDED = ['w_in', 'w_out', 'ffn_w_up', 'ffn_w_down', 'conv_dw_w', 'ffn_conv_w']
SHARD_AXIS = {'w_in': 2, 'w_out': 1, 'ffn_w_up': 2, 'ffn_w_down': 1, 'conv_dw_w': 2, 'ffn_conv_w': 2}
SMALL = [n for n in WEIGHTS if n not in SHARDED]


def _call(body, *, grid=(), vmem_mb=48, **kw):
    params = pltpu.CompilerParams(dimension_semantics=("arbitrary",) * len(grid), vmem_limit_bytes=vmem_mb << 20)
    return pl.pallas_call(body, grid=grid, compiler_params=params, **kw)


def _sds(shape, dtype):
    return jax.ShapeDtypeStruct(shape, dtype)


def _ln_rows(z):
    mu = jnp.mean(z, axis=-1, keepdims=True)
    zc = z - mu
    var = jnp.mean(zc * zc, axis=-1, keepdims=True)
    rstd = lax.rsqrt(var + LN_EPS)
    return zc * rstd, rstd


def _ln_bwd_rows(dxhat, xhat, rstd):
    m1 = jnp.mean(dxhat, axis=-1, keepdims=True)
    m2 = jnp.mean(dxhat * xhat, axis=-1, keepdims=True)
    return rstd * (dxhat - m1 - xhat * m2)


def _colsum(v):
    return jnp.sum(v, axis=0, keepdims=True)


def _mm(a_list, w_list, *, name, tm, tn=None, bias=None, resid=None, resid_scale=1.0, ln=None, splits=None,
        out_dtype=F32):
    na = len(a_list)
    M = a_list[0].shape[0]
    N = w_list[0].shape[1]
    tn = N if tn is None else tn
    assert M % tm == 0 and N % tn == 0
    assert ln is None or tn == N
    assert splits is None or tn == N

    def body(*refs):
        a_refs, w_refs = refs[:na], refs[na:2 * na]
        pos = 2 * na
        acc = None
        for a_ref, w_ref in zip(a_refs, w_refs):
            t = jnp.dot(a_ref[...].astype(BF16), w_ref[...], preferred_element_type=F32)
            acc = t if acc is None else acc + t
        if bias is not None:
            acc = acc + refs[pos][...]
            pos += 1
        if resid is not None:
            acc = resid_scale * refs[pos][...] + acc
            pos += 1
        if ln is not None:
            g_ref, b_ref = refs[pos], refs[pos + 1]
            y_ref, xhat_ref, rstd_ref = refs[pos + 2], refs[pos + 3], refs[pos + 4]
            xhat, rstd = _ln_rows(acc)
            y_ref[...] = xhat * g_ref[...] + b_ref[...]
            xhat_ref[...] = xhat
            rstd_ref[...] = rstd
        elif splits is not None:
            c0 = 0
            for o_ref, (width, dtype, scale) in zip(refs[pos:], splits):
                part = acc[:, c0:c0 + width]
                if scale != 1.0:
                    part = part * scale
                o_ref[...] = part.astype(dtype)
                c0 += width
        else:
            refs[pos][...] = acc.astype(out_dtype)

    in_specs = [pl.BlockSpec((tm, a.shape[1]), lambda j, i: (i, 0)) for a in a_list]
    in_specs += [pl.BlockSpec((w.shape[0], tn), lambda j, i: (0, j)) for w in w_list]
    args = list(a_list) + list(w_list)
    if bias is not None:
        in_specs.append(pl.BlockSpec((1, tn), lambda j, i: (0, j)))
        args.append(bias.reshape(1, N))
    if resid is not None:
        in_specs.append(pl.BlockSpec((tm, tn), lambda j, i: (i, j)))
        args.append(resid)
    if ln is not None:
        in_specs += [pl.BlockSpec((1, N), lambda j, i: (0, 0))] * 2
        args += [ln[0].reshape(1, N), ln[1].reshape(1, N)]
        out_shape = (_sds((M, N), F32), _sds((M, N), F32), _sds((M, 1), F32))
        out_specs = (pl.BlockSpec((tm, N), lambda j, i: (i, 0)), pl.BlockSpec((tm, N), lambda j, i: (i, 0)),
                     pl.BlockSpec((tm, 1), lambda j, i: (i, 0)))
    elif splits is not None:
        out_shape = tuple(_sds((M, w), d) for (w, d, _) in splits)
        out_specs = tuple(pl.BlockSpec((tm, w), lambda j, i: (i, 0)) for (w, _, _) in splits)
    else:
        out_shape = _sds((M, N), out_dtype)
        out_specs = pl.BlockSpec((tm, tn), lambda j, i: (i, j))
    return _call(body, grid=(N // tn, M // tm), in_specs=in_specs, out_specs=out_specs, out_shape=out_shape,
                 name=name, vmem_mb=56)(*args)


def _mm_tn(a, dy, *, name, tm, tn, tk, colsum=False):
    S, Ka = a.shape
    N = dy.shape[1]
    assert S % tk == 0 and Ka % tm == 0 and N % tn == 0

    def body(a_ref, dy_ref, out_ref, *cs):
        i, k = pl.program_id(1), pl.program_id(2)
        dyb = dy_ref[...]
        part = lax.dot_general(a_ref[...].astype(BF16), dyb.astype(BF16), (((0,), (0,)), ((), ())),
                               preferred_element_type=F32)

        @pl.when(k == 0)
        def _():
            out_ref[...] = part

        @pl.when(k > 0)
        def _():
            out_ref[...] += part

        if colsum:
            cs_ref = cs[0]
            s = _colsum(dyb.astype(F32))

            @pl.when((i == 0) & (k == 0))
            def _():
                cs_ref[...] = s

            @pl.when((i == 0) & (k > 0))
            def _():
                cs_ref[...] += s

    out_shape = [_sds((Ka, N), F32)]
    out_specs = [pl.BlockSpec((tm, tn), lambda j, i, k: (i, j))]
    if colsum:
        out_shape.append(_sds((1, N), F32))
        out_specs.append(pl.BlockSpec((1, tn), lambda j, i, k: (0, j)))
    res = _call(body, grid=(N // tn, Ka // tm, S // tk),
                in_specs=[pl.BlockSpec((tk, tm), lambda j, i, k: (k, i)), pl.BlockSpec((tk, tn), lambda j, i, k: (k, j))],
                out_specs=tuple(out_specs), out_shape=tuple(out_shape), name=name, vmem_mb=56)(a, dy)
    return res if colsum else res[0]


def _ln_bwd(xhat, rstd, g, *, name, ts, dy=None, b=None, target=None):
    S, D = xhat.shape
    from_loss = target is not None

    def body(*refs):
        if from_loss:
            xhat_ref, rstd_ref, g_ref, b_ref, t_ref, dz_ref, dg_ref, db_ref, dzs_ref, loss_ref = refs
        else:
            xhat_ref, rstd_ref, g_ref, dy_ref, dz_ref, dg_ref, db_ref, dzs_ref = refs
        i = pl.program_id(0)
        xh = xhat_ref[...]
        gg = g_ref[...]
        if from_loss:
            err = xh * gg + b_ref[...] - t_ref[...]
            dyv = err * (1.0 / D)
            lsum = (0.5 / D) * jnp.sum(err * err, axis=(0, 1), keepdims=True)
        else:
            dyv = dy_ref[...]
        dz = _ln_bwd_rows(dyv * gg, xh, rstd_ref[...])
        dz_ref[...] = dz
        parts = [(dg_ref, _colsum(dyv * xh)), (db_ref, _colsum(dyv)), (dzs_ref, _colsum(dz))]
        if from_loss:
            parts.append((loss_ref, lsum))

        @pl.when(i == 0)
        def _():
            for r, v in parts:
                r[...] = v

        @pl.when(i > 0)
        def _():
            for r, v in parts:
                r[...] += v

    row = pl.BlockSpec((ts, D), lambda i: (i, 0))
    vec = pl.BlockSpec((1, D), lambda i: (0, 0))
    in_specs = [row, pl.BlockSpec((ts, 1), lambda i: (i, 0)), vec]
    args = [xhat, rstd, g.reshape(1, D)]
    if from_loss:
        in_specs += [vec, row]
        args += [b.reshape(1, D), target]
    else:
        in_specs += [row]
        args += [dy]
    out_shape = [_sds((S, D), F32), _sds((1, D), F32), _sds((1, D), F32), _sds((1, D), F32)]
    out_specs = [row, vec, vec, vec]
    if from_loss:
        out_shape.append(_sds((1, 1), F32))
        out_specs.append(pl.BlockSpec((1, 1), lambda i: (0, 0)))
    return _call(body, grid=(S // ts,), in_specs=in_specs, out_specs=tuple(out_specs), out_shape=tuple(out_shape),
                 name=name)(*args)


def _glu(v):
    return v[:, :CONV_CH] * jax.nn.sigmoid(v[:, CONV_CH:])


def _conv_fwd(h_a, dw_w, dw_b, ln_g, ln_b, *, name, ts):
    S = h_a.shape[0]
    C, K, HB = CONV_CH, CONV_WIDTH, CONV_HALO
    RC = 128

    def body(h_ref, halo_ref, w_ref, b_ref, g_ref, bb_ref, out_ref, hc_ref, gbuf):
        i = pl.program_id(0)
        gbuf[0:HB, :] = jnp.where(i > 0, _glu(halo_ref[...]), 0.0)
        gbuf[HB:HB + ts, :] = _glu(h_ref[...])
        for r0 in range(0, ts, RC):
            acc = jnp.zeros((RC, C), F32) + b_ref[...]
            for k in range(K):
                acc = acc + w_ref[k:k + 1, :] * gbuf[pl.ds(r0 + HB - (K - 1) + k, RC), :]
            hc_ref[r0:r0 + RC, :] = acc
            xhat, _ = _ln_rows(acc)
            hn = xhat * g_ref[...] + bb_ref[...]
            out_ref[r0:r0 + RC, :] = (hn * jax.nn.sigmoid(hn)).astype(BF16)

    nb = ts // HB
    vec = pl.BlockSpec((1, C), lambda i: (0, 0))
    return _call(body, grid=(S // ts,),
                 in_specs=[pl.BlockSpec((ts, 2 * C), lambda i: (i, 0)),
                           pl.BlockSpec((HB, 2 * C), lambda i: (jnp.maximum(i * nb - 1, 0), 0)),
                           pl.BlockSpec((K, C), lambda i: (0, 0)), vec, vec, vec],
                 out_specs=(pl.BlockSpec((ts, C), lambda i: (i, 0)), pl.BlockSpec((ts, C), lambda i: (i, 0))),
                 out_shape=(_sds((S, C), BF16), _sds((S, C), F32)),
                 scratch_shapes=[pltpu.VMEM((HB + ts, C), F32)], name=name)(
        h_a, h_a, dw_w, dw_b.reshape(1, C), ln_g.reshape(1, C), ln_b.reshape(1, C))


def _conv_bwd(h_a, hc, dout, dw_w, ln_g, ln_b, *, name, ts):
    S = h_a.shape[0]
    C, K, HB = CONV_CH, CONV_WIDTH, CONV_HALO
    RC = 128
    n = S // ts

    def dconv_out(hc_v, do_v, g_ref, bb_ref):
        xhat, rstd = _ln_rows(hc_v)
        hn = xhat * g_ref[...] + bb_ref[...]
        sg = jax.nn.sigmoid(hn)
        dhn = do_v * (sg * (1.0 + hn * (1.0 - sg)))
        return _ln_bwd_rows(dhn * g_ref[...], xhat, rstd), dhn, xhat

    def body(h_ref, hprev_ref, hc_ref, hcnext_ref, do_ref, donext_ref, w_ref, g_ref, bb_ref,
             dh_ref, dw_ref, dwb_ref, dg_ref, db_ref, gbuf, dbuf):
        i = pl.program_id(0)
        hv = h_ref[...]
        gbuf[0:HB, :] = jnp.where(i > 0, _glu(hprev_ref[...]), 0.0)
        gbuf[HB:HB + ts, :] = _glu(hv)
        dhc, dhn, xhat = dconv_out(hc_ref[...], do_ref[...], g_ref, bb_ref)
        dhc_next, _, _ = dconv_out(hcnext_ref[...], donext_ref[...], g_ref, bb_ref)
        dbuf[0:ts, :] = dhc
        dbuf[ts:ts + HB, :] = jnp.where(i < n - 1, dhc_next, 0.0)
        dw_rows = []
        for k in range(K):
            acc_k = jnp.zeros((1, C), F32)
            for r0 in range(0, ts, RC):
                acc_k = acc_k + _colsum(dbuf[r0:r0 + RC, :] * gbuf[pl.ds(r0 + HB - (K - 1) + k, RC), :])
            dw_rows.append(acc_k)
        dw_rows.append(jnp.zeros((1, C), F32))
        dw_tile = jnp.concatenate(dw_rows, axis=0)
        for r0 in range(0, ts, RC):
            acc = jnp.zeros((RC, C), F32)
            for k in range(K):
                acc = acc + w_ref[k:k + 1, :] * dbuf[pl.ds(r0 + (K - 1) - k, RC), :]
            a = hv[r0:r0 + RC, :C]
            sg = jax.nn.sigmoid(hv[r0:r0 + RC, C:])
            dh_ref[r0:r0 + RC, :C] = (acc * sg).astype(BF16)
            dh_ref[r0:r0 + RC, C:] = (acc * a * sg * (1.0 - sg)).astype(BF16)
        parts = [(dw_ref, dw_tile), (dwb_ref, _colsum(dhc)), (dg_ref, _colsum(dhn * xhat)), (db_ref, _colsum(dhn))]

        @pl.when(i == 0)
        def _():
            for r, v in parts:
                r[...] = v

        @pl.when(i > 0)
        def _():
            for r, v in parts:
                r[...] += v

    nb = ts // HB
    last = S // HB - 1
    vec = pl.BlockSpec((1, C), lambda i: (0, 0))
    nxt = lambda i: (jnp.minimum((i + 1) * nb, last), 0)
    return _call(body, grid=(n,),
                 in_specs=[pl.BlockSpec((ts, 2 * C), lambda i: (i, 0)),
                           pl.BlockSpec((HB, 2 * C), lambda i: (jnp.maximum(i * nb - 1, 0), 0)),
                           pl.BlockSpec((ts, C), lambda i: (i, 0)), pl.BlockSpec((HB, C), nxt),
                           pl.BlockSpec((ts, C), lambda i: (i, 0)), pl.BlockSpec((HB, C), nxt),
                           pl.BlockSpec((K, C), lambda i: (0, 0)), vec, vec],
                 out_specs=(pl.BlockSpec((ts, 2 * C), lambda i: (i, 0)), pl.BlockSpec((K + 1, C), lambda i: (0, 0)),
                            vec, vec, vec),
                 out_shape=(_sds((S, 2 * C), BF16), _sds((K + 1, C), F32), _sds((1, C), F32), _sds((1, C), F32),
                            _sds((1, C), F32)),
                 scratch_shapes=[pltpu.VMEM((HB + ts, C), F32), pltpu.VMEM((ts + HB, C), F32)], name=name)(
        h_a, h_a, hc, hc, dout, dout, dw_w, ln_g.reshape(1, C), ln_b.reshape(1, C))


def _gmlp_mix(vn_bf, w_ref, mix_buf, ts):
    for ch in range(ts // CHUNK):
        for g in range(GMLP_GROUPS):
            vg = vn_bf[ch * CHUNK:(ch + 1) * CHUNK, g * GMLP_GROUP_DIM:(g + 1) * GMLP_GROUP_DIM]
            mix_buf[ch * CHUNK:(ch + 1) * CHUNK, g * GMLP_GROUP_DIM:(g + 1) * GMLP_GROUP_DIM] = jnp.dot(
                w_ref[g], vg, preferred_element_type=F32)


def _gmlp_fwd(h_c, ln_g, ln_b, w_tril, bs_rows, *, name, ts):
    S = h_c.shape[0]
    C = GMLP_CH

    def body(h_ref, g_ref, b_ref, w_ref, bs_ref, out_ref, mix_buf):
        hv = h_ref[...]
        xhat, _ = _ln_rows(hv[:, C:])
        vn = (xhat * g_ref[...] + b_ref[...]).astype(BF16)
        _gmlp_mix(vn, w_ref, mix_buf, ts)
        for ch in range(ts // CHUNK):
            rows = slice(ch * CHUNK, (ch + 1) * CHUNK)
            out_ref[rows, :] = (hv[rows, :C] * (mix_buf[rows, :] + bs_ref[...])).astype(BF16)

    vec = pl.BlockSpec((1, C), lambda i: (0, 0))
    return _call(body, grid=(S // ts,),
                 in_specs=[pl.BlockSpec((ts, 2 * C), lambda i: (i, 0)), vec, vec,
                           pl.BlockSpec((GMLP_GROUPS, CHUNK, CHUNK), lambda i: (0, 0, 0)),
                           pl.BlockSpec((CHUNK, C), lambda i: (0, 0))],
                 out_specs=pl.BlockSpec((ts, C), lambda i: (i, 0)), out_shape=_sds((S, C), BF16),
                 scratch_shapes=[pltpu.VMEM((ts, C), F32)], name=name)(
        h_c, ln_g.reshape(1, C), ln_b.reshape(1, C), w_tril, bs_rows)


def _gmlp_bwd(h_c, dout, ln_g, ln_b, w_tril, w_tril_t, bs_rows, *, name, ts):
    S = h_c.shape[0]
    C, G, GD = GMLP_CH, GMLP_GROUPS, GMLP_GROUP_DIM

    def body(h_ref, do_ref, g_ref, b_ref, w_ref, wt_ref, bs_ref, dh_ref, dw_ref, dbs_ref, dg_ref, db_ref,
             mix_buf, dvn_buf):
        i = pl.program_id(0)
        hv = h_ref[...]
        u = hv[:, :C]
        xhat, rstd = _ln_rows(hv[:, C:])
        vn = (xhat * g_ref[...] + b_ref[...]).astype(BF16)
        _gmlp_mix(vn, w_ref, mix_buf, ts)
        do = do_ref[...]
        dmixed = do * u
        dm_bf = dmixed.astype(BF16)
        lane = lax.broadcasted_iota(jnp.int32, (CHUNK, LANES), 1)
        dbs = jnp.zeros((CHUNK, LANES), F32)
        dws = [jnp.zeros((CHUNK, CHUNK), F32) for _ in range(G)]
        for ch in range(ts // CHUNK):
            rows = slice(ch * CHUNK, (ch + 1) * CHUNK)
            dh_ref[rows, :C] = (do[rows, :] * (mix_buf[rows, :] + bs_ref[...])).astype(BF16)
            for g in range(G):
                cols = slice(g * GD, (g + 1) * GD)
                dmg = dm_bf[rows, cols]
                dvn_buf[rows, cols] = jnp.dot(wt_ref[g], dmg, preferred_element_type=F32)
                dws[g] = dws[g] + lax.dot_general(dmg, vn[rows, cols], (((1,), (1,)), ((), ())),
                                                  preferred_element_type=F32)
                rs = jnp.sum(dmixed[rows, cols], axis=1, keepdims=True)
                dbs = dbs + jnp.where(lane == g, rs, 0.0)
        dvn = dvn_buf[...]
        dh_ref[:, C:] = _ln_bwd_rows(dvn * g_ref[...], xhat, rstd).astype(BF16)
        dgv, dbv = _colsum(dvn * xhat), _colsum(dvn)

        @pl.when(i == 0)
        def _():
            for g in range(G):
                dw_ref[g] = dws[g]
            dbs_ref[...] = dbs
            dg_ref[...] = dgv
            db_ref[...] = dbv

        @pl.when(i > 0)
        def _():
            for g in range(G):
                dw_ref[g] += dws[g]
            dbs_ref[...] += dbs
            dg_ref[...] += dgv
            db_ref[...] += dbv

    vec = pl.BlockSpec((1, C), lambda i: (0, 0))
    wspec = pl.BlockSpec((G, CHUNK, CHUNK), lambda i: (0, 0, 0))
    return _call(body, grid=(S // ts,),
                 in_specs=[pl.BlockSpec((ts, 2 * C), lambda i: (i, 0)),
                           pl.BlockSpec((ts, C), lambda i: (i, 0)), vec, vec, wspec, wspec,
                           pl.BlockSpec((CHUNK, C), lambda i: (0, 0))],
                 out_specs=(pl.BlockSpec((ts, 2 * C), lambda i: (i, 0)), wspec,
                            pl.BlockSpec((CHUNK, LANES), lambda i: (0, 0)), vec, vec),
                 out_shape=(_sds((S, 2 * C), BF16), _sds((G, CHUNK, CHUNK), F32), _sds((CHUNK, LANES), F32),
                            _sds((1, C), F32), _sds((1, C), F32)),
                 scratch_shapes=[pltpu.VMEM((ts, C), F32), pltpu.VMEM((ts, C), F32)], name=name)(
        h_c, dout, ln_g.reshape(1, C), ln_b.reshape(1, C), w_tril, w_tril_t, bs_rows)


def _conv3(buf, w_ref, b_ref, rows, off):
    return (w_ref[0:1, :] * buf[pl.ds(off - 2, rows), :] + w_ref[1:2, :] * buf[pl.ds(off - 1, rows), :]
            + w_ref[2:3, :] * buf[pl.ds(off, rows), :]) + b_ref[...]


def _ffn_act_fwd(hu, cw, cb, *, name, ts, tc):
    S, F2 = hu.shape
    F = F2 // 2
    nj = F // tc
    HB = FFN_HALO
    nb = ts // HB

    def body(g_ref, v_ref, gh_ref, vh_ref, wg_ref, wv_ref, bg_ref, bv_ref, act_ref, gbuf, vbuf):
        i = pl.program_id(0)
        gbuf[0:HB, :] = jnp.where(i > 0, gh_ref[...], 0.0)
        gbuf[HB:HB + ts, :] = g_ref[...]
        vbuf[0:HB, :] = jnp.where(i > 0, vh_ref[...], 0.0)
        vbuf[HB:HB + ts, :] = v_ref[...]
        gc = _conv3(gbuf, wg_ref, bg_ref, ts, HB)
        vc = _conv3(vbuf, wv_ref, bv_ref, ts, HB)
        act_ref[...] = (gc * jax.nn.sigmoid(gc) * vc).astype(BF16)

    prev = lambda off: (lambda i, j: (jnp.maximum(i * nb - 1, 0), j + off))
    return _call(body, grid=(S // ts, nj),
                 in_specs=[pl.BlockSpec((ts, tc), lambda i, j: (i, j)), pl.BlockSpec((ts, tc), lambda i, j: (i, j + nj)),
                           pl.BlockSpec((HB, tc), prev(0)), pl.BlockSpec((HB, tc), prev(nj)),
                           pl.BlockSpec((3, tc), lambda i, j: (0, j)), pl.BlockSpec((3, tc), lambda i, j: (0, j + nj)),
                           pl.BlockSpec((1, tc), lambda i, j: (0, j)), pl.BlockSpec((1, tc), lambda i, j: (0, j + nj))],
                 out_specs=pl.BlockSpec((ts, tc), lambda i, j: (i, j)), out_shape=_sds((S, F), BF16),
                 scratch_shapes=[pltpu.VMEM((HB + ts, tc), F32), pltpu.VMEM((HB + ts, tc), F32)], name=name)(
        hu, hu, hu, hu, cw, cw, cb.reshape(1, F2), cb.reshape(1, F2))


def _ffn_act_bwd(hu, dact, cw, cb, *, name, ts, tc):
    S, F2 = hu.shape
    F = F2 // 2
    nj = F // tc
    HB = FFN_HALO
    nb = ts // HB
    n = S // ts
    last = S // HB - 1

    def body(g_ref, v_ref, gp_ref, vp_ref, gn_ref, vn_ref, da_ref, dan_ref, wg_ref, wv_ref, bg_ref, bv_ref,
             dhg_ref, dhv_ref, dwg_ref, dwv_ref, dbg_ref, dbv_ref, dug_ref, duv_ref, gbuf, vbuf, dgb, dvb):
        i = pl.program_id(1)
        for buf, p_ref, t_ref, n_ref in ((gbuf, gp_ref, g_ref, gn_ref), (vbuf, vp_ref, v_ref, vn_ref)):
            buf[0:HB, :] = jnp.where(i > 0, p_ref[...], 0.0)
            buf[HB:HB + ts, :] = t_ref[...]
            buf[HB + ts:HB + ts + HB, :] = n_ref[...]
        R = ts + HB
        gc = _conv3(gbuf, wg_ref, bg_ref, R, HB)
        vc = _conv3(vbuf, wv_ref, bv_ref, R, HB)
        dgb[0:ts, :] = da_ref[...]
        dgb[ts:R, :] = jnp.where(i < n - 1, dan_ref[...], 0.0)
        da = dgb[...]
        sg = jax.nn.sigmoid(gc)
        dvb[...] = da * (gc * sg)
        dgb[...] = da * vc * (sg * (1.0 + gc * (1.0 - sg)))
        for dbuf, buf, w_ref, dh_ref, dw_ref, db_ref, du_ref in (
                (dgb, gbuf, wg_ref, dhg_ref, dwg_ref, dbg_ref, dug_ref),
                (dvb, vbuf, wv_ref, dhv_ref, dwv_ref, dbv_ref, duv_ref)):
            d0 = dbuf[0:ts, :]
            dhu = w_ref[2:3, :] * d0 + w_ref[1:2, :] * dbuf[pl.ds(1, ts), :] + w_ref[0:1, :] * dbuf[pl.ds(2, ts), :]
            dh_ref[...] = dhu.astype(BF16)
            dw = jnp.concatenate([_colsum(d0 * buf[pl.ds(HB - 2 + k, ts), :]) for k in range(3)], axis=0)
            parts = [(dw_ref, dw), (db_ref, _colsum(d0)), (du_ref, _colsum(dhu))]

            @pl.when(i == 0)
            def _():
                for r, v in parts:
                    r[...] = v

            @pl.when(i > 0)
            def _():
                for r, v in parts:
                    r[...] += v

    prev = lambda off: (lambda j, i: (jnp.maximum(i * nb - 1, 0), j + off))
    nxt = lambda off: (lambda j, i: (jnp.minimum((i + 1) * nb, last), j + off))
    tile = lambda off: (lambda j, i: (i, j + off))
    vec = lambda rows: pl.BlockSpec((rows, tc), lambda j, i: (0, j))
    F2s = cb.reshape(1, F2)
    return _call(body, grid=(nj, n),
                 in_specs=[pl.BlockSpec((ts, tc), tile(0)), pl.BlockSpec((ts, tc), tile(nj)),
                           pl.BlockSpec((HB, tc), prev(0)), pl.BlockSpec((HB, tc), prev(nj)),
                           pl.BlockSpec((HB, tc), nxt(0)), pl.BlockSpec((HB, tc), nxt(nj)),
                           pl.BlockSpec((ts, tc), tile(0)), pl.BlockSpec((HB, tc), nxt(0)),
                           pl.BlockSpec((3, tc), lambda j, i: (0, j)), pl.BlockSpec((3, tc), lambda j, i: (0, j + nj)),
                           pl.BlockSpec((1, tc), lambda j, i: (0, j)), pl.BlockSpec((1, tc), lambda j, i: (0, j + nj))],
                 out_specs=(pl.BlockSpec((ts, tc), tile(0)), pl.BlockSpec((ts, tc), tile(0)),
                            vec(3), vec(3), vec(1), vec(1), vec(1), vec(1)),
                 out_shape=(_sds((S, F), BF16), _sds((S, F), BF16), _sds((3, F), F32), _sds((3, F), F32),
                            _sds((1, F), F32), _sds((1, F), F32), _sds((1, F), F32), _sds((1, F), F32)),
                 scratch_shapes=[pltpu.VMEM((ts + 2 * HB, tc), F32), pltpu.VMEM((ts + 2 * HB, tc), F32),
                                 pltpu.VMEM((ts + HB, tc), F32), pltpu.VMEM((ts + HB, tc), F32)], name=name)(
        hu, hu, hu, hu, hu, hu, dact, dact, cw, cw, F2s, F2s)


def _t5_bucket(dist):
    max_exact = N_BUCKETS // 2
    d = np.maximum(dist, 1).astype(np.float64)
    large = max_exact + (np.log(d / max_exact) / math.log(MAX_DISTANCE / max_exact)
                         * (N_BUCKETS - max_exact)).astype(np.int32)
    large = np.minimum(large, N_BUCKETS - 1)
    return np.where(dist < max_exact, dist, large).astype(np.int32)


def _pattern_tables(window, dilation):
    qi = np.arange(ATTN_BLOCK)[:, None]
    kj = np.arange(2 * ATTN_BLOCK)[None, :]
    dist = qi + ATTN_BLOCK - kj
    valid = (dist >= 0) & (dist <= window // dilation)
    bucket = _t5_bucket(np.clip(dist, 0, None) * dilation)
    return bucket, valid


def _to_dilated(a, d):
    if d == 1:
        return a
    S, C = a.shape
    return a.reshape(S // d, d, C).transpose(1, 0, 2).reshape(S, C)


def _from_dilated(a, d):
    if d == 1:
        return a
    S, C = a.shape
    return a.reshape(d, S // d, C).transpose(1, 0, 2).reshape(S, C)


def _attn_group(S):
    nb_min = (S // PATTERNS[-1][1]) // ATTN_BLOCK
    return math.gcd(8, nb_min)


def _attn_fwd(qkv, bias, *, name, nb, G):
    S = qkv.shape[0]
    B, HD = ATTN_BLOCK, HEAD_DIM
    GR = G * B
    ng = S // GR

    def body(q_ref, k_ref, kh_ref, v_ref, vh_ref, bias_ref, o_ref, lse_ref, kbuf, vbuf):
        g = pl.program_id(1)
        halo_ok = (g * G) % nb != 0
        kbuf[0:B, :] = kh_ref[...]
        kbuf[B:B + GR, :] = k_ref[...]
        vbuf[0:B, :] = vh_ref[...]
        vbuf[B:B + GR, :] = v_ref[...]
        col = lax.broadcasted_iota(jnp.int32, (B, 2 * B), 1)

        def blk(bi, carry):
            r0 = pl.multiple_of(bi * B, B)
            q2 = q_ref[pl.ds(r0, B), :]
            kk = kbuf[pl.ds(r0, 2 * B), :]
            vv = vbuf[pl.ds(r0, 2 * B), :]
            no_prev = jnp.logical_and(bi == 0, jnp.logical_not(halo_ok))
            for hh in range(2):
                cs = slice(hh * HD, (hh + 1) * HD)
                s = lax.dot_general(q2[:, cs], kk[:, cs], (((1,), (1,)), ((), ())), preferred_element_type=F32)
                s = s + bias_ref[hh]
                s = jnp.where(jnp.logical_and(no_prev, col < B), NEG, s)
                m = jnp.max(s, axis=1, keepdims=True)
                p = jnp.exp(s - m)
                l = jnp.sum(p, axis=1, keepdims=True)
                o = jnp.dot(p.astype(BF16), vv[:, cs], preferred_element_type=F32) / l
                o_ref[pl.ds(r0, B), cs] = o
                lse_ref[pl.ds(r0, B), cs] = jnp.broadcast_to(m + jnp.log(l), (B, HD))
            return carry

        lax.fori_loop(0, G, blk, 0)

    halo = lambda off: (lambda hp, g: (jnp.maximum(g * G - 1, 0), off + hp))
    main = lambda off: (lambda hp, g: (g, off + hp))
    return _call(body, grid=(4, ng),
                 in_specs=[pl.BlockSpec((GR, LANES), main(0)), pl.BlockSpec((GR, LANES), main(4)),
                           pl.BlockSpec((B, LANES), halo(4)), pl.BlockSpec((GR, LANES), main(8)),
                           pl.BlockSpec((B, LANES), halo(8)), pl.BlockSpec((2, B, 2 * B), lambda hp, g: (hp, 0, 0))],
                 out_specs=(pl.BlockSpec((GR, LANES), main(0)), pl.BlockSpec((GR, LANES), main(0))),
                 out_shape=(_sds((S, ATTN_CH), F32), _sds((S, ATTN_CH), F32)),
                 scratch_shapes=[pltpu.VMEM((B + GR, LANES), BF16), pltpu.VMEM((B + GR, LANES), BF16)], name=name)(
        qkv, qkv, qkv, qkv, qkv, bias)


def _attn_merge(o_list, lse_list, *, name, ts):
    S, C = o_list[0].shape
    P = len(o_list)

    def body(*refs):
        o_refs, l_refs = refs[:P], refs[P:2 * P]
        out_ref, lse_ref = refs[2 * P], refs[2 * P + 1]
        ls = [r[...] for r in l_refs]
        m = ls[0]
        for l in ls[1:]:
            m = jnp.maximum(m, l)
        ws = [jnp.exp(l - m) for l in ls]
        den = ws[0]
        for w in ws[1:]:
            den = den + w
        num = ws[0] * o_refs[0][...]
        for w, r in zip(ws[1:], o_refs[1:]):
            num = num + w * r[...]
        out_ref[...] = num / den
        lse_ref[...] = m + jnp.log(den)

    row = pl.BlockSpec((ts, C), lambda i: (i, 0))
    return _call(body, grid=(S // ts,), in_specs=[row] * (2 * P), out_specs=(row, row),
                 out_shape=(_sds((S, C), F32), _sds((S, C), F32)), name=name)(*o_list, *lse_list)


def _attn_prep(dout, out, *, name, ts):
    S, C = out.shape
    HD = HEAD_DIM

    def body(do_ref, o_ref, d_ref, dob_ref):
        do = do_ref[...]
        prod = do * o_ref[...]
        dob_ref[...] = do.astype(BF16)
        for h in range(ATTN_HEADS):
            cs = slice(h * HD, (h + 1) * HD)
            d_ref[:, cs] = jnp.broadcast_to(jnp.sum(prod[:, cs], axis=1, keepdims=True), (ts, HD))

    row = pl.BlockSpec((ts, C), lambda i: (i, 0))
    return _call(body, grid=(S // ts,), in_specs=[row, row],
                 out_specs=(row, row), out_shape=(_sds((S, C), F32), _sds((S, C), BF16)), name=name)(dout, out)


def _attn_bwd(qkv, do, lse, dd, bias, *, name, nb, G):
    S = qkv.shape[0]
    B, HD = ATTN_BLOCK, HEAD_DIM
    GR = G * B
    ng = S // GR
    nblk = S // B

    def body(q_ref, k_ref, kh_ref, v_ref, vh_ref, do_ref, lse_ref, dd_ref, qn_ref, don_ref, lsen_ref, ddn_ref,
             bias_ref, dq_ref, dk_ref, dv_ref, dbias_ref, kbuf, vbuf, dkbuf, dvbuf):
        g = pl.program_id(1)
        halo_ok = (g * G) % nb != 0
        next_ok = jnp.logical_and(((g + 1) * G) % nb != 0, g < ng - 1)
        kbuf[0:B, :] = kh_ref[...]
        kbuf[B:B + GR, :] = k_ref[...]
        vbuf[0:B, :] = vh_ref[...]
        vbuf[B:B + GR, :] = v_ref[...]
        dkbuf[...] = jnp.zeros_like(dkbuf)
        dvbuf[...] = jnp.zeros_like(dvbuf)
        col = lax.broadcasted_iota(jnp.int32, (B, 2 * B), 1)

        @pl.when(g == 0)
        def _():
            dbias_ref[...] = jnp.zeros_like(dbias_ref)

        def blk(bi, carry):
            r0 = pl.multiple_of(bi * B, B)
            q2 = q_ref[pl.ds(r0, B), :]
            do2 = do_ref[pl.ds(r0, B), :]
            kk = kbuf[pl.ds(r0, 2 * B), :]
            vv = vbuf[pl.ds(r0, 2 * B), :]
            no_prev = jnp.logical_and(bi == 0, jnp.logical_not(halo_ok))
            for hh in range(2):
                cs = slice(hh * HD, (hh + 1) * HD)
                s = lax.dot_general(q2[:, cs], kk[:, cs], (((1,), (1,)), ((), ())), preferred_element_type=F32)
                s = s + bias_ref[hh]
                s = jnp.where(jnp.logical_and(no_prev, col < B), NEG, s)
                p = jnp.exp(s - lse_ref[pl.ds(r0, B), hh * HD:hh * HD + 1])
                dp = lax.dot_general(do2[:, cs], vv[:, cs], (((1,), (1,)), ((), ())), preferred_element_type=F32)
                ds = p * (dp - dd_ref[pl.ds(r0, B), hh * HD:hh * HD + 1])
                dbias_ref[hh] += ds
                ds_bf = ds.astype(BF16)
                dq_ref[pl.ds(r0, B), cs] = jnp.dot(ds_bf, kk[:, cs], preferred_element_type=F32)
                dkbuf[pl.ds(r0, 2 * B), cs] += lax.dot_general(ds_bf, q2[:, cs], (((0,), (0,)), ((), ())),
                                                               preferred_element_type=F32)
                dvbuf[pl.ds(r0, 2 * B), cs] += lax.dot_general(p.astype(BF16), do2[:, cs], (((0,), (0,)), ((), ())),
                                                               preferred_element_type=F32)
            return carry

        lax.fori_loop(0, G, blk, 0)

        @pl.when(next_ok)
        def _():
            qn = qn_ref[...]
            don = don_ref[...]
            kl = kbuf[GR:GR + B, :]
            vl = vbuf[GR:GR + B, :]
            for hh in range(2):
                cs = slice(hh * HD, (hh + 1) * HD)
                s = lax.dot_general(qn[:, cs], kl[:, cs], (((1,), (1,)), ((), ())), preferred_element_type=F32)
                s = s + bias_ref[hh, :, 0:B]
                p = jnp.exp(s - lsen_ref[:, hh * HD:hh * HD + 1])
                dp = lax.dot_general(don[:, cs], vl[:, cs], (((1,), (1,)), ((), ())), preferred_element_type=F32)
                ds = p * (dp - ddn_ref[:, hh * HD:hh * HD + 1])
                dkbuf[GR:GR + B, cs] += lax.dot_general(ds.astype(BF16), qn[:, cs], (((0,), (0,)), ((), ())),
                                                        preferred_element_type=F32)
                dvbuf[GR:GR + B, cs] += lax.dot_general(p.astype(BF16), don[:, cs], (((0,), (0,)), ((), ())),
                                                        preferred_element_type=F32)

        dk_ref[...] = dkbuf[B:B + GR, :]
        dv_ref[...] = dvbuf[B:B + GR, :]

    halo = lambda off: (lambda hp, g: (jnp.maximum(g * G - 1, 0), off + hp))
    main = lambda off: (lambda hp, g: (g, off + hp))
    nxt = lambda off: (lambda hp, g: (jnp.minimum((g + 1) * G, nblk - 1), off + hp))
    big, small = (lambda m: pl.BlockSpec((GR, LANES), m)), (lambda m: pl.BlockSpec((B, LANES), m))
    return _call(body, grid=(4, ng),
                 in_specs=[big(main(0)), big(main(4)), small(halo(4)), big(main(8)), small(halo(8)),
                           big(main(0)), big(main(0)), big(main(0)),
                           small(nxt(0)), small(nxt(0)), small(nxt(0)), small(nxt(0)),
                           pl.BlockSpec((2, B, 2 * B), lambda hp, g: (hp, 0, 0))],
                 out_specs=(big(main(0)), big(main(0)), big(main(0)),
                            pl.BlockSpec((2, B, 2 * B), lambda hp, g: (hp, 0, 0))),
                 out_shape=(_sds((S, ATTN_CH), F32),) * 3 + (_sds((ATTN_HEADS, B, 2 * B), F32),),
                 scratch_shapes=[pltpu.VMEM((B + GR, LANES), BF16), pltpu.VMEM((B + GR, LANES), BF16),
                                 pltpu.VMEM((B + GR, LANES), F32), pltpu.VMEM((B + GR, LANES), F32)], name=name)(
        qkv, qkv, qkv, qkv, qkv, do, lse, dd, qkv, do, lse, dd, bias)


def _attn_combine(dq_list, dk_list, dv_list, *, name, ts):
    S, C = dq_list[0].shape
    P = len(dq_list)
    scale = HEAD_DIM ** -0.5

    def body(*refs):
        out_ref = refs[3 * P]
        for part in range(3):
            acc = refs[part * P][...]
            for r in refs[part * P + 1:(part + 1) * P]:
                acc = acc + r[...]
            if part == 0:
                acc = acc * scale
            out_ref[:, part * C:(part + 1) * C] = acc.astype(BF16)

    row = pl.BlockSpec((ts, C), lambda i: (i, 0))
    return _call(body, grid=(S // ts,), in_specs=[row] * (3 * P),
                 out_specs=pl.BlockSpec((ts, 3 * C), lambda i: (i, 0)), out_shape=_sds((S, 3 * C), BF16), name=name)(
        *dq_list, *dk_list, *dv_list)


def _bias_tables(table, bucket_flat, *, name):
    P, _, K = bucket_flat.shape
    H = table.shape[1]
    KC = 4096

    def body(t_ref, bk_ref, out_ref):
        row = lax.broadcasted_iota(jnp.int32, (N_BUCKETS, KC), 0)
        for c in range(K // KC):
            bk = bk_ref[0, :, c * KC:(c + 1) * KC]
            onehot = (row == bk).astype(F32)
            vals = jnp.dot(t_ref[...], onehot, preferred_element_type=F32, precision=lax.Precision.HIGHEST)
            out_ref[0, :, c * KC:(c + 1) * KC] = jnp.where(bk >= 0, vals, NEG)

    return _call(body, grid=(P,),
                 in_specs=[pl.BlockSpec((H, N_BUCKETS), lambda p: (0, 0)), pl.BlockSpec((1, 1, K), lambda p: (p, 0, 0))],
                 out_specs=pl.BlockSpec((1, H, K), lambda p: (p, 0, 0)), out_shape=_sds((P, H, K), F32),
                 name=name)(table.T, bucket_flat)


def _bias_grad(dbias_flat, bucket_flat, *, name):
    P, H, K = dbias_flat.shape
    KC = 4096

    def body(db_ref, bk_ref, out_ref):
        p = pl.program_id(0)
        acc = jnp.zeros((N_BUCKETS, H), F32)
        row = lax.broadcasted_iota(jnp.int32, (N_BUCKETS, KC), 0)
        for c in range(K // KC):
            onehot = (row == bk_ref[0, :, c * KC:(c + 1) * KC]).astype(F32)
            acc = acc + lax.dot_general(onehot, db_ref[0, :, c * KC:(c + 1) * KC], (((1,), (1,)), ((), ())),
                                        preferred_element_type=F32, precision=lax.Precision.HIGHEST)

        @pl.when(p == 0)
        def _():
            out_ref[...] = acc

        @pl.when(p > 0)
        def _():
            out_ref[...] += acc

    return _call(body, grid=(P,),
                 in_specs=[pl.BlockSpec((1, H, K), lambda p: (p, 0, 0)), pl.BlockSpec((1, 1, K), lambda p: (p, 0, 0))],
                 out_specs=pl.BlockSpec((N_BUCKETS, H), lambda p: (0, 0)), out_shape=_sds((N_BUCKETS, H), F32),
                 name=name)(dbias_flat, bucket_flat)


def _allgather(block, *, name):
    R, W = block.shape

    def body(x_ref, out_ref, send_sems, recv_sems, local_sem):
        x, y, c = lax.axis_index("x"), lax.axis_index("y"), lax.axis_index("c")
        me, sibling = (x, y, c), (x, y, 1 - c)
        chips = [(1 - x, y), (x, 1 - y), (1 - x, 1 - y)]

        def slot(px, py, pc):
            return out_ref.at[4 * px + 2 * py + pc]

        def copy(k, blk, to, src=None):
            return pltpu.make_async_remote_copy(src_ref=slot(*blk) if src is None else src, dst_ref=slot(*blk),
                                                send_sem=send_sems.at[k], recv_sem=recv_sems.at[k], device_id=to,
                                                device_id_type=MESH)

        mine = pltpu.make_async_copy(x_ref, slot(*me), local_sem)
        mine.start()
        first = [copy(0, me, sibling, src=x_ref)]
        first += [copy(1 + j, me, (*chip, c), src=x_ref) for j, chip in enumerate(chips)]
        for cp in first:
            cp.start()
        passed = [copy(4 + j, (*chip, c), sibling) for j, chip in enumerate(chips)]
        for j, chip in enumerate(chips):
            copy(1 + j, (*chip, c), me).wait_recv()
            passed[j].start()
        copy(0, sibling, me).wait_recv()
        for j, chip in enumerate(chips):
            copy(4 + j, (*chip, 1 - c), me).wait_recv()
        for cp in first + passed:
            cp.wait_send()
        mine.wait()

    any_spec = pl.BlockSpec(memory_space=pl.ANY)
    return pl.pallas_call(body, out_shape=_sds((N_DEV, R, W), block.dtype), in_specs=[any_spec], out_specs=any_spec,
                          scratch_shapes=[pltpu.SemaphoreType.DMA((7,)), pltpu.SemaphoreType.DMA((7,)),
                                          pltpu.SemaphoreType.DMA], name=name)(block)


def _exchange(send, *, name):
    _, R, W = send.shape

    def body(send_ref, recv_ref, send_sems, recv_sems, local_sem):
        x, y, c = lax.axis_index("x"), lax.axis_index("y"), lax.axis_index("c")
        me = 4 * x + 2 * y + c
        mine = pltpu.make_async_copy(send_ref.at[me], recv_ref.at[me], local_sem)
        mine.start()
        copies = []
        for k in range(1, N_DEV):
            px = 1 - x if k & 4 else x
            py = 1 - y if k & 2 else y
            pc = 1 - c if k & 1 else c
            cp = pltpu.make_async_remote_copy(src_ref=send_ref.at[4 * px + 2 * py + pc], dst_ref=recv_ref.at[me],
                                              send_sem=send_sems.at[k - 1], recv_sem=recv_sems.at[k - 1],
                                              device_id=(px, py, pc), device_id_type=MESH)
            cp.start()
            copies.append(cp)
        for cp in copies:
            cp.wait_recv()
        for cp in copies:
            cp.wait_send()
        mine.wait()

    any_spec = pl.BlockSpec(memory_space=pl.ANY)
    return pl.pallas_call(body, out_shape=_sds(send.shape, send.dtype), in_specs=[any_spec], out_specs=any_spec,
                          scratch_shapes=[pltpu.SemaphoreType.DMA((7,)), pltpu.SemaphoreType.DMA((7,)),
                                          pltpu.SemaphoreType.DMA], name=name)(send)


def _adamw(w, m, v, g_parts, *, name, tr):
    R, W = w.shape
    bc1 = 1.0 - ADAM_B1 ** ADAM_STEP
    bc2 = 1.0 - ADAM_B2 ** ADAM_STEP

    def body(w_ref, m_ref, v_ref, g_ref, go_ref, d_ref, mo_ref, vo_ref):
        g = g_ref[0]
        for i in range(1, N_DEV):
            g = g + g_ref[i]
        mn = ADAM_B1 * m_ref[...] + (1.0 - ADAM_B1) * g
        vn = ADAM_B2 * v_ref[...] + (1.0 - ADAM_B2) * (g * g)
        m_hat = mn / bc1
        v_hat = vn / bc2
        go_ref[...] = g
        d_ref[...] = -ADAM_LR * (m_hat / (jnp.sqrt(v_hat) + ADAM_EPS) + ADAM_WD * w_ref[...])
        mo_ref[...] = mn
        vo_ref[...] = vn

    row = pl.BlockSpec((tr, W), lambda i: (i, 0))
    return _call(body, grid=(R // tr,), in_specs=[row, row, row, pl.BlockSpec((N_DEV, tr, W), lambda i: (0, i, 0))],
                 out_specs=(row,) * 4, out_shape=(_sds((R, W), F32),) * 4, name=name)(w, m, v, g_parts)


def _round_up(n, k):
    return -(-n // k) * k


def _pack(arrs, width, row_mult):
    pieces, offs, r = [], [], 0
    for a in arrs:
        n = a.size
        rows = _round_up(n, width) // width
        flat = a.reshape(-1)
        if rows * width != n:
            flat = jnp.pad(flat, (0, rows * width - n))
        pieces.append(flat.reshape(rows, width))
        offs.append((r, rows, n))
        r += rows
    total = _round_up(r, row_mult)
    if total != r:
        pieces.append(jnp.zeros((total - r, width), pieces[0].dtype))
    return jnp.concatenate(pieces, axis=0), offs


def _unpack(pack, offs, shapes):
    out = []
    for (r, rows, n), shp in zip(offs, shapes):
        out.append(pack[r:r + rows].reshape(-1)[:n].reshape(shp))
    return out


def _gather_axis(full8, axis):
    moved = jnp.moveaxis(full8, 0, axis)
    shp = list(moved.shape)
    shp[axis:axis + 2] = [shp[axis] * shp[axis + 1]]
    return moved.reshape(shp)


def _split_axis(full, axis):
    shp = list(full.shape)
    shp[axis:axis + 1] = [N_DEV, shp[axis] // N_DEV]
    return jnp.moveaxis(full.reshape(shp), axis, 0)


def kernel(x, w_in, b_in, conv_dw_w, conv_dw_b, conv_ln_g, conv_ln_b, rel_bias_table, gmlp_ln_g, gmlp_ln_b, gmlp_w_s, gmlp_b_s, w_out, b_out, ln1_g, ln1_b, ffn_w_up, ffn_b_up, ffn_conv_w, ffn_conv_b, ffn_w_down, ffn_b_down, ln2_g, ln2_b, loss_target, m_w_in, m_b_in, m_conv_dw_w, m_conv_dw_b, m_conv_ln_g, m_conv_ln_b, m_rel_bias_table, m_gmlp_ln_g, m_gmlp_ln_b, m_gmlp_w_s, m_gmlp_b_s, m_w_out, m_b_out, m_ln1_g, m_ln1_b, m_ffn_w_up, m_ffn_b_up, m_ffn_conv_w, m_ffn_conv_b, m_ffn_w_down, m_ffn_b_down, m_ln2_g, m_ln2_b, v_w_in, v_b_in, v_conv_dw_w, v_conv_dw_b, v_conv_ln_g, v_conv_ln_b, v_rel_bias_table, v_gmlp_ln_g, v_gmlp_ln_b, v_gmlp_w_s, v_gmlp_b_s, v_w_out, v_b_out, v_ln1_g, v_ln1_b, v_ffn_w_up, v_ffn_b_up, v_ffn_conv_w, v_ffn_conv_b, v_ffn_w_down, v_ffn_b_down, v_ln2_g, v_ln2_b):
    W = dict(w_in=w_in, b_in=b_in, conv_dw_w=conv_dw_w, conv_dw_b=conv_dw_b, conv_ln_g=conv_ln_g,
             conv_ln_b=conv_ln_b, rel_bias_table=rel_bias_table, gmlp_ln_g=gmlp_ln_g, gmlp_ln_b=gmlp_ln_b,
             gmlp_w_s=gmlp_w_s, gmlp_b_s=gmlp_b_s, w_out=w_out, b_out=b_out, ln1_g=ln1_g, ln1_b=ln1_b,
             ffn_w_up=ffn_w_up, ffn_b_up=ffn_b_up, ffn_conv_w=ffn_conv_w, ffn_conv_b=ffn_conv_b,
             ffn_w_down=ffn_w_down, ffn_b_down=ffn_b_down, ln2_g=ln2_g, ln2_b=ln2_b)
    Mo = dict(w_in=m_w_in, b_in=m_b_in, conv_dw_w=m_conv_dw_w, conv_dw_b=m_conv_dw_b, conv_ln_g=m_conv_ln_g,
              conv_ln_b=m_conv_ln_b, rel_bias_table=m_rel_bias_table, gmlp_ln_g=m_gmlp_ln_g, gmlp_ln_b=m_gmlp_ln_b,
              gmlp_w_s=m_gmlp_w_s, gmlp_b_s=m_gmlp_b_s, w_out=m_w_out, b_out=m_b_out, ln1_g=m_ln1_g, ln1_b=m_ln1_b,
              ffn_w_up=m_ffn_w_up, ffn_b_up=m_ffn_b_up, ffn_conv_w=m_ffn_conv_w, ffn_conv_b=m_ffn_conv_b,
              ffn_w_down=m_ffn_w_down, ffn_b_down=m_ffn_b_down, ln2_g=m_ln2_g, ln2_b=m_ln2_b)
    Vo = dict(w_in=v_w_in, b_in=v_b_in, conv_dw_w=v_conv_dw_w, conv_dw_b=v_conv_dw_b, conv_ln_g=v_conv_ln_g,
              conv_ln_b=v_conv_ln_b, rel_bias_table=v_rel_bias_table, gmlp_ln_g=v_gmlp_ln_g, gmlp_ln_b=v_gmlp_ln_b,
              gmlp_w_s=v_gmlp_w_s, gmlp_b_s=v_gmlp_b_s, w_out=v_w_out, b_out=v_b_out, ln1_g=v_ln1_g, ln1_b=v_ln1_b,
              ffn_w_up=v_ffn_w_up, ffn_b_up=v_ffn_b_up, ffn_conv_w=v_ffn_conv_w, ffn_conv_b=v_ffn_conv_b,
              ffn_w_down=v_ffn_w_down, ffn_b_down=v_ffn_b_down, ln2_g=v_ln2_g, ln2_b=v_ln2_b)

    xs = x[0]
    target = loss_target[0]
    S, D = xs.shape
    F2 = ffn_b_up.shape[1]
    F = F2 // 2
    ts = min(512, S)
    G = _attn_group(S)
    tc = F // 2 if (F // 2) % LANES == 0 else F

    mat_names, dw_names = SHARDED[:4], SHARDED[4:]
    payload = [W[n].astype(BF16) for n in mat_names]
    payload += [lax.bitcast_convert_type(W[n], BF16) for n in dw_names]
    wpack, woffs = _pack(payload, PACK_W, 16)
    wall = _allgather(wpack, name="weight_allgather")
    full = {}
    for n, (r, rows, cnt), a in zip(mat_names + dw_names, woffs, payload):
        parts = wall[:, r:r + rows].reshape(N_DEV, -1)[:, :cnt].reshape((N_DEV,) + a.shape)
        if n in dw_names:
            parts = lax.bitcast_convert_type(parts, F32)
        full[n] = _gather_axis(parts, SHARD_AXIS[n])

    tables = [_pattern_tables(w, d) for (w, d) in PATTERNS]
    bucket_flat = jnp.asarray(np.stack([np.where(v, b, -1).reshape(1, -1) for (b, v) in tables]).astype(np.int32))
    bias_all = _bias_tables(rel_bias_table, bucket_flat, name="bias_tables")
    biases = [bias_all[p].reshape(ATTN_HEADS, ATTN_BLOCK, 2 * ATTN_BLOCK) for p in range(len(PATTERNS))]
    nbs = [(S // d) // ATTN_BLOCK for (_, d) in PATTERNS]
    dils = [d for (_, d) in PATTERNS]
    scale = HEAD_DIM ** -0.5

    saved = []
    cur = xs
    for l in range(DEPTH):
        Win, Wout, Wup, Wdown = full['w_in'][l], full['w_out'][l], full['ffn_w_up'][l], full['ffn_w_down'][l]
        qcols = slice(2 * CONV_CH, 2 * CONV_CH + ATTN_CH)
        Win_s = Win.at[:, qcols].multiply(scale)
        b_in_s = b_in[l].at[qcols].multiply(scale)
        h_a, qkv, h_c = _mm([cur], [Win_s], bias=b_in_s, tm=ts, name="in_proj",
                            splits=((2 * CONV_CH, F32, 1.0), (3 * ATTN_CH, BF16, 1.0), (2 * GMLP_CH, F32, 1.0)))
        conv_out, hc = _conv_fwd(h_a, full['conv_dw_w'][l], conv_dw_b[l], conv_ln_g[l], conv_ln_b[l],
                                 name="conv_fwd", ts=ts)
        qkv_d = [_to_dilated(qkv, d) for d in dils]
        o_nat, lse_nat = [], []
        for p, d in enumerate(dils):
            o_p, lse_p = _attn_fwd(qkv_d[p], biases[p], name=f"attn_fwd_d{d}", nb=nbs[p], G=G)
            o_nat.append(_from_dilated(o_p, d))
            lse_nat.append(_from_dilated(lse_p, d))
        attn_out, lse = _attn_merge(o_nat, lse_nat, name="attn_merge", ts=ts)
        w_tril = jnp.tril(gmlp_w_s[l]).astype(BF16)
        bs_rows = jnp.repeat(gmlp_b_s[l].T, GMLP_GROUP_DIM, axis=1)
        gm_out = _gmlp_fwd(h_c, gmlp_ln_g[l], gmlp_ln_b[l], w_tril, bs_rows, name="gmlp_fwd", ts=ts)
        x1, xhat1, rstd1 = _mm([conv_out, attn_out, gm_out],
                               [Wout[:CONV_CH], Wout[CONV_CH:CONV_CH + ATTN_CH], Wout[CONV_CH + ATTN_CH:]],
                               bias=b_out[l], resid=cur, resid_scale=ALPHA, ln=(ln1_g[l], ln1_b[l]), tm=ts,
                               name="out_proj_ln")
        hu = _mm([x1], [Wup], bias=ffn_b_up[l], tm=ts, tn=F, name="ffn_up")
        act = _ffn_act_fwd(hu, full['ffn_conv_w'][l], ffn_conv_b[l], name="ffn_act_fwd", ts=min(256, S), tc=tc)
        x2, xhat2, rstd2 = _mm([act], [Wdown], bias=ffn_b_down[l], resid=x1, resid_scale=ALPHA,
                               ln=(ln2_g[l], ln2_b[l]), tm=ts, name="ffn_down_ln")
        saved.append(dict(x0=cur, h_a=h_a, h_c=h_c, qkv_d=qkv_d, hc=hc, conv_out=conv_out, attn_out=attn_out,
                          lse=lse, gm_out=gm_out, w_tril=w_tril, bs_rows=bs_rows, x1=x1, xhat1=xhat1, rstd1=rstd1,
                          hu=hu, act=act, xhat2=xhat2, rstd2=rstd2))
        cur = x2

    grads = {n: [None] * DEPTH for n in WEIGHTS if n != 'rel_bias_table'}
    drel = None
    dx = None
    loss_part = None
    tk = min(512, S)
    for l in reversed(range(DEPTH)):
        sv = saved[l]
        Win, Wout, Wup, Wdown = full['w_in'][l], full['w_out'][l], full['ffn_w_up'][l], full['ffn_w_down'][l]
        if dx is None:
            dz2, dg2, db2, dzs2, loss_part = _ln_bwd(sv['xhat2'], sv['rstd2'], ln2_g[l], b=ln2_b[l], target=target,
                                                     name="ln2_bwd_loss", ts=ts)
        else:
            dz2, dg2, db2, dzs2 = _ln_bwd(sv['xhat2'], sv['rstd2'], ln2_g[l], dy=dx, name="ln_bwd", ts=ts)
        grads['ln2_g'][l], grads['ln2_b'][l], grads['ffn_b_down'][l] = dg2[0], db2[0], dzs2[0]
        grads['ffn_w_down'][l] = _mm_tn(sv['act'], dz2, tm=F // 2 if (F // 2) % LANES == 0 else F, tn=D, tk=tk,
                                        name="dw_down")
        dact = _mm([dz2], [Wdown.T], tm=ts, name="dact")
        dhg, dhv, dwg, dwv, dbg, dbv, dug, duv = _ffn_act_bwd(sv['hu'], dact, full['ffn_conv_w'][l], ffn_conv_b[l],
                                                              name="ffn_act_bwd", ts=min(256, S), tc=tc)
        grads['ffn_conv_w'][l] = jnp.concatenate([dwg, dwv], axis=1)
        grads['ffn_conv_b'][l] = jnp.concatenate([dbg, dbv], axis=1)[0]
        grads['ffn_b_up'][l] = jnp.concatenate([dug, duv], axis=1)[0]
        grads['ffn_w_up'][l] = jnp.concatenate(
            [_mm_tn(sv['x1'], dhg, tm=D, tn=tc, tk=tk, name="dw_up"),
             _mm_tn(sv['x1'], dhv, tm=D, tn=tc, tk=tk, name="dw_up")], axis=1)
        WupT = Wup.T
        dx1 = _mm([dhg, dhv], [WupT[:F], WupT[F:]], resid=dz2, resid_scale=ALPHA, tm=ts, name="dx1")
        dz1, dg1, db1, dzs1 = _ln_bwd(sv['xhat1'], sv['rstd1'], ln1_g[l], dy=dx1, name="ln_bwd", ts=ts)
        grads['ln1_g'][l], grads['ln1_b'][l], grads['b_out'][l] = dg1[0], db1[0], dzs1[0]
        grads['w_out'][l] = jnp.concatenate(
            [_mm_tn(sv['conv_out'], dz1, tm=CONV_CH, tn=D, tk=tk, name="dw_out_conv"),
             _mm_tn(sv['attn_out'], dz1, tm=ATTN_CH, tn=D, tk=tk, name="dw_out_attn"),
             _mm_tn(sv['gm_out'], dz1, tm=GMLP_CH, tn=D, tk=tk, name="dw_out_conv")], axis=0)
        dc_conv, dc_attn, dc_gm = _mm([dz1], [Wout.T], tm=ts, name="dcat",
                                      splits=((CONV_CH, F32, 1.0), (ATTN_CH, F32, 1.0), (GMLP_CH, F32, 1.0)))
        dh_a, ddw, ddwb, dclg, dclb = _conv_bwd(sv['h_a'], sv['hc'], dc_conv, full['conv_dw_w'][l], conv_ln_g[l],
                                                conv_ln_b[l], name="conv_bwd", ts=ts)
        grads['conv_dw_w'][l], grads['conv_dw_b'][l] = ddw[:CONV_WIDTH], ddwb[0]
        grads['conv_ln_g'][l], grads['conv_ln_b'][l] = dclg[0], dclb[0]
        dd, do_bf = _attn_prep(dc_attn, sv['attn_out'], name="attn_prep", ts=ts)
        dqs, dks, dvs, dbs = [], [], [], []
        for p, d in enumerate(dils):
            dq, dk, dv, dbias = _attn_bwd(sv['qkv_d'][p], _to_dilated(do_bf, d), _to_dilated(sv['lse'], d),
                                          _to_dilated(dd, d), biases[p], name=f"attn_bwd_d{d}", nb=nbs[p], G=G)
            dqs.append(_from_dilated(dq, d))
            dks.append(_from_dilated(dk, d))
            dvs.append(_from_dilated(dv, d))
            dbs.append(dbias.reshape(1, ATTN_HEADS, -1))
        dqkv = _attn_combine(dqs, dks, dvs, name="attn_combine", ts=ts)
        dr = _bias_grad(jnp.concatenate(dbs, axis=0), bucket_flat, name="bias_grad")
        drel = dr if drel is None else drel + dr
        w_tril_t = jnp.swapaxes(sv['w_tril'], 1, 2)
        dh_c, dws, dbs_acc, dglg, dglb = _gmlp_bwd(sv['h_c'], dc_gm, gmlp_ln_g[l], gmlp_ln_b[l], sv['w_tril'],
                                                   w_tril_t, sv['bs_rows'], name="gmlp_bwd", ts=ts)
        grads['gmlp_w_s'][l] = jnp.tril(dws)
        grads['gmlp_b_s'][l] = dbs_acc[:, :GMLP_GROUPS].T
        grads['gmlp_ln_g'][l], grads['gmlp_ln_b'][l] = dglg[0], dglb[0]
        dwa, ca = _mm_tn(sv['x0'], dh_a, tm=D, tn=2 * CONV_CH, tk=tk, colsum=True, name="dw_in_side")
        dwq, cq = _mm_tn(sv['x0'], dqkv, tm=D, tn=ATTN_CH, tk=tk, colsum=True, name="dw_in_qkv")
        dwc, cc = _mm_tn(sv['x0'], dh_c, tm=D, tn=2 * GMLP_CH, tk=tk, colsum=True, name="dw_in_side")
        grads['w_in'][l] = jnp.concatenate([dwa, dwq, dwc], axis=1)
        grads['b_in'][l] = jnp.concatenate([ca, cq, cc], axis=1)[0]
        WinT = Win.T
        e1, e2 = 2 * CONV_CH, 2 * CONV_CH + 3 * ATTN_CH
        dx = _mm([dh_a, dqkv, dh_c], [WinT[:e1], WinT[e1:e2], WinT[e2:]], resid=dz1, resid_scale=ALPHA, tm=ts,
                 name="dx0")

    gfull = {n: jnp.stack(v) for n, v in grads.items()}
    gfull['rel_bias_table'] = drel

    pieces, goffs, r = [], [], 0
    for n in SHARDED:
        parts = _split_axis(gfull[n], SHARD_AXIS[n]).reshape(N_DEV, -1)
        cnt = parts.shape[1]
        rows = _round_up(cnt, PACK_W) // PACK_W
        if rows * PACK_W != cnt:
            parts = jnp.pad(parts, ((0, 0), (0, rows * PACK_W - cnt)))
        pieces.append(parts.reshape(N_DEV, rows, PACK_W))
        goffs.append((r, rows, cnt))
        r += rows
    tr = 512 if r >= 512 else _round_up(r, 8)
    total = _round_up(r, tr)
    if total != r:
        pieces.append(jnp.zeros((N_DEV, total - r, PACK_W), F32))
    send = jnp.concatenate(pieces, axis=1)
    recv = _exchange(send, name="grad_exchange")
    packs = [_pack([src[n] for n in SHARDED], PACK_W, tr)[0] for src in (W, Mo, Vo)]
    outs = _adamw(packs[0], packs[1], packs[2], recv, name="adamw_sharded", tr=tr)
    shard_out = [_unpack(o, goffs, [W[n].shape for n in SHARDED]) for o in outs]

    gsmall, soffs = _pack([gfull[n] for n in SMALL], LANES, 8)
    gall = _allgather(gsmall, name="small_grad_allgather")
    spacks = [_pack([src[n] for n in SMALL], LANES, 8)[0] for src in (W, Mo, Vo)]
    souts = _adamw(spacks[0], spacks[1], spacks[2], gall, name="adamw_small", tr=gsmall.shape[0])
    small_out = [_unpack(o, soffs, [W[n].shape for n in SMALL]) for o in souts]

    loss = lax.psum(loss_part[0, 0], ("x", "y", "c"))
    by_kind = []
    for kind in range(4):
        d = dict(zip(SHARDED, shard_out[kind]))
        d.update(zip(SMALL, small_out[kind]))
        by_kind.append([d[n] for n in WEIGHTS])
    return (loss, dx[None], *by_kind[0], *by_kind[1], *by_kind[2], *by_kind[3])
```

```python
import math

import numpy as np
import jax
import jax.numpy as jnp
from jax import lax
from jax.experimental import pallas as pl
from jax.experimental.pallas import tpu as pltpu

F32 = jnp.float32
BF16 = jnp.bfloat16

DEPTH = 2
HEAD_DIM = 64
CONV_CH = 256
CONV_WIDTH = 31
ATTN_HEADS = 8
ATTN_CH = ATTN_HEADS * HEAD_DIM
PATTERNS = ((128, 1), (512, 4), (2048, 16))
ATTN_BLOCK = 128
N_BUCKETS = 32
MAX_DISTANCE = 2048
GMLP_CH = 256
GMLP_GROUPS = 4
GMLP_GROUP_DIM = GMLP_CH // GMLP_GROUPS
CHUNK = 128
FFN_CONV_WIDTH = 3
LN_EPS = 1e-5
ALPHA = (2.0 * DEPTH) ** 0.25
ADAM_LR = 0.001
ADAM_B1 = 0.9
ADAM_B2 = 0.999
ADAM_EPS = 1e-08
ADAM_WD = 0.01
ADAM_STEP = 10
NEG = -1e30
N_DEV = 8
LANES = 128
CONV_HALO = 32
FFN_HALO = 8
FFN_ROWS = 16
PACK_W = 512
MESH = pl.DeviceIdType.MESH

WEIGHTS = ['w_in', 'b_in', 'conv_dw_w', 'conv_dw_b', 'conv_ln_g', 'conv_ln_b', 'rel_bias_table', 'gmlp_ln_g',
           'gmlp_ln_b', 'gmlp_w_s', 'gmlp_b_s', 'w_out', 'b_out', 'ln1_g', 'ln1_b', 'ffn_w_up', 'ffn_b_up',
           'ffn_conv_w', 'ffn_conv_b', 'ffn_w_down', 'ffn_b_down', 'ln2_g', 'ln2_b']
SHARDED = ['w_in', 'w_out', 'ffn_w_up', 'ffn_w_down', 'conv_dw_w', 'ffn_conv_w']
SHARD_AXIS = {'w_in': 2, 'w_out': 1, 'ffn_w_up': 2, 'ffn_w_down': 1, 'conv_dw_w': 2, 'ffn_conv_w': 2}
SMALL = [n for n in WEIGHTS if n not in SHARDED]


def _call(body, *, grid=(), vmem_mb=48, **kw):
    params = pltpu.CompilerParams(dimension_semantics=("arbitrary",) * len(grid), vmem_limit_bytes=vmem_mb << 20)
    return pl.pallas_call(body, grid=grid, compiler_params=params, **kw)


def _sds(shape, dtype):
    return jax.ShapeDtypeStruct(shape, dtype)


def _ln_rows(z):
    mu = jnp.mean(z, axis=-1, keepdims=True)
    zc = z - mu
    var = jnp.mean(zc * zc, axis=-1, keepdims=True)
    rstd = lax.rsqrt(var + LN_EPS)
    return zc * rstd, rstd


def _ln_bwd_rows(dxhat, xhat, rstd):
    m1 = jnp.mean(dxhat, axis=-1, keepdims=True)
    m2 = jnp.mean(dxhat * xhat, axis=-1, keepdims=True)
    return rstd * (dxhat - m1 - xhat * m2)


def _colsum(v):
    return jnp.sum(v, axis=0, keepdims=True)


def _mm(a_list, w_list, *, name, tm, tn=None, bias=None, resid=None, resid_scale=1.0, ln=None, splits=None,
        out_dtype=F32):
    na = len(a_list)
    M = a_list[0].shape[0]
    N = w_list[0].shape[1]
    tn = N if tn is None else tn
    assert M % tm == 0 and N % tn == 0
    assert ln is None or tn == N
    assert splits is None or tn == N

    def body(*refs):
        a_refs, w_refs = refs[:na], refs[na:2 * na]
        pos = 2 * na
        acc = None
        for a_ref, w_ref in zip(a_refs, w_refs):
            t = jnp.dot(a_ref[...].astype(BF16), w_ref[...], preferred_element_type=F32)
            acc = t if acc is None else acc + t
        if bias is not None:
            acc = acc + refs[pos][...]
            pos += 1
        if resid is not None:
            acc = resid_scale * refs[pos][...] + acc
            pos += 1
        if ln is not None:
            g_ref, b_ref = refs[pos], refs[pos + 1]
            y_ref, xhat_ref, rstd_ref = refs[pos + 2], refs[pos + 3], refs[pos + 4]
            xhat, rstd = _ln_rows(acc)
            y_ref[...] = xhat * g_ref[...] + b_ref[...]
            xhat_ref[...] = xhat
            rstd_ref[...] = rstd
        elif splits is not None:
            c0 = 0
            for o_ref, (width, dtype, scale) in zip(refs[pos:], splits):
                part = acc[:, c0:c0 + width]
                if scale != 1.0:
                    part = part * scale
                o_ref[...] = part.astype(dtype)
                c0 += width
        else:
            refs[pos][...] = acc.astype(out_dtype)

    in_specs = [pl.BlockSpec((tm, a.shape[1]), lambda j, i: (i, 0)) for a in a_list]
    in_specs += [pl.BlockSpec((w.shape[0], tn), lambda j, i: (0, j)) for w in w_list]
    args = list(a_list) + list(w_list)
    if bias is not None:
        in_specs.append(pl.BlockSpec((1, tn), lambda j, i: (0, j)))
        args.append(bias.reshape(1, N))
    if resid is not None:
        in_specs.append(pl.BlockSpec((tm, tn), lambda j, i: (i, j)))
        args.append(resid)
    if ln is not None:
        in_specs += [pl.BlockSpec((1, N), lambda j, i: (0, 0))] * 2
        args += [ln[0].reshape(1, N), ln[1].reshape(1, N)]
        out_shape = (_sds((M, N), F32), _sds((M, N), F32), _sds((M, 1), F32))
        out_specs = (pl.BlockSpec((tm, N), lambda j, i: (i, 0)), pl.BlockSpec((tm, N), lambda j, i: (i, 0)),
                     pl.BlockSpec((tm, 1), lambda j, i: (i, 0)))
    elif splits is not None:
        out_shape = tuple(_sds((M, w), d) for (w, d, _) in splits)
        out_specs = tuple(pl.BlockSpec((tm, w), lambda j, i: (i, 0)) for (w, _, _) in splits)
    else:
        out_shape = _sds((M, N), out_dtype)
        out_specs = pl.BlockSpec((tm, tn), lambda j, i: (i, j))
    return _call(body, grid=(N // tn, M // tm), in_specs=in_specs, out_specs=out_specs, out_shape=out_shape,
                 name=name, vmem_mb=56)(*args)


def _mm_tn(a, dy, *, name, tm, tn, tk, colsum=False):
    S, Ka = a.shape
    N = dy.shape[1]
    assert S % tk == 0 and Ka % tm == 0 and N % tn == 0

    def body(a_ref, dy_ref, out_ref, *cs):
        i, k = pl.program_id(1), pl.program_id(2)
        dyb = dy_ref[...]
        part = lax.dot_general(a_ref[...].astype(BF16), dyb.astype(BF16), (((0,), (0,)), ((), ())),
                               preferred_element_type=F32)

        @pl.when(k == 0)
        def _():
            out_ref[...] = part

        @pl.when(k > 0)
        def _():
            out_ref[...] += part

        if colsum:
            cs_ref = cs[0]
            s = _colsum(dyb.astype(F32))

            @pl.when((i == 0) & (k == 0))
            def _():
                cs_ref[...] = s

            @pl.when((i == 0) & (k > 0))
            def _():
                cs_ref[...] += s

    out_shape = [_sds((Ka, N), F32)]
    out_specs = [pl.BlockSpec((tm, tn), lambda j, i, k: (i, j))]
    if colsum:
        out_shape.append(_sds((1, N), F32))
        out_specs.append(pl.BlockSpec((1, tn), lambda j, i, k: (0, j)))
    res = _call(body, grid=(N // tn, Ka // tm, S // tk),
                in_specs=[pl.BlockSpec((tk, tm), lambda j, i, k: (k, i)), pl.BlockSpec((tk, tn), lambda j, i, k: (k, j))],
                out_specs=tuple(out_specs), out_shape=tuple(out_shape), name=name, vmem_mb=56)(a, dy)
    return res if colsum else res[0]


def _ln_bwd(xhat, rstd, g, *, name, ts, dy=None, b=None, target=None):
    S, D = xhat.shape
    from_loss = target is not None

    def body(*refs):
        if from_loss:
            xhat_ref, rstd_ref, g_ref, b_ref, t_ref, dz_ref, dg_ref, db_ref, dzs_ref, loss_ref = refs
        else:
            xhat_ref, rstd_ref, g_ref, dy_ref, dz_ref, dg_ref, db_ref, dzs_ref = refs
        i = pl.program_id(0)
        xh = xhat_ref[...]
        gg = g_ref[...]
        if from_loss:
            err = xh * gg + b_ref[...] - t_ref[...]
            dyv = err * (1.0 / D)
            lsum = (0.5 / D) * jnp.sum(err * err, axis=(0, 1), keepdims=True)
        else:
            dyv = dy_ref[...]
        dz = _ln_bwd_rows(dyv * gg, xh, rstd_ref[...])
        dz_ref[...] = dz
        parts = [(dg_ref, _colsum(dyv * xh)), (db_ref, _colsum(dyv)), (dzs_ref, _colsum(dz))]
        if from_loss:
            parts.append((loss_ref, lsum))

        @pl.when(i == 0)
        def _():
            for r, v in parts:
                r[...] = v

        @pl.when(i > 0)
        def _():
            for r, v in parts:
                r[...] += v

    row = pl.BlockSpec((ts, D), lambda i: (i, 0))
    vec = pl.BlockSpec((1, D), lambda i: (0, 0))
    in_specs = [row, pl.BlockSpec((ts, 1), lambda i: (i, 0)), vec]
    args = [xhat, rstd, g.reshape(1, D)]
    if from_loss:
        in_specs += [vec, row]
        args += [b.reshape(1, D), target]
    else:
        in_specs += [row]
        args += [dy]
    out_shape = [_sds((S, D), F32), _sds((1, D), F32), _sds((1, D), F32), _sds((1, D), F32)]
    out_specs = [row, vec, vec, vec]
    if from_loss:
        out_shape.append(_sds((1, 1), F32))
        out_specs.append(pl.BlockSpec((1, 1), lambda i: (0, 0)))
    return _call(body, grid=(S // ts,), in_specs=in_specs, out_specs=tuple(out_specs), out_shape=tuple(out_shape),
                 name=name)(*args)


def _glu(v):
    return v[:, :CONV_CH] * jax.nn.sigmoid(v[:, CONV_CH:])


def _conv_fwd(h_a, dw_w, dw_b, ln_g, ln_b, *, name, ts):
    S = h_a.shape[0]
    C, K, HB = CONV_CH, CONV_WIDTH, CONV_HALO
    RC = 128

    def body(h_ref, halo_ref, w_ref, b_ref, g_ref, bb_ref, out_ref, hc_ref, gbuf):
        i = pl.program_id(0)
        gbuf[0:HB, :] = jnp.where(i > 0, _glu(halo_ref[...]), 0.0)
        gbuf[HB:HB + ts, :] = _glu(h_ref[...])
        for r0 in range(0, ts, RC):
            acc = jnp.zeros((RC, C), F32) + b_ref[...]
            for k in range(K):
                acc = acc + w_ref[k:k + 1, :] * gbuf[pl.ds(r0 + HB - (K - 1) + k, RC), :]
            hc_ref[r0:r0 + RC, :] = acc
            xhat, _ = _ln_rows(acc)
            hn = xhat * g_ref[...] + bb_ref[...]
            out_ref[r0:r0 + RC, :] = (hn * jax.nn.sigmoid(hn)).astype(BF16)

    nb = ts // HB
    vec = pl.BlockSpec((1, C), lambda i: (0, 0))
    return _call(body, grid=(S // ts,),
                 in_specs=[pl.BlockSpec((ts, 2 * C), lambda i: (i, 0)),
                           pl.BlockSpec((HB, 2 * C), lambda i: (jnp.maximum(i * nb - 1, 0), 0)),
                           pl.BlockSpec((K, C), lambda i: (0, 0)), vec, vec, vec],
                 out_specs=(pl.BlockSpec((ts, C), lambda i: (i, 0)), pl.BlockSpec((ts, C), lambda i: (i, 0))),
                 out_shape=(_sds((S, C), BF16), _sds((S, C), F32)),
                 scratch_shapes=[pltpu.VMEM((HB + ts, C), F32)], name=name)(
        h_a, h_a, dw_w, dw_b.reshape(1, C), ln_g.reshape(1, C), ln_b.reshape(1, C))


def _conv_bwd(h_a, hc, dout, dw_w, ln_g, ln_b, *, name, ts):
    S = h_a.shape[0]
    C, K, HB = CONV_CH, CONV_WIDTH, CONV_HALO
    RC = 128
    n = S // ts

    def dconv_out(hc_v, do_v, g_ref, bb_ref):
        xhat, rstd = _ln_rows(hc_v)
        hn = xhat * g_ref[...] + bb_ref[...]
        sg = jax.nn.sigmoid(hn)
        dhn = do_v * (sg * (1.0 + hn * (1.0 - sg)))
        return _ln_bwd_rows(dhn * g_ref[...], xhat, rstd), dhn, xhat

    def body(h_ref, hprev_ref, hc_ref, hcnext_ref, do_ref, donext_ref, w_ref, g_ref, bb_ref,
             dh_ref, dw_ref, dwb_ref, dg_ref, db_ref, gbuf, dbuf):
        i = pl.program_id(0)
        hv = h_ref[...]
        gbuf[0:HB, :] = jnp.where(i > 0, _glu(hprev_ref[...]), 0.0)
        gbuf[HB:HB + ts, :] = _glu(hv)
        dhc, dhn, xhat = dconv_out(hc_ref[...], do_ref[...], g_ref, bb_ref)
        dhc_next, _, _ = dconv_out(hcnext_ref[...], donext_ref[...], g_ref, bb_ref)
        dbuf[0:ts, :] = dhc
        dbuf[ts:ts + HB, :] = jnp.where(i < n - 1, dhc_next, 0.0)
        dw_rows = []
        for k in range(K):
            acc_k = jnp.zeros((1, C), F32)
            for r0 in range(0, ts, RC):
                acc_k = acc_k + _colsum(dbuf[r0:r0 + RC, :] * gbuf[pl.ds(r0 + HB - (K - 1) + k, RC), :])
            dw_rows.append(acc_k)
        dw_rows.append(jnp.zeros((1, C), F32))
        dw_tile = jnp.concatenate(dw_rows, axis=0)
        for r0 in range(0, ts, RC):
            acc = jnp.zeros((RC, C), F32)
            for k in range(K):
                acc = acc + w_ref[k:k + 1, :] * dbuf[pl.ds(r0 + (K - 1) - k, RC), :]
            a = hv[r0:r0 + RC, :C]
            sg = jax.nn.sigmoid(hv[r0:r0 + RC, C:])
            dh_ref[r0:r0 + RC, :C] = (acc * sg).astype(BF16)
            dh_ref[r0:r0 + RC, C:] = (acc * a * sg * (1.0 - sg)).astype(BF16)
        parts = [(dw_ref, dw_tile), (dwb_ref, _colsum(dhc)), (dg_ref, _colsum(dhn * xhat)), (db_ref, _colsum(dhn))]

        @pl.when(i == 0)
        def _():
            for r, v in parts:
                r[...] = v

        @pl.when(i > 0)
        def _():
            for r, v in parts:
                r[...] += v

    nb = ts // HB
    last = S // HB - 1
    vec = pl.BlockSpec((1, C), lambda i: (0, 0))
    nxt = lambda i: (jnp.minimum((i + 1) * nb, last), 0)
    return _call(body, grid=(n,),
                 in_specs=[pl.BlockSpec((ts, 2 * C), lambda i: (i, 0)),
                           pl.BlockSpec((HB, 2 * C), lambda i: (jnp.maximum(i * nb - 1, 0), 0)),
                           pl.BlockSpec((ts, C), lambda i: (i, 0)), pl.BlockSpec((HB, C), nxt),
                           pl.BlockSpec((ts, C), lambda i: (i, 0)), pl.BlockSpec((HB, C), nxt),
                           pl.BlockSpec((K, C), lambda i: (0, 0)), vec, vec],
                 out_specs=(pl.BlockSpec((ts, 2 * C), lambda i: (i, 0)), pl.BlockSpec((K + 1, C), lambda i: (0, 0)),
                            vec, vec, vec),
                 out_shape=(_sds((S, 2 * C), BF16), _sds((K + 1, C), F32), _sds((1, C), F32), _sds((1, C), F32),
                            _sds((1, C), F32)),
                 scratch_shapes=[pltpu.VMEM((HB + ts, C), F32), pltpu.VMEM((ts + HB, C), F32)], name=name)(
        h_a, h_a, hc, hc, dout, dout, dw_w, ln_g.reshape(1, C), ln_b.reshape(1, C))


def _gmlp_mix(vn_bf, w_ref, mix_buf, ts):
    for ch in range(ts // CHUNK):
        for g in range(GMLP_GROUPS):
            vg = vn_bf[ch * CHUNK:(ch + 1) * CHUNK, g * GMLP_GROUP_DIM:(g + 1) * GMLP_GROUP_DIM]
            mix_buf[ch * CHUNK:(ch + 1) * CHUNK, g * GMLP_GROUP_DIM:(g + 1) * GMLP_GROUP_DIM] = jnp.dot(
                w_ref[g], vg, preferred_element_type=F32)


def _gmlp_fwd(h_c, ln_g, ln_b, w_tril, bs_rows, *, name, ts):
    S = h_c.shape[0]
    C = GMLP_CH

    def body(h_ref, g_ref, b_ref, w_ref, bs_ref, out_ref, mix_buf):
        hv = h_ref[...]
        xhat, _ = _ln_rows(hv[:, C:])
        vn = (xhat * g_ref[...] + b_ref[...]).astype(BF16)
        _gmlp_mix(vn, w_ref, mix_buf, ts)
        for ch in range(ts // CHUNK):
            rows = slice(ch * CHUNK, (ch + 1) * CHUNK)
            out_ref[rows, :] = (hv[rows, :C] * (mix_buf[rows, :] + bs_ref[...])).astype(BF16)

    vec = pl.BlockSpec((1, C), lambda i: (0, 0))
    return _call(body, grid=(S // ts,),
                 in_specs=[pl.BlockSpec((ts, 2 * C), lambda i: (i, 0)), vec, vec,
                           pl.BlockSpec((GMLP_GROUPS, CHUNK, CHUNK), lambda i: (0, 0, 0)),
                           pl.BlockSpec((CHUNK, C), lambda i: (0, 0))],
                 out_specs=pl.BlockSpec((ts, C), lambda i: (i, 0)), out_shape=_sds((S, C), BF16),
                 scratch_shapes=[pltpu.VMEM((ts, C), F32)], name=name)(
        h_c, ln_g.reshape(1, C), ln_b.reshape(1, C), w_tril, bs_rows)


def _gmlp_bwd(h_c, dout, ln_g, ln_b, w_tril, w_tril_t, bs_rows, *, name, ts):
    S = h_c.shape[0]
    C, G, GD = GMLP_CH, GMLP_GROUPS, GMLP_GROUP_DIM

    def body(h_ref, do_ref, g_ref, b_ref, w_ref, wt_ref, bs_ref, dh_ref, dw_ref, dbs_ref, dg_ref, db_ref,
             mix_buf, dvn_buf):
        i = pl.program_id(0)
        hv = h_ref[...]
        u = hv[:, :C]
        xhat, rstd = _ln_rows(hv[:, C:])
        vn = (xhat * g_ref[...] + b_ref[...]).astype(BF16)
        _gmlp_mix(vn, w_ref, mix_buf, ts)
        do = do_ref[...]
        dmixed = do * u
        dm_bf = dmixed.astype(BF16)
        lane = lax.broadcasted_iota(jnp.int32, (CHUNK, LANES), 1)
        dbs = jnp.zeros((CHUNK, LANES), F32)
        dws = [jnp.zeros((CHUNK, CHUNK), F32) for _ in range(G)]
        for ch in range(ts // CHUNK):
            rows = slice(ch * CHUNK, (ch + 1) * CHUNK)
            dh_ref[rows, :C] = (do[rows, :] * (mix_buf[rows, :] + bs_ref[...])).astype(BF16)
            for g in range(G):
                cols = slice(g * GD, (g + 1) * GD)
                dmg = dm_bf[rows, cols]
                dvn_buf[rows, cols] = jnp.dot(wt_ref[g], dmg, preferred_element_type=F32)
                dws[g] = dws[g] + lax.dot_general(dmg, vn[rows, cols], (((1,), (1,)), ((), ())),
                                                  preferred_element_type=F32)
                rs = jnp.sum(dmixed[rows, cols], axis=1, keepdims=True)
                dbs = dbs + jnp.where(lane == g, rs, 0.0)
        dvn = dvn_buf[...]
        dh_ref[:, C:] = _ln_bwd_rows(dvn * g_ref[...], xhat, rstd).astype(BF16)
        dgv, dbv = _colsum(dvn * xhat), _colsum(dvn)

        @pl.when(i == 0)
        def _():
            for g in range(G):
                dw_ref[g] = dws[g]
            dbs_ref[...] = dbs
            dg_ref[...] = dgv
            db_ref[...] = dbv

        @pl.when(i > 0)
        def _():
            for g in range(G):
                dw_ref[g] += dws[g]
            dbs_ref[...] += dbs
            dg_ref[...] += dgv
            db_ref[...] += dbv

    vec = pl.BlockSpec((1, C), lambda i: (0, 0))
    wspec = pl.BlockSpec((G, CHUNK, CHUNK), lambda i: (0, 0, 0))
    return _call(body, grid=(S // ts,),
                 in_specs=[pl.BlockSpec((ts, 2 * C), lambda i: (i, 0)),
                           pl.BlockSpec((ts, C), lambda i: (i, 0)), vec, vec, wspec, wspec,
                           pl.BlockSpec((CHUNK, C), lambda i: (0, 0))],
                 out_specs=(pl.BlockSpec((ts, 2 * C), lambda i: (i, 0)), wspec,
                            pl.BlockSpec((CHUNK, LANES), lambda i: (0, 0)), vec, vec),
                 out_shape=(_sds((S, 2 * C), BF16), _sds((G, CHUNK, CHUNK), F32), _sds((CHUNK, LANES), F32),
                            _sds((1, C), F32), _sds((1, C), F32)),
                 scratch_shapes=[pltpu.VMEM((ts, C), F32), pltpu.VMEM((ts, C), F32)], name=name)(
        h_c, dout, ln_g.reshape(1, C), ln_b.reshape(1, C), w_tril, w_tril_t, bs_rows)


def _conv3(buf, w_ref, b_ref, rows, off):
    return (w_ref[0:1, :] * buf[pl.ds(off - 2, rows), :] + w_ref[1:2, :] * buf[pl.ds(off - 1, rows), :]
            + w_ref[2:3, :] * buf[pl.ds(off, rows), :]) + b_ref[...]


def _ffn_act_fwd(hu, cw, cb, *, name, ts, tc):
    S, F2 = hu.shape
    F = F2 // 2
    nj = F // tc
    HB = FFN_HALO
    nb = ts // HB

    def body(g_ref, v_ref, gh_ref, vh_ref, wg_ref, wv_ref, bg_ref, bv_ref, act_ref, gbuf, vbuf):
        i = pl.program_id(0)
        gbuf[0:HB, :] = jnp.where(i > 0, gh_ref[...], 0.0)
        gbuf[HB:HB + ts, :] = g_ref[...]
        vbuf[0:HB, :] = jnp.where(i > 0, vh_ref[...], 0.0)
        vbuf[HB:HB + ts, :] = v_ref[...]
        for r0 in range(0, ts, FFN_ROWS):
            gc = _conv3(gbuf, wg_ref, bg_ref, FFN_ROWS, HB + r0)
            vc = _conv3(vbuf, wv_ref, bv_ref, FFN_ROWS, HB + r0)
            act_ref[r0:r0 + FFN_ROWS, :] = (gc * jax.nn.sigmoid(gc) * vc).astype(BF16)

    prev = lambda off: (lambda i, j: (jnp.maximum(i * nb - 1, 0), j + off))
    return _call(body, grid=(S // ts, nj),
                 in_specs=[pl.BlockSpec((ts, tc), lambda i, j: (i, j)), pl.BlockSpec((ts, tc), lambda i, j: (i, j + nj)),
                           pl.BlockSpec((HB, tc), prev(0)), pl.BlockSpec((HB, tc), prev(nj)),
                           pl.BlockSpec((3, tc), lambda i, j: (0, j)), pl.BlockSpec((3, tc), lambda i, j: (0, j + nj)),
                           pl.BlockSpec((1, tc), lambda i, j: (0, j)), pl.BlockSpec((1, tc), lambda i, j: (0, j + nj))],
                 out_specs=pl.BlockSpec((ts, tc), lambda i, j: (i, j)), out_shape=_sds((S, F), BF16),
                 scratch_shapes=[pltpu.VMEM((HB + ts, tc), F32), pltpu.VMEM((HB + ts, tc), F32)], name=name)(
        hu, hu, hu, hu, cw, cw, cb.reshape(1, F2), cb.reshape(1, F2))


def _ffn_act_bwd(hu, dact, cw, cb, *, name, ts, tc):
    S, F2 = hu.shape
    F = F2 // 2
    nj = F // tc
    HB = FFN_HALO
    nb = ts // HB
    n = S // ts
    last = S // HB - 1

    def body(g_ref, v_ref, gp_ref, vp_ref, gn_ref, vn_ref, da_ref, dan_ref, wg_ref, wv_ref, bg_ref, bv_ref,
             dhg_ref, dhv_ref, dwg_ref, dwv_ref, dbg_ref, dbv_ref, dug_ref, duv_ref, gbuf, vbuf, dgb, dvb, accg, accv):
        i = pl.program_id(1)
        for buf, p_ref, t_ref, n_ref in ((gbuf, gp_ref, g_ref, gn_ref), (vbuf, vp_ref, v_ref, vn_ref)):
            buf[0:HB, :] = jnp.where(i > 0, p_ref[...], 0.0)
            buf[HB:HB + ts, :] = t_ref[...]
            buf[HB + ts:HB + ts + HB, :] = n_ref[...]
        RC = FFN_ROWS
        for r0 in list(range(0, ts, RC)) + [ts]:
            rows = RC if r0 < ts else HB
            gc = _conv3(gbuf, wg_ref, bg_ref, rows, HB + r0)
            vc = _conv3(vbuf, wv_ref, bv_ref, rows, HB + r0)
            da = da_ref[r0:r0 + rows, :] if r0 < ts else jnp.where(i < n - 1, dan_ref[...], 0.0)
            sg = jax.nn.sigmoid(gc)
            dvb[r0:r0 + rows, :] = da * (gc * sg)
            dgb[r0:r0 + rows, :] = da * vc * (sg * (1.0 + gc * (1.0 - sg)))

        @pl.when(i == 0)
        def _():
            accg[...] = jnp.zeros_like(accg)
            accv[...] = jnp.zeros_like(accv)

        def fold(v):
            out = v[0:8, :]
            for r in range(8, RC, 8):
                out = out + v[r:r + 8, :]
            return out

        for dbuf, buf, w_ref, dh_ref, acc in ((dgb, gbuf, wg_ref, dhg_ref, accg), (dvb, vbuf, wv_ref, dhv_ref, accv)):
            for r0 in range(0, ts, RC):
                taps = [dbuf[r0 + 2 - k:r0 + 2 - k + RC, :] for k in range(3)]
                dhu = w_ref[2:3, :] * taps[2] + w_ref[1:2, :] * taps[1] + w_ref[0:1, :] * taps[0]
                dh_ref[r0:r0 + RC, :] = dhu.astype(BF16)
                h = buf[HB + r0:HB + r0 + RC, :]
                for k in range(3):
                    acc[8 * k:8 * k + 8, :] += fold(h * taps[k])
                acc[24:32, :] += fold(taps[2])
                acc[32:40, :] += fold(dhu)

        @pl.when(i == n - 1)
        def _():
            for acc, dw_ref, db_ref, du_ref in ((accg, dwg_ref, dbg_ref, dug_ref), (accv, dwv_ref, dbv_ref, duv_ref)):
                for k in range(3):
                    dw_ref[k:k + 1, :] = _colsum(acc[8 * k:8 * k + 8, :])
                db_ref[...] = _colsum(acc[24:32, :])
                du_ref[...] = _colsum(acc[32:40, :])

    prev = lambda off: (lambda j, i: (jnp.maximum(i * nb - 1, 0), j + off))
    nxt = lambda off: (lambda j, i: (jnp.minimum((i + 1) * nb, last), j + off))
    tile = lambda off: (lambda j, i: (i, j + off))
    vec = lambda rows: pl.BlockSpec((rows, tc), lambda j, i: (0, j))
    F2s = cb.reshape(1, F2)
    return _call(body, grid=(nj, n),
                 in_specs=[pl.BlockSpec((ts, tc), tile(0)), pl.BlockSpec((ts, tc), tile(nj)),
                           pl.BlockSpec((HB, tc), prev(0)), pl.BlockSpec((HB, tc), prev(nj)),
                           pl.BlockSpec((HB, tc), nxt(0)), pl.BlockSpec((HB, tc), nxt(nj)),
                           pl.BlockSpec((ts, tc), tile(0)), pl.BlockSpec((HB, tc), nxt(0)),
                           pl.BlockSpec((3, tc), lambda j, i: (0, j)), pl.BlockSpec((3, tc), lambda j, i: (0, j + nj)),
                           pl.BlockSpec((1, tc), lambda j, i: (0, j)), pl.BlockSpec((1, tc), lambda j, i: (0, j + nj))],
                 out_specs=(pl.BlockSpec((ts, tc), tile(0)), pl.BlockSpec((ts, tc), tile(0)),
                            vec(3), vec(3), vec(1), vec(1), vec(1), vec(1)),
                 out_shape=(_sds((S, F), BF16), _sds((S, F), BF16), _sds((3, F), F32), _sds((3, F), F32),
                            _sds((1, F), F32), _sds((1, F), F32), _sds((1, F), F32), _sds((1, F), F32)),
                 scratch_shapes=[pltpu.VMEM((ts + 2 * HB, tc), F32), pltpu.VMEM((ts + 2 * HB, tc), F32),
                                 pltpu.VMEM((ts + HB, tc), F32), pltpu.VMEM((ts + HB, tc), F32),
                                 pltpu.VMEM((40, tc), F32), pltpu.VMEM((40, tc), F32)], name=name)(
        hu, hu, hu, hu, hu, hu, dact, dact, cw, cw, F2s, F2s)


def _t5_bucket(dist):
    max_exact = N_BUCKETS // 2
    d = np.maximum(dist, 1).astype(np.float64)
    large = max_exact + (np.log(d / max_exact) / math.log(MAX_DISTANCE / max_exact)
                         * (N_BUCKETS - max_exact)).astype(np.int32)
    large = np.minimum(large, N_BUCKETS - 1)
    return np.where(dist < max_exact, dist, large).astype(np.int32)


def _pattern_tables(window, dilation):
    qi = np.arange(ATTN_BLOCK)[:, None]
    kj = np.arange(2 * ATTN_BLOCK)[None, :]
    dist = qi + ATTN_BLOCK - kj
    valid = (dist >= 0) & (dist <= window // dilation)
    bucket = _t5_bucket(np.clip(dist, 0, None) * dilation)
    return bucket, valid


def _dilate_qkv(qkv, *, name, ts):
    S, C = qkv.shape
    dils = [d for (_, d) in PATTERNS if d > 1]

    def body(x_ref, nat_ref, *outs):
        nat_ref[...] = x_ref[...].astype(BF16)
        for d, o_ref in zip(dils, outs):
            for r in range(d):
                o_ref[r] = x_ref[pl.ds(r, ts // d, stride=d), :].astype(BF16)

    out_shape = (_sds((S, C), BF16),) + tuple(_sds((d, S // d, C), BF16) for d in dils)
    out_specs = (_dil_spec(1, ts),) + tuple(_dil_spec(d, ts) for d in dils)
    res = _call(body, grid=(S // ts, C // LANES), in_specs=[_dil_spec(1, ts)], out_specs=out_specs,
                out_shape=out_shape, name=name)(qkv)
    return [res[0]] + [r.reshape(S, C) for r in res[1:]]


def _attn_group(S):
    nb_min = (S // PATTERNS[-1][1]) // ATTN_BLOCK
    return math.gcd(8, nb_min)


def _attn_fwd(qkv, bias, *, name, nb, G):
    S = qkv.shape[0]
    B, HD = ATTN_BLOCK, HEAD_DIM
    GR = G * B
    ng = S // GR

    def body(q_ref, k_ref, kh_ref, v_ref, vh_ref, bias_ref, o_ref, lse_ref, kbuf, vbuf):
        g = pl.program_id(1)
        halo_ok = (g * G) % nb != 0
        kbuf[0:B, :] = kh_ref[...]
        kbuf[B:B + GR, :] = k_ref[...]
        vbuf[0:B, :] = vh_ref[...]
        vbuf[B:B + GR, :] = v_ref[...]
        col = lax.broadcasted_iota(jnp.int32, (B, 2 * B), 1)

        for bi in range(G):
            r0 = bi * B
            q2 = q_ref[r0:r0 + B, :]
            kk = kbuf[r0:r0 + 2 * B, :]
            vv = vbuf[r0:r0 + 2 * B, :]
            for hh in range(2):
                cs = slice(hh * HD, (hh + 1) * HD)
                s = lax.dot_general(q2[:, cs], kk[:, cs], (((1,), (1,)), ((), ())), preferred_element_type=F32)
                s = s + bias_ref[hh]
                if bi == 0:
                    s = jnp.where(jnp.logical_and(jnp.logical_not(halo_ok), col < B), NEG, s)
                m = jnp.max(s, axis=1, keepdims=True)
                p = jnp.exp(s - m)
                l = jnp.sum(p, axis=1, keepdims=True)
                o = jnp.dot(p.astype(BF16), vv[:, cs], preferred_element_type=F32) / l
                o_ref[r0:r0 + B, cs] = o
                lse_ref[r0:r0 + B, cs] = jnp.broadcast_to(m + jnp.log(l), (B, HD))

    halo = lambda off: (lambda hp, g: (jnp.maximum(g * G - 1, 0), off + hp))
    main = lambda off: (lambda hp, g: (g, off + hp))
    return _call(body, grid=(4, ng),
                 in_specs=[pl.BlockSpec((GR, LANES), main(0)), pl.BlockSpec((GR, LANES), main(4)),
                           pl.BlockSpec((B, LANES), halo(4)), pl.BlockSpec((GR, LANES), main(8)),
                           pl.BlockSpec((B, LANES), halo(8)), pl.BlockSpec((2, B, 2 * B), lambda hp, g: (hp, 0, 0))],
                 out_specs=(pl.BlockSpec((GR, LANES), main(0)), pl.BlockSpec((GR, LANES), main(0))),
                 out_shape=(_sds((S, ATTN_CH), F32), _sds((S, ATTN_CH), F32)),
                 scratch_shapes=[pltpu.VMEM((B + GR, LANES), BF16), pltpu.VMEM((B + GR, LANES), BF16)], name=name)(
        qkv, qkv, qkv, qkv, qkv, bias)


def _dil_spec(d, ts, col0=0):
    if d == 1:
        return pl.BlockSpec((ts, LANES), lambda i, j: (i, j + col0))
    return pl.BlockSpec((d, ts // d, LANES), lambda i, j: (0, i, j + col0))


def _dil_view(a, d):
    return a if d == 1 else a.reshape(d, a.shape[0] // d, a.shape[1])


def _undilate(src_ref, nat_ref, d, ts, accumulate=False):
    for r in range(d):
        rows = pl.ds(r, ts // d, stride=d)
        if accumulate:
            nat_ref[rows, :] = nat_ref[rows, :] + src_ref[r]
        else:
            nat_ref[rows, :] = src_ref[r]


def _attn_merge(o_list, lse_list, dils, *, name, ts):
    S, C = o_list[0].shape
    P = len(o_list)
    nd = sum(1 for d in dils if d > 1)

    def body(*refs):
        o_refs, l_refs = refs[:P], refs[P:2 * P]
        out_ref, lse_ref = refs[2 * P], refs[2 * P + 1]
        lse_d_refs = refs[2 * P + 2:2 * P + 2 + nd]
        scratch = list(refs[2 * P + 2 + nd:])
        os_, ls = [], []
        for o_ref, l_ref, d in zip(o_refs, l_refs, dils):
            if d > 1:
                so, sl = scratch.pop(0), scratch.pop(0)
                _undilate(o_ref, so, d, ts)
                _undilate(l_ref, sl, d, ts)
                o_ref, l_ref = so, sl
            os_.append(o_ref)
            ls.append(l_ref[...])
        o_refs = os_
        m = ls[0]
        for l in ls[1:]:
            m = jnp.maximum(m, l)
        ws = [jnp.exp(l - m) for l in ls]
        den = ws[0]
        for w in ws[1:]:
            den = den + w
        num = ws[0] * o_refs[0][...]
        for w, r in zip(ws[1:], o_refs[1:]):
            num = num + w * r[...]
        out_ref[...] = num / den
        lse_ref[...] = m + jnp.log(den)
        for l_out, d in zip(lse_d_refs, [d for d in dils if d > 1]):
            for r in range(d):
                l_out[r] = lse_ref[pl.ds(r, ts // d, stride=d), :]

    row = _dil_spec(1, ts)
    dd = [d for d in dils if d > 1]
    res = _call(body, grid=(S // ts, C // LANES), in_specs=[_dil_spec(d, ts) for d in dils] * 2,
                out_specs=(row, row) + tuple(_dil_spec(d, ts) for d in dd),
                out_shape=(_sds((S, C), F32), _sds((S, C), F32)) + tuple(_sds((d, S // d, C), F32) for d in dd),
                scratch_shapes=[pltpu.VMEM((ts, LANES), F32)] * (2 * nd), name=name)(
        *[_dil_view(o, d) for o, d in zip(o_list, dils)], *[_dil_view(l, d) for l, d in zip(lse_list, dils)])
    lse_by_d = {1: res[1]}
    lse_by_d.update({d: r.reshape(S, C) for d, r in zip(dd, res[2:])})
    return res[0], [lse_by_d[d] for d in dils]


def _attn_prep(dout, out, dils, *, name, ts):
    S, C = out.shape
    HD = HEAD_DIM
    dd = [d for d in dils if d > 1]
    nd = len(dd)

    def body(do_ref, o_ref, d_ref, dob_ref, *outs):
        do = do_ref[...]
        prod = do * o_ref[...]
        dob_ref[...] = do.astype(BF16)
        for h in range(LANES // HD):
            cs = slice(h * HD, (h + 1) * HD)
            d_ref[:, cs] = jnp.broadcast_to(jnp.sum(prod[:, cs], axis=1, keepdims=True), (ts, HD))
        for d, dd_out, do_out in zip(dd, outs[:nd], outs[nd:]):
            for r in range(d):
                rows = pl.ds(r, ts // d, stride=d)
                dd_out[r] = d_ref[rows, :]
                do_out[r] = do_ref[rows, :].astype(BF16)

    row = _dil_spec(1, ts)
    res = _call(body, grid=(S // ts, C // LANES), in_specs=[row, row],
                out_specs=(row, row) + tuple(_dil_spec(d, ts) for d in dd) * 2,
                out_shape=(_sds((S, C), F32), _sds((S, C), BF16)) + tuple(_sds((d, S // d, C), F32) for d in dd)
                + tuple(_sds((d, S // d, C), BF16) for d in dd), name=name)(dout, out)
    dd_by_d, do_by_d = {1: res[0]}, {1: res[1]}
    dd_by_d.update({d: r.reshape(S, C) for d, r in zip(dd, res[2:2 + nd])})
    do_by_d.update({d: r.reshape(S, C) for d, r in zip(dd, res[2 + nd:])})
    return [dd_by_d[d] for d in dils], [do_by_d[d] for d in dils]


def _attn_bwd(qkv, do, lse, dd, bias, *, name, nb, G):
    S = qkv.shape[0]
    B, HD = ATTN_BLOCK, HEAD_DIM
    GR = G * B
    ng = S // GR
    nblk = S // B

    def body(q_ref, k_ref, kh_ref, v_ref, vh_ref, do_ref, lse_ref, dd_ref, qn_ref, don_ref, lsen_ref, ddn_ref,
             bias_ref, dq_ref, dk_ref, dv_ref, dbias_ref, kbuf, vbuf, dkp, dvp):
        g = pl.program_id(1)
        halo_ok = (g * G) % nb != 0
        next_ok = jnp.logical_and(((g + 1) * G) % nb != 0, g < ng - 1)
        kbuf[0:B, :] = kh_ref[...]
        kbuf[B:B + GR, :] = k_ref[...]
        vbuf[0:B, :] = vh_ref[...]
        vbuf[B:B + GR, :] = v_ref[...]
        col = lax.broadcasted_iota(jnp.int32, (B, 2 * B), 1)

        def tn_dot(a, b):
            return lax.dot_general(a, b, (((0,), (0,)), ((), ())), preferred_element_type=F32)

        @pl.when(g == 0)
        def _():
            dbias_ref[...] = jnp.zeros_like(dbias_ref)

        for bi in range(G):
            r0 = bi * B
            q2 = q_ref[r0:r0 + B, :]
            do2 = do_ref[r0:r0 + B, :]
            kk = kbuf[r0:r0 + 2 * B, :]
            vv = vbuf[r0:r0 + 2 * B, :]
            for hh in range(2):
                cs = slice(hh * HD, (hh + 1) * HD)
                s = lax.dot_general(q2[:, cs], kk[:, cs], (((1,), (1,)), ((), ())), preferred_element_type=F32)
                s = s + bias_ref[hh]
                if bi == 0:
                    s = jnp.where(jnp.logical_and(jnp.logical_not(halo_ok), col < B), NEG, s)
                p = jnp.exp(s - lse_ref[r0:r0 + B, hh * HD:hh * HD + 1])
                dp = lax.dot_general(do2[:, cs], vv[:, cs], (((1,), (1,)), ((), ())), preferred_element_type=F32)
                ds = p * (dp - dd_ref[r0:r0 + B, hh * HD:hh * HD + 1])
                dbias_ref[hh] += ds
                ds_bf, p_bf = ds.astype(BF16), p.astype(BF16)
                dq_ref[r0:r0 + B, cs] = jnp.dot(ds_bf, kk[:, cs], preferred_element_type=F32)
                dk_ref[r0:r0 + B, cs] = tn_dot(ds_bf[:, B:], q2[:, cs])
                dv_ref[r0:r0 + B, cs] = tn_dot(p_bf[:, B:], do2[:, cs])
                if bi > 0:
                    dkp[r0 - B:r0, cs] = tn_dot(ds_bf[:, :B], q2[:, cs])
                    dvp[r0 - B:r0, cs] = tn_dot(p_bf[:, :B], do2[:, cs])

        @pl.when(next_ok)
        def _():
            qn = qn_ref[...]
            don = don_ref[...]
            kl = kbuf[GR:GR + B, :]
            vl = vbuf[GR:GR + B, :]
            for hh in range(2):
                cs = slice(hh * HD, (hh + 1) * HD)
                s = lax.dot_general(qn[:, cs], kl[:, cs], (((1,), (1,)), ((), ())), preferred_element_type=F32)
                s = s + bias_ref[hh, :, 0:B]
                p = jnp.exp(s - lsen_ref[:, hh * HD:hh * HD + 1])
                dp = lax.dot_general(don[:, cs], vl[:, cs], (((1,), (1,)), ((), ())), preferred_element_type=F32)
                ds = p * (dp - ddn_ref[:, hh * HD:hh * HD + 1])
                dkp[GR - B:GR, cs] = tn_dot(ds.astype(BF16), qn[:, cs])
                dvp[GR - B:GR, cs] = tn_dot(p.astype(BF16), don[:, cs])

        @pl.when(jnp.logical_not(next_ok))
        def _():
            dkp[GR - B:GR, :] = jnp.zeros((B, LANES), F32)
            dvp[GR - B:GR, :] = jnp.zeros((B, LANES), F32)

        dk_ref[...] += dkp[...]
        dv_ref[...] += dvp[...]

    halo = lambda off: (lambda hp, g: (jnp.maximum(g * G - 1, 0), off + hp))
    main = lambda off: (lambda hp, g: (g, off + hp))
    nxt = lambda off: (lambda hp, g: (jnp.minimum((g + 1) * G, nblk - 1), off + hp))
    big, small = (lambda m: pl.BlockSpec((GR, LANES), m)), (lambda m: pl.BlockSpec((B, LANES), m))
    return _call(body, grid=(4, ng),
                 in_specs=[big(main(0)), big(main(4)), small(halo(4)), big(main(8)), small(halo(8)),
                           big(main(0)), big(main(0)), big(main(0)),
                           small(nxt(0)), small(nxt(0)), small(nxt(0)), small(nxt(0)),
                           pl.BlockSpec((2, B, 2 * B), lambda hp, g: (hp, 0, 0))],
                 out_specs=(big(main(0)), big(main(0)), big(main(0)),
                            pl.BlockSpec((2, B, 2 * B), lambda hp, g: (hp, 0, 0))),
                 out_shape=(_sds((S, ATTN_CH), F32),) * 3 + (_sds((ATTN_HEADS, B, 2 * B), F32),),
                 scratch_shapes=[pltpu.VMEM((B + GR, LANES), BF16), pltpu.VMEM((B + GR, LANES), BF16),
                                 pltpu.VMEM((GR, LANES), F32), pltpu.VMEM((GR, LANES), F32)], name=name)(
        qkv, qkv, qkv, qkv, qkv, do, lse, dd, qkv, do, lse, dd, bias)


def _attn_combine(dq_list, dk_list, dv_list, dils, *, name, ts):
    S, C = dq_list[0].shape
    P = len(dq_list)
    scale = HEAD_DIM ** -0.5
    assert dils[0] == 1

    def body(*refs):
        out_refs, acc = refs[3 * P:3 * P + 3], refs[3 * P + 3]
        for part in range(3):
            acc[...] = refs[part * P][...]
            for r, d in zip(refs[part * P + 1:(part + 1) * P], dils[1:]):
                _undilate(r, acc, d, ts, accumulate=True)
            tot = acc[...]
            if part == 0:
                tot = tot * scale
            out_refs[part][...] = tot.astype(BF16)

    return _call(body, grid=(S // ts, C // LANES), in_specs=[_dil_spec(d, ts) for d in dils] * 3,
                 out_specs=(_dil_spec(1, ts),) * 3, out_shape=(_sds((S, C), BF16),) * 3,
                 scratch_shapes=[pltpu.VMEM((ts, LANES), F32)], name=name)(
        *[_dil_view(a, d) for lst in (dq_list, dk_list, dv_list) for a, d in zip(lst, dils)])


def _bias_tables(table, bucket_flat, *, name):
    P, _, K = bucket_flat.shape
    H = table.shape[1]
    KC = 4096

    def body(t_ref, bk_ref, out_ref):
        row = lax.broadcasted_iota(jnp.int32, (N_BUCKETS, KC), 0)
        for c in range(K // KC):
            bk = bk_ref[0, :, c * KC:(c + 1) * KC]
            onehot = (row == bk).astype(F32)
            vals = jnp.dot(t_ref[...], onehot, preferred_element_type=F32, precision=lax.Precision.HIGHEST)
            out_ref[0, :, c * KC:(c + 1) * KC] = jnp.where(bk >= 0, vals, NEG)

    return _call(body, grid=(P,),
                 in_specs=[pl.BlockSpec((H, N_BUCKETS), lambda p: (0, 0)), pl.BlockSpec((1, 1, K), lambda p: (p, 0, 0))],
                 out_specs=pl.BlockSpec((1, H, K), lambda p: (p, 0, 0)), out_shape=_sds((P, H, K), F32),
                 name=name)(table.T, bucket_flat)


def _bias_grad(dbias_flat, bucket_flat, *, name):
    P, H, K = dbias_flat.shape
    KC = 4096

    def body(db_ref, bk_ref, out_ref):
        p = pl.program_id(0)
        acc = jnp.zeros((N_BUCKETS, H), F32)
        row = lax.broadcasted_iota(jnp.int32, (N_BUCKETS, KC), 0)
        for c in range(K // KC):
            onehot = (row == bk_ref[0, :, c * KC:(c + 1) * KC]).astype(F32)
            acc = acc + lax.dot_general(onehot, db_ref[0, :, c * KC:(c + 1) * KC], (((1,), (1,)), ((), ())),
                                        preferred_element_type=F32, precision=lax.Precision.HIGHEST)

        @pl.when(p == 0)
        def _():
            out_ref[...] = acc

        @pl.when(p > 0)
        def _():
            out_ref[...] += acc

    return _call(body, grid=(P,),
                 in_specs=[pl.BlockSpec((1, H, K), lambda p: (p, 0, 0)), pl.BlockSpec((1, 1, K), lambda p: (p, 0, 0))],
                 out_specs=pl.BlockSpec((N_BUCKETS, H), lambda p: (0, 0)), out_shape=_sds((N_BUCKETS, H), F32),
                 name=name)(dbias_flat, bucket_flat)


def _allgather(block, *, name):
    R, W = block.shape

    def body(x_ref, out_ref, send_sems, recv_sems, local_sem):
        x, y, c = lax.axis_index("x"), lax.axis_index("y"), lax.axis_index("c")
        me, sibling = (x, y, c), (x, y, 1 - c)
        chips = [(1 - x, y), (x, 1 - y), (1 - x, 1 - y)]

        def slot(px, py, pc):
            return out_ref.at[4 * px + 2 * py + pc]

        def copy(k, blk, to, src=None):
            return pltpu.make_async_remote_copy(src_ref=slot(*blk) if src is None else src, dst_ref=slot(*blk),
                                                send_sem=send_sems.at[k], recv_sem=recv_sems.at[k], device_id=to,
                                                device_id_type=MESH)

        mine = pltpu.make_async_copy(x_ref, slot(*me), local_sem)
        mine.start()
        first = [copy(0, me, sibling, src=x_ref)]
        first += [copy(1 + j, me, (*chip, c), src=x_ref) for j, chip in enumerate(chips)]
        for cp in first:
            cp.start()
        passed = [copy(4 + j, (*chip, c), sibling) for j, chip in enumerate(chips)]
        for j, chip in enumerate(chips):
            copy(1 + j, (*chip, c), me).wait_recv()
            passed[j].start()
        copy(0, sibling, me).wait_recv()
        for j, chip in enumerate(chips):
            copy(4 + j, (*chip, 1 - c), me).wait_recv()
        for cp in first + passed:
            cp.wait_send()
        mine.wait()

    any_spec = pl.BlockSpec(memory_space=pl.ANY)
    return pl.pallas_call(body, out_shape=_sds((N_DEV, R, W), block.dtype), in_specs=[any_spec], out_specs=any_spec,
                          scratch_shapes=[pltpu.SemaphoreType.DMA((7,)), pltpu.SemaphoreType.DMA((7,)),
                                          pltpu.SemaphoreType.DMA], name=name)(block)


def _exchange(send, *, name):
    _, R, W = send.shape

    def body(send_ref, recv_ref, send_sems, recv_sems, local_sem):
        x, y, c = lax.axis_index("x"), lax.axis_index("y"), lax.axis_index("c")
        me = 4 * x + 2 * y + c
        mine = pltpu.make_async_copy(send_ref.at[me], recv_ref.at[me], local_sem)
        mine.start()
        copies = []
        for k in range(1, N_DEV):
            px = 1 - x if k & 4 else x
            py = 1 - y if k & 2 else y
            pc = 1 - c if k & 1 else c
            cp = pltpu.make_async_remote_copy(src_ref=send_ref.at[4 * px + 2 * py + pc], dst_ref=recv_ref.at[me],
                                              send_sem=send_sems.at[k - 1], recv_sem=recv_sems.at[k - 1],
                                              device_id=(px, py, pc), device_id_type=MESH)
            cp.start()
            copies.append(cp)
        for cp in copies:
            cp.wait_recv()
        for cp in copies:
            cp.wait_send()
        mine.wait()

    any_spec = pl.BlockSpec(memory_space=pl.ANY)
    return pl.pallas_call(body, out_shape=_sds(send.shape, send.dtype), in_specs=[any_spec], out_specs=any_spec,
                          scratch_shapes=[pltpu.SemaphoreType.DMA((7,)), pltpu.SemaphoreType.DMA((7,)),
                                          pltpu.SemaphoreType.DMA], name=name)(send)


def _adamw(w, m, v, g_parts, *, name, tr):
    R, W = w.shape
    bc1 = 1.0 - ADAM_B1 ** ADAM_STEP
    bc2 = 1.0 - ADAM_B2 ** ADAM_STEP

    def body(w_ref, m_ref, v_ref, g_ref, go_ref, d_ref, mo_ref, vo_ref):
        g = g_ref[0].astype(F32)
        for i in range(1, N_DEV):
            g = g + g_ref[i].astype(F32)
        mn = ADAM_B1 * m_ref[...] + (1.0 - ADAM_B1) * g
        vn = ADAM_B2 * v_ref[...] + (1.0 - ADAM_B2) * (g * g)
        m_hat = mn / bc1
        v_hat = vn / bc2
        go_ref[...] = g
        d_ref[...] = -ADAM_LR * (m_hat / (jnp.sqrt(v_hat) + ADAM_EPS) + ADAM_WD * w_ref[...])
        mo_ref[...] = mn
        vo_ref[...] = vn

    row = pl.BlockSpec((tr, W), lambda i: (i, 0))
    return _call(body, grid=(R // tr,), in_specs=[row, row, row, pl.BlockSpec((N_DEV, tr, W), lambda i: (0, i, 0))],
                 out_specs=(row,) * 4, out_shape=(_sds((R, W), F32),) * 4, name=name)(w, m, v, g_parts)


def _round_up(n, k):
    return -(-n // k) * k


def _pack(arrs, width, row_mult):
    pieces, offs, r = [], [], 0
    for a in arrs:
        n = a.size
        rows = _round_up(n, width) // width
        flat = a.reshape(-1)
        if rows * width != n:
            flat = jnp.pad(flat, (0, rows * width - n))
        pieces.append(flat.reshape(rows, width))
        offs.append((r, rows, n))
        r += rows
    total = _round_up(r, row_mult)
    if total != r:
        pieces.append(jnp.zeros((total - r, width), pieces[0].dtype))
    return jnp.concatenate(pieces, axis=0), offs


def _unpack(pack, offs, shapes):
    out = []
    for (r, rows, n), shp in zip(offs, shapes):
        out.append(pack[r:r + rows].reshape(-1)[:n].reshape(shp))
    return out


def _gather_axis(full8, axis):
    moved = jnp.moveaxis(full8, 0, axis)
    shp = list(moved.shape)
    shp[axis:axis + 2] = [shp[axis] * shp[axis + 1]]
    return moved.reshape(shp)


def _split_axis(full, axis):
    shp = list(full.shape)
    shp[axis:axis + 1] = [N_DEV, shp[axis] // N_DEV]
    return jnp.moveaxis(full.reshape(shp), axis, 0)


def kernel(x, w_in, b_in, conv_dw_w, conv_dw_b, conv_ln_g, conv_ln_b, rel_bias_table, gmlp_ln_g, gmlp_ln_b, gmlp_w_s, gmlp_b_s, w_out, b_out, ln1_g, ln1_b, ffn_w_up, ffn_b_up, ffn_conv_w, ffn_conv_b, ffn_w_down, ffn_b_down, ln2_g, ln2_b, loss_target, m_w_in, m_b_in, m_conv_dw_w, m_conv_dw_b, m_conv_ln_g, m_conv_ln_b, m_rel_bias_table, m_gmlp_ln_g, m_gmlp_ln_b, m_gmlp_w_s, m_gmlp_b_s, m_w_out, m_b_out, m_ln1_g, m_ln1_b, m_ffn_w_up, m_ffn_b_up, m_ffn_conv_w, m_ffn_conv_b, m_ffn_w_down, m_ffn_b_down, m_ln2_g, m_ln2_b, v_w_in, v_b_in, v_conv_dw_w, v_conv_dw_b, v_conv_ln_g, v_conv_ln_b, v_rel_bias_table, v_gmlp_ln_g, v_gmlp_ln_b, v_gmlp_w_s, v_gmlp_b_s, v_w_out, v_b_out, v_ln1_g, v_ln1_b, v_ffn_w_up, v_ffn_b_up, v_ffn_conv_w, v_ffn_conv_b, v_ffn_w_down, v_ffn_b_down, v_ln2_g, v_ln2_b):
    W = dict(w_in=w_in, b_in=b_in, conv_dw_w=conv_dw_w, conv_dw_b=conv_dw_b, conv_ln_g=conv_ln_g,
             conv_ln_b=conv_ln_b, rel_bias_table=rel_bias_table, gmlp_ln_g=gmlp_ln_g, gmlp_ln_b=gmlp_ln_b,
             gmlp_w_s=gmlp_w_s, gmlp_b_s=gmlp_b_s, w_out=w_out, b_out=b_out, ln1_g=ln1_g, ln1_b=ln1_b,
             ffn_w_up=ffn_w_up, ffn_b_up=ffn_b_up, ffn_conv_w=ffn_conv_w, ffn_conv_b=ffn_conv_b,
             ffn_w_down=ffn_w_down, ffn_b_down=ffn_b_down, ln2_g=ln2_g, ln2_b=ln2_b)
    Mo = dict(w_in=m_w_in, b_in=m_b_in, conv_dw_w=m_conv_dw_w, conv_dw_b=m_conv_dw_b, conv_ln_g=m_conv_ln_g,
              conv_ln_b=m_conv_ln_b, rel_bias_table=m_rel_bias_table, gmlp_ln_g=m_gmlp_ln_g, gmlp_ln_b=m_gmlp_ln_b,
              gmlp_w_s=m_gmlp_w_s, gmlp_b_s=m_gmlp_b_s, w_out=m_w_out, b_out=m_b_out, ln1_g=m_ln1_g, ln1_b=m_ln1_b,
              ffn_w_up=m_ffn_w_up, ffn_b_up=m_ffn_b_up, ffn_conv_w=m_ffn_conv_w, ffn_conv_b=m_ffn_conv_b,
              ffn_w_down=m_ffn_w_down, ffn_b_down=m_ffn_b_down, ln2_g=m_ln2_g, ln2_b=m_ln2_b)
    Vo = dict(w_in=v_w_in, b_in=v_b_in, conv_dw_w=v_conv_dw_w, conv_dw_b=v_conv_dw_b, conv_ln_g=v_conv_ln_g,
              conv_ln_b=v_conv_ln_b, rel_bias_table=v_rel_bias_table, gmlp_ln_g=v_gmlp_ln_g, gmlp_ln_b=v_gmlp_ln_b,
              gmlp_w_s=v_gmlp_w_s, gmlp_b_s=v_gmlp_b_s, w_out=v_w_out, b_out=v_b_out, ln1_g=v_ln1_g, ln1_b=v_ln1_b,
              ffn_w_up=v_ffn_w_up, ffn_b_up=v_ffn_b_up, ffn_conv_w=v_ffn_conv_w, ffn_conv_b=v_ffn_conv_b,
              ffn_w_down=v_ffn_w_down, ffn_b_down=v_ffn_b_down, ln2_g=v_ln2_g, ln2_b=v_ln2_b)

    xs = x[0]
    target = loss_target[0]
    S, D = xs.shape
    F2 = ffn_b_up.shape[1]
    F = F2 // 2
    ts = min(512, S)
    G = _attn_group(S)
    tc = F // 2 if (F // 2) % LANES == 0 else F

    mat_names, dw_names = SHARDED[:4], SHARDED[4:]
    payload = [W[n].astype(BF16) for n in mat_names]
    payload += [lax.bitcast_convert_type(W[n], BF16) for n in dw_names]
    wpack, woffs = _pack(payload, PACK_W, 16)
    wall = _allgather(wpack, name="weight_allgather")
    full = {}
    for n, (r, rows, cnt), a in zip(mat_names + dw_names, woffs, payload):
        parts = wall[:, r:r + rows].reshape(N_DEV, -1)[:, :cnt].reshape((N_DEV,) + a.shape)
        if n in dw_names:
            parts = lax.bitcast_convert_type(parts, F32)
        full[n] = _gather_axis(parts, SHARD_AXIS[n])

    tables = [_pattern_tables(w, d) for (w, d) in PATTERNS]
    bucket_flat = jnp.asarray(np.stack([np.where(v, b, -1).reshape(1, -1) for (b, v) in tables]).astype(np.int32))
    bias_all = _bias_tables(rel_bias_table, bucket_flat, name="bias_tables")
    biases = [bias_all[p].reshape(ATTN_HEADS, ATTN_BLOCK, 2 * ATTN_BLOCK) for p in range(len(PATTERNS))]
    nbs = [(S // d) // ATTN_BLOCK for (_, d) in PATTERNS]
    dils = [d for (_, d) in PATTERNS]
    scale = HEAD_DIM ** -0.5

    saved = []
    cur = xs
    for l in range(DEPTH):
        Win, Wout, Wup, Wdown = full['w_in'][l], full['w_out'][l], full['ffn_w_up'][l], full['ffn_w_down'][l]
        qcols = slice(2 * CONV_CH, 2 * CONV_CH + ATTN_CH)
        Win_s = Win.at[:, qcols].multiply(scale)
        b_in_s = b_in[l].at[qcols].multiply(scale)
        h_a, qkv, h_c = _mm([cur], [Win_s], bias=b_in_s, tm=ts, name="in_proj",
                            splits=((2 * CONV_CH, F32, 1.0), (3 * ATTN_CH, F32, 1.0), (2 * GMLP_CH, F32, 1.0)))
        conv_out, hc = _conv_fwd(h_a, full['conv_dw_w'][l], conv_dw_b[l], conv_ln_g[l], conv_ln_b[l],
                                 name="conv_fwd", ts=ts)
        qkv_d = _dilate_qkv(qkv, name="dilate_qkv", ts=ts)
        o_ps, lse_ps = [], []
        for p, d in enumerate(dils):
            o_p, lse_p = _attn_fwd(qkv_d[p], biases[p], name=f"attn_fwd_d{d}", nb=nbs[p], G=G)
            o_ps.append(o_p)
            lse_ps.append(lse_p)
        attn_out, lse = _attn_merge(o_ps, lse_ps, dils, name="attn_merge", ts=ts)
        w_tril = jnp.tril(gmlp_w_s[l]).astype(BF16)
        bs_rows = jnp.repeat(gmlp_b_s[l].T, GMLP_GROUP_DIM, axis=1)
        gm_out = _gmlp_fwd(h_c, gmlp_ln_g[l], gmlp_ln_b[l], w_tril, bs_rows, name="gmlp_fwd", ts=ts)
        x1, xhat1, rstd1 = _mm([conv_out, attn_out, gm_out],
                               [Wout[:CONV_CH], Wout[CONV_CH:CONV_CH + ATTN_CH], Wout[CONV_CH + ATTN_CH:]],
                               bias=b_out[l], resid=cur, resid_scale=ALPHA, ln=(ln1_g[l], ln1_b[l]), tm=ts,
                               name="out_proj_ln")
        hu = _mm([x1], [Wup], bias=ffn_b_up[l], tm=ts, tn=F, name="ffn_up")
        act = _ffn_act_fwd(hu, full['ffn_conv_w'][l], ffn_conv_b[l], name="ffn_act_fwd", ts=min(256, S), tc=tc)
        x2, xhat2, rstd2 = _mm([act], [Wdown], bias=ffn_b_down[l], resid=x1, resid_scale=ALPHA,
                               ln=(ln2_g[l], ln2_b[l]), tm=ts, name="ffn_down_ln")
        saved.append(dict(x0=cur, h_a=h_a, h_c=h_c, qkv_d=qkv_d, hc=hc, conv_out=conv_out, attn_out=attn_out,
                          lse=lse, gm_out=gm_out, w_tril=w_tril, bs_rows=bs_rows, x1=x1, xhat1=xhat1, rstd1=rstd1,
                          hu=hu, act=act, xhat2=xhat2, rstd2=rstd2))
        cur = x2

    grads = {n: [None] * DEPTH for n in WEIGHTS if n != 'rel_bias_table'}
    drel = None
    dx = None
    loss_part = None
    tk = min(512, S)
    for l in reversed(range(DEPTH)):
        sv = saved[l]
        Win, Wout, Wup, Wdown = full['w_in'][l], full['w_out'][l], full['ffn_w_up'][l], full['ffn_w_down'][l]
        if dx is None:
            dz2, dg2, db2, dzs2, loss_part = _ln_bwd(sv['xhat2'], sv['rstd2'], ln2_g[l], b=ln2_b[l], target=target,
                                                     name="ln2_bwd_loss", ts=ts)
        else:
            dz2, dg2, db2, dzs2 = _ln_bwd(sv['xhat2'], sv['rstd2'], ln2_g[l], dy=dx, name="ln_bwd", ts=ts)
        grads['ln2_g'][l], grads['ln2_b'][l], grads['ffn_b_down'][l] = dg2[0], db2[0], dzs2[0]
        grads['ffn_w_down'][l] = _mm_tn(sv['act'], dz2, tm=F // 2 if (F // 2) % LANES == 0 else F, tn=D, tk=tk,
                                        name="dw_down")
        dact = _mm([dz2], [Wdown.T], tm=ts, name="dact")
        dhg, dhv, dwg, dwv, dbg, dbv, dug, duv = _ffn_act_bwd(sv['hu'], dact, full['ffn_conv_w'][l], ffn_conv_b[l],
                                                              name="ffn_act_bwd", ts=min(256, S), tc=tc)
        grads['ffn_conv_w'][l] = jnp.concatenate([dwg, dwv], axis=1)
        grads['ffn_conv_b'][l] = jnp.concatenate([dbg, dbv], axis=1)[0]
        grads['ffn_b_up'][l] = jnp.concatenate([dug, duv], axis=1)[0]
        grads['ffn_w_up'][l] = jnp.concatenate(
            [_mm_tn(sv['x1'], dhg, tm=D, tn=tc, tk=tk, name="dw_up"),
             _mm_tn(sv['x1'], dhv, tm=D, tn=tc, tk=tk, name="dw_up")], axis=1)
        WupT = Wup.T
        dx1 = _mm([dhg, dhv], [WupT[:F], WupT[F:]], resid=dz2, resid_scale=ALPHA, tm=ts, name="dx1")
        dz1, dg1, db1, dzs1 = _ln_bwd(sv['xhat1'], sv['rstd1'], ln1_g[l], dy=dx1, name="ln_bwd", ts=ts)
        grads['ln1_g'][l], grads['ln1_b'][l], grads['b_out'][l] = dg1[0], db1[0], dzs1[0]
        grads['w_out'][l] = jnp.concatenate(
            [_mm_tn(sv['conv_out'], dz1, tm=CONV_CH, tn=D, tk=tk, name="dw_out_conv"),
             _mm_tn(sv['attn_out'], dz1, tm=ATTN_CH, tn=D, tk=tk, name="dw_out_attn"),
             _mm_tn(sv['gm_out'], dz1, tm=GMLP_CH, tn=D, tk=tk, name="dw_out_conv")], axis=0)
        dc_conv, dc_attn, dc_gm = _mm([dz1], [Wout.T], tm=ts, name="dcat",
                                      splits=((CONV_CH, F32, 1.0), (ATTN_CH, F32, 1.0), (GMLP_CH, F32, 1.0)))
        dh_a, ddw, ddwb, dclg, dclb = _conv_bwd(sv['h_a'], sv['hc'], dc_conv, full['conv_dw_w'][l], conv_ln_g[l],
                                                conv_ln_b[l], name="conv_bwd", ts=ts)
        grads['conv_dw_w'][l], grads['conv_dw_b'][l] = ddw[:CONV_WIDTH], ddwb[0]
        grads['conv_ln_g'][l], grads['conv_ln_b'][l] = dclg[0], dclb[0]
        dd_d, do_d = _attn_prep(dc_attn, sv['attn_out'], dils, name="attn_prep", ts=ts)
        dqs, dks, dvs, dbs = [], [], [], []
        for p, d in enumerate(dils):
            dq, dk, dv, dbias = _attn_bwd(sv['qkv_d'][p], do_d[p], sv['lse'][p], dd_d[p], biases[p],
                                          name=f"attn_bwd_d{d}", nb=nbs[p], G=G)
            dqs.append(dq)
            dks.append(dk)
            dvs.append(dv)
            dbs.append(dbias.reshape(1, ATTN_HEADS, -1))
        dqkv = _attn_combine(dqs, dks, dvs, dils, name="attn_combine", ts=ts)
        dr = _bias_grad(jnp.concatenate(dbs, axis=0), bucket_flat, name="bias_grad")
        drel = dr if drel is None else drel + dr
        w_tril_t = jnp.swapaxes(sv['w_tril'], 1, 2)
        dh_c, dws, dbs_acc, dglg, dglb = _gmlp_bwd(sv['h_c'], dc_gm, gmlp_ln_g[l], gmlp_ln_b[l], sv['w_tril'],
                                                   w_tril_t, sv['bs_rows'], name="gmlp_bwd", ts=ts)
        grads['gmlp_w_s'][l] = jnp.tril(dws)
        grads['gmlp_b_s'][l] = dbs_acc[:, :GMLP_GROUPS].T
        grads['gmlp_ln_g'][l], grads['gmlp_ln_b'][l] = dglg[0], dglb[0]
        dwa, ca = _mm_tn(sv['x0'], dh_a, tm=D, tn=2 * CONV_CH, tk=tk, colsum=True, name="dw_in_side")
        dwq = [_mm_tn(sv['x0'], part, tm=D, tn=ATTN_CH, tk=tk, colsum=True, name="dw_in_side") for part in dqkv]
        dwc, cc = _mm_tn(sv['x0'], dh_c, tm=D, tn=2 * GMLP_CH, tk=tk, colsum=True, name="dw_in_side")
        grads['w_in'][l] = jnp.concatenate([dwa] + [w for w, _ in dwq] + [dwc], axis=1)
        grads['b_in'][l] = jnp.concatenate([ca] + [c for _, c in dwq] + [cc], axis=1)[0]
        WinT = Win.T
        edges = [0, 2 * CONV_CH] + [2 * CONV_CH + k * ATTN_CH for k in (1, 2, 3)] + [WinT.shape[0]]
        dx = _mm([dh_a, *dqkv, dh_c], [WinT[a:b] for a, b in zip(edges[:-1], edges[1:])], resid=dz1,
                 resid_scale=ALPHA, tm=ts, name="dx0")

    gfull = {n: jnp.stack(v) for n, v in grads.items()}
    gfull['rel_bias_table'] = drel

    pieces, goffs, r = [], [], 0
    for n in SHARDED:
        parts = _split_axis(gfull[n], SHARD_AXIS[n]).reshape(N_DEV, -1)
        cnt = parts.shape[1]
        rows = _round_up(cnt, PACK_W) // PACK_W
        if rows * PACK_W != cnt:
            parts = jnp.pad(parts, ((0, 0), (0, rows * PACK_W - cnt)))
        pieces.append(parts.reshape(N_DEV, rows, PACK_W))
        goffs.append((r, rows, cnt))
        r += rows
    tr = 512 if r >= 512 else _round_up(r, 16)
    total = _round_up(r, tr)
    if total != r:
        pieces.append(jnp.zeros((N_DEV, total - r, PACK_W), F32))
    send = jnp.concatenate(pieces, axis=1).astype(BF16)
    recv = _exchange(send, name="grad_exchange")
    packs = [_pack([src[n] for n in SHARDED], PACK_W, tr)[0] for src in (W, Mo, Vo)]
    outs = _adamw(packs[0], packs[1], packs[2], recv, name="adamw_sharded", tr=tr)
    shard_out = [_unpack(o, goffs, [W[n].shape for n in SHARDED]) for o in outs]

    gsmall, soffs = _pack([gfull[n] for n in SMALL], LANES, 8)
    gall = _allgather(gsmall, name="small_grad_allgather")
    spacks = [_pack([src[n] for n in SMALL], LANES, 8)[0] for src in (W, Mo, Vo)]
    souts = _adamw(spacks[0], spacks[1], spacks[2], gall, name="adamw_small", tr=gsmall.shape[0])
    small_out = [_unpack(o, soffs, [W[n].shape for n in SMALL]) for o in souts]

    loss = lax.psum(loss_part[0, 0], ("x", "y", "c"))
    by_kind = []
    for kind in range(4):
        d = dict(zip(SHARDED, shard_out[kind]))
        d.update(zip(SMALL, small_out[kind]))
        by_kind.append([d[n] for n in WEIGHTS])
    return (loss, dx[None], *by_kind[0], *by_kind[1], *by_kind[2], *by_kind[3])
```

```python
import math

import numpy as np
import jax
import jax.numpy as jnp
from jax import lax
from jax.experimental import pallas as pl
from jax.experimental.pallas import tpu as pltpu

F32 = jnp.float32
BF16 = jnp.bfloat16

DEPTH = 2
HEAD_DIM = 64
CONV_CH = 256
CONV_WIDTH = 31
ATTN_HEADS = 8
ATTN_CH = ATTN_HEADS * HEAD_DIM
PATTERNS = ((128, 1), (512, 4), (2048, 16))
ATTN_BLOCK = 128
N_BUCKETS = 32
MAX_DISTANCE = 2048
GMLP_CH = 256
GMLP_GROUPS = 4
GMLP_GROUP_DIM = GMLP_CH // GMLP_GROUPS
CHUNK = 128
FFN_CONV_WIDTH = 3
LN_EPS = 1e-5
ALPHA = (2.0 * DEPTH) ** 0.25
ADAM_LR = 0.001
ADAM_B1 = 0.9
ADAM_B2 = 0.999
ADAM_EPS = 1e-08
ADAM_WD = 0.01
ADAM_STEP = 10
NEG = -1e30
N_DEV = 8
LANES = 128
CONV_HALO = 32
FFN_HALO = 8
FFN_ROWS = 16
MESH = pl.DeviceIdType.MESH

WEIGHTS = ['w_in', 'b_in', 'conv_dw_w', 'conv_dw_b', 'conv_ln_g', 'conv_ln_b', 'rel_bias_table', 'gmlp_ln_g',
           'gmlp_ln_b', 'gmlp_w_s', 'gmlp_b_s', 'w_out', 'b_out', 'ln1_g', 'ln1_b', 'ffn_w_up', 'ffn_b_up',
           'ffn_conv_w', 'ffn_conv_b', 'ffn_w_down', 'ffn_b_down', 'ln2_g', 'ln2_b']
SHARDED = ['w_in', 'w_out', 'ffn_w_up', 'ffn_w_down', 'conv_dw_w', 'ffn_conv_w']
SHARD_AXIS = {'w_in': 2, 'w_out': 1, 'ffn_w_up': 2, 'ffn_w_down': 1, 'conv_dw_w': 2, 'ffn_conv_w': 2}
SMALL = [n for n in WEIGHTS if n not in SHARDED]


def _call(body, *, grid=(), vmem_mb=48, **kw):
    params = pltpu.CompilerParams(dimension_semantics=("arbitrary",) * len(grid), vmem_limit_bytes=vmem_mb << 20)
    return pl.pallas_call(body, grid=grid, compiler_params=params, **kw)


def _sds(shape, dtype):
    return jax.ShapeDtypeStruct(shape, dtype)


def _ln_rows(z):
    mu = jnp.mean(z, axis=-1, keepdims=True)
    zc = z - mu
    var = jnp.mean(zc * zc, axis=-1, keepdims=True)
    rstd = lax.rsqrt(var + LN_EPS)
    return zc * rstd, rstd


def _ln_bwd_rows(dxhat, xhat, rstd):
    m1 = jnp.mean(dxhat, axis=-1, keepdims=True)
    m2 = jnp.mean(dxhat * xhat, axis=-1, keepdims=True)
    return rstd * (dxhat - m1 - xhat * m2)


def _colsum(v):
    return jnp.sum(v, axis=0, keepdims=True)


def _mm(a_list, w_list, *, name, tm, tn=None, bias=None, resid=None, resid_scale=1.0, ln=None, splits=None,
        out_dtype=F32):
    na = len(a_list)
    M = a_list[0].shape[0]
    N = w_list[0].shape[1]
    tn = N if tn is None else tn
    assert M % tm == 0 and N % tn == 0
    assert ln is None or tn == N
    assert splits is None or tn == N

    def body(*refs):
        a_refs, w_refs = refs[:na], refs[na:2 * na]
        pos = 2 * na
        acc = None
        for a_ref, w_ref in zip(a_refs, w_refs):
            t = jnp.dot(a_ref[...].astype(BF16), w_ref[...], preferred_element_type=F32)
            acc = t if acc is None else acc + t
        if bias is not None:
            acc = acc + refs[pos][...]
            pos += 1
        if resid is not None:
            acc = resid_scale * refs[pos][...] + acc
            pos += 1
        if ln is not None:
            g_ref, b_ref = refs[pos], refs[pos + 1]
            y_ref, xhat_ref, rstd_ref = refs[pos + 2], refs[pos + 3], refs[pos + 4]
            xhat, rstd = _ln_rows(acc)
            y_ref[...] = xhat * g_ref[...] + b_ref[...]
            xhat_ref[...] = xhat
            rstd_ref[...] = rstd
        elif splits is not None:
            c0 = 0
            for o_ref, (width, dtype, scale) in zip(refs[pos:], splits):
                part = acc[:, c0:c0 + width]
                if scale != 1.0:
                    part = part * scale
                o_ref[...] = part.astype(dtype)
                c0 += width
        else:
            refs[pos][...] = acc.astype(out_dtype)

    in_specs = [pl.BlockSpec((tm, a.shape[1]), lambda j, i: (i, 0)) for a in a_list]
    in_specs += [pl.BlockSpec((w.shape[0], tn), lambda j, i: (0, j)) for w in w_list]
    args = list(a_list) + list(w_list)
    if bias is not None:
        in_specs.append(pl.BlockSpec((1, tn), lambda j, i: (0, j)))
        args.append(bias.reshape(1, N))
    if resid is not None:
        in_specs.append(pl.BlockSpec((tm, tn), lambda j, i: (i, j)))
        args.append(resid)
    if ln is not None:
        in_specs += [pl.BlockSpec((1, N), lambda j, i: (0, 0))] * 2
        args += [ln[0].reshape(1, N), ln[1].reshape(1, N)]
        out_shape = (_sds((M, N), F32), _sds((M, N), F32), _sds((M, 1), F32))
        out_specs = (pl.BlockSpec((tm, N), lambda j, i: (i, 0)), pl.BlockSpec((tm, N), lambda j, i: (i, 0)),
                     pl.BlockSpec((tm, 1), lambda j, i: (i, 0)))
    elif splits is not None:
        out_shape = tuple(_sds((M, w), d) for (w, d, _) in splits)
        out_specs = tuple(pl.BlockSpec((tm, w), lambda j, i: (i, 0)) for (w, _, _) in splits)
    else:
        out_shape = _sds((M, N), out_dtype)
        out_specs = pl.BlockSpec((tm, tn), lambda j, i: (i, j))
    return _call(body, grid=(N // tn, M // tm), in_specs=in_specs, out_specs=out_specs, out_shape=out_shape,
                 name=name, vmem_mb=56)(*args)


def _mm_tn(a, dy, *, name, tm, tn, tk, colsum=False):
    S, Ka = a.shape
    N = dy.shape[1]
    assert S % tk == 0 and Ka % tm == 0 and N % tn == 0

    def body(a_ref, dy_ref, out_ref, *cs):
        i, k = pl.program_id(1), pl.program_id(2)
        dyb = dy_ref[...]
        part = lax.dot_general(a_ref[...].astype(BF16), dyb.astype(BF16), (((0,), (0,)), ((), ())),
                               preferred_element_type=F32)

        @pl.when(k == 0)
        def _():
            out_ref[...] = part

        @pl.when(k > 0)
        def _():
            out_ref[...] += part

        if colsum:
            cs_ref = cs[0]
            s = _colsum(dyb.astype(F32))

            @pl.when((i == 0) & (k == 0))
            def _():
                cs_ref[...] = s

            @pl.when((i == 0) & (k > 0))
            def _():
                cs_ref[...] += s

    out_shape = [_sds((Ka, N), F32)]
    out_specs = [pl.BlockSpec((tm, tn), lambda j, i, k: (i, j))]
    if colsum:
        out_shape.append(_sds((1, N), F32))
        out_specs.append(pl.BlockSpec((1, tn), lambda j, i, k: (0, j)))
    res = _call(body, grid=(N // tn, Ka // tm, S // tk),
                in_specs=[pl.BlockSpec((tk, tm), lambda j, i, k: (k, i)), pl.BlockSpec((tk, tn), lambda j, i, k: (k, j))],
                out_specs=tuple(out_specs), out_shape=tuple(out_shape), name=name, vmem_mb=56)(a, dy)
    return res if colsum else res[0]


def _ln_bwd(xhat, rstd, g, *, name, ts, dy=None, b=None, target=None):
    S, D = xhat.shape
    from_loss = target is not None

    def body(*refs):
        if from_loss:
            xhat_ref, rstd_ref, g_ref, b_ref, t_ref, dz_ref, dg_ref, db_ref, dzs_ref, loss_ref = refs
        else:
            xhat_ref, rstd_ref, g_ref, dy_ref, dz_ref, dg_ref, db_ref, dzs_ref = refs
        i = pl.program_id(0)
        xh = xhat_ref[...]
        gg = g_ref[...]
        if from_loss:
            err = xh * gg + b_ref[...] - t_ref[...]
            dyv = err * (1.0 / D)
            lsum = (0.5 / D) * jnp.sum(err * err, axis=(0, 1), keepdims=True)
        else:
            dyv = dy_ref[...]
        dz = _ln_bwd_rows(dyv * gg, xh, rstd_ref[...])
        dz_ref[...] = dz
        parts = [(dg_ref, _colsum(dyv * xh)), (db_ref, _colsum(dyv)), (dzs_ref, _colsum(dz))]
        if from_loss:
            parts.append((loss_ref, lsum))

        @pl.when(i == 0)
        def _():
            for r, v in parts:
                r[...] = v

        @pl.when(i > 0)
        def _():
            for r, v in parts:
                r[...] += v

    row = pl.BlockSpec((ts, D), lambda i: (i, 0))
    vec = pl.BlockSpec((1, D), lambda i: (0, 0))
    in_specs = [row, pl.BlockSpec((ts, 1), lambda i: (i, 0)), vec]
    args = [xhat, rstd, g.reshape(1, D)]
    if from_loss:
        in_specs += [vec, row]
        args += [b.reshape(1, D), target]
    else:
        in_specs += [row]
        args += [dy]
    out_shape = [_sds((S, D), F32), _sds((1, D), F32), _sds((1, D), F32), _sds((1, D), F32)]
    out_specs = [row, vec, vec, vec]
    if from_loss:
        out_shape.append(_sds((1, 1), F32))
        out_specs.append(pl.BlockSpec((1, 1), lambda i: (0, 0)))
    return _call(body, grid=(S // ts,), in_specs=in_specs, out_specs=tuple(out_specs), out_shape=tuple(out_shape),
                 name=name)(*args)


def _glu(v):
    return v[:, :CONV_CH] * jax.nn.sigmoid(v[:, CONV_CH:])


def _conv_fwd(h_a, dw_w, dw_b, ln_g, ln_b, *, name, ts):
    S = h_a.shape[0]
    C, K, HB = CONV_CH, CONV_WIDTH, CONV_HALO
    RC = 128

    def body(h_ref, halo_ref, w_ref, b_ref, g_ref, bb_ref, out_ref, hc_ref, gbuf):
        i = pl.program_id(0)
        gbuf[0:HB, :] = jnp.where(i > 0, _glu(halo_ref[...]), 0.0)
        gbuf[HB:HB + ts, :] = _glu(h_ref[...])
        for r0 in range(0, ts, RC):
            acc = jnp.zeros((RC, C), F32) + b_ref[...]
            for k in range(K):
                acc = acc + w_ref[k:k + 1, :] * gbuf[pl.ds(r0 + HB - (K - 1) + k, RC), :]
            hc_ref[r0:r0 + RC, :] = acc
            xhat, _ = _ln_rows(acc)
            hn = xhat * g_ref[...] + bb_ref[...]
            out_ref[r0:r0 + RC, :] = (hn * jax.nn.sigmoid(hn)).astype(BF16)

    nb = ts // HB
    vec = pl.BlockSpec((1, C), lambda i: (0, 0))
    return _call(body, grid=(S // ts,),
                 in_specs=[pl.BlockSpec((ts, 2 * C), lambda i: (i, 0)),
                           pl.BlockSpec((HB, 2 * C), lambda i: (jnp.maximum(i * nb - 1, 0), 0)),
                           pl.BlockSpec((K, C), lambda i: (0, 0)), vec, vec, vec],
                 out_specs=(pl.BlockSpec((ts, C), lambda i: (i, 0)), pl.BlockSpec((ts, C), lambda i: (i, 0))),
                 out_shape=(_sds((S, C), BF16), _sds((S, C), F32)),
                 scratch_shapes=[pltpu.VMEM((HB + ts, C), F32)], name=name)(
        h_a, h_a, dw_w, dw_b.reshape(1, C), ln_g.reshape(1, C), ln_b.reshape(1, C))


def _conv_bwd(h_a, hc, dout, dw_w, ln_g, ln_b, *, name, ts):
    S = h_a.shape[0]
    C, K, HB = CONV_CH, CONV_WIDTH, CONV_HALO
    RC = 128
    n = S // ts

    def dconv_out(hc_v, do_v, g_ref, bb_ref):
        xhat, rstd = _ln_rows(hc_v)
        hn = xhat * g_ref[...] + bb_ref[...]
        sg = jax.nn.sigmoid(hn)
        dhn = do_v * (sg * (1.0 + hn * (1.0 - sg)))
        return _ln_bwd_rows(dhn * g_ref[...], xhat, rstd), dhn, xhat

    def body(h_ref, hprev_ref, hc_ref, hcnext_ref, do_ref, donext_ref, w_ref, g_ref, bb_ref,
             dh_ref, dw_ref, dwb_ref, dg_ref, db_ref, gbuf, dbuf):
        i = pl.program_id(0)
        hv = h_ref[...]
        gbuf[0:HB, :] = jnp.where(i > 0, _glu(hprev_ref[...]), 0.0)
        gbuf[HB:HB + ts, :] = _glu(hv)
        dhc, dhn, xhat = dconv_out(hc_ref[...], do_ref[...], g_ref, bb_ref)
        dhc_next, _, _ = dconv_out(hcnext_ref[...], donext_ref[...], g_ref, bb_ref)
        dbuf[0:ts, :] = dhc
        dbuf[ts:ts + HB, :] = jnp.where(i < n - 1, dhc_next, 0.0)
        dw_rows = []
        for k in range(K):
            acc_k = jnp.zeros((1, C), F32)
            for r0 in range(0, ts, RC):
                acc_k = acc_k + _colsum(dbuf[r0:r0 + RC, :] * gbuf[pl.ds(r0 + HB - (K - 1) + k, RC), :])
            dw_rows.append(acc_k)
        dw_rows.append(jnp.zeros((1, C), F32))
        dw_tile = jnp.concatenate(dw_rows, axis=0)
        for r0 in range(0, ts, RC):
            acc = jnp.zeros((RC, C), F32)
            for k in range(K):
                acc = acc + w_ref[k:k + 1, :] * dbuf[pl.ds(r0 + (K - 1) - k, RC), :]
            a = hv[r0:r0 + RC, :C]
            sg = jax.nn.sigmoid(hv[r0:r0 + RC, C:])
            dh_ref[r0:r0 + RC, :C] = (acc * sg).astype(BF16)
            dh_ref[r0:r0 + RC, C:] = (acc * a * sg * (1.0 - sg)).astype(BF16)
        parts = [(dw_ref, dw_tile), (dwb_ref, _colsum(dhc)), (dg_ref, _colsum(dhn * xhat)), (db_ref, _colsum(dhn))]

        @pl.when(i == 0)
        def _():
            for r, v in parts:
                r[...] = v

        @pl.when(i > 0)
        def _():
            for r, v in parts:
                r[...] += v

    nb = ts // HB
    last = S // HB - 1
    vec = pl.BlockSpec((1, C), lambda i: (0, 0))
    nxt = lambda i: (jnp.minimum((i + 1) * nb, last), 0)
    return _call(body, grid=(n,),
                 in_specs=[pl.BlockSpec((ts, 2 * C), lambda i: (i, 0)),
                           pl.BlockSpec((HB, 2 * C), lambda i: (jnp.maximum(i * nb - 1, 0), 0)),
                           pl.BlockSpec((ts, C), lambda i: (i, 0)), pl.BlockSpec((HB, C), nxt),
                           pl.BlockSpec((ts, C), lambda i: (i, 0)), pl.BlockSpec((HB, C), nxt),
                           pl.BlockSpec((K, C), lambda i: (0, 0)), vec, vec],
                 out_specs=(pl.BlockSpec((ts, 2 * C), lambda i: (i, 0)), pl.BlockSpec((K + 1, C), lambda i: (0, 0)),
                            vec, vec, vec),
                 out_shape=(_sds((S, 2 * C), BF16), _sds((K + 1, C), F32), _sds((1, C), F32), _sds((1, C), F32),
                            _sds((1, C), F32)),
                 scratch_shapes=[pltpu.VMEM((HB + ts, C), F32), pltpu.VMEM((ts + HB, C), F32)], name=name)(
        h_a, h_a, hc, hc, dout, dout, dw_w, ln_g.reshape(1, C), ln_b.reshape(1, C))


def _gmlp_mix(vn_bf, w_ref, mix_buf, ts):
    for ch in range(ts // CHUNK):
        for g in range(GMLP_GROUPS):
            vg = vn_bf[ch * CHUNK:(ch + 1) * CHUNK, g * GMLP_GROUP_DIM:(g + 1) * GMLP_GROUP_DIM]
            mix_buf[ch * CHUNK:(ch + 1) * CHUNK, g * GMLP_GROUP_DIM:(g + 1) * GMLP_GROUP_DIM] = jnp.dot(
                w_ref[g], vg, preferred_element_type=F32)


def _gmlp_fwd(h_c, ln_g, ln_b, w_tril, bs_rows, *, name, ts):
    S = h_c.shape[0]
    C = GMLP_CH

    def body(h_ref, g_ref, b_ref, w_ref, bs_ref, out_ref, mix_buf):
        hv = h_ref[...]
        xhat, _ = _ln_rows(hv[:, C:])
        vn = (xhat * g_ref[...] + b_ref[...]).astype(BF16)
        _gmlp_mix(vn, w_ref, mix_buf, ts)
        for ch in range(ts // CHUNK):
            rows = slice(ch * CHUNK, (ch + 1) * CHUNK)
            out_ref[rows, :] = (hv[rows, :C] * (mix_buf[rows, :] + bs_ref[...])).astype(BF16)

    vec = pl.BlockSpec((1, C), lambda i: (0, 0))
    return _call(body, grid=(S // ts,),
                 in_specs=[pl.BlockSpec((ts, 2 * C), lambda i: (i, 0)), vec, vec,
                           pl.BlockSpec((GMLP_GROUPS, CHUNK, CHUNK), lambda i: (0, 0, 0)),
                           pl.BlockSpec((CHUNK, C), lambda i: (0, 0))],
                 out_specs=pl.BlockSpec((ts, C), lambda i: (i, 0)), out_shape=_sds((S, C), BF16),
                 scratch_shapes=[pltpu.VMEM((ts, C), F32)], name=name)(
        h_c, ln_g.reshape(1, C), ln_b.reshape(1, C), w_tril, bs_rows)


def _gmlp_bwd(h_c, dout, ln_g, ln_b, w_tril, w_tril_t, bs_rows, *, name, ts):
    S = h_c.shape[0]
    C, G, GD = GMLP_CH, GMLP_GROUPS, GMLP_GROUP_DIM

    def body(h_ref, do_ref, g_ref, b_ref, w_ref, wt_ref, bs_ref, dh_ref, dw_ref, dbs_ref, dg_ref, db_ref,
             mix_buf, dvn_buf):
        i = pl.program_id(0)
        hv = h_ref[...]
        u = hv[:, :C]
        xhat, rstd = _ln_rows(hv[:, C:])
        vn = (xhat * g_ref[...] + b_ref[...]).astype(BF16)
        _gmlp_mix(vn, w_ref, mix_buf, ts)
        do = do_ref[...]
        dmixed = do * u
        dm_bf = dmixed.astype(BF16)
        lane = lax.broadcasted_iota(jnp.int32, (CHUNK, LANES), 1)
        dbs = jnp.zeros((CHUNK, LANES), F32)
        dws = [jnp.zeros((CHUNK, CHUNK), F32) for _ in range(G)]
        for ch in range(ts // CHUNK):
            rows = slice(ch * CHUNK, (ch + 1) * CHUNK)
            dh_ref[rows, :C] = (do[rows, :] * (mix_buf[rows, :] + bs_ref[...])).astype(BF16)
            for g in range(G):
                cols = slice(g * GD, (g + 1) * GD)
                dmg = dm_bf[rows, cols]
                dvn_buf[rows, cols] = jnp.dot(wt_ref[g], dmg, preferred_element_type=F32)
                dws[g] = dws[g] + lax.dot_general(dmg, vn[rows, cols], (((1,), (1,)), ((), ())),
                                                  preferred_element_type=F32)
                rs = jnp.sum(dmixed[rows, cols], axis=1, keepdims=True)
                dbs = dbs + jnp.where(lane == g, rs, 0.0)
        dvn = dvn_buf[...]
        dh_ref[:, C:] = _ln_bwd_rows(dvn * g_ref[...], xhat, rstd).astype(BF16)
        dgv, dbv = _colsum(dvn * xhat), _colsum(dvn)

        @pl.when(i == 0)
        def _():
            for g in range(G):
                dw_ref[g] = dws[g]
            dbs_ref[...] = dbs
            dg_ref[...] = dgv
            db_ref[...] = dbv

        @pl.when(i > 0)
        def _():
            for g in range(G):
                dw_ref[g] += dws[g]
            dbs_ref[...] += dbs
            dg_ref[...] += dgv
            db_ref[...] += dbv

    vec = pl.BlockSpec((1, C), lambda i: (0, 0))
    wspec = pl.BlockSpec((G, CHUNK, CHUNK), lambda i: (0, 0, 0))
    return _call(body, grid=(S // ts,),
                 in_specs=[pl.BlockSpec((ts, 2 * C), lambda i: (i, 0)),
                           pl.BlockSpec((ts, C), lambda i: (i, 0)), vec, vec, wspec, wspec,
                           pl.BlockSpec((CHUNK, C), lambda i: (0, 0))],
                 out_specs=(pl.BlockSpec((ts, 2 * C), lambda i: (i, 0)), wspec,
                            pl.BlockSpec((CHUNK, LANES), lambda i: (0, 0)), vec, vec),
                 out_shape=(_sds((S, 2 * C), BF16), _sds((G, CHUNK, CHUNK), F32), _sds((CHUNK, LANES), F32),
                            _sds((1, C), F32), _sds((1, C), F32)),
                 scratch_shapes=[pltpu.VMEM((ts, C), F32), pltpu.VMEM((ts, C), F32)], name=name)(
        h_c, dout, ln_g.reshape(1, C), ln_b.reshape(1, C), w_tril, w_tril_t, bs_rows)


def _conv3(buf, w_ref, b_ref, rows, off):
    return (w_ref[0:1, :] * buf[pl.ds(off - 2, rows), :] + w_ref[1:2, :] * buf[pl.ds(off - 1, rows), :]
            + w_ref[2:3, :] * buf[pl.ds(off, rows), :]) + b_ref[...]


def _ffn_act_fwd(hu, cw, cb, *, name, ts, tc):
    S, F2 = hu.shape
    F = F2 // 2
    nj = F // tc
    HB = FFN_HALO
    nb = ts // HB

    def body(g_ref, v_ref, gh_ref, vh_ref, wg_ref, wv_ref, bg_ref, bv_ref, act_ref, gbuf, vbuf):
        i = pl.program_id(0)
        gbuf[0:HB, :] = jnp.where(i > 0, gh_ref[...], 0.0)
        gbuf[HB:HB + ts, :] = g_ref[...]
        vbuf[0:HB, :] = jnp.where(i > 0, vh_ref[...], 0.0)
        vbuf[HB:HB + ts, :] = v_ref[...]
        for r0 in range(0, ts, FFN_ROWS):
            gc = _conv3(gbuf, wg_ref, bg_ref, FFN_ROWS, HB + r0)
            vc = _conv3(vbuf, wv_ref, bv_ref, FFN_ROWS, HB + r0)
            act_ref[r0:r0 + FFN_ROWS, :] = (gc * jax.nn.sigmoid(gc) * vc).astype(BF16)

    prev = lambda off: (lambda i, j: (jnp.maximum(i * nb - 1, 0), j + off))
    return _call(body, grid=(S // ts, nj),
                 in_specs=[pl.BlockSpec((ts, tc), lambda i, j: (i, j)), pl.BlockSpec((ts, tc), lambda i, j: (i, j + nj)),
                           pl.BlockSpec((HB, tc), prev(0)), pl.BlockSpec((HB, tc), prev(nj)),
                           pl.BlockSpec((3, tc), lambda i, j: (0, j)), pl.BlockSpec((3, tc), lambda i, j: (0, j + nj)),
                           pl.BlockSpec((1, tc), lambda i, j: (0, j)), pl.BlockSpec((1, tc), lambda i, j: (0, j + nj))],
                 out_specs=pl.BlockSpec((ts, tc), lambda i, j: (i, j)), out_shape=_sds((S, F), BF16),
                 scratch_shapes=[pltpu.VMEM((HB + ts, tc), F32), pltpu.VMEM((HB + ts, tc), F32)], name=name)(
        hu, hu, hu, hu, cw, cw, cb.reshape(1, F2), cb.reshape(1, F2))


def _ffn_act_bwd(hu, dact, cw, cb, *, name, ts, tc):
    S, F2 = hu.shape
    F = F2 // 2
    nj = F // tc
    HB = FFN_HALO
    nb = ts // HB
    n = S // ts
    last = S // HB - 1

    def body(g_ref, v_ref, gp_ref, vp_ref, gn_ref, vn_ref, da_ref, dan_ref, wg_ref, wv_ref, bg_ref, bv_ref,
             dhg_ref, dhv_ref, dwg_ref, dwv_ref, dbg_ref, dbv_ref, dug_ref, duv_ref, gbuf, vbuf, dgb, dvb, accg, accv):
        i = pl.program_id(1)
        for buf, p_ref, t_ref, n_ref in ((gbuf, gp_ref, g_ref, gn_ref), (vbuf, vp_ref, v_ref, vn_ref)):
            buf[0:HB, :] = jnp.where(i > 0, p_ref[...], 0.0)
            buf[HB:HB + ts, :] = t_ref[...]
            buf[HB + ts:HB + ts + HB, :] = n_ref[...]
        RC = FFN_ROWS
        for r0 in list(range(0, ts, RC)) + [ts]:
            rows = RC if r0 < ts else HB
            gc = _conv3(gbuf, wg_ref, bg_ref, rows, HB + r0)
            vc = _conv3(vbuf, wv_ref, bv_ref, rows, HB + r0)
            da = da_ref[r0:r0 + rows, :] if r0 < ts else jnp.where(i < n - 1, dan_ref[...], 0.0)
            sg = jax.nn.sigmoid(gc)
            dvb[r0:r0 + rows, :] = da * (gc * sg)
            dgb[r0:r0 + rows, :] = da * vc * (sg * (1.0 + gc * (1.0 - sg)))

        @pl.when(i == 0)
        def _():
            accg[...] = jnp.zeros_like(accg)
            accv[...] = jnp.zeros_like(accv)

        def fold(v):
            out = v[0:8, :]
            for r in range(8, RC, 8):
                out = out + v[r:r + 8, :]
            return out

        for dbuf, buf, w_ref, dh_ref, acc in ((dgb, gbuf, wg_ref, dhg_ref, accg), (dvb, vbuf, wv_ref, dhv_ref, accv)):
            for r0 in range(0, ts, RC):
                taps = [dbuf[r0 + 2 - k:r0 + 2 - k + RC, :] for k in range(3)]
                dhu = w_ref[2:3, :] * taps[2] + w_ref[1:2, :] * taps[1] + w_ref[0:1, :] * taps[0]
                dh_ref[r0:r0 + RC, :] = dhu.astype(BF16)
                h = buf[HB + r0:HB + r0 + RC, :]
                for k in range(3):
                    acc[8 * k:8 * k + 8, :] += fold(h * taps[k])
                acc[24:32, :] += fold(taps[2])
                acc[32:40, :] += fold(dhu)

        @pl.when(i == n - 1)
        def _():
            for acc, dw_ref, db_ref, du_ref in ((accg, dwg_ref, dbg_ref, dug_ref), (accv, dwv_ref, dbv_ref, duv_ref)):
                for k in range(3):
                    dw_ref[k:k + 1, :] = _colsum(acc[8 * k:8 * k + 8, :])
                db_ref[...] = _colsum(acc[24:32, :])
                du_ref[...] = _colsum(acc[32:40, :])

    prev = lambda off: (lambda j, i: (jnp.maximum(i * nb - 1, 0), j + off))
    nxt = lambda off: (lambda j, i: (jnp.minimum((i + 1) * nb, last), j + off))
    tile = lambda off: (lambda j, i: (i, j + off))
    vec = lambda rows: pl.BlockSpec((rows, tc), lambda j, i: (0, j))
    F2s = cb.reshape(1, F2)
    return _call(body, grid=(nj, n),
                 in_specs=[pl.BlockSpec((ts, tc), tile(0)), pl.BlockSpec((ts, tc), tile(nj)),
                           pl.BlockSpec((HB, tc), prev(0)), pl.BlockSpec((HB, tc), prev(nj)),
                           pl.BlockSpec((HB, tc), nxt(0)), pl.BlockSpec((HB, tc), nxt(nj)),
                           pl.BlockSpec((ts, tc), tile(0)), pl.BlockSpec((HB, tc), nxt(0)),
                           pl.BlockSpec((3, tc), lambda j, i: (0, j)), pl.BlockSpec((3, tc), lambda j, i: (0, j + nj)),
                           pl.BlockSpec((1, tc), lambda j, i: (0, j)), pl.BlockSpec((1, tc), lambda j, i: (0, j + nj))],
                 out_specs=(pl.BlockSpec((ts, tc), tile(0)), pl.BlockSpec((ts, tc), tile(0)),
                            vec(3), vec(3), vec(1), vec(1), vec(1), vec(1)),
                 out_shape=(_sds((S, F), BF16), _sds((S, F), BF16), _sds((3, F), F32), _sds((3, F), F32),
                            _sds((1, F), F32), _sds((1, F), F32), _sds((1, F), F32), _sds((1, F), F32)),
                 scratch_shapes=[pltpu.VMEM((ts + 2 * HB, tc), F32), pltpu.VMEM((ts + 2 * HB, tc), F32),
                                 pltpu.VMEM((ts + HB, tc), F32), pltpu.VMEM((ts + HB, tc), F32),
                                 pltpu.VMEM((40, tc), F32), pltpu.VMEM((40, tc), F32)], name=name)(
        hu, hu, hu, hu, hu, hu, dact, dact, cw, cw, F2s, F2s)


def _t5_bucket(dist):
    max_exact = N_BUCKETS // 2
    d = np.maximum(dist, 1).astype(np.float64)
    large = max_exact + (np.log(d / max_exact) / math.log(MAX_DISTANCE / max_exact)
                         * (N_BUCKETS - max_exact)).astype(np.int32)
    large = np.minimum(large, N_BUCKETS - 1)
    return np.where(dist < max_exact, dist, large).astype(np.int32)


def _pattern_tables(window, dilation):
    qi = np.arange(ATTN_BLOCK)[:, None]
    kj = np.arange(2 * ATTN_BLOCK)[None, :]
    dist = qi + ATTN_BLOCK - kj
    valid = (dist >= 0) & (dist <= window // dilation)
    bucket = _t5_bucket(np.clip(dist, 0, None) * dilation)
    return bucket, valid


def _dilate_qkv(qkv, *, name, ts):
    S, C = qkv.shape
    dils = [d for (_, d) in PATTERNS if d > 1]

    def body(x_ref, nat_ref, *rest):
        outs, tmp = rest[:-1], rest[-1]
        nat_ref[...] = x_ref[...].astype(BF16)
        for j in range(C // LANES):
            cols = slice(j * LANES, (j + 1) * LANES)
            tmp[...] = x_ref[:, cols]
            for d, o_ref in zip(dils, outs):
                _dilate(tmp, o_ref, cols, d, ts, BF16)

    out_shape = (_sds((S, C), BF16),) + tuple(_sds((d, S // d, C), BF16) for d in dils)
    out_specs = (_dil_spec(1, ts, C),) + tuple(_dil_spec(d, ts, C) for d in dils)
    res = _call(body, grid=(S // ts,), in_specs=[_dil_spec(1, ts, C)], out_specs=out_specs, out_shape=out_shape,
                scratch_shapes=[pltpu.VMEM((ts, LANES), F32)], name=name)(qkv)
    return [res[0]] + [r.reshape(S, C) for r in res[1:]]


def _attn_group(S):
    nb_min = (S // PATTERNS[-1][1]) // ATTN_BLOCK
    return math.gcd(8, nb_min)


def _attn_fwd(qkv, bias, *, name, nb, G):
    S = qkv.shape[0]
    B, HD = ATTN_BLOCK, HEAD_DIM
    GR = G * B
    ng = S // GR

    def body(q_ref, k_ref, kh_ref, v_ref, vh_ref, bias_ref, o_ref, lse_ref, kbuf, vbuf):
        g = pl.program_id(1)
        halo_ok = (g * G) % nb != 0
        kbuf[0:B, :] = kh_ref[...]
        kbuf[B:B + GR, :] = k_ref[...]
        vbuf[0:B, :] = vh_ref[...]
        vbuf[B:B + GR, :] = v_ref[...]
        col = lax.broadcasted_iota(jnp.int32, (B, 2 * B), 1)

        for bi in range(G):
            r0 = bi * B
            q2 = q_ref[r0:r0 + B, :]
            kk = kbuf[r0:r0 + 2 * B, :]
            vv = vbuf[r0:r0 + 2 * B, :]
            for hh in range(2):
                cs = slice(hh * HD, (hh + 1) * HD)
                s = lax.dot_general(q2[:, cs], kk[:, cs], (((1,), (1,)), ((), ())), preferred_element_type=F32)
                s = s + bias_ref[hh]
                if bi == 0:
                    s = jnp.where(jnp.logical_and(jnp.logical_not(halo_ok), col < B), NEG, s)
                m = jnp.max(s, axis=1, keepdims=True)
                p = jnp.exp(s - m)
                l = jnp.sum(p, axis=1, keepdims=True)
                o = jnp.dot(p.astype(BF16), vv[:, cs], preferred_element_type=F32) / l
                o_ref[r0:r0 + B, cs] = o
                lse_ref[r0:r0 + B, cs] = jnp.broadcast_to(m + jnp.log(l), (B, HD))

    halo = lambda off: (lambda hp, g: (jnp.maximum(g * G - 1, 0), off + hp))
    main = lambda off: (lambda hp, g: (g, off + hp))
    return _call(body, grid=(4, ng),
                 in_specs=[pl.BlockSpec((GR, LANES), main(0)), pl.BlockSpec((GR, LANES), main(4)),
                           pl.BlockSpec((B, LANES), halo(4)), pl.BlockSpec((GR, LANES), main(8)),
                           pl.BlockSpec((B, LANES), halo(8)), pl.BlockSpec((2, B, 2 * B), lambda hp, g: (hp, 0, 0))],
                 out_specs=(pl.BlockSpec((GR, LANES), main(0)), pl.BlockSpec((GR, LANES), main(0))),
                 out_shape=(_sds((S, ATTN_CH), F32), _sds((S, ATTN_CH), F32)),
                 scratch_shapes=[pltpu.VMEM((B + GR, LANES), BF16), pltpu.VMEM((B + GR, LANES), BF16)], name=name)(
        qkv, qkv, qkv, qkv, qkv, bias)


def _dil_spec(d, ts, C):
    if d == 1:
        return pl.BlockSpec((ts, C), lambda i: (i, 0))
    return pl.BlockSpec((d, ts // d, C), lambda i: (0, i, 0))


def _dil_view(a, d):
    return a if d == 1 else a.reshape(d, a.shape[0] // d, a.shape[1])


def _dilate(nat_tmp, dst_ref, cols, d, ts, dtype=F32):
    for r in range(d):
        dst_ref[r, :, cols] = nat_tmp[pl.ds(r, ts // d, stride=d), :].astype(dtype)


def _undilate(src_ref, nat_tmp, cols, d, ts, accumulate=False):
    for r in range(d):
        rows = pl.ds(r, ts // d, stride=d)
        if accumulate:
            nat_tmp[rows, :] = nat_tmp[rows, :] + src_ref[r, :, cols]
        else:
            nat_tmp[rows, :] = src_ref[r, :, cols]


def _attn_merge(o_list, lse_list, dils, *, name, ts):
    S, C = o_list[0].shape
    P = len(o_list)
    nd = sum(1 for d in dils if d > 1)

    def body(*refs):
        o_refs, l_refs = refs[:P], refs[P:2 * P]
        out_ref, lse_ref = refs[2 * P], refs[2 * P + 1]
        lse_d_refs = refs[2 * P + 2:2 * P + 2 + nd]
        scratch = list(refs[2 * P + 2 + nd:])
        tmp = scratch.pop()
        for j in range(C // LANES):
            cols = slice(j * LANES, (j + 1) * LANES)
            os_, ls, free = [], [], list(scratch)
            for o_ref, l_ref, d in zip(o_refs, l_refs, dils):
                if d > 1:
                    so, sl = free.pop(0), free.pop(0)
                    _undilate(o_ref, so, cols, d, ts)
                    _undilate(l_ref, sl, cols, d, ts)
                    os_.append(so[...])
                    ls.append(sl[...])
                else:
                    os_.append(o_ref[:, cols])
                    ls.append(l_ref[:, cols])
            m = ls[0]
            for l in ls[1:]:
                m = jnp.maximum(m, l)
            ws = [jnp.exp(l - m) for l in ls]
            den = ws[0]
            for w in ws[1:]:
                den = den + w
            num = ws[0] * os_[0]
            for w, o in zip(ws[1:], os_[1:]):
                num = num + w * o
            out_ref[:, cols] = num / den
            tmp[...] = m + jnp.log(den)
            lse_ref[:, cols] = tmp[...]
            for l_out, d in zip(lse_d_refs, [d for d in dils if d > 1]):
                _dilate(tmp, l_out, cols, d, ts)

    row = _dil_spec(1, ts, C)
    dd = [d for d in dils if d > 1]
    res = _call(body, grid=(S // ts,), in_specs=[_dil_spec(d, ts, C) for d in dils] * 2,
                out_specs=(row, row) + tuple(_dil_spec(d, ts, C) for d in dd),
                out_shape=(_sds((S, C), F32), _sds((S, C), F32)) + tuple(_sds((d, S // d, C), F32) for d in dd),
                scratch_shapes=[pltpu.VMEM((ts, LANES), F32)] * (2 * nd + 1), name=name)(
        *[_dil_view(o, d) for o, d in zip(o_list, dils)], *[_dil_view(l, d) for l, d in zip(lse_list, dils)])
    lse_by_d = {1: res[1]}
    lse_by_d.update({d: r.reshape(S, C) for d, r in zip(dd, res[2:])})
    return res[0], [lse_by_d[d] for d in dils]


def _attn_prep(dout, out, dils, *, name, ts):
    S, C = out.shape
    HD = HEAD_DIM
    dd = [d for d in dils if d > 1]
    nd = len(dd)

    def body(do_ref, o_ref, d_ref, dob_ref, *rest):
        outs, tmp_d, tmp_o = rest[:2 * nd], rest[2 * nd], rest[2 * nd + 1]
        dob_ref[...] = do_ref[...].astype(BF16)
        for j in range(C // LANES):
            cols = slice(j * LANES, (j + 1) * LANES)
            do = do_ref[:, cols]
            prod = do * o_ref[:, cols]
            tmp_o[...] = do
            for h in range(LANES // HD):
                cs = slice(h * HD, (h + 1) * HD)
                tmp_d[:, cs] = jnp.broadcast_to(jnp.sum(prod[:, cs], axis=1, keepdims=True), (ts, HD))
            d_ref[:, cols] = tmp_d[...]
            for d, dd_out, do_out in zip(dd, outs[:nd], outs[nd:]):
                _dilate(tmp_d, dd_out, cols, d, ts)
                _dilate(tmp_o, do_out, cols, d, ts, BF16)

    row = _dil_spec(1, ts, C)
    res = _call(body, grid=(S // ts,), in_specs=[row, row],
                out_specs=(row, row) + tuple(_dil_spec(d, ts, C) for d in dd) * 2,
                out_shape=(_sds((S, C), F32), _sds((S, C), BF16)) + tuple(_sds((d, S // d, C), F32) for d in dd)
                + tuple(_sds((d, S // d, C), BF16) for d in dd),
                scratch_shapes=[pltpu.VMEM((ts, LANES), F32)] * 2, name=name)(dout, out)
    dd_by_d, do_by_d = {1: res[0]}, {1: res[1]}
    dd_by_d.update({d: r.reshape(S, C) for d, r in zip(dd, res[2:2 + nd])})
    do_by_d.update({d: r.reshape(S, C) for d, r in zip(dd, res[2 + nd:])})
    return [dd_by_d[d] for d in dils], [do_by_d[d] for d in dils]


def _attn_bwd(qkv, do, lse, dd, bias, *, name, nb, G):
    S = qkv.shape[0]
    B, HD = ATTN_BLOCK, HEAD_DIM
    GR = G * B
    ng = S // GR
    nblk = S // B

    def body(q_ref, k_ref, kh_ref, v_ref, vh_ref, do_ref, lse_ref, dd_ref, qn_ref, don_ref, lsen_ref, ddn_ref,
             bias_ref, dq_ref, dk_ref, dv_ref, dbias_ref, kbuf, vbuf, dkp, dvp):
        g = pl.program_id(1)
        halo_ok = (g * G) % nb != 0
        next_ok = jnp.logical_and(((g + 1) * G) % nb != 0, g < ng - 1)
        kbuf[0:B, :] = kh_ref[...]
        kbuf[B:B + GR, :] = k_ref[...]
        vbuf[0:B, :] = vh_ref[...]
        vbuf[B:B + GR, :] = v_ref[...]
        col = lax.broadcasted_iota(jnp.int32, (B, 2 * B), 1)

        def tn_dot(a, b):
            return lax.dot_general(a, b, (((0,), (0,)), ((), ())), preferred_element_type=F32)

        @pl.when(g == 0)
        def _():
            dbias_ref[...] = jnp.zeros_like(dbias_ref)

        for bi in range(G):
            r0 = bi * B
            q2 = q_ref[r0:r0 + B, :]
            do2 = do_ref[r0:r0 + B, :]
            kk = kbuf[r0:r0 + 2 * B, :]
            vv = vbuf[r0:r0 + 2 * B, :]
            for hh in range(2):
                cs = slice(hh * HD, (hh + 1) * HD)
                s = lax.dot_general(q2[:, cs], kk[:, cs], (((1,), (1,)), ((), ())), preferred_element_type=F32)
                s = s + bias_ref[hh]
                if bi == 0:
                    s = jnp.where(jnp.logical_and(jnp.logical_not(halo_ok), col < B), NEG, s)
                p = jnp.exp(s - lse_ref[r0:r0 + B, hh * HD:hh * HD + 1])
                dp = lax.dot_general(do2[:, cs], vv[:, cs], (((1,), (1,)), ((), ())), preferred_element_type=F32)
                ds = p * (dp - dd_ref[r0:r0 + B, hh * HD:hh * HD + 1])
                dbias_ref[hh] += ds
                ds_bf, p_bf = ds.astype(BF16), p.astype(BF16)
                dq_ref[r0:r0 + B, cs] = jnp.dot(ds_bf, kk[:, cs], preferred_element_type=F32)
                dk_ref[r0:r0 + B, cs] = tn_dot(ds_bf[:, B:], q2[:, cs])
                dv_ref[r0:r0 + B, cs] = tn_dot(p_bf[:, B:], do2[:, cs])
                if bi > 0:
                    dkp[r0 - B:r0, cs] = tn_dot(ds_bf[:, :B], q2[:, cs])
                    dvp[r0 - B:r0, cs] = tn_dot(p_bf[:, :B], do2[:, cs])

        @pl.when(next_ok)
        def _():
            qn = qn_ref[...]
            don = don_ref[...]
            kl = kbuf[GR:GR + B, :]
            vl = vbuf[GR:GR + B, :]
            for hh in range(2):
                cs = slice(hh * HD, (hh + 1) * HD)
                s = lax.dot_general(qn[:, cs], kl[:, cs], (((1,), (1,)), ((), ())), preferred_element_type=F32)
                s = s + bias_ref[hh, :, 0:B]
                p = jnp.exp(s - lsen_ref[:, hh * HD:hh * HD + 1])
                dp = lax.dot_general(don[:, cs], vl[:, cs], (((1,), (1,)), ((), ())), preferred_element_type=F32)
                ds = p * (dp - ddn_ref[:, hh * HD:hh * HD + 1])
                dkp[GR - B:GR, cs] = tn_dot(ds.astype(BF16), qn[:, cs])
                dvp[GR - B:GR, cs] = tn_dot(p.astype(BF16), don[:, cs])

        @pl.when(jnp.logical_not(next_ok))
        def _():
            dkp[GR - B:GR, :] = jnp.zeros((B, LANES), F32)
            dvp[GR - B:GR, :] = jnp.zeros((B, LANES), F32)

        dk_ref[...] += dkp[...]
        dv_ref[...] += dvp[...]

    halo = lambda off: (lambda hp, g: (jnp.maximum(g * G - 1, 0), off + hp))
    main = lambda off: (lambda hp, g: (g, off + hp))
    nxt = lambda off: (lambda hp, g: (jnp.minimum((g + 1) * G, nblk - 1), off + hp))
    big, small = (lambda m: pl.BlockSpec((GR, LANES), m)), (lambda m: pl.BlockSpec((B, LANES), m))
    return _call(body, grid=(4, ng),
                 in_specs=[big(main(0)), big(main(4)), small(halo(4)), big(main(8)), small(halo(8)),
                           big(main(0)), big(main(0)), big(main(0)),
                           small(nxt(0)), small(nxt(0)), small(nxt(0)), small(nxt(0)),
                           pl.BlockSpec((2, B, 2 * B), lambda hp, g: (hp, 0, 0))],
                 out_specs=(big(main(0)), big(main(0)), big(main(0)),
                            pl.BlockSpec((2, B, 2 * B), lambda hp, g: (hp, 0, 0))),
                 out_shape=(_sds((S, ATTN_CH), F32),) * 3 + (_sds((ATTN_HEADS, B, 2 * B), F32),),
                 scratch_shapes=[pltpu.VMEM((B + GR, LANES), BF16), pltpu.VMEM((B + GR, LANES), BF16),
                                 pltpu.VMEM((GR, LANES), F32), pltpu.VMEM((GR, LANES), F32)], name=name)(
        qkv, qkv, qkv, qkv, qkv, do, lse, dd, qkv, do, lse, dd, bias)


def _attn_combine(dq_list, dk_list, dv_list, dils, *, name, ts):
    S, C = dq_list[0].shape
    P = len(dq_list)
    scale = HEAD_DIM ** -0.5
    assert dils[0] == 1

    def body(*refs):
        out_refs, acc = refs[3 * P:3 * P + 3], refs[3 * P + 3]
        for part in range(3):
            for j in range(C // LANES):
                cols = slice(j * LANES, (j + 1) * LANES)
                acc[...] = refs[part * P][:, cols]
                for r, d in zip(refs[part * P + 1:(part + 1) * P], dils[1:]):
                    _undilate(r, acc, cols, d, ts, accumulate=True)
                tot = acc[...]
                if part == 0:
                    tot = tot * scale
                out_refs[part][:, cols] = tot.astype(BF16)

    return _call(body, grid=(S // ts,), in_specs=[_dil_spec(d, ts, C) for d in dils] * 3,
                 out_specs=(_dil_spec(1, ts, C),) * 3, out_shape=(_sds((S, C), BF16),) * 3,
                 scratch_shapes=[pltpu.VMEM((ts, LANES), F32)], name=name)(
        *[_dil_view(a, d) for lst in (dq_list, dk_list, dv_list) for a, d in zip(lst, dils)])


def _bias_tables(table, bucket_flat, *, name):
    P, _, K = bucket_flat.shape
    H = table.shape[1]
    KC = 4096

    def body(t_ref, bk_ref, out_ref):
        row = lax.broadcasted_iota(jnp.int32, (N_BUCKETS, KC), 0)
        for c in range(K // KC):
            bk = bk_ref[0, :, c * KC:(c + 1) * KC]
            onehot = (row == bk).astype(F32)
            vals = jnp.dot(t_ref[...], onehot, preferred_element_type=F32, precision=lax.Precision.HIGHEST)
            out_ref[0, :, c * KC:(c + 1) * KC] = jnp.where(bk >= 0, vals, NEG)

    return _call(body, grid=(P,),
                 in_specs=[pl.BlockSpec((H, N_BUCKETS), lambda p: (0, 0)), pl.BlockSpec((1, 1, K), lambda p: (p, 0, 0))],
                 out_specs=pl.BlockSpec((1, H, K), lambda p: (p, 0, 0)), out_shape=_sds((P, H, K), F32),
                 name=name)(table.T, bucket_flat)


def _bias_grad(dbias_flat, bucket_flat, *, name):
    P, H, K = dbias_flat.shape
    KC = 4096

    def body(db_ref, bk_ref, out_ref):
        p = pl.program_id(0)
        acc = jnp.zeros((N_BUCKETS, H), F32)
        row = lax.broadcasted_iota(jnp.int32, (N_BUCKETS, KC), 0)
        for c in range(K // KC):
            onehot = (row == bk_ref[0, :, c * KC:(c + 1) * KC]).astype(F32)
            acc = acc + lax.dot_general(onehot, db_ref[0, :, c * KC:(c + 1) * KC], (((1,), (1,)), ((), ())),
                                        preferred_element_type=F32, precision=lax.Precision.HIGHEST)

        @pl.when(p == 0)
        def _():
            out_ref[...] = acc

        @pl.when(p > 0)
        def _():
            out_ref[...] += acc

    return _call(body, grid=(P,),
                 in_specs=[pl.BlockSpec((1, H, K), lambda p: (p, 0, 0)), pl.BlockSpec((1, 1, K), lambda p: (p, 0, 0))],
                 out_specs=pl.BlockSpec((N_BUCKETS, H), lambda p: (0, 0)), out_shape=_sds((N_BUCKETS, H), F32),
                 name=name)(dbias_flat, bucket_flat)


def _allgather(blocks, *, name):
    n = len(blocks)

    def body(*refs):
        x_refs, out_refs = refs[:n], refs[n:2 * n]
        send_sems, recv_sems, local_sems = refs[2 * n:]
        x, y, c = lax.axis_index("x"), lax.axis_index("y"), lax.axis_index("c")
        me, sibling = (x, y, c), (x, y, 1 - c)
        chips = [(1 - x, y), (x, 1 - y), (1 - x, 1 - y)]

        def copy(i, k, blk, to, own=False):
            slot = out_refs[i].at[4 * blk[0] + 2 * blk[1] + blk[2]]
            return pltpu.make_async_remote_copy(src_ref=x_refs[i] if own else slot, dst_ref=slot,
                                                send_sem=send_sems.at[7 * i + k], recv_sem=recv_sems.at[7 * i + k],
                                                device_id=to, device_id_type=MESH)

        mine = [pltpu.make_async_copy(x_refs[i], out_refs[i].at[4 * x + 2 * y + c], local_sems.at[i])
                for i in range(n)]
        for cp in mine:
            cp.start()
        first = []
        for i in range(n):
            first.append(copy(i, 0, me, sibling, own=True))
            first += [copy(i, 1 + j, me, (*chip, c), own=True) for j, chip in enumerate(chips)]
        for cp in first:
            cp.start()
        passed = []
        for j, chip in enumerate(chips):
            for i in range(n):
                copy(i, 1 + j, (*chip, c), me).wait_recv()
                fwd = copy(i, 4 + j, (*chip, c), sibling)
                fwd.start()
                passed.append(fwd)
        for i in range(n):
            copy(i, 0, sibling, me).wait_recv()
            for j, chip in enumerate(chips):
                copy(i, 4 + j, (*chip, 1 - c), me).wait_recv()
        for cp in first + passed:
            cp.wait_send()
        for cp in mine:
            cp.wait()

    any_spec = pl.BlockSpec(memory_space=pl.ANY)
    return pl.pallas_call(body, out_shape=tuple(_sds((N_DEV,) + b.shape, b.dtype) for b in blocks),
                          in_specs=[any_spec] * n, out_specs=(any_spec,) * n,
                          scratch_shapes=[pltpu.SemaphoreType.DMA((7 * n,)), pltpu.SemaphoreType.DMA((7 * n,)),
                                          pltpu.SemaphoreType.DMA((n,))], name=name)(*blocks)


def _exchange(sends, *, name):
    n = len(sends)

    def body(*refs):
        send_refs, recv_refs = refs[:n], refs[n:2 * n]
        send_sems, recv_sems, local_sems = refs[2 * n:]
        x, y, c = lax.axis_index("x"), lax.axis_index("y"), lax.axis_index("c")
        me = 4 * x + 2 * y + c
        mine = [pltpu.make_async_copy(send_refs[i].at[me], recv_refs[i].at[me], local_sems.at[i]) for i in range(n)]
        for cp in mine:
            cp.start()
        copies = []
        for k in range(1, N_DEV):
            px = 1 - x if k & 4 else x
            py = 1 - y if k & 2 else y
            pc = 1 - c if k & 1 else c
            for i in range(n):
                cp = pltpu.make_async_remote_copy(src_ref=send_refs[i].at[4 * px + 2 * py + pc],
                                                  dst_ref=recv_refs[i].at[me],
                                                  send_sem=send_sems.at[7 * i + k - 1],
                                                  recv_sem=recv_sems.at[7 * i + k - 1],
                                                  device_id=(px, py, pc), device_id_type=MESH)
                cp.start()
                copies.append(cp)
        for cp in copies:
            cp.wait_recv()
        for cp in copies:
            cp.wait_send()
        for cp in mine:
            cp.wait()

    any_spec = pl.BlockSpec(memory_space=pl.ANY)
    return pl.pallas_call(body, out_shape=tuple(_sds(s.shape, s.dtype) for s in sends), in_specs=[any_spec] * n,
                          out_specs=(any_spec,) * n,
                          scratch_shapes=[pltpu.SemaphoreType.DMA((7 * n,)), pltpu.SemaphoreType.DMA((7 * n,)),
                                          pltpu.SemaphoreType.DMA((n,))], name=name)(*sends)


def _adamw(w, m, v, g_parts, *, name, tr):
    R, W = w.shape
    bc1 = 1.0 - ADAM_B1 ** ADAM_STEP
    bc2 = 1.0 - ADAM_B2 ** ADAM_STEP

    def body(w_ref, m_ref, v_ref, g_ref, go_ref, d_ref, mo_ref, vo_ref):
        g = g_ref[0].astype(F32)
        for i in range(1, N_DEV):
            g = g + g_ref[i].astype(F32)
        mn = ADAM_B1 * m_ref[...] + (1.0 - ADAM_B1) * g
        vn = ADAM_B2 * v_ref[...] + (1.0 - ADAM_B2) * (g * g)
        m_hat = mn / bc1
        v_hat = vn / bc2
        go_ref[...] = g
        d_ref[...] = -ADAM_LR * (m_hat / (jnp.sqrt(v_hat) + ADAM_EPS) + ADAM_WD * w_ref[...])
        mo_ref[...] = mn
        vo_ref[...] = vn

    row = pl.BlockSpec((tr, W), lambda i: (i, 0))
    return _call(body, grid=(R // tr,), in_specs=[row, row, row, pl.BlockSpec((N_DEV, tr, W), lambda i: (0, i, 0))],
                 out_specs=(row,) * 4, out_shape=(_sds((R, W), F32),) * 4, name=name)(w, m, v, g_parts)


def _round_up(n, k):
    return -(-n // k) * k


def _pack(arrs, width, row_mult):
    pieces, offs, r = [], [], 0
    for a in arrs:
        n = a.size
        rows = _round_up(n, width) // width
        flat = a.reshape(-1)
        if rows * width != n:
            flat = jnp.pad(flat, (0, rows * width - n))
        pieces.append(flat.reshape(rows, width))
        offs.append((r, rows, n))
        r += rows
    total = _round_up(r, row_mult)
    if total != r:
        pieces.append(jnp.zeros((total - r, width), pieces[0].dtype))
    return jnp.concatenate(pieces, axis=0), offs


def _unpack(pack, offs, shapes):
    out = []
    for (r, rows, n), shp in zip(offs, shapes):
        out.append(pack[r:r + rows].reshape(-1)[:n].reshape(shp))
    return out


def _gather_axis(full8, axis):
    moved = jnp.moveaxis(full8, 0, axis)
    shp = list(moved.shape)
    shp[axis:axis + 2] = [shp[axis] * shp[axis + 1]]
    return moved.reshape(shp)


def _split_axis(full, axis):
    shp = list(full.shape)
    shp[axis:axis + 1] = [N_DEV, shp[axis] // N_DEV]
    return jnp.moveaxis(full.reshape(shp), axis, 0)


def kernel(x, w_in, b_in, conv_dw_w, conv_dw_b, conv_ln_g, conv_ln_b, rel_bias_table, gmlp_ln_g, gmlp_ln_b, gmlp_w_s, gmlp_b_s, w_out, b_out, ln1_g, ln1_b, ffn_w_up, ffn_b_up, ffn_conv_w, ffn_conv_b, ffn_w_down, ffn_b_down, ln2_g, ln2_b, loss_target, m_w_in, m_b_in, m_conv_dw_w, m_conv_dw_b, m_conv_ln_g, m_conv_ln_b, m_rel_bias_table, m_gmlp_ln_g, m_gmlp_ln_b, m_gmlp_w_s, m_gmlp_b_s, m_w_out, m_b_out, m_ln1_g, m_ln1_b, m_ffn_w_up, m_ffn_b_up, m_ffn_conv_w, m_ffn_conv_b, m_ffn_w_down, m_ffn_b_down, m_ln2_g, m_ln2_b, v_w_in, v_b_in, v_conv_dw_w, v_conv_dw_b, v_conv_ln_g, v_conv_ln_b, v_rel_bias_table, v_gmlp_ln_g, v_gmlp_ln_b, v_gmlp_w_s, v_gmlp_b_s, v_w_out, v_b_out, v_ln1_g, v_ln1_b, v_ffn_w_up, v_ffn_b_up, v_ffn_conv_w, v_ffn_conv_b, v_ffn_w_down, v_ffn_b_down, v_ln2_g, v_ln2_b):
    W = dict(w_in=w_in, b_in=b_in, conv_dw_w=conv_dw_w, conv_dw_b=conv_dw_b, conv_ln_g=conv_ln_g,
             conv_ln_b=conv_ln_b, rel_bias_table=rel_bias_table, gmlp_ln_g=gmlp_ln_g, gmlp_ln_b=gmlp_ln_b,
             gmlp_w_s=gmlp_w_s, gmlp_b_s=gmlp_b_s, w_out=w_out, b_out=b_out, ln1_g=ln1_g, ln1_b=ln1_b,
             ffn_w_up=ffn_w_up, ffn_b_up=ffn_b_up, ffn_conv_w=ffn_conv_w, ffn_conv_b=ffn_conv_b,
             ffn_w_down=ffn_w_down, ffn_b_down=ffn_b_down, ln2_g=ln2_g, ln2_b=ln2_b)
    Mo = dict(w_in=m_w_in, b_in=m_b_in, conv_dw_w=m_conv_dw_w, conv_dw_b=m_conv_dw_b, conv_ln_g=m_conv_ln_g,
              conv_ln_b=m_conv_ln_b, rel_bias_table=m_rel_bias_table, gmlp_ln_g=m_gmlp_ln_g, gmlp_ln_b=m_gmlp_ln_b,
              gmlp_w_s=m_gmlp_w_s, gmlp_b_s=m_gmlp_b_s, w_out=m_w_out, b_out=m_b_out, ln1_g=m_ln1_g, ln1_b=m_ln1_b,
              ffn_w_up=m_ffn_w_up, ffn_b_up=m_ffn_b_up, ffn_conv_w=m_ffn_conv_w, ffn_conv_b=m_ffn_conv_b,
              ffn_w_down=m_ffn_w_down, ffn_b_down=m_ffn_b_down, ln2_g=m_ln2_g, ln2_b=m_ln2_b)
    Vo = dict(w_in=v_w_in, b_in=v_b_in, conv_dw_w=v_conv_dw_w, conv_dw_b=v_conv_dw_b, conv_ln_g=v_conv_ln_g,
              conv_ln_b=v_conv_ln_b, rel_bias_table=v_rel_bias_table, gmlp_ln_g=v_gmlp_ln_g, gmlp_ln_b=v_gmlp_ln_b,
              gmlp_w_s=v_gmlp_w_s, gmlp_b_s=v_gmlp_b_s, w_out=v_w_out, b_out=v_b_out, ln1_g=v_ln1_g, ln1_b=v_ln1_b,
              ffn_w_up=v_ffn_w_up, ffn_b_up=v_ffn_b_up, ffn_conv_w=v_ffn_conv_w, ffn_conv_b=v_ffn_conv_b,
              ffn_w_down=v_ffn_w_down, ffn_b_down=v_ffn_b_down, ln2_g=v_ln2_g, ln2_b=v_ln2_b)

    xs = x[0]
    target = loss_target[0]
    S, D = xs.shape
    F2 = ffn_b_up.shape[1]
    F = F2 // 2
    ts = min(512, S)
    G = _attn_group(S)
    tc = F // 2 if (F // 2) % LANES == 0 else F

    mat_names = SHARDED[:4]
    payload = [W[n].astype(BF16) if n in mat_names else W[n] for n in SHARDED]
    wall = _allgather(payload, name="weight_allgather")
    full = {n: _gather_axis(parts, SHARD_AXIS[n]) for n, parts in zip(SHARDED, wall)}

    tables = [_pattern_tables(w, d) for (w, d) in PATTERNS]
    bucket_flat = jnp.asarray(np.stack([np.where(v, b, -1).reshape(1, -1) for (b, v) in tables]).astype(np.int32))
    bias_all = _bias_tables(rel_bias_table, bucket_flat, name="bias_tables")
    biases = [bias_all[p].reshape(ATTN_HEADS, ATTN_BLOCK, 2 * ATTN_BLOCK) for p in range(len(PATTERNS))]
    nbs = [(S // d) // ATTN_BLOCK for (_, d) in PATTERNS]
    dils = [d for (_, d) in PATTERNS]
    scale = HEAD_DIM ** -0.5

    saved = []
    cur = xs
    for l in range(DEPTH):
        Win, Wout, Wup, Wdown = full['w_in'][l], full['w_out'][l], full['ffn_w_up'][l], full['ffn_w_down'][l]
        qcols = slice(2 * CONV_CH, 2 * CONV_CH + ATTN_CH)
        Win_s = Win.at[:, qcols].multiply(scale)
        b_in_s = b_in[l].at[qcols].multiply(scale)
        h_a, qkv, h_c = _mm([cur], [Win_s], bias=b_in_s, tm=ts, name="in_proj",
                            splits=((2 * CONV_CH, F32, 1.0), (3 * ATTN_CH, F32, 1.0), (2 * GMLP_CH, F32, 1.0)))
        conv_out, hc = _conv_fwd(h_a, full['conv_dw_w'][l], conv_dw_b[l], conv_ln_g[l], conv_ln_b[l],
                                 name="conv_fwd", ts=ts)
        qkv_d = _dilate_qkv(qkv, name="dilate_qkv", ts=ts)
        o_ps, lse_ps = [], []
        for p, d in enumerate(dils):
            o_p, lse_p = _attn_fwd(qkv_d[p], biases[p], name=f"attn_fwd_d{d}", nb=nbs[p], G=G)
            o_ps.append(o_p)
            lse_ps.append(lse_p)
        attn_out, lse = _attn_merge(o_ps, lse_ps, dils, name="attn_merge", ts=ts)
        w_tril = jnp.tril(gmlp_w_s[l]).astype(BF16)
        bs_rows = jnp.repeat(gmlp_b_s[l].T, GMLP_GROUP_DIM, axis=1)
        gm_out = _gmlp_fwd(h_c, gmlp_ln_g[l], gmlp_ln_b[l], w_tril, bs_rows, name="gmlp_fwd", ts=ts)
        x1, xhat1, rstd1 = _mm([conv_out, attn_out, gm_out],
                               [Wout[:CONV_CH], Wout[CONV_CH:CONV_CH + ATTN_CH], Wout[CONV_CH + ATTN_CH:]],
                               bias=b_out[l], resid=cur, resid_scale=ALPHA, ln=(ln1_g[l], ln1_b[l]), tm=ts,
                               name="out_proj_ln")
        hu = _mm([x1], [Wup], bias=ffn_b_up[l], tm=ts, tn=F, name="ffn_up")
        act = _ffn_act_fwd(hu, full['ffn_conv_w'][l], ffn_conv_b[l], name="ffn_act_fwd", ts=min(256, S), tc=tc)
        x2, xhat2, rstd2 = _mm([act], [Wdown], bias=ffn_b_down[l], resid=x1, resid_scale=ALPHA,
                               ln=(ln2_g[l], ln2_b[l]), tm=ts, name="ffn_down_ln")
        saved.append(dict(x0=cur, h_a=h_a, h_c=h_c, qkv_d=qkv_d, hc=hc, conv_out=conv_out, attn_out=attn_out,
                          lse=lse, gm_out=gm_out, w_tril=w_tril, bs_rows=bs_rows, x1=x1, xhat1=xhat1, rstd1=rstd1,
                          hu=hu, act=act, xhat2=xhat2, rstd2=rstd2))
        cur = x2

    grads = {n: [None] * DEPTH for n in WEIGHTS if n != 'rel_bias_table'}
    drel = None
    dx = None
    loss_part = None
    tk = min(512, S)
    for l in reversed(range(DEPTH)):
        sv = saved[l]
        Win, Wout, Wup, Wdown = full['w_in'][l], full['w_out'][l], full['ffn_w_up'][l], full['ffn_w_down'][l]
        if dx is None:
            dz2, dg2, db2, dzs2, loss_part = _ln_bwd(sv['xhat2'], sv['rstd2'], ln2_g[l], b=ln2_b[l], target=target,
                                                     name="ln2_bwd_loss", ts=ts)
        else:
            dz2, dg2, db2, dzs2 = _ln_bwd(sv['xhat2'], sv['rstd2'], ln2_g[l], dy=dx, name="ln_bwd", ts=ts)
        grads['ln2_g'][l], grads['ln2_b'][l], grads['ffn_b_down'][l] = dg2[0], db2[0], dzs2[0]
        grads['ffn_w_down'][l] = _mm_tn(sv['act'], dz2, tm=F // 2 if (F // 2) % LANES == 0 else F, tn=D, tk=tk,
                                        name="dw_down")
        dact = _mm([dz2], [Wdown.T], tm=ts, name="dact")
        dhg, dhv, dwg, dwv, dbg, dbv, dug, duv = _ffn_act_bwd(sv['hu'], dact, full['ffn_conv_w'][l], ffn_conv_b[l],
                                                              name="ffn_act_bwd", ts=min(256, S), tc=tc)
        grads['ffn_conv_w'][l] = jnp.concatenate([dwg, dwv], axis=1)
        grads['ffn_conv_b'][l] = jnp.concatenate([dbg, dbv], axis=1)[0]
        grads['ffn_b_up'][l] = jnp.concatenate([dug, duv], axis=1)[0]
        grads['ffn_w_up'][l] = jnp.concatenate(
            [_mm_tn(sv['x1'], dhg, tm=D, tn=tc, tk=tk, name="dw_up"),
             _mm_tn(sv['x1'], dhv, tm=D, tn=tc, tk=tk, name="dw_up")], axis=1)
        WupT = Wup.T
        dx1 = _mm([dhg, dhv], [WupT[:F], WupT[F:]], resid=dz2, resid_scale=ALPHA, tm=ts, name="dx1")
        dz1, dg1, db1, dzs1 = _ln_bwd(sv['xhat1'], sv['rstd1'], ln1_g[l], dy=dx1, name="ln_bwd", ts=ts)
        grads['ln1_g'][l], grads['ln1_b'][l], grads['b_out'][l] = dg1[0], db1[0], dzs1[0]
        grads['w_out'][l] = jnp.concatenate(
            [_mm_tn(sv['conv_out'], dz1, tm=CONV_CH, tn=D, tk=tk, name="dw_out_conv"),
             _mm_tn(sv['attn_out'], dz1, tm=ATTN_CH, tn=D, tk=tk, name="dw_out_attn"),
             _mm_tn(sv['gm_out'], dz1, tm=GMLP_CH, tn=D, tk=tk, name="dw_out_conv")], axis=0)
        dc_conv, dc_attn, dc_gm = _mm([dz1], [Wout.T], tm=ts, name="dcat",
                                      splits=((CONV_CH, F32, 1.0), (ATTN_CH, F32, 1.0), (GMLP_CH, F32, 1.0)))
        dh_a, ddw, ddwb, dclg, dclb = _conv_bwd(sv['h_a'], sv['hc'], dc_conv, full['conv_dw_w'][l], conv_ln_g[l],
                                                conv_ln_b[l], name="conv_bwd", ts=ts)
        grads['conv_dw_w'][l], grads['conv_dw_b'][l] = ddw[:CONV_WIDTH], ddwb[0]
        grads['conv_ln_g'][l], grads['conv_ln_b'][l] = dclg[0], dclb[0]
        dd_d, do_d = _attn_prep(dc_attn, sv['attn_out'], dils, name="attn_prep", ts=ts)
        dqs, dks, dvs, dbs = [], [], [], []
        for p, d in enumerate(dils):
            dq, dk, dv, dbias = _attn_bwd(sv['qkv_d'][p], do_d[p], sv['lse'][p], dd_d[p], biases[p],
                                          name=f"attn_bwd_d{d}", nb=nbs[p], G=G)
            dqs.append(dq)
            dks.append(dk)
            dvs.append(dv)
            dbs.append(dbias.reshape(1, ATTN_HEADS, -1))
        dqkv = _attn_combine(dqs, dks, dvs, dils, name="attn_combine", ts=ts)
        dr = _bias_grad(jnp.concatenate(dbs, axis=0), bucket_flat, name="bias_grad")
        drel = dr if drel is None else drel + dr
        w_tril_t = jnp.swapaxes(sv['w_tril'], 1, 2)
        dh_c, dws, dbs_acc, dglg, dglb = _gmlp_bwd(sv['h_c'], dc_gm, gmlp_ln_g[l], gmlp_ln_b[l], sv['w_tril'],
                                                   w_tril_t, sv['bs_rows'], name="gmlp_bwd", ts=ts)
        grads['gmlp_w_s'][l] = jnp.tril(dws)
        grads['gmlp_b_s'][l] = dbs_acc[:, :GMLP_GROUPS].T
        grads['gmlp_ln_g'][l], grads['gmlp_ln_b'][l] = dglg[0], dglb[0]
        dwa, ca = _mm_tn(sv['x0'], dh_a, tm=D, tn=2 * CONV_CH, tk=tk, colsum=True, name="dw_in_side")
        dwq = [_mm_tn(sv['x0'], part, tm=D, tn=ATTN_CH, tk=tk, colsum=True, name="dw_in_side") for part in dqkv]
        dwc, cc = _mm_tn(sv['x0'], dh_c, tm=D, tn=2 * GMLP_CH, tk=tk, colsum=True, name="dw_in_side")
        grads['w_in'][l] = jnp.concatenate([dwa] + [w for w, _ in dwq] + [dwc], axis=1)
        grads['b_in'][l] = jnp.concatenate([ca] + [c for _, c in dwq] + [cc], axis=1)[0]
        WinT = Win.T
        edges = [0, 2 * CONV_CH] + [2 * CONV_CH + k * ATTN_CH for k in (1, 2, 3)] + [WinT.shape[0]]
        dx = _mm([dh_a, *dqkv, dh_c], [WinT[a:b] for a, b in zip(edges[:-1], edges[1:])], resid=dz1,
                 resid_scale=ALPHA, tm=ts, name="dx0")

    gfull = {n: jnp.stack(v) for n, v in grads.items()}
    gfull['rel_bias_table'] = drel

    sends = []
    for n in SHARDED:
        parts = _split_axis(gfull[n], SHARD_AXIS[n])
        sends.append(parts.reshape(N_DEV, -1, parts.shape[-1]).astype(BF16))
    recvs = _exchange(sends, name="grad_exchange")
    shard_out = [[], [], [], []]
    for n, recv in zip(SHARDED, recvs):
        shp = W[n].shape
        rows = recv.shape[1]
        tr = rows // 4 if rows % 64 == 0 else rows
        outs = _adamw(*[src[n].reshape(rows, shp[-1]) for src in (W, Mo, Vo)], recv, name=f"adamw_{n}", tr=tr)
        for kind in range(4):
            shard_out[kind].append(outs[kind].reshape(shp))

    gsmall, soffs = _pack([gfull[n] for n in SMALL], LANES, 8)
    gall = _allgather([gsmall], name="small_grad_allgather")[0]
    spacks = [_pack([src[n] for n in SMALL], LANES, 8)[0] for src in (W, Mo, Vo)]
    souts = _adamw(spacks[0], spacks[1], spacks[2], gall, name="adamw_small", tr=gsmall.shape[0])
    small_out = [_unpack(o, soffs, [W[n].shape for n in SMALL]) for o in souts]

    loss = lax.psum(loss_part[0, 0], ("x", "y", "c"))
    by_kind = []
    for kind in range(4):
        d = dict(zip(SHARDED, shard_out[kind]))
        d.update(zip(SMALL, small_out[kind]))
        by_kind.append([d[n] for n in WEIGHTS])
    return (loss, dx[None], *by_kind[0], *by_kind[1], *by_kind[2], *by_kind[3])
```

```python
import math

import numpy as np
import jax
import jax.numpy as jnp
from jax import lax
from jax.experimental import pallas as pl
from jax.experimental.pallas import tpu as pltpu

F32 = jnp.float32
BF16 = jnp.bfloat16

DEPTH = 2
HEAD_DIM = 64
CONV_CH = 256
CONV_WIDTH = 31
ATTN_HEADS = 8
ATTN_CH = ATTN_HEADS * HEAD_DIM
PATTERNS = ((128, 1), (512, 4), (2048, 16))
ATTN_BLOCK = 128
N_BUCKETS = 32
MAX_DISTANCE = 2048
GMLP_CH = 256
GMLP_GROUPS = 4
GMLP_GROUP_DIM = GMLP_CH // GMLP_GROUPS
CHUNK = 128
FFN_CONV_WIDTH = 3
LN_EPS = 1e-5
ALPHA = (2.0 * DEPTH) ** 0.25
ADAM_LR = 0.001
ADAM_B1 = 0.9
ADAM_B2 = 0.999
ADAM_EPS = 1e-08
ADAM_WD = 0.01
ADAM_STEP = 10
NEG = -1e30
N_DEV = 8
LANES = 128
CONV_HALO = 32
FFN_HALO = 8
FFN_ROWS = 16
MESH = pl.DeviceIdType.MESH

WEIGHTS = ['w_in', 'b_in', 'conv_dw_w', 'conv_dw_b', 'conv_ln_g', 'conv_ln_b', 'rel_bias_table', 'gmlp_ln_g',
           'gmlp_ln_b', 'gmlp_w_s', 'gmlp_b_s', 'w_out', 'b_out', 'ln1_g', 'ln1_b', 'ffn_w_up', 'ffn_b_up',
           'ffn_conv_w', 'ffn_conv_b', 'ffn_w_down', 'ffn_b_down', 'ln2_g', 'ln2_b']
SHARDED = ['w_in', 'w_out', 'ffn_w_up', 'ffn_w_down', 'conv_dw_w', 'ffn_conv_w']
SHARD_AXIS = {'w_in': 2, 'w_out': 1, 'ffn_w_up': 2, 'ffn_w_down': 1, 'conv_dw_w': 2, 'ffn_conv_w': 2}
SMALL = [n for n in WEIGHTS if n not in SHARDED]


def _call(body, *, grid=(), vmem_mb=48, **kw):
    params = pltpu.CompilerParams(dimension_semantics=("arbitrary",) * len(grid), vmem_limit_bytes=vmem_mb << 20)
    return pl.pallas_call(body, grid=grid, compiler_params=params, **kw)


def _sds(shape, dtype):
    return jax.ShapeDtypeStruct(shape, dtype)


def _ln_rows(z):
    mu = jnp.mean(z, axis=-1, keepdims=True)
    zc = z - mu
    var = jnp.mean(zc * zc, axis=-1, keepdims=True)
    rstd = lax.rsqrt(var + LN_EPS)
    return zc * rstd, rstd


def _ln_bwd_rows(dxhat, xhat, rstd):
    m1 = jnp.mean(dxhat, axis=-1, keepdims=True)
    m2 = jnp.mean(dxhat * xhat, axis=-1, keepdims=True)
    return rstd * (dxhat - m1 - xhat * m2)


def _colsum(v):
    return jnp.sum(v, axis=0, keepdims=True)


def _mm(a_list, w_list, *, name, tm, tn=None, bias=None, resid=None, resid_scale=1.0, ln=None, splits=None,
        out_dtype=F32):
    na = len(a_list)
    M = a_list[0].shape[0]
    N = w_list[0].shape[1]
    tn = N if tn is None else tn
    assert M % tm == 0 and N % tn == 0
    assert ln is None or tn == N
    assert splits is None or tn == N

    def body(*refs):
        a_refs, w_refs = refs[:na], refs[na:2 * na]
        pos = 2 * na
        acc = None
        for a_ref, w_ref in zip(a_refs, w_refs):
            t = jnp.dot(a_ref[...].astype(BF16), w_ref[...], preferred_element_type=F32)
            acc = t if acc is None else acc + t
        if bias is not None:
            acc = acc + refs[pos][...]
            pos += 1
        if resid is not None:
            acc = resid_scale * refs[pos][...] + acc
            pos += 1
        if ln is not None:
            g_ref, b_ref = refs[pos], refs[pos + 1]
            y_ref, xhat_ref, rstd_ref = refs[pos + 2], refs[pos + 3], refs[pos + 4]
            xhat, rstd = _ln_rows(acc)
            y_ref[...] = xhat * g_ref[...] + b_ref[...]
            xhat_ref[...] = xhat
            rstd_ref[...] = rstd
        elif splits is not None:
            c0 = 0
            for o_ref, (width, dtype, scale) in zip(refs[pos:], splits):
                part = acc[:, c0:c0 + width]
                if scale != 1.0:
                    part = part * scale
                o_ref[...] = part.astype(dtype)
                c0 += width
        else:
            refs[pos][...] = acc.astype(out_dtype)

    in_specs = [pl.BlockSpec((tm, a.shape[1]), lambda j, i: (i, 0)) for a in a_list]
    in_specs += [pl.BlockSpec((w.shape[0], tn), lambda j, i: (0, j)) for w in w_list]
    args = list(a_list) + list(w_list)
    if bias is not None:
        in_specs.append(pl.BlockSpec((1, tn), lambda j, i: (0, j)))
        args.append(bias.reshape(1, N))
    if resid is not None:
        in_specs.append(pl.BlockSpec((tm, tn), lambda j, i: (i, j)))
        args.append(resid)
    if ln is not None:
        in_specs += [pl.BlockSpec((1, N), lambda j, i: (0, 0))] * 2
        args += [ln[0].reshape(1, N), ln[1].reshape(1, N)]
        out_shape = (_sds((M, N), F32), _sds((M, N), F32), _sds((M, 1), F32))
        out_specs = (pl.BlockSpec((tm, N), lambda j, i: (i, 0)), pl.BlockSpec((tm, N), lambda j, i: (i, 0)),
                     pl.BlockSpec((tm, 1), lambda j, i: (i, 0)))
    elif splits is not None:
        out_shape = tuple(_sds((M, w), d) for (w, d, _) in splits)
        out_specs = tuple(pl.BlockSpec((tm, w), lambda j, i: (i, 0)) for (w, _, _) in splits)
    else:
        out_shape = _sds((M, N), out_dtype)
        out_specs = pl.BlockSpec((tm, tn), lambda j, i: (i, j))
    return _call(body, grid=(N // tn, M // tm), in_specs=in_specs, out_specs=out_specs, out_shape=out_shape,
                 name=name, vmem_mb=56)(*args)


def _mm_tn(a, dy, *, name, tm, tn, tk, colsum=False):
    S, Ka = a.shape
    N = dy.shape[1]
    assert S % tk == 0 and Ka % tm == 0 and N % tn == 0

    def body(a_ref, dy_ref, out_ref, *cs):
        i, k = pl.program_id(1), pl.program_id(2)
        dyb = dy_ref[...]
        part = lax.dot_general(a_ref[...].astype(BF16), dyb.astype(BF16), (((0,), (0,)), ((), ())),
                               preferred_element_type=F32)

        @pl.when(k == 0)
        def _():
            out_ref[...] = part

        @pl.when(k > 0)
        def _():
            out_ref[...] += part

        if colsum:
            cs_ref = cs[0]
            s = _colsum(dyb.astype(F32))

            @pl.when((i == 0) & (k == 0))
            def _():
                cs_ref[...] = s

            @pl.when((i == 0) & (k > 0))
            def _():
                cs_ref[...] += s

    out_shape = [_sds((Ka, N), F32)]
    out_specs = [pl.BlockSpec((tm, tn), lambda j, i, k: (i, j))]
    if colsum:
        out_shape.append(_sds((1, N), F32))
        out_specs.append(pl.BlockSpec((1, tn), lambda j, i, k: (0, j)))
    res = _call(body, grid=(N // tn, Ka // tm, S // tk),
                in_specs=[pl.BlockSpec((tk, tm), lambda j, i, k: (k, i)), pl.BlockSpec((tk, tn), lambda j, i, k: (k, j))],
                out_specs=tuple(out_specs), out_shape=tuple(out_shape), name=name, vmem_mb=56)(a, dy)
    return res if colsum else res[0]


def _ln_bwd(xhat, rstd, g, *, name, ts, dy=None, b=None, target=None):
    S, D = xhat.shape
    from_loss = target is not None

    def body(*refs):
        if from_loss:
            xhat_ref, rstd_ref, g_ref, b_ref, t_ref, dz_ref, dg_ref, db_ref, dzs_ref, loss_ref = refs
        else:
            xhat_ref, rstd_ref, g_ref, dy_ref, dz_ref, dg_ref, db_ref, dzs_ref = refs
        i = pl.program_id(0)
        xh = xhat_ref[...]
        gg = g_ref[...]
        if from_loss:
            err = xh * gg + b_ref[...] - t_ref[...]
            dyv = err * (1.0 / D)
            lsum = (0.5 / D) * jnp.sum(err * err, axis=(0, 1), keepdims=True)
        else:
            dyv = dy_ref[...]
        dz = _ln_bwd_rows(dyv * gg, xh, rstd_ref[...])
        dz_ref[...] = dz
        parts = [(dg_ref, _colsum(dyv * xh)), (db_ref, _colsum(dyv)), (dzs_ref, _colsum(dz))]
        if from_loss:
            parts.append((loss_ref, lsum))

        @pl.when(i == 0)
        def _():
            for r, v in parts:
                r[...] = v

        @pl.when(i > 0)
        def _():
            for r, v in parts:
                r[...] += v

    row = pl.BlockSpec((ts, D), lambda i: (i, 0))
    vec = pl.BlockSpec((1, D), lambda i: (0, 0))
    in_specs = [row, pl.BlockSpec((ts, 1), lambda i: (i, 0)), vec]
    args = [xhat, rstd, g.reshape(1, D)]
    if from_loss:
        in_specs += [vec, row]
        args += [b.reshape(1, D), target]
    else:
        in_specs += [row]
        args += [dy]
    out_shape = [_sds((S, D), F32), _sds((1, D), F32), _sds((1, D), F32), _sds((1, D), F32)]
    out_specs = [row, vec, vec, vec]
    if from_loss:
        out_shape.append(_sds((1, 1), F32))
        out_specs.append(pl.BlockSpec((1, 1), lambda i: (0, 0)))
    return _call(body, grid=(S // ts,), in_specs=in_specs, out_specs=tuple(out_specs), out_shape=tuple(out_shape),
                 name=name)(*args)


def _glu(v):
    return v[:, :CONV_CH] * jax.nn.sigmoid(v[:, CONV_CH:])


def _conv_fwd(h_a, dw_w, dw_b, ln_g, ln_b, *, name, ts):
    S = h_a.shape[0]
    C, K, HB = CONV_CH, CONV_WIDTH, CONV_HALO
    RC = 128

    def body(h_ref, halo_ref, w_ref, b_ref, g_ref, bb_ref, out_ref, hc_ref, gbuf):
        i = pl.program_id(0)
        gbuf[0:HB, :] = jnp.where(i > 0, _glu(halo_ref[...]), 0.0)
        gbuf[HB:HB + ts, :] = _glu(h_ref[...])
        for r0 in range(0, ts, RC):
            acc = jnp.zeros((RC, C), F32) + b_ref[...]
            for k in range(K):
                acc = acc + w_ref[k:k + 1, :] * gbuf[pl.ds(r0 + HB - (K - 1) + k, RC), :]
            hc_ref[r0:r0 + RC, :] = acc
            xhat, _ = _ln_rows(acc)
            hn = xhat * g_ref[...] + bb_ref[...]
            out_ref[r0:r0 + RC, :] = (hn * jax.nn.sigmoid(hn)).astype(BF16)

    nb = ts // HB
    vec = pl.BlockSpec((1, C), lambda i: (0, 0))
    return _call(body, grid=(S // ts,),
                 in_specs=[pl.BlockSpec((ts, 2 * C), lambda i: (i, 0)),
                           pl.BlockSpec((HB, 2 * C), lambda i: (jnp.maximum(i * nb - 1, 0), 0)),
                           pl.BlockSpec((K, C), lambda i: (0, 0)), vec, vec, vec],
                 out_specs=(pl.BlockSpec((ts, C), lambda i: (i, 0)), pl.BlockSpec((ts, C), lambda i: (i, 0))),
                 out_shape=(_sds((S, C), BF16), _sds((S, C), F32)),
                 scratch_shapes=[pltpu.VMEM((HB + ts, C), F32)], name=name)(
        h_a, h_a, dw_w, dw_b.reshape(1, C), ln_g.reshape(1, C), ln_b.reshape(1, C))


def _conv_bwd(h_a, hc, dout, dw_w, ln_g, ln_b, *, name, ts):
    S = h_a.shape[0]
    C, K, HB = CONV_CH, CONV_WIDTH, CONV_HALO
    RC = 128
    n = S // ts

    def dconv_out(hc_v, do_v, g_ref, bb_ref):
        xhat, rstd = _ln_rows(hc_v)
        hn = xhat * g_ref[...] + bb_ref[...]
        sg = jax.nn.sigmoid(hn)
        dhn = do_v * (sg * (1.0 + hn * (1.0 - sg)))
        return _ln_bwd_rows(dhn * g_ref[...], xhat, rstd), dhn, xhat

    def body(h_ref, hprev_ref, hc_ref, hcnext_ref, do_ref, donext_ref, w_ref, g_ref, bb_ref,
             dh_ref, dw_ref, dwb_ref, dg_ref, db_ref, gbuf, dbuf):
        i = pl.program_id(0)
        hv = h_ref[...]
        gbuf[0:HB, :] = jnp.where(i > 0, _glu(hprev_ref[...]), 0.0)
        gbuf[HB:HB + ts, :] = _glu(hv)
        dhc, dhn, xhat = dconv_out(hc_ref[...], do_ref[...], g_ref, bb_ref)
        dhc_next, _, _ = dconv_out(hcnext_ref[...], donext_ref[...], g_ref, bb_ref)
        dbuf[0:ts, :] = dhc
        dbuf[ts:ts + HB, :] = jnp.where(i < n - 1, dhc_next, 0.0)
        dw_rows = []
        for k in range(K):
            acc_k = jnp.zeros((1, C), F32)
            for r0 in range(0, ts, RC):
                acc_k = acc_k + _colsum(dbuf[r0:r0 + RC, :] * gbuf[pl.ds(r0 + HB - (K - 1) + k, RC), :])
            dw_rows.append(acc_k)
        dw_rows.append(jnp.zeros((1, C), F32))
        dw_tile = jnp.concatenate(dw_rows, axis=0)
        for r0 in range(0, ts, RC):
            acc = jnp.zeros((RC, C), F32)
            for k in range(K):
                acc = acc + w_ref[k:k + 1, :] * dbuf[pl.ds(r0 + (K - 1) - k, RC), :]
            a = hv[r0:r0 + RC, :C]
            sg = jax.nn.sigmoid(hv[r0:r0 + RC, C:])
            dh_ref[r0:r0 + RC, :C] = (acc * sg).astype(BF16)
            dh_ref[r0:r0 + RC, C:] = (acc * a * sg * (1.0 - sg)).astype(BF16)
        parts = [(dw_ref, dw_tile), (dwb_ref, _colsum(dhc)), (dg_ref, _colsum(dhn * xhat)), (db_ref, _colsum(dhn))]

        @pl.when(i == 0)
        def _():
            for r, v in parts:
                r[...] = v

        @pl.when(i > 0)
        def _():
            for r, v in parts:
                r[...] += v

    nb = ts // HB
    last = S // HB - 1
    vec = pl.BlockSpec((1, C), lambda i: (0, 0))
    nxt = lambda i: (jnp.minimum((i + 1) * nb, last), 0)
    return _call(body, grid=(n,),
                 in_specs=[pl.BlockSpec((ts, 2 * C), lambda i: (i, 0)),
                           pl.BlockSpec((HB, 2 * C), lambda i: (jnp.maximum(i * nb - 1, 0), 0)),
                           pl.BlockSpec((ts, C), lambda i: (i, 0)), pl.BlockSpec((HB, C), nxt),
                           pl.BlockSpec((ts, C), lambda i: (i, 0)), pl.BlockSpec((HB, C), nxt),
                           pl.BlockSpec((K, C), lambda i: (0, 0)), vec, vec],
                 out_specs=(pl.BlockSpec((ts, 2 * C), lambda i: (i, 0)), pl.BlockSpec((K + 1, C), lambda i: (0, 0)),
                            vec, vec, vec),
                 out_shape=(_sds((S, 2 * C), BF16), _sds((K + 1, C), F32), _sds((1, C), F32), _sds((1, C), F32),
                            _sds((1, C), F32)),
                 scratch_shapes=[pltpu.VMEM((HB + ts, C), F32), pltpu.VMEM((ts + HB, C), F32)], name=name)(
        h_a, h_a, hc, hc, dout, dout, dw_w, ln_g.reshape(1, C), ln_b.reshape(1, C))


def _gmlp_mix(vn_bf, w_ref, mix_buf, ts):
    for ch in range(ts // CHUNK):
        for g in range(GMLP_GROUPS):
            vg = vn_bf[ch * CHUNK:(ch + 1) * CHUNK, g * GMLP_GROUP_DIM:(g + 1) * GMLP_GROUP_DIM]
            mix_buf[ch * CHUNK:(ch + 1) * CHUNK, g * GMLP_GROUP_DIM:(g + 1) * GMLP_GROUP_DIM] = jnp.dot(
                w_ref[g], vg, preferred_element_type=F32)


def _gmlp_fwd(h_c, ln_g, ln_b, w_tril, bs_rows, *, name, ts):
    S = h_c.shape[0]
    C = GMLP_CH

    def body(h_ref, g_ref, b_ref, w_ref, bs_ref, out_ref, mix_buf):
        hv = h_ref[...]
        xhat, _ = _ln_rows(hv[:, C:])
        vn = (xhat * g_ref[...] + b_ref[...]).astype(BF16)
        _gmlp_mix(vn, w_ref, mix_buf, ts)
        for ch in range(ts // CHUNK):
            rows = slice(ch * CHUNK, (ch + 1) * CHUNK)
            out_ref[rows, :] = (hv[rows, :C] * (mix_buf[rows, :] + bs_ref[...])).astype(BF16)

    vec = pl.BlockSpec((1, C), lambda i: (0, 0))
    return _call(body, grid=(S // ts,),
                 in_specs=[pl.BlockSpec((ts, 2 * C), lambda i: (i, 0)), vec, vec,
                           pl.BlockSpec((GMLP_GROUPS, CHUNK, CHUNK), lambda i: (0, 0, 0)),
                           pl.BlockSpec((CHUNK, C), lambda i: (0, 0))],
                 out_specs=pl.BlockSpec((ts, C), lambda i: (i, 0)), out_shape=_sds((S, C), BF16),
                 scratch_shapes=[pltpu.VMEM((ts, C), F32)], name=name)(
        h_c, ln_g.reshape(1, C), ln_b.reshape(1, C), w_tril, bs_rows)


def _gmlp_bwd(h_c, dout, ln_g, ln_b, w_tril, w_tril_t, bs_rows, *, name, ts):
    S = h_c.shape[0]
    C, G, GD = GMLP_CH, GMLP_GROUPS, GMLP_GROUP_DIM

    def body(h_ref, do_ref, g_ref, b_ref, w_ref, wt_ref, bs_ref, dh_ref, dw_ref, dbs_ref, dg_ref, db_ref,
             mix_buf, dvn_buf):
        i = pl.program_id(0)
        hv = h_ref[...]
        u = hv[:, :C]
        xhat, rstd = _ln_rows(hv[:, C:])
        vn = (xhat * g_ref[...] + b_ref[...]).astype(BF16)
        _gmlp_mix(vn, w_ref, mix_buf, ts)
        do = do_ref[...]
        dmixed = do * u
        dm_bf = dmixed.astype(BF16)
        lane = lax.broadcasted_iota(jnp.int32, (CHUNK, LANES), 1)
        dbs = jnp.zeros((CHUNK, LANES), F32)
        dws = [jnp.zeros((CHUNK, CHUNK), F32) for _ in range(G)]
        for ch in range(ts // CHUNK):
            rows = slice(ch * CHUNK, (ch + 1) * CHUNK)
            dh_ref[rows, :C] = (do[rows, :] * (mix_buf[rows, :] + bs_ref[...])).astype(BF16)
            for g in range(G):
                cols = slice(g * GD, (g + 1) * GD)
                dmg = dm_bf[rows, cols]
                dvn_buf[rows, cols] = jnp.dot(wt_ref[g], dmg, preferred_element_type=F32)
                dws[g] = dws[g] + lax.dot_general(dmg, vn[rows, cols], (((1,), (1,)), ((), ())),
                                                  preferred_element_type=F32)
                rs = jnp.sum(dmixed[rows, cols], axis=1, keepdims=True)
                dbs = dbs + jnp.where(lane == g, rs, 0.0)
        dvn = dvn_buf[...]
        dh_ref[:, C:] = _ln_bwd_rows(dvn * g_ref[...], xhat, rstd).astype(BF16)
        dgv, dbv = _colsum(dvn * xhat), _colsum(dvn)

        @pl.when(i == 0)
        def _():
            for g in range(G):
                dw_ref[g] = dws[g]
            dbs_ref[...] = dbs
            dg_ref[...] = dgv
            db_ref[...] = dbv

        @pl.when(i > 0)
        def _():
            for g in range(G):
                dw_ref[g] += dws[g]
            dbs_ref[...] += dbs
            dg_ref[...] += dgv
            db_ref[...] += dbv

    vec = pl.BlockSpec((1, C), lambda i: (0, 0))
    wspec = pl.BlockSpec((G, CHUNK, CHUNK), lambda i: (0, 0, 0))
    return _call(body, grid=(S // ts,),
                 in_specs=[pl.BlockSpec((ts, 2 * C), lambda i: (i, 0)),
                           pl.BlockSpec((ts, C), lambda i: (i, 0)), vec, vec, wspec, wspec,
                           pl.BlockSpec((CHUNK, C), lambda i: (0, 0))],
                 out_specs=(pl.BlockSpec((ts, 2 * C), lambda i: (i, 0)), wspec,
                            pl.BlockSpec((CHUNK, LANES), lambda i: (0, 0)), vec, vec),
                 out_shape=(_sds((S, 2 * C), BF16), _sds((G, CHUNK, CHUNK), F32), _sds((CHUNK, LANES), F32),
                            _sds((1, C), F32), _sds((1, C), F32)),
                 scratch_shapes=[pltpu.VMEM((ts, C), F32), pltpu.VMEM((ts, C), F32)], name=name)(
        h_c, dout, ln_g.reshape(1, C), ln_b.reshape(1, C), w_tril, w_tril_t, bs_rows)


def _conv3(buf, w_ref, b_ref, rows, off):
    return (w_ref[0:1, :] * buf[pl.ds(off - 2, rows), :] + w_ref[1:2, :] * buf[pl.ds(off - 1, rows), :]
            + w_ref[2:3, :] * buf[pl.ds(off, rows), :]) + b_ref[...]


def _ffn_act_fwd(hu, cw, cb, *, name, ts, tc):
    S, F2 = hu.shape
    F = F2 // 2
    nj = F // tc
    HB = FFN_HALO
    nb = ts // HB

    def body(g_ref, v_ref, gh_ref, vh_ref, wg_ref, wv_ref, bg_ref, bv_ref, act_ref, dg_ref, dv_ref, gbuf, vbuf):
        i = pl.program_id(0)
        gbuf[0:HB, :] = jnp.where(i > 0, gh_ref[...], 0.0)
        gbuf[HB:HB + ts, :] = g_ref[...]
        vbuf[0:HB, :] = jnp.where(i > 0, vh_ref[...], 0.0)
        vbuf[HB:HB + ts, :] = v_ref[...]
        for r0 in range(0, ts, FFN_ROWS):
            rows = slice(r0, r0 + FFN_ROWS)
            gc = _conv3(gbuf, wg_ref, bg_ref, FFN_ROWS, HB + r0)
            vc = _conv3(vbuf, wv_ref, bv_ref, FFN_ROWS, HB + r0)
            sg = jax.nn.sigmoid(gc)
            silu = gc * sg
            act_ref[rows, :] = (silu * vc).astype(BF16)
            dg_ref[rows, :] = (vc * (sg * (1.0 + gc * (1.0 - sg)))).astype(BF16)
            dv_ref[rows, :] = silu.astype(BF16)

    prev = lambda off: (lambda i, j: (jnp.maximum(i * nb - 1, 0), j + off))
    out = pl.BlockSpec((ts, tc), lambda i, j: (i, j))
    return _call(body, grid=(S // ts, nj),
                 in_specs=[pl.BlockSpec((ts, tc), lambda i, j: (i, j)), pl.BlockSpec((ts, tc), lambda i, j: (i, j + nj)),
                           pl.BlockSpec((HB, tc), prev(0)), pl.BlockSpec((HB, tc), prev(nj)),
                           pl.BlockSpec((3, tc), lambda i, j: (0, j)), pl.BlockSpec((3, tc), lambda i, j: (0, j + nj)),
                           pl.BlockSpec((1, tc), lambda i, j: (0, j)), pl.BlockSpec((1, tc), lambda i, j: (0, j + nj))],
                 out_specs=(out, out, out), out_shape=(_sds((S, F), BF16),) * 3,
                 scratch_shapes=[pltpu.VMEM((HB + ts, tc), F32), pltpu.VMEM((HB + ts, tc), F32)], name=name)(
        hu, hu, hu, hu, cw, cw, cb.reshape(1, F2), cb.reshape(1, F2))


def _ffn_act_bwd(hu, dact, dact_dg, dact_dv, cw, *, name, ts, tc):
    S, F2 = hu.shape
    F = F2 // 2
    nj = F // tc
    HB = FFN_HALO
    HB16 = 16
    n = S // ts

    def body(g_ref, v_ref, lg_ref, lv_ref, lgn_ref, lvn_ref, da_ref, dan_ref, wg_ref, wv_ref,
             dhg_ref, dhv_ref, dwg_ref, dwv_ref, dbg_ref, dbv_ref, dug_ref, duv_ref, dgb, dvb, accg, accv):
        i = pl.program_id(1)
        RC = FFN_ROWS
        for r0 in range(0, ts, RC):
            rows = slice(r0, r0 + RC)
            da = da_ref[rows, :]
            dgb[rows, :] = da * lg_ref[rows, :].astype(F32)
            dvb[rows, :] = da * lv_ref[rows, :].astype(F32)
        dan = jnp.where(i < n - 1, dan_ref[...], 0.0)
        dgb[ts:ts + HB, :] = dan * lgn_ref[...].astype(F32)[0:HB, :]
        dvb[ts:ts + HB, :] = dan * lvn_ref[...].astype(F32)[0:HB, :]

        @pl.when(i == 0)
        def _():
            accg[...] = jnp.zeros_like(accg)
            accv[...] = jnp.zeros_like(accv)

        def fold(v):
            out = v[0:8, :]
            for r in range(8, RC, 8):
                out = out + v[r:r + 8, :]
            return out

        for dbuf, h_ref, w_ref, dh_ref, acc in ((dgb, g_ref, wg_ref, dhg_ref, accg), (dvb, v_ref, wv_ref, dhv_ref, accv)):
            for r0 in range(0, ts, RC):
                taps = [dbuf[r0 + 2 - k:r0 + 2 - k + RC, :] for k in range(3)]
                dhu = w_ref[2:3, :] * taps[2] + w_ref[1:2, :] * taps[1] + w_ref[0:1, :] * taps[0]
                dh_ref[r0:r0 + RC, :] = dhu.astype(BF16)
                h = h_ref[r0:r0 + RC, :]
                for k in range(3):
                    acc[8 * k:8 * k + 8, :] += fold(h * taps[k])
                acc[24:32, :] += fold(taps[2])
                acc[32:40, :] += fold(dhu)

        @pl.when(i == n - 1)
        def _():
            for acc, dw_ref, db_ref, du_ref in ((accg, dwg_ref, dbg_ref, dug_ref), (accv, dwv_ref, dbv_ref, duv_ref)):
                for k in range(3):
                    dw_ref[k:k + 1, :] = _colsum(acc[8 * k:8 * k + 8, :])
                db_ref[...] = _colsum(acc[24:32, :])
                du_ref[...] = _colsum(acc[32:40, :])

    nxt = lambda hb: (lambda j, i: (jnp.minimum((i + 1) * (ts // hb), S // hb - 1), j))
    tile = lambda off: (lambda j, i: (i, j + off))
    vec = lambda rows: pl.BlockSpec((rows, tc), lambda j, i: (0, j))
    return _call(body, grid=(nj, n),
                 in_specs=[pl.BlockSpec((ts, tc), tile(0)), pl.BlockSpec((ts, tc), tile(nj)),
                           pl.BlockSpec((ts, tc), tile(0)), pl.BlockSpec((ts, tc), tile(0)),
                           pl.BlockSpec((HB16, tc), nxt(HB16)), pl.BlockSpec((HB16, tc), nxt(HB16)),
                           pl.BlockSpec((ts, tc), tile(0)), pl.BlockSpec((HB, tc), nxt(HB)),
                           pl.BlockSpec((3, tc), lambda j, i: (0, j)), pl.BlockSpec((3, tc), lambda j, i: (0, j + nj))],
                 out_specs=(pl.BlockSpec((ts, tc), tile(0)), pl.BlockSpec((ts, tc), tile(0)),
                            vec(3), vec(3), vec(1), vec(1), vec(1), vec(1)),
                 out_shape=(_sds((S, F), BF16), _sds((S, F), BF16), _sds((3, F), F32), _sds((3, F), F32),
                            _sds((1, F), F32), _sds((1, F), F32), _sds((1, F), F32), _sds((1, F), F32)),
                 scratch_shapes=[pltpu.VMEM((ts + HB, tc), F32), pltpu.VMEM((ts + HB, tc), F32),
                                 pltpu.VMEM((40, tc), F32), pltpu.VMEM((40, tc), F32)], name=name)(
        hu, hu, dact_dg, dact_dv, dact_dg, dact_dv, dact, dact, cw, cw)


def _t5_bucket(dist):
    max_exact = N_BUCKETS // 2
    d = np.maximum(dist, 1).astype(np.float64)
    large = max_exact + (np.log(d / max_exact) / math.log(MAX_DISTANCE / max_exact)
                         * (N_BUCKETS - max_exact)).astype(np.int32)
    large = np.minimum(large, N_BUCKETS - 1)
    return np.where(dist < max_exact, dist, large).astype(np.int32)


def _pattern_tables(window, dilation):
    qi = np.arange(ATTN_BLOCK)[:, None]
    kj = np.arange(2 * ATTN_BLOCK)[None, :]
    dist = qi + ATTN_BLOCK - kj
    valid = (dist >= 0) & (dist <= window // dilation)
    bucket = _t5_bucket(np.clip(dist, 0, None) * dilation)
    return bucket, valid


def _dilate_qkv(qkv, *, name, ts):
    S, C = qkv.shape
    dils = [d for (_, d) in PATTERNS if d > 1]

    def body(x_ref, nat_ref, *rest):
        outs, tmp = rest[:-1], rest[-1]
        nat_ref[...] = x_ref[...].astype(BF16)
        for j in range(C // LANES):
            cols = slice(j * LANES, (j + 1) * LANES)
            tmp[...] = x_ref[:, cols]
            for d, o_ref in zip(dils, outs):
                _dilate(tmp, o_ref, cols, d, ts, BF16)

    out_shape = (_sds((S, C), BF16),) + tuple(_sds((d, S // d, C), BF16) for d in dils)
    out_specs = (_dil_spec(1, ts, C),) + tuple(_dil_spec(d, ts, C) for d in dils)
    res = _call(body, grid=(S // ts,), in_specs=[_dil_spec(1, ts, C)], out_specs=out_specs, out_shape=out_shape,
                scratch_shapes=[pltpu.VMEM((ts, LANES), F32)], name=name)(qkv)
    return [res[0]] + [r.reshape(S, C) for r in res[1:]]


def _attn_group(S):
    nb_min = (S // PATTERNS[-1][1]) // ATTN_BLOCK
    return math.gcd(8, nb_min)


def _attn_fwd(qkv, bias, *, name, nb, G):
    S = qkv.shape[0]
    B, HD = ATTN_BLOCK, HEAD_DIM
    GR = G * B
    ng = S // GR

    def body(q_ref, k_ref, kh_ref, v_ref, vh_ref, bias_ref, o_ref, lse_ref, kbuf, vbuf):
        g = pl.program_id(1)
        halo_ok = (g * G) % nb != 0
        kbuf[0:B, :] = kh_ref[...]
        kbuf[B:B + GR, :] = k_ref[...]
        vbuf[0:B, :] = vh_ref[...]
        vbuf[B:B + GR, :] = v_ref[...]
        col = lax.broadcasted_iota(jnp.int32, (B, 2 * B), 1)

        for bi in range(G):
            r0 = bi * B
            q2 = q_ref[r0:r0 + B, :]
            kk = kbuf[r0:r0 + 2 * B, :]
            vv = vbuf[r0:r0 + 2 * B, :]
            for hh in range(2):
                cs = slice(hh * HD, (hh + 1) * HD)
                s = lax.dot_general(q2[:, cs], kk[:, cs], (((1,), (1,)), ((), ())), preferred_element_type=F32)
                s = s + bias_ref[hh]
                if bi == 0:
                    s = jnp.where(jnp.logical_and(jnp.logical_not(halo_ok), col < B), NEG, s)
                m = jnp.max(s, axis=1, keepdims=True)
                p = jnp.exp(s - m)
                l = jnp.sum(p, axis=1, keepdims=True)
                o = jnp.dot(p.astype(BF16), vv[:, cs], preferred_element_type=F32) / l
                o_ref[r0:r0 + B, cs] = o
                lse_ref[r0:r0 + B, cs] = jnp.broadcast_to(m + jnp.log(l), (B, HD))

    halo = lambda off: (lambda hp, g: (jnp.maximum(g * G - 1, 0), off + hp))
    main = lambda off: (lambda hp, g: (g, off + hp))
    return _call(body, grid=(4, ng),
                 in_specs=[pl.BlockSpec((GR, LANES), main(0)), pl.BlockSpec((GR, LANES), main(4)),
                           pl.BlockSpec((B, LANES), halo(4)), pl.BlockSpec((GR, LANES), main(8)),
                           pl.BlockSpec((B, LANES), halo(8)), pl.BlockSpec((2, B, 2 * B), lambda hp, g: (hp, 0, 0))],
                 out_specs=(pl.BlockSpec((GR, LANES), main(0)), pl.BlockSpec((GR, LANES), main(0))),
                 out_shape=(_sds((S, ATTN_CH), F32), _sds((S, ATTN_CH), F32)),
                 scratch_shapes=[pltpu.VMEM((B + GR, LANES), BF16), pltpu.VMEM((B + GR, LANES), BF16)], name=name)(
        qkv, qkv, qkv, qkv, qkv, bias)


def _dil_spec(d, ts, C):
    if d == 1:
        return pl.BlockSpec((ts, C), lambda i: (i, 0))
    return pl.BlockSpec((d, ts // d, C), lambda i: (0, i, 0))


def _dil_view(a, d):
    return a if d == 1 else a.reshape(d, a.shape[0] // d, a.shape[1])


def _dilate(nat_tmp, dst_ref, cols, d, ts, dtype=F32):
    for r in range(d):
        dst_ref[r, :, cols] = nat_tmp[pl.ds(r, ts // d, stride=d), :].astype(dtype)


def _undilate(src_ref, nat_tmp, cols, d, ts, accumulate=False):
    for r in range(d):
        rows = pl.ds(r, ts // d, stride=d)
        if accumulate:
            nat_tmp[rows, :] = nat_tmp[rows, :] + src_ref[r, :, cols]
        else:
            nat_tmp[rows, :] = src_ref[r, :, cols]


def _attn_merge(o_list, lse_list, dils, *, name, ts):
    S, C = o_list[0].shape
    P = len(o_list)
    nd = sum(1 for d in dils if d > 1)

    def body(*refs):
        o_refs, l_refs = refs[:P], refs[P:2 * P]
        out_ref, lse_ref = refs[2 * P], refs[2 * P + 1]
        lse_d_refs = refs[2 * P + 2:2 * P + 2 + nd]
        scratch = list(refs[2 * P + 2 + nd:])
        tmp = scratch.pop()
        for j in range(C // LANES):
            cols = slice(j * LANES, (j + 1) * LANES)
            os_, ls, free = [], [], list(scratch)
            for o_ref, l_ref, d in zip(o_refs, l_refs, dils):
                if d > 1:
                    so, sl = free.pop(0), free.pop(0)
                    _undilate(o_ref, so, cols, d, ts)
                    _undilate(l_ref, sl, cols, d, ts)
                    os_.append(so[...])
                    ls.append(sl[...])
                else:
                    os_.append(o_ref[:, cols])
                    ls.append(l_ref[:, cols])
            m = ls[0]
            for l in ls[1:]:
                m = jnp.maximum(m, l)
            ws = [jnp.exp(l - m) for l in ls]
            den = ws[0]
            for w in ws[1:]:
                den = den + w
            num = ws[0] * os_[0]
            for w, o in zip(ws[1:], os_[1:]):
                num = num + w * o
            out_ref[:, cols] = num / den
            tmp[...] = m + jnp.log(den)
            lse_ref[:, cols] = tmp[...]
            for l_out, d in zip(lse_d_refs, [d for d in dils if d > 1]):
                _dilate(tmp, l_out, cols, d, ts)

    row = _dil_spec(1, ts, C)
    dd = [d for d in dils if d > 1]
    res = _call(body, grid=(S // ts,), in_specs=[_dil_spec(d, ts, C) for d in dils] * 2,
                out_specs=(row, row) + tuple(_dil_spec(d, ts, C) for d in dd),
                out_shape=(_sds((S, C), F32), _sds((S, C), F32)) + tuple(_sds((d, S // d, C), F32) for d in dd),
                scratch_shapes=[pltpu.VMEM((ts, LANES), F32)] * (2 * nd + 1), name=name)(
        *[_dil_view(o, d) for o, d in zip(o_list, dils)], *[_dil_view(l, d) for l, d in zip(lse_list, dils)])
    lse_by_d = {1: res[1]}
    lse_by_d.update({d: r.reshape(S, C) for d, r in zip(dd, res[2:])})
    return res[0], [lse_by_d[d] for d in dils]


def _attn_prep(dout, out, dils, *, name, ts):
    S, C = out.shape
    HD = HEAD_DIM
    dd = [d for d in dils if d > 1]
    nd = len(dd)

    def body(do_ref, o_ref, d_ref, dob_ref, *rest):
        outs, tmp_d, tmp_o = rest[:2 * nd], rest[2 * nd], rest[2 * nd + 1]
        dob_ref[...] = do_ref[...].astype(BF16)
        for j in range(C // LANES):
            cols = slice(j * LANES, (j + 1) * LANES)
            do = do_ref[:, cols]
            prod = do * o_ref[:, cols]
            tmp_o[...] = do
            for h in range(LANES // HD):
                cs = slice(h * HD, (h + 1) * HD)
                tmp_d[:, cs] = jnp.broadcast_to(jnp.sum(prod[:, cs], axis=1, keepdims=True), (ts, HD))
            d_ref[:, cols] = tmp_d[...]
            for d, dd_out, do_out in zip(dd, outs[:nd], outs[nd:]):
                _dilate(tmp_d, dd_out, cols, d, ts)
                _dilate(tmp_o, do_out, cols, d, ts, BF16)

    row = _dil_spec(1, ts, C)
    res = _call(body, grid=(S // ts,), in_specs=[row, row],
                out_specs=(row, row) + tuple(_dil_spec(d, ts, C) for d in dd) * 2,
                out_shape=(_sds((S, C), F32), _sds((S, C), BF16)) + tuple(_sds((d, S // d, C), F32) for d in dd)
                + tuple(_sds((d, S // d, C), BF16) for d in dd),
                scratch_shapes=[pltpu.VMEM((ts, LANES), F32)] * 2, name=name)(dout, out)
    dd_by_d, do_by_d = {1: res[0]}, {1: res[1]}
    dd_by_d.update({d: r.reshape(S, C) for d, r in zip(dd, res[2:2 + nd])})
    do_by_d.update({d: r.reshape(S, C) for d, r in zip(dd, res[2 + nd:])})
    return [dd_by_d[d] for d in dils], [do_by_d[d] for d in dils]


def _attn_bwd(qkv, do, lse, dd, bias, *, name, nb, G):
    S = qkv.shape[0]
    B, HD = ATTN_BLOCK, HEAD_DIM
    GR = G * B
    ng = S // GR
    nblk = S // B

    def body(q_ref, k_ref, kh_ref, v_ref, vh_ref, do_ref, lse_ref, dd_ref, qn_ref, don_ref, lsen_ref, ddn_ref,
             bias_ref, dq_ref, dk_ref, dv_ref, dbias_ref, kbuf, vbuf, dkp, dvp):
        g = pl.program_id(1)
        halo_ok = (g * G) % nb != 0
        next_ok = jnp.logical_and(((g + 1) * G) % nb != 0, g < ng - 1)
        kbuf[0:B, :] = kh_ref[...]
        kbuf[B:B + GR, :] = k_ref[...]
        vbuf[0:B, :] = vh_ref[...]
        vbuf[B:B + GR, :] = v_ref[...]
        col = lax.broadcasted_iota(jnp.int32, (B, 2 * B), 1)

        def tn_dot(a, b):
            return lax.dot_general(a, b, (((0,), (0,)), ((), ())), preferred_element_type=F32)

        @pl.when(g == 0)
        def _():
            dbias_ref[...] = jnp.zeros_like(dbias_ref)

        for bi in range(G):
            r0 = bi * B
            q2 = q_ref[r0:r0 + B, :]
            do2 = do_ref[r0:r0 + B, :]
            kk = kbuf[r0:r0 + 2 * B, :]
            vv = vbuf[r0:r0 + 2 * B, :]
            for hh in range(2):
                cs = slice(hh * HD, (hh + 1) * HD)
                s = lax.dot_general(q2[:, cs], kk[:, cs], (((1,), (1,)), ((), ())), preferred_element_type=F32)
                s = s + bias_ref[hh]
                if bi == 0:
                    s = jnp.where(jnp.logical_and(jnp.logical_not(halo_ok), col < B), NEG, s)
                p = jnp.exp(s - lse_ref[r0:r0 + B, hh * HD:hh * HD + 1])
                dp = lax.dot_general(do2[:, cs], vv[:, cs], (((1,), (1,)), ((), ())), preferred_element_type=F32)
                ds = p * (dp - dd_ref[r0:r0 + B, hh * HD:hh * HD + 1])
                dbias_ref[hh] += ds
                ds_bf, p_bf = ds.astype(BF16), p.astype(BF16)
                dq_ref[r0:r0 + B, cs] = jnp.dot(ds_bf, kk[:, cs], preferred_element_type=F32)
                dk_ref[r0:r0 + B, cs] = tn_dot(ds_bf[:, B:], q2[:, cs])
                dv_ref[r0:r0 + B, cs] = tn_dot(p_bf[:, B:], do2[:, cs])
                if bi > 0:
                    dkp[r0 - B:r0, cs] = tn_dot(ds_bf[:, :B], q2[:, cs])
                    dvp[r0 - B:r0, cs] = tn_dot(p_bf[:, :B], do2[:, cs])

        @pl.when(next_ok)
        def _():
            qn = qn_ref[...]
            don = don_ref[...]
            kl = kbuf[GR:GR + B, :]
            vl = vbuf[GR:GR + B, :]
            for hh in range(2):
                cs = slice(hh * HD, (hh + 1) * HD)
                s = lax.dot_general(qn[:, cs], kl[:, cs], (((1,), (1,)), ((), ())), preferred_element_type=F32)
                s = s + bias_ref[hh, :, 0:B]
                p = jnp.exp(s - lsen_ref[:, hh * HD:hh * HD + 1])
                dp = lax.dot_general(don[:, cs], vl[:, cs], (((1,), (1,)), ((), ())), preferred_element_type=F32)
                ds = p * (dp - ddn_ref[:, hh * HD:hh * HD + 1])
                dkp[GR - B:GR, cs] = tn_dot(ds.astype(BF16), qn[:, cs])
                dvp[GR - B:GR, cs] = tn_dot(p.astype(BF16), don[:, cs])

        @pl.when(jnp.logical_not(next_ok))
        def _():
            dkp[GR - B:GR, :] = jnp.zeros((B, LANES), F32)
            dvp[GR - B:GR, :] = jnp.zeros((B, LANES), F32)

        dk_ref[...] += dkp[...]
        dv_ref[...] += dvp[...]

    halo = lambda off: (lambda hp, g: (jnp.maximum(g * G - 1, 0), off + hp))
    main = lambda off: (lambda hp, g: (g, off + hp))
    nxt = lambda off: (lambda hp, g: (jnp.minimum((g + 1) * G, nblk - 1), off + hp))
    big, small = (lambda m: pl.BlockSpec((GR, LANES), m)), (lambda m: pl.BlockSpec((B, LANES), m))
    return _call(body, grid=(4, ng),
                 in_specs=[big(main(0)), big(main(4)), small(halo(4)), big(main(8)), small(halo(8)),
                           big(main(0)), big(main(0)), big(main(0)),
                           small(nxt(0)), small(nxt(0)), small(nxt(0)), small(nxt(0)),
                           pl.BlockSpec((2, B, 2 * B), lambda hp, g: (hp, 0, 0))],
                 out_specs=(big(main(0)), big(main(0)), big(main(0)),
                            pl.BlockSpec((2, B, 2 * B), lambda hp, g: (hp, 0, 0))),
                 out_shape=(_sds((S, ATTN_CH), F32),) * 3 + (_sds((ATTN_HEADS, B, 2 * B), F32),),
                 scratch_shapes=[pltpu.VMEM((B + GR, LANES), BF16), pltpu.VMEM((B + GR, LANES), BF16),
                                 pltpu.VMEM((GR, LANES), F32), pltpu.VMEM((GR, LANES), F32)], name=name)(
        qkv, qkv, qkv, qkv, qkv, do, lse, dd, qkv, do, lse, dd, bias)


def _attn_combine(dq_list, dk_list, dv_list, dils, *, name, ts):
    S, C = dq_list[0].shape
    P = len(dq_list)
    scale = HEAD_DIM ** -0.5
    assert dils[0] == 1

    def body(*refs):
        out_refs, acc = refs[3 * P:3 * P + 3], refs[3 * P + 3]
        for part in range(3):
            for j in range(C // LANES):
                cols = slice(j * LANES, (j + 1) * LANES)
                acc[...] = refs[part * P][:, cols]
                for r, d in zip(refs[part * P + 1:(part + 1) * P], dils[1:]):
                    _undilate(r, acc, cols, d, ts, accumulate=True)
                tot = acc[...]
                if part == 0:
                    tot = tot * scale
                out_refs[part][:, cols] = tot.astype(BF16)

    return _call(body, grid=(S // ts,), in_specs=[_dil_spec(d, ts, C) for d in dils] * 3,
                 out_specs=(_dil_spec(1, ts, C),) * 3, out_shape=(_sds((S, C), BF16),) * 3,
                 scratch_shapes=[pltpu.VMEM((ts, LANES), F32)], name=name)(
        *[_dil_view(a, d) for lst in (dq_list, dk_list, dv_list) for a, d in zip(lst, dils)])


def _bias_tables(table, bucket_flat, *, name):
    P, _, K = bucket_flat.shape
    H = table.shape[1]
    KC = 4096

    def body(t_ref, bk_ref, out_ref):
        row = lax.broadcasted_iota(jnp.int32, (N_BUCKETS, KC), 0)
        for c in range(K // KC):
            bk = bk_ref[0, :, c * KC:(c + 1) * KC]
            onehot = (row == bk).astype(F32)
            vals = jnp.dot(t_ref[...], onehot, preferred_element_type=F32, precision=lax.Precision.HIGHEST)
            out_ref[0, :, c * KC:(c + 1) * KC] = jnp.where(bk >= 0, vals, NEG)

    return _call(body, grid=(P,),
                 in_specs=[pl.BlockSpec((H, N_BUCKETS), lambda p: (0, 0)), pl.BlockSpec((1, 1, K), lambda p: (p, 0, 0))],
                 out_specs=pl.BlockSpec((1, H, K), lambda p: (p, 0, 0)), out_shape=_sds((P, H, K), F32),
                 name=name)(table.T, bucket_flat)


def _bias_grad(dbias_flat, bucket_flat, *, name):
    P, H, K = dbias_flat.shape
    KC = 4096

    def body(db_ref, bk_ref, out_ref):
        p = pl.program_id(0)
        acc = jnp.zeros((N_BUCKETS, H), F32)
        row = lax.broadcasted_iota(jnp.int32, (N_BUCKETS, KC), 0)
        for c in range(K // KC):
            onehot = (row == bk_ref[0, :, c * KC:(c + 1) * KC]).astype(F32)
            acc = acc + lax.dot_general(onehot, db_ref[0, :, c * KC:(c + 1) * KC], (((1,), (1,)), ((), ())),
                                        preferred_element_type=F32, precision=lax.Precision.HIGHEST)

        @pl.when(p == 0)
        def _():
            out_ref[...] = acc

        @pl.when(p > 0)
        def _():
            out_ref[...] += acc

    return _call(body, grid=(P,),
                 in_specs=[pl.BlockSpec((1, H, K), lambda p: (p, 0, 0)), pl.BlockSpec((1, 1, K), lambda p: (p, 0, 0))],
                 out_specs=pl.BlockSpec((N_BUCKETS, H), lambda p: (0, 0)), out_shape=_sds((N_BUCKETS, H), F32),
                 name=name)(dbias_flat, bucket_flat)


def _allgather(blocks, *, name):
    n = len(blocks)

    def body(*refs):
        x_refs, out_refs = refs[:n], refs[n:2 * n]
        send_sems, recv_sems, local_sems = refs[2 * n:]
        x, y, c = lax.axis_index("x"), lax.axis_index("y"), lax.axis_index("c")
        me, sibling = (x, y, c), (x, y, 1 - c)
        chips = [(1 - x, y), (x, 1 - y), (1 - x, 1 - y)]

        def copy(i, k, blk, to, own=False):
            slot = out_refs[i].at[4 * blk[0] + 2 * blk[1] + blk[2]]
            return pltpu.make_async_remote_copy(src_ref=x_refs[i] if own else slot, dst_ref=slot,
                                                send_sem=send_sems.at[7 * i + k], recv_sem=recv_sems.at[7 * i + k],
                                                device_id=to, device_id_type=MESH)

        mine = [pltpu.make_async_copy(x_refs[i], out_refs[i].at[4 * x + 2 * y + c], local_sems.at[i])
                for i in range(n)]
        for cp in mine:
            cp.start()
        first = []
        for i in range(n):
            first.append(copy(i, 0, me, sibling, own=True))
            first += [copy(i, 1 + j, me, (*chip, c), own=True) for j, chip in enumerate(chips)]
        for cp in first:
            cp.start()
        passed = []
        for j, chip in enumerate(chips):
            for i in range(n):
                copy(i, 1 + j, (*chip, c), me).wait_recv()
                fwd = copy(i, 4 + j, (*chip, c), sibling)
                fwd.start()
                passed.append(fwd)
        for i in range(n):
            copy(i, 0, sibling, me).wait_recv()
            for j, chip in enumerate(chips):
                copy(i, 4 + j, (*chip, 1 - c), me).wait_recv()
        for cp in first + passed:
            cp.wait_send()
        for cp in mine:
            cp.wait()

    any_spec = pl.BlockSpec(memory_space=pl.ANY)
    return pl.pallas_call(body, out_shape=tuple(_sds((N_DEV,) + b.shape, b.dtype) for b in blocks),
                          in_specs=[any_spec] * n, out_specs=(any_spec,) * n,
                          scratch_shapes=[pltpu.SemaphoreType.DMA((7 * n,)), pltpu.SemaphoreType.DMA((7 * n,)),
                                          pltpu.SemaphoreType.DMA((n,))], name=name)(*blocks)


def _exchange(sends, *, name):
    n = len(sends)

    def body(*refs):
        send_refs, recv_refs = refs[:n], refs[n:2 * n]
        send_sems, recv_sems, local_sems = refs[2 * n:]
        x, y, c = lax.axis_index("x"), lax.axis_index("y"), lax.axis_index("c")
        me = 4 * x + 2 * y + c
        mine = [pltpu.make_async_copy(send_refs[i].at[me], recv_refs[i].at[me], local_sems.at[i]) for i in range(n)]
        for cp in mine:
            cp.start()
        copies = []
        for k in range(1, N_DEV):
            px = 1 - x if k & 4 else x
            py = 1 - y if k & 2 else y
            pc = 1 - c if k & 1 else c
            for i in range(n):
                cp = pltpu.make_async_remote_copy(src_ref=send_refs[i].at[4 * px + 2 * py + pc],
                                                  dst_ref=recv_refs[i].at[me],
                                                  send_sem=send_sems.at[7 * i + k - 1],
                                                  recv_sem=recv_sems.at[7 * i + k - 1],
                                                  device_id=(px, py, pc), device_id_type=MESH)
                cp.start()
                copies.append(cp)
        for cp in copies:
            cp.wait_recv()
        for cp in copies:
            cp.wait_send()
        for cp in mine:
            cp.wait()

    any_spec = pl.BlockSpec(memory_space=pl.ANY)
    return pl.pallas_call(body, out_shape=tuple(_sds(s.shape, s.dtype) for s in sends), in_specs=[any_spec] * n,
                          out_specs=(any_spec,) * n,
                          scratch_shapes=[pltpu.SemaphoreType.DMA((7 * n,)), pltpu.SemaphoreType.DMA((7 * n,)),
                                          pltpu.SemaphoreType.DMA((n,))], name=name)(*sends)


def _adamw(w, m, v, g_parts, *, name, tr):
    R, W = w.shape
    bc1 = 1.0 - ADAM_B1 ** ADAM_STEP
    bc2 = 1.0 - ADAM_B2 ** ADAM_STEP

    def body(w_ref, m_ref, v_ref, g_ref, go_ref, d_ref, mo_ref, vo_ref):
        g = g_ref[0].astype(F32)
        for i in range(1, N_DEV):
            g = g + g_ref[i].astype(F32)
        mn = ADAM_B1 * m_ref[...] + (1.0 - ADAM_B1) * g
        vn = ADAM_B2 * v_ref[...] + (1.0 - ADAM_B2) * (g * g)
        m_hat = mn / bc1
        v_hat = vn / bc2
        go_ref[...] = g
        d_ref[...] = -ADAM_LR * (m_hat / (jnp.sqrt(v_hat) + ADAM_EPS) + ADAM_WD * w_ref[...])
        mo_ref[...] = mn
        vo_ref[...] = vn

    row = pl.BlockSpec((tr, W), lambda i: (i, 0))
    return _call(body, grid=(R // tr,), in_specs=[row, row, row, pl.BlockSpec((N_DEV, tr, W), lambda i: (0, i, 0))],
                 out_specs=(row,) * 4, out_shape=(_sds((R, W), F32),) * 4, name=name)(w, m, v, g_parts)


def _round_up(n, k):
    return -(-n // k) * k


def _pack(arrs, width, row_mult):
    pieces, offs, r = [], [], 0
    for a in arrs:
        n = a.size
        rows = _round_up(n, width) // width
        flat = a.reshape(-1)
        if rows * width != n:
            flat = jnp.pad(flat, (0, rows * width - n))
        pieces.append(flat.reshape(rows, width))
        offs.append((r, rows, n))
        r += rows
    total = _round_up(r, row_mult)
    if total != r:
        pieces.append(jnp.zeros((total - r, width), pieces[0].dtype))
    return jnp.concatenate(pieces, axis=0), offs


def _unpack(pack, offs, shapes):
    out = []
    for (r, rows, n), shp in zip(offs, shapes):
        out.append(pack[r:r + rows].reshape(-1)[:n].reshape(shp))
    return out


def _gather_axis(full8, axis):
    moved = jnp.moveaxis(full8, 0, axis)
    shp = list(moved.shape)
    shp[axis:axis + 2] = [shp[axis] * shp[axis + 1]]
    return moved.reshape(shp)


def _split_axis(full, axis):
    shp = list(full.shape)
    shp[axis:axis + 1] = [N_DEV, shp[axis] // N_DEV]
    return jnp.moveaxis(full.reshape(shp), axis, 0)


def kernel(x, w_in, b_in, conv_dw_w, conv_dw_b, conv_ln_g, conv_ln_b, rel_bias_table, gmlp_ln_g, gmlp_ln_b, gmlp_w_s, gmlp_b_s, w_out, b_out, ln1_g, ln1_b, ffn_w_up, ffn_b_up, ffn_conv_w, ffn_conv_b, ffn_w_down, ffn_b_down, ln2_g, ln2_b, loss_target, m_w_in, m_b_in, m_conv_dw_w, m_conv_dw_b, m_conv_ln_g, m_conv_ln_b, m_rel_bias_table, m_gmlp_ln_g, m_gmlp_ln_b, m_gmlp_w_s, m_gmlp_b_s, m_w_out, m_b_out, m_ln1_g, m_ln1_b, m_ffn_w_up, m_ffn_b_up, m_ffn_conv_w, m_ffn_conv_b, m_ffn_w_down, m_ffn_b_down, m_ln2_g, m_ln2_b, v_w_in, v_b_in, v_conv_dw_w, v_conv_dw_b, v_conv_ln_g, v_conv_ln_b, v_rel_bias_table, v_gmlp_ln_g, v_gmlp_ln_b, v_gmlp_w_s, v_gmlp_b_s, v_w_out, v_b_out, v_ln1_g, v_ln1_b, v_ffn_w_up, v_ffn_b_up, v_ffn_conv_w, v_ffn_conv_b, v_ffn_w_down, v_ffn_b_down, v_ln2_g, v_ln2_b):
    W = dict(w_in=w_in, b_in=b_in, conv_dw_w=conv_dw_w, conv_dw_b=conv_dw_b, conv_ln_g=conv_ln_g,
             conv_ln_b=conv_ln_b, rel_bias_table=rel_bias_table, gmlp_ln_g=gmlp_ln_g, gmlp_ln_b=gmlp_ln_b,
             gmlp_w_s=gmlp_w_s, gmlp_b_s=gmlp_b_s, w_out=w_out, b_out=b_out, ln1_g=ln1_g, ln1_b=ln1_b,
             ffn_w_up=ffn_w_up, ffn_b_up=ffn_b_up, ffn_conv_w=ffn_conv_w, ffn_conv_b=ffn_conv_b,
             ffn_w_down=ffn_w_down, ffn_b_down=ffn_b_down, ln2_g=ln2_g, ln2_b=ln2_b)
    Mo = dict(w_in=m_w_in, b_in=m_b_in, conv_dw_w=m_conv_dw_w, conv_dw_b=m_conv_dw_b, conv_ln_g=m_conv_ln_g,
              conv_ln_b=m_conv_ln_b, rel_bias_table=m_rel_bias_table, gmlp_ln_g=m_gmlp_ln_g, gmlp_ln_b=m_gmlp_ln_b,
              gmlp_w_s=m_gmlp_w_s, gmlp_b_s=m_gmlp_b_s, w_out=m_w_out, b_out=m_b_out, ln1_g=m_ln1_g, ln1_b=m_ln1_b,
              ffn_w_up=m_ffn_w_up, ffn_b_up=m_ffn_b_up, ffn_conv_w=m_ffn_conv_w, ffn_conv_b=m_ffn_conv_b,
              ffn_w_down=m_ffn_w_down, ffn_b_down=m_ffn_b_down, ln2_g=m_ln2_g, ln2_b=m_ln2_b)
    Vo = dict(w_in=v_w_in, b_in=v_b_in, conv_dw_w=v_conv_dw_w, conv_dw_b=v_conv_dw_b, conv_ln_g=v_conv_ln_g,
              conv_ln_b=v_conv_ln_b, rel_bias_table=v_rel_bias_table, gmlp_ln_g=v_gmlp_ln_g, gmlp_ln_b=v_gmlp_ln_b,
              gmlp_w_s=v_gmlp_w_s, gmlp_b_s=v_gmlp_b_s, w_out=v_w_out, b_out=v_b_out, ln1_g=v_ln1_g, ln1_b=v_ln1_b,
              ffn_w_up=v_ffn_w_up, ffn_b_up=v_ffn_b_up, ffn_conv_w=v_ffn_conv_w, ffn_conv_b=v_ffn_conv_b,
              ffn_w_down=v_ffn_w_down, ffn_b_down=v_ffn_b_down, ln2_g=v_ln2_g, ln2_b=v_ln2_b)

    xs = x[0]
    target = loss_target[0]
    S, D = xs.shape
    F2 = ffn_b_up.shape[1]
    F = F2 // 2
    ts = min(512, S)
    G = _attn_group(S)
    tc = F // 2 if (F // 2) % LANES == 0 else F

    mat_names = SHARDED[:4]
    payload = [W[n].astype(BF16) if n in mat_names else W[n] for n in SHARDED]
    wall = _allgather(payload, name="weight_allgather")
    full = {n: _gather_axis(parts, SHARD_AXIS[n]) for n, parts in zip(SHARDED, wall)}

    tables = [_pattern_tables(w, d) for (w, d) in PATTERNS]
    bucket_flat = jnp.asarray(np.stack([np.where(v, b, -1).reshape(1, -1) for (b, v) in tables]).astype(np.int32))
    bias_all = _bias_tables(rel_bias_table, bucket_flat, name="bias_tables")
    biases = [bias_all[p].reshape(ATTN_HEADS, ATTN_BLOCK, 2 * ATTN_BLOCK) for p in range(len(PATTERNS))]
    nbs = [(S // d) // ATTN_BLOCK for (_, d) in PATTERNS]
    dils = [d for (_, d) in PATTERNS]
    scale = HEAD_DIM ** -0.5

    saved = []
    cur = xs
    for l in range(DEPTH):
        Win, Wout, Wup, Wdown = full['w_in'][l], full['w_out'][l], full['ffn_w_up'][l], full['ffn_w_down'][l]
        qcols = slice(2 * CONV_CH, 2 * CONV_CH + ATTN_CH)
        Win_s = Win.at[:, qcols].multiply(scale)
        b_in_s = b_in[l].at[qcols].multiply(scale)
        h_a, qkv, h_c = _mm([cur], [Win_s], bias=b_in_s, tm=ts, name="in_proj",
                            splits=((2 * CONV_CH, F32, 1.0), (3 * ATTN_CH, F32, 1.0), (2 * GMLP_CH, F32, 1.0)))
        conv_out, hc = _conv_fwd(h_a, full['conv_dw_w'][l], conv_dw_b[l], conv_ln_g[l], conv_ln_b[l],
                                 name="conv_fwd", ts=ts)
        qkv_d = _dilate_qkv(qkv, name="dilate_qkv", ts=ts)
        o_ps, lse_ps = [], []
        for p, d in enumerate(dils):
            o_p, lse_p = _attn_fwd(qkv_d[p], biases[p], name=f"attn_fwd_d{d}", nb=nbs[p], G=G)
            o_ps.append(o_p)
            lse_ps.append(lse_p)
        attn_out, lse = _attn_merge(o_ps, lse_ps, dils, name="attn_merge", ts=ts)
        w_tril = jnp.tril(gmlp_w_s[l]).astype(BF16)
        bs_rows = jnp.repeat(gmlp_b_s[l].T, GMLP_GROUP_DIM, axis=1)
        gm_out = _gmlp_fwd(h_c, gmlp_ln_g[l], gmlp_ln_b[l], w_tril, bs_rows, name="gmlp_fwd", ts=ts)
        x1, xhat1, rstd1 = _mm([conv_out, attn_out, gm_out],
                               [Wout[:CONV_CH], Wout[CONV_CH:CONV_CH + ATTN_CH], Wout[CONV_CH + ATTN_CH:]],
                               bias=b_out[l], resid=cur, resid_scale=ALPHA, ln=(ln1_g[l], ln1_b[l]), tm=ts,
                               name="out_proj_ln")
        hu = _mm([x1], [Wup], bias=ffn_b_up[l], tm=ts, tn=F, name="ffn_up")
        act, act_dg, act_dv = _ffn_act_fwd(hu, full['ffn_conv_w'][l], ffn_conv_b[l], name="ffn_act_fwd",
                                           ts=min(256, S), tc=tc)
        x2, xhat2, rstd2 = _mm([act], [Wdown], bias=ffn_b_down[l], resid=x1, resid_scale=ALPHA,
                               ln=(ln2_g[l], ln2_b[l]), tm=ts, name="ffn_down_ln")
        saved.append(dict(x0=cur, h_a=h_a, h_c=h_c, qkv_d=qkv_d, hc=hc, conv_out=conv_out, attn_out=attn_out,
                          lse=lse, gm_out=gm_out, w_tril=w_tril, bs_rows=bs_rows, x1=x1, xhat1=xhat1, rstd1=rstd1,
                          hu=hu, act=act, act_dg=act_dg, act_dv=act_dv, xhat2=xhat2, rstd2=rstd2))
        cur = x2

    grads = {n: [None] * DEPTH for n in WEIGHTS if n != 'rel_bias_table'}
    drel = None
    dx = None
    loss_part = None
    tk = min(1024, S)
    for l in reversed(range(DEPTH)):
        sv = saved[l]
        Win, Wout, Wup, Wdown = full['w_in'][l], full['w_out'][l], full['ffn_w_up'][l], full['ffn_w_down'][l]
        if dx is None:
            dz2, dg2, db2, dzs2, loss_part = _ln_bwd(sv['xhat2'], sv['rstd2'], ln2_g[l], b=ln2_b[l], target=target,
                                                     name="ln2_bwd_loss", ts=ts)
        else:
            dz2, dg2, db2, dzs2 = _ln_bwd(sv['xhat2'], sv['rstd2'], ln2_g[l], dy=dx, name="ln_bwd", ts=ts)
        grads['ln2_g'][l], grads['ln2_b'][l], grads['ffn_b_down'][l] = dg2[0], db2[0], dzs2[0]
        grads['ffn_w_down'][l] = _mm_tn(sv['act'], dz2, tm=F // 2 if (F // 2) % LANES == 0 else F, tn=D, tk=tk,
                                        name="dw_down")
        dact = _mm([dz2], [Wdown.T], tm=ts, name="dact")
        dhg, dhv, dwg, dwv, dbg, dbv, dug, duv = _ffn_act_bwd(sv['hu'], dact, sv['act_dg'], sv['act_dv'],
                                                              full['ffn_conv_w'][l], name="ffn_act_bwd",
                                                              ts=min(256, S), tc=tc)
        grads['ffn_conv_w'][l] = jnp.concatenate([dwg, dwv], axis=1)
        grads['ffn_conv_b'][l] = jnp.concatenate([dbg, dbv], axis=1)[0]
        grads['ffn_b_up'][l] = jnp.concatenate([dug, duv], axis=1)[0]
        grads['ffn_w_up'][l] = jnp.concatenate(
            [_mm_tn(sv['x1'], dhg, tm=D, tn=tc, tk=tk, name="dw_up"),
             _mm_tn(sv['x1'], dhv, tm=D, tn=tc, tk=tk, name="dw_up")], axis=1)
        WupT = Wup.T
        dx1 = _mm([dhg, dhv], [WupT[:F], WupT[F:]], resid=dz2, resid_scale=ALPHA, tm=ts, name="dx1")
        dz1, dg1, db1, dzs1 = _ln_bwd(sv['xhat1'], sv['rstd1'], ln1_g[l], dy=dx1, name="ln_bwd", ts=ts)
        grads['ln1_g'][l], grads['ln1_b'][l], grads['b_out'][l] = dg1[0], db1[0], dzs1[0]
        grads['w_out'][l] = jnp.concatenate(
            [_mm_tn(sv['conv_out'], dz1, tm=CONV_CH, tn=D, tk=tk, name="dw_out_conv"),
             _mm_tn(sv['attn_out'], dz1, tm=ATTN_CH, tn=D, tk=tk, name="dw_out_attn"),
             _mm_tn(sv['gm_out'], dz1, tm=GMLP_CH, tn=D, tk=tk, name="dw_out_conv")], axis=0)
        dc_conv, dc_attn, dc_gm = _mm([dz1], [Wout.T], tm=ts, name="dcat",
                                      splits=((CONV_CH, F32, 1.0), (ATTN_CH, F32, 1.0), (GMLP_CH, F32, 1.0)))
        dh_a, ddw, ddwb, dclg, dclb = _conv_bwd(sv['h_a'], sv['hc'], dc_conv, full['conv_dw_w'][l], conv_ln_g[l],
                                                conv_ln_b[l], name="conv_bwd", ts=ts)
        grads['conv_dw_w'][l], grads['conv_dw_b'][l] = ddw[:CONV_WIDTH], ddwb[0]
        grads['conv_ln_g'][l], grads['conv_ln_b'][l] = dclg[0], dclb[0]
        dd_d, do_d = _attn_prep(dc_attn, sv['attn_out'], dils, name="attn_prep", ts=ts)
        dqs, dks, dvs, dbs = [], [], [], []
        for p, d in enumerate(dils):
            dq, dk, dv, dbias = _attn_bwd(sv['qkv_d'][p], do_d[p], sv['lse'][p], dd_d[p], biases[p],
                                          name=f"attn_bwd_d{d}", nb=nbs[p], G=G)
            dqs.append(dq)
            dks.append(dk)
            dvs.append(dv)
            dbs.append(dbias.reshape(1, ATTN_HEADS, -1))
        dqkv = _attn_combine(dqs, dks, dvs, dils, name="attn_combine", ts=ts)
        dr = _bias_grad(jnp.concatenate(dbs, axis=0), bucket_flat, name="bias_grad")
        drel = dr if drel is None else drel + dr
        w_tril_t = jnp.swapaxes(sv['w_tril'], 1, 2)
        dh_c, dws, dbs_acc, dglg, dglb = _gmlp_bwd(sv['h_c'], dc_gm, gmlp_ln_g[l], gmlp_ln_b[l], sv['w_tril'],
                                                   w_tril_t, sv['bs_rows'], name="gmlp_bwd", ts=ts)
        grads['gmlp_w_s'][l] = jnp.tril(dws)
        grads['gmlp_b_s'][l] = dbs_acc[:, :GMLP_GROUPS].T
        grads['gmlp_ln_g'][l], grads['gmlp_ln_b'][l] = dglg[0], dglb[0]
        dwa, ca = _mm_tn(sv['x0'], dh_a, tm=D, tn=2 * CONV_CH, tk=tk, colsum=True, name="dw_in_side")
        dwq = [_mm_tn(sv['x0'], part, tm=D, tn=ATTN_CH, tk=tk, colsum=True, name="dw_in_side") for part in dqkv]
        dwc, cc = _mm_tn(sv['x0'], dh_c, tm=D, tn=2 * GMLP_CH, tk=tk, colsum=True, name="dw_in_side")
        grads['w_in'][l] = jnp.concatenate([dwa] + [w for w, _ in dwq] + [dwc], axis=1)
        grads['b_in'][l] = jnp.concatenate([ca] + [c for _, c in dwq] + [cc], axis=1)[0]
        WinT = Win.T
        edges = [0, 2 * CONV_CH] + [2 * CONV_CH + k * ATTN_CH for k in (1, 2, 3)] + [WinT.shape[0]]
        dx = _mm([dh_a, *dqkv, dh_c], [WinT[a:b] for a, b in zip(edges[:-1], edges[1:])], resid=dz1,
                 resid_scale=ALPHA, tm=ts, name="dx0")

    gfull = {n: jnp.stack(v) for n, v in grads.items()}
    gfull['rel_bias_table'] = drel

    sends = []
    for n in SHARDED:
        parts = _split_axis(gfull[n], SHARD_AXIS[n])
        sends.append(parts.reshape(N_DEV, -1, parts.shape[-1]).astype(BF16))
    recvs = _exchange(sends, name="grad_exchange")
    shard_out = [[], [], [], []]
    for n, recv in zip(SHARDED, recvs):
        shp = W[n].shape
        rows = recv.shape[1]
        tr = rows // 4 if rows % 64 == 0 else rows
        outs = _adamw(*[src[n].reshape(rows, shp[-1]) for src in (W, Mo, Vo)], recv, name=f"adamw_{n}", tr=tr)
        for kind in range(4):
            shard_out[kind].append(outs[kind].reshape(shp))

    gsmall, soffs = _pack([gfull[n] for n in SMALL], LANES, 8)
    gall = _allgather([gsmall], name="small_grad_allgather")[0]
    spacks = [_pack([src[n] for n in SMALL], LANES, 8)[0] for src in (W, Mo, Vo)]
    souts = _adamw(spacks[0], spacks[1], spacks[2], gall, name="adamw_small", tr=gsmall.shape[0])
    small_out = [_unpack(o, soffs, [W[n].shape for n in SMALL]) for o in souts]

    loss = lax.psum(loss_part[0, 0], ("x", "y", "c"))
    by_kind = []
    for kind in range(4):
        d = dict(zip(SHARDED, shard_out[kind]))
        d.update(zip(SMALL, small_out[kind]))
        by_kind.append([d[n] for n in WEIGHTS])
    return (loss, dx[None], *by_kind[0], *by_kind[1], *by_kind[2], *by_kind[3])
```

```python
import math

import numpy as np
import jax
import jax.numpy as jnp
from jax import lax
from jax.experimental import pallas as pl
from jax.experimental.pallas import tpu as pltpu

F32 = jnp.float32
BF16 = jnp.bfloat16

DEPTH = 2
HEAD_DIM = 64
CONV_CH = 256
CONV_WIDTH = 31
ATTN_HEADS = 8
ATTN_CH = ATTN_HEADS * HEAD_DIM
PATTERNS = ((128, 1), (512, 4), (2048, 16))
ATTN_BLOCK = 128
N_BUCKETS = 32
MAX_DISTANCE = 2048
GMLP_CH = 256
GMLP_GROUPS = 4
GMLP_GROUP_DIM = GMLP_CH // GMLP_GROUPS
CHUNK = 128
FFN_CONV_WIDTH = 3
LN_EPS = 1e-5
ALPHA = (2.0 * DEPTH) ** 0.25
ADAM_LR = 0.001
ADAM_B1 = 0.9
ADAM_B2 = 0.999
ADAM_EPS = 1e-08
ADAM_WD = 0.01
ADAM_STEP = 10
NEG = -1e30
N_DEV = 8
LANES = 128
CONV_HALO = 32
FFN_HALO = 8
FFN_ROWS = 16
MESH = pl.DeviceIdType.MESH

WEIGHTS = ['w_in', 'b_in', 'conv_dw_w', 'conv_dw_b', 'conv_ln_g', 'conv_ln_b', 'rel_bias_table', 'gmlp_ln_g',
           'gmlp_ln_b', 'gmlp_w_s', 'gmlp_b_s', 'w_out', 'b_out', 'ln1_g', 'ln1_b', 'ffn_w_up', 'ffn_b_up',
           'ffn_conv_w', 'ffn_conv_b', 'ffn_w_down', 'ffn_b_down', 'ln2_g', 'ln2_b']
SHARDED = ['w_in', 'w_out', 'ffn_w_up', 'ffn_w_down', 'conv_dw_w', 'ffn_conv_w']
SHARD_AXIS = {'w_in': 2, 'w_out': 1, 'ffn_w_up': 2, 'ffn_w_down': 1, 'conv_dw_w': 2, 'ffn_conv_w': 2}
SMALL = [n for n in WEIGHTS if n not in SHARDED]


def _call(body, *, grid=(), vmem_mb=48, **kw):
    params = pltpu.CompilerParams(dimension_semantics=("arbitrary",) * len(grid), vmem_limit_bytes=vmem_mb << 20)
    return pl.pallas_call(body, grid=grid, compiler_params=params, **kw)


def _sds(shape, dtype):
    return jax.ShapeDtypeStruct(shape, dtype)


def _ln_rows(z):
    mu = jnp.mean(z, axis=-1, keepdims=True)
    zc = z - mu
    var = jnp.mean(zc * zc, axis=-1, keepdims=True)
    rstd = lax.rsqrt(var + LN_EPS)
    return zc * rstd, rstd


def _ln_bwd_rows(dxhat, xhat, rstd):
    m1 = jnp.mean(dxhat, axis=-1, keepdims=True)
    m2 = jnp.mean(dxhat * xhat, axis=-1, keepdims=True)
    return rstd * (dxhat - m1 - xhat * m2)


def _colsum(v):
    return jnp.sum(v, axis=0, keepdims=True)


def _mm(a_list, w_list, *, name, tm, tn=None, bias=None, resid=None, resid_scale=1.0, ln=None, splits=None,
        out_dtype=F32):
    na = len(a_list)
    M = a_list[0].shape[0]
    N = w_list[0].shape[1]
    tn = N if tn is None else tn
    assert M % tm == 0 and N % tn == 0
    assert ln is None or tn == N
    assert splits is None or tn == N

    def body(*refs):
        a_refs, w_refs = refs[:na], refs[na:2 * na]
        pos = 2 * na
        acc = None
        for a_ref, w_ref in zip(a_refs, w_refs):
            t = jnp.dot(a_ref[...].astype(BF16), w_ref[...], preferred_element_type=F32)
            acc = t if acc is None else acc + t
        if bias is not None:
            acc = acc + refs[pos][...]
            pos += 1
        if resid is not None:
            acc = resid_scale * refs[pos][...] + acc
            pos += 1
        if ln is not None:
            g_ref, b_ref = refs[pos], refs[pos + 1]
            y_ref, xhat_ref, rstd_ref = refs[pos + 2], refs[pos + 3], refs[pos + 4]
            xhat, rstd = _ln_rows(acc)
            y_ref[...] = xhat * g_ref[...] + b_ref[...]
            xhat_ref[...] = xhat
            rstd_ref[...] = rstd
        elif splits is not None:
            c0 = 0
            for o_ref, (width, dtype, scale) in zip(refs[pos:], splits):
                part = acc[:, c0:c0 + width]
                if scale != 1.0:
                    part = part * scale
                o_ref[...] = part.astype(dtype)
                c0 += width
        else:
            refs[pos][...] = acc.astype(out_dtype)

    in_specs = [pl.BlockSpec((tm, a.shape[1]), lambda j, i: (i, 0)) for a in a_list]
    in_specs += [pl.BlockSpec((w.shape[0], tn), lambda j, i: (0, j)) for w in w_list]
    args = list(a_list) + list(w_list)
    if bias is not None:
        in_specs.append(pl.BlockSpec((1, tn), lambda j, i: (0, j)))
        args.append(bias.reshape(1, N))
    if resid is not None:
        in_specs.append(pl.BlockSpec((tm, tn), lambda j, i: (i, j)))
        args.append(resid)
    if ln is not None:
        in_specs += [pl.BlockSpec((1, N), lambda j, i: (0, 0))] * 2
        args += [ln[0].reshape(1, N), ln[1].reshape(1, N)]
        out_shape = (_sds((M, N), F32), _sds((M, N), F32), _sds((M, 1), F32))
        out_specs = (pl.BlockSpec((tm, N), lambda j, i: (i, 0)), pl.BlockSpec((tm, N), lambda j, i: (i, 0)),
                     pl.BlockSpec((tm, 1), lambda j, i: (i, 0)))
    elif splits is not None:
        out_shape = tuple(_sds((M, w), d) for (w, d, _) in splits)
        out_specs = tuple(pl.BlockSpec((tm, w), lambda j, i: (i, 0)) for (w, _, _) in splits)
    else:
        out_shape = _sds((M, N), out_dtype)
        out_specs = pl.BlockSpec((tm, tn), lambda j, i: (i, j))
    return _call(body, grid=(N // tn, M // tm), in_specs=in_specs, out_specs=out_specs, out_shape=out_shape,
                 name=name, vmem_mb=56)(*args)


def _mm_tn(a, dy, *, name, tm, tn, tk, colsum=False):
    S, Ka = a.shape
    N = dy.shape[1]
    assert S % tk == 0 and Ka % tm == 0 and N % tn == 0

    def body(a_ref, dy_ref, out_ref, *cs):
        i, k = pl.program_id(1), pl.program_id(2)
        dyb = dy_ref[...]
        part = lax.dot_general(a_ref[...].astype(BF16), dyb.astype(BF16), (((0,), (0,)), ((), ())),
                               preferred_element_type=F32)

        @pl.when(k == 0)
        def _():
            out_ref[...] = part

        @pl.when(k > 0)
        def _():
            out_ref[...] += part

        if colsum:
            cs_ref = cs[0]
            s = _colsum(dyb.astype(F32))

            @pl.when((i == 0) & (k == 0))
            def _():
                cs_ref[...] = s

            @pl.when((i == 0) & (k > 0))
            def _():
                cs_ref[...] += s

    out_shape = [_sds((Ka, N), F32)]
    out_specs = [pl.BlockSpec((tm, tn), lambda j, i, k: (i, j))]
    if colsum:
        out_shape.append(_sds((1, N), F32))
        out_specs.append(pl.BlockSpec((1, tn), lambda j, i, k: (0, j)))
    res = _call(body, grid=(N // tn, Ka // tm, S // tk),
                in_specs=[pl.BlockSpec((tk, tm), lambda j, i, k: (k, i)), pl.BlockSpec((tk, tn), lambda j, i, k: (k, j))],
                out_specs=tuple(out_specs), out_shape=tuple(out_shape), name=name, vmem_mb=56)(a, dy)
    return res if colsum else res[0]


def _ln_bwd(xhat, rstd, g, *, name, ts, dy=None, b=None, target=None):
    S, D = xhat.shape
    from_loss = target is not None

    def body(*refs):
        if from_loss:
            xhat_ref, rstd_ref, g_ref, b_ref, t_ref, dz_ref, dg_ref, db_ref, dzs_ref, loss_ref = refs
        else:
            xhat_ref, rstd_ref, g_ref, dy_ref, dz_ref, dg_ref, db_ref, dzs_ref = refs
        i = pl.program_id(0)
        xh = xhat_ref[...]
        gg = g_ref[...]
        if from_loss:
            err = xh * gg + b_ref[...] - t_ref[...]
            dyv = err * (1.0 / D)
            lsum = (0.5 / D) * jnp.sum(err * err, axis=(0, 1), keepdims=True)
        else:
            dyv = dy_ref[...]
        dz = _ln_bwd_rows(dyv * gg, xh, rstd_ref[...])
        dz_ref[...] = dz
        parts = [(dg_ref, _colsum(dyv * xh)), (db_ref, _colsum(dyv)), (dzs_ref, _colsum(dz))]
        if from_loss:
            parts.append((loss_ref, lsum))

        @pl.when(i == 0)
        def _():
            for r, v in parts:
                r[...] = v

        @pl.when(i > 0)
        def _():
            for r, v in parts:
                r[...] += v

    row = pl.BlockSpec((ts, D), lambda i: (i, 0))
    vec = pl.BlockSpec((1, D), lambda i: (0, 0))
    in_specs = [row, pl.BlockSpec((ts, 1), lambda i: (i, 0)), vec]
    args = [xhat, rstd, g.reshape(1, D)]
    if from_loss:
        in_specs += [vec, row]
        args += [b.reshape(1, D), target]
    else:
        in_specs += [row]
        args += [dy]
    out_shape = [_sds((S, D), F32), _sds((1, D), F32), _sds((1, D), F32), _sds((1, D), F32)]
    out_specs = [row, vec, vec, vec]
    if from_loss:
        out_shape.append(_sds((1, 1), F32))
        out_specs.append(pl.BlockSpec((1, 1), lambda i: (0, 0)))
    return _call(body, grid=(S // ts,), in_specs=in_specs, out_specs=tuple(out_specs), out_shape=tuple(out_shape),
                 name=name)(*args)


def _glu(v):
    return v[:, :CONV_CH] * jax.nn.sigmoid(v[:, CONV_CH:])


def _conv_fwd(h_a, dw_w, dw_b, ln_g, ln_b, *, name, ts):
    S = h_a.shape[0]
    C, K, HB = CONV_CH, CONV_WIDTH, CONV_HALO
    RC = 128

    def body(h_ref, halo_ref, w_ref, b_ref, g_ref, bb_ref, out_ref, hc_ref, gbuf):
        i = pl.program_id(0)
        gbuf[0:HB, :] = jnp.where(i > 0, _glu(halo_ref[...]), 0.0)
        gbuf[HB:HB + ts, :] = _glu(h_ref[...])
        for r0 in range(0, ts, RC):
            acc = jnp.zeros((RC, C), F32) + b_ref[...]
            for k in range(K):
                acc = acc + w_ref[k:k + 1, :] * gbuf[pl.ds(r0 + HB - (K - 1) + k, RC), :]
            hc_ref[r0:r0 + RC, :] = acc
            xhat, _ = _ln_rows(acc)
            hn = xhat * g_ref[...] + bb_ref[...]
            out_ref[r0:r0 + RC, :] = (hn * jax.nn.sigmoid(hn)).astype(BF16)

    nb = ts // HB
    vec = pl.BlockSpec((1, C), lambda i: (0, 0))
    return _call(body, grid=(S // ts,),
                 in_specs=[pl.BlockSpec((ts, 2 * C), lambda i: (i, 0)),
                           pl.BlockSpec((HB, 2 * C), lambda i: (jnp.maximum(i * nb - 1, 0), 0)),
                           pl.BlockSpec((K, C), lambda i: (0, 0)), vec, vec, vec],
                 out_specs=(pl.BlockSpec((ts, C), lambda i: (i, 0)), pl.BlockSpec((ts, C), lambda i: (i, 0))),
                 out_shape=(_sds((S, C), BF16), _sds((S, C), F32)),
                 scratch_shapes=[pltpu.VMEM((HB + ts, C), F32)], name=name)(
        h_a, h_a, dw_w, dw_b.reshape(1, C), ln_g.reshape(1, C), ln_b.reshape(1, C))


def _conv_bwd(h_a, hc, dout, dw_w, ln_g, ln_b, *, name, ts):
    S = h_a.shape[0]
    C, K, HB = CONV_CH, CONV_WIDTH, CONV_HALO
    RC = 128
    n = S // ts

    def dconv_out(hc_v, do_v, g_ref, bb_ref):
        xhat, rstd = _ln_rows(hc_v)
        hn = xhat * g_ref[...] + bb_ref[...]
        sg = jax.nn.sigmoid(hn)
        dhn = do_v * (sg * (1.0 + hn * (1.0 - sg)))
        return _ln_bwd_rows(dhn * g_ref[...], xhat, rstd), dhn, xhat

    def body(h_ref, hprev_ref, hc_ref, hcnext_ref, do_ref, donext_ref, w_ref, g_ref, bb_ref,
             dh_ref, dw_ref, dwb_ref, dg_ref, db_ref, gbuf, dbuf):
        i = pl.program_id(0)
        hv = h_ref[...]
        gbuf[0:HB, :] = jnp.where(i > 0, _glu(hprev_ref[...]), 0.0)
        gbuf[HB:HB + ts, :] = _glu(hv)
        dhc, dhn, xhat = dconv_out(hc_ref[...], do_ref[...], g_ref, bb_ref)
        dhc_next, _, _ = dconv_out(hcnext_ref[...], donext_ref[...], g_ref, bb_ref)
        dbuf[0:ts, :] = dhc
        dbuf[ts:ts + HB, :] = jnp.where(i < n - 1, dhc_next, 0.0)
        dw_rows = []
        for k in range(K):
            acc_k = jnp.zeros((1, C), F32)
            for r0 in range(0, ts, RC):
                acc_k = acc_k + _colsum(dbuf[r0:r0 + RC, :] * gbuf[pl.ds(r0 + HB - (K - 1) + k, RC), :])
            dw_rows.append(acc_k)
        dw_rows.append(jnp.zeros((1, C), F32))
        dw_tile = jnp.concatenate(dw_rows, axis=0)
        for r0 in range(0, ts, RC):
            acc = jnp.zeros((RC, C), F32)
            for k in range(K):
                acc = acc + w_ref[k:k + 1, :] * dbuf[pl.ds(r0 + (K - 1) - k, RC), :]
            a = hv[r0:r0 + RC, :C]
            sg = jax.nn.sigmoid(hv[r0:r0 + RC, C:])
            dh_ref[r0:r0 + RC, :C] = (acc * sg).astype(BF16)
            dh_ref[r0:r0 + RC, C:] = (acc * a * sg * (1.0 - sg)).astype(BF16)
        parts = [(dw_ref, dw_tile), (dwb_ref, _colsum(dhc)), (dg_ref, _colsum(dhn * xhat)), (db_ref, _colsum(dhn))]

        @pl.when(i == 0)
        def _():
            for r, v in parts:
                r[...] = v

        @pl.when(i > 0)
        def _():
            for r, v in parts:
                r[...] += v

    nb = ts // HB
    last = S // HB - 1
    vec = pl.BlockSpec((1, C), lambda i: (0, 0))
    nxt = lambda i: (jnp.minimum((i + 1) * nb, last), 0)
    return _call(body, grid=(n,),
                 in_specs=[pl.BlockSpec((ts, 2 * C), lambda i: (i, 0)),
                           pl.BlockSpec((HB, 2 * C), lambda i: (jnp.maximum(i * nb - 1, 0), 0)),
                           pl.BlockSpec((ts, C), lambda i: (i, 0)), pl.BlockSpec((HB, C), nxt),
                           pl.BlockSpec((ts, C), lambda i: (i, 0)), pl.BlockSpec((HB, C), nxt),
                           pl.BlockSpec((K, C), lambda i: (0, 0)), vec, vec],
                 out_specs=(pl.BlockSpec((ts, 2 * C), lambda i: (i, 0)), pl.BlockSpec((K + 1, C), lambda i: (0, 0)),
                            vec, vec, vec),
                 out_shape=(_sds((S, 2 * C), BF16), _sds((K + 1, C), F32), _sds((1, C), F32), _sds((1, C), F32),
                            _sds((1, C), F32)),
                 scratch_shapes=[pltpu.VMEM((HB + ts, C), F32), pltpu.VMEM((ts + HB, C), F32)], name=name)(
        h_a, h_a, hc, hc, dout, dout, dw_w, ln_g.reshape(1, C), ln_b.reshape(1, C))


def _gmlp_mix(vn_bf, w_ref, mix_buf, ts):
    for ch in range(ts // CHUNK):
        for g in range(GMLP_GROUPS):
            vg = vn_bf[ch * CHUNK:(ch + 1) * CHUNK, g * GMLP_GROUP_DIM:(g + 1) * GMLP_GROUP_DIM]
            mix_buf[ch * CHUNK:(ch + 1) * CHUNK, g * GMLP_GROUP_DIM:(g + 1) * GMLP_GROUP_DIM] = jnp.dot(
                w_ref[g], vg, preferred_element_type=F32)


def _gmlp_fwd(h_c, ln_g, ln_b, w_tril, bs_rows, *, name, ts):
    S = h_c.shape[0]
    C = GMLP_CH

    def body(h_ref, g_ref, b_ref, w_ref, bs_ref, out_ref, mix_buf):
        hv = h_ref[...]
        xhat, _ = _ln_rows(hv[:, C:])
        vn = (xhat * g_ref[...] + b_ref[...]).astype(BF16)
        _gmlp_mix(vn, w_ref, mix_buf, ts)
        for ch in range(ts // CHUNK):
            rows = slice(ch * CHUNK, (ch + 1) * CHUNK)
            out_ref[rows, :] = (hv[rows, :C] * (mix_buf[rows, :] + bs_ref[...])).astype(BF16)

    vec = pl.BlockSpec((1, C), lambda i: (0, 0))
    return _call(body, grid=(S // ts,),
                 in_specs=[pl.BlockSpec((ts, 2 * C), lambda i: (i, 0)), vec, vec,
                           pl.BlockSpec((GMLP_GROUPS, CHUNK, CHUNK), lambda i: (0, 0, 0)),
                           pl.BlockSpec((CHUNK, C), lambda i: (0, 0))],
                 out_specs=pl.BlockSpec((ts, C), lambda i: (i, 0)), out_shape=_sds((S, C), BF16),
                 scratch_shapes=[pltpu.VMEM((ts, C), F32)], name=name)(
        h_c, ln_g.reshape(1, C), ln_b.reshape(1, C), w_tril, bs_rows)


def _gmlp_bwd(h_c, dout, ln_g, ln_b, w_tril, w_tril_t, bs_rows, *, name, ts):
    S = h_c.shape[0]
    C, G, GD = GMLP_CH, GMLP_GROUPS, GMLP_GROUP_DIM

    def body(h_ref, do_ref, g_ref, b_ref, w_ref, wt_ref, bs_ref, dh_ref, dw_ref, dbs_ref, dg_ref, db_ref,
             mix_buf, dvn_buf):
        i = pl.program_id(0)
        hv = h_ref[...]
        u = hv[:, :C]
        xhat, rstd = _ln_rows(hv[:, C:])
        vn = (xhat * g_ref[...] + b_ref[...]).astype(BF16)
        _gmlp_mix(vn, w_ref, mix_buf, ts)
        do = do_ref[...]
        dmixed = do * u
        dm_bf = dmixed.astype(BF16)
        lane = lax.broadcasted_iota(jnp.int32, (CHUNK, LANES), 1)
        dbs = jnp.zeros((CHUNK, LANES), F32)
        dws = [jnp.zeros((CHUNK, CHUNK), F32) for _ in range(G)]
        for ch in range(ts // CHUNK):
            rows = slice(ch * CHUNK, (ch + 1) * CHUNK)
            dh_ref[rows, :C] = (do[rows, :] * (mix_buf[rows, :] + bs_ref[...])).astype(BF16)
            for g in range(G):
                cols = slice(g * GD, (g + 1) * GD)
                dmg = dm_bf[rows, cols]
                dvn_buf[rows, cols] = jnp.dot(wt_ref[g], dmg, preferred_element_type=F32)
                dws[g] = dws[g] + lax.dot_general(dmg, vn[rows, cols], (((1,), (1,)), ((), ())),
                                                  preferred_element_type=F32)
                rs = jnp.sum(dmixed[rows, cols], axis=1, keepdims=True)
                dbs = dbs + jnp.where(lane == g, rs, 0.0)
        dvn = dvn_buf[...]
        dh_ref[:, C:] = _ln_bwd_rows(dvn * g_ref[...], xhat, rstd).astype(BF16)
        dgv, dbv = _colsum(dvn * xhat), _colsum(dvn)

        @pl.when(i == 0)
        def _():
            for g in range(G):
                dw_ref[g] = dws[g]
            dbs_ref[...] = dbs
            dg_ref[...] = dgv
            db_ref[...] = dbv

        @pl.when(i > 0)
        def _():
            for g in range(G):
                dw_ref[g] += dws[g]
            dbs_ref[...] += dbs
            dg_ref[...] += dgv
            db_ref[...] += dbv

    vec = pl.BlockSpec((1, C), lambda i: (0, 0))
    wspec = pl.BlockSpec((G, CHUNK, CHUNK), lambda i: (0, 0, 0))
    return _call(body, grid=(S // ts,),
                 in_specs=[pl.BlockSpec((ts, 2 * C), lambda i: (i, 0)),
                           pl.BlockSpec((ts, C), lambda i: (i, 0)), vec, vec, wspec, wspec,
                           pl.BlockSpec((CHUNK, C), lambda i: (0, 0))],
                 out_specs=(pl.BlockSpec((ts, 2 * C), lambda i: (i, 0)), wspec,
                            pl.BlockSpec((CHUNK, LANES), lambda i: (0, 0)), vec, vec),
                 out_shape=(_sds((S, 2 * C), BF16), _sds((G, CHUNK, CHUNK), F32), _sds((CHUNK, LANES), F32),
                            _sds((1, C), F32), _sds((1, C), F32)),
                 scratch_shapes=[pltpu.VMEM((ts, C), F32), pltpu.VMEM((ts, C), F32)], name=name)(
        h_c, dout, ln_g.reshape(1, C), ln_b.reshape(1, C), w_tril, w_tril_t, bs_rows)


def _conv3(buf, w_ref, b_ref, rows, off):
    return (w_ref[0:1, :] * buf[pl.ds(off - 2, rows), :] + w_ref[1:2, :] * buf[pl.ds(off - 1, rows), :]
            + w_ref[2:3, :] * buf[pl.ds(off, rows), :]) + b_ref[...]


def _ffn_act_fwd(hu, cw, cb, *, name, ts, tc):
    S, F2 = hu.shape
    F = F2 // 2
    nj = F // tc
    HB = FFN_HALO
    nb = ts // HB

    def body(g_ref, v_ref, gh_ref, vh_ref, wg_ref, wv_ref, bg_ref, bv_ref, act_ref, dg_ref, dv_ref, gbuf, vbuf):
        i = pl.program_id(0)
        gbuf[0:HB, :] = jnp.where(i > 0, gh_ref[...], 0.0)
        gbuf[HB:HB + ts, :] = g_ref[...]
        vbuf[0:HB, :] = jnp.where(i > 0, vh_ref[...], 0.0)
        vbuf[HB:HB + ts, :] = v_ref[...]
        for r0 in range(0, ts, FFN_ROWS):
            rows = slice(r0, r0 + FFN_ROWS)
            gc = _conv3(gbuf, wg_ref, bg_ref, FFN_ROWS, HB + r0)
            vc = _conv3(vbuf, wv_ref, bv_ref, FFN_ROWS, HB + r0)
            sg = jax.nn.sigmoid(gc)
            silu = gc * sg
            act_ref[rows, :] = (silu * vc).astype(BF16)
            dg_ref[rows, :] = (vc * (sg * (1.0 + gc * (1.0 - sg)))).astype(BF16)
            dv_ref[rows, :] = silu.astype(BF16)

    prev = lambda off: (lambda i, j: (jnp.maximum(i * nb - 1, 0), j + off))
    out = pl.BlockSpec((ts, tc), lambda i, j: (i, j))
    return _call(body, grid=(S // ts, nj),
                 in_specs=[pl.BlockSpec((ts, tc), lambda i, j: (i, j)), pl.BlockSpec((ts, tc), lambda i, j: (i, j + nj)),
                           pl.BlockSpec((HB, tc), prev(0)), pl.BlockSpec((HB, tc), prev(nj)),
                           pl.BlockSpec((3, tc), lambda i, j: (0, j)), pl.BlockSpec((3, tc), lambda i, j: (0, j + nj)),
                           pl.BlockSpec((1, tc), lambda i, j: (0, j)), pl.BlockSpec((1, tc), lambda i, j: (0, j + nj))],
                 out_specs=(out, out, out), out_shape=(_sds((S, F), BF16),) * 3,
                 scratch_shapes=[pltpu.VMEM((HB + ts, tc), F32), pltpu.VMEM((HB + ts, tc), F32)], name=name)(
        hu, hu, hu, hu, cw, cw, cb.reshape(1, F2), cb.reshape(1, F2))


def _ffn_act_bwd(hu, dact, dact_dg, dact_dv, cw, *, name, ts, tc):
    S, F2 = hu.shape
    F = F2 // 2
    nj = F // tc
    HB = FFN_HALO
    HB16 = 16
    n = S // ts

    def body(g_ref, v_ref, lg_ref, lv_ref, lgn_ref, lvn_ref, da_ref, dan_ref, wg_ref, wv_ref,
             dhg_ref, dhv_ref, dwg_ref, dwv_ref, dbg_ref, dbv_ref, dug_ref, duv_ref, dgb, dvb, accg, accv):
        i = pl.program_id(1)
        RC = FFN_ROWS
        for r0 in range(0, ts, RC):
            rows = slice(r0, r0 + RC)
            da = da_ref[rows, :]
            dgb[rows, :] = da * lg_ref[rows, :].astype(F32)
            dvb[rows, :] = da * lv_ref[rows, :].astype(F32)
        dan = jnp.where(i < n - 1, dan_ref[...], 0.0)
        dgb[ts:ts + HB, :] = dan * lgn_ref[...].astype(F32)[0:HB, :]
        dvb[ts:ts + HB, :] = dan * lvn_ref[...].astype(F32)[0:HB, :]

        @pl.when(i == 0)
        def _():
            accg[...] = jnp.zeros_like(accg)
            accv[...] = jnp.zeros_like(accv)

        def fold(v):
            out = v[0:8, :]
            for r in range(8, RC, 8):
                out = out + v[r:r + 8, :]
            return out

        for dbuf, h_ref, w_ref, dh_ref, acc in ((dgb, g_ref, wg_ref, dhg_ref, accg), (dvb, v_ref, wv_ref, dhv_ref, accv)):
            for r0 in range(0, ts, RC):
                taps = [dbuf[r0 + 2 - k:r0 + 2 - k + RC, :] for k in range(3)]
                dhu = w_ref[2:3, :] * taps[2] + w_ref[1:2, :] * taps[1] + w_ref[0:1, :] * taps[0]
                dh_ref[r0:r0 + RC, :] = dhu.astype(BF16)
                h = h_ref[r0:r0 + RC, :]
                for k in range(3):
                    acc[8 * k:8 * k + 8, :] += fold(h * taps[k])
                acc[24:32, :] += fold(taps[2])
                acc[32:40, :] += fold(dhu)

        @pl.when(i == n - 1)
        def _():
            for acc, dw_ref, db_ref, du_ref in ((accg, dwg_ref, dbg_ref, dug_ref), (accv, dwv_ref, dbv_ref, duv_ref)):
                for k in range(3):
                    dw_ref[k:k + 1, :] = _colsum(acc[8 * k:8 * k + 8, :])
                db_ref[...] = _colsum(acc[24:32, :])
                du_ref[...] = _colsum(acc[32:40, :])

    nxt = lambda hb: (lambda j, i: (jnp.minimum((i + 1) * (ts // hb), S // hb - 1), j))
    tile = lambda off: (lambda j, i: (i, j + off))
    vec = lambda rows: pl.BlockSpec((rows, tc), lambda j, i: (0, j))
    return _call(body, grid=(nj, n),
                 in_specs=[pl.BlockSpec((ts, tc), tile(0)), pl.BlockSpec((ts, tc), tile(nj)),
                           pl.BlockSpec((ts, tc), tile(0)), pl.BlockSpec((ts, tc), tile(0)),
                           pl.BlockSpec((HB16, tc), nxt(HB16)), pl.BlockSpec((HB16, tc), nxt(HB16)),
                           pl.BlockSpec((ts, tc), tile(0)), pl.BlockSpec((HB, tc), nxt(HB)),
                           pl.BlockSpec((3, tc), lambda j, i: (0, j)), pl.BlockSpec((3, tc), lambda j, i: (0, j + nj))],
                 out_specs=(pl.BlockSpec((ts, tc), tile(0)), pl.BlockSpec((ts, tc), tile(0)),
                            vec(3), vec(3), vec(1), vec(1), vec(1), vec(1)),
                 out_shape=(_sds((S, F), BF16), _sds((S, F), BF16), _sds((3, F), F32), _sds((3, F), F32),
                            _sds((1, F), F32), _sds((1, F), F32), _sds((1, F), F32), _sds((1, F), F32)),
                 scratch_shapes=[pltpu.VMEM((ts + HB, tc), F32), pltpu.VMEM((ts + HB, tc), F32),
                                 pltpu.VMEM((40, tc), F32), pltpu.VMEM((40, tc), F32)], name=name)(
        hu, hu, dact_dg, dact_dv, dact_dg, dact_dv, dact, dact, cw, cw)


def _t5_bucket(dist):
    max_exact = N_BUCKETS // 2
    d = np.maximum(dist, 1).astype(np.float64)
    large = max_exact + (np.log(d / max_exact) / math.log(MAX_DISTANCE / max_exact)
                         * (N_BUCKETS - max_exact)).astype(np.int32)
    large = np.minimum(large, N_BUCKETS - 1)
    return np.where(dist < max_exact, dist, large).astype(np.int32)


def _pattern_tables(window, dilation):
    qi = np.arange(ATTN_BLOCK)[:, None]
    kj = np.arange(2 * ATTN_BLOCK)[None, :]
    dist = qi + ATTN_BLOCK - kj
    valid = (dist >= 0) & (dist <= window // dilation)
    bucket = _t5_bucket(np.clip(dist, 0, None) * dilation)
    return bucket, valid


def _dilate_qkv(qkv, *, name, ts):
    S, C = qkv.shape
    dils = [d for (_, d) in PATTERNS if d > 1]

    def body(x_ref, nat_ref, *rest):
        outs, tmp = rest[:-1], rest[-1]
        nat_ref[...] = x_ref[...].astype(BF16)
        for j in range(C // LANES):
            cols = slice(j * LANES, (j + 1) * LANES)
            tmp[...] = x_ref[:, cols]
            for d, o_ref in zip(dils, outs):
                _dilate(tmp, o_ref, cols, d, ts, BF16)

    out_shape = (_sds((S, C), BF16),) + tuple(_sds((d, S // d, C), BF16) for d in dils)
    out_specs = (_dil_spec(1, ts, C),) + tuple(_dil_spec(d, ts, C) for d in dils)
    res = _call(body, grid=(S // ts,), in_specs=[_dil_spec(1, ts, C)], out_specs=out_specs, out_shape=out_shape,
                scratch_shapes=[pltpu.VMEM((ts, LANES), F32)], name=name)(qkv)
    return [res[0]] + [r.reshape(S, C) for r in res[1:]]


def _attn_group(S):
    nb_min = (S // PATTERNS[-1][1]) // ATTN_BLOCK
    return math.gcd(8, nb_min)


def _attn_fwd(qkv, bias, *, name, nb, G):
    S = qkv.shape[0]
    B, HD = ATTN_BLOCK, HEAD_DIM
    GR = G * B
    ng = S // GR

    def body(q_ref, k_ref, kh_ref, v_ref, vh_ref, bias_ref, o_ref, lse_ref, kbuf, vbuf):
        g = pl.program_id(1)
        halo_ok = (g * G) % nb != 0
        kbuf[0:B, :] = kh_ref[...]
        kbuf[B:B + GR, :] = k_ref[...]
        vbuf[0:B, :] = vh_ref[...]
        vbuf[B:B + GR, :] = v_ref[...]
        col = lax.broadcasted_iota(jnp.int32, (B, 2 * B), 1)
        head0 = lax.broadcasted_iota(jnp.int32, (B, LANES), 1) < HD

        for bi in range(G):
            r0 = bi * B
            q2 = q_ref[r0:r0 + B, :]
            kk = kbuf[r0:r0 + 2 * B, :]
            vv = vbuf[r0:r0 + 2 * B, :]
            zero = jnp.zeros_like(q2)
            os_, ls_, lses = [], [], []
            for hh in range(2):
                qh = jnp.where(head0, q2, zero) if hh == 0 else jnp.where(head0, zero, q2)
                s = lax.dot_general(qh, kk, (((1,), (1,)), ((), ())), preferred_element_type=F32)
                s = s + bias_ref[hh]
                if bi == 0:
                    s = jnp.where(jnp.logical_and(jnp.logical_not(halo_ok), col < B), NEG, s)
                m = jnp.max(s, axis=1, keepdims=True)
                p = jnp.exp(s - m)
                l = jnp.sum(p, axis=1, keepdims=True)
                os_.append(jnp.dot(p.astype(BF16), vv, preferred_element_type=F32))
                ls_.append(l)
                lses.append(m + jnp.log(l))
            o_ref[r0:r0 + B, :] = jnp.where(head0, os_[0], os_[1]) / jnp.where(head0, ls_[0], ls_[1])
            lse_ref[r0:r0 + B, :] = jnp.where(head0, lses[0], lses[1])

    halo = lambda off: (lambda hp, g: (jnp.maximum(g * G - 1, 0), off + hp))
    main = lambda off: (lambda hp, g: (g, off + hp))
    return _call(body, grid=(4, ng),
                 in_specs=[pl.BlockSpec((GR, LANES), main(0)), pl.BlockSpec((GR, LANES), main(4)),
                           pl.BlockSpec((B, LANES), halo(4)), pl.BlockSpec((GR, LANES), main(8)),
                           pl.BlockSpec((B, LANES), halo(8)), pl.BlockSpec((2, B, 2 * B), lambda hp, g: (hp, 0, 0))],
                 out_specs=(pl.BlockSpec((GR, LANES), main(0)), pl.BlockSpec((GR, LANES), main(0))),
                 out_shape=(_sds((S, ATTN_CH), F32), _sds((S, ATTN_CH), F32)),
                 scratch_shapes=[pltpu.VMEM((B + GR, LANES), BF16), pltpu.VMEM((B + GR, LANES), BF16)], name=name)(
        qkv, qkv, qkv, qkv, qkv, bias)


def _dil_spec(d, ts, C):
    if d == 1:
        return pl.BlockSpec((ts, C), lambda i: (i, 0))
    return pl.BlockSpec((d, ts // d, C), lambda i: (0, i, 0))


def _dil_view(a, d):
    return a if d == 1 else a.reshape(d, a.shape[0] // d, a.shape[1])


def _dilate(nat_tmp, dst_ref, cols, d, ts, dtype=F32):
    for r in range(d):
        dst_ref[r, :, cols] = nat_tmp[pl.ds(r, ts // d, stride=d), :].astype(dtype)


def _undilate(src_ref, nat_tmp, cols, d, ts, accumulate=False):
    for r in range(d):
        rows = pl.ds(r, ts // d, stride=d)
        if accumulate:
            nat_tmp[rows, :] = nat_tmp[rows, :] + src_ref[r, :, cols]
        else:
            nat_tmp[rows, :] = src_ref[r, :, cols]


def _attn_merge(o_list, lse_list, dils, *, name, ts):
    S, C = o_list[0].shape
    P = len(o_list)
    nd = sum(1 for d in dils if d > 1)

    def body(*refs):
        o_refs, l_refs = refs[:P], refs[P:2 * P]
        out_ref, lse_ref = refs[2 * P], refs[2 * P + 1]
        lse_d_refs = refs[2 * P + 2:2 * P + 2 + nd]
        scratch = list(refs[2 * P + 2 + nd:])
        tmp = scratch.pop()
        for j in range(C // LANES):
            cols = slice(j * LANES, (j + 1) * LANES)
            os_, ls, free = [], [], list(scratch)
            for o_ref, l_ref, d in zip(o_refs, l_refs, dils):
                if d > 1:
                    so, sl = free.pop(0), free.pop(0)
                    _undilate(o_ref, so, cols, d, ts)
                    _undilate(l_ref, sl, cols, d, ts)
                    os_.append(so[...])
                    ls.append(sl[...])
                else:
                    os_.append(o_ref[:, cols])
                    ls.append(l_ref[:, cols])
            m = ls[0]
            for l in ls[1:]:
                m = jnp.maximum(m, l)
            ws = [jnp.exp(l - m) for l in ls]
            den = ws[0]
            for w in ws[1:]:
                den = den + w
            num = ws[0] * os_[0]
            for w, o in zip(ws[1:], os_[1:]):
                num = num + w * o
            out_ref[:, cols] = num / den
            tmp[...] = m + jnp.log(den)
            lse_ref[:, cols] = tmp[...]
            for l_out, d in zip(lse_d_refs, [d for d in dils if d > 1]):
                _dilate(tmp, l_out, cols, d, ts)

    row = _dil_spec(1, ts, C)
    dd = [d for d in dils if d > 1]
    res = _call(body, grid=(S // ts,), in_specs=[_dil_spec(d, ts, C) for d in dils] * 2,
                out_specs=(row, row) + tuple(_dil_spec(d, ts, C) for d in dd),
                out_shape=(_sds((S, C), F32), _sds((S, C), F32)) + tuple(_sds((d, S // d, C), F32) for d in dd),
                scratch_shapes=[pltpu.VMEM((ts, LANES), F32)] * (2 * nd + 1), name=name)(
        *[_dil_view(o, d) for o, d in zip(o_list, dils)], *[_dil_view(l, d) for l, d in zip(lse_list, dils)])
    lse_by_d = {1: res[1]}
    lse_by_d.update({d: r.reshape(S, C) for d, r in zip(dd, res[2:])})
    return res[0], [lse_by_d[d] for d in dils]


def _attn_prep(dout, out, dils, *, name, ts):
    S, C = out.shape
    HD = HEAD_DIM
    dd = [d for d in dils if d > 1]
    nd = len(dd)

    def body(do_ref, o_ref, d_ref, dob_ref, *rest):
        outs, tmp_d, tmp_o = rest[:2 * nd], rest[2 * nd], rest[2 * nd + 1]
        dob_ref[...] = do_ref[...].astype(BF16)
        for j in range(C // LANES):
            cols = slice(j * LANES, (j + 1) * LANES)
            do = do_ref[:, cols]
            prod = do * o_ref[:, cols]
            tmp_o[...] = do
            for h in range(LANES // HD):
                cs = slice(h * HD, (h + 1) * HD)
                tmp_d[:, cs] = jnp.broadcast_to(jnp.sum(prod[:, cs], axis=1, keepdims=True), (ts, HD))
            d_ref[:, cols] = tmp_d[...]
            for d, dd_out, do_out in zip(dd, outs[:nd], outs[nd:]):
                _dilate(tmp_d, dd_out, cols, d, ts)
                _dilate(tmp_o, do_out, cols, d, ts, BF16)

    row = _dil_spec(1, ts, C)
    res = _call(body, grid=(S // ts,), in_specs=[row, row],
                out_specs=(row, row) + tuple(_dil_spec(d, ts, C) for d in dd) * 2,
                out_shape=(_sds((S, C), F32), _sds((S, C), BF16)) + tuple(_sds((d, S // d, C), F32) for d in dd)
                + tuple(_sds((d, S // d, C), BF16) for d in dd),
                scratch_shapes=[pltpu.VMEM((ts, LANES), F32)] * 2, name=name)(dout, out)
    dd_by_d, do_by_d = {1: res[0]}, {1: res[1]}
    dd_by_d.update({d: r.reshape(S, C) for d, r in zip(dd, res[2:2 + nd])})
    do_by_d.update({d: r.reshape(S, C) for d, r in zip(dd, res[2 + nd:])})
    return [dd_by_d[d] for d in dils], [do_by_d[d] for d in dils]


def _attn_bwd(qkv, do, lse, dd, bias, *, name, nb, G):
    S = qkv.shape[0]
    B, HD = ATTN_BLOCK, HEAD_DIM
    GR = G * B
    ng = S // GR
    nblk = S // B

    def body(q_ref, k_ref, kh_ref, v_ref, vh_ref, do_ref, lse_ref, dd_ref, qn_ref, don_ref, lsen_ref, ddn_ref,
             bias_ref, dq_ref, dk_ref, dv_ref, dbias_ref, kbuf, vbuf, dkp, dvp):
        g = pl.program_id(1)
        halo_ok = (g * G) % nb != 0
        next_ok = jnp.logical_and(((g + 1) * G) % nb != 0, g < ng - 1)
        kbuf[0:B, :] = kh_ref[...]
        kbuf[B:B + GR, :] = k_ref[...]
        vbuf[0:B, :] = vh_ref[...]
        vbuf[B:B + GR, :] = v_ref[...]
        col = lax.broadcasted_iota(jnp.int32, (B, 2 * B), 1)

        def tn_dot(a, b):
            return lax.dot_general(a, b, (((0,), (0,)), ((), ())), preferred_element_type=F32)

        @pl.when(g == 0)
        def _():
            dbias_ref[...] = jnp.zeros_like(dbias_ref)

        head0 = lax.broadcasted_iota(jnp.int32, (B, LANES), 1) < HD

        def nt_dot(a, b):
            return lax.dot_general(a, b, (((1,), (1,)), ((), ())), preferred_element_type=F32)

        def one_head(x, hh):
            zero = jnp.zeros_like(x)
            return jnp.where(head0, x, zero) if hh == 0 else jnp.where(head0, zero, x)

        def pick(per_head):
            return jnp.where(head0, per_head[0], per_head[1])

        for bi in range(G):
            r0 = bi * B
            q2 = q_ref[r0:r0 + B, :]
            do2 = do_ref[r0:r0 + B, :]
            kk = kbuf[r0:r0 + 2 * B, :]
            vv = vbuf[r0:r0 + 2 * B, :]
            dq, dko, dvo, dkq, dvq = [], [], [], [], []
            for hh in range(2):
                s = nt_dot(one_head(q2, hh), kk) + bias_ref[hh]
                if bi == 0:
                    s = jnp.where(jnp.logical_and(jnp.logical_not(halo_ok), col < B), NEG, s)
                p = jnp.exp(s - lse_ref[r0:r0 + B, hh * HD:hh * HD + 1])
                dp = nt_dot(one_head(do2, hh), vv)
                ds = p * (dp - dd_ref[r0:r0 + B, hh * HD:hh * HD + 1])
                dbias_ref[hh] += ds
                ds_bf, p_bf = ds.astype(BF16), p.astype(BF16)
                dq.append(jnp.dot(ds_bf, kk, preferred_element_type=F32))
                dko.append(tn_dot(ds_bf[:, B:], q2))
                dvo.append(tn_dot(p_bf[:, B:], do2))
                if bi > 0:
                    dkq.append(tn_dot(ds_bf[:, :B], q2))
                    dvq.append(tn_dot(p_bf[:, :B], do2))
            dq_ref[r0:r0 + B, :] = pick(dq)
            dk_ref[r0:r0 + B, :] = pick(dko)
            dv_ref[r0:r0 + B, :] = pick(dvo)
            if bi > 0:
                dkp[r0 - B:r0, :] = pick(dkq)
                dvp[r0 - B:r0, :] = pick(dvq)

        @pl.when(next_ok)
        def _():
            qn = qn_ref[...]
            don = don_ref[...]
            kl = kbuf[GR:GR + B, :]
            vl = vbuf[GR:GR + B, :]
            dkq, dvq = [], []
            for hh in range(2):
                s = nt_dot(one_head(qn, hh), kl) + bias_ref[hh, :, 0:B]
                p = jnp.exp(s - lsen_ref[:, hh * HD:hh * HD + 1])
                dp = nt_dot(one_head(don, hh), vl)
                ds = p * (dp - ddn_ref[:, hh * HD:hh * HD + 1])
                dkq.append(tn_dot(ds.astype(BF16), qn))
                dvq.append(tn_dot(p.astype(BF16), don))
            dkp[GR - B:GR, :] = pick(dkq)
            dvp[GR - B:GR, :] = pick(dvq)

        @pl.when(jnp.logical_not(next_ok))
        def _():
            dkp[GR - B:GR, :] = jnp.zeros((B, LANES), F32)
            dvp[GR - B:GR, :] = jnp.zeros((B, LANES), F32)

        dk_ref[...] += dkp[...]
        dv_ref[...] += dvp[...]

    halo = lambda off: (lambda hp, g: (jnp.maximum(g * G - 1, 0), off + hp))
    main = lambda off: (lambda hp, g: (g, off + hp))
    nxt = lambda off: (lambda hp, g: (jnp.minimum((g + 1) * G, nblk - 1), off + hp))
    big, small = (lambda m: pl.BlockSpec((GR, LANES), m)), (lambda m: pl.BlockSpec((B, LANES), m))
    return _call(body, grid=(4, ng),
                 in_specs=[big(main(0)), big(main(4)), small(halo(4)), big(main(8)), small(halo(8)),
                           big(main(0)), big(main(0)), big(main(0)),
                           small(nxt(0)), small(nxt(0)), small(nxt(0)), small(nxt(0)),
                           pl.BlockSpec((2, B, 2 * B), lambda hp, g: (hp, 0, 0))],
                 out_specs=(big(main(0)), big(main(0)), big(main(0)),
                            pl.BlockSpec((2, B, 2 * B), lambda hp, g: (hp, 0, 0))),
                 out_shape=(_sds((S, ATTN_CH), F32),) * 3 + (_sds((ATTN_HEADS, B, 2 * B), F32),),
                 scratch_shapes=[pltpu.VMEM((B + GR, LANES), BF16), pltpu.VMEM((B + GR, LANES), BF16),
                                 pltpu.VMEM((GR, LANES), F32), pltpu.VMEM((GR, LANES), F32)], name=name)(
        qkv, qkv, qkv, qkv, qkv, do, lse, dd, qkv, do, lse, dd, bias)


def _attn_combine(dq_list, dk_list, dv_list, dils, *, name, ts):
    S, C = dq_list[0].shape
    P = len(dq_list)
    scale = HEAD_DIM ** -0.5
    assert dils[0] == 1

    def body(*refs):
        out_refs, acc = refs[3 * P:3 * P + 3], refs[3 * P + 3]
        for part in range(3):
            for j in range(C // LANES):
                cols = slice(j * LANES, (j + 1) * LANES)
                acc[...] = refs[part * P][:, cols]
                for r, d in zip(refs[part * P + 1:(part + 1) * P], dils[1:]):
                    _undilate(r, acc, cols, d, ts, accumulate=True)
                tot = acc[...]
                if part == 0:
                    tot = tot * scale
                out_refs[part][:, cols] = tot.astype(BF16)

    return _call(body, grid=(S // ts,), in_specs=[_dil_spec(d, ts, C) for d in dils] * 3,
                 out_specs=(_dil_spec(1, ts, C),) * 3, out_shape=(_sds((S, C), BF16),) * 3,
                 scratch_shapes=[pltpu.VMEM((ts, LANES), F32)], name=name)(
        *[_dil_view(a, d) for lst in (dq_list, dk_list, dv_list) for a, d in zip(lst, dils)])


def _bias_tables(table, bucket_flat, *, name):
    P, _, K = bucket_flat.shape
    H = table.shape[1]
    KC = 4096

    def body(t_ref, bk_ref, out_ref):
        row = lax.broadcasted_iota(jnp.int32, (N_BUCKETS, KC), 0)
        for c in range(K // KC):
            bk = bk_ref[0, :, c * KC:(c + 1) * KC]
            onehot = (row == bk).astype(F32)
            vals = jnp.dot(t_ref[...], onehot, preferred_element_type=F32, precision=lax.Precision.HIGHEST)
            out_ref[0, :, c * KC:(c + 1) * KC] = jnp.where(bk >= 0, vals, NEG)

    return _call(body, grid=(P,),
                 in_specs=[pl.BlockSpec((H, N_BUCKETS), lambda p: (0, 0)), pl.BlockSpec((1, 1, K), lambda p: (p, 0, 0))],
                 out_specs=pl.BlockSpec((1, H, K), lambda p: (p, 0, 0)), out_shape=_sds((P, H, K), F32),
                 name=name)(table.T, bucket_flat)


def _bias_grad(dbias_flat, bucket_flat, *, name):
    P, H, K = dbias_flat.shape
    KC = 4096

    def body(db_ref, bk_ref, out_ref):
        p = pl.program_id(0)
        acc = jnp.zeros((N_BUCKETS, H), F32)
        row = lax.broadcasted_iota(jnp.int32, (N_BUCKETS, KC), 0)
        for c in range(K // KC):
            onehot = (row == bk_ref[0, :, c * KC:(c + 1) * KC]).astype(F32)
            acc = acc + lax.dot_general(onehot, db_ref[0, :, c * KC:(c + 1) * KC], (((1,), (1,)), ((), ())),
                                        preferred_element_type=F32, precision=lax.Precision.HIGHEST)

        @pl.when(p == 0)
        def _():
            out_ref[...] = acc

        @pl.when(p > 0)
        def _():
            out_ref[...] += acc

    return _call(body, grid=(P,),
                 in_specs=[pl.BlockSpec((1, H, K), lambda p: (p, 0, 0)), pl.BlockSpec((1, 1, K), lambda p: (p, 0, 0))],
                 out_specs=pl.BlockSpec((N_BUCKETS, H), lambda p: (0, 0)), out_shape=_sds((N_BUCKETS, H), F32),
                 name=name)(dbias_flat, bucket_flat)


def _allgather(blocks, *, name):
    n = len(blocks)

    def body(*refs):
        x_refs, out_refs = refs[:n], refs[n:2 * n]
        send_sems, recv_sems, local_sems = refs[2 * n:]
        x, y, c = lax.axis_index("x"), lax.axis_index("y"), lax.axis_index("c")
        me, sibling = (x, y, c), (x, y, 1 - c)
        chips = [(1 - x, y), (x, 1 - y), (1 - x, 1 - y)]

        def copy(i, k, blk, to, own=False):
            slot = out_refs[i].at[4 * blk[0] + 2 * blk[1] + blk[2]]
            return pltpu.make_async_remote_copy(src_ref=x_refs[i] if own else slot, dst_ref=slot,
                                                send_sem=send_sems.at[7 * i + k], recv_sem=recv_sems.at[7 * i + k],
                                                device_id=to, device_id_type=MESH)

        mine = [pltpu.make_async_copy(x_refs[i], out_refs[i].at[4 * x + 2 * y + c], local_sems.at[i])
                for i in range(n)]
        for cp in mine:
            cp.start()
        first = []
        for i in range(n):
            first.append(copy(i, 0, me, sibling, own=True))
            first += [copy(i, 1 + j, me, (*chip, c), own=True) for j, chip in enumerate(chips)]
        for cp in first:
            cp.start()
        passed = []
        for j, chip in enumerate(chips):
            for i in range(n):
                copy(i, 1 + j, (*chip, c), me).wait_recv()
                fwd = copy(i, 4 + j, (*chip, c), sibling)
                fwd.start()
                passed.append(fwd)
        for i in range(n):
            copy(i, 0, sibling, me).wait_recv()
            for j, chip in enumerate(chips):
                copy(i, 4 + j, (*chip, 1 - c), me).wait_recv()
        for cp in first + passed:
            cp.wait_send()
        for cp in mine:
            cp.wait()

    any_spec = pl.BlockSpec(memory_space=pl.ANY)
    return pl.pallas_call(body, out_shape=tuple(_sds((N_DEV,) + b.shape, b.dtype) for b in blocks),
                          in_specs=[any_spec] * n, out_specs=(any_spec,) * n,
                          scratch_shapes=[pltpu.SemaphoreType.DMA((7 * n,)), pltpu.SemaphoreType.DMA((7 * n,)),
                                          pltpu.SemaphoreType.DMA((n,))], name=name)(*blocks)


def _exchange(sends, *, name):
    n = len(sends)

    def body(*refs):
        send_refs, recv_refs = refs[:n], refs[n:2 * n]
        send_sems, recv_sems, local_sems = refs[2 * n:]
        x, y, c = lax.axis_index("x"), lax.axis_index("y"), lax.axis_index("c")
        me = 4 * x + 2 * y + c
        mine = [pltpu.make_async_copy(send_refs[i].at[me], recv_refs[i].at[me], local_sems.at[i]) for i in range(n)]
        for cp in mine:
            cp.start()
        copies = []
        for k in range(1, N_DEV):
            px = 1 - x if k & 4 else x
            py = 1 - y if k & 2 else y
            pc = 1 - c if k & 1 else c
            for i in range(n):
                cp = pltpu.make_async_remote_copy(src_ref=send_refs[i].at[4 * px + 2 * py + pc],
                                                  dst_ref=recv_refs[i].at[me],
                                                  send_sem=send_sems.at[7 * i + k - 1],
                                                  recv_sem=recv_sems.at[7 * i + k - 1],
                                                  device_id=(px, py, pc), device_id_type=MESH)
                cp.start()
                copies.append(cp)
        for cp in copies:
            cp.wait_recv()
        for cp in copies:
            cp.wait_send()
        for cp in mine:
            cp.wait()

    any_spec = pl.BlockSpec(memory_space=pl.ANY)
    return pl.pallas_call(body, out_shape=tuple(_sds(s.shape, s.dtype) for s in sends), in_specs=[any_spec] * n,
                          out_specs=(any_spec,) * n,
                          scratch_shapes=[pltpu.SemaphoreType.DMA((7 * n,)), pltpu.SemaphoreType.DMA((7 * n,)),
                                          pltpu.SemaphoreType.DMA((n,))], name=name)(*sends)


def _adamw(w, m, v, g_parts, *, name, tr):
    R, W = w.shape
    bc1 = 1.0 - ADAM_B1 ** ADAM_STEP
    bc2 = 1.0 - ADAM_B2 ** ADAM_STEP

    def body(w_ref, m_ref, v_ref, g_ref, go_ref, d_ref, mo_ref, vo_ref):
        g = g_ref[0].astype(F32)
        for i in range(1, N_DEV):
            g = g + g_ref[i].astype(F32)
        mn = ADAM_B1 * m_ref[...] + (1.0 - ADAM_B1) * g
        vn = ADAM_B2 * v_ref[...] + (1.0 - ADAM_B2) * (g * g)
        m_hat = mn / bc1
        v_hat = vn / bc2
        go_ref[...] = g
        d_ref[...] = -ADAM_LR * (m_hat / (jnp.sqrt(v_hat) + ADAM_EPS) + ADAM_WD * w_ref[...])
        mo_ref[...] = mn
        vo_ref[...] = vn

    row = pl.BlockSpec((tr, W), lambda i: (i, 0))
    return _call(body, grid=(R // tr,), in_specs=[row, row, row, pl.BlockSpec((N_DEV, tr, W), lambda i: (0, i, 0))],
                 out_specs=(row,) * 4, out_shape=(_sds((R, W), F32),) * 4, name=name)(w, m, v, g_parts)


def _round_up(n, k):
    return -(-n // k) * k


def _pack(arrs, width, row_mult):
    pieces, offs, r = [], [], 0
    for a in arrs:
        n = a.size
        rows = _round_up(n, width) // width
        flat = a.reshape(-1)
        if rows * width != n:
            flat = jnp.pad(flat, (0, rows * width - n))
        pieces.append(flat.reshape(rows, width))
        offs.append((r, rows, n))
        r += rows
    total = _round_up(r, row_mult)
    if total != r:
        pieces.append(jnp.zeros((total - r, width), pieces[0].dtype))
    return jnp.concatenate(pieces, axis=0), offs


def _unpack(pack, offs, shapes):
    out = []
    for (r, rows, n), shp in zip(offs, shapes):
        out.append(pack[r:r + rows].reshape(-1)[:n].reshape(shp))
    return out


def _gather_axis(full8, axis):
    moved = jnp.moveaxis(full8, 0, axis)
    shp = list(moved.shape)
    shp[axis:axis + 2] = [shp[axis] * shp[axis + 1]]
    return moved.reshape(shp)


def _split_axis(full, axis):
    shp = list(full.shape)
    shp[axis:axis + 1] = [N_DEV, shp[axis] // N_DEV]
    return jnp.moveaxis(full.reshape(shp), axis, 0)


def kernel(x, w_in, b_in, conv_dw_w, conv_dw_b, conv_ln_g, conv_ln_b, rel_bias_table, gmlp_ln_g, gmlp_ln_b, gmlp_w_s, gmlp_b_s, w_out, b_out, ln1_g, ln1_b, ffn_w_up, ffn_b_up, ffn_conv_w, ffn_conv_b, ffn_w_down, ffn_b_down, ln2_g, ln2_b, loss_target, m_w_in, m_b_in, m_conv_dw_w, m_conv_dw_b, m_conv_ln_g, m_conv_ln_b, m_rel_bias_table, m_gmlp_ln_g, m_gmlp_ln_b, m_gmlp_w_s, m_gmlp_b_s, m_w_out, m_b_out, m_ln1_g, m_ln1_b, m_ffn_w_up, m_ffn_b_up, m_ffn_conv_w, m_ffn_conv_b, m_ffn_w_down, m_ffn_b_down, m_ln2_g, m_ln2_b, v_w_in, v_b_in, v_conv_dw_w, v_conv_dw_b, v_conv_ln_g, v_conv_ln_b, v_rel_bias_table, v_gmlp_ln_g, v_gmlp_ln_b, v_gmlp_w_s, v_gmlp_b_s, v_w_out, v_b_out, v_ln1_g, v_ln1_b, v_ffn_w_up, v_ffn_b_up, v_ffn_conv_w, v_ffn_conv_b, v_ffn_w_down, v_ffn_b_down, v_ln2_g, v_ln2_b):
    W = dict(w_in=w_in, b_in=b_in, conv_dw_w=conv_dw_w, conv_dw_b=conv_dw_b, conv_ln_g=conv_ln_g,
             conv_ln_b=conv_ln_b, rel_bias_table=rel_bias_table, gmlp_ln_g=gmlp_ln_g, gmlp_ln_b=gmlp_ln_b,
             gmlp_w_s=gmlp_w_s, gmlp_b_s=gmlp_b_s, w_out=w_out, b_out=b_out, ln1_g=ln1_g, ln1_b=ln1_b,
             ffn_w_up=ffn_w_up, ffn_b_up=ffn_b_up, ffn_conv_w=ffn_conv_w, ffn_conv_b=ffn_conv_b,
             ffn_w_down=ffn_w_down, ffn_b_down=ffn_b_down, ln2_g=ln2_g, ln2_b=ln2_b)
    Mo = dict(w_in=m_w_in, b_in=m_b_in, conv_dw_w=m_conv_dw_w, conv_dw_b=m_conv_dw_b, conv_ln_g=m_conv_ln_g,
              conv_ln_b=m_conv_ln_b, rel_bias_table=m_rel_bias_table, gmlp_ln_g=m_gmlp_ln_g, gmlp_ln_b=m_gmlp_ln_b,
              gmlp_w_s=m_gmlp_w_s, gmlp_b_s=m_gmlp_b_s, w_out=m_w_out, b_out=m_b_out, ln1_g=m_ln1_g, ln1_b=m_ln1_b,
              ffn_w_up=m_ffn_w_up, ffn_b_up=m_ffn_b_up, ffn_conv_w=m_ffn_conv_w, ffn_conv_b=m_ffn_conv_b,
              ffn_w_down=m_ffn_w_down, ffn_b_down=m_ffn_b_down, ln2_g=m_ln2_g, ln2_b=m_ln2_b)
    Vo = dict(w_in=v_w_in, b_in=v_b_in, conv_dw_w=v_conv_dw_w, conv_dw_b=v_conv_dw_b, conv_ln_g=v_conv_ln_g,
              conv_ln_b=v_conv_ln_b, rel_bias_table=v_rel_bias_table, gmlp_ln_g=v_gmlp_ln_g, gmlp_ln_b=v_gmlp_ln_b,
              gmlp_w_s=v_gmlp_w_s, gmlp_b_s=v_gmlp_b_s, w_out=v_w_out, b_out=v_b_out, ln1_g=v_ln1_g, ln1_b=v_ln1_b,
              ffn_w_up=v_ffn_w_up, ffn_b_up=v_ffn_b_up, ffn_conv_w=v_ffn_conv_w, ffn_conv_b=v_ffn_conv_b,
              ffn_w_down=v_ffn_w_down, ffn_b_down=v_ffn_b_down, ln2_g=v_ln2_g, ln2_b=v_ln2_b)

    xs = x[0]
    target = loss_target[0]
    S, D = xs.shape
    F2 = ffn_b_up.shape[1]
    F = F2 // 2
    ts = min(512, S)
    G = _attn_group(S)
    tc = F // 2 if (F // 2) % LANES == 0 else F

    mat_names = SHARDED[:4]
    payload = [W[n].astype(BF16) if n in mat_names else W[n] for n in SHARDED]
    wall = _allgather(payload, name="weight_allgather")
    full = {n: _gather_axis(parts, SHARD_AXIS[n]) for n, parts in zip(SHARDED, wall)}

    tables = [_pattern_tables(w, d) for (w, d) in PATTERNS]
    bucket_flat = jnp.asarray(np.stack([np.where(v, b, -1).reshape(1, -1) for (b, v) in tables]).astype(np.int32))
    bias_all = _bias_tables(rel_bias_table, bucket_flat, name="bias_tables")
    biases = [bias_all[p].reshape(ATTN_HEADS, ATTN_BLOCK, 2 * ATTN_BLOCK) for p in range(len(PATTERNS))]
    nbs = [(S // d) // ATTN_BLOCK for (_, d) in PATTERNS]
    dils = [d for (_, d) in PATTERNS]
    scale = HEAD_DIM ** -0.5

    saved = []
    cur = xs
    for l in range(DEPTH):
        Win, Wout, Wup, Wdown = full['w_in'][l], full['w_out'][l], full['ffn_w_up'][l], full['ffn_w_down'][l]
        qcols = slice(2 * CONV_CH, 2 * CONV_CH + ATTN_CH)
        Win_s = Win.at[:, qcols].multiply(scale)
        b_in_s = b_in[l].at[qcols].multiply(scale)
        h_a, qkv, h_c = _mm([cur], [Win_s], bias=b_in_s, tm=ts, name="in_proj",
                            splits=((2 * CONV_CH, F32, 1.0), (3 * ATTN_CH, F32, 1.0), (2 * GMLP_CH, F32, 1.0)))
        conv_out, hc = _conv_fwd(h_a, full['conv_dw_w'][l], conv_dw_b[l], conv_ln_g[l], conv_ln_b[l],
                                 name="conv_fwd", ts=ts)
        qkv_d = _dilate_qkv(qkv, name="dilate_qkv", ts=ts)
        o_ps, lse_ps = [], []
        for p, d in enumerate(dils):
            o_p, lse_p = _attn_fwd(qkv_d[p], biases[p], name=f"attn_fwd_d{d}", nb=nbs[p], G=G)
            o_ps.append(o_p)
            lse_ps.append(lse_p)
        attn_out, lse = _attn_merge(o_ps, lse_ps, dils, name="attn_merge", ts=ts)
        w_tril = jnp.tril(gmlp_w_s[l]).astype(BF16)
        bs_rows = jnp.repeat(gmlp_b_s[l].T, GMLP_GROUP_DIM, axis=1)
        gm_out = _gmlp_fwd(h_c, gmlp_ln_g[l], gmlp_ln_b[l], w_tril, bs_rows, name="gmlp_fwd", ts=ts)
        x1, xhat1, rstd1 = _mm([conv_out, attn_out, gm_out],
                               [Wout[:CONV_CH], Wout[CONV_CH:CONV_CH + ATTN_CH], Wout[CONV_CH + ATTN_CH:]],
                               bias=b_out[l], resid=cur, resid_scale=ALPHA, ln=(ln1_g[l], ln1_b[l]), tm=ts,
                               name="out_proj_ln")
        hu = _mm([x1], [Wup], bias=ffn_b_up[l], tm=ts, tn=F, name="ffn_up")
        act, act_dg, act_dv = _ffn_act_fwd(hu, full['ffn_conv_w'][l], ffn_conv_b[l], name="ffn_act_fwd",
                                           ts=min(256, S), tc=tc)
        x2, xhat2, rstd2 = _mm([act], [Wdown], bias=ffn_b_down[l], resid=x1, resid_scale=ALPHA,
                               ln=(ln2_g[l], ln2_b[l]), tm=ts, name="ffn_down_ln")
        saved.append(dict(x0=cur, h_a=h_a, h_c=h_c, qkv_d=qkv_d, hc=hc, conv_out=conv_out, attn_out=attn_out,
                          lse=lse, gm_out=gm_out, w_tril=w_tril, bs_rows=bs_rows, x1=x1, xhat1=xhat1, rstd1=rstd1,
                          hu=hu, act=act, act_dg=act_dg, act_dv=act_dv, xhat2=xhat2, rstd2=rstd2))
        cur = x2

    grads = {n: [None] * DEPTH for n in WEIGHTS if n != 'rel_bias_table'}
    drel = None
    dx = None
    loss_part = None
    tk = min(1024, S)
    for l in reversed(range(DEPTH)):
        sv = saved[l]
        Win, Wout, Wup, Wdown = full['w_in'][l], full['w_out'][l], full['ffn_w_up'][l], full['ffn_w_down'][l]
        if dx is None:
            dz2, dg2, db2, dzs2, loss_part = _ln_bwd(sv['xhat2'], sv['rstd2'], ln2_g[l], b=ln2_b[l], target=target,
                                                     name="ln2_bwd_loss", ts=ts)
        else:
            dz2, dg2, db2, dzs2 = _ln_bwd(sv['xhat2'], sv['rstd2'], ln2_g[l], dy=dx, name="ln_bwd", ts=ts)
        grads['ln2_g'][l], grads['ln2_b'][l], grads['ffn_b_down'][l] = dg2[0], db2[0], dzs2[0]
        grads['ffn_w_down'][l] = _mm_tn(sv['act'], dz2, tm=F // 2 if (F // 2) % LANES == 0 else F, tn=D, tk=tk,
                                        name="dw_down")
        dact = _mm([dz2], [Wdown.T], tm=ts, name="dact")
        dhg, dhv, dwg, dwv, dbg, dbv, dug, duv = _ffn_act_bwd(sv['hu'], dact, sv['act_dg'], sv['act_dv'],
                                                              full['ffn_conv_w'][l], name="ffn_act_bwd",
                                                              ts=min(256, S), tc=tc)
        grads['ffn_conv_w'][l] = jnp.concatenate([dwg, dwv], axis=1)
        grads['ffn_conv_b'][l] = jnp.concatenate([dbg, dbv], axis=1)[0]
        grads['ffn_b_up'][l] = jnp.concatenate([dug, duv], axis=1)[0]
        grads['ffn_w_up'][l] = jnp.concatenate(
            [_mm_tn(sv['x1'], dhg, tm=D, tn=tc, tk=tk, name="dw_up"),
             _mm_tn(sv['x1'], dhv, tm=D, tn=tc, tk=tk, name="dw_up")], axis=1)
        WupT = Wup.T
        dx1 = _mm([dhg, dhv], [WupT[:F], WupT[F:]], resid=dz2, resid_scale=ALPHA, tm=ts, name="dx1")
        dz1, dg1, db1, dzs1 = _ln_bwd(sv['xhat1'], sv['rstd1'], ln1_g[l], dy=dx1, name="ln_bwd", ts=ts)
        grads['ln1_g'][l], grads['ln1_b'][l], grads['b_out'][l] = dg1[0], db1[0], dzs1[0]
        grads['w_out'][l] = jnp.concatenate(
            [_mm_tn(sv['conv_out'], dz1, tm=CONV_CH, tn=D, tk=tk, name="dw_out_conv"),
             _mm_tn(sv['attn_out'], dz1, tm=ATTN_CH, tn=D, tk=tk, name="dw_out_attn"),
             _mm_tn(sv['gm_out'], dz1, tm=GMLP_CH, tn=D, tk=tk, name="dw_out_conv")], axis=0)
        dc_conv, dc_attn, dc_gm = _mm([dz1], [Wout.T], tm=ts, name="dcat",
                                      splits=((CONV_CH, F32, 1.0), (ATTN_CH, F32, 1.0), (GMLP_CH, F32, 1.0)))
        dh_a, ddw, ddwb, dclg, dclb = _conv_bwd(sv['h_a'], sv['hc'], dc_conv, full['conv_dw_w'][l], conv_ln_g[l],
                                                conv_ln_b[l], name="conv_bwd", ts=ts)
        grads['conv_dw_w'][l], grads['conv_dw_b'][l] = ddw[:CONV_WIDTH], ddwb[0]
        grads['conv_ln_g'][l], grads['conv_ln_b'][l] = dclg[0], dclb[0]
        dd_d, do_d = _attn_prep(dc_attn, sv['attn_out'], dils, name="attn_prep", ts=ts)
        dqs, dks, dvs, dbs = [], [], [], []
        for p, d in enumerate(dils):
            dq, dk, dv, dbias = _attn_bwd(sv['qkv_d'][p], do_d[p], sv['lse'][p], dd_d[p], biases[p],
                                          name=f"attn_bwd_d{d}", nb=nbs[p], G=G)
            dqs.append(dq)
            dks.append(dk)
            dvs.append(dv)
            dbs.append(dbias.reshape(1, ATTN_HEADS, -1))
        dqkv = _attn_combine(dqs, dks, dvs, dils, name="attn_combine", ts=ts)
        dr = _bias_grad(jnp.concatenate(dbs, axis=0), bucket_flat, name="bias_grad")
        drel = dr if drel is None else drel + dr
        w_tril_t = jnp.swapaxes(sv['w_tril'], 1, 2)
        dh_c, dws, dbs_acc, dglg, dglb = _gmlp_bwd(sv['h_c'], dc_gm, gmlp_ln_g[l], gmlp_ln_b[l], sv['w_tril'],
                                                   w_tril_t, sv['bs_rows'], name="gmlp_bwd", ts=ts)
        grads['gmlp_w_s'][l] = jnp.tril(dws)
        grads['gmlp_b_s'][l] = dbs_acc[:, :GMLP_GROUPS].T
        grads['gmlp_ln_g'][l], grads['gmlp_ln_b'][l] = dglg[0], dglb[0]
        dwa, ca = _mm_tn(sv['x0'], dh_a, tm=D, tn=2 * CONV_CH, tk=tk, colsum=True, name="dw_in_side")
        dwq = [_mm_tn(sv['x0'], part, tm=D, tn=ATTN_CH, tk=tk, colsum=True, name="dw_in_side") for part in dqkv]
        dwc, cc = _mm_tn(sv['x0'], dh_c, tm=D, tn=2 * GMLP_CH, tk=tk, colsum=True, name="dw_in_side")
        grads['w_in'][l] = jnp.concatenate([dwa] + [w for w, _ in dwq] + [dwc], axis=1)
        grads['b_in'][l] = jnp.concatenate([ca] + [c for _, c in dwq] + [cc], axis=1)[0]
        WinT = Win.T
        edges = [0, 2 * CONV_CH] + [2 * CONV_CH + k * ATTN_CH for k in (1, 2, 3)] + [WinT.shape[0]]
        dx = _mm([dh_a, *dqkv, dh_c], [WinT[a:b] for a, b in zip(edges[:-1], edges[1:])], resid=dz1,
                 resid_scale=ALPHA, tm=ts, name="dx0")

    gfull = {n: jnp.stack(v) for n, v in grads.items()}
    gfull['rel_bias_table'] = drel

    sends = []
    for n in SHARDED:
        parts = _split_axis(gfull[n], SHARD_AXIS[n])
        sends.append(parts.reshape(N_DEV, -1, parts.shape[-1]).astype(BF16))
    recvs = _exchange(sends, name="grad_exchange")
    shard_out = [[], [], [], []]
    for n, recv in zip(SHARDED, recvs):
        shp = W[n].shape
        rows = recv.shape[1]
        tr = rows // 4 if rows % 64 == 0 else rows
        outs = _adamw(*[src[n].reshape(rows, shp[-1]) for src in (W, Mo, Vo)], recv, name=f"adamw_{n}", tr=tr)
        for kind in range(4):
            shard_out[kind].append(outs[kind].reshape(shp))

    gsmall, soffs = _pack([gfull[n] for n in SMALL], LANES, 8)
    gall = _allgather([gsmall], name="small_grad_allgather")[0]
    spacks = [_pack([src[n] for n in SMALL], LANES, 8)[0] for src in (W, Mo, Vo)]
    souts = _adamw(spacks[0], spacks[1], spacks[2], gall, name="adamw_small", tr=gsmall.shape[0])
    small_out = [_unpack(o, soffs, [W[n].shape for n in SMALL]) for o in souts]

    loss = lax.psum(loss_part[0, 0], ("x", "y", "c"))
    by_kind = []
    for kind in range(4):
        d = dict(zip(SHARDED, shard_out[kind]))
        d.update(zip(SMALL, small_out[kind]))
        by_kind.append([d[n] for n in WEIGHTS])
    return (loss, dx[None], *by_kind[0], *by_kind[1], *by_kind[2], *by_kind[3])
```

```python
import math

import numpy as np
import jax
import jax.numpy as jnp
from jax import lax
from jax.experimental import pallas as pl
from jax.experimental.pallas import tpu as pltpu

F32 = jnp.float32
BF16 = jnp.bfloat16

DEPTH = 2
HEAD_DIM = 64
CONV_CH = 256
CONV_WIDTH = 31
ATTN_HEADS = 8
ATTN_CH = ATTN_HEADS * HEAD_DIM
PATTERNS = ((128, 1), (512, 4), (2048, 16))
ATTN_BLOCK = 128
N_BUCKETS = 32
MAX_DISTANCE = 2048
GMLP_CH = 256
GMLP_GROUPS = 4
GMLP_GROUP_DIM = GMLP_CH // GMLP_GROUPS
CHUNK = 128
FFN_CONV_WIDTH = 3
LN_EPS = 1e-5
ALPHA = (2.0 * DEPTH) ** 0.25
ADAM_LR = 0.001
ADAM_B1 = 0.9
ADAM_B2 = 0.999
ADAM_EPS = 1e-08
ADAM_WD = 0.01
ADAM_STEP = 10
NEG = -1e30
N_DEV = 8
LANES = 128
CONV_HALO = 32
FFN_HALO = 8
FFN_ROWS = 16
MESH = pl.DeviceIdType.MESH

WEIGHTS = ['w_in', 'b_in', 'conv_dw_w', 'conv_dw_b', 'conv_ln_g', 'conv_ln_b', 'rel_bias_table', 'gmlp_ln_g',
           'gmlp_ln_b', 'gmlp_w_s', 'gmlp_b_s', 'w_out', 'b_out', 'ln1_g', 'ln1_b', 'ffn_w_up', 'ffn_b_up',
           'ffn_conv_w', 'ffn_conv_b', 'ffn_w_down', 'ffn_b_down', 'ln2_g', 'ln2_b']
SHARDED = ['w_in', 'w_out', 'ffn_w_up', 'ffn_w_down', 'conv_dw_w', 'ffn_conv_w']
SHARD_AXIS = {'w_in': 2, 'w_out': 1, 'ffn_w_up': 2, 'ffn_w_down': 1, 'conv_dw_w': 2, 'ffn_conv_w': 2}
SMALL = [n for n in WEIGHTS if n not in SHARDED]


def _call(body, *, grid=(), vmem_mb=48, **kw):
    params = pltpu.CompilerParams(dimension_semantics=("arbitrary",) * len(grid), vmem_limit_bytes=vmem_mb << 20)
    return pl.pallas_call(body, grid=grid, compiler_params=params, **kw)


def _sds(shape, dtype):
    return jax.ShapeDtypeStruct(shape, dtype)


def _ln_rows(z):
    mu = jnp.mean(z, axis=-1, keepdims=True)
    zc = z - mu
    var = jnp.mean(zc * zc, axis=-1, keepdims=True)
    rstd = lax.rsqrt(var + LN_EPS)
    return zc * rstd, rstd


def _ln_bwd_rows(dxhat, xhat, rstd):
    m1 = jnp.mean(dxhat, axis=-1, keepdims=True)
    m2 = jnp.mean(dxhat * xhat, axis=-1, keepdims=True)
    return rstd * (dxhat - m1 - xhat * m2)


def _colsum(v):
    return jnp.sum(v, axis=0, keepdims=True)


def _mm(a_list, w_list, *, name, tm, tn=None, bias=None, resid=None, resid_scale=1.0, ln=None, splits=None,
        out_dtype=F32):
    na = len(a_list)
    M = a_list[0].shape[0]
    N = w_list[0].shape[1]
    tn = N if tn is None else tn
    assert M % tm == 0 and N % tn == 0
    assert ln is None or tn == N
    assert splits is None or tn == N

    def body(*refs):
        a_refs, w_refs = refs[:na], refs[na:2 * na]
        pos = 2 * na
        acc = None
        for a_ref, w_ref in zip(a_refs, w_refs):
            t = jnp.dot(a_ref[...].astype(BF16), w_ref[...], preferred_element_type=F32)
            acc = t if acc is None else acc + t
        if bias is not None:
            acc = acc + refs[pos][...]
            pos += 1
        if resid is not None:
            acc = resid_scale * refs[pos][...] + acc
            pos += 1
        if ln is not None:
            g_ref, b_ref = refs[pos], refs[pos + 1]
            y_ref, xhat_ref, rstd_ref = refs[pos + 2], refs[pos + 3], refs[pos + 4]
            xhat, rstd = _ln_rows(acc)
            y_ref[...] = xhat * g_ref[...] + b_ref[...]
            xhat_ref[...] = xhat
            rstd_ref[...] = rstd
        elif splits is not None:
            c0 = 0
            for o_ref, (width, dtype, scale) in zip(refs[pos:], splits):
                part = acc[:, c0:c0 + width]
                if scale != 1.0:
                    part = part * scale
                o_ref[...] = part.astype(dtype)
                c0 += width
        else:
            refs[pos][...] = acc.astype(out_dtype)

    in_specs = [pl.BlockSpec((tm, a.shape[1]), lambda j, i: (i, 0)) for a in a_list]
    in_specs += [pl.BlockSpec((w.shape[0], tn), lambda j, i: (0, j)) for w in w_list]
    args = list(a_list) + list(w_list)
    if bias is not None:
        in_specs.append(pl.BlockSpec((1, tn), lambda j, i: (0, j)))
        args.append(bias.reshape(1, N))
    if resid is not None:
        in_specs.append(pl.BlockSpec((tm, tn), lambda j, i: (i, j)))
        args.append(resid)
    if ln is not None:
        in_specs += [pl.BlockSpec((1, N), lambda j, i: (0, 0))] * 2
        args += [ln[0].reshape(1, N), ln[1].reshape(1, N)]
        out_shape = (_sds((M, N), F32), _sds((M, N), F32), _sds((M, 1), F32))
        out_specs = (pl.BlockSpec((tm, N), lambda j, i: (i, 0)), pl.BlockSpec((tm, N), lambda j, i: (i, 0)),
                     pl.BlockSpec((tm, 1), lambda j, i: (i, 0)))
    elif splits is not None:
        out_shape = tuple(_sds((M, w), d) for (w, d, _) in splits)
        out_specs = tuple(pl.BlockSpec((tm, w), lambda j, i: (i, 0)) for (w, _, _) in splits)
    else:
        out_shape = _sds((M, N), out_dtype)
        out_specs = pl.BlockSpec((tm, tn), lambda j, i: (i, j))
    return _call(body, grid=(N // tn, M // tm), in_specs=in_specs, out_specs=out_specs, out_shape=out_shape,
                 name=name, vmem_mb=56)(*args)


def _mm_tn(a, dy, *, name, tm, tn, tk, colsum=False):
    S, Ka = a.shape
    N = dy.shape[1]
    assert S % tk == 0 and Ka % tm == 0 and N % tn == 0

    def body(a_ref, dy_ref, out_ref, *cs):
        i, k = pl.program_id(1), pl.program_id(2)
        dyb = dy_ref[...]
        part = lax.dot_general(a_ref[...].astype(BF16), dyb.astype(BF16), (((0,), (0,)), ((), ())),
                               preferred_element_type=F32)

        @pl.when(k == 0)
        def _():
            out_ref[...] = part

        @pl.when(k > 0)
        def _():
            out_ref[...] += part

        if colsum:
            cs_ref = cs[0]
            s = _colsum(dyb.astype(F32))

            @pl.when((i == 0) & (k == 0))
            def _():
                cs_ref[...] = s

            @pl.when((i == 0) & (k > 0))
            def _():
                cs_ref[...] += s

    out_shape = [_sds((Ka, N), F32)]
    out_specs = [pl.BlockSpec((tm, tn), lambda j, i, k: (i, j))]
    if colsum:
        out_shape.append(_sds((1, N), F32))
        out_specs.append(pl.BlockSpec((1, tn), lambda j, i, k: (0, j)))
    res = _call(body, grid=(N // tn, Ka // tm, S // tk),
                in_specs=[pl.BlockSpec((tk, tm), lambda j, i, k: (k, i)), pl.BlockSpec((tk, tn), lambda j, i, k: (k, j))],
                out_specs=tuple(out_specs), out_shape=tuple(out_shape), name=name, vmem_mb=56)(a, dy)
    return res if colsum else res[0]


def _mm_tn_shared(a, dys, *, name, tk):
    S, Ka = a.shape
    n = len(dys)
    assert S % tk == 0

    def body(*refs):
        a_ref, dy_refs, outs = refs[0], refs[1:1 + n], refs[1 + n:]
        k = pl.program_id(0)
        a_bf = a_ref[...].astype(BF16)
        for i, dy_ref in enumerate(dy_refs):
            dyb = dy_ref[...]
            part = lax.dot_general(a_bf, dyb.astype(BF16), (((0,), (0,)), ((), ())), preferred_element_type=F32)
            s = _colsum(dyb.astype(F32))
            w_ref, c_ref = outs[2 * i], outs[2 * i + 1]

            @pl.when(k == 0)
            def _():
                w_ref[...] = part
                c_ref[...] = s

            @pl.when(k > 0)
            def _():
                w_ref[...] += part
                c_ref[...] += s

    in_specs = [pl.BlockSpec((tk, Ka), lambda k: (k, 0))]
    in_specs += [pl.BlockSpec((tk, dy.shape[1]), lambda k: (k, 0)) for dy in dys]
    out_specs, out_shape = [], []
    for dy in dys:
        N = dy.shape[1]
        out_specs += [pl.BlockSpec((Ka, N), lambda k: (0, 0)), pl.BlockSpec((1, N), lambda k: (0, 0))]
        out_shape += [_sds((Ka, N), F32), _sds((1, N), F32)]
    res = _call(body, grid=(S // tk,), in_specs=in_specs, out_specs=tuple(out_specs), out_shape=tuple(out_shape),
                name=name, vmem_mb=56)(a, *dys)
    return [(res[2 * i], res[2 * i + 1]) for i in range(n)]


def _ln_bwd(xhat, rstd, g, *, name, ts, dy=None, b=None, target=None):
    S, D = xhat.shape
    from_loss = target is not None

    def body(*refs):
        if from_loss:
            xhat_ref, rstd_ref, g_ref, b_ref, t_ref, dz_ref, dg_ref, db_ref, dzs_ref, loss_ref = refs
        else:
            xhat_ref, rstd_ref, g_ref, dy_ref, dz_ref, dg_ref, db_ref, dzs_ref = refs
        i = pl.program_id(0)
        xh = xhat_ref[...]
        gg = g_ref[...]
        if from_loss:
            err = xh * gg + b_ref[...] - t_ref[...]
            dyv = err * (1.0 / D)
            lsum = (0.5 / D) * jnp.sum(err * err, axis=(0, 1), keepdims=True)
        else:
            dyv = dy_ref[...]
        dz = _ln_bwd_rows(dyv * gg, xh, rstd_ref[...])
        dz_ref[...] = dz
        parts = [(dg_ref, _colsum(dyv * xh)), (db_ref, _colsum(dyv)), (dzs_ref, _colsum(dz))]
        if from_loss:
            parts.append((loss_ref, lsum))

        @pl.when(i == 0)
        def _():
            for r, v in parts:
                r[...] = v

        @pl.when(i > 0)
        def _():
            for r, v in parts:
                r[...] += v

    row = pl.BlockSpec((ts, D), lambda i: (i, 0))
    vec = pl.BlockSpec((1, D), lambda i: (0, 0))
    in_specs = [row, pl.BlockSpec((ts, 1), lambda i: (i, 0)), vec]
    args = [xhat, rstd, g.reshape(1, D)]
    if from_loss:
        in_specs += [vec, row]
        args += [b.reshape(1, D), target]
    else:
        in_specs += [row]
        args += [dy]
    out_shape = [_sds((S, D), F32), _sds((1, D), F32), _sds((1, D), F32), _sds((1, D), F32)]
    out_specs = [row, vec, vec, vec]
    if from_loss:
        out_shape.append(_sds((1, 1), F32))
        out_specs.append(pl.BlockSpec((1, 1), lambda i: (0, 0)))
    return _call(body, grid=(S // ts,), in_specs=in_specs, out_specs=tuple(out_specs), out_shape=tuple(out_shape),
                 name=name)(*args)


def _glu(v):
    return v[:, :CONV_CH] * jax.nn.sigmoid(v[:, CONV_CH:])


def _conv_fwd(h_a, dw_w, dw_b, ln_g, ln_b, *, name, ts):
    S = h_a.shape[0]
    C, K, HB = CONV_CH, CONV_WIDTH, CONV_HALO
    RC = 128

    def body(h_ref, halo_ref, w_ref, b_ref, g_ref, bb_ref, out_ref, hc_ref, gbuf):
        i = pl.program_id(0)
        gbuf[0:HB, :] = jnp.where(i > 0, _glu(halo_ref[...]), 0.0)
        gbuf[HB:HB + ts, :] = _glu(h_ref[...])
        for r0 in range(0, ts, RC):
            acc = jnp.zeros((RC, C), F32) + b_ref[...]
            for k in range(K):
                acc = acc + w_ref[k:k + 1, :] * gbuf[pl.ds(r0 + HB - (K - 1) + k, RC), :]
            hc_ref[r0:r0 + RC, :] = acc
            xhat, _ = _ln_rows(acc)
            hn = xhat * g_ref[...] + bb_ref[...]
            out_ref[r0:r0 + RC, :] = (hn * jax.nn.sigmoid(hn)).astype(BF16)

    nb = ts // HB
    vec = pl.BlockSpec((1, C), lambda i: (0, 0))
    return _call(body, grid=(S // ts,),
                 in_specs=[pl.BlockSpec((ts, 2 * C), lambda i: (i, 0)),
                           pl.BlockSpec((HB, 2 * C), lambda i: (jnp.maximum(i * nb - 1, 0), 0)),
                           pl.BlockSpec((K, C), lambda i: (0, 0)), vec, vec, vec],
                 out_specs=(pl.BlockSpec((ts, C), lambda i: (i, 0)), pl.BlockSpec((ts, C), lambda i: (i, 0))),
                 out_shape=(_sds((S, C), BF16), _sds((S, C), F32)),
                 scratch_shapes=[pltpu.VMEM((HB + ts, C), F32)], name=name)(
        h_a, h_a, dw_w, dw_b.reshape(1, C), ln_g.reshape(1, C), ln_b.reshape(1, C))


def _conv_bwd(h_a, hc, dout, dw_w, ln_g, ln_b, *, name, ts):
    S = h_a.shape[0]
    C, K, HB = CONV_CH, CONV_WIDTH, CONV_HALO
    RC = 128
    n = S // ts

    def dconv_out(hc_v, do_v, g_ref, bb_ref):
        xhat, rstd = _ln_rows(hc_v)
        hn = xhat * g_ref[...] + bb_ref[...]
        sg = jax.nn.sigmoid(hn)
        dhn = do_v * (sg * (1.0 + hn * (1.0 - sg)))
        return _ln_bwd_rows(dhn * g_ref[...], xhat, rstd), dhn, xhat

    def body(h_ref, hprev_ref, hc_ref, hcnext_ref, do_ref, donext_ref, w_ref, g_ref, bb_ref,
             dh_ref, dw_ref, dwb_ref, dg_ref, db_ref, gbuf, dbuf):
        i = pl.program_id(0)
        hv = h_ref[...]
        gbuf[0:HB, :] = jnp.where(i > 0, _glu(hprev_ref[...]), 0.0)
        gbuf[HB:HB + ts, :] = _glu(hv)
        dhc, dhn, xhat = dconv_out(hc_ref[...], do_ref[...], g_ref, bb_ref)
        dhc_next, _, _ = dconv_out(hcnext_ref[...], donext_ref[...], g_ref, bb_ref)
        dbuf[0:ts, :] = dhc
        dbuf[ts:ts + HB, :] = jnp.where(i < n - 1, dhc_next, 0.0)
        dw_rows = []
        for k in range(K):
            acc_k = jnp.zeros((1, C), F32)
            for r0 in range(0, ts, RC):
                acc_k = acc_k + _colsum(dbuf[r0:r0 + RC, :] * gbuf[pl.ds(r0 + HB - (K - 1) + k, RC), :])
            dw_rows.append(acc_k)
        dw_rows.append(jnp.zeros((1, C), F32))
        dw_tile = jnp.concatenate(dw_rows, axis=0)
        for r0 in range(0, ts, RC):
            acc = jnp.zeros((RC, C), F32)
            for k in range(K):
                acc = acc + w_ref[k:k + 1, :] * dbuf[pl.ds(r0 + (K - 1) - k, RC), :]
            a = hv[r0:r0 + RC, :C]
            sg = jax.nn.sigmoid(hv[r0:r0 + RC, C:])
            dh_ref[r0:r0 + RC, :C] = (acc * sg).astype(BF16)
            dh_ref[r0:r0 + RC, C:] = (acc * a * sg * (1.0 - sg)).astype(BF16)
        parts = [(dw_ref, dw_tile), (dwb_ref, _colsum(dhc)), (dg_ref, _colsum(dhn * xhat)), (db_ref, _colsum(dhn))]

        @pl.when(i == 0)
        def _():
            for r, v in parts:
                r[...] = v

        @pl.when(i > 0)
        def _():
            for r, v in parts:
                r[...] += v

    nb = ts // HB
    last = S // HB - 1
    vec = pl.BlockSpec((1, C), lambda i: (0, 0))
    nxt = lambda i: (jnp.minimum((i + 1) * nb, last), 0)
    return _call(body, grid=(n,),
                 in_specs=[pl.BlockSpec((ts, 2 * C), lambda i: (i, 0)),
                           pl.BlockSpec((HB, 2 * C), lambda i: (jnp.maximum(i * nb - 1, 0), 0)),
                           pl.BlockSpec((ts, C), lambda i: (i, 0)), pl.BlockSpec((HB, C), nxt),
                           pl.BlockSpec((ts, C), lambda i: (i, 0)), pl.BlockSpec((HB, C), nxt),
                           pl.BlockSpec((K, C), lambda i: (0, 0)), vec, vec],
                 out_specs=(pl.BlockSpec((ts, 2 * C), lambda i: (i, 0)), pl.BlockSpec((K + 1, C), lambda i: (0, 0)),
                            vec, vec, vec),
                 out_shape=(_sds((S, 2 * C), BF16), _sds((K + 1, C), F32), _sds((1, C), F32), _sds((1, C), F32),
                            _sds((1, C), F32)),
                 scratch_shapes=[pltpu.VMEM((HB + ts, C), F32), pltpu.VMEM((ts + HB, C), F32)], name=name)(
        h_a, h_a, hc, hc, dout, dout, dw_w, ln_g.reshape(1, C), ln_b.reshape(1, C))


def _gmlp_mix(vn_bf, w_ref, mix_buf, ts):
    for ch in range(ts // CHUNK):
        for g in range(GMLP_GROUPS):
            vg = vn_bf[ch * CHUNK:(ch + 1) * CHUNK, g * GMLP_GROUP_DIM:(g + 1) * GMLP_GROUP_DIM]
            mix_buf[ch * CHUNK:(ch + 1) * CHUNK, g * GMLP_GROUP_DIM:(g + 1) * GMLP_GROUP_DIM] = jnp.dot(
                w_ref[g], vg, preferred_element_type=F32)


def _gmlp_fwd(h_c, ln_g, ln_b, w_tril, bs_rows, *, name, ts):
    S = h_c.shape[0]
    C = GMLP_CH

    def body(h_ref, g_ref, b_ref, w_ref, bs_ref, out_ref, mix_buf):
        hv = h_ref[...]
        xhat, _ = _ln_rows(hv[:, C:])
        vn = (xhat * g_ref[...] + b_ref[...]).astype(BF16)
        _gmlp_mix(vn, w_ref, mix_buf, ts)
        for ch in range(ts // CHUNK):
            rows = slice(ch * CHUNK, (ch + 1) * CHUNK)
            out_ref[rows, :] = (hv[rows, :C] * (mix_buf[rows, :] + bs_ref[...])).astype(BF16)

    vec = pl.BlockSpec((1, C), lambda i: (0, 0))
    return _call(body, grid=(S // ts,),
                 in_specs=[pl.BlockSpec((ts, 2 * C), lambda i: (i, 0)), vec, vec,
                           pl.BlockSpec((GMLP_GROUPS, CHUNK, CHUNK), lambda i: (0, 0, 0)),
                           pl.BlockSpec((CHUNK, C), lambda i: (0, 0))],
                 out_specs=pl.BlockSpec((ts, C), lambda i: (i, 0)), out_shape=_sds((S, C), BF16),
                 scratch_shapes=[pltpu.VMEM((ts, C), F32)], name=name)(
        h_c, ln_g.reshape(1, C), ln_b.reshape(1, C), w_tril, bs_rows)


def _gmlp_bwd(h_c, dout, ln_g, ln_b, w_tril, w_tril_t, bs_rows, *, name, ts):
    S = h_c.shape[0]
    C, G, GD = GMLP_CH, GMLP_GROUPS, GMLP_GROUP_DIM

    def body(h_ref, do_ref, g_ref, b_ref, w_ref, wt_ref, bs_ref, dh_ref, dw_ref, dbs_ref, dg_ref, db_ref,
             mix_buf, dvn_buf):
        i = pl.program_id(0)
        hv = h_ref[...]
        u = hv[:, :C]
        xhat, rstd = _ln_rows(hv[:, C:])
        vn = (xhat * g_ref[...] + b_ref[...]).astype(BF16)
        _gmlp_mix(vn, w_ref, mix_buf, ts)
        do = do_ref[...]
        dmixed = do * u
        dm_bf = dmixed.astype(BF16)
        lane = lax.broadcasted_iota(jnp.int32, (CHUNK, LANES), 1)
        dbs = jnp.zeros((CHUNK, LANES), F32)
        dws = [jnp.zeros((CHUNK, CHUNK), F32) for _ in range(G)]
        for ch in range(ts // CHUNK):
            rows = slice(ch * CHUNK, (ch + 1) * CHUNK)
            dh_ref[rows, :C] = (do[rows, :] * (mix_buf[rows, :] + bs_ref[...])).astype(BF16)
            for g in range(G):
                cols = slice(g * GD, (g + 1) * GD)
                dmg = dm_bf[rows, cols]
                dvn_buf[rows, cols] = jnp.dot(wt_ref[g], dmg, preferred_element_type=F32)
                dws[g] = dws[g] + lax.dot_general(dmg, vn[rows, cols], (((1,), (1,)), ((), ())),
                                                  preferred_element_type=F32)
                rs = jnp.sum(dmixed[rows, cols], axis=1, keepdims=True)
                dbs = dbs + jnp.where(lane == g, rs, 0.0)
        dvn = dvn_buf[...]
        dh_ref[:, C:] = _ln_bwd_rows(dvn * g_ref[...], xhat, rstd).astype(BF16)
        dgv, dbv = _colsum(dvn * xhat), _colsum(dvn)

        @pl.when(i == 0)
        def _():
            for g in range(G):
                dw_ref[g] = dws[g]
            dbs_ref[...] = dbs
            dg_ref[...] = dgv
            db_ref[...] = dbv

        @pl.when(i > 0)
        def _():
            for g in range(G):
                dw_ref[g] += dws[g]
            dbs_ref[...] += dbs
            dg_ref[...] += dgv
            db_ref[...] += dbv

    vec = pl.BlockSpec((1, C), lambda i: (0, 0))
    wspec = pl.BlockSpec((G, CHUNK, CHUNK), lambda i: (0, 0, 0))
    return _call(body, grid=(S // ts,),
                 in_specs=[pl.BlockSpec((ts, 2 * C), lambda i: (i, 0)),
                           pl.BlockSpec((ts, C), lambda i: (i, 0)), vec, vec, wspec, wspec,
                           pl.BlockSpec((CHUNK, C), lambda i: (0, 0))],
                 out_specs=(pl.BlockSpec((ts, 2 * C), lambda i: (i, 0)), wspec,
                            pl.BlockSpec((CHUNK, LANES), lambda i: (0, 0)), vec, vec),
                 out_shape=(_sds((S, 2 * C), BF16), _sds((G, CHUNK, CHUNK), F32), _sds((CHUNK, LANES), F32),
                            _sds((1, C), F32), _sds((1, C), F32)),
                 scratch_shapes=[pltpu.VMEM((ts, C), F32), pltpu.VMEM((ts, C), F32)], name=name)(
        h_c, dout, ln_g.reshape(1, C), ln_b.reshape(1, C), w_tril, w_tril_t, bs_rows)


def _conv3(buf, w_ref, b_ref, rows, off):
    return (w_ref[0:1, :] * buf[pl.ds(off - 2, rows), :] + w_ref[1:2, :] * buf[pl.ds(off - 1, rows), :]
            + w_ref[2:3, :] * buf[pl.ds(off, rows), :]) + b_ref[...]


def _ffn_act_fwd(hu, cw, cb, *, name, ts, tc):
    S, F2 = hu.shape
    F = F2 // 2
    nj = F // tc
    HB = FFN_HALO
    nb = ts // HB

    def body(g_ref, v_ref, gh_ref, vh_ref, wg_ref, wv_ref, bg_ref, bv_ref, act_ref, dg_ref, dv_ref, gbuf, vbuf):
        i = pl.program_id(0)
        gbuf[0:HB, :] = jnp.where(i > 0, gh_ref[...], 0.0)
        gbuf[HB:HB + ts, :] = g_ref[...]
        vbuf[0:HB, :] = jnp.where(i > 0, vh_ref[...], 0.0)
        vbuf[HB:HB + ts, :] = v_ref[...]
        for r0 in range(0, ts, FFN_ROWS):
            rows = slice(r0, r0 + FFN_ROWS)
            gc = _conv3(gbuf, wg_ref, bg_ref, FFN_ROWS, HB + r0)
            vc = _conv3(vbuf, wv_ref, bv_ref, FFN_ROWS, HB + r0)
            sg = jax.nn.sigmoid(gc)
            silu = gc * sg
            act_ref[rows, :] = (silu * vc).astype(BF16)
            dg_ref[rows, :] = (vc * (sg * (1.0 + gc * (1.0 - sg)))).astype(BF16)
            dv_ref[rows, :] = silu.astype(BF16)

    prev = lambda off: (lambda i, j: (jnp.maximum(i * nb - 1, 0), j + off))
    out = pl.BlockSpec((ts, tc), lambda i, j: (i, j))
    return _call(body, grid=(S // ts, nj),
                 in_specs=[pl.BlockSpec((ts, tc), lambda i, j: (i, j)), pl.BlockSpec((ts, tc), lambda i, j: (i, j + nj)),
                           pl.BlockSpec((HB, tc), prev(0)), pl.BlockSpec((HB, tc), prev(nj)),
                           pl.BlockSpec((3, tc), lambda i, j: (0, j)), pl.BlockSpec((3, tc), lambda i, j: (0, j + nj)),
                           pl.BlockSpec((1, tc), lambda i, j: (0, j)), pl.BlockSpec((1, tc), lambda i, j: (0, j + nj))],
                 out_specs=(out, out, out), out_shape=(_sds((S, F), BF16),) * 3,
                 scratch_shapes=[pltpu.VMEM((HB + ts, tc), F32), pltpu.VMEM((HB + ts, tc), F32)], name=name)(
        hu, hu, hu, hu, cw, cw, cb.reshape(1, F2), cb.reshape(1, F2))


def _ffn_act_bwd(hu, dact, dact_dg, dact_dv, cw, *, name, ts, tc):
    S, F2 = hu.shape
    F = F2 // 2
    nj = F // tc
    HB = FFN_HALO
    HB16 = 16
    n = S // ts

    def body(g_ref, v_ref, lg_ref, lv_ref, lgn_ref, lvn_ref, da_ref, dan_ref, wg_ref, wv_ref,
             dhg_ref, dhv_ref, dwg_ref, dwv_ref, dbg_ref, dbv_ref, dug_ref, duv_ref, dgb, dvb, accg, accv):
        i = pl.program_id(1)
        RC = FFN_ROWS
        for r0 in range(0, ts, RC):
            rows = slice(r0, r0 + RC)
            da = da_ref[rows, :]
            dgb[rows, :] = da * lg_ref[rows, :].astype(F32)
            dvb[rows, :] = da * lv_ref[rows, :].astype(F32)
        dan = jnp.where(i < n - 1, dan_ref[...], 0.0)
        dgb[ts:ts + HB, :] = dan * lgn_ref[...].astype(F32)[0:HB, :]
        dvb[ts:ts + HB, :] = dan * lvn_ref[...].astype(F32)[0:HB, :]

        @pl.when(i == 0)
        def _():
            accg[...] = jnp.zeros_like(accg)
            accv[...] = jnp.zeros_like(accv)

        def fold(v):
            out = v[0:8, :]
            for r in range(8, RC, 8):
                out = out + v[r:r + 8, :]
            return out

        for dbuf, h_ref, w_ref, dh_ref, acc in ((dgb, g_ref, wg_ref, dhg_ref, accg), (dvb, v_ref, wv_ref, dhv_ref, accv)):
            for r0 in range(0, ts, RC):
                taps = [dbuf[r0 + 2 - k:r0 + 2 - k + RC, :] for k in range(3)]
                dhu = w_ref[2:3, :] * taps[2] + w_ref[1:2, :] * taps[1] + w_ref[0:1, :] * taps[0]
                dh_ref[r0:r0 + RC, :] = dhu.astype(BF16)
                h = h_ref[r0:r0 + RC, :]
                for k in range(3):
                    acc[8 * k:8 * k + 8, :] += fold(h * taps[k])
                acc[24:32, :] += fold(taps[2])
                acc[32:40, :] += fold(dhu)

        @pl.when(i == n - 1)
        def _():
            for acc, dw_ref, db_ref, du_ref in ((accg, dwg_ref, dbg_ref, dug_ref), (accv, dwv_ref, dbv_ref, duv_ref)):
                for k in range(3):
                    dw_ref[k:k + 1, :] = _colsum(acc[8 * k:8 * k + 8, :])
                db_ref[...] = _colsum(acc[24:32, :])
                du_ref[...] = _colsum(acc[32:40, :])

    nxt = lambda hb: (lambda j, i: (jnp.minimum((i + 1) * (ts // hb), S // hb - 1), j))
    tile = lambda off: (lambda j, i: (i, j + off))
    vec = lambda rows: pl.BlockSpec((rows, tc), lambda j, i: (0, j))
    return _call(body, grid=(nj, n),
                 in_specs=[pl.BlockSpec((ts, tc), tile(0)), pl.BlockSpec((ts, tc), tile(nj)),
                           pl.BlockSpec((ts, tc), tile(0)), pl.BlockSpec((ts, tc), tile(0)),
                           pl.BlockSpec((HB16, tc), nxt(HB16)), pl.BlockSpec((HB16, tc), nxt(HB16)),
                           pl.BlockSpec((ts, tc), tile(0)), pl.BlockSpec((HB, tc), nxt(HB)),
                           pl.BlockSpec((3, tc), lambda j, i: (0, j)), pl.BlockSpec((3, tc), lambda j, i: (0, j + nj))],
                 out_specs=(pl.BlockSpec((ts, tc), tile(0)), pl.BlockSpec((ts, tc), tile(0)),
                            vec(3), vec(3), vec(1), vec(1), vec(1), vec(1)),
                 out_shape=(_sds((S, F), BF16), _sds((S, F), BF16), _sds((3, F), F32), _sds((3, F), F32),
                            _sds((1, F), F32), _sds((1, F), F32), _sds((1, F), F32), _sds((1, F), F32)),
                 scratch_shapes=[pltpu.VMEM((ts + HB, tc), F32), pltpu.VMEM((ts + HB, tc), F32),
                                 pltpu.VMEM((40, tc), F32), pltpu.VMEM((40, tc), F32)], name=name)(
        hu, hu, dact_dg, dact_dv, dact_dg, dact_dv, dact, dact, cw, cw)


def _t5_bucket(dist):
    max_exact = N_BUCKETS // 2
    d = np.maximum(dist, 1).astype(np.float64)
    large = max_exact + (np.log(d / max_exact) / math.log(MAX_DISTANCE / max_exact)
                         * (N_BUCKETS - max_exact)).astype(np.int32)
    large = np.minimum(large, N_BUCKETS - 1)
    return np.where(dist < max_exact, dist, large).astype(np.int32)


def _pattern_tables(window, dilation):
    qi = np.arange(ATTN_BLOCK)[:, None]
    kj = np.arange(2 * ATTN_BLOCK)[None, :]
    dist = qi + ATTN_BLOCK - kj
    valid = (dist >= 0) & (dist <= window // dilation)
    bucket = _t5_bucket(np.clip(dist, 0, None) * dilation)
    return bucket, valid


def _dilate_qkv(qkv, *, name, ts):
    S, C = qkv.shape
    dils = [d for (_, d) in PATTERNS if d > 1]

    def body(x_ref, nat_ref, *rest):
        outs, tmp = rest[:-1], rest[-1]
        nat_ref[...] = x_ref[...].astype(BF16)
        for j in range(C // LANES):
            cols = slice(j * LANES, (j + 1) * LANES)
            tmp[...] = x_ref[:, cols]
            for d, o_ref in zip(dils, outs):
                _dilate(tmp, o_ref, cols, d, ts, BF16)

    out_shape = (_sds((S, C), BF16),) + tuple(_sds((d, S // d, C), BF16) for d in dils)
    out_specs = (_dil_spec(1, ts, C),) + tuple(_dil_spec(d, ts, C) for d in dils)
    res = _call(body, grid=(S // ts,), in_specs=[_dil_spec(1, ts, C)], out_specs=out_specs, out_shape=out_shape,
                scratch_shapes=[pltpu.VMEM((ts, LANES), F32)], name=name)(qkv)
    return [res[0]] + [r.reshape(S, C) for r in res[1:]]


def _attn_group(S):
    nb_min = (S // PATTERNS[-1][1]) // ATTN_BLOCK
    return math.gcd(8, nb_min)


def _attn_fwd(qkv, bias, *, name, nb, G):
    S = qkv.shape[0]
    B, HD = ATTN_BLOCK, HEAD_DIM
    GR = G * B
    ng = S // GR

    def body(q_ref, k_ref, kh_ref, v_ref, vh_ref, bias_ref, o_ref, lse_ref, kbuf, vbuf):
        g = pl.program_id(1)
        halo_ok = (g * G) % nb != 0
        kbuf[0:B, :] = kh_ref[...]
        kbuf[B:B + GR, :] = k_ref[...]
        vbuf[0:B, :] = vh_ref[...]
        vbuf[B:B + GR, :] = v_ref[...]
        col = lax.broadcasted_iota(jnp.int32, (B, 2 * B), 1)
        head0 = lax.broadcasted_iota(jnp.int32, (B, LANES), 1) < HD

        for bi in range(G):
            r0 = bi * B
            q2 = q_ref[r0:r0 + B, :]
            kk = kbuf[r0:r0 + 2 * B, :]
            vv = vbuf[r0:r0 + 2 * B, :]
            zero = jnp.zeros_like(q2)
            os_, ls_, lses = [], [], []
            for hh in range(2):
                qh = jnp.where(head0, q2, zero) if hh == 0 else jnp.where(head0, zero, q2)
                s = lax.dot_general(qh, kk, (((1,), (1,)), ((), ())), preferred_element_type=F32)
                s = s + bias_ref[hh]
                if bi == 0:
                    s = jnp.where(jnp.logical_and(jnp.logical_not(halo_ok), col < B), NEG, s)
                m = jnp.max(s, axis=1, keepdims=True)
                p = jnp.exp(s - m)
                l = jnp.sum(p, axis=1, keepdims=True)
                os_.append(jnp.dot(p.astype(BF16), vv, preferred_element_type=F32))
                ls_.append(l)
                lses.append(m + jnp.log(l))
            o_ref[r0:r0 + B, :] = (jnp.where(head0, os_[0], os_[1]) / jnp.where(head0, ls_[0], ls_[1])).astype(BF16)
            lse_ref[r0:r0 + B, :] = jnp.where(head0, lses[0], lses[1])

    halo = lambda off: (lambda hp, g: (jnp.maximum(g * G - 1, 0), off + hp))
    main = lambda off: (lambda hp, g: (g, off + hp))
    return _call(body, grid=(4, ng),
                 in_specs=[pl.BlockSpec((GR, LANES), main(0)), pl.BlockSpec((GR, LANES), main(4)),
                           pl.BlockSpec((B, LANES), halo(4)), pl.BlockSpec((GR, LANES), main(8)),
                           pl.BlockSpec((B, LANES), halo(8)), pl.BlockSpec((2, B, 2 * B), lambda hp, g: (hp, 0, 0))],
                 out_specs=(pl.BlockSpec((GR, LANES), main(0)), pl.BlockSpec((GR, LANES), main(0))),
                 out_shape=(_sds((S, ATTN_CH), BF16), _sds((S, ATTN_CH), F32)),
                 scratch_shapes=[pltpu.VMEM((B + GR, LANES), BF16), pltpu.VMEM((B + GR, LANES), BF16)], name=name)(
        qkv, qkv, qkv, qkv, qkv, bias)


def _dil_spec(d, ts, C):
    if d == 1:
        return pl.BlockSpec((ts, C), lambda i: (i, 0))
    return pl.BlockSpec((d, ts // d, C), lambda i: (0, i, 0))


def _dil_view(a, d):
    return a if d == 1 else a.reshape(d, a.shape[0] // d, a.shape[1])


def _dilate(nat_tmp, dst_ref, cols, d, ts, dtype=F32):
    for r in range(d):
        dst_ref[r, :, cols] = nat_tmp[pl.ds(r, ts // d, stride=d), :].astype(dtype)


def _undilate(src_ref, nat_tmp, cols, d, ts, accumulate=False):
    for r in range(d):
        rows = pl.ds(r, ts // d, stride=d)
        if accumulate:
            nat_tmp[rows, :] = nat_tmp[rows, :] + src_ref[r, :, cols].astype(F32)
        else:
            nat_tmp[rows, :] = src_ref[r, :, cols].astype(F32)


def _attn_merge(o_list, lse_list, dils, *, name, ts):
    S, C = o_list[0].shape
    P = len(o_list)
    nd = sum(1 for d in dils if d > 1)

    def body(*refs):
        o_refs, l_refs = refs[:P], refs[P:2 * P]
        out_ref, lse_ref = refs[2 * P], refs[2 * P + 1]
        lse_d_refs = refs[2 * P + 2:2 * P + 2 + nd]
        scratch = list(refs[2 * P + 2 + nd:])
        tmp = scratch.pop()
        for j in range(C // LANES):
            cols = slice(j * LANES, (j + 1) * LANES)
            os_, ls, free = [], [], list(scratch)
            for o_ref, l_ref, d in zip(o_refs, l_refs, dils):
                if d > 1:
                    so, sl = free.pop(0), free.pop(0)
                    _undilate(o_ref, so, cols, d, ts)
                    _undilate(l_ref, sl, cols, d, ts)
                    os_.append(so[...])
                    ls.append(sl[...])
                else:
                    os_.append(o_ref[:, cols].astype(F32))
                    ls.append(l_ref[:, cols])
            m = ls[0]
            for l in ls[1:]:
                m = jnp.maximum(m, l)
            ws = [jnp.exp(l - m) for l in ls]
            den = ws[0]
            for w in ws[1:]:
                den = den + w
            num = ws[0] * os_[0]
            for w, o in zip(ws[1:], os_[1:]):
                num = num + w * o
            out_ref[:, cols] = num / den
            tmp[...] = m + jnp.log(den)
            lse_ref[:, cols] = tmp[...]
            for l_out, d in zip(lse_d_refs, [d for d in dils if d > 1]):
                _dilate(tmp, l_out, cols, d, ts)

    row = _dil_spec(1, ts, C)
    dd = [d for d in dils if d > 1]
    res = _call(body, grid=(S // ts,), in_specs=[_dil_spec(d, ts, C) for d in dils] * 2,
                out_specs=(row, row) + tuple(_dil_spec(d, ts, C) for d in dd),
                out_shape=(_sds((S, C), F32), _sds((S, C), F32)) + tuple(_sds((d, S // d, C), F32) for d in dd),
                scratch_shapes=[pltpu.VMEM((ts, LANES), F32)] * (2 * nd + 1), name=name)(
        *[_dil_view(o, d) for o, d in zip(o_list, dils)], *[_dil_view(l, d) for l, d in zip(lse_list, dils)])
    lse_by_d = {1: res[1]}
    lse_by_d.update({d: r.reshape(S, C) for d, r in zip(dd, res[2:])})
    return res[0], [lse_by_d[d] for d in dils]


def _attn_prep(dout, out, dils, *, name, ts):
    S, C = out.shape
    HD = HEAD_DIM
    dd = [d for d in dils if d > 1]
    nd = len(dd)

    def body(do_ref, o_ref, d_ref, dob_ref, *rest):
        outs, tmp_d, tmp_o = rest[:2 * nd], rest[2 * nd], rest[2 * nd + 1]
        dob_ref[...] = do_ref[...].astype(BF16)
        for j in range(C // LANES):
            cols = slice(j * LANES, (j + 1) * LANES)
            do = do_ref[:, cols]
            prod = do * o_ref[:, cols]
            tmp_o[...] = do
            for h in range(LANES // HD):
                cs = slice(h * HD, (h + 1) * HD)
                tmp_d[:, cs] = jnp.broadcast_to(jnp.sum(prod[:, cs], axis=1, keepdims=True), (ts, HD))
            d_ref[:, cols] = tmp_d[...]
            for d, dd_out, do_out in zip(dd, outs[:nd], outs[nd:]):
                _dilate(tmp_d, dd_out, cols, d, ts)
                _dilate(tmp_o, do_out, cols, d, ts, BF16)

    row = _dil_spec(1, ts, C)
    res = _call(body, grid=(S // ts,), in_specs=[row, row],
                out_specs=(row, row) + tuple(_dil_spec(d, ts, C) for d in dd) * 2,
                out_shape=(_sds((S, C), F32), _sds((S, C), BF16)) + tuple(_sds((d, S // d, C), F32) for d in dd)
                + tuple(_sds((d, S // d, C), BF16) for d in dd),
                scratch_shapes=[pltpu.VMEM((ts, LANES), F32)] * 2, name=name)(dout, out)
    dd_by_d, do_by_d = {1: res[0]}, {1: res[1]}
    dd_by_d.update({d: r.reshape(S, C) for d, r in zip(dd, res[2:2 + nd])})
    do_by_d.update({d: r.reshape(S, C) for d, r in zip(dd, res[2 + nd:])})
    return [dd_by_d[d] for d in dils], [do_by_d[d] for d in dils]


def _attn_bwd(qkv, do, lse, dd, bias, *, name, nb, G):
    S = qkv.shape[0]
    B, HD = ATTN_BLOCK, HEAD_DIM
    GR = G * B
    ng = S // GR
    nblk = S // B

    def body(q_ref, k_ref, kh_ref, v_ref, vh_ref, do_ref, lse_ref, dd_ref, qn_ref, don_ref, lsen_ref, ddn_ref,
             bias_ref, dq_ref, dk_ref, dv_ref, dbias_ref, kbuf, vbuf, dkp, dvp, dko_s, dvo_s):
        g = pl.program_id(1)
        halo_ok = (g * G) % nb != 0
        next_ok = jnp.logical_and(((g + 1) * G) % nb != 0, g < ng - 1)
        kbuf[0:B, :] = kh_ref[...]
        kbuf[B:B + GR, :] = k_ref[...]
        vbuf[0:B, :] = vh_ref[...]
        vbuf[B:B + GR, :] = v_ref[...]
        col = lax.broadcasted_iota(jnp.int32, (B, 2 * B), 1)

        def tn_dot(a, b):
            return lax.dot_general(a, b, (((0,), (0,)), ((), ())), preferred_element_type=F32)

        @pl.when(g == 0)
        def _():
            dbias_ref[...] = jnp.zeros_like(dbias_ref)

        head0 = lax.broadcasted_iota(jnp.int32, (B, LANES), 1) < HD

        def nt_dot(a, b):
            return lax.dot_general(a, b, (((1,), (1,)), ((), ())), preferred_element_type=F32)

        def one_head(x, hh):
            zero = jnp.zeros_like(x)
            return jnp.where(head0, x, zero) if hh == 0 else jnp.where(head0, zero, x)

        def pick(per_head):
            return jnp.where(head0, per_head[0], per_head[1])

        for bi in range(G):
            r0 = bi * B
            q2 = q_ref[r0:r0 + B, :]
            do2 = do_ref[r0:r0 + B, :]
            kk = kbuf[r0:r0 + 2 * B, :]
            vv = vbuf[r0:r0 + 2 * B, :]
            dq, dko, dvo, dkq, dvq = [], [], [], [], []
            for hh in range(2):
                s = nt_dot(one_head(q2, hh), kk) + bias_ref[hh]
                if bi == 0:
                    s = jnp.where(jnp.logical_and(jnp.logical_not(halo_ok), col < B), NEG, s)
                p = jnp.exp(s - lse_ref[r0:r0 + B, hh * HD:hh * HD + 1])
                dp = nt_dot(one_head(do2, hh), vv)
                ds = p * (dp - dd_ref[r0:r0 + B, hh * HD:hh * HD + 1])
                dbias_ref[hh] += ds
                ds_bf, p_bf = ds.astype(BF16), p.astype(BF16)
                dq.append(jnp.dot(ds_bf, kk, preferred_element_type=F32))
                dko.append(tn_dot(ds_bf[:, B:], q2))
                dvo.append(tn_dot(p_bf[:, B:], do2))
                if bi > 0:
                    dkq.append(tn_dot(ds_bf[:, :B], q2))
                    dvq.append(tn_dot(p_bf[:, :B], do2))
            dq_ref[r0:r0 + B, :] = pick(dq).astype(BF16)
            dko_s[r0:r0 + B, :] = pick(dko)
            dvo_s[r0:r0 + B, :] = pick(dvo)
            if bi > 0:
                dkp[r0 - B:r0, :] = pick(dkq)
                dvp[r0 - B:r0, :] = pick(dvq)

        @pl.when(next_ok)
        def _():
            qn = qn_ref[...]
            don = don_ref[...]
            kl = kbuf[GR:GR + B, :]
            vl = vbuf[GR:GR + B, :]
            dkq, dvq = [], []
            for hh in range(2):
                s = nt_dot(one_head(qn, hh), kl) + bias_ref[hh, :, 0:B]
                p = jnp.exp(s - lsen_ref[:, hh * HD:hh * HD + 1])
                dp = nt_dot(one_head(don, hh), vl)
                ds = p * (dp - ddn_ref[:, hh * HD:hh * HD + 1])
                dkq.append(tn_dot(ds.astype(BF16), qn))
                dvq.append(tn_dot(p.astype(BF16), don))
            dkp[GR - B:GR, :] = pick(dkq)
            dvp[GR - B:GR, :] = pick(dvq)

        @pl.when(jnp.logical_not(next_ok))
        def _():
            dkp[GR - B:GR, :] = jnp.zeros((B, LANES), F32)
            dvp[GR - B:GR, :] = jnp.zeros((B, LANES), F32)

        dk_ref[...] = (dko_s[...] + dkp[...]).astype(BF16)
        dv_ref[...] = (dvo_s[...] + dvp[...]).astype(BF16)

    halo = lambda off: (lambda hp, g: (jnp.maximum(g * G - 1, 0), off + hp))
    main = lambda off: (lambda hp, g: (g, off + hp))
    nxt = lambda off: (lambda hp, g: (jnp.minimum((g + 1) * G, nblk - 1), off + hp))
    big, small = (lambda m: pl.BlockSpec((GR, LANES), m)), (lambda m: pl.BlockSpec((B, LANES), m))
    return _call(body, grid=(4, ng),
                 in_specs=[big(main(0)), big(main(4)), small(halo(4)), big(main(8)), small(halo(8)),
                           big(main(0)), big(main(0)), big(main(0)),
                           small(nxt(0)), small(nxt(0)), small(nxt(0)), small(nxt(0)),
                           pl.BlockSpec((2, B, 2 * B), lambda hp, g: (hp, 0, 0))],
                 out_specs=(big(main(0)), big(main(0)), big(main(0)),
                            pl.BlockSpec((2, B, 2 * B), lambda hp, g: (hp, 0, 0))),
                 out_shape=(_sds((S, ATTN_CH), BF16),) * 3 + (_sds((ATTN_HEADS, B, 2 * B), F32),),
                 scratch_shapes=[pltpu.VMEM((B + GR, LANES), BF16), pltpu.VMEM((B + GR, LANES), BF16)]
                 + [pltpu.VMEM((GR, LANES), F32)] * 4, name=name)(
        qkv, qkv, qkv, qkv, qkv, do, lse, dd, qkv, do, lse, dd, bias)


def _attn_combine(dq_list, dk_list, dv_list, dils, *, name, ts):
    S, C = dq_list[0].shape
    P = len(dq_list)
    scale = HEAD_DIM ** -0.5
    assert dils[0] == 1

    def body(*refs):
        out_refs, acc = refs[3 * P:3 * P + 3], refs[3 * P + 3]
        for part in range(3):
            for j in range(C // LANES):
                cols = slice(j * LANES, (j + 1) * LANES)
                acc[...] = refs[part * P][:, cols].astype(F32)
                for r, d in zip(refs[part * P + 1:(part + 1) * P], dils[1:]):
                    _undilate(r, acc, cols, d, ts, accumulate=True)
                tot = acc[...]
                if part == 0:
                    tot = tot * scale
                out_refs[part][:, cols] = tot.astype(BF16)

    return _call(body, grid=(S // ts,), in_specs=[_dil_spec(d, ts, C) for d in dils] * 3,
                 out_specs=(_dil_spec(1, ts, C),) * 3, out_shape=(_sds((S, C), BF16),) * 3,
                 scratch_shapes=[pltpu.VMEM((ts, LANES), F32)], name=name)(
        *[_dil_view(a, d) for lst in (dq_list, dk_list, dv_list) for a, d in zip(lst, dils)])


def _bias_tables(table, bucket_flat, *, name):
    P, _, K = bucket_flat.shape
    H = table.shape[1]
    KC = 4096

    def body(t_ref, bk_ref, out_ref):
        row = lax.broadcasted_iota(jnp.int32, (N_BUCKETS, KC), 0)
        for c in range(K // KC):
            bk = bk_ref[0, :, c * KC:(c + 1) * KC]
            onehot = (row == bk).astype(F32)
            vals = jnp.dot(t_ref[...], onehot, preferred_element_type=F32, precision=lax.Precision.HIGHEST)
            out_ref[0, :, c * KC:(c + 1) * KC] = jnp.where(bk >= 0, vals, NEG)

    return _call(body, grid=(P,),
                 in_specs=[pl.BlockSpec((H, N_BUCKETS), lambda p: (0, 0)), pl.BlockSpec((1, 1, K), lambda p: (p, 0, 0))],
                 out_specs=pl.BlockSpec((1, H, K), lambda p: (p, 0, 0)), out_shape=_sds((P, H, K), F32),
                 name=name)(table.T, bucket_flat)


def _bias_grad(dbias_flat, bucket_flat, *, name):
    P, H, K = dbias_flat.shape
    KC = 4096

    def body(db_ref, bk_ref, out_ref):
        p = pl.program_id(0)
        acc = jnp.zeros((N_BUCKETS, H), F32)
        row = lax.broadcasted_iota(jnp.int32, (N_BUCKETS, KC), 0)
        for c in range(K // KC):
            onehot = (row == bk_ref[0, :, c * KC:(c + 1) * KC]).astype(F32)
            acc = acc + lax.dot_general(onehot, db_ref[0, :, c * KC:(c + 1) * KC], (((1,), (1,)), ((), ())),
                                        preferred_element_type=F32, precision=lax.Precision.HIGHEST)

        @pl.when(p == 0)
        def _():
            out_ref[...] = acc

        @pl.when(p > 0)
        def _():
            out_ref[...] += acc

    return _call(body, grid=(P,),
                 in_specs=[pl.BlockSpec((1, H, K), lambda p: (p, 0, 0)), pl.BlockSpec((1, 1, K), lambda p: (p, 0, 0))],
                 out_specs=pl.BlockSpec((N_BUCKETS, H), lambda p: (0, 0)), out_shape=_sds((N_BUCKETS, H), F32),
                 name=name)(dbias_flat, bucket_flat)


def _allgather(blocks, *, name):
    n = len(blocks)

    def body(*refs):
        x_refs, out_refs = refs[:n], refs[n:2 * n]
        send_sems, recv_sems, local_sems = refs[2 * n:]
        x, y, c = lax.axis_index("x"), lax.axis_index("y"), lax.axis_index("c")
        me, sibling = (x, y, c), (x, y, 1 - c)
        chips = [(1 - x, y), (x, 1 - y), (1 - x, 1 - y)]

        def copy(i, k, blk, to, own=False):
            slot = out_refs[i].at[4 * blk[0] + 2 * blk[1] + blk[2]]
            return pltpu.make_async_remote_copy(src_ref=x_refs[i] if own else slot, dst_ref=slot,
                                                send_sem=send_sems.at[7 * i + k], recv_sem=recv_sems.at[7 * i + k],
                                                device_id=to, device_id_type=MESH)

        mine = [pltpu.make_async_copy(x_refs[i], out_refs[i].at[4 * x + 2 * y + c], local_sems.at[i])
                for i in range(n)]
        for cp in mine:
            cp.start()
        first = []
        for i in range(n):
            first.append(copy(i, 0, me, sibling, own=True))
            first += [copy(i, 1 + j, me, (*chip, c), own=True) for j, chip in enumerate(chips)]
        for cp in first:
            cp.start()
        passed = []
        for j, chip in enumerate(chips):
            for i in range(n):
                copy(i, 1 + j, (*chip, c), me).wait_recv()
                fwd = copy(i, 4 + j, (*chip, c), sibling)
                fwd.start()
                passed.append(fwd)
        for i in range(n):
            copy(i, 0, sibling, me).wait_recv()
            for j, chip in enumerate(chips):
                copy(i, 4 + j, (*chip, 1 - c), me).wait_recv()
        for cp in first + passed:
            cp.wait_send()
        for cp in mine:
            cp.wait()

    any_spec = pl.BlockSpec(memory_space=pl.ANY)
    return pl.pallas_call(body, out_shape=tuple(_sds((N_DEV,) + b.shape, b.dtype) for b in blocks),
                          in_specs=[any_spec] * n, out_specs=(any_spec,) * n,
                          scratch_shapes=[pltpu.SemaphoreType.DMA((7 * n,)), pltpu.SemaphoreType.DMA((7 * n,)),
                                          pltpu.SemaphoreType.DMA((n,))], name=name)(*blocks)


def _exchange(sends, *, name):
    n = len(sends)

    def body(*refs):
        send_refs, recv_refs = refs[:n], refs[n:2 * n]
        send_sems, recv_sems, local_sems = refs[2 * n:]
        x, y, c = lax.axis_index("x"), lax.axis_index("y"), lax.axis_index("c")
        me = 4 * x + 2 * y + c
        mine = [pltpu.make_async_copy(send_refs[i].at[me], recv_refs[i].at[me], local_sems.at[i]) for i in range(n)]
        for cp in mine:
            cp.start()
        copies = []
        for k in range(1, N_DEV):
            px = 1 - x if k & 4 else x
            py = 1 - y if k & 2 else y
            pc = 1 - c if k & 1 else c
            for i in range(n):
                cp = pltpu.make_async_remote_copy(src_ref=send_refs[i].at[4 * px + 2 * py + pc],
                                                  dst_ref=recv_refs[i].at[me],
                                                  send_sem=send_sems.at[7 * i + k - 1],
                                                  recv_sem=recv_sems.at[7 * i + k - 1],
                                                  device_id=(px, py, pc), device_id_type=MESH)
                cp.start()
                copies.append(cp)
        for cp in copies:
            cp.wait_recv()
        for cp in copies:
            cp.wait_send()
        for cp in mine:
            cp.wait()

    any_spec = pl.BlockSpec(memory_space=pl.ANY)
    return pl.pallas_call(body, out_shape=tuple(_sds(s.shape, s.dtype) for s in sends), in_specs=[any_spec] * n,
                          out_specs=(any_spec,) * n,
                          scratch_shapes=[pltpu.SemaphoreType.DMA((7 * n,)), pltpu.SemaphoreType.DMA((7 * n,)),
                                          pltpu.SemaphoreType.DMA((n,))], name=name)(*sends)


def _adamw(w, m, v, g_parts, *, name, tr):
    R, W = w.shape
    bc1 = 1.0 - ADAM_B1 ** ADAM_STEP
    bc2 = 1.0 - ADAM_B2 ** ADAM_STEP

    def body(w_ref, m_ref, v_ref, g_ref, go_ref, d_ref, mo_ref, vo_ref):
        g = g_ref[0].astype(F32)
        for i in range(1, N_DEV):
            g = g + g_ref[i].astype(F32)
        mn = ADAM_B1 * m_ref[...] + (1.0 - ADAM_B1) * g
        vn = ADAM_B2 * v_ref[...] + (1.0 - ADAM_B2) * (g * g)
        m_hat = mn / bc1
        v_hat = vn / bc2
        go_ref[...] = g
        d_ref[...] = -ADAM_LR * (m_hat / (jnp.sqrt(v_hat) + ADAM_EPS) + ADAM_WD * w_ref[...])
        mo_ref[...] = mn
        vo_ref[...] = vn

    row = pl.BlockSpec((tr, W), lambda i: (i, 0))
    return _call(body, grid=(R // tr,), in_specs=[row, row, row, pl.BlockSpec((N_DEV, tr, W), lambda i: (0, i, 0))],
                 out_specs=(row,) * 4, out_shape=(_sds((R, W), F32),) * 4, name=name)(w, m, v, g_parts)


def _round_up(n, k):
    return -(-n // k) * k


def _pack(arrs, width, row_mult):
    pieces, offs, r = [], [], 0
    for a in arrs:
        n = a.size
        rows = _round_up(n, width) // width
        flat = a.reshape(-1)
        if rows * width != n:
            flat = jnp.pad(flat, (0, rows * width - n))
        pieces.append(flat.reshape(rows, width))
        offs.append((r, rows, n))
        r += rows
    total = _round_up(r, row_mult)
    if total != r:
        pieces.append(jnp.zeros((total - r, width), pieces[0].dtype))
    return jnp.concatenate(pieces, axis=0), offs


def _unpack(pack, offs, shapes):
    out = []
    for (r, rows, n), shp in zip(offs, shapes):
        out.append(pack[r:r + rows].reshape(-1)[:n].reshape(shp))
    return out


def _gather_axis(full8, axis):
    moved = jnp.moveaxis(full8, 0, axis)
    shp = list(moved.shape)
    shp[axis:axis + 2] = [shp[axis] * shp[axis + 1]]
    return moved.reshape(shp)


def _split_axis(full, axis):
    shp = list(full.shape)
    shp[axis:axis + 1] = [N_DEV, shp[axis] // N_DEV]
    return jnp.moveaxis(full.reshape(shp), axis, 0)


def kernel(x, w_in, b_in, conv_dw_w, conv_dw_b, conv_ln_g, conv_ln_b, rel_bias_table, gmlp_ln_g, gmlp_ln_b, gmlp_w_s, gmlp_b_s, w_out, b_out, ln1_g, ln1_b, ffn_w_up, ffn_b_up, ffn_conv_w, ffn_conv_b, ffn_w_down, ffn_b_down, ln2_g, ln2_b, loss_target, m_w_in, m_b_in, m_conv_dw_w, m_conv_dw_b, m_conv_ln_g, m_conv_ln_b, m_rel_bias_table, m_gmlp_ln_g, m_gmlp_ln_b, m_gmlp_w_s, m_gmlp_b_s, m_w_out, m_b_out, m_ln1_g, m_ln1_b, m_ffn_w_up, m_ffn_b_up, m_ffn_conv_w, m_ffn_conv_b, m_ffn_w_down, m_ffn_b_down, m_ln2_g, m_ln2_b, v_w_in, v_b_in, v_conv_dw_w, v_conv_dw_b, v_conv_ln_g, v_conv_ln_b, v_rel_bias_table, v_gmlp_ln_g, v_gmlp_ln_b, v_gmlp_w_s, v_gmlp_b_s, v_w_out, v_b_out, v_ln1_g, v_ln1_b, v_ffn_w_up, v_ffn_b_up, v_ffn_conv_w, v_ffn_conv_b, v_ffn_w_down, v_ffn_b_down, v_ln2_g, v_ln2_b):
    W = dict(w_in=w_in, b_in=b_in, conv_dw_w=conv_dw_w, conv_dw_b=conv_dw_b, conv_ln_g=conv_ln_g,
             conv_ln_b=conv_ln_b, rel_bias_table=rel_bias_table, gmlp_ln_g=gmlp_ln_g, gmlp_ln_b=gmlp_ln_b,
             gmlp_w_s=gmlp_w_s, gmlp_b_s=gmlp_b_s, w_out=w_out, b_out=b_out, ln1_g=ln1_g, ln1_b=ln1_b,
             ffn_w_up=ffn_w_up, ffn_b_up=ffn_b_up, ffn_conv_w=ffn_conv_w, ffn_conv_b=ffn_conv_b,
             ffn_w_down=ffn_w_down, ffn_b_down=ffn_b_down, ln2_g=ln2_g, ln2_b=ln2_b)
    Mo = dict(w_in=m_w_in, b_in=m_b_in, conv_dw_w=m_conv_dw_w, conv_dw_b=m_conv_dw_b, conv_ln_g=m_conv_ln_g,
              conv_ln_b=m_conv_ln_b, rel_bias_table=m_rel_bias_table, gmlp_ln_g=m_gmlp_ln_g, gmlp_ln_b=m_gmlp_ln_b,
              gmlp_w_s=m_gmlp_w_s, gmlp_b_s=m_gmlp_b_s, w_out=m_w_out, b_out=m_b_out, ln1_g=m_ln1_g, ln1_b=m_ln1_b,
              ffn_w_up=m_ffn_w_up, ffn_b_up=m_ffn_b_up, ffn_conv_w=m_ffn_conv_w, ffn_conv_b=m_ffn_conv_b,
              ffn_w_down=m_ffn_w_down, ffn_b_down=m_ffn_b_down, ln2_g=m_ln2_g, ln2_b=m_ln2_b)
    Vo = dict(w_in=v_w_in, b_in=v_b_in, conv_dw_w=v_conv_dw_w, conv_dw_b=v_conv_dw_b, conv_ln_g=v_conv_ln_g,
              conv_ln_b=v_conv_ln_b, rel_bias_table=v_rel_bias_table, gmlp_ln_g=v_gmlp_ln_g, gmlp_ln_b=v_gmlp_ln_b,
              gmlp_w_s=v_gmlp_w_s, gmlp_b_s=v_gmlp_b_s, w_out=v_w_out, b_out=v_b_out, ln1_g=v_ln1_g, ln1_b=v_ln1_b,
              ffn_w_up=v_ffn_w_up, ffn_b_up=v_ffn_b_up, ffn_conv_w=v_ffn_conv_w, ffn_conv_b=v_ffn_conv_b,
              ffn_w_down=v_ffn_w_down, ffn_b_down=v_ffn_b_down, ln2_g=v_ln2_g, ln2_b=v_ln2_b)

    xs = x[0]
    target = loss_target[0]
    S, D = xs.shape
    F2 = ffn_b_up.shape[1]
    F = F2 // 2
    ts = min(512, S)
    G = _attn_group(S)
    tc = F // 2 if (F // 2) % LANES == 0 else F

    mat_names = SHARDED[:4]
    payload = [W[n].astype(BF16) if n in mat_names else W[n] for n in SHARDED]
    wall = _allgather(payload, name="weight_allgather")
    full = {n: _gather_axis(parts, SHARD_AXIS[n]) for n, parts in zip(SHARDED, wall)}

    tables = [_pattern_tables(w, d) for (w, d) in PATTERNS]
    bucket_flat = jnp.asarray(np.stack([np.where(v, b, -1).reshape(1, -1) for (b, v) in tables]).astype(np.int32))
    bias_all = _bias_tables(rel_bias_table, bucket_flat, name="bias_tables")
    biases = [bias_all[p].reshape(ATTN_HEADS, ATTN_BLOCK, 2 * ATTN_BLOCK) for p in range(len(PATTERNS))]
    nbs = [(S // d) // ATTN_BLOCK for (_, d) in PATTERNS]
    dils = [d for (_, d) in PATTERNS]
    scale = HEAD_DIM ** -0.5

    saved = []
    cur = xs
    for l in range(DEPTH):
        Win, Wout, Wup, Wdown = full['w_in'][l], full['w_out'][l], full['ffn_w_up'][l], full['ffn_w_down'][l]
        qcols = slice(2 * CONV_CH, 2 * CONV_CH + ATTN_CH)
        Win_s = Win.at[:, qcols].multiply(scale)
        b_in_s = b_in[l].at[qcols].multiply(scale)
        h_a, qkv, h_c = _mm([cur], [Win_s], bias=b_in_s, tm=ts, name="in_proj",
                            splits=((2 * CONV_CH, F32, 1.0), (3 * ATTN_CH, F32, 1.0), (2 * GMLP_CH, F32, 1.0)))
        conv_out, hc = _conv_fwd(h_a, full['conv_dw_w'][l], conv_dw_b[l], conv_ln_g[l], conv_ln_b[l],
                                 name="conv_fwd", ts=ts)
        qkv_d = _dilate_qkv(qkv, name="dilate_qkv", ts=ts)
        o_ps, lse_ps = [], []
        for p, d in enumerate(dils):
            o_p, lse_p = _attn_fwd(qkv_d[p], biases[p], name=f"attn_fwd_d{d}", nb=nbs[p], G=G)
            o_ps.append(o_p)
            lse_ps.append(lse_p)
        attn_out, lse = _attn_merge(o_ps, lse_ps, dils, name="attn_merge", ts=ts)
        w_tril = jnp.tril(gmlp_w_s[l]).astype(BF16)
        bs_rows = jnp.repeat(gmlp_b_s[l].T, GMLP_GROUP_DIM, axis=1)
        gm_out = _gmlp_fwd(h_c, gmlp_ln_g[l], gmlp_ln_b[l], w_tril, bs_rows, name="gmlp_fwd", ts=ts)
        x1, xhat1, rstd1 = _mm([conv_out, attn_out, gm_out],
                               [Wout[:CONV_CH], Wout[CONV_CH:CONV_CH + ATTN_CH], Wout[CONV_CH + ATTN_CH:]],
                               bias=b_out[l], resid=cur, resid_scale=ALPHA, ln=(ln1_g[l], ln1_b[l]), tm=ts,
                               name="out_proj_ln")
        hu = _mm([x1], [Wup], bias=ffn_b_up[l], tm=ts, tn=F, name="ffn_up")
        act, act_dg, act_dv = _ffn_act_fwd(hu, full['ffn_conv_w'][l], ffn_conv_b[l], name="ffn_act_fwd",
                                           ts=min(256, S), tc=tc)
        x2, xhat2, rstd2 = _mm([act], [Wdown], bias=ffn_b_down[l], resid=x1, resid_scale=ALPHA,
                               ln=(ln2_g[l], ln2_b[l]), tm=ts, name="ffn_down_ln")
        saved.append(dict(x0=cur, h_a=h_a, h_c=h_c, qkv_d=qkv_d, hc=hc, conv_out=conv_out, attn_out=attn_out,
                          lse=lse, gm_out=gm_out, w_tril=w_tril, bs_rows=bs_rows, x1=x1, xhat1=xhat1, rstd1=rstd1,
                          hu=hu, act=act, act_dg=act_dg, act_dv=act_dv, xhat2=xhat2, rstd2=rstd2))
        cur = x2

    grads = {n: [None] * DEPTH for n in WEIGHTS if n != 'rel_bias_table'}
    drel = None
    dx = None
    loss_part = None
    tk = min(1024, S)
    for l in reversed(range(DEPTH)):
        sv = saved[l]
        Win, Wout, Wup, Wdown = full['w_in'][l], full['w_out'][l], full['ffn_w_up'][l], full['ffn_w_down'][l]
        if dx is None:
            dz2, dg2, db2, dzs2, loss_part = _ln_bwd(sv['xhat2'], sv['rstd2'], ln2_g[l], b=ln2_b[l], target=target,
                                                     name="ln2_bwd_loss", ts=ts)
        else:
            dz2, dg2, db2, dzs2 = _ln_bwd(sv['xhat2'], sv['rstd2'], ln2_g[l], dy=dx, name="ln_bwd", ts=ts)
        grads['ln2_g'][l], grads['ln2_b'][l], grads['ffn_b_down'][l] = dg2[0], db2[0], dzs2[0]
        grads['ffn_w_down'][l] = _mm_tn(sv['act'], dz2, tm=F // 2 if (F // 2) % LANES == 0 else F, tn=D, tk=tk,
                                        name="dw_down")
        dact = _mm([dz2], [Wdown.T], tm=ts, name="dact")
        dhg, dhv, dwg, dwv, dbg, dbv, dug, duv = _ffn_act_bwd(sv['hu'], dact, sv['act_dg'], sv['act_dv'],
                                                              full['ffn_conv_w'][l], name="ffn_act_bwd",
                                                              ts=min(256, S), tc=tc)
        grads['ffn_conv_w'][l] = jnp.concatenate([dwg, dwv], axis=1)
        grads['ffn_conv_b'][l] = jnp.concatenate([dbg, dbv], axis=1)[0]
        grads['ffn_b_up'][l] = jnp.concatenate([dug, duv], axis=1)[0]
        grads['ffn_w_up'][l] = jnp.concatenate(
            [_mm_tn(sv['x1'], dhg, tm=D, tn=tc, tk=tk, name="dw_up"),
             _mm_tn(sv['x1'], dhv, tm=D, tn=tc, tk=tk, name="dw_up")], axis=1)
        WupT = Wup.T
        dx1 = _mm([dhg, dhv], [WupT[:F], WupT[F:]], resid=dz2, resid_scale=ALPHA, tm=ts, name="dx1")
        dz1, dg1, db1, dzs1 = _ln_bwd(sv['xhat1'], sv['rstd1'], ln1_g[l], dy=dx1, name="ln_bwd", ts=ts)
        grads['ln1_g'][l], grads['ln1_b'][l], grads['b_out'][l] = dg1[0], db1[0], dzs1[0]
        grads['w_out'][l] = jnp.concatenate(
            [_mm_tn(sv['conv_out'], dz1, tm=CONV_CH, tn=D, tk=tk, name="dw_out_conv"),
             _mm_tn(sv['attn_out'], dz1, tm=ATTN_CH, tn=D, tk=tk, name="dw_out_attn"),
             _mm_tn(sv['gm_out'], dz1, tm=GMLP_CH, tn=D, tk=tk, name="dw_out_conv")], axis=0)
        dc_conv, dc_attn, dc_gm = _mm([dz1], [Wout.T], tm=ts, name="dcat",
                                      splits=((CONV_CH, F32, 1.0), (ATTN_CH, F32, 1.0), (GMLP_CH, F32, 1.0)))
        dh_a, ddw, ddwb, dclg, dclb = _conv_bwd(sv['h_a'], sv['hc'], dc_conv, full['conv_dw_w'][l], conv_ln_g[l],
                                                conv_ln_b[l], name="conv_bwd", ts=ts)
        grads['conv_dw_w'][l], grads['conv_dw_b'][l] = ddw[:CONV_WIDTH], ddwb[0]
        grads['conv_ln_g'][l], grads['conv_ln_b'][l] = dclg[0], dclb[0]
        dd_d, do_d = _attn_prep(dc_attn, sv['attn_out'], dils, name="attn_prep", ts=ts)
        dqs, dks, dvs, dbs = [], [], [], []
        for p, d in enumerate(dils):
            dq, dk, dv, dbias = _attn_bwd(sv['qkv_d'][p], do_d[p], sv['lse'][p], dd_d[p], biases[p],
                                          name=f"attn_bwd_d{d}", nb=nbs[p], G=G)
            dqs.append(dq)
            dks.append(dk)
            dvs.append(dv)
            dbs.append(dbias.reshape(1, ATTN_HEADS, -1))
        dqkv = _attn_combine(dqs, dks, dvs, dils, name="attn_combine", ts=ts)
        dr = _bias_grad(jnp.concatenate(dbs, axis=0), bucket_flat, name="bias_grad")
        drel = dr if drel is None else drel + dr
        w_tril_t = jnp.swapaxes(sv['w_tril'], 1, 2)
        dh_c, dws, dbs_acc, dglg, dglb = _gmlp_bwd(sv['h_c'], dc_gm, gmlp_ln_g[l], gmlp_ln_b[l], sv['w_tril'],
                                                   w_tril_t, sv['bs_rows'], name="gmlp_bwd", ts=ts)
        grads['gmlp_w_s'][l] = jnp.tril(dws)
        grads['gmlp_b_s'][l] = dbs_acc[:, :GMLP_GROUPS].T
        grads['gmlp_ln_g'][l], grads['gmlp_ln_b'][l] = dglg[0], dglb[0]
        dw_in = _mm_tn_shared(sv['x0'], [dh_a, *dqkv, dh_c], tk=tk, name="dw_in")
        grads['w_in'][l] = jnp.concatenate([w for w, _ in dw_in], axis=1)
        grads['b_in'][l] = jnp.concatenate([c for _, c in dw_in], axis=1)[0]
        WinT = Win.T
        edges = [0, 2 * CONV_CH] + [2 * CONV_CH + k * ATTN_CH for k in (1, 2, 3)] + [WinT.shape[0]]
        dx = _mm([dh_a, *dqkv, dh_c], [WinT[a:b] for a, b in zip(edges[:-1], edges[1:])], resid=dz1,
                 resid_scale=ALPHA, tm=ts, name="dx0")

    gfull = {n: jnp.stack(v) for n, v in grads.items()}
    gfull['rel_bias_table'] = drel

    sends = []
    for n in SHARDED:
        parts = _split_axis(gfull[n], SHARD_AXIS[n])
        sends.append(parts.reshape(N_DEV, -1, parts.shape[-1]).astype(BF16))
    recvs = _exchange(sends, name="grad_exchange")
    shard_out = [[], [], [], []]
    for n, recv in zip(SHARDED, recvs):
        shp = W[n].shape
        rows = recv.shape[1]
        tr = rows // 4 if rows % 64 == 0 else rows
        outs = _adamw(*[src[n].reshape(rows, shp[-1]) for src in (W, Mo, Vo)], recv, name=f"adamw_{n}", tr=tr)
        for kind in range(4):
            shard_out[kind].append(outs[kind].reshape(shp))

    gsmall, soffs = _pack([gfull[n] for n in SMALL], LANES, 8)
    gall = _allgather([gsmall], name="small_grad_allgather")[0]
    spacks = [_pack([src[n] for n in SMALL], LANES, 8)[0] for src in (W, Mo, Vo)]
    souts = _adamw(spacks[0], spacks[1], spacks[2], gall, name="adamw_small", tr=gsmall.shape[0])
    small_out = [_unpack(o, soffs, [W[n].shape for n in SMALL]) for o in souts]

    loss = lax.psum(loss_part[0, 0], ("x", "y", "c"))
    by_kind = []
    for kind in range(4):
        d = dict(zip(SHARDED, shard_out[kind]))
        d.update(zip(SMALL, small_out[kind]))
        by_kind.append([d[n] for n in WEIGHTS])
    return (loss, dx[None], *by_kind[0], *by_kind[1], *by_kind[2], *by_kind[3])
```

```python
import math

import numpy as np
import jax
import jax.numpy as jnp
from jax import lax
from jax.experimental import pallas as pl
from jax.experimental.pallas import tpu as pltpu

F32 = jnp.float32
BF16 = jnp.bfloat16

DEPTH = 2
HEAD_DIM = 64
CONV_CH = 256
CONV_WIDTH = 31
ATTN_HEADS = 8
ATTN_CH = ATTN_HEADS * HEAD_DIM
PATTERNS = ((128, 1), (512, 4), (2048, 16))
ATTN_BLOCK = 128
N_BUCKETS = 32
MAX_DISTANCE = 2048
GMLP_CH = 256
GMLP_GROUPS = 4
GMLP_GROUP_DIM = GMLP_CH // GMLP_GROUPS
CHUNK = 128
FFN_CONV_WIDTH = 3
LN_EPS = 1e-5
ALPHA = (2.0 * DEPTH) ** 0.25
ADAM_LR = 0.001
ADAM_B1 = 0.9
ADAM_B2 = 0.999
ADAM_EPS = 1e-08
ADAM_WD = 0.01
ADAM_STEP = 10
NEG = -1e30
N_DEV = 8
LANES = 128
CONV_HALO = 32
FFN_HALO = 8
FFN_ROWS = 16
MESH = pl.DeviceIdType.MESH

WEIGHTS = ['w_in', 'b_in', 'conv_dw_w', 'conv_dw_b', 'conv_ln_g', 'conv_ln_b', 'rel_bias_table', 'gmlp_ln_g',
           'gmlp_ln_b', 'gmlp_w_s', 'gmlp_b_s', 'w_out', 'b_out', 'ln1_g', 'ln1_b', 'ffn_w_up', 'ffn_b_up',
           'ffn_conv_w', 'ffn_conv_b', 'ffn_w_down', 'ffn_b_down', 'ln2_g', 'ln2_b']
SHARDED = ['w_in', 'w_out', 'ffn_w_up', 'ffn_w_down', 'conv_dw_w', 'ffn_conv_w']
SHARD_AXIS = {'w_in': 2, 'w_out': 1, 'ffn_w_up': 2, 'ffn_w_down': 1, 'conv_dw_w': 2, 'ffn_conv_w': 2}
SMALL = [n for n in WEIGHTS if n not in SHARDED]


def _call(body, *, grid=(), vmem_mb=48, **kw):
    params = pltpu.CompilerParams(dimension_semantics=("arbitrary",) * len(grid), vmem_limit_bytes=vmem_mb << 20)
    return pl.pallas_call(body, grid=grid, compiler_params=params, **kw)


def _sds(shape, dtype):
    return jax.ShapeDtypeStruct(shape, dtype)


def _ln_rows(z):
    mu = jnp.mean(z, axis=-1, keepdims=True)
    zc = z - mu
    var = jnp.mean(zc * zc, axis=-1, keepdims=True)
    rstd = lax.rsqrt(var + LN_EPS)
    return zc * rstd, rstd


def _ln_bwd_rows(dxhat, xhat, rstd):
    m1 = jnp.mean(dxhat, axis=-1, keepdims=True)
    m2 = jnp.mean(dxhat * xhat, axis=-1, keepdims=True)
    return rstd * (dxhat - m1 - xhat * m2)


def _colsum(v):
    return jnp.sum(v, axis=0, keepdims=True)


def _mm(a_list, w_list, *, name, tm, tn=None, bias=None, resid=None, resid_scale=1.0, ln=None, splits=None,
        out_dtype=F32):
    na = len(a_list)
    M = a_list[0].shape[0]
    N = w_list[0].shape[1]
    tn = N if tn is None else tn
    assert M % tm == 0 and N % tn == 0
    assert ln is None or tn == N
    assert splits is None or tn == N

    def body(*refs):
        a_refs, w_refs = refs[:na], refs[na:2 * na]
        pos = 2 * na
        acc = None
        for a_ref, w_ref in zip(a_refs, w_refs):
            t = jnp.dot(a_ref[...].astype(BF16), w_ref[...], preferred_element_type=F32)
            acc = t if acc is None else acc + t
        if bias is not None:
            acc = acc + refs[pos][...]
            pos += 1
        if resid is not None:
            acc = resid_scale * refs[pos][...] + acc
            pos += 1
        if ln is not None:
            g_ref, b_ref = refs[pos], refs[pos + 1]
            y_ref, xhat_ref, rstd_ref = refs[pos + 2], refs[pos + 3], refs[pos + 4]
            xhat, rstd = _ln_rows(acc)
            y_ref[...] = xhat * g_ref[...] + b_ref[...]
            xhat_ref[...] = xhat
            rstd_ref[...] = rstd
        elif splits is not None:
            c0 = 0
            for o_ref, (width, dtype, scale) in zip(refs[pos:], splits):
                part = acc[:, c0:c0 + width]
                if scale != 1.0:
                    part = part * scale
                o_ref[...] = part.astype(dtype)
                c0 += width
        else:
            refs[pos][...] = acc.astype(out_dtype)

    in_specs = [pl.BlockSpec((tm, a.shape[1]), lambda j, i: (i, 0)) for a in a_list]
    in_specs += [pl.BlockSpec((w.shape[0], tn), lambda j, i: (0, j)) for w in w_list]
    args = list(a_list) + list(w_list)
    if bias is not None:
        in_specs.append(pl.BlockSpec((1, tn), lambda j, i: (0, j)))
        args.append(bias.reshape(1, N))
    if resid is not None:
        in_specs.append(pl.BlockSpec((tm, tn), lambda j, i: (i, j)))
        args.append(resid)
    if ln is not None:
        in_specs += [pl.BlockSpec((1, N), lambda j, i: (0, 0))] * 2
        args += [ln[0].reshape(1, N), ln[1].reshape(1, N)]
        out_shape = (_sds((M, N), F32), _sds((M, N), F32), _sds((M, 1), F32))
        out_specs = (pl.BlockSpec((tm, N), lambda j, i: (i, 0)), pl.BlockSpec((tm, N), lambda j, i: (i, 0)),
                     pl.BlockSpec((tm, 1), lambda j, i: (i, 0)))
    elif splits is not None:
        out_shape = tuple(_sds((M, w), d) for (w, d, _) in splits)
        out_specs = tuple(pl.BlockSpec((tm, w), lambda j, i: (i, 0)) for (w, _, _) in splits)
    else:
        out_shape = _sds((M, N), out_dtype)
        out_specs = pl.BlockSpec((tm, tn), lambda j, i: (i, j))
    return _call(body, grid=(N // tn, M // tm), in_specs=in_specs, out_specs=out_specs, out_shape=out_shape,
                 name=name, vmem_mb=56)(*args)


def _mm_tn(a, dy, *, name, tm, tn, tk, colsum=False):
    S, Ka = a.shape
    N = dy.shape[1]
    assert S % tk == 0 and Ka % tm == 0 and N % tn == 0

    def body(a_ref, dy_ref, out_ref, *cs):
        i, k = pl.program_id(1), pl.program_id(2)
        dyb = dy_ref[...]
        part = lax.dot_general(a_ref[...].astype(BF16), dyb.astype(BF16), (((0,), (0,)), ((), ())),
                               preferred_element_type=F32)

        @pl.when(k == 0)
        def _():
            out_ref[...] = part

        @pl.when(k > 0)
        def _():
            out_ref[...] += part

        if colsum:
            cs_ref = cs[0]
            s = _colsum(dyb.astype(F32))

            @pl.when((i == 0) & (k == 0))
            def _():
                cs_ref[...] = s

            @pl.when((i == 0) & (k > 0))
            def _():
                cs_ref[...] += s

    out_shape = [_sds((Ka, N), F32)]
    out_specs = [pl.BlockSpec((tm, tn), lambda j, i, k: (i, j))]
    if colsum:
        out_shape.append(_sds((1, N), F32))
        out_specs.append(pl.BlockSpec((1, tn), lambda j, i, k: (0, j)))
    res = _call(body, grid=(N // tn, Ka // tm, S // tk),
                in_specs=[pl.BlockSpec((tk, tm), lambda j, i, k: (k, i)), pl.BlockSpec((tk, tn), lambda j, i, k: (k, j))],
                out_specs=tuple(out_specs), out_shape=tuple(out_shape), name=name, vmem_mb=56)(a, dy)
    return res if colsum else res[0]


def _mm_tn_shared(a, dys, *, name, tk):
    S, Ka = a.shape
    n = len(dys)
    assert S % tk == 0

    def body(*refs):
        a_ref, dy_refs, outs = refs[0], refs[1:1 + n], refs[1 + n:]
        k = pl.program_id(0)
        a_bf = a_ref[...].astype(BF16)
        for i, dy_ref in enumerate(dy_refs):
            dyb = dy_ref[...]
            part = lax.dot_general(a_bf, dyb.astype(BF16), (((0,), (0,)), ((), ())), preferred_element_type=F32)
            s = _colsum(dyb.astype(F32))
            w_ref, c_ref = outs[2 * i], outs[2 * i + 1]

            @pl.when(k == 0)
            def _():
                w_ref[...] = part
                c_ref[...] = s

            @pl.when(k > 0)
            def _():
                w_ref[...] += part
                c_ref[...] += s

    in_specs = [pl.BlockSpec((tk, Ka), lambda k: (k, 0))]
    in_specs += [pl.BlockSpec((tk, dy.shape[1]), lambda k: (k, 0)) for dy in dys]
    out_specs, out_shape = [], []
    for dy in dys:
        N = dy.shape[1]
        out_specs += [pl.BlockSpec((Ka, N), lambda k: (0, 0)), pl.BlockSpec((1, N), lambda k: (0, 0))]
        out_shape += [_sds((Ka, N), F32), _sds((1, N), F32)]
    res = _call(body, grid=(S // tk,), in_specs=in_specs, out_specs=tuple(out_specs), out_shape=tuple(out_shape),
                name=name, vmem_mb=56)(a, *dys)
    return [(res[2 * i], res[2 * i + 1]) for i in range(n)]


def _ln_bwd(xhat, rstd, g, *, name, ts, dy=None, b=None, target=None):
    S, D = xhat.shape
    from_loss = target is not None

    def body(*refs):
        if from_loss:
            xhat_ref, rstd_ref, g_ref, b_ref, t_ref, dz_ref, dg_ref, db_ref, dzs_ref, loss_ref = refs
        else:
            xhat_ref, rstd_ref, g_ref, dy_ref, dz_ref, dg_ref, db_ref, dzs_ref = refs
        i = pl.program_id(0)
        xh = xhat_ref[...]
        gg = g_ref[...]
        if from_loss:
            err = xh * gg + b_ref[...] - t_ref[...]
            dyv = err * (1.0 / D)
            lsum = (0.5 / D) * jnp.sum(err * err, axis=(0, 1), keepdims=True)
        else:
            dyv = dy_ref[...]
        dz = _ln_bwd_rows(dyv * gg, xh, rstd_ref[...])
        dz_ref[...] = dz
        parts = [(dg_ref, _colsum(dyv * xh)), (db_ref, _colsum(dyv)), (dzs_ref, _colsum(dz))]
        if from_loss:
            parts.append((loss_ref, lsum))

        @pl.when(i == 0)
        def _():
            for r, v in parts:
                r[...] = v

        @pl.when(i > 0)
        def _():
            for r, v in parts:
                r[...] += v

    row = pl.BlockSpec((ts, D), lambda i: (i, 0))
    vec = pl.BlockSpec((1, D), lambda i: (0, 0))
    in_specs = [row, pl.BlockSpec((ts, 1), lambda i: (i, 0)), vec]
    args = [xhat, rstd, g.reshape(1, D)]
    if from_loss:
        in_specs += [vec, row]
        args += [b.reshape(1, D), target]
    else:
        in_specs += [row]
        args += [dy]
    out_shape = [_sds((S, D), F32), _sds((1, D), F32), _sds((1, D), F32), _sds((1, D), F32)]
    out_specs = [row, vec, vec, vec]
    if from_loss:
        out_shape.append(_sds((1, 1), F32))
        out_specs.append(pl.BlockSpec((1, 1), lambda i: (0, 0)))
    return _call(body, grid=(S // ts,), in_specs=in_specs, out_specs=tuple(out_specs), out_shape=tuple(out_shape),
                 name=name)(*args)


def _glu(v):
    return v[:, :CONV_CH] * jax.nn.sigmoid(v[:, CONV_CH:])


def _conv_fwd(h_a, dw_w, dw_b, ln_g, ln_b, *, name, ts):
    S = h_a.shape[0]
    C, K, HB = CONV_CH, CONV_WIDTH, CONV_HALO
    RC = 128

    def body(h_ref, halo_ref, w_ref, b_ref, g_ref, bb_ref, out_ref, hc_ref, gbuf):
        i = pl.program_id(0)
        gbuf[0:HB, :] = jnp.where(i > 0, _glu(halo_ref[...]), 0.0)
        gbuf[HB:HB + ts, :] = _glu(h_ref[...])
        for r0 in range(0, ts, RC):
            acc = jnp.zeros((RC, C), F32) + b_ref[...]
            for k in range(K):
                acc = acc + w_ref[k:k + 1, :] * gbuf[pl.ds(r0 + HB - (K - 1) + k, RC), :]
            hc_ref[r0:r0 + RC, :] = acc
            xhat, _ = _ln_rows(acc)
            hn = xhat * g_ref[...] + bb_ref[...]
            out_ref[r0:r0 + RC, :] = (hn * jax.nn.sigmoid(hn)).astype(BF16)

    nb = ts // HB
    vec = pl.BlockSpec((1, C), lambda i: (0, 0))
    return _call(body, grid=(S // ts,),
                 in_specs=[pl.BlockSpec((ts, 2 * C), lambda i: (i, 0)),
                           pl.BlockSpec((HB, 2 * C), lambda i: (jnp.maximum(i * nb - 1, 0), 0)),
                           pl.BlockSpec((K, C), lambda i: (0, 0)), vec, vec, vec],
                 out_specs=(pl.BlockSpec((ts, C), lambda i: (i, 0)), pl.BlockSpec((ts, C), lambda i: (i, 0))),
                 out_shape=(_sds((S, C), BF16), _sds((S, C), F32)),
                 scratch_shapes=[pltpu.VMEM((HB + ts, C), F32)], name=name)(
        h_a, h_a, dw_w, dw_b.reshape(1, C), ln_g.reshape(1, C), ln_b.reshape(1, C))


def _conv_bwd(h_a, hc, dout, dw_w, ln_g, ln_b, *, name, ts):
    S = h_a.shape[0]
    C, K, HB = CONV_CH, CONV_WIDTH, CONV_HALO
    RC = 128
    n = S // ts

    def dconv_out(hc_v, do_v, g_ref, bb_ref):
        xhat, rstd = _ln_rows(hc_v)
        hn = xhat * g_ref[...] + bb_ref[...]
        sg = jax.nn.sigmoid(hn)
        dhn = do_v * (sg * (1.0 + hn * (1.0 - sg)))
        return _ln_bwd_rows(dhn * g_ref[...], xhat, rstd), dhn, xhat

    def body(h_ref, hprev_ref, hc_ref, hcnext_ref, do_ref, donext_ref, w_ref, g_ref, bb_ref,
             dh_ref, dw_ref, dwb_ref, dg_ref, db_ref, gbuf, dbuf):
        i = pl.program_id(0)
        hv = h_ref[...]
        gbuf[0:HB, :] = jnp.where(i > 0, _glu(hprev_ref[...]), 0.0)
        gbuf[HB:HB + ts, :] = _glu(hv)
        dhc, dhn, xhat = dconv_out(hc_ref[...], do_ref[...], g_ref, bb_ref)
        dhc_next, _, _ = dconv_out(hcnext_ref[...], donext_ref[...], g_ref, bb_ref)
        dbuf[0:ts, :] = dhc
        dbuf[ts:ts + HB, :] = jnp.where(i < n - 1, dhc_next, 0.0)
        dw_rows = []
        for k in range(K):
            acc_k = jnp.zeros((1, C), F32)
            for r0 in range(0, ts, RC):
                acc_k = acc_k + _colsum(dbuf[r0:r0 + RC, :] * gbuf[pl.ds(r0 + HB - (K - 1) + k, RC), :])
            dw_rows.append(acc_k)
        dw_rows.append(jnp.zeros((1, C), F32))
        dw_tile = jnp.concatenate(dw_rows, axis=0)
        for r0 in range(0, ts, RC):
            acc = jnp.zeros((RC, C), F32)
            for k in range(K):
                acc = acc + w_ref[k:k + 1, :] * dbuf[pl.ds(r0 + (K - 1) - k, RC), :]
            a = hv[r0:r0 + RC, :C]
            sg = jax.nn.sigmoid(hv[r0:r0 + RC, C:])
            dh_ref[r0:r0 + RC, :C] = (acc * sg).astype(BF16)
            dh_ref[r0:r0 + RC, C:] = (acc * a * sg * (1.0 - sg)).astype(BF16)
        parts = [(dw_ref, dw_tile), (dwb_ref, _colsum(dhc)), (dg_ref, _colsum(dhn * xhat)), (db_ref, _colsum(dhn))]

        @pl.when(i == 0)
        def _():
            for r, v in parts:
                r[...] = v

        @pl.when(i > 0)
        def _():
            for r, v in parts:
                r[...] += v

    nb = ts // HB
    last = S // HB - 1
    vec = pl.BlockSpec((1, C), lambda i: (0, 0))
    nxt = lambda i: (jnp.minimum((i + 1) * nb, last), 0)
    return _call(body, grid=(n,),
                 in_specs=[pl.BlockSpec((ts, 2 * C), lambda i: (i, 0)),
                           pl.BlockSpec((HB, 2 * C), lambda i: (jnp.maximum(i * nb - 1, 0), 0)),
                           pl.BlockSpec((ts, C), lambda i: (i, 0)), pl.BlockSpec((HB, C), nxt),
                           pl.BlockSpec((ts, C), lambda i: (i, 0)), pl.BlockSpec((HB, C), nxt),
                           pl.BlockSpec((K, C), lambda i: (0, 0)), vec, vec],
                 out_specs=(pl.BlockSpec((ts, 2 * C), lambda i: (i, 0)), pl.BlockSpec((K + 1, C), lambda i: (0, 0)),
                            vec, vec, vec),
                 out_shape=(_sds((S, 2 * C), BF16), _sds((K + 1, C), F32), _sds((1, C), F32), _sds((1, C), F32),
                            _sds((1, C), F32)),
                 scratch_shapes=[pltpu.VMEM((HB + ts, C), F32), pltpu.VMEM((ts + HB, C), F32)], name=name)(
        h_a, h_a, hc, hc, dout, dout, dw_w, ln_g.reshape(1, C), ln_b.reshape(1, C))


def _gmlp_mix(vn_bf, w_ref, mix_buf, ts):
    for ch in range(ts // CHUNK):
        for g in range(GMLP_GROUPS):
            vg = vn_bf[ch * CHUNK:(ch + 1) * CHUNK, g * GMLP_GROUP_DIM:(g + 1) * GMLP_GROUP_DIM]
            mix_buf[ch * CHUNK:(ch + 1) * CHUNK, g * GMLP_GROUP_DIM:(g + 1) * GMLP_GROUP_DIM] = jnp.dot(
                w_ref[g], vg, preferred_element_type=F32)


def _gmlp_fwd(h_c, ln_g, ln_b, w_tril, bs_rows, *, name, ts):
    S = h_c.shape[0]
    C = GMLP_CH

    def body(h_ref, g_ref, b_ref, w_ref, bs_ref, out_ref, mix_buf):
        hv = h_ref[...]
        xhat, _ = _ln_rows(hv[:, C:])
        vn = (xhat * g_ref[...] + b_ref[...]).astype(BF16)
        _gmlp_mix(vn, w_ref, mix_buf, ts)
        for ch in range(ts // CHUNK):
            rows = slice(ch * CHUNK, (ch + 1) * CHUNK)
            out_ref[rows, :] = (hv[rows, :C] * (mix_buf[rows, :] + bs_ref[...])).astype(BF16)

    vec = pl.BlockSpec((1, C), lambda i: (0, 0))
    return _call(body, grid=(S // ts,),
                 in_specs=[pl.BlockSpec((ts, 2 * C), lambda i: (i, 0)), vec, vec,
                           pl.BlockSpec((GMLP_GROUPS, CHUNK, CHUNK), lambda i: (0, 0, 0)),
                           pl.BlockSpec((CHUNK, C), lambda i: (0, 0))],
                 out_specs=pl.BlockSpec((ts, C), lambda i: (i, 0)), out_shape=_sds((S, C), BF16),
                 scratch_shapes=[pltpu.VMEM((ts, C), F32)], name=name)(
        h_c, ln_g.reshape(1, C), ln_b.reshape(1, C), w_tril, bs_rows)


def _gmlp_bwd(h_c, dout, ln_g, ln_b, w_tril, w_tril_t, bs_rows, *, name, ts):
    S = h_c.shape[0]
    C, G, GD = GMLP_CH, GMLP_GROUPS, GMLP_GROUP_DIM

    def body(h_ref, do_ref, g_ref, b_ref, w_ref, wt_ref, bs_ref, dh_ref, dw_ref, dbs_ref, dg_ref, db_ref,
             mix_buf, dvn_buf):
        i = pl.program_id(0)
        hv = h_ref[...]
        u = hv[:, :C]
        xhat, rstd = _ln_rows(hv[:, C:])
        vn = (xhat * g_ref[...] + b_ref[...]).astype(BF16)
        _gmlp_mix(vn, w_ref, mix_buf, ts)
        do = do_ref[...]
        dmixed = do * u
        dm_bf = dmixed.astype(BF16)
        lane = lax.broadcasted_iota(jnp.int32, (CHUNK, LANES), 1)
        dbs = jnp.zeros((CHUNK, LANES), F32)
        dws = [jnp.zeros((CHUNK, CHUNK), F32) for _ in range(G)]
        for ch in range(ts // CHUNK):
            rows = slice(ch * CHUNK, (ch + 1) * CHUNK)
            dh_ref[rows, :C] = (do[rows, :] * (mix_buf[rows, :] + bs_ref[...])).astype(BF16)
            for g in range(G):
                cols = slice(g * GD, (g + 1) * GD)
                dmg = dm_bf[rows, cols]
                dvn_buf[rows, cols] = jnp.dot(wt_ref[g], dmg, preferred_element_type=F32)
                dws[g] = dws[g] + lax.dot_general(dmg, vn[rows, cols], (((1,), (1,)), ((), ())),
                                                  preferred_element_type=F32)
                rs = jnp.sum(dmixed[rows, cols], axis=1, keepdims=True)
                dbs = dbs + jnp.where(lane == g, rs, 0.0)
        dvn = dvn_buf[...]
        dh_ref[:, C:] = _ln_bwd_rows(dvn * g_ref[...], xhat, rstd).astype(BF16)
        dgv, dbv = _colsum(dvn * xhat), _colsum(dvn)

        @pl.when(i == 0)
        def _():
            for g in range(G):
                dw_ref[g] = dws[g]
            dbs_ref[...] = dbs
            dg_ref[...] = dgv
            db_ref[...] = dbv

        @pl.when(i > 0)
        def _():
            for g in range(G):
                dw_ref[g] += dws[g]
            dbs_ref[...] += dbs
            dg_ref[...] += dgv
            db_ref[...] += dbv

    vec = pl.BlockSpec((1, C), lambda i: (0, 0))
    wspec = pl.BlockSpec((G, CHUNK, CHUNK), lambda i: (0, 0, 0))
    return _call(body, grid=(S // ts,),
                 in_specs=[pl.BlockSpec((ts, 2 * C), lambda i: (i, 0)),
                           pl.BlockSpec((ts, C), lambda i: (i, 0)), vec, vec, wspec, wspec,
                           pl.BlockSpec((CHUNK, C), lambda i: (0, 0))],
                 out_specs=(pl.BlockSpec((ts, 2 * C), lambda i: (i, 0)), wspec,
                            pl.BlockSpec((CHUNK, LANES), lambda i: (0, 0)), vec, vec),
                 out_shape=(_sds((S, 2 * C), BF16), _sds((G, CHUNK, CHUNK), F32), _sds((CHUNK, LANES), F32),
                            _sds((1, C), F32), _sds((1, C), F32)),
                 scratch_shapes=[pltpu.VMEM((ts, C), F32), pltpu.VMEM((ts, C), F32)], name=name)(
        h_c, dout, ln_g.reshape(1, C), ln_b.reshape(1, C), w_tril, w_tril_t, bs_rows)


def _ffn_act_fwd(hu, cw, cb, *, name, ts, tc):
    S, F2 = hu.shape
    F = F2 // 2
    nj = F // tc
    HB = FFN_HALO
    nb = ts // HB

    def body(g_ref, v_ref, gh_ref, vh_ref, wg_ref, wv_ref, bg_ref, bv_ref, act_ref, dg_ref, dv_ref):
        i = pl.program_id(0)
        row = lax.broadcasted_iota(jnp.int32, (HB, tc), 0)

        def conv_chunk(x_ref, w_ref, b_ref, c, carry):
            cur = x_ref[c * HB:(c + 1) * HB, :]
            r1, r2 = pltpu.roll(cur, 1, 0), pltpu.roll(cur, 2, 0)
            x1 = jnp.where(row < 1, carry[0], r1)
            x2 = jnp.where(row < 2, carry[1], r2)
            out = (w_ref[0:1, :] * x2 + w_ref[1:2, :] * x1 + w_ref[2:3, :] * cur) + b_ref[...]
            return out, (r1, r2)

        def first_carry(h_ref):
            prev = jnp.where(i > 0, h_ref[...], 0.0)
            return pltpu.roll(prev, 1, 0), pltpu.roll(prev, 2, 0)

        cg, cv = first_carry(gh_ref), first_carry(vh_ref)
        for r0 in range(0, ts, FFN_ROWS):
            acts, dgs, dvs = [], [], []
            for c in range(r0 // HB, (r0 + FFN_ROWS) // HB):
                gc, cg = conv_chunk(g_ref, wg_ref, bg_ref, c, cg)
                vc, cv = conv_chunk(v_ref, wv_ref, bv_ref, c, cv)
                sg = jax.nn.sigmoid(gc)
                silu = gc * sg
                acts.append(silu * vc)
                dgs.append(vc * (sg * (1.0 + gc * (1.0 - sg))))
                dvs.append(silu)
            rows = slice(r0, r0 + FFN_ROWS)
            act_ref[rows, :] = jnp.concatenate(acts, axis=0).astype(BF16)
            dg_ref[rows, :] = jnp.concatenate(dgs, axis=0).astype(BF16)
            dv_ref[rows, :] = jnp.concatenate(dvs, axis=0).astype(BF16)

    prev = lambda off: (lambda i, j: (jnp.maximum(i * nb - 1, 0), j + off))
    out = pl.BlockSpec((ts, tc), lambda i, j: (i, j))
    return _call(body, grid=(S // ts, nj),
                 in_specs=[pl.BlockSpec((ts, tc), lambda i, j: (i, j)), pl.BlockSpec((ts, tc), lambda i, j: (i, j + nj)),
                           pl.BlockSpec((HB, tc), prev(0)), pl.BlockSpec((HB, tc), prev(nj)),
                           pl.BlockSpec((3, tc), lambda i, j: (0, j)), pl.BlockSpec((3, tc), lambda i, j: (0, j + nj)),
                           pl.BlockSpec((1, tc), lambda i, j: (0, j)), pl.BlockSpec((1, tc), lambda i, j: (0, j + nj))],
                 out_specs=(out, out, out), out_shape=(_sds((S, F), BF16),) * 3, name=name)(
        hu, hu, hu, hu, cw, cw, cb.reshape(1, F2), cb.reshape(1, F2))


def _ffn_act_bwd(hu, dact, dact_dg, dact_dv, cw, *, name, ts, tc):
    S, F2 = hu.shape
    F = F2 // 2
    nj = F // tc
    HB = FFN_HALO
    HB16 = 16
    n = S // ts

    def body(g_ref, v_ref, lg_ref, lv_ref, lgn_ref, lvn_ref, da_ref, dan_ref, wg_ref, wv_ref,
             dhg_ref, dhv_ref, dwg_ref, dwv_ref, dbg_ref, dbv_ref, dug_ref, duv_ref, dgb, dvb, accg, accv):
        i = pl.program_id(1)
        RC = FFN_ROWS
        for r0 in range(0, ts, RC):
            rows = slice(r0, r0 + RC)
            da = da_ref[rows, :]
            dgb[rows, :] = da * lg_ref[rows, :].astype(F32)
            dvb[rows, :] = da * lv_ref[rows, :].astype(F32)
        dan = jnp.where(i < n - 1, dan_ref[...], 0.0)
        dgb[ts:ts + HB, :] = dan * lgn_ref[...].astype(F32)[0:HB, :]
        dvb[ts:ts + HB, :] = dan * lvn_ref[...].astype(F32)[0:HB, :]

        @pl.when(i == 0)
        def _():
            accg[...] = jnp.zeros_like(accg)
            accv[...] = jnp.zeros_like(accv)

        row = lax.broadcasted_iota(jnp.int32, (HB, tc), 0)

        for dbuf, h_ref, w_ref, dh_ref, acc in ((dgb, g_ref, wg_ref, dhg_ref, accg), (dvb, v_ref, wv_ref, dhv_ref, accv)):
            cur = dbuf[0:HB, :]
            c7, c6 = pltpu.roll(cur, HB - 1, 0), pltpu.roll(cur, HB - 2, 0)
            for r0 in range(0, ts, RC):
                sums, dhus = [None] * 5, []
                for c in range(r0 // HB, (r0 + RC) // HB):
                    nxt_c = dbuf[(c + 1) * HB:(c + 2) * HB, :]
                    n7, n6 = pltpu.roll(nxt_c, HB - 1, 0), pltpu.roll(nxt_c, HB - 2, 0)
                    taps = [jnp.where(row >= HB - 2, n6, c6), jnp.where(row >= HB - 1, n7, c7), cur]
                    dhu = w_ref[2:3, :] * taps[2] + w_ref[1:2, :] * taps[1] + w_ref[0:1, :] * taps[0]
                    dhus.append(dhu)
                    h = h_ref[c * HB:(c + 1) * HB, :]
                    parts = [h * taps[0], h * taps[1], h * taps[2], taps[2], dhu]
                    sums = [p if s is None else s + p for s, p in zip(sums, parts)]
                    cur, c7, c6 = nxt_c, n7, n6
                dh_ref[r0:r0 + RC, :] = jnp.concatenate(dhus, axis=0).astype(BF16)
                for k in range(5):
                    acc[8 * k:8 * k + 8, :] += sums[k]

        @pl.when(i == n - 1)
        def _():
            for acc, dw_ref, db_ref, du_ref in ((accg, dwg_ref, dbg_ref, dug_ref), (accv, dwv_ref, dbv_ref, duv_ref)):
                for k in range(3):
                    dw_ref[k:k + 1, :] = _colsum(acc[8 * k:8 * k + 8, :])
                db_ref[...] = _colsum(acc[24:32, :])
                du_ref[...] = _colsum(acc[32:40, :])

    nxt = lambda hb: (lambda j, i: (jnp.minimum((i + 1) * (ts // hb), S // hb - 1), j))
    tile = lambda off: (lambda j, i: (i, j + off))
    vec = lambda rows: pl.BlockSpec((rows, tc), lambda j, i: (0, j))
    return _call(body, grid=(nj, n),
                 in_specs=[pl.BlockSpec((ts, tc), tile(0)), pl.BlockSpec((ts, tc), tile(nj)),
                           pl.BlockSpec((ts, tc), tile(0)), pl.BlockSpec((ts, tc), tile(0)),
                           pl.BlockSpec((HB16, tc), nxt(HB16)), pl.BlockSpec((HB16, tc), nxt(HB16)),
                           pl.BlockSpec((ts, tc), tile(0)), pl.BlockSpec((HB, tc), nxt(HB)),
                           pl.BlockSpec((3, tc), lambda j, i: (0, j)), pl.BlockSpec((3, tc), lambda j, i: (0, j + nj))],
                 out_specs=(pl.BlockSpec((ts, tc), tile(0)), pl.BlockSpec((ts, tc), tile(0)),
                            vec(3), vec(3), vec(1), vec(1), vec(1), vec(1)),
                 out_shape=(_sds((S, F), BF16), _sds((S, F), BF16), _sds((3, F), F32), _sds((3, F), F32),
                            _sds((1, F), F32), _sds((1, F), F32), _sds((1, F), F32), _sds((1, F), F32)),
                 scratch_shapes=[pltpu.VMEM((ts + HB, tc), F32), pltpu.VMEM((ts + HB, tc), F32),
                                 pltpu.VMEM((40, tc), F32), pltpu.VMEM((40, tc), F32)], name=name)(
        hu, hu, dact_dg, dact_dv, dact_dg, dact_dv, dact, dact, cw, cw)


def _t5_bucket(dist):
    max_exact = N_BUCKETS // 2
    d = np.maximum(dist, 1).astype(np.float64)
    large = max_exact + (np.log(d / max_exact) / math.log(MAX_DISTANCE / max_exact)
                         * (N_BUCKETS - max_exact)).astype(np.int32)
    large = np.minimum(large, N_BUCKETS - 1)
    return np.where(dist < max_exact, dist, large).astype(np.int32)


def _pattern_tables(window, dilation):
    qi = np.arange(ATTN_BLOCK)[:, None]
    kj = np.arange(2 * ATTN_BLOCK)[None, :]
    dist = qi + ATTN_BLOCK - kj
    valid = (dist >= 0) & (dist <= window // dilation)
    bucket = _t5_bucket(np.clip(dist, 0, None) * dilation)
    return bucket, valid


def _dilate_qkv(qkv, *, name, ts):
    S, C = qkv.shape
    dils = [d for (_, d) in PATTERNS if d > 1]

    def body(x_ref, nat_ref, *rest):
        outs, tmp = rest[:-1], rest[-1]
        nat_ref[...] = x_ref[...].astype(BF16)
        for j in range(C // LANES):
            cols = slice(j * LANES, (j + 1) * LANES)
            tmp[...] = x_ref[:, cols]
            for d, o_ref in zip(dils, outs):
                _dilate(tmp, o_ref, cols, d, ts, BF16)

    out_shape = (_sds((S, C), BF16),) + tuple(_sds((d, S // d, C), BF16) for d in dils)
    out_specs = (_dil_spec(1, ts, C),) + tuple(_dil_spec(d, ts, C) for d in dils)
    res = _call(body, grid=(S // ts,), in_specs=[_dil_spec(1, ts, C)], out_specs=out_specs, out_shape=out_shape,
                scratch_shapes=[pltpu.VMEM((ts, LANES), F32)], name=name)(qkv)
    return [res[0]] + [r.reshape(S, C) for r in res[1:]]


def _attn_group(S):
    nb_min = (S // PATTERNS[-1][1]) // ATTN_BLOCK
    return math.gcd(8, nb_min)


def _attn_fwd(qkv, bias, *, name, nb, G):
    S = qkv.shape[0]
    B, HD = ATTN_BLOCK, HEAD_DIM
    GR = G * B
    ng = S // GR

    def body(q_ref, k_ref, kh_ref, v_ref, vh_ref, bias_ref, o_ref, lse_ref, kbuf, vbuf):
        g = pl.program_id(1)
        halo_ok = (g * G) % nb != 0
        kbuf[0:B, :] = kh_ref[...]
        kbuf[B:B + GR, :] = k_ref[...]
        vbuf[0:B, :] = vh_ref[...]
        vbuf[B:B + GR, :] = v_ref[...]
        col = lax.broadcasted_iota(jnp.int32, (B, 2 * B), 1)
        head0 = lax.broadcasted_iota(jnp.int32, (B, LANES), 1) < HD

        for bi in range(G):
            r0 = bi * B
            q2 = q_ref[r0:r0 + B, :]
            kk = kbuf[r0:r0 + 2 * B, :]
            vv = vbuf[r0:r0 + 2 * B, :]
            zero = jnp.zeros_like(q2)
            os_, ls_, lses = [], [], []
            for hh in range(2):
                qh = jnp.where(head0, q2, zero) if hh == 0 else jnp.where(head0, zero, q2)
                s = lax.dot_general(qh, kk, (((1,), (1,)), ((), ())), preferred_element_type=F32)
                s = s + bias_ref[hh]
                if bi == 0:
                    s = jnp.where(jnp.logical_and(jnp.logical_not(halo_ok), col < B), NEG, s)
                m = jnp.max(s, axis=1, keepdims=True)
                p = jnp.exp(s - m)
                l = jnp.sum(p, axis=1, keepdims=True)
                os_.append(jnp.dot(p.astype(BF16), vv, preferred_element_type=F32))
                ls_.append(l)
                lses.append(m + jnp.log(l))
            o_ref[r0:r0 + B, :] = (jnp.where(head0, os_[0], os_[1]) / jnp.where(head0, ls_[0], ls_[1])).astype(BF16)
            lse_ref[r0:r0 + B, :] = jnp.where(head0, lses[0], lses[1])

    halo = lambda off: (lambda hp, g: (jnp.maximum(g * G - 1, 0), off + hp))
    main = lambda off: (lambda hp, g: (g, off + hp))
    return _call(body, grid=(4, ng),
                 in_specs=[pl.BlockSpec((GR, LANES), main(0)), pl.BlockSpec((GR, LANES), main(4)),
                           pl.BlockSpec((B, LANES), halo(4)), pl.BlockSpec((GR, LANES), main(8)),
                           pl.BlockSpec((B, LANES), halo(8)), pl.BlockSpec((2, B, 2 * B), lambda hp, g: (hp, 0, 0))],
                 out_specs=(pl.BlockSpec((GR, LANES), main(0)), pl.BlockSpec((GR, LANES), main(0))),
                 out_shape=(_sds((S, ATTN_CH), BF16), _sds((S, ATTN_CH), F32)),
                 scratch_shapes=[pltpu.VMEM((B + GR, LANES), BF16), pltpu.VMEM((B + GR, LANES), BF16)], name=name)(
        qkv, qkv, qkv, qkv, qkv, bias)


def _dil_spec(d, ts, C):
    if d == 1:
        return pl.BlockSpec((ts, C), lambda i: (i, 0))
    return pl.BlockSpec((d, ts // d, C), lambda i: (0, i, 0))


def _dil_view(a, d):
    return a if d == 1 else a.reshape(d, a.shape[0] // d, a.shape[1])


def _dilate(nat_tmp, dst_ref, cols, d, ts, dtype=F32):
    for r in range(d):
        dst_ref[r, :, cols] = nat_tmp[pl.ds(r, ts // d, stride=d), :].astype(dtype)


def _undilate(src_ref, nat_tmp, cols, d, ts, accumulate=False):
    for r in range(d):
        rows = pl.ds(r, ts // d, stride=d)
        if accumulate:
            nat_tmp[rows, :] = nat_tmp[rows, :] + src_ref[r, :, cols].astype(F32)
        else:
            nat_tmp[rows, :] = src_ref[r, :, cols].astype(F32)


def _attn_merge(o_list, lse_list, dils, *, name, ts):
    S, C = o_list[0].shape
    P = len(o_list)
    nd = sum(1 for d in dils if d > 1)

    def body(*refs):
        o_refs, l_refs = refs[:P], refs[P:2 * P]
        out_ref, lse_ref = refs[2 * P], refs[2 * P + 1]
        lse_d_refs = refs[2 * P + 2:2 * P + 2 + nd]
        scratch = list(refs[2 * P + 2 + nd:])
        tmp = scratch.pop()
        for j in range(C // LANES):
            cols = slice(j * LANES, (j + 1) * LANES)
            os_, ls, free = [], [], list(scratch)
            for o_ref, l_ref, d in zip(o_refs, l_refs, dils):
                if d > 1:
                    so, sl = free.pop(0), free.pop(0)
                    _undilate(o_ref, so, cols, d, ts)
                    _undilate(l_ref, sl, cols, d, ts)
                    os_.append(so[...])
                    ls.append(sl[...])
                else:
                    os_.append(o_ref[:, cols].astype(F32))
                    ls.append(l_ref[:, cols])
            m = ls[0]
            for l in ls[1:]:
                m = jnp.maximum(m, l)
            ws = [jnp.exp(l - m) for l in ls]
            den = ws[0]
            for w in ws[1:]:
                den = den + w
            num = ws[0] * os_[0]
            for w, o in zip(ws[1:], os_[1:]):
                num = num + w * o
            out_ref[:, cols] = num / den
            tmp[...] = m + jnp.log(den)
            lse_ref[:, cols] = tmp[...]
            for l_out, d in zip(lse_d_refs, [d for d in dils if d > 1]):
                _dilate(tmp, l_out, cols, d, ts)

    row = _dil_spec(1, ts, C)
    dd = [d for d in dils if d > 1]
    res = _call(body, grid=(S // ts,), in_specs=[_dil_spec(d, ts, C) for d in dils] * 2,
                out_specs=(row, row) + tuple(_dil_spec(d, ts, C) for d in dd),
                out_shape=(_sds((S, C), F32), _sds((S, C), F32)) + tuple(_sds((d, S // d, C), F32) for d in dd),
                scratch_shapes=[pltpu.VMEM((ts, LANES), F32)] * (2 * nd + 1), name=name)(
        *[_dil_view(o, d) for o, d in zip(o_list, dils)], *[_dil_view(l, d) for l, d in zip(lse_list, dils)])
    lse_by_d = {1: res[1]}
    lse_by_d.update({d: r.reshape(S, C) for d, r in zip(dd, res[2:])})
    return res[0], [lse_by_d[d] for d in dils]


def _attn_prep(dout, out, dils, *, name, ts):
    S, C = out.shape
    HD = HEAD_DIM
    dd = [d for d in dils if d > 1]
    nd = len(dd)

    def body(do_ref, o_ref, d_ref, dob_ref, *rest):
        outs, tmp_d, tmp_o = rest[:2 * nd], rest[2 * nd], rest[2 * nd + 1]
        dob_ref[...] = do_ref[...].astype(BF16)
        for j in range(C // LANES):
            cols = slice(j * LANES, (j + 1) * LANES)
            do = do_ref[:, cols]
            prod = do * o_ref[:, cols]
            tmp_o[...] = do
            for h in range(LANES // HD):
                cs = slice(h * HD, (h + 1) * HD)
                tmp_d[:, cs] = jnp.broadcast_to(jnp.sum(prod[:, cs], axis=1, keepdims=True), (ts, HD))
            d_ref[:, cols] = tmp_d[...]
            for d, dd_out, do_out in zip(dd, outs[:nd], outs[nd:]):
                _dilate(tmp_d, dd_out, cols, d, ts)
                _dilate(tmp_o, do_out, cols, d, ts, BF16)

    row = _dil_spec(1, ts, C)
    res = _call(body, grid=(S // ts,), in_specs=[row, row],
                out_specs=(row, row) + tuple(_dil_spec(d, ts, C) for d in dd) * 2,
                out_shape=(_sds((S, C), F32), _sds((S, C), BF16)) + tuple(_sds((d, S // d, C), F32) for d in dd)
                + tuple(_sds((d, S // d, C), BF16) for d in dd),
                scratch_shapes=[pltpu.VMEM((ts, LANES), F32)] * 2, name=name)(dout, out)
    dd_by_d, do_by_d = {1: res[0]}, {1: res[1]}
    dd_by_d.update({d: r.reshape(S, C) for d, r in zip(dd, res[2:2 + nd])})
    do_by_d.update({d: r.reshape(S, C) for d, r in zip(dd, res[2 + nd:])})
    return [dd_by_d[d] for d in dils], [do_by_d[d] for d in dils]


def _attn_bwd(qkv, do, lse, dd, bias, *, name, nb, G):
    S = qkv.shape[0]
    B, HD = ATTN_BLOCK, HEAD_DIM
    GR = G * B
    ng = S // GR
    nblk = S // B

    def body(q_ref, k_ref, kh_ref, v_ref, vh_ref, do_ref, lse_ref, dd_ref, qn_ref, don_ref, lsen_ref, ddn_ref,
             bias_ref, dq_ref, dk_ref, dv_ref, dbias_ref, kbuf, vbuf, dkp, dvp, dko_s, dvo_s):
        g = pl.program_id(1)
        halo_ok = (g * G) % nb != 0
        next_ok = jnp.logical_and(((g + 1) * G) % nb != 0, g < ng - 1)
        kbuf[0:B, :] = kh_ref[...]
        kbuf[B:B + GR, :] = k_ref[...]
        vbuf[0:B, :] = vh_ref[...]
        vbuf[B:B + GR, :] = v_ref[...]
        col = lax.broadcasted_iota(jnp.int32, (B, 2 * B), 1)

        def tn_dot(a, b):
            return lax.dot_general(a, b, (((0,), (0,)), ((), ())), preferred_element_type=F32)

        @pl.when(g == 0)
        def _():
            dbias_ref[...] = jnp.zeros_like(dbias_ref)

        head0 = lax.broadcasted_iota(jnp.int32, (B, LANES), 1) < HD

        def nt_dot(a, b):
            return lax.dot_general(a, b, (((1,), (1,)), ((), ())), preferred_element_type=F32)

        def one_head(x, hh):
            zero = jnp.zeros_like(x)
            return jnp.where(head0, x, zero) if hh == 0 else jnp.where(head0, zero, x)

        def pick(per_head):
            return jnp.where(head0, per_head[0], per_head[1])

        for bi in range(G):
            r0 = bi * B
            q2 = q_ref[r0:r0 + B, :]
            do2 = do_ref[r0:r0 + B, :]
            kk = kbuf[r0:r0 + 2 * B, :]
            vv = vbuf[r0:r0 + 2 * B, :]
            dq, dko, dvo, dkq, dvq = [], [], [], [], []
            for hh in range(2):
                s = nt_dot(one_head(q2, hh), kk) + bias_ref[hh]
                if bi == 0:
                    s = jnp.where(jnp.logical_and(jnp.logical_not(halo_ok), col < B), NEG, s)
                p = jnp.exp(s - lse_ref[r0:r0 + B, hh * HD:hh * HD + 1])
                dp = nt_dot(one_head(do2, hh), vv)
                ds = p * (dp - dd_ref[r0:r0 + B, hh * HD:hh * HD + 1])
                dbias_ref[hh] += ds
                ds_bf, p_bf = ds.astype(BF16), p.astype(BF16)
                dq.append(jnp.dot(ds_bf, kk, preferred_element_type=F32))
                dko.append(tn_dot(ds_bf[:, B:], q2))
                dvo.append(tn_dot(p_bf[:, B:], do2))
                if bi > 0:
                    dkq.append(tn_dot(ds_bf[:, :B], q2))
                    dvq.append(tn_dot(p_bf[:, :B], do2))
            dq_ref[r0:r0 + B, :] = pick(dq).astype(BF16)
            dko_s[r0:r0 + B, :] = pick(dko)
            dvo_s[r0:r0 + B, :] = pick(dvo)
            if bi > 0:
                dkp[r0 - B:r0, :] = pick(dkq)
                dvp[r0 - B:r0, :] = pick(dvq)

        @pl.when(next_ok)
        def _():
            qn = qn_ref[...]
            don = don_ref[...]
            kl = kbuf[GR:GR + B, :]
            vl = vbuf[GR:GR + B, :]
            dkq, dvq = [], []
            for hh in range(2):
                s = nt_dot(one_head(qn, hh), kl) + bias_ref[hh, :, 0:B]
                p = jnp.exp(s - lsen_ref[:, hh * HD:hh * HD + 1])
                dp = nt_dot(one_head(don, hh), vl)
                ds = p * (dp - ddn_ref[:, hh * HD:hh * HD + 1])
                dkq.append(tn_dot(ds.astype(BF16), qn))
                dvq.append(tn_dot(p.astype(BF16), don))
            dkp[GR - B:GR, :] = pick(dkq)
            dvp[GR - B:GR, :] = pick(dvq)

        @pl.when(jnp.logical_not(next_ok))
        def _():
            dkp[GR - B:GR, :] = jnp.zeros((B, LANES), F32)
            dvp[GR - B:GR, :] = jnp.zeros((B, LANES), F32)

        dk_ref[...] = (dko_s[...] + dkp[...]).astype(BF16)
        dv_ref[...] = (dvo_s[...] + dvp[...]).astype(BF16)

    halo = lambda off: (lambda hp, g: (jnp.maximum(g * G - 1, 0), off + hp))
    main = lambda off: (lambda hp, g: (g, off + hp))
    nxt = lambda off: (lambda hp, g: (jnp.minimum((g + 1) * G, nblk - 1), off + hp))
    big, small = (lambda m: pl.BlockSpec((GR, LANES), m)), (lambda m: pl.BlockSpec((B, LANES), m))
    return _call(body, grid=(4, ng),
                 in_specs=[big(main(0)), big(main(4)), small(halo(4)), big(main(8)), small(halo(8)),
                           big(main(0)), big(main(0)), big(main(0)),
                           small(nxt(0)), small(nxt(0)), small(nxt(0)), small(nxt(0)),
                           pl.BlockSpec((2, B, 2 * B), lambda hp, g: (hp, 0, 0))],
                 out_specs=(big(main(0)), big(main(0)), big(main(0)),
                            pl.BlockSpec((2, B, 2 * B), lambda hp, g: (hp, 0, 0))),
                 out_shape=(_sds((S, ATTN_CH), BF16),) * 3 + (_sds((ATTN_HEADS, B, 2 * B), F32),),
                 scratch_shapes=[pltpu.VMEM((B + GR, LANES), BF16), pltpu.VMEM((B + GR, LANES), BF16)]
                 + [pltpu.VMEM((GR, LANES), F32)] * 4, name=name)(
        qkv, qkv, qkv, qkv, qkv, do, lse, dd, qkv, do, lse, dd, bias)


def _attn_combine(dq_list, dk_list, dv_list, dils, *, name, ts):
    S, C = dq_list[0].shape
    P = len(dq_list)
    scale = HEAD_DIM ** -0.5
    assert dils[0] == 1

    def body(*refs):
        out_refs, acc = refs[3 * P:3 * P + 3], refs[3 * P + 3]
        for part in range(3):
            for j in range(C // LANES):
                cols = slice(j * LANES, (j + 1) * LANES)
                acc[...] = refs[part * P][:, cols].astype(F32)
                for r, d in zip(refs[part * P + 1:(part + 1) * P], dils[1:]):
                    _undilate(r, acc, cols, d, ts, accumulate=True)
                tot = acc[...]
                if part == 0:
                    tot = tot * scale
                out_refs[part][:, cols] = tot.astype(BF16)

    return _call(body, grid=(S // ts,), in_specs=[_dil_spec(d, ts, C) for d in dils] * 3,
                 out_specs=(_dil_spec(1, ts, C),) * 3, out_shape=(_sds((S, C), BF16),) * 3,
                 scratch_shapes=[pltpu.VMEM((ts, LANES), F32)], name=name)(
        *[_dil_view(a, d) for lst in (dq_list, dk_list, dv_list) for a, d in zip(lst, dils)])


def _bias_tables(table, bucket_flat, *, name):
    P, _, K = bucket_flat.shape
    H = table.shape[1]
    KC = 4096

    def body(t_ref, bk_ref, out_ref):
        row = lax.broadcasted_iota(jnp.int32, (N_BUCKETS, KC), 0)
        for c in range(K // KC):
            bk = bk_ref[0, :, c * KC:(c + 1) * KC]
            onehot = (row == bk).astype(F32)
            vals = jnp.dot(t_ref[...], onehot, preferred_element_type=F32, precision=lax.Precision.HIGHEST)
            out_ref[0, :, c * KC:(c + 1) * KC] = jnp.where(bk >= 0, vals, NEG)

    return _call(body, grid=(P,),
                 in_specs=[pl.BlockSpec((H, N_BUCKETS), lambda p: (0, 0)), pl.BlockSpec((1, 1, K), lambda p: (p, 0, 0))],
                 out_specs=pl.BlockSpec((1, H, K), lambda p: (p, 0, 0)), out_shape=_sds((P, H, K), F32),
                 name=name)(table.T, bucket_flat)


def _bias_grad(dbias_flat, bucket_flat, *, name):
    P, H, K = dbias_flat.shape
    KC = 4096

    def body(db_ref, bk_ref, out_ref):
        p = pl.program_id(0)
        acc = jnp.zeros((N_BUCKETS, H), F32)
        row = lax.broadcasted_iota(jnp.int32, (N_BUCKETS, KC), 0)
        for c in range(K // KC):
            onehot = (row == bk_ref[0, :, c * KC:(c + 1) * KC]).astype(F32)
            acc = acc + lax.dot_general(onehot, db_ref[0, :, c * KC:(c + 1) * KC], (((1,), (1,)), ((), ())),
                                        preferred_element_type=F32, precision=lax.Precision.HIGHEST)

        @pl.when(p == 0)
        def _():
            out_ref[...] = acc

        @pl.when(p > 0)
        def _():
            out_ref[...] += acc

    return _call(body, grid=(P,),
                 in_specs=[pl.BlockSpec((1, H, K), lambda p: (p, 0, 0)), pl.BlockSpec((1, 1, K), lambda p: (p, 0, 0))],
                 out_specs=pl.BlockSpec((N_BUCKETS, H), lambda p: (0, 0)), out_shape=_sds((N_BUCKETS, H), F32),
                 name=name)(dbias_flat, bucket_flat)


def _allgather(blocks, *, name):
    n = len(blocks)

    def body(*refs):
        x_refs, out_refs = refs[:n], refs[n:2 * n]
        send_sems, recv_sems, local_sems = refs[2 * n:]
        x, y, c = lax.axis_index("x"), lax.axis_index("y"), lax.axis_index("c")
        me, sibling = (x, y, c), (x, y, 1 - c)
        chips = [(1 - x, y), (x, 1 - y), (1 - x, 1 - y)]

        def copy(i, k, blk, to, own=False):
            slot = out_refs[i].at[4 * blk[0] + 2 * blk[1] + blk[2]]
            return pltpu.make_async_remote_copy(src_ref=x_refs[i] if own else slot, dst_ref=slot,
                                                send_sem=send_sems.at[7 * i + k], recv_sem=recv_sems.at[7 * i + k],
                                                device_id=to, device_id_type=MESH)

        mine = [pltpu.make_async_copy(x_refs[i], out_refs[i].at[4 * x + 2 * y + c], local_sems.at[i])
                for i in range(n)]
        for cp in mine:
            cp.start()
        first = []
        for i in range(n):
            first.append(copy(i, 0, me, sibling, own=True))
            first += [copy(i, 1 + j, me, (*chip, c), own=True) for j, chip in enumerate(chips)]
        for cp in first:
            cp.start()
        passed = []
        for j, chip in enumerate(chips):
            for i in range(n):
                copy(i, 1 + j, (*chip, c), me).wait_recv()
                fwd = copy(i, 4 + j, (*chip, c), sibling)
                fwd.start()
                passed.append(fwd)
        for i in range(n):
            copy(i, 0, sibling, me).wait_recv()
            for j, chip in enumerate(chips):
                copy(i, 4 + j, (*chip, 1 - c), me).wait_recv()
        for cp in first + passed:
            cp.wait_send()
        for cp in mine:
            cp.wait()

    any_spec = pl.BlockSpec(memory_space=pl.ANY)
    return pl.pallas_call(body, out_shape=tuple(_sds((N_DEV,) + b.shape, b.dtype) for b in blocks),
                          in_specs=[any_spec] * n, out_specs=(any_spec,) * n,
                          scratch_shapes=[pltpu.SemaphoreType.DMA((7 * n,)), pltpu.SemaphoreType.DMA((7 * n,)),
                                          pltpu.SemaphoreType.DMA((n,))], name=name)(*blocks)


def _exchange(sends, *, name):
    n = len(sends)

    def body(*refs):
        send_refs, recv_refs = refs[:n], refs[n:2 * n]
        send_sems, recv_sems, local_sems = refs[2 * n:]
        x, y, c = lax.axis_index("x"), lax.axis_index("y"), lax.axis_index("c")
        me = 4 * x + 2 * y + c
        mine = [pltpu.make_async_copy(send_refs[i].at[me], recv_refs[i].at[me], local_sems.at[i]) for i in range(n)]
        for cp in mine:
            cp.start()
        copies = []
        for k in range(1, N_DEV):
            px = 1 - x if k & 4 else x
            py = 1 - y if k & 2 else y
            pc = 1 - c if k & 1 else c
            for i in range(n):
                cp = pltpu.make_async_remote_copy(src_ref=send_refs[i].at[4 * px + 2 * py + pc],
                                                  dst_ref=recv_refs[i].at[me],
                                                  send_sem=send_sems.at[7 * i + k - 1],
                                                  recv_sem=recv_sems.at[7 * i + k - 1],
                                                  device_id=(px, py, pc), device_id_type=MESH)
                cp.start()
                copies.append(cp)
        for cp in copies:
            cp.wait_recv()
        for cp in copies:
            cp.wait_send()
        for cp in mine:
            cp.wait()

    any_spec = pl.BlockSpec(memory_space=pl.ANY)
    return pl.pallas_call(body, out_shape=tuple(_sds(s.shape, s.dtype) for s in sends), in_specs=[any_spec] * n,
                          out_specs=(any_spec,) * n,
                          scratch_shapes=[pltpu.SemaphoreType.DMA((7 * n,)), pltpu.SemaphoreType.DMA((7 * n,)),
                                          pltpu.SemaphoreType.DMA((n,))], name=name)(*sends)


def _adamw(w, m, v, g_parts, *, name, tr):
    R, W = w.shape
    bc1 = 1.0 - ADAM_B1 ** ADAM_STEP
    bc2 = 1.0 - ADAM_B2 ** ADAM_STEP

    def body(w_ref, m_ref, v_ref, g_ref, go_ref, d_ref, mo_ref, vo_ref):
        g = g_ref[0].astype(F32)
        for i in range(1, N_DEV):
            g = g + g_ref[i].astype(F32)
        mn = ADAM_B1 * m_ref[...] + (1.0 - ADAM_B1) * g
        vn = ADAM_B2 * v_ref[...] + (1.0 - ADAM_B2) * (g * g)
        m_hat = mn / bc1
        v_hat = vn / bc2
        go_ref[...] = g
        d_ref[...] = -ADAM_LR * (m_hat / (jnp.sqrt(v_hat) + ADAM_EPS) + ADAM_WD * w_ref[...])
        mo_ref[...] = mn
        vo_ref[...] = vn

    row = pl.BlockSpec((tr, W), lambda i: (i, 0))
    return _call(body, grid=(R // tr,), in_specs=[row, row, row, pl.BlockSpec((N_DEV, tr, W), lambda i: (0, i, 0))],
                 out_specs=(row,) * 4, out_shape=(_sds((R, W), F32),) * 4, name=name)(w, m, v, g_parts)


def _round_up(n, k):
    return -(-n // k) * k


def _pack(arrs, width, row_mult):
    pieces, offs, r = [], [], 0
    for a in arrs:
        n = a.size
        rows = _round_up(n, width) // width
        flat = a.reshape(-1)
        if rows * width != n:
            flat = jnp.pad(flat, (0, rows * width - n))
        pieces.append(flat.reshape(rows, width))
        offs.append((r, rows, n))
        r += rows
    total = _round_up(r, row_mult)
    if total != r:
        pieces.append(jnp.zeros((total - r, width), pieces[0].dtype))
    return jnp.concatenate(pieces, axis=0), offs


def _unpack(pack, offs, shapes):
    out = []
    for (r, rows, n), shp in zip(offs, shapes):
        out.append(pack[r:r + rows].reshape(-1)[:n].reshape(shp))
    return out


def _gather_axis(full8, axis):
    moved = jnp.moveaxis(full8, 0, axis)
    shp = list(moved.shape)
    shp[axis:axis + 2] = [shp[axis] * shp[axis + 1]]
    return moved.reshape(shp)


def _split_axis(full, axis):
    shp = list(full.shape)
    shp[axis:axis + 1] = [N_DEV, shp[axis] // N_DEV]
    return jnp.moveaxis(full.reshape(shp), axis, 0)


def kernel(x, w_in, b_in, conv_dw_w, conv_dw_b, conv_ln_g, conv_ln_b, rel_bias_table, gmlp_ln_g, gmlp_ln_b, gmlp_w_s, gmlp_b_s, w_out, b_out, ln1_g, ln1_b, ffn_w_up, ffn_b_up, ffn_conv_w, ffn_conv_b, ffn_w_down, ffn_b_down, ln2_g, ln2_b, loss_target, m_w_in, m_b_in, m_conv_dw_w, m_conv_dw_b, m_conv_ln_g, m_conv_ln_b, m_rel_bias_table, m_gmlp_ln_g, m_gmlp_ln_b, m_gmlp_w_s, m_gmlp_b_s, m_w_out, m_b_out, m_ln1_g, m_ln1_b, m_ffn_w_up, m_ffn_b_up, m_ffn_conv_w, m_ffn_conv_b, m_ffn_w_down, m_ffn_b_down, m_ln2_g, m_ln2_b, v_w_in, v_b_in, v_conv_dw_w, v_conv_dw_b, v_conv_ln_g, v_conv_ln_b, v_rel_bias_table, v_gmlp_ln_g, v_gmlp_ln_b, v_gmlp_w_s, v_gmlp_b_s, v_w_out, v_b_out, v_ln1_g, v_ln1_b, v_ffn_w_up, v_ffn_b_up, v_ffn_conv_w, v_ffn_conv_b, v_ffn_w_down, v_ffn_b_down, v_ln2_g, v_ln2_b):
    W = dict(w_in=w_in, b_in=b_in, conv_dw_w=conv_dw_w, conv_dw_b=conv_dw_b, conv_ln_g=conv_ln_g,
             conv_ln_b=conv_ln_b, rel_bias_table=rel_bias_table, gmlp_ln_g=gmlp_ln_g, gmlp_ln_b=gmlp_ln_b,
             gmlp_w_s=gmlp_w_s, gmlp_b_s=gmlp_b_s, w_out=w_out, b_out=b_out, ln1_g=ln1_g, ln1_b=ln1_b,
             ffn_w_up=ffn_w_up, ffn_b_up=ffn_b_up, ffn_conv_w=ffn_conv_w, ffn_conv_b=ffn_conv_b,
             ffn_w_down=ffn_w_down, ffn_b_down=ffn_b_down, ln2_g=ln2_g, ln2_b=ln2_b)
    Mo = dict(w_in=m_w_in, b_in=m_b_in, conv_dw_w=m_conv_dw_w, conv_dw_b=m_conv_dw_b, conv_ln_g=m_conv_ln_g,
              conv_ln_b=m_conv_ln_b, rel_bias_table=m_rel_bias_table, gmlp_ln_g=m_gmlp_ln_g, gmlp_ln_b=m_gmlp_ln_b,
              gmlp_w_s=m_gmlp_w_s, gmlp_b_s=m_gmlp_b_s, w_out=m_w_out, b_out=m_b_out, ln1_g=m_ln1_g, ln1_b=m_ln1_b,
              ffn_w_up=m_ffn_w_up, ffn_b_up=m_ffn_b_up, ffn_conv_w=m_ffn_conv_w, ffn_conv_b=m_ffn_conv_b,
              ffn_w_down=m_ffn_w_down, ffn_b_down=m_ffn_b_down, ln2_g=m_ln2_g, ln2_b=m_ln2_b)
    Vo = dict(w_in=v_w_in, b_in=v_b_in, conv_dw_w=v_conv_dw_w, conv_dw_b=v_conv_dw_b, conv_ln_g=v_conv_ln_g,
              conv_ln_b=v_conv_ln_b, rel_bias_table=v_rel_bias_table, gmlp_ln_g=v_gmlp_ln_g, gmlp_ln_b=v_gmlp_ln_b,
              gmlp_w_s=v_gmlp_w_s, gmlp_b_s=v_gmlp_b_s, w_out=v_w_out, b_out=v_b_out, ln1_g=v_ln1_g, ln1_b=v_ln1_b,
              ffn_w_up=v_ffn_w_up, ffn_b_up=v_ffn_b_up, ffn_conv_w=v_ffn_conv_w, ffn_conv_b=v_ffn_conv_b,
              ffn_w_down=v_ffn_w_down, ffn_b_down=v_ffn_b_down, ln2_g=v_ln2_g, ln2_b=v_ln2_b)

    xs = x[0]
    target = loss_target[0]
    S, D = xs.shape
    F2 = ffn_b_up.shape[1]
    F = F2 // 2
    ts = min(512, S)
    G = _attn_group(S)
    tc = F // 2 if (F // 2) % LANES == 0 else F

    mat_names = SHARDED[:4]
    payload = [W[n].astype(BF16) if n in mat_names else W[n] for n in SHARDED]
    wall = _allgather(payload, name="weight_allgather")
    full = {n: _gather_axis(parts, SHARD_AXIS[n]) for n, parts in zip(SHARDED, wall)}

    tables = [_pattern_tables(w, d) for (w, d) in PATTERNS]
    bucket_flat = jnp.asarray(np.stack([np.where(v, b, -1).reshape(1, -1) for (b, v) in tables]).astype(np.int32))
    bias_all = _bias_tables(rel_bias_table, bucket_flat, name="bias_tables")
    biases = [bias_all[p].reshape(ATTN_HEADS, ATTN_BLOCK, 2 * ATTN_BLOCK) for p in range(len(PATTERNS))]
    nbs = [(S // d) // ATTN_BLOCK for (_, d) in PATTERNS]
    dils = [d for (_, d) in PATTERNS]
    scale = HEAD_DIM ** -0.5

    saved = []
    cur = xs
    for l in range(DEPTH):
        Win, Wout, Wup, Wdown = full['w_in'][l], full['w_out'][l], full['ffn_w_up'][l], full['ffn_w_down'][l]
        qcols = slice(2 * CONV_CH, 2 * CONV_CH + ATTN_CH)
        Win_s = Win.at[:, qcols].multiply(scale)
        b_in_s = b_in[l].at[qcols].multiply(scale)
        h_a, qkv, h_c = _mm([cur], [Win_s], bias=b_in_s, tm=ts, name="in_proj",
                            splits=((2 * CONV_CH, F32, 1.0), (3 * ATTN_CH, F32, 1.0), (2 * GMLP_CH, F32, 1.0)))
        conv_out, hc = _conv_fwd(h_a, full['conv_dw_w'][l], conv_dw_b[l], conv_ln_g[l], conv_ln_b[l],
                                 name="conv_fwd", ts=ts)
        qkv_d = _dilate_qkv(qkv, name="dilate_qkv", ts=ts)
        o_ps, lse_ps = [], []
        for p, d in enumerate(dils):
            o_p, lse_p = _attn_fwd(qkv_d[p], biases[p], name=f"attn_fwd_d{d}", nb=nbs[p], G=G)
            o_ps.append(o_p)
            lse_ps.append(lse_p)
        attn_out, lse = _attn_merge(o_ps, lse_ps, dils, name="attn_merge", ts=ts)
        w_tril = jnp.tril(gmlp_w_s[l]).astype(BF16)
        bs_rows = jnp.repeat(gmlp_b_s[l].T, GMLP_GROUP_DIM, axis=1)
        gm_out = _gmlp_fwd(h_c, gmlp_ln_g[l], gmlp_ln_b[l], w_tril, bs_rows, name="gmlp_fwd", ts=ts)
        x1, xhat1, rstd1 = _mm([conv_out, attn_out, gm_out],
                               [Wout[:CONV_CH], Wout[CONV_CH:CONV_CH + ATTN_CH], Wout[CONV_CH + ATTN_CH:]],
                               bias=b_out[l], resid=cur, resid_scale=ALPHA, ln=(ln1_g[l], ln1_b[l]), tm=ts,
                               name="out_proj_ln")
        hu = _mm([x1], [Wup], bias=ffn_b_up[l], tm=ts, tn=F, name="ffn_up")
        act, act_dg, act_dv = _ffn_act_fwd(hu, full['ffn_conv_w'][l], ffn_conv_b[l], name="ffn_act_fwd",
                                           ts=min(256, S), tc=tc)
        x2, xhat2, rstd2 = _mm([act], [Wdown], bias=ffn_b_down[l], resid=x1, resid_scale=ALPHA,
                               ln=(ln2_g[l], ln2_b[l]), tm=ts, name="ffn_down_ln")
        saved.append(dict(x0=cur, h_a=h_a, h_c=h_c, qkv_d=qkv_d, hc=hc, conv_out=conv_out, attn_out=attn_out,
                          lse=lse, gm_out=gm_out, w_tril=w_tril, bs_rows=bs_rows, x1=x1, xhat1=xhat1, rstd1=rstd1,
                          hu=hu, act=act, act_dg=act_dg, act_dv=act_dv, xhat2=xhat2, rstd2=rstd2))
        cur = x2

    grads = {n: [None] * DEPTH for n in WEIGHTS if n != 'rel_bias_table'}
    drel = None
    dx = None
    loss_part = None
    tk = min(1024, S)
    for l in reversed(range(DEPTH)):
        sv = saved[l]
        Win, Wout, Wup, Wdown = full['w_in'][l], full['w_out'][l], full['ffn_w_up'][l], full['ffn_w_down'][l]
        if dx is None:
            dz2, dg2, db2, dzs2, loss_part = _ln_bwd(sv['xhat2'], sv['rstd2'], ln2_g[l], b=ln2_b[l], target=target,
                                                     name="ln2_bwd_loss", ts=ts)
        else:
            dz2, dg2, db2, dzs2 = _ln_bwd(sv['xhat2'], sv['rstd2'], ln2_g[l], dy=dx, name="ln_bwd", ts=ts)
        grads['ln2_g'][l], grads['ln2_b'][l], grads['ffn_b_down'][l] = dg2[0], db2[0], dzs2[0]
        grads['ffn_w_down'][l] = _mm_tn(sv['act'], dz2, tm=F // 2 if (F // 2) % LANES == 0 else F, tn=D, tk=tk,
                                        name="dw_down")
        dact = _mm([dz2], [Wdown.T], tm=ts, name="dact")
        dhg, dhv, dwg, dwv, dbg, dbv, dug, duv = _ffn_act_bwd(sv['hu'], dact, sv['act_dg'], sv['act_dv'],
                                                              full['ffn_conv_w'][l], name="ffn_act_bwd",
                                                              ts=min(256, S), tc=tc)
        grads['ffn_conv_w'][l] = jnp.concatenate([dwg, dwv], axis=1)
        grads['ffn_conv_b'][l] = jnp.concatenate([dbg, dbv], axis=1)[0]
        grads['ffn_b_up'][l] = jnp.concatenate([dug, duv], axis=1)[0]
        grads['ffn_w_up'][l] = jnp.concatenate(
            [_mm_tn(sv['x1'], dhg, tm=D, tn=tc, tk=tk, name="dw_up"),
             _mm_tn(sv['x1'], dhv, tm=D, tn=tc, tk=tk, name="dw_up")], axis=1)
        WupT = Wup.T
        dx1 = _mm([dhg, dhv], [WupT[:F], WupT[F:]], resid=dz2, resid_scale=ALPHA, tm=ts, name="dx1")
        dz1, dg1, db1, dzs1 = _ln_bwd(sv['xhat1'], sv['rstd1'], ln1_g[l], dy=dx1, name="ln_bwd", ts=ts)
        grads['ln1_g'][l], grads['ln1_b'][l], grads['b_out'][l] = dg1[0], db1[0], dzs1[0]
        grads['w_out'][l] = jnp.concatenate(
            [_mm_tn(sv['conv_out'], dz1, tm=CONV_CH, tn=D, tk=tk, name="dw_out_conv"),
             _mm_tn(sv['attn_out'], dz1, tm=ATTN_CH, tn=D, tk=tk, name="dw_out_attn"),
             _mm_tn(sv['gm_out'], dz1, tm=GMLP_CH, tn=D, tk=tk, name="dw_out_conv")], axis=0)
        dc_conv, dc_attn, dc_gm = _mm([dz1], [Wout.T], tm=ts, name="dcat",
                                      splits=((CONV_CH, F32, 1.0), (ATTN_CH, F32, 1.0), (GMLP_CH, F32, 1.0)))
        dh_a, ddw, ddwb, dclg, dclb = _conv_bwd(sv['h_a'], sv['hc'], dc_conv, full['conv_dw_w'][l], conv_ln_g[l],
                                                conv_ln_b[l], name="conv_bwd", ts=ts)
        grads['conv_dw_w'][l], grads['conv_dw_b'][l] = ddw[:CONV_WIDTH], ddwb[0]
        grads['conv_ln_g'][l], grads['conv_ln_b'][l] = dclg[0], dclb[0]
        dd_d, do_d = _attn_prep(dc_attn, sv['attn_out'], dils, name="attn_prep", ts=ts)
        dqs, dks, dvs, dbs = [], [], [], []
        for p, d in enumerate(dils):
            dq, dk, dv, dbias = _attn_bwd(sv['qkv_d'][p], do_d[p], sv['lse'][p], dd_d[p], biases[p],
                                          name=f"attn_bwd_d{d}", nb=nbs[p], G=G)
            dqs.append(dq)
            dks.append(dk)
            dvs.append(dv)
            dbs.append(dbias.reshape(1, ATTN_HEADS, -1))
        dqkv = _attn_combine(dqs, dks, dvs, dils, name="attn_combine", ts=ts)
        dr = _bias_grad(jnp.concatenate(dbs, axis=0), bucket_flat, name="bias_grad")
        drel = dr if drel is None else drel + dr
        w_tril_t = jnp.swapaxes(sv['w_tril'], 1, 2)
        dh_c, dws, dbs_acc, dglg, dglb = _gmlp_bwd(sv['h_c'], dc_gm, gmlp_ln_g[l], gmlp_ln_b[l], sv['w_tril'],
                                                   w_tril_t, sv['bs_rows'], name="gmlp_bwd", ts=ts)
        grads['gmlp_w_s'][l] = jnp.tril(dws)
        grads['gmlp_b_s'][l] = dbs_acc[:, :GMLP_GROUPS].T
        grads['gmlp_ln_g'][l], grads['gmlp_ln_b'][l] = dglg[0], dglb[0]
        dw_in = _mm_tn_shared(sv['x0'], [dh_a, *dqkv, dh_c], tk=tk, name="dw_in")
        grads['w_in'][l] = jnp.concatenate([w for w, _ in dw_in], axis=1)
        grads['b_in'][l] = jnp.concatenate([c for _, c in dw_in], axis=1)[0]
        WinT = Win.T
        edges = [0, 2 * CONV_CH] + [2 * CONV_CH + k * ATTN_CH for k in (1, 2, 3)] + [WinT.shape[0]]
        dx = _mm([dh_a, *dqkv, dh_c], [WinT[a:b] for a, b in zip(edges[:-1], edges[1:])], resid=dz1,
                 resid_scale=ALPHA, tm=ts, name="dx0")

    gfull = {n: jnp.stack(v) for n, v in grads.items()}
    gfull['rel_bias_table'] = drel

    sends = []
    for n in SHARDED:
        parts = _split_axis(gfull[n], SHARD_AXIS[n])
        sends.append(parts.reshape(N_DEV, -1, parts.shape[-1]).astype(BF16))
    recvs = _exchange(sends, name="grad_exchange")
    shard_out = [[], [], [], []]
    for n, recv in zip(SHARDED, recvs):
        shp = W[n].shape
        rows = recv.shape[1]
        tr = rows // 4 if rows % 64 == 0 else rows
        outs = _adamw(*[src[n].reshape(rows, shp[-1]) for src in (W, Mo, Vo)], recv, name=f"adamw_{n}", tr=tr)
        for kind in range(4):
            shard_out[kind].append(outs[kind].reshape(shp))

    gsmall, soffs = _pack([gfull[n] for n in SMALL], LANES, 8)
    gall = _allgather([gsmall], name="small_grad_allgather")[0]
    spacks = [_pack([src[n] for n in SMALL], LANES, 8)[0] for src in (W, Mo, Vo)]
    souts = _adamw(spacks[0], spacks[1], spacks[2], gall, name="adamw_small", tr=gsmall.shape[0])
    small_out = [_unpack(o, soffs, [W[n].shape for n in SMALL]) for o in souts]

    loss = lax.psum(loss_part[0, 0], ("x", "y", "c"))
    by_kind = []
    for kind in range(4):
        d = dict(zip(SHARDED, shard_out[kind]))
        d.update(zip(SMALL, small_out[kind]))
        by_kind.append([d[n] for n in WEIGHTS])
    return (loss, dx[None], *by_kind[0], *by_kind[1], *by_kind[2], *by_kind[3])
```

```python
import math

import numpy as np
import jax
import jax.numpy as jnp
from jax import lax
from jax.experimental import pallas as pl
from jax.experimental.pallas import tpu as pltpu

F32 = jnp.float32
BF16 = jnp.bfloat16

DEPTH = 2
HEAD_DIM = 64
CONV_CH = 256
CONV_WIDTH = 31
ATTN_HEADS = 8
ATTN_CH = ATTN_HEADS * HEAD_DIM
PATTERNS = ((128, 1), (512, 4), (2048, 16))
ATTN_BLOCK = 128
N_BUCKETS = 32
MAX_DISTANCE = 2048
GMLP_CH = 256
GMLP_GROUPS = 4
GMLP_GROUP_DIM = GMLP_CH // GMLP_GROUPS
CHUNK = 128
FFN_CONV_WIDTH = 3
LN_EPS = 1e-5
ALPHA = (2.0 * DEPTH) ** 0.25
ADAM_LR = 0.001
ADAM_B1 = 0.9
ADAM_B2 = 0.999
ADAM_EPS = 1e-08
ADAM_WD = 0.01
ADAM_STEP = 10
NEG = -1e30
N_DEV = 8
LANES = 128
CONV_HALO = 32
FFN_HALO = 8
FFN_ROWS = 16
FFN_STRIP = 256
MESH = pl.DeviceIdType.MESH

WEIGHTS = ['w_in', 'b_in', 'conv_dw_w', 'conv_dw_b', 'conv_ln_g', 'conv_ln_b', 'rel_bias_table', 'gmlp_ln_g',
           'gmlp_ln_b', 'gmlp_w_s', 'gmlp_b_s', 'w_out', 'b_out', 'ln1_g', 'ln1_b', 'ffn_w_up', 'ffn_b_up',
           'ffn_conv_w', 'ffn_conv_b', 'ffn_w_down', 'ffn_b_down', 'ln2_g', 'ln2_b']
SHARDED = ['w_in', 'w_out', 'ffn_w_up', 'ffn_w_down', 'conv_dw_w', 'ffn_conv_w']
SHARD_AXIS = {'w_in': 2, 'w_out': 1, 'ffn_w_up': 2, 'ffn_w_down': 1, 'conv_dw_w': 2, 'ffn_conv_w': 2}
SMALL = [n for n in WEIGHTS if n not in SHARDED]


def _call(body, *, grid=(), vmem_mb=48, **kw):
    params = pltpu.CompilerParams(dimension_semantics=("arbitrary",) * len(grid), vmem_limit_bytes=vmem_mb << 20)
    return pl.pallas_call(body, grid=grid, compiler_params=params, **kw)


def _sds(shape, dtype):
    return jax.ShapeDtypeStruct(shape, dtype)


def _ln_rows(z):
    mu = jnp.mean(z, axis=-1, keepdims=True)
    zc = z - mu
    var = jnp.mean(zc * zc, axis=-1, keepdims=True)
    rstd = lax.rsqrt(var + LN_EPS)
    return zc * rstd, rstd


def _ln_bwd_rows(dxhat, xhat, rstd):
    m1 = jnp.mean(dxhat, axis=-1, keepdims=True)
    m2 = jnp.mean(dxhat * xhat, axis=-1, keepdims=True)
    return rstd * (dxhat - m1 - xhat * m2)


def _colsum(v):
    return jnp.sum(v, axis=0, keepdims=True)


def _mm(a_list, w_list, *, name, tm, tn=None, bias=None, resid=None, resid_scale=1.0, ln=None, splits=None,
        out_dtype=F32):
    na = len(a_list)
    M = a_list[0].shape[0]
    N = w_list[0].shape[1]
    tn = N if tn is None else tn
    assert M % tm == 0 and N % tn == 0
    assert ln is None or tn == N
    assert splits is None or tn == N

    def body(*refs):
        a_refs, w_refs = refs[:na], refs[na:2 * na]
        pos = 2 * na
        acc = None
        for a_ref, w_ref in zip(a_refs, w_refs):
            t = jnp.dot(a_ref[...].astype(BF16), w_ref[...], preferred_element_type=F32)
            acc = t if acc is None else acc + t
        if bias is not None:
            acc = acc + refs[pos][...]
            pos += 1
        if resid is not None:
            acc = resid_scale * refs[pos][...] + acc
            pos += 1
        if ln is not None:
            g_ref, b_ref = refs[pos], refs[pos + 1]
            y_ref, xhat_ref, rstd_ref = refs[pos + 2], refs[pos + 3], refs[pos + 4]
            xhat, rstd = _ln_rows(acc)
            y_ref[...] = xhat * g_ref[...] + b_ref[...]
            xhat_ref[...] = xhat
            rstd_ref[...] = rstd
        elif splits is not None:
            c0 = 0
            for o_ref, (width, dtype, scale) in zip(refs[pos:], splits):
                part = acc[:, c0:c0 + width]
                if scale != 1.0:
                    part = part * scale
                o_ref[...] = part.astype(dtype)
                c0 += width
        else:
            refs[pos][...] = acc.astype(out_dtype)

    in_specs = [pl.BlockSpec((tm, a.shape[1]), lambda j, i: (i, 0)) for a in a_list]
    in_specs += [pl.BlockSpec((w.shape[0], tn), lambda j, i: (0, j)) for w in w_list]
    args = list(a_list) + list(w_list)
    if bias is not None:
        in_specs.append(pl.BlockSpec((1, tn), lambda j, i: (0, j)))
        args.append(bias.reshape(1, N))
    if resid is not None:
        in_specs.append(pl.BlockSpec((tm, tn), lambda j, i: (i, j)))
        args.append(resid)
    if ln is not None:
        in_specs += [pl.BlockSpec((1, N), lambda j, i: (0, 0))] * 2
        args += [ln[0].reshape(1, N), ln[1].reshape(1, N)]
        out_shape = (_sds((M, N), F32), _sds((M, N), F32), _sds((M, 1), F32))
        out_specs = (pl.BlockSpec((tm, N), lambda j, i: (i, 0)), pl.BlockSpec((tm, N), lambda j, i: (i, 0)),
                     pl.BlockSpec((tm, 1), lambda j, i: (i, 0)))
    elif splits is not None:
        out_shape = tuple(_sds((M, w), d) for (w, d, _) in splits)
        out_specs = tuple(pl.BlockSpec((tm, w), lambda j, i: (i, 0)) for (w, _, _) in splits)
    else:
        out_shape = _sds((M, N), out_dtype)
        out_specs = pl.BlockSpec((tm, tn), lambda j, i: (i, j))
    return _call(body, grid=(N // tn, M // tm), in_specs=in_specs, out_specs=out_specs, out_shape=out_shape,
                 name=name, vmem_mb=56)(*args)


def _mm_tn(a, dy, *, name, tm, tn, tk, colsum=False):
    S, Ka = a.shape
    N = dy.shape[1]
    assert S % tk == 0 and Ka % tm == 0 and N % tn == 0

    def body(a_ref, dy_ref, out_ref, *cs):
        i, k = pl.program_id(1), pl.program_id(2)
        dyb = dy_ref[...]
        part = lax.dot_general(a_ref[...].astype(BF16), dyb.astype(BF16), (((0,), (0,)), ((), ())),
                               preferred_element_type=F32)

        @pl.when(k == 0)
        def _():
            out_ref[...] = part

        @pl.when(k > 0)
        def _():
            out_ref[...] += part

        if colsum:
            cs_ref = cs[0]
            s = _colsum(dyb.astype(F32))

            @pl.when((i == 0) & (k == 0))
            def _():
                cs_ref[...] = s

            @pl.when((i == 0) & (k > 0))
            def _():
                cs_ref[...] += s

    out_shape = [_sds((Ka, N), F32)]
    out_specs = [pl.BlockSpec((tm, tn), lambda j, i, k: (i, j))]
    if colsum:
        out_shape.append(_sds((1, N), F32))
        out_specs.append(pl.BlockSpec((1, tn), lambda j, i, k: (0, j)))
    res = _call(body, grid=(N // tn, Ka // tm, S // tk),
                in_specs=[pl.BlockSpec((tk, tm), lambda j, i, k: (k, i)), pl.BlockSpec((tk, tn), lambda j, i, k: (k, j))],
                out_specs=tuple(out_specs), out_shape=tuple(out_shape), name=name, vmem_mb=56)(a, dy)
    return res if colsum else res[0]


def _mm_tn_shared(a, dys, *, name, tk):
    S, Ka = a.shape
    n = len(dys)
    assert S % tk == 0

    def body(*refs):
        a_ref, dy_refs, outs = refs[0], refs[1:1 + n], refs[1 + n:]
        k = pl.program_id(0)
        a_bf = a_ref[...].astype(BF16)
        for i, dy_ref in enumerate(dy_refs):
            dyb = dy_ref[...]
            part = lax.dot_general(a_bf, dyb.astype(BF16), (((0,), (0,)), ((), ())), preferred_element_type=F32)
            s = _colsum(dyb.astype(F32))
            w_ref, c_ref = outs[2 * i], outs[2 * i + 1]

            @pl.when(k == 0)
            def _():
                w_ref[...] = part
                c_ref[...] = s

            @pl.when(k > 0)
            def _():
                w_ref[...] += part
                c_ref[...] += s

    in_specs = [pl.BlockSpec((tk, Ka), lambda k: (k, 0))]
    in_specs += [pl.BlockSpec((tk, dy.shape[1]), lambda k: (k, 0)) for dy in dys]
    out_specs, out_shape = [], []
    for dy in dys:
        N = dy.shape[1]
        out_specs += [pl.BlockSpec((Ka, N), lambda k: (0, 0)), pl.BlockSpec((1, N), lambda k: (0, 0))]
        out_shape += [_sds((Ka, N), F32), _sds((1, N), F32)]
    res = _call(body, grid=(S // tk,), in_specs=in_specs, out_specs=tuple(out_specs), out_shape=tuple(out_shape),
                name=name, vmem_mb=56)(a, *dys)
    return [(res[2 * i], res[2 * i + 1]) for i in range(n)]


def _ln_bwd(xhat, rstd, g, *, name, ts, dy=None, b=None, target=None):
    S, D = xhat.shape
    from_loss = target is not None

    def body(*refs):
        if from_loss:
            xhat_ref, rstd_ref, g_ref, b_ref, t_ref, dz_ref, dg_ref, db_ref, dzs_ref, loss_ref = refs
        else:
            xhat_ref, rstd_ref, g_ref, dy_ref, dz_ref, dg_ref, db_ref, dzs_ref = refs
        i = pl.program_id(0)
        xh = xhat_ref[...]
        gg = g_ref[...]
        if from_loss:
            err = xh * gg + b_ref[...] - t_ref[...]
            dyv = err * (1.0 / D)
            lsum = (0.5 / D) * jnp.sum(err * err, axis=(0, 1), keepdims=True)
        else:
            dyv = dy_ref[...]
        dz = _ln_bwd_rows(dyv * gg, xh, rstd_ref[...])
        dz_ref[...] = dz
        parts = [(dg_ref, _colsum(dyv * xh)), (db_ref, _colsum(dyv)), (dzs_ref, _colsum(dz))]
        if from_loss:
            parts.append((loss_ref, lsum))

        @pl.when(i == 0)
        def _():
            for r, v in parts:
                r[...] = v

        @pl.when(i > 0)
        def _():
            for r, v in parts:
                r[...] += v

    row = pl.BlockSpec((ts, D), lambda i: (i, 0))
    vec = pl.BlockSpec((1, D), lambda i: (0, 0))
    in_specs = [row, pl.BlockSpec((ts, 1), lambda i: (i, 0)), vec]
    args = [xhat, rstd, g.reshape(1, D)]
    if from_loss:
        in_specs += [vec, row]
        args += [b.reshape(1, D), target]
    else:
        in_specs += [row]
        args += [dy]
    out_shape = [_sds((S, D), F32), _sds((1, D), F32), _sds((1, D), F32), _sds((1, D), F32)]
    out_specs = [row, vec, vec, vec]
    if from_loss:
        out_shape.append(_sds((1, 1), F32))
        out_specs.append(pl.BlockSpec((1, 1), lambda i: (0, 0)))
    return _call(body, grid=(S // ts,), in_specs=in_specs, out_specs=tuple(out_specs), out_shape=tuple(out_shape),
                 name=name)(*args)


def _glu(v):
    return v[:, :CONV_CH] * jax.nn.sigmoid(v[:, CONV_CH:])


def _conv_fwd(h_a, dw_w, dw_b, ln_g, ln_b, *, name, ts):
    S = h_a.shape[0]
    C, K, HB = CONV_CH, CONV_WIDTH, CONV_HALO
    RC = 128

    def body(h_ref, halo_ref, w_ref, b_ref, g_ref, bb_ref, out_ref, hc_ref, gbuf):
        i = pl.program_id(0)
        gbuf[0:HB, :] = jnp.where(i > 0, _glu(halo_ref[...]), 0.0)
        gbuf[HB:HB + ts, :] = _glu(h_ref[...])
        for r0 in range(0, ts, RC):
            acc = jnp.zeros((RC, C), F32) + b_ref[...]
            for k in range(K):
                acc = acc + w_ref[k:k + 1, :] * gbuf[pl.ds(r0 + HB - (K - 1) + k, RC), :]
            hc_ref[r0:r0 + RC, :] = acc
            xhat, _ = _ln_rows(acc)
            hn = xhat * g_ref[...] + bb_ref[...]
            out_ref[r0:r0 + RC, :] = (hn * jax.nn.sigmoid(hn)).astype(BF16)

    nb = ts // HB
    vec = pl.BlockSpec((1, C), lambda i: (0, 0))
    return _call(body, grid=(S // ts,),
                 in_specs=[pl.BlockSpec((ts, 2 * C), lambda i: (i, 0)),
                           pl.BlockSpec((HB, 2 * C), lambda i: (jnp.maximum(i * nb - 1, 0), 0)),
                           pl.BlockSpec((K, C), lambda i: (0, 0)), vec, vec, vec],
                 out_specs=(pl.BlockSpec((ts, C), lambda i: (i, 0)), pl.BlockSpec((ts, C), lambda i: (i, 0))),
                 out_shape=(_sds((S, C), BF16), _sds((S, C), F32)),
                 scratch_shapes=[pltpu.VMEM((HB + ts, C), F32)], name=name)(
        h_a, h_a, dw_w, dw_b.reshape(1, C), ln_g.reshape(1, C), ln_b.reshape(1, C))


def _conv_bwd(h_a, hc, dout, dw_w, ln_g, ln_b, *, name, ts):
    S = h_a.shape[0]
    C, K, HB = CONV_CH, CONV_WIDTH, CONV_HALO
    RC = 128
    n = S // ts

    def dconv_out(hc_v, do_v, g_ref, bb_ref):
        xhat, rstd = _ln_rows(hc_v)
        hn = xhat * g_ref[...] + bb_ref[...]
        sg = jax.nn.sigmoid(hn)
        dhn = do_v * (sg * (1.0 + hn * (1.0 - sg)))
        return _ln_bwd_rows(dhn * g_ref[...], xhat, rstd), dhn, xhat

    def body(h_ref, hprev_ref, hc_ref, hcnext_ref, do_ref, donext_ref, w_ref, g_ref, bb_ref,
             dh_ref, dw_ref, dwb_ref, dg_ref, db_ref, gbuf, dbuf):
        i = pl.program_id(0)
        hv = h_ref[...]
        gbuf[0:HB, :] = jnp.where(i > 0, _glu(hprev_ref[...]), 0.0)
        gbuf[HB:HB + ts, :] = _glu(hv)
        dhc, dhn, xhat = dconv_out(hc_ref[...], do_ref[...], g_ref, bb_ref)
        dhc_next, _, _ = dconv_out(hcnext_ref[...], donext_ref[...], g_ref, bb_ref)
        dbuf[0:ts, :] = dhc
        dbuf[ts:ts + HB, :] = jnp.where(i < n - 1, dhc_next, 0.0)
        dw_rows = []
        for k in range(K):
            acc_k = jnp.zeros((1, C), F32)
            for r0 in range(0, ts, RC):
                acc_k = acc_k + _colsum(dbuf[r0:r0 + RC, :] * gbuf[pl.ds(r0 + HB - (K - 1) + k, RC), :])
            dw_rows.append(acc_k)
        dw_rows.append(jnp.zeros((1, C), F32))
        dw_tile = jnp.concatenate(dw_rows, axis=0)
        for r0 in range(0, ts, RC):
            acc = jnp.zeros((RC, C), F32)
            for k in range(K):
                acc = acc + w_ref[k:k + 1, :] * dbuf[pl.ds(r0 + (K - 1) - k, RC), :]
            a = hv[r0:r0 + RC, :C]
            sg = jax.nn.sigmoid(hv[r0:r0 + RC, C:])
            dh_ref[r0:r0 + RC, :C] = (acc * sg).astype(BF16)
            dh_ref[r0:r0 + RC, C:] = (acc * a * sg * (1.0 - sg)).astype(BF16)
        parts = [(dw_ref, dw_tile), (dwb_ref, _colsum(dhc)), (dg_ref, _colsum(dhn * xhat)), (db_ref, _colsum(dhn))]

        @pl.when(i == 0)
        def _():
            for r, v in parts:
                r[...] = v

        @pl.when(i > 0)
        def _():
            for r, v in parts:
                r[...] += v

    nb = ts // HB
    last = S // HB - 1
    vec = pl.BlockSpec((1, C), lambda i: (0, 0))
    nxt = lambda i: (jnp.minimum((i + 1) * nb, last), 0)
    return _call(body, grid=(n,),
                 in_specs=[pl.BlockSpec((ts, 2 * C), lambda i: (i, 0)),
                           pl.BlockSpec((HB, 2 * C), lambda i: (jnp.maximum(i * nb - 1, 0), 0)),
                           pl.BlockSpec((ts, C), lambda i: (i, 0)), pl.BlockSpec((HB, C), nxt),
                           pl.BlockSpec((ts, C), lambda i: (i, 0)), pl.BlockSpec((HB, C), nxt),
                           pl.BlockSpec((K, C), lambda i: (0, 0)), vec, vec],
                 out_specs=(pl.BlockSpec((ts, 2 * C), lambda i: (i, 0)), pl.BlockSpec((K + 1, C), lambda i: (0, 0)),
                            vec, vec, vec),
                 out_shape=(_sds((S, 2 * C), BF16), _sds((K + 1, C), F32), _sds((1, C), F32), _sds((1, C), F32),
                            _sds((1, C), F32)),
                 scratch_shapes=[pltpu.VMEM((HB + ts, C), F32), pltpu.VMEM((ts + HB, C), F32)], name=name)(
        h_a, h_a, hc, hc, dout, dout, dw_w, ln_g.reshape(1, C), ln_b.reshape(1, C))


def _gmlp_mix(vn_bf, w_ref, mix_buf, ts):
    for ch in range(ts // CHUNK):
        for g in range(GMLP_GROUPS):
            vg = vn_bf[ch * CHUNK:(ch + 1) * CHUNK, g * GMLP_GROUP_DIM:(g + 1) * GMLP_GROUP_DIM]
            mix_buf[ch * CHUNK:(ch + 1) * CHUNK, g * GMLP_GROUP_DIM:(g + 1) * GMLP_GROUP_DIM] = jnp.dot(
                w_ref[g], vg, preferred_element_type=F32)


def _gmlp_fwd(h_c, ln_g, ln_b, w_tril, bs_rows, *, name, ts):
    S = h_c.shape[0]
    C = GMLP_CH

    def body(h_ref, g_ref, b_ref, w_ref, bs_ref, out_ref, mix_buf):
        hv = h_ref[...]
        xhat, _ = _ln_rows(hv[:, C:])
        vn = (xhat * g_ref[...] + b_ref[...]).astype(BF16)
        _gmlp_mix(vn, w_ref, mix_buf, ts)
        for ch in range(ts // CHUNK):
            rows = slice(ch * CHUNK, (ch + 1) * CHUNK)
            out_ref[rows, :] = (hv[rows, :C] * (mix_buf[rows, :] + bs_ref[...])).astype(BF16)

    vec = pl.BlockSpec((1, C), lambda i: (0, 0))
    return _call(body, grid=(S // ts,),
                 in_specs=[pl.BlockSpec((ts, 2 * C), lambda i: (i, 0)), vec, vec,
                           pl.BlockSpec((GMLP_GROUPS, CHUNK, CHUNK), lambda i: (0, 0, 0)),
                           pl.BlockSpec((CHUNK, C), lambda i: (0, 0))],
                 out_specs=pl.BlockSpec((ts, C), lambda i: (i, 0)), out_shape=_sds((S, C), BF16),
                 scratch_shapes=[pltpu.VMEM((ts, C), F32)], name=name)(
        h_c, ln_g.reshape(1, C), ln_b.reshape(1, C), w_tril, bs_rows)


def _gmlp_bwd(h_c, dout, ln_g, ln_b, w_tril, w_tril_t, bs_rows, *, name, ts):
    S = h_c.shape[0]
    C, G, GD = GMLP_CH, GMLP_GROUPS, GMLP_GROUP_DIM

    def body(h_ref, do_ref, g_ref, b_ref, w_ref, wt_ref, bs_ref, dh_ref, dw_ref, dbs_ref, dg_ref, db_ref,
             mix_buf, dvn_buf):
        i = pl.program_id(0)
        hv = h_ref[...]
        u = hv[:, :C]
        xhat, rstd = _ln_rows(hv[:, C:])
        vn = (xhat * g_ref[...] + b_ref[...]).astype(BF16)
        _gmlp_mix(vn, w_ref, mix_buf, ts)
        do = do_ref[...]
        dmixed = do * u
        dm_bf = dmixed.astype(BF16)
        lane = lax.broadcasted_iota(jnp.int32, (CHUNK, LANES), 1)
        dbs = jnp.zeros((CHUNK, LANES), F32)
        dws = [jnp.zeros((CHUNK, CHUNK), F32) for _ in range(G)]
        for ch in range(ts // CHUNK):
            rows = slice(ch * CHUNK, (ch + 1) * CHUNK)
            dh_ref[rows, :C] = (do[rows, :] * (mix_buf[rows, :] + bs_ref[...])).astype(BF16)
            for g in range(G):
                cols = slice(g * GD, (g + 1) * GD)
                dmg = dm_bf[rows, cols]
                dvn_buf[rows, cols] = jnp.dot(wt_ref[g], dmg, preferred_element_type=F32)
                dws[g] = dws[g] + lax.dot_general(dmg, vn[rows, cols], (((1,), (1,)), ((), ())),
                                                  preferred_element_type=F32)
                rs = jnp.sum(dmixed[rows, cols], axis=1, keepdims=True)
                dbs = dbs + jnp.where(lane == g, rs, 0.0)
        dvn = dvn_buf[...]
        dh_ref[:, C:] = _ln_bwd_rows(dvn * g_ref[...], xhat, rstd).astype(BF16)
        dgv, dbv = _colsum(dvn * xhat), _colsum(dvn)

        @pl.when(i == 0)
        def _():
            for g in range(G):
                dw_ref[g] = dws[g]
            dbs_ref[...] = dbs
            dg_ref[...] = dgv
            db_ref[...] = dbv

        @pl.when(i > 0)
        def _():
            for g in range(G):
                dw_ref[g] += dws[g]
            dbs_ref[...] += dbs
            dg_ref[...] += dgv
            db_ref[...] += dbv

    vec = pl.BlockSpec((1, C), lambda i: (0, 0))
    wspec = pl.BlockSpec((G, CHUNK, CHUNK), lambda i: (0, 0, 0))
    return _call(body, grid=(S // ts,),
                 in_specs=[pl.BlockSpec((ts, 2 * C), lambda i: (i, 0)),
                           pl.BlockSpec((ts, C), lambda i: (i, 0)), vec, vec, wspec, wspec,
                           pl.BlockSpec((CHUNK, C), lambda i: (0, 0))],
                 out_specs=(pl.BlockSpec((ts, 2 * C), lambda i: (i, 0)), wspec,
                            pl.BlockSpec((CHUNK, LANES), lambda i: (0, 0)), vec, vec),
                 out_shape=(_sds((S, 2 * C), BF16), _sds((G, CHUNK, CHUNK), F32), _sds((CHUNK, LANES), F32),
                            _sds((1, C), F32), _sds((1, C), F32)),
                 scratch_shapes=[pltpu.VMEM((ts, C), F32), pltpu.VMEM((ts, C), F32)], name=name)(
        h_c, dout, ln_g.reshape(1, C), ln_b.reshape(1, C), w_tril, w_tril_t, bs_rows)


def _ffn_act_fwd(hu, cw, cb, *, name, ts, tc):
    S, F2 = hu.shape
    F = F2 // 2
    nj = F // tc
    HB = FFN_HALO
    nb = ts // HB

    def body(g_ref, v_ref, gh_ref, vh_ref, wg_ref, wv_ref, bg_ref, bv_ref, act_ref, dg_ref, dv_ref):
        i = pl.program_id(0)
        row = lax.broadcasted_iota(jnp.int32, (HB, tc), 0)

        def conv_chunk(x_ref, w_ref, b_ref, c, carry):
            cur = x_ref[c * HB:(c + 1) * HB, :]
            r1, r2 = pltpu.roll(cur, 1, 0), pltpu.roll(cur, 2, 0)
            x1 = jnp.where(row < 1, carry[0], r1)
            x2 = jnp.where(row < 2, carry[1], r2)
            out = (w_ref[0:1, :] * x2 + w_ref[1:2, :] * x1 + w_ref[2:3, :] * cur) + b_ref[...]
            return out, (r1, r2)

        def first_carry(h_ref):
            prev = jnp.where(i > 0, h_ref[...], 0.0)
            return pltpu.roll(prev, 1, 0), pltpu.roll(prev, 2, 0)

        cg, cv = first_carry(gh_ref), first_carry(vh_ref)
        for r0 in range(0, ts, FFN_ROWS):
            acts, dgs, dvs = [], [], []
            for c in range(r0 // HB, (r0 + FFN_ROWS) // HB):
                gc, cg = conv_chunk(g_ref, wg_ref, bg_ref, c, cg)
                vc, cv = conv_chunk(v_ref, wv_ref, bv_ref, c, cv)
                sg = jax.nn.sigmoid(gc)
                silu = gc * sg
                acts.append(silu * vc)
                dgs.append(vc * (sg * (1.0 + gc * (1.0 - sg))))
                dvs.append(silu)
            rows = slice(r0, r0 + FFN_ROWS)
            act_ref[rows, :] = jnp.concatenate(acts, axis=0).astype(BF16)
            dg_ref[rows, :] = jnp.concatenate(dgs, axis=0).astype(BF16)
            dv_ref[rows, :] = jnp.concatenate(dvs, axis=0).astype(BF16)

    prev = lambda off: (lambda i, j: (jnp.maximum(i * nb - 1, 0), j + off))
    out = pl.BlockSpec((ts, tc), lambda i, j: (i, j))
    return _call(body, grid=(S // ts, nj),
                 in_specs=[pl.BlockSpec((ts, tc), lambda i, j: (i, j)), pl.BlockSpec((ts, tc), lambda i, j: (i, j + nj)),
                           pl.BlockSpec((HB, tc), prev(0)), pl.BlockSpec((HB, tc), prev(nj)),
                           pl.BlockSpec((3, tc), lambda i, j: (0, j)), pl.BlockSpec((3, tc), lambda i, j: (0, j + nj)),
                           pl.BlockSpec((1, tc), lambda i, j: (0, j)), pl.BlockSpec((1, tc), lambda i, j: (0, j + nj))],
                 out_specs=(out, out, out), out_shape=(_sds((S, F), BF16),) * 3, name=name)(
        hu, hu, hu, hu, cw, cw, cb.reshape(1, F2), cb.reshape(1, F2))


def _ffn_act_bwd(hu, dact, dact_dg, dact_dv, cw, *, name, ts, tc):
    S, F2 = hu.shape
    F = F2 // 2
    nj = F // tc
    HB = FFN_HALO
    HB16 = 16
    n = S // ts

    def body(g_ref, v_ref, lg_ref, lv_ref, lgn_ref, lvn_ref, da_ref, dan_ref, wg_ref, wv_ref,
             dhg_ref, dhv_ref, dwg_ref, dwv_ref, dbg_ref, dbv_ref, dug_ref, duv_ref, accg, accv):
        i = pl.program_id(1)
        RC = FFN_ROWS
        npairs = ts // RC
        last_tile = i == n - 1

        @pl.when(i == 0)
        def _():
            accg[...] = jnp.zeros_like(accg)
            accv[...] = jnp.zeros_like(accv)

        for l_ref, ln_ref, h_ref, w_ref, dh_ref, acc in ((lg_ref, lgn_ref, g_ref, wg_ref, dhg_ref, accg),
                                                          (lv_ref, lvn_ref, v_ref, wv_ref, dhv_ref, accv)):
            for c0 in range(0, tc, FFN_STRIP):
                cols = slice(c0, min(c0 + FFN_STRIP, tc))
                sw = cols.stop - cols.start
                row = lax.broadcasted_iota(jnp.int32, (HB, sw), 0)
                w0, w1, w2 = w_ref[0:1, cols], w_ref[1:2, cols], w_ref[2:3, cols]

                def d_rows(p):
                    rows = slice(p * RC, (p + 1) * RC)
                    return da_ref[rows, cols] * l_ref[rows, cols].astype(F32)

                after = jnp.where(last_tile, 0.0, dan_ref[:, cols]) * ln_ref[:, cols].astype(F32)[0:HB, :]
                pair = d_rows(0)
                cur = pair[0:HB, :]
                c7, c6 = pltpu.roll(cur, HB - 1, 0), pltpu.roll(cur, HB - 2, 0)
                sums = [None] * 5
                for p in range(npairs):
                    nxt_pair = d_rows(p + 1) if p + 1 < npairs else None
                    dhus = []
                    for half, nxt_c in enumerate((pair[HB:RC, :], after if nxt_pair is None else nxt_pair[0:HB, :])):
                        c = 2 * p + half
                        n7, n6 = pltpu.roll(nxt_c, HB - 1, 0), pltpu.roll(nxt_c, HB - 2, 0)
                        taps = [jnp.where(row >= HB - 2, n6, c6), jnp.where(row >= HB - 1, n7, c7), cur]
                        dhu = w2 * taps[2] + w1 * taps[1] + w0 * taps[0]
                        dhus.append(dhu)
                        h = h_ref[c * HB:(c + 1) * HB, cols]
                        parts = [h * taps[0], h * taps[1], h * taps[2], taps[2], dhu]
                        sums = [q if s is None else s + q for s, q in zip(sums, parts)]
                        cur, c7, c6 = nxt_c, n7, n6
                    dh_ref[p * RC:(p + 1) * RC, cols] = jnp.concatenate(dhus, axis=0).astype(BF16)
                    pair = nxt_pair
                for k in range(5):
                    acc[8 * k:8 * k + 8, cols] += sums[k]

        @pl.when(i == n - 1)
        def _():
            for acc, dw_ref, db_ref, du_ref in ((accg, dwg_ref, dbg_ref, dug_ref), (accv, dwv_ref, dbv_ref, duv_ref)):
                for k in range(3):
                    dw_ref[k:k + 1, :] = _colsum(acc[8 * k:8 * k + 8, :])
                db_ref[...] = _colsum(acc[24:32, :])
                du_ref[...] = _colsum(acc[32:40, :])

    nxt = lambda hb: (lambda j, i: (jnp.minimum((i + 1) * (ts // hb), S // hb - 1), j))
    tile = lambda off: (lambda j, i: (i, j + off))
    vec = lambda rows: pl.BlockSpec((rows, tc), lambda j, i: (0, j))
    return _call(body, grid=(nj, n),
                 in_specs=[pl.BlockSpec((ts, tc), tile(0)), pl.BlockSpec((ts, tc), tile(nj)),
                           pl.BlockSpec((ts, tc), tile(0)), pl.BlockSpec((ts, tc), tile(0)),
                           pl.BlockSpec((HB16, tc), nxt(HB16)), pl.BlockSpec((HB16, tc), nxt(HB16)),
                           pl.BlockSpec((ts, tc), tile(0)), pl.BlockSpec((HB, tc), nxt(HB)),
                           pl.BlockSpec((3, tc), lambda j, i: (0, j)), pl.BlockSpec((3, tc), lambda j, i: (0, j + nj))],
                 out_specs=(pl.BlockSpec((ts, tc), tile(0)), pl.BlockSpec((ts, tc), tile(0)),
                            vec(3), vec(3), vec(1), vec(1), vec(1), vec(1)),
                 out_shape=(_sds((S, F), BF16), _sds((S, F), BF16), _sds((3, F), F32), _sds((3, F), F32),
                            _sds((1, F), F32), _sds((1, F), F32), _sds((1, F), F32), _sds((1, F), F32)),
                 scratch_shapes=[pltpu.VMEM((40, tc), F32), pltpu.VMEM((40, tc), F32)], name=name)(
        hu, hu, dact_dg, dact_dv, dact_dg, dact_dv, dact, dact, cw, cw)


def _t5_bucket(dist):
    max_exact = N_BUCKETS // 2
    d = np.maximum(dist, 1).astype(np.float64)
    large = max_exact + (np.log(d / max_exact) / math.log(MAX_DISTANCE / max_exact)
                         * (N_BUCKETS - max_exact)).astype(np.int32)
    large = np.minimum(large, N_BUCKETS - 1)
    return np.where(dist < max_exact, dist, large).astype(np.int32)


def _pattern_tables(window, dilation):
    qi = np.arange(ATTN_BLOCK)[:, None]
    kj = np.arange(2 * ATTN_BLOCK)[None, :]
    dist = qi + ATTN_BLOCK - kj
    valid = (dist >= 0) & (dist <= window // dilation)
    bucket = _t5_bucket(np.clip(dist, 0, None) * dilation)
    return bucket, valid


def _dilate_qkv(qkv, *, name, ts):
    S, C = qkv.shape
    dils = [d for (_, d) in PATTERNS if d > 1]

    def body(x_ref, nat_ref, *rest):
        outs, tmp = rest[:-1], rest[-1]
        nat_ref[...] = x_ref[...].astype(BF16)
        for j in range(C // LANES):
            cols = slice(j * LANES, (j + 1) * LANES)
            tmp[...] = x_ref[:, cols]
            for d, o_ref in zip(dils, outs):
                _dilate(tmp, o_ref, cols, d, ts, BF16)

    out_shape = (_sds((S, C), BF16),) + tuple(_sds((d, S // d, C), BF16) for d in dils)
    out_specs = (_dil_spec(1, ts, C),) + tuple(_dil_spec(d, ts, C) for d in dils)
    res = _call(body, grid=(S // ts,), in_specs=[_dil_spec(1, ts, C)], out_specs=out_specs, out_shape=out_shape,
                scratch_shapes=[pltpu.VMEM((ts, LANES), F32)], name=name)(qkv)
    return [res[0]] + [r.reshape(S, C) for r in res[1:]]


def _attn_group(S):
    nb_min = (S // PATTERNS[-1][1]) // ATTN_BLOCK
    return math.gcd(8, nb_min)


def _attn_fwd(qkv, bias, *, name, nb, G):
    S = qkv.shape[0]
    B, HD = ATTN_BLOCK, HEAD_DIM
    GR = G * B
    ng = S // GR

    def body(q_ref, k_ref, kh_ref, v_ref, vh_ref, bias_ref, o_ref, lse_ref, kbuf, vbuf):
        g = pl.program_id(1)
        halo_ok = (g * G) % nb != 0
        kbuf[0:B, :] = kh_ref[...]
        kbuf[B:B + GR, :] = k_ref[...]
        vbuf[0:B, :] = vh_ref[...]
        vbuf[B:B + GR, :] = v_ref[...]
        col = lax.broadcasted_iota(jnp.int32, (B, 2 * B), 1)
        head0 = lax.broadcasted_iota(jnp.int32, (B, LANES), 1) < HD

        for bi in range(G):
            r0 = bi * B
            q2 = q_ref[r0:r0 + B, :]
            kk = kbuf[r0:r0 + 2 * B, :]
            vv = vbuf[r0:r0 + 2 * B, :]
            zero = jnp.zeros_like(q2)
            os_, ls_, lses = [], [], []
            for hh in range(2):
                qh = jnp.where(head0, q2, zero) if hh == 0 else jnp.where(head0, zero, q2)
                s = lax.dot_general(qh, kk, (((1,), (1,)), ((), ())), preferred_element_type=F32)
                s = s + bias_ref[hh]
                if bi == 0:
                    s = jnp.where(jnp.logical_and(jnp.logical_not(halo_ok), col < B), NEG, s)
                m = jnp.max(s, axis=1, keepdims=True)
                p = jnp.exp(s - m)
                l = jnp.sum(p, axis=1, keepdims=True)
                os_.append(jnp.dot(p.astype(BF16), vv, preferred_element_type=F32))
                ls_.append(l)
                lses.append(m + jnp.log(l))
            o_ref[r0:r0 + B, :] = (jnp.where(head0, os_[0], os_[1]) / jnp.where(head0, ls_[0], ls_[1])).astype(BF16)
            lse_ref[r0:r0 + B, :] = jnp.where(head0, lses[0], lses[1])

    halo = lambda off: (lambda hp, g: (jnp.maximum(g * G - 1, 0), off + hp))
    main = lambda off: (lambda hp, g: (g, off + hp))
    return _call(body, grid=(4, ng),
                 in_specs=[pl.BlockSpec((GR, LANES), main(0)), pl.BlockSpec((GR, LANES), main(4)),
                           pl.BlockSpec((B, LANES), halo(4)), pl.BlockSpec((GR, LANES), main(8)),
                           pl.BlockSpec((B, LANES), halo(8)), pl.BlockSpec((2, B, 2 * B), lambda hp, g: (hp, 0, 0))],
                 out_specs=(pl.BlockSpec((GR, LANES), main(0)), pl.BlockSpec((GR, LANES), main(0))),
                 out_shape=(_sds((S, ATTN_CH), BF16), _sds((S, ATTN_CH), F32)),
                 scratch_shapes=[pltpu.VMEM((B + GR, LANES), BF16), pltpu.VMEM((B + GR, LANES), BF16)], name=name)(
        qkv, qkv, qkv, qkv, qkv, bias)


def _dil_spec(d, ts, C):
    if d == 1:
        return pl.BlockSpec((ts, C), lambda i: (i, 0))
    return pl.BlockSpec((d, ts // d, C), lambda i: (0, i, 0))


def _dil_view(a, d):
    return a if d == 1 else a.reshape(d, a.shape[0] // d, a.shape[1])


def _dilate(nat_tmp, dst_ref, cols, d, ts, dtype=F32):
    for r in range(d):
        dst_ref[r, :, cols] = nat_tmp[pl.ds(r, ts // d, stride=d), :].astype(dtype)


def _undilate(src_ref, nat_tmp, cols, d, ts, accumulate=False):
    for r in range(d):
        rows = pl.ds(r, ts // d, stride=d)
        if accumulate:
            nat_tmp[rows, :] = nat_tmp[rows, :] + src_ref[r, :, cols].astype(F32)
        else:
            nat_tmp[rows, :] = src_ref[r, :, cols].astype(F32)


def _attn_merge(o_list, lse_list, dils, *, name, ts):
    S, C = o_list[0].shape
    P = len(o_list)
    nd = sum(1 for d in dils if d > 1)

    def body(*refs):
        o_refs, l_refs = refs[:P], refs[P:2 * P]
        out_ref, lse_ref = refs[2 * P], refs[2 * P + 1]
        lse_d_refs = refs[2 * P + 2:2 * P + 2 + nd]
        scratch = list(refs[2 * P + 2 + nd:])
        tmp = scratch.pop()
        for j in range(C // LANES):
            cols = slice(j * LANES, (j + 1) * LANES)
            os_, ls, free = [], [], list(scratch)
            for o_ref, l_ref, d in zip(o_refs, l_refs, dils):
                if d > 1:
                    so, sl = free.pop(0), free.pop(0)
                    _undilate(o_ref, so, cols, d, ts)
                    _undilate(l_ref, sl, cols, d, ts)
                    os_.append(so[...])
                    ls.append(sl[...])
                else:
                    os_.append(o_ref[:, cols].astype(F32))
                    ls.append(l_ref[:, cols])
            m = ls[0]
            for l in ls[1:]:
                m = jnp.maximum(m, l)
            ws = [jnp.exp(l - m) for l in ls]
            den = ws[0]
            for w in ws[1:]:
                den = den + w
            num = ws[0] * os_[0]
            for w, o in zip(ws[1:], os_[1:]):
                num = num + w * o
            out_ref[:, cols] = num / den
            tmp[...] = m + jnp.log(den)
            lse_ref[:, cols] = tmp[...]
            for l_out, d in zip(lse_d_refs, [d for d in dils if d > 1]):
                _dilate(tmp, l_out, cols, d, ts)

    row = _dil_spec(1, ts, C)
    dd = [d for d in dils if d > 1]
    res = _call(body, grid=(S // ts,), in_specs=[_dil_spec(d, ts, C) for d in dils] * 2,
                out_specs=(row, row) + tuple(_dil_spec(d, ts, C) for d in dd),
                out_shape=(_sds((S, C), F32), _sds((S, C), F32)) + tuple(_sds((d, S // d, C), F32) for d in dd),
                scratch_shapes=[pltpu.VMEM((ts, LANES), F32)] * (2 * nd + 1), name=name)(
        *[_dil_view(o, d) for o, d in zip(o_list, dils)], *[_dil_view(l, d) for l, d in zip(lse_list, dils)])
    lse_by_d = {1: res[1]}
    lse_by_d.update({d: r.reshape(S, C) for d, r in zip(dd, res[2:])})
    return res[0], [lse_by_d[d] for d in dils]


def _attn_prep(dout, out, dils, *, name, ts):
    S, C = out.shape
    HD = HEAD_DIM
    dd = [d for d in dils if d > 1]
    nd = len(dd)

    def body(do_ref, o_ref, d_ref, dob_ref, *rest):
        outs, tmp_d, tmp_o = rest[:2 * nd], rest[2 * nd], rest[2 * nd + 1]
        dob_ref[...] = do_ref[...].astype(BF16)
        for j in range(C // LANES):
            cols = slice(j * LANES, (j + 1) * LANES)
            do = do_ref[:, cols]
            prod = do * o_ref[:, cols]
            tmp_o[...] = do
            for h in range(LANES // HD):
                cs = slice(h * HD, (h + 1) * HD)
                tmp_d[:, cs] = jnp.broadcast_to(jnp.sum(prod[:, cs], axis=1, keepdims=True), (ts, HD))
            d_ref[:, cols] = tmp_d[...]
            for d, dd_out, do_out in zip(dd, outs[:nd], outs[nd:]):
                _dilate(tmp_d, dd_out, cols, d, ts)
                _dilate(tmp_o, do_out, cols, d, ts, BF16)

    row = _dil_spec(1, ts, C)
    res = _call(body, grid=(S // ts,), in_specs=[row, row],
                out_specs=(row, row) + tuple(_dil_spec(d, ts, C) for d in dd) * 2,
                out_shape=(_sds((S, C), F32), _sds((S, C), BF16)) + tuple(_sds((d, S // d, C), F32) for d in dd)
                + tuple(_sds((d, S // d, C), BF16) for d in dd),
                scratch_shapes=[pltpu.VMEM((ts, LANES), F32)] * 2, name=name)(dout, out)
    dd_by_d, do_by_d = {1: res[0]}, {1: res[1]}
    dd_by_d.update({d: r.reshape(S, C) for d, r in zip(dd, res[2:2 + nd])})
    do_by_d.update({d: r.reshape(S, C) for d, r in zip(dd, res[2 + nd:])})
    return [dd_by_d[d] for d in dils], [do_by_d[d] for d in dils]


def _attn_bwd(qkv, do, lse, dd, bias, *, name, nb, G):
    S = qkv.shape[0]
    B, HD = ATTN_BLOCK, HEAD_DIM
    GR = G * B
    ng = S // GR
    nblk = S // B

    def body(q_ref, k_ref, kh_ref, v_ref, vh_ref, do_ref, lse_ref, dd_ref, qn_ref, don_ref, lsen_ref, ddn_ref,
             bias_ref, dq_ref, dk_ref, dv_ref, dbias_ref, kbuf, vbuf, dkp, dvp, dko_s, dvo_s):
        g = pl.program_id(1)
        halo_ok = (g * G) % nb != 0
        next_ok = jnp.logical_and(((g + 1) * G) % nb != 0, g < ng - 1)
        kbuf[0:B, :] = kh_ref[...]
        kbuf[B:B + GR, :] = k_ref[...]
        vbuf[0:B, :] = vh_ref[...]
        vbuf[B:B + GR, :] = v_ref[...]
        col = lax.broadcasted_iota(jnp.int32, (B, 2 * B), 1)

        def tn_dot(a, b):
            return lax.dot_general(a, b, (((0,), (0,)), ((), ())), preferred_element_type=F32)

        @pl.when(g == 0)
        def _():
            dbias_ref[...] = jnp.zeros_like(dbias_ref)

        head0 = lax.broadcasted_iota(jnp.int32, (B, LANES), 1) < HD

        def nt_dot(a, b):
            return lax.dot_general(a, b, (((1,), (1,)), ((), ())), preferred_element_type=F32)

        def one_head(x, hh):
            zero = jnp.zeros_like(x)
            return jnp.where(head0, x, zero) if hh == 0 else jnp.where(head0, zero, x)

        def pick(per_head):
            return jnp.where(head0, per_head[0], per_head[1])

        for bi in range(G):
            r0 = bi * B
            q2 = q_ref[r0:r0 + B, :]
            do2 = do_ref[r0:r0 + B, :]
            kk = kbuf[r0:r0 + 2 * B, :]
            vv = vbuf[r0:r0 + 2 * B, :]
            dq, dko, dvo, dkq, dvq = [], [], [], [], []
            for hh in range(2):
                s = nt_dot(one_head(q2, hh), kk) + bias_ref[hh]
                if bi == 0:
                    s = jnp.where(jnp.logical_and(jnp.logical_not(halo_ok), col < B), NEG, s)
                p = jnp.exp(s - lse_ref[r0:r0 + B, hh * HD:hh * HD + 1])
                dp = nt_dot(one_head(do2, hh), vv)
                ds = p * (dp - dd_ref[r0:r0 + B, hh * HD:hh * HD + 1])
                dbias_ref[hh] += ds
                ds_bf, p_bf = ds.astype(BF16), p.astype(BF16)
                dq.append(jnp.dot(ds_bf, kk, preferred_element_type=F32))
                dko.append(tn_dot(ds_bf[:, B:], q2))
                dvo.append(tn_dot(p_bf[:, B:], do2))
                if bi > 0:
                    dkq.append(tn_dot(ds_bf[:, :B], q2))
                    dvq.append(tn_dot(p_bf[:, :B], do2))
            dq_ref[r0:r0 + B, :] = pick(dq).astype(BF16)
            dko_s[r0:r0 + B, :] = pick(dko)
            dvo_s[r0:r0 + B, :] = pick(dvo)
            if bi > 0:
                dkp[r0 - B:r0, :] = pick(dkq)
                dvp[r0 - B:r0, :] = pick(dvq)

        @pl.when(next_ok)
        def _():
            qn = qn_ref[...]
            don = don_ref[...]
            kl = kbuf[GR:GR + B, :]
            vl = vbuf[GR:GR + B, :]
            dkq, dvq = [], []
            for hh in range(2):
                s = nt_dot(one_head(qn, hh), kl) + bias_ref[hh, :, 0:B]
                p = jnp.exp(s - lsen_ref[:, hh * HD:hh * HD + 1])
                dp = nt_dot(one_head(don, hh), vl)
                ds = p * (dp - ddn_ref[:, hh * HD:hh * HD + 1])
                dkq.append(tn_dot(ds.astype(BF16), qn))
                dvq.append(tn_dot(p.astype(BF16), don))
            dkp[GR - B:GR, :] = pick(dkq)
            dvp[GR - B:GR, :] = pick(dvq)

        @pl.when(jnp.logical_not(next_ok))
        def _():
            dkp[GR - B:GR, :] = jnp.zeros((B, LANES), F32)
            dvp[GR - B:GR, :] = jnp.zeros((B, LANES), F32)

        dk_ref[...] = (dko_s[...] + dkp[...]).astype(BF16)
        dv_ref[...] = (dvo_s[...] + dvp[...]).astype(BF16)

    halo = lambda off: (lambda hp, g: (jnp.maximum(g * G - 1, 0), off + hp))
    main = lambda off: (lambda hp, g: (g, off + hp))
    nxt = lambda off: (lambda hp, g: (jnp.minimum((g + 1) * G, nblk - 1), off + hp))
    big, small = (lambda m: pl.BlockSpec((GR, LANES), m)), (lambda m: pl.BlockSpec((B, LANES), m))
    return _call(body, grid=(4, ng),
                 in_specs=[big(main(0)), big(main(4)), small(halo(4)), big(main(8)), small(halo(8)),
                           big(main(0)), big(main(0)), big(main(0)),
                           small(nxt(0)), small(nxt(0)), small(nxt(0)), small(nxt(0)),
                           pl.BlockSpec((2, B, 2 * B), lambda hp, g: (hp, 0, 0))],
                 out_specs=(big(main(0)), big(main(0)), big(main(0)),
                            pl.BlockSpec((2, B, 2 * B), lambda hp, g: (hp, 0, 0))),
                 out_shape=(_sds((S, ATTN_CH), BF16),) * 3 + (_sds((ATTN_HEADS, B, 2 * B), F32),),
                 scratch_shapes=[pltpu.VMEM((B + GR, LANES), BF16), pltpu.VMEM((B + GR, LANES), BF16)]
                 + [pltpu.VMEM((GR, LANES), F32)] * 4, name=name)(
        qkv, qkv, qkv, qkv, qkv, do, lse, dd, qkv, do, lse, dd, bias)


def _attn_combine(dq_list, dk_list, dv_list, dils, *, name, ts):
    S, C = dq_list[0].shape
    P = len(dq_list)
    scale = HEAD_DIM ** -0.5
    assert dils[0] == 1

    def body(*refs):
        out_refs, acc = refs[3 * P:3 * P + 3], refs[3 * P + 3]
        for part in range(3):
            for j in range(C // LANES):
                cols = slice(j * LANES, (j + 1) * LANES)
                acc[...] = refs[part * P][:, cols].astype(F32)
                for r, d in zip(refs[part * P + 1:(part + 1) * P], dils[1:]):
                    _undilate(r, acc, cols, d, ts, accumulate=True)
                tot = acc[...]
                if part == 0:
                    tot = tot * scale
                out_refs[part][:, cols] = tot.astype(BF16)

    return _call(body, grid=(S // ts,), in_specs=[_dil_spec(d, ts, C) for d in dils] * 3,
                 out_specs=(_dil_spec(1, ts, C),) * 3, out_shape=(_sds((S, C), BF16),) * 3,
                 scratch_shapes=[pltpu.VMEM((ts, LANES), F32)], name=name)(
        *[_dil_view(a, d) for lst in (dq_list, dk_list, dv_list) for a, d in zip(lst, dils)])


def _bias_tables(table, bucket_flat, *, name):
    P, _, K = bucket_flat.shape
    H = table.shape[1]
    KC = 4096

    def body(t_ref, bk_ref, out_ref):
        row = lax.broadcasted_iota(jnp.int32, (N_BUCKETS, KC), 0)
        for c in range(K // KC):
            bk = bk_ref[0, :, c * KC:(c + 1) * KC]
            onehot = (row == bk).astype(F32)
            vals = jnp.dot(t_ref[...], onehot, preferred_element_type=F32, precision=lax.Precision.HIGHEST)
            out_ref[0, :, c * KC:(c + 1) * KC] = jnp.where(bk >= 0, vals, NEG)

    return _call(body, grid=(P,),
                 in_specs=[pl.BlockSpec((H, N_BUCKETS), lambda p: (0, 0)), pl.BlockSpec((1, 1, K), lambda p: (p, 0, 0))],
                 out_specs=pl.BlockSpec((1, H, K), lambda p: (p, 0, 0)), out_shape=_sds((P, H, K), F32),
                 name=name)(table.T, bucket_flat)


def _bias_grad(dbias_flat, bucket_flat, *, name):
    P, H, K = dbias_flat.shape
    KC = 4096

    def body(db_ref, bk_ref, out_ref):
        p = pl.program_id(0)
        acc = jnp.zeros((N_BUCKETS, H), F32)
        row = lax.broadcasted_iota(jnp.int32, (N_BUCKETS, KC), 0)
        for c in range(K // KC):
            onehot = (row == bk_ref[0, :, c * KC:(c + 1) * KC]).astype(F32)
            acc = acc + lax.dot_general(onehot, db_ref[0, :, c * KC:(c + 1) * KC], (((1,), (1,)), ((), ())),
                                        preferred_element_type=F32, precision=lax.Precision.HIGHEST)

        @pl.when(p == 0)
        def _():
            out_ref[...] = acc

        @pl.when(p > 0)
        def _():
            out_ref[...] += acc

    return _call(body, grid=(P,),
                 in_specs=[pl.BlockSpec((1, H, K), lambda p: (p, 0, 0)), pl.BlockSpec((1, 1, K), lambda p: (p, 0, 0))],
                 out_specs=pl.BlockSpec((N_BUCKETS, H), lambda p: (0, 0)), out_shape=_sds((N_BUCKETS, H), F32),
                 name=name)(dbias_flat, bucket_flat)


def _allgather(blocks, *, name):
    n = len(blocks)

    def body(*refs):
        x_refs, out_refs = refs[:n], refs[n:2 * n]
        send_sems, recv_sems, local_sems = refs[2 * n:]
        x, y, c = lax.axis_index("x"), lax.axis_index("y"), lax.axis_index("c")
        me, sibling = (x, y, c), (x, y, 1 - c)
        chips = [(1 - x, y), (x, 1 - y), (1 - x, 1 - y)]

        def copy(i, k, blk, to, own=False):
            slot = out_refs[i].at[4 * blk[0] + 2 * blk[1] + blk[2]]
            return pltpu.make_async_remote_copy(src_ref=x_refs[i] if own else slot, dst_ref=slot,
                                                send_sem=send_sems.at[7 * i + k], recv_sem=recv_sems.at[7 * i + k],
                                                device_id=to, device_id_type=MESH)

        mine = [pltpu.make_async_copy(x_refs[i], out_refs[i].at[4 * x + 2 * y + c], local_sems.at[i])
                for i in range(n)]
        for cp in mine:
            cp.start()
        first = []
        for i in range(n):
            first.append(copy(i, 0, me, sibling, own=True))
            first += [copy(i, 1 + j, me, (*chip, c), own=True) for j, chip in enumerate(chips)]
        for cp in first:
            cp.start()
        passed = []
        for j, chip in enumerate(chips):
            for i in range(n):
                copy(i, 1 + j, (*chip, c), me).wait_recv()
                fwd = copy(i, 4 + j, (*chip, c), sibling)
                fwd.start()
                passed.append(fwd)
        for i in range(n):
            copy(i, 0, sibling, me).wait_recv()
            for j, chip in enumerate(chips):
                copy(i, 4 + j, (*chip, 1 - c), me).wait_recv()
        for cp in first + passed:
            cp.wait_send()
        for cp in mine:
            cp.wait()

    any_spec = pl.BlockSpec(memory_space=pl.ANY)
    return pl.pallas_call(body, out_shape=tuple(_sds((N_DEV,) + b.shape, b.dtype) for b in blocks),
                          in_specs=[any_spec] * n, out_specs=(any_spec,) * n,
                          scratch_shapes=[pltpu.SemaphoreType.DMA((7 * n,)), pltpu.SemaphoreType.DMA((7 * n,)),
                                          pltpu.SemaphoreType.DMA((n,))], name=name)(*blocks)


def _exchange(sends, *, name):
    n = len(sends)

    def body(*refs):
        send_refs, recv_refs = refs[:n], refs[n:2 * n]
        send_sems, recv_sems, local_sems = refs[2 * n:]
        x, y, c = lax.axis_index("x"), lax.axis_index("y"), lax.axis_index("c")
        me = 4 * x + 2 * y + c
        mine = [pltpu.make_async_copy(send_refs[i].at[me], recv_refs[i].at[me], local_sems.at[i]) for i in range(n)]
        for cp in mine:
            cp.start()
        copies = []
        for k in range(1, N_DEV):
            px = 1 - x if k & 4 else x
            py = 1 - y if k & 2 else y
            pc = 1 - c if k & 1 else c
            for i in range(n):
                cp = pltpu.make_async_remote_copy(src_ref=send_refs[i].at[4 * px + 2 * py + pc],
                                                  dst_ref=recv_refs[i].at[me],
                                                  send_sem=send_sems.at[7 * i + k - 1],
                                                  recv_sem=recv_sems.at[7 * i + k - 1],
                                                  device_id=(px, py, pc), device_id_type=MESH)
                cp.start()
                copies.append(cp)
        for cp in copies:
            cp.wait_recv()
        for cp in copies:
            cp.wait_send()
        for cp in mine:
            cp.wait()

    any_spec = pl.BlockSpec(memory_space=pl.ANY)
    return pl.pallas_call(body, out_shape=tuple(_sds(s.shape, s.dtype) for s in sends), in_specs=[any_spec] * n,
                          out_specs=(any_spec,) * n,
                          scratch_shapes=[pltpu.SemaphoreType.DMA((7 * n,)), pltpu.SemaphoreType.DMA((7 * n,)),
                                          pltpu.SemaphoreType.DMA((n,))], name=name)(*sends)


def _adamw(w, m, v, g_parts, *, name, tr):
    R, W = w.shape
    bc1 = 1.0 - ADAM_B1 ** ADAM_STEP
    bc2 = 1.0 - ADAM_B2 ** ADAM_STEP

    def body(w_ref, m_ref, v_ref, g_ref, go_ref, d_ref, mo_ref, vo_ref):
        g = g_ref[0].astype(F32)
        for i in range(1, N_DEV):
            g = g + g_ref[i].astype(F32)
        mn = ADAM_B1 * m_ref[...] + (1.0 - ADAM_B1) * g
        vn = ADAM_B2 * v_ref[...] + (1.0 - ADAM_B2) * (g * g)
        m_hat = mn / bc1
        v_hat = vn / bc2
        go_ref[...] = g
        d_ref[...] = -ADAM_LR * (m_hat / (jnp.sqrt(v_hat) + ADAM_EPS) + ADAM_WD * w_ref[...])
        mo_ref[...] = mn
        vo_ref[...] = vn

    row = pl.BlockSpec((tr, W), lambda i: (i, 0))
    return _call(body, grid=(R // tr,), in_specs=[row, row, row, pl.BlockSpec((N_DEV, tr, W), lambda i: (0, i, 0))],
                 out_specs=(row,) * 4, out_shape=(_sds((R, W), F32),) * 4, name=name)(w, m, v, g_parts)


def _round_up(n, k):
    return -(-n // k) * k


def _pack(arrs, width, row_mult):
    pieces, offs, r = [], [], 0
    for a in arrs:
        n = a.size
        rows = _round_up(n, width) // width
        flat = a.reshape(-1)
        if rows * width != n:
            flat = jnp.pad(flat, (0, rows * width - n))
        pieces.append(flat.reshape(rows, width))
        offs.append((r, rows, n))
        r += rows
    total = _round_up(r, row_mult)
    if total != r:
        pieces.append(jnp.zeros((total - r, width), pieces[0].dtype))
    return jnp.concatenate(pieces, axis=0), offs


def _unpack(pack, offs, shapes):
    out = []
    for (r, rows, n), shp in zip(offs, shapes):
        out.append(pack[r:r + rows].reshape(-1)[:n].reshape(shp))
    return out


def _gather_axis(full8, axis):
    moved = jnp.moveaxis(full8, 0, axis)
    shp = list(moved.shape)
    shp[axis:axis + 2] = [shp[axis] * shp[axis + 1]]
    return moved.reshape(shp)


def _split_axis(full, axis):
    shp = list(full.shape)
    shp[axis:axis + 1] = [N_DEV, shp[axis] // N_DEV]
    return jnp.moveaxis(full.reshape(shp), axis, 0)


def kernel(x, w_in, b_in, conv_dw_w, conv_dw_b, conv_ln_g, conv_ln_b, rel_bias_table, gmlp_ln_g, gmlp_ln_b, gmlp_w_s, gmlp_b_s, w_out, b_out, ln1_g, ln1_b, ffn_w_up, ffn_b_up, ffn_conv_w, ffn_conv_b, ffn_w_down, ffn_b_down, ln2_g, ln2_b, loss_target, m_w_in, m_b_in, m_conv_dw_w, m_conv_dw_b, m_conv_ln_g, m_conv_ln_b, m_rel_bias_table, m_gmlp_ln_g, m_gmlp_ln_b, m_gmlp_w_s, m_gmlp_b_s, m_w_out, m_b_out, m_ln1_g, m_ln1_b, m_ffn_w_up, m_ffn_b_up, m_ffn_conv_w, m_ffn_conv_b, m_ffn_w_down, m_ffn_b_down, m_ln2_g, m_ln2_b, v_w_in, v_b_in, v_conv_dw_w, v_conv_dw_b, v_conv_ln_g, v_conv_ln_b, v_rel_bias_table, v_gmlp_ln_g, v_gmlp_ln_b, v_gmlp_w_s, v_gmlp_b_s, v_w_out, v_b_out, v_ln1_g, v_ln1_b, v_ffn_w_up, v_ffn_b_up, v_ffn_conv_w, v_ffn_conv_b, v_ffn_w_down, v_ffn_b_down, v_ln2_g, v_ln2_b):
    W = dict(w_in=w_in, b_in=b_in, conv_dw_w=conv_dw_w, conv_dw_b=conv_dw_b, conv_ln_g=conv_ln_g,
             conv_ln_b=conv_ln_b, rel_bias_table=rel_bias_table, gmlp_ln_g=gmlp_ln_g, gmlp_ln_b=gmlp_ln_b,
             gmlp_w_s=gmlp_w_s, gmlp_b_s=gmlp_b_s, w_out=w_out, b_out=b_out, ln1_g=ln1_g, ln1_b=ln1_b,
             ffn_w_up=ffn_w_up, ffn_b_up=ffn_b_up, ffn_conv_w=ffn_conv_w, ffn_conv_b=ffn_conv_b,
             ffn_w_down=ffn_w_down, ffn_b_down=ffn_b_down, ln2_g=ln2_g, ln2_b=ln2_b)
    Mo = dict(w_in=m_w_in, b_in=m_b_in, conv_dw_w=m_conv_dw_w, conv_dw_b=m_conv_dw_b, conv_ln_g=m_conv_ln_g,
              conv_ln_b=m_conv_ln_b, rel_bias_table=m_rel_bias_table, gmlp_ln_g=m_gmlp_ln_g, gmlp_ln_b=m_gmlp_ln_b,
              gmlp_w_s=m_gmlp_w_s, gmlp_b_s=m_gmlp_b_s, w_out=m_w_out, b_out=m_b_out, ln1_g=m_ln1_g, ln1_b=m_ln1_b,
              ffn_w_up=m_ffn_w_up, ffn_b_up=m_ffn_b_up, ffn_conv_w=m_ffn_conv_w, ffn_conv_b=m_ffn_conv_b,
              ffn_w_down=m_ffn_w_down, ffn_b_down=m_ffn_b_down, ln2_g=m_ln2_g, ln2_b=m_ln2_b)
    Vo = dict(w_in=v_w_in, b_in=v_b_in, conv_dw_w=v_conv_dw_w, conv_dw_b=v_conv_dw_b, conv_ln_g=v_conv_ln_g,
              conv_ln_b=v_conv_ln_b, rel_bias_table=v_rel_bias_table, gmlp_ln_g=v_gmlp_ln_g, gmlp_ln_b=v_gmlp_ln_b,
              gmlp_w_s=v_gmlp_w_s, gmlp_b_s=v_gmlp_b_s, w_out=v_w_out, b_out=v_b_out, ln1_g=v_ln1_g, ln1_b=v_ln1_b,
              ffn_w_up=v_ffn_w_up, ffn_b_up=v_ffn_b_up, ffn_conv_w=v_ffn_conv_w, ffn_conv_b=v_ffn_conv_b,
              ffn_w_down=v_ffn_w_down, ffn_b_down=v_ffn_b_down, ln2_g=v_ln2_g, ln2_b=v_ln2_b)

    xs = x[0]
    target = loss_target[0]
    S, D = xs.shape
    F2 = ffn_b_up.shape[1]
    F = F2 // 2
    ts = min(512, S)
    G = _attn_group(S)
    tc = F // 2 if (F // 2) % LANES == 0 else F

    mat_names = SHARDED[:4]
    payload = [W[n].astype(BF16) if n in mat_names else W[n] for n in SHARDED]
    wall = _allgather(payload, name="weight_allgather")
    full = {n: _gather_axis(parts, SHARD_AXIS[n]) for n, parts in zip(SHARDED, wall)}

    tables = [_pattern_tables(w, d) for (w, d) in PATTERNS]
    bucket_flat = jnp.asarray(np.stack([np.where(v, b, -1).reshape(1, -1) for (b, v) in tables]).astype(np.int32))
    bias_all = _bias_tables(rel_bias_table, bucket_flat, name="bias_tables")
    biases = [bias_all[p].reshape(ATTN_HEADS, ATTN_BLOCK, 2 * ATTN_BLOCK) for p in range(len(PATTERNS))]
    nbs = [(S // d) // ATTN_BLOCK for (_, d) in PATTERNS]
    dils = [d for (_, d) in PATTERNS]
    scale = HEAD_DIM ** -0.5

    saved = []
    cur = xs
    for l in range(DEPTH):
        Win, Wout, Wup, Wdown = full['w_in'][l], full['w_out'][l], full['ffn_w_up'][l], full['ffn_w_down'][l]
        qcols = slice(2 * CONV_CH, 2 * CONV_CH + ATTN_CH)
        Win_s = Win.at[:, qcols].multiply(scale)
        b_in_s = b_in[l].at[qcols].multiply(scale)
        h_a, qkv, h_c = _mm([cur], [Win_s], bias=b_in_s, tm=ts, name="in_proj",
                            splits=((2 * CONV_CH, F32, 1.0), (3 * ATTN_CH, F32, 1.0), (2 * GMLP_CH, F32, 1.0)))
        conv_out, hc = _conv_fwd(h_a, full['conv_dw_w'][l], conv_dw_b[l], conv_ln_g[l], conv_ln_b[l],
                                 name="conv_fwd", ts=ts)
        qkv_d = _dilate_qkv(qkv, name="dilate_qkv", ts=ts)
        o_ps, lse_ps = [], []
        for p, d in enumerate(dils):
            o_p, lse_p = _attn_fwd(qkv_d[p], biases[p], name=f"attn_fwd_d{d}", nb=nbs[p], G=G)
            o_ps.append(o_p)
            lse_ps.append(lse_p)
        attn_out, lse = _attn_merge(o_ps, lse_ps, dils, name="attn_merge", ts=ts)
        w_tril = jnp.tril(gmlp_w_s[l]).astype(BF16)
        bs_rows = jnp.repeat(gmlp_b_s[l].T, GMLP_GROUP_DIM, axis=1)
        gm_out = _gmlp_fwd(h_c, gmlp_ln_g[l], gmlp_ln_b[l], w_tril, bs_rows, name="gmlp_fwd", ts=ts)
        x1, xhat1, rstd1 = _mm([conv_out, attn_out, gm_out],
                               [Wout[:CONV_CH], Wout[CONV_CH:CONV_CH + ATTN_CH], Wout[CONV_CH + ATTN_CH:]],
                               bias=b_out[l], resid=cur, resid_scale=ALPHA, ln=(ln1_g[l], ln1_b[l]), tm=ts,
                               name="out_proj_ln")
        hu = _mm([x1], [Wup], bias=ffn_b_up[l], tm=ts, tn=F, name="ffn_up")
        act, act_dg, act_dv = _ffn_act_fwd(hu, full['ffn_conv_w'][l], ffn_conv_b[l], name="ffn_act_fwd",
                                           ts=min(256, S), tc=tc)
        x2, xhat2, rstd2 = _mm([act], [Wdown], bias=ffn_b_down[l], resid=x1, resid_scale=ALPHA,
                               ln=(ln2_g[l], ln2_b[l]), tm=ts, name="ffn_down_ln")
        saved.append(dict(x0=cur, h_a=h_a, h_c=h_c, qkv_d=qkv_d, hc=hc, conv_out=conv_out, attn_out=attn_out,
                          lse=lse, gm_out=gm_out, w_tril=w_tril, bs_rows=bs_rows, x1=x1, xhat1=xhat1, rstd1=rstd1,
                          hu=hu, act=act, act_dg=act_dg, act_dv=act_dv, xhat2=xhat2, rstd2=rstd2))
        cur = x2

    grads = {n: [None] * DEPTH for n in WEIGHTS if n != 'rel_bias_table'}
    drel = None
    dx = None
    loss_part = None
    tk = min(1024, S)
    for l in reversed(range(DEPTH)):
        sv = saved[l]
        Win, Wout, Wup, Wdown = full['w_in'][l], full['w_out'][l], full['ffn_w_up'][l], full['ffn_w_down'][l]
        if dx is None:
            dz2, dg2, db2, dzs2, loss_part = _ln_bwd(sv['xhat2'], sv['rstd2'], ln2_g[l], b=ln2_b[l], target=target,
                                                     name="ln2_bwd_loss", ts=ts)
        else:
            dz2, dg2, db2, dzs2 = _ln_bwd(sv['xhat2'], sv['rstd2'], ln2_g[l], dy=dx, name="ln_bwd", ts=ts)
        grads['ln2_g'][l], grads['ln2_b'][l], grads['ffn_b_down'][l] = dg2[0], db2[0], dzs2[0]
        grads['ffn_w_down'][l] = _mm_tn(sv['act'], dz2, tm=F // 2 if (F // 2) % LANES == 0 else F, tn=D, tk=tk,
                                        name="dw_down")
        dact = _mm([dz2], [Wdown.T], tm=ts, name="dact")
        dhg, dhv, dwg, dwv, dbg, dbv, dug, duv = _ffn_act_bwd(sv['hu'], dact, sv['act_dg'], sv['act_dv'],
                                                              full['ffn_conv_w'][l], name="ffn_act_bwd",
                                                              ts=min(256, S), tc=tc)
        grads['ffn_conv_w'][l] = jnp.concatenate([dwg, dwv], axis=1)
        grads['ffn_conv_b'][l] = jnp.concatenate([dbg, dbv], axis=1)[0]
        grads['ffn_b_up'][l] = jnp.concatenate([dug, duv], axis=1)[0]
        grads['ffn_w_up'][l] = jnp.concatenate(
            [_mm_tn(sv['x1'], dhg, tm=D, tn=tc, tk=tk, name="dw_up"),
             _mm_tn(sv['x1'], dhv, tm=D, tn=tc, tk=tk, name="dw_up")], axis=1)
        WupT = Wup.T
        dx1 = _mm([dhg, dhv], [WupT[:F], WupT[F:]], resid=dz2, resid_scale=ALPHA, tm=ts, name="dx1")
        dz1, dg1, db1, dzs1 = _ln_bwd(sv['xhat1'], sv['rstd1'], ln1_g[l], dy=dx1, name="ln_bwd", ts=ts)
        grads['ln1_g'][l], grads['ln1_b'][l], grads['b_out'][l] = dg1[0], db1[0], dzs1[0]
        grads['w_out'][l] = jnp.concatenate(
            [_mm_tn(sv['conv_out'], dz1, tm=CONV_CH, tn=D, tk=tk, name="dw_out_conv"),
             _mm_tn(sv['attn_out'], dz1, tm=ATTN_CH, tn=D, tk=tk, name="dw_out_attn"),
             _mm_tn(sv['gm_out'], dz1, tm=GMLP_CH, tn=D, tk=tk, name="dw_out_conv")], axis=0)
        dc_conv, dc_attn, dc_gm = _mm([dz1], [Wout.T], tm=ts, name="dcat",
                                      splits=((CONV_CH, F32, 1.0), (ATTN_CH, F32, 1.0), (GMLP_CH, F32, 1.0)))
        dh_a, ddw, ddwb, dclg, dclb = _conv_bwd(sv['h_a'], sv['hc'], dc_conv, full['conv_dw_w'][l], conv_ln_g[l],
                                                conv_ln_b[l], name="conv_bwd", ts=ts)
        grads['conv_dw_w'][l], grads['conv_dw_b'][l] = ddw[:CONV_WIDTH], ddwb[0]
        grads['conv_ln_g'][l], grads['conv_ln_b'][l] = dclg[0], dclb[0]
        dd_d, do_d = _attn_prep(dc_attn, sv['attn_out'], dils, name="attn_prep", ts=ts)
        dqs, dks, dvs, dbs = [], [], [], []
        for p, d in enumerate(dils):
            dq, dk, dv, dbias = _attn_bwd(sv['qkv_d'][p], do_d[p], sv['lse'][p], dd_d[p], biases[p],
                                          name=f"attn_bwd_d{d}", nb=nbs[p], G=G)
            dqs.append(dq)
            dks.append(dk)
            dvs.append(dv)
            dbs.append(dbias.reshape(1, ATTN_HEADS, -1))
        dqkv = _attn_combine(dqs, dks, dvs, dils, name="attn_combine", ts=ts)
        dr = _bias_grad(jnp.concatenate(dbs, axis=0), bucket_flat, name="bias_grad")
        drel = dr if drel is None else drel + dr
        w_tril_t = jnp.swapaxes(sv['w_tril'], 1, 2)
        dh_c, dws, dbs_acc, dglg, dglb = _gmlp_bwd(sv['h_c'], dc_gm, gmlp_ln_g[l], gmlp_ln_b[l], sv['w_tril'],
                                                   w_tril_t, sv['bs_rows'], name="gmlp_bwd", ts=ts)
        grads['gmlp_w_s'][l] = jnp.tril(dws)
        grads['gmlp_b_s'][l] = dbs_acc[:, :GMLP_GROUPS].T
        grads['gmlp_ln_g'][l], grads['gmlp_ln_b'][l] = dglg[0], dglb[0]
        dw_in = _mm_tn_shared(sv['x0'], [dh_a, *dqkv, dh_c], tk=tk, name="dw_in")
        grads['w_in'][l] = jnp.concatenate([w for w, _ in dw_in], axis=1)
        grads['b_in'][l] = jnp.concatenate([c for _, c in dw_in], axis=1)[0]
        WinT = Win.T
        edges = [0, 2 * CONV_CH] + [2 * CONV_CH + k * ATTN_CH for k in (1, 2, 3)] + [WinT.shape[0]]
        dx = _mm([dh_a, *dqkv, dh_c], [WinT[a:b] for a, b in zip(edges[:-1], edges[1:])], resid=dz1,
                 resid_scale=ALPHA, tm=ts, name="dx0")

    gfull = {n: jnp.stack(v) for n, v in grads.items()}
    gfull['rel_bias_table'] = drel

    sends = []
    for n in SHARDED:
        parts = _split_axis(gfull[n], SHARD_AXIS[n])
        sends.append(parts.reshape(N_DEV, -1, parts.shape[-1]).astype(BF16))
    recvs = _exchange(sends, name="grad_exchange")
    shard_out = [[], [], [], []]
    for n, recv in zip(SHARDED, recvs):
        shp = W[n].shape
        rows = recv.shape[1]
        tr = rows // 4 if rows % 64 == 0 else rows
        outs = _adamw(*[src[n].reshape(rows, shp[-1]) for src in (W, Mo, Vo)], recv, name=f"adamw_{n}", tr=tr)
        for kind in range(4):
            shard_out[kind].append(outs[kind].reshape(shp))

    gsmall, soffs = _pack([gfull[n] for n in SMALL], LANES, 8)
    gall = _allgather([gsmall], name="small_grad_allgather")[0]
    spacks = [_pack([src[n] for n in SMALL], LANES, 8)[0] for src in (W, Mo, Vo)]
    souts = _adamw(spacks[0], spacks[1], spacks[2], gall, name="adamw_small", tr=gsmall.shape[0])
    small_out = [_unpack(o, soffs, [W[n].shape for n in SMALL]) for o in souts]

    loss = lax.psum(loss_part[0, 0], ("x", "y", "c"))
    by_kind = []
    for kind in range(4):
        d = dict(zip(SHARDED, shard_out[kind]))
        d.update(zip(SMALL, small_out[kind]))
        by_kind.append([d[n] for n in WEIGHTS])
    return (loss, dx[None], *by_kind[0], *by_kind[1], *by_kind[2], *by_kind[3])
```

```python
import math

import numpy as np
import jax
import jax.numpy as jnp
from jax import lax
from jax.experimental import pallas as pl
from jax.experimental.pallas import tpu as pltpu

F32 = jnp.float32
BF16 = jnp.bfloat16

DEPTH = 2
HEAD_DIM = 64
CONV_CH = 256
CONV_WIDTH = 31
ATTN_HEADS = 8
ATTN_CH = ATTN_HEADS * HEAD_DIM
PATTERNS = ((128, 1), (512, 4), (2048, 16))
ATTN_BLOCK = 128
N_BUCKETS = 32
MAX_DISTANCE = 2048
GMLP_CH = 256
GMLP_GROUPS = 4
GMLP_GROUP_DIM = GMLP_CH // GMLP_GROUPS
CHUNK = 128
FFN_CONV_WIDTH = 3
LN_EPS = 1e-5
ALPHA = (2.0 * DEPTH) ** 0.25
ADAM_LR = 0.001
ADAM_B1 = 0.9
ADAM_B2 = 0.999
ADAM_EPS = 1e-08
ADAM_WD = 0.01
ADAM_STEP = 10
NEG = -1e30
N_DEV = 8
LANES = 128
CONV_HALO = 32
FFN_HALO = 8
FFN_ROWS = 16
FFN_STRIP = 256
MESH = pl.DeviceIdType.MESH

WEIGHTS = ['w_in', 'b_in', 'conv_dw_w', 'conv_dw_b', 'conv_ln_g', 'conv_ln_b', 'rel_bias_table', 'gmlp_ln_g',
           'gmlp_ln_b', 'gmlp_w_s', 'gmlp_b_s', 'w_out', 'b_out', 'ln1_g', 'ln1_b', 'ffn_w_up', 'ffn_b_up',
           'ffn_conv_w', 'ffn_conv_b', 'ffn_w_down', 'ffn_b_down', 'ln2_g', 'ln2_b']
SHARDED = ['w_in', 'w_out', 'ffn_w_up', 'ffn_w_down', 'conv_dw_w', 'ffn_conv_w']
SHARD_AXIS = {'w_in': 2, 'w_out': 1, 'ffn_w_up': 2, 'ffn_w_down': 1, 'conv_dw_w': 2, 'ffn_conv_w': 2}
SMALL = [n for n in WEIGHTS if n not in SHARDED]


def _call(body, *, grid=(), vmem_mb=48, **kw):
    params = pltpu.CompilerParams(dimension_semantics=("arbitrary",) * len(grid), vmem_limit_bytes=vmem_mb << 20)
    return pl.pallas_call(body, grid=grid, compiler_params=params, **kw)


def _sds(shape, dtype):
    return jax.ShapeDtypeStruct(shape, dtype)


def _ln_rows(z):
    mu = jnp.mean(z, axis=-1, keepdims=True)
    zc = z - mu
    var = jnp.mean(zc * zc, axis=-1, keepdims=True)
    rstd = lax.rsqrt(var + LN_EPS)
    return zc * rstd, rstd


def _ln_bwd_rows(dxhat, xhat, rstd):
    m1 = jnp.mean(dxhat, axis=-1, keepdims=True)
    m2 = jnp.mean(dxhat * xhat, axis=-1, keepdims=True)
    return rstd * (dxhat - m1 - xhat * m2)


def _colsum(v):
    return jnp.sum(v, axis=0, keepdims=True)


def _mm(a_list, w_list, *, name, tm, tn=None, bias=None, resid=None, resid_scale=1.0, ln=None, splits=None,
        out_dtype=F32):
    na = len(a_list)
    M = a_list[0].shape[0]
    N = w_list[0].shape[1]
    tn = N if tn is None else tn
    assert M % tm == 0 and N % tn == 0
    assert ln is None or tn == N
    assert splits is None or tn == N

    def body(*refs):
        a_refs, w_refs = refs[:na], refs[na:2 * na]
        pos = 2 * na
        acc = None
        for a_ref, w_ref in zip(a_refs, w_refs):
            t = jnp.dot(a_ref[...].astype(BF16), w_ref[...], preferred_element_type=F32)
            acc = t if acc is None else acc + t
        if bias is not None:
            acc = acc + refs[pos][...]
            pos += 1
        if resid is not None:
            acc = resid_scale * refs[pos][...] + acc
            pos += 1
        if ln is not None:
            g_ref, b_ref = refs[pos], refs[pos + 1]
            y_ref, xhat_ref, rstd_ref = refs[pos + 2], refs[pos + 3], refs[pos + 4]
            xhat, rstd = _ln_rows(acc)
            y_ref[...] = xhat * g_ref[...] + b_ref[...]
            xhat_ref[...] = xhat
            rstd_ref[...] = rstd
        elif splits is not None:
            c0 = 0
            for o_ref, (width, dtype, scale) in zip(refs[pos:], splits):
                part = acc[:, c0:c0 + width]
                if scale != 1.0:
                    part = part * scale
                o_ref[...] = part.astype(dtype)
                c0 += width
        else:
            refs[pos][...] = acc.astype(out_dtype)

    in_specs = [pl.BlockSpec((tm, a.shape[1]), lambda j, i: (i, 0)) for a in a_list]
    in_specs += [pl.BlockSpec((w.shape[0], tn), lambda j, i: (0, j)) for w in w_list]
    args = list(a_list) + list(w_list)
    if bias is not None:
        in_specs.append(pl.BlockSpec((1, tn), lambda j, i: (0, j)))
        args.append(bias.reshape(1, N))
    if resid is not None:
        in_specs.append(pl.BlockSpec((tm, tn), lambda j, i: (i, j)))
        args.append(resid)
    if ln is not None:
        in_specs += [pl.BlockSpec((1, N), lambda j, i: (0, 0))] * 2
        args += [ln[0].reshape(1, N), ln[1].reshape(1, N)]
        out_shape = (_sds((M, N), F32), _sds((M, N), F32), _sds((M, 1), F32))
        out_specs = (pl.BlockSpec((tm, N), lambda j, i: (i, 0)), pl.BlockSpec((tm, N), lambda j, i: (i, 0)),
                     pl.BlockSpec((tm, 1), lambda j, i: (i, 0)))
    elif splits is not None:
        out_shape = tuple(_sds((M, w), d) for (w, d, _) in splits)
        out_specs = tuple(pl.BlockSpec((tm, w), lambda j, i: (i, 0)) for (w, _, _) in splits)
    else:
        out_shape = _sds((M, N), out_dtype)
        out_specs = pl.BlockSpec((tm, tn), lambda j, i: (i, j))
    return _call(body, grid=(N // tn, M // tm), in_specs=in_specs, out_specs=out_specs, out_shape=out_shape,
                 name=name, vmem_mb=56)(*args)


def _mm_tn(a, dy, *, name, tm, tn, tk, colsum=False):
    S, Ka = a.shape
    N = dy.shape[1]
    assert S % tk == 0 and Ka % tm == 0 and N % tn == 0

    def body(a_ref, dy_ref, out_ref, *cs):
        i, k = pl.program_id(1), pl.program_id(2)
        dyb = dy_ref[...]
        part = lax.dot_general(a_ref[...].astype(BF16), dyb.astype(BF16), (((0,), (0,)), ((), ())),
                               preferred_element_type=F32)

        @pl.when(k == 0)
        def _():
            out_ref[...] = part

        @pl.when(k > 0)
        def _():
            out_ref[...] += part

        if colsum:
            cs_ref = cs[0]
            s = _colsum(dyb.astype(F32))

            @pl.when((i == 0) & (k == 0))
            def _():
                cs_ref[...] = s

            @pl.when((i == 0) & (k > 0))
            def _():
                cs_ref[...] += s

    out_shape = [_sds((Ka, N), F32)]
    out_specs = [pl.BlockSpec((tm, tn), lambda j, i, k: (i, j))]
    if colsum:
        out_shape.append(_sds((1, N), F32))
        out_specs.append(pl.BlockSpec((1, tn), lambda j, i, k: (0, j)))
    res = _call(body, grid=(N // tn, Ka // tm, S // tk),
                in_specs=[pl.BlockSpec((tk, tm), lambda j, i, k: (k, i)), pl.BlockSpec((tk, tn), lambda j, i, k: (k, j))],
                out_specs=tuple(out_specs), out_shape=tuple(out_shape), name=name, vmem_mb=56)(a, dy)
    return res if colsum else res[0]


def _mm_tn_shared(a, dys, *, name, tk):
    S, Ka = a.shape
    n = len(dys)
    assert S % tk == 0

    def body(*refs):
        a_ref, dy_refs, outs = refs[0], refs[1:1 + n], refs[1 + n:]
        k = pl.program_id(0)
        a_bf = a_ref[...].astype(BF16)
        for i, dy_ref in enumerate(dy_refs):
            dyb = dy_ref[...]
            part = lax.dot_general(a_bf, dyb.astype(BF16), (((0,), (0,)), ((), ())), preferred_element_type=F32)
            s = _colsum(dyb.astype(F32))
            w_ref, c_ref = outs[2 * i], outs[2 * i + 1]

            @pl.when(k == 0)
            def _():
                w_ref[...] = part
                c_ref[...] = s

            @pl.when(k > 0)
            def _():
                w_ref[...] += part
                c_ref[...] += s

    in_specs = [pl.BlockSpec((tk, Ka), lambda k: (k, 0))]
    in_specs += [pl.BlockSpec((tk, dy.shape[1]), lambda k: (k, 0)) for dy in dys]
    out_specs, out_shape = [], []
    for dy in dys:
        N = dy.shape[1]
        out_specs += [pl.BlockSpec((Ka, N), lambda k: (0, 0)), pl.BlockSpec((1, N), lambda k: (0, 0))]
        out_shape += [_sds((Ka, N), F32), _sds((1, N), F32)]
    res = _call(body, grid=(S // tk,), in_specs=in_specs, out_specs=tuple(out_specs), out_shape=tuple(out_shape),
                name=name, vmem_mb=56)(a, *dys)
    return [(res[2 * i], res[2 * i + 1]) for i in range(n)]


def _ln_bwd(xhat, rstd, g, *, name, ts, dy=None, b=None, target=None):
    S, D = xhat.shape
    from_loss = target is not None

    def body(*refs):
        if from_loss:
            xhat_ref, rstd_ref, g_ref, b_ref, t_ref, dz_ref, dg_ref, db_ref, dzs_ref, loss_ref = refs
        else:
            xhat_ref, rstd_ref, g_ref, dy_ref, dz_ref, dg_ref, db_ref, dzs_ref = refs
        i = pl.program_id(0)
        xh = xhat_ref[...]
        gg = g_ref[...]
        if from_loss:
            err = xh * gg + b_ref[...] - t_ref[...]
            dyv = err * (1.0 / D)
            lsum = (0.5 / D) * jnp.sum(err * err, axis=(0, 1), keepdims=True)
        else:
            dyv = dy_ref[...]
        dz = _ln_bwd_rows(dyv * gg, xh, rstd_ref[...])
        dz_ref[...] = dz
        parts = [(dg_ref, _colsum(dyv * xh)), (db_ref, _colsum(dyv)), (dzs_ref, _colsum(dz))]
        if from_loss:
            parts.append((loss_ref, lsum))

        @pl.when(i == 0)
        def _():
            for r, v in parts:
                r[...] = v

        @pl.when(i > 0)
        def _():
            for r, v in parts:
                r[...] += v

    row = pl.BlockSpec((ts, D), lambda i: (i, 0))
    vec = pl.BlockSpec((1, D), lambda i: (0, 0))
    in_specs = [row, pl.BlockSpec((ts, 1), lambda i: (i, 0)), vec]
    args = [xhat, rstd, g.reshape(1, D)]
    if from_loss:
        in_specs += [vec, row]
        args += [b.reshape(1, D), target]
    else:
        in_specs += [row]
        args += [dy]
    out_shape = [_sds((S, D), F32), _sds((1, D), F32), _sds((1, D), F32), _sds((1, D), F32)]
    out_specs = [row, vec, vec, vec]
    if from_loss:
        out_shape.append(_sds((1, 1), F32))
        out_specs.append(pl.BlockSpec((1, 1), lambda i: (0, 0)))
    return _call(body, grid=(S // ts,), in_specs=in_specs, out_specs=tuple(out_specs), out_shape=tuple(out_shape),
                 name=name)(*args)


def _glu(v):
    return v[:, :CONV_CH] * jax.nn.sigmoid(v[:, CONV_CH:])


def _copies_moved_back(buf, sh, rows):
    for b in range(1, 8):
        sh[b - 1, 8:rows, :] = buf[pl.ds(8 - b, rows - 8), :]


def _copies_moved_ahead(buf, sh, rows):
    for b in range(1, 8):
        sh[b - 1, 0:rows - 8, :] = buf[pl.ds(b, rows - 8), :]


def _rows_back(buf, sh, start, s, n):
    a, b = divmod(s, 8)
    return buf[pl.ds(start - 8 * a, n), :] if b == 0 else sh[b - 1, pl.ds(start - 8 * a, n), :]


def _rows_ahead(buf, sh, start, s, n):
    a, b = divmod(s, 8)
    return buf[pl.ds(start + 8 * a, n), :] if b == 0 else sh[b - 1, pl.ds(start + 8 * a, n), :]


def _conv_fwd(h_a, dw_w, dw_b, ln_g, ln_b, *, name, ts):
    S = h_a.shape[0]
    C, K, HB = CONV_CH, CONV_WIDTH, CONV_HALO
    RC = 128

    def body(h_ref, halo_ref, w_ref, b_ref, g_ref, bb_ref, out_ref, hc_ref, gbuf, gsh):
        i = pl.program_id(0)
        gbuf[0:HB, :] = jnp.where(i > 0, _glu(halo_ref[...]), 0.0)
        gbuf[HB:HB + ts, :] = _glu(h_ref[...])
        _copies_moved_back(gbuf, gsh, HB + ts)
        for r0 in range(0, ts, RC):
            acc = jnp.zeros((RC, C), F32) + b_ref[...]
            for k in range(K):
                acc = acc + w_ref[k:k + 1, :] * _rows_back(gbuf, gsh, r0 + HB, K - 1 - k, RC)
            hc_ref[r0:r0 + RC, :] = acc
            xhat, _ = _ln_rows(acc)
            hn = xhat * g_ref[...] + bb_ref[...]
            out_ref[r0:r0 + RC, :] = (hn * jax.nn.sigmoid(hn)).astype(BF16)

    nb = ts // HB
    vec = pl.BlockSpec((1, C), lambda i: (0, 0))
    return _call(body, grid=(S // ts,),
                 in_specs=[pl.BlockSpec((ts, 2 * C), lambda i: (i, 0)),
                           pl.BlockSpec((HB, 2 * C), lambda i: (jnp.maximum(i * nb - 1, 0), 0)),
                           pl.BlockSpec((K, C), lambda i: (0, 0)), vec, vec, vec],
                 out_specs=(pl.BlockSpec((ts, C), lambda i: (i, 0)), pl.BlockSpec((ts, C), lambda i: (i, 0))),
                 out_shape=(_sds((S, C), BF16), _sds((S, C), F32)),
                 scratch_shapes=[pltpu.VMEM((HB + ts, C), F32), pltpu.VMEM((7, HB + ts, C), F32)], name=name)(
        h_a, h_a, dw_w, dw_b.reshape(1, C), ln_g.reshape(1, C), ln_b.reshape(1, C))


def _conv_bwd(h_a, hc, dout, dw_w, ln_g, ln_b, *, name, ts):
    S = h_a.shape[0]
    C, K, HB = CONV_CH, CONV_WIDTH, CONV_HALO
    RC = 128
    n = S // ts

    def dconv_out(hc_v, do_v, g_ref, bb_ref):
        xhat, rstd = _ln_rows(hc_v)
        hn = xhat * g_ref[...] + bb_ref[...]
        sg = jax.nn.sigmoid(hn)
        dhn = do_v * (sg * (1.0 + hn * (1.0 - sg)))
        return _ln_bwd_rows(dhn * g_ref[...], xhat, rstd), dhn, xhat

    def body(h_ref, hprev_ref, hc_ref, hcnext_ref, do_ref, donext_ref, w_ref, g_ref, bb_ref,
             dh_ref, dw_ref, dwb_ref, dg_ref, db_ref, gbuf, dbuf, gsh, dsh):
        i = pl.program_id(0)
        hv = h_ref[...]
        gbuf[0:HB, :] = jnp.where(i > 0, _glu(hprev_ref[...]), 0.0)
        gbuf[HB:HB + ts, :] = _glu(hv)
        dhc, dhn, xhat = dconv_out(hc_ref[...], do_ref[...], g_ref, bb_ref)
        dhc_next, _, _ = dconv_out(hcnext_ref[...], donext_ref[...], g_ref, bb_ref)
        dbuf[0:ts, :] = dhc
        dbuf[ts:ts + HB, :] = jnp.where(i < n - 1, dhc_next, 0.0)
        _copies_moved_back(gbuf, gsh, HB + ts)
        _copies_moved_ahead(dbuf, dsh, ts + HB)
        dw_rows = []
        for k in range(K):
            acc_k = jnp.zeros((1, C), F32)
            for r0 in range(0, ts, RC):
                acc_k = acc_k + _colsum(dbuf[r0:r0 + RC, :] * _rows_back(gbuf, gsh, r0 + HB, K - 1 - k, RC))
            dw_rows.append(acc_k)
        dw_rows.append(jnp.zeros((1, C), F32))
        dw_tile = jnp.concatenate(dw_rows, axis=0)
        for r0 in range(0, ts, RC):
            acc = jnp.zeros((RC, C), F32)
            for k in range(K):
                acc = acc + w_ref[k:k + 1, :] * _rows_ahead(dbuf, dsh, r0, K - 1 - k, RC)
            a = hv[r0:r0 + RC, :C]
            sg = jax.nn.sigmoid(hv[r0:r0 + RC, C:])
            dh_ref[r0:r0 + RC, :C] = (acc * sg).astype(BF16)
            dh_ref[r0:r0 + RC, C:] = (acc * a * sg * (1.0 - sg)).astype(BF16)
        parts = [(dw_ref, dw_tile), (dwb_ref, _colsum(dhc)), (dg_ref, _colsum(dhn * xhat)), (db_ref, _colsum(dhn))]

        @pl.when(i == 0)
        def _():
            for r, v in parts:
                r[...] = v

        @pl.when(i > 0)
        def _():
            for r, v in parts:
                r[...] += v

    nb = ts // HB
    last = S // HB - 1
    vec = pl.BlockSpec((1, C), lambda i: (0, 0))
    nxt = lambda i: (jnp.minimum((i + 1) * nb, last), 0)
    return _call(body, grid=(n,),
                 in_specs=[pl.BlockSpec((ts, 2 * C), lambda i: (i, 0)),
                           pl.BlockSpec((HB, 2 * C), lambda i: (jnp.maximum(i * nb - 1, 0), 0)),
                           pl.BlockSpec((ts, C), lambda i: (i, 0)), pl.BlockSpec((HB, C), nxt),
                           pl.BlockSpec((ts, C), lambda i: (i, 0)), pl.BlockSpec((HB, C), nxt),
                           pl.BlockSpec((K, C), lambda i: (0, 0)), vec, vec],
                 out_specs=(pl.BlockSpec((ts, 2 * C), lambda i: (i, 0)), pl.BlockSpec((K + 1, C), lambda i: (0, 0)),
                            vec, vec, vec),
                 out_shape=(_sds((S, 2 * C), BF16), _sds((K + 1, C), F32), _sds((1, C), F32), _sds((1, C), F32),
                            _sds((1, C), F32)),
                 scratch_shapes=[pltpu.VMEM((HB + ts, C), F32), pltpu.VMEM((ts + HB, C), F32),
                                 pltpu.VMEM((7, HB + ts, C), F32), pltpu.VMEM((7, ts + HB, C), F32)], name=name)(
        h_a, h_a, hc, hc, dout, dout, dw_w, ln_g.reshape(1, C), ln_b.reshape(1, C))


def _gmlp_mix(vn_bf, w_ref, mix_buf, ts):
    for ch in range(ts // CHUNK):
        for g in range(GMLP_GROUPS):
            vg = vn_bf[ch * CHUNK:(ch + 1) * CHUNK, g * GMLP_GROUP_DIM:(g + 1) * GMLP_GROUP_DIM]
            mix_buf[ch * CHUNK:(ch + 1) * CHUNK, g * GMLP_GROUP_DIM:(g + 1) * GMLP_GROUP_DIM] = jnp.dot(
                w_ref[g], vg, preferred_element_type=F32)


def _gmlp_fwd(h_c, ln_g, ln_b, w_tril, bs_rows, *, name, ts):
    S = h_c.shape[0]
    C = GMLP_CH

    def body(h_ref, g_ref, b_ref, w_ref, bs_ref, out_ref, mix_buf):
        hv = h_ref[...]
        xhat, _ = _ln_rows(hv[:, C:])
        vn = (xhat * g_ref[...] + b_ref[...]).astype(BF16)
        _gmlp_mix(vn, w_ref, mix_buf, ts)
        for ch in range(ts // CHUNK):
            rows = slice(ch * CHUNK, (ch + 1) * CHUNK)
            out_ref[rows, :] = (hv[rows, :C] * (mix_buf[rows, :] + bs_ref[...])).astype(BF16)

    vec = pl.BlockSpec((1, C), lambda i: (0, 0))
    return _call(body, grid=(S // ts,),
                 in_specs=[pl.BlockSpec((ts, 2 * C), lambda i: (i, 0)), vec, vec,
                           pl.BlockSpec((GMLP_GROUPS, CHUNK, CHUNK), lambda i: (0, 0, 0)),
                           pl.BlockSpec((CHUNK, C), lambda i: (0, 0))],
                 out_specs=pl.BlockSpec((ts, C), lambda i: (i, 0)), out_shape=_sds((S, C), BF16),
                 scratch_shapes=[pltpu.VMEM((ts, C), F32)], name=name)(
        h_c, ln_g.reshape(1, C), ln_b.reshape(1, C), w_tril, bs_rows)


def _gmlp_bwd(h_c, dout, ln_g, ln_b, w_tril, w_tril_t, bs_rows, *, name, ts):
    S = h_c.shape[0]
    C, G, GD = GMLP_CH, GMLP_GROUPS, GMLP_GROUP_DIM

    def body(h_ref, do_ref, g_ref, b_ref, w_ref, wt_ref, bs_ref, dh_ref, dw_ref, dbs_ref, dg_ref, db_ref,
             mix_buf, dvn_buf):
        i = pl.program_id(0)
        hv = h_ref[...]
        u = hv[:, :C]
        xhat, rstd = _ln_rows(hv[:, C:])
        vn = (xhat * g_ref[...] + b_ref[...]).astype(BF16)
        _gmlp_mix(vn, w_ref, mix_buf, ts)
        do = do_ref[...]
        dmixed = do * u
        dm_bf = dmixed.astype(BF16)
        lane = lax.broadcasted_iota(jnp.int32, (CHUNK, LANES), 1)
        dbs = jnp.zeros((CHUNK, LANES), F32)
        dws = [jnp.zeros((CHUNK, CHUNK), F32) for _ in range(G)]
        for ch in range(ts // CHUNK):
            rows = slice(ch * CHUNK, (ch + 1) * CHUNK)
            dh_ref[rows, :C] = (do[rows, :] * (mix_buf[rows, :] + bs_ref[...])).astype(BF16)
            for g in range(G):
                cols = slice(g * GD, (g + 1) * GD)
                dmg = dm_bf[rows, cols]
                dvn_buf[rows, cols] = jnp.dot(wt_ref[g], dmg, preferred_element_type=F32)
                dws[g] = dws[g] + lax.dot_general(dmg, vn[rows, cols], (((1,), (1,)), ((), ())),
                                                  preferred_element_type=F32)
                rs = jnp.sum(dmixed[rows, cols], axis=1, keepdims=True)
                dbs = dbs + jnp.where(lane == g, rs, 0.0)
        dvn = dvn_buf[...]
        dh_ref[:, C:] = _ln_bwd_rows(dvn * g_ref[...], xhat, rstd).astype(BF16)
        dgv, dbv = _colsum(dvn * xhat), _colsum(dvn)

        @pl.when(i == 0)
        def _():
            for g in range(G):
                dw_ref[g] = dws[g]
            dbs_ref[...] = dbs
            dg_ref[...] = dgv
            db_ref[...] = dbv

        @pl.when(i > 0)
        def _():
            for g in range(G):
                dw_ref[g] += dws[g]
            dbs_ref[...] += dbs
            dg_ref[...] += dgv
            db_ref[...] += dbv

    vec = pl.BlockSpec((1, C), lambda i: (0, 0))
    wspec = pl.BlockSpec((G, CHUNK, CHUNK), lambda i: (0, 0, 0))
    return _call(body, grid=(S // ts,),
                 in_specs=[pl.BlockSpec((ts, 2 * C), lambda i: (i, 0)),
                           pl.BlockSpec((ts, C), lambda i: (i, 0)), vec, vec, wspec, wspec,
                           pl.BlockSpec((CHUNK, C), lambda i: (0, 0))],
                 out_specs=(pl.BlockSpec((ts, 2 * C), lambda i: (i, 0)), wspec,
                            pl.BlockSpec((CHUNK, LANES), lambda i: (0, 0)), vec, vec),
                 out_shape=(_sds((S, 2 * C), BF16), _sds((G, CHUNK, CHUNK), F32), _sds((CHUNK, LANES), F32),
                            _sds((1, C), F32), _sds((1, C), F32)),
                 scratch_shapes=[pltpu.VMEM((ts, C), F32), pltpu.VMEM((ts, C), F32)], name=name)(
        h_c, dout, ln_g.reshape(1, C), ln_b.reshape(1, C), w_tril, w_tril_t, bs_rows)


def _ffn_act_fwd(hu, cw, cb, *, name, ts, tc):
    S, F2 = hu.shape
    F = F2 // 2
    nj = F // tc
    HB = FFN_HALO
    nb = ts // HB

    def body(g_ref, v_ref, gh_ref, vh_ref, wg_ref, wv_ref, bg_ref, bv_ref, act_ref, dg_ref, dv_ref):
        i = pl.program_id(0)
        row = lax.broadcasted_iota(jnp.int32, (HB, tc), 0)

        def conv_chunk(x_ref, w_ref, b_ref, c, carry):
            cur = x_ref[c * HB:(c + 1) * HB, :]
            r1, r2 = pltpu.roll(cur, 1, 0), pltpu.roll(cur, 2, 0)
            x1 = jnp.where(row < 1, carry[0], r1)
            x2 = jnp.where(row < 2, carry[1], r2)
            out = (w_ref[0:1, :] * x2 + w_ref[1:2, :] * x1 + w_ref[2:3, :] * cur) + b_ref[...]
            return out, (r1, r2)

        def first_carry(h_ref):
            prev = jnp.where(i > 0, h_ref[...], 0.0)
            return pltpu.roll(prev, 1, 0), pltpu.roll(prev, 2, 0)

        cg, cv = first_carry(gh_ref), first_carry(vh_ref)
        for r0 in range(0, ts, FFN_ROWS):
            acts, dgs, dvs = [], [], []
            for c in range(r0 // HB, (r0 + FFN_ROWS) // HB):
                gc, cg = conv_chunk(g_ref, wg_ref, bg_ref, c, cg)
                vc, cv = conv_chunk(v_ref, wv_ref, bv_ref, c, cv)
                sg = jax.nn.sigmoid(gc)
                silu = gc * sg
                acts.append(silu * vc)
                dgs.append(vc * (sg * (1.0 + gc * (1.0 - sg))))
                dvs.append(silu)
            rows = slice(r0, r0 + FFN_ROWS)
            act_ref[rows, :] = jnp.concatenate(acts, axis=0).astype(BF16)
            dg_ref[rows, :] = jnp.concatenate(dgs, axis=0).astype(BF16)
            dv_ref[rows, :] = jnp.concatenate(dvs, axis=0).astype(BF16)

    prev = lambda off: (lambda i, j: (jnp.maximum(i * nb - 1, 0), j + off))
    out = pl.BlockSpec((ts, tc), lambda i, j: (i, j))
    return _call(body, grid=(S // ts, nj),
                 in_specs=[pl.BlockSpec((ts, tc), lambda i, j: (i, j)), pl.BlockSpec((ts, tc), lambda i, j: (i, j + nj)),
                           pl.BlockSpec((HB, tc), prev(0)), pl.BlockSpec((HB, tc), prev(nj)),
                           pl.BlockSpec((3, tc), lambda i, j: (0, j)), pl.BlockSpec((3, tc), lambda i, j: (0, j + nj)),
                           pl.BlockSpec((1, tc), lambda i, j: (0, j)), pl.BlockSpec((1, tc), lambda i, j: (0, j + nj))],
                 out_specs=(out, out, out), out_shape=(_sds((S, F), BF16),) * 3, name=name)(
        hu, hu, hu, hu, cw, cw, cb.reshape(1, F2), cb.reshape(1, F2))


def _ffn_act_bwd(hu, dact, dact_dg, dact_dv, cw, *, name, ts, tc):
    S, F2 = hu.shape
    F = F2 // 2
    nj = F // tc
    HB = FFN_HALO
    HB16 = 16
    n = S // ts

    def body(g_ref, v_ref, lg_ref, lv_ref, lgn_ref, lvn_ref, da_ref, dan_ref, wg_ref, wv_ref,
             dhg_ref, dhv_ref, dwg_ref, dwv_ref, dbg_ref, dbv_ref, dug_ref, duv_ref, accg, accv):
        i = pl.program_id(1)
        RC = FFN_ROWS
        npairs = ts // RC
        last_tile = i == n - 1

        @pl.when(i == 0)
        def _():
            accg[...] = jnp.zeros_like(accg)
            accv[...] = jnp.zeros_like(accv)

        for l_ref, ln_ref, h_ref, w_ref, dh_ref, acc in ((lg_ref, lgn_ref, g_ref, wg_ref, dhg_ref, accg),
                                                          (lv_ref, lvn_ref, v_ref, wv_ref, dhv_ref, accv)):
            for c0 in range(0, tc, FFN_STRIP):
                cols = slice(c0, min(c0 + FFN_STRIP, tc))
                sw = cols.stop - cols.start
                row = lax.broadcasted_iota(jnp.int32, (HB, sw), 0)
                w0, w1, w2 = w_ref[0:1, cols], w_ref[1:2, cols], w_ref[2:3, cols]

                def d_rows(p):
                    rows = slice(p * RC, (p + 1) * RC)
                    return da_ref[rows, cols] * l_ref[rows, cols].astype(F32)

                after = jnp.where(last_tile, 0.0, dan_ref[:, cols]) * ln_ref[:, cols].astype(F32)[0:HB, :]
                pair = d_rows(0)
                cur = pair[0:HB, :]
                c7, c6 = pltpu.roll(cur, HB - 1, 0), pltpu.roll(cur, HB - 2, 0)
                sums = [None] * 5
                for p in range(npairs):
                    nxt_pair = d_rows(p + 1) if p + 1 < npairs else None
                    dhus = []
                    for half, nxt_c in enumerate((pair[HB:RC, :], after if nxt_pair is None else nxt_pair[0:HB, :])):
                        c = 2 * p + half
                        n7, n6 = pltpu.roll(nxt_c, HB - 1, 0), pltpu.roll(nxt_c, HB - 2, 0)
                        taps = [jnp.where(row >= HB - 2, n6, c6), jnp.where(row >= HB - 1, n7, c7), cur]
                        dhu = w2 * taps[2] + w1 * taps[1] + w0 * taps[0]
                        dhus.append(dhu)
                        h = h_ref[c * HB:(c + 1) * HB, cols]
                        parts = [h * taps[0], h * taps[1], h * taps[2], taps[2], dhu]
                        sums = [q if s is None else s + q for s, q in zip(sums, parts)]
                        cur, c7, c6 = nxt_c, n7, n6
                    dh_ref[p * RC:(p + 1) * RC, cols] = jnp.concatenate(dhus, axis=0).astype(BF16)
                    pair = nxt_pair
                for k in range(5):
                    acc[8 * k:8 * k + 8, cols] += sums[k]

        @pl.when(i == n - 1)
        def _():
            for acc, dw_ref, db_ref, du_ref in ((accg, dwg_ref, dbg_ref, dug_ref), (accv, dwv_ref, dbv_ref, duv_ref)):
                for k in range(3):
                    dw_ref[k:k + 1, :] = _colsum(acc[8 * k:8 * k + 8, :])
                db_ref[...] = _colsum(acc[24:32, :])
                du_ref[...] = _colsum(acc[32:40, :])

    nxt = lambda hb: (lambda j, i: (jnp.minimum((i + 1) * (ts // hb), S // hb - 1), j))
    tile = lambda off: (lambda j, i: (i, j + off))
    vec = lambda rows: pl.BlockSpec((rows, tc), lambda j, i: (0, j))
    return _call(body, grid=(nj, n),
                 in_specs=[pl.BlockSpec((ts, tc), tile(0)), pl.BlockSpec((ts, tc), tile(nj)),
                           pl.BlockSpec((ts, tc), tile(0)), pl.BlockSpec((ts, tc), tile(0)),
                           pl.BlockSpec((HB16, tc), nxt(HB16)), pl.BlockSpec((HB16, tc), nxt(HB16)),
                           pl.BlockSpec((ts, tc), tile(0)), pl.BlockSpec((HB, tc), nxt(HB)),
                           pl.BlockSpec((3, tc), lambda j, i: (0, j)), pl.BlockSpec((3, tc), lambda j, i: (0, j + nj))],
                 out_specs=(pl.BlockSpec((ts, tc), tile(0)), pl.BlockSpec((ts, tc), tile(0)),
                            vec(3), vec(3), vec(1), vec(1), vec(1), vec(1)),
                 out_shape=(_sds((S, F), BF16), _sds((S, F), BF16), _sds((3, F), F32), _sds((3, F), F32),
                            _sds((1, F), F32), _sds((1, F), F32), _sds((1, F), F32), _sds((1, F), F32)),
                 scratch_shapes=[pltpu.VMEM((40, tc), F32), pltpu.VMEM((40, tc), F32)], name=name)(
        hu, hu, dact_dg, dact_dv, dact_dg, dact_dv, dact, dact, cw, cw)


def _t5_bucket(dist):
    max_exact = N_BUCKETS // 2
    d = np.maximum(dist, 1).astype(np.float64)
    large = max_exact + (np.log(d / max_exact) / math.log(MAX_DISTANCE / max_exact)
                         * (N_BUCKETS - max_exact)).astype(np.int32)
    large = np.minimum(large, N_BUCKETS - 1)
    return np.where(dist < max_exact, dist, large).astype(np.int32)


def _pattern_tables(window, dilation):
    qi = np.arange(ATTN_BLOCK)[:, None]
    kj = np.arange(2 * ATTN_BLOCK)[None, :]
    dist = qi + ATTN_BLOCK - kj
    valid = (dist >= 0) & (dist <= window // dilation)
    bucket = _t5_bucket(np.clip(dist, 0, None) * dilation)
    return bucket, valid


def _dilate_qkv(qkv, *, name, ts):
    S, C = qkv.shape
    dils = [d for (_, d) in PATTERNS if d > 1]

    def body(x_ref, nat_ref, *rest):
        outs, tmp = rest[:-1], rest[-1]
        nat_ref[...] = x_ref[...].astype(BF16)
        for j in range(C // LANES):
            cols = slice(j * LANES, (j + 1) * LANES)
            tmp[...] = x_ref[:, cols]
            for d, o_ref in zip(dils, outs):
                _dilate(tmp, o_ref, cols, d, ts, BF16)

    out_shape = (_sds((S, C), BF16),) + tuple(_sds((d, S // d, C), BF16) for d in dils)
    out_specs = (_dil_spec(1, ts, C),) + tuple(_dil_spec(d, ts, C) for d in dils)
    res = _call(body, grid=(S // ts,), in_specs=[_dil_spec(1, ts, C)], out_specs=out_specs, out_shape=out_shape,
                scratch_shapes=[pltpu.VMEM((ts, LANES), F32)], name=name)(qkv)
    return [res[0]] + [r.reshape(S, C) for r in res[1:]]


def _attn_group(S):
    nb_min = (S // PATTERNS[-1][1]) // ATTN_BLOCK
    return math.gcd(8, nb_min)


def _attn_fwd(qkv, bias, *, name, nb, G):
    S = qkv.shape[0]
    B, HD = ATTN_BLOCK, HEAD_DIM
    GR = G * B
    ng = S // GR

    def body(q_ref, k_ref, kh_ref, v_ref, vh_ref, bias_ref, o_ref, lse_ref, kbuf, vbuf):
        g = pl.program_id(1)
        halo_ok = (g * G) % nb != 0
        kbuf[0:B, :] = kh_ref[...]
        kbuf[B:B + GR, :] = k_ref[...]
        vbuf[0:B, :] = vh_ref[...]
        vbuf[B:B + GR, :] = v_ref[...]
        col = lax.broadcasted_iota(jnp.int32, (B, 2 * B), 1)
        head0 = lax.broadcasted_iota(jnp.int32, (B, LANES), 1) < HD

        for bi in range(G):
            r0 = bi * B
            q2 = q_ref[r0:r0 + B, :]
            kk = kbuf[r0:r0 + 2 * B, :]
            vv = vbuf[r0:r0 + 2 * B, :]
            zero = jnp.zeros_like(q2)
            os_, ls_, lses = [], [], []
            for hh in range(2):
                qh = jnp.where(head0, q2, zero) if hh == 0 else jnp.where(head0, zero, q2)
                s = lax.dot_general(qh, kk, (((1,), (1,)), ((), ())), preferred_element_type=F32)
                s = s + bias_ref[hh]
                if bi == 0:
                    s = jnp.where(jnp.logical_and(jnp.logical_not(halo_ok), col < B), NEG, s)
                m = jnp.max(s, axis=1, keepdims=True)
                p = jnp.exp(s - m)
                l = jnp.sum(p, axis=1, keepdims=True)
                os_.append(jnp.dot(p.astype(BF16), vv, preferred_element_type=F32))
                ls_.append(l)
                lses.append(m + jnp.log(l))
            o_ref[r0:r0 + B, :] = (jnp.where(head0, os_[0], os_[1]) / jnp.where(head0, ls_[0], ls_[1])).astype(BF16)
            lse_ref[r0:r0 + B, :] = jnp.where(head0, lses[0], lses[1])

    halo = lambda off: (lambda hp, g: (jnp.maximum(g * G - 1, 0), off + hp))
    main = lambda off: (lambda hp, g: (g, off + hp))
    return _call(body, grid=(4, ng),
                 in_specs=[pl.BlockSpec((GR, LANES), main(0)), pl.BlockSpec((GR, LANES), main(4)),
                           pl.BlockSpec((B, LANES), halo(4)), pl.BlockSpec((GR, LANES), main(8)),
                           pl.BlockSpec((B, LANES), halo(8)), pl.BlockSpec((2, B, 2 * B), lambda hp, g: (hp, 0, 0))],
                 out_specs=(pl.BlockSpec((GR, LANES), main(0)), pl.BlockSpec((GR, LANES), main(0))),
                 out_shape=(_sds((S, ATTN_CH), BF16), _sds((S, ATTN_CH), F32)),
                 scratch_shapes=[pltpu.VMEM((B + GR, LANES), BF16), pltpu.VMEM((B + GR, LANES), BF16)], name=name)(
        qkv, qkv, qkv, qkv, qkv, bias)


def _dil_spec(d, ts, C):
    if d == 1:
        return pl.BlockSpec((ts, C), lambda i: (i, 0))
    return pl.BlockSpec((d, ts // d, C), lambda i: (0, i, 0))


def _dil_view(a, d):
    return a if d == 1 else a.reshape(d, a.shape[0] // d, a.shape[1])


def _dilate(nat_tmp, dst_ref, cols, d, ts, dtype=F32):
    for r in range(d):
        dst_ref[r, :, cols] = nat_tmp[pl.ds(r, ts // d, stride=d), :].astype(dtype)


def _undilate(src_ref, nat_tmp, cols, d, ts, accumulate=False):
    for r in range(d):
        rows = pl.ds(r, ts // d, stride=d)
        if accumulate:
            nat_tmp[rows, :] = nat_tmp[rows, :] + src_ref[r, :, cols].astype(F32)
        else:
            nat_tmp[rows, :] = src_ref[r, :, cols].astype(F32)


def _attn_merge(o_list, lse_list, dils, *, name, ts):
    S, C = o_list[0].shape
    P = len(o_list)
    nd = sum(1 for d in dils if d > 1)

    def body(*refs):
        o_refs, l_refs = refs[:P], refs[P:2 * P]
        out_ref, lse_ref = refs[2 * P], refs[2 * P + 1]
        lse_d_refs = refs[2 * P + 2:2 * P + 2 + nd]
        scratch = list(refs[2 * P + 2 + nd:])
        tmp = scratch.pop()
        for j in range(C // LANES):
            cols = slice(j * LANES, (j + 1) * LANES)
            os_, ls, free = [], [], list(scratch)
            for o_ref, l_ref, d in zip(o_refs, l_refs, dils):
                if d > 1:
                    so, sl = free.pop(0), free.pop(0)
                    _undilate(o_ref, so, cols, d, ts)
                    _undilate(l_ref, sl, cols, d, ts)
                    os_.append(so[...])
                    ls.append(sl[...])
                else:
                    os_.append(o_ref[:, cols].astype(F32))
                    ls.append(l_ref[:, cols])
            m = ls[0]
            for l in ls[1:]:
                m = jnp.maximum(m, l)
            ws = [jnp.exp(l - m) for l in ls]
            den = ws[0]
            for w in ws[1:]:
                den = den + w
            num = ws[0] * os_[0]
            for w, o in zip(ws[1:], os_[1:]):
                num = num + w * o
            out_ref[:, cols] = num / den
            tmp[...] = m + jnp.log(den)
            lse_ref[:, cols] = tmp[...]
            for l_out, d in zip(lse_d_refs, [d for d in dils if d > 1]):
                _dilate(tmp, l_out, cols, d, ts)

    row = _dil_spec(1, ts, C)
    dd = [d for d in dils if d > 1]
    res = _call(body, grid=(S // ts,), in_specs=[_dil_spec(d, ts, C) for d in dils] * 2,
                out_specs=(row, row) + tuple(_dil_spec(d, ts, C) for d in dd),
                out_shape=(_sds((S, C), F32), _sds((S, C), F32)) + tuple(_sds((d, S // d, C), F32) for d in dd),
                scratch_shapes=[pltpu.VMEM((ts, LANES), F32)] * (2 * nd + 1), name=name)(
        *[_dil_view(o, d) for o, d in zip(o_list, dils)], *[_dil_view(l, d) for l, d in zip(lse_list, dils)])
    lse_by_d = {1: res[1]}
    lse_by_d.update({d: r.reshape(S, C) for d, r in zip(dd, res[2:])})
    return res[0], [lse_by_d[d] for d in dils]


def _attn_prep(dout, out, dils, *, name, ts):
    S, C = out.shape
    HD = HEAD_DIM
    dd = [d for d in dils if d > 1]
    nd = len(dd)

    def body(do_ref, o_ref, d_ref, dob_ref, *rest):
        outs, tmp_d, tmp_o = rest[:2 * nd], rest[2 * nd], rest[2 * nd + 1]
        dob_ref[...] = do_ref[...].astype(BF16)
        for j in range(C // LANES):
            cols = slice(j * LANES, (j + 1) * LANES)
            do = do_ref[:, cols]
            prod = do * o_ref[:, cols]
            tmp_o[...] = do
            for h in range(LANES // HD):
                cs = slice(h * HD, (h + 1) * HD)
                tmp_d[:, cs] = jnp.broadcast_to(jnp.sum(prod[:, cs], axis=1, keepdims=True), (ts, HD))
            d_ref[:, cols] = tmp_d[...]
            for d, dd_out, do_out in zip(dd, outs[:nd], outs[nd:]):
                _dilate(tmp_d, dd_out, cols, d, ts)
                _dilate(tmp_o, do_out, cols, d, ts, BF16)

    row = _dil_spec(1, ts, C)
    res = _call(body, grid=(S // ts,), in_specs=[row, row],
                out_specs=(row, row) + tuple(_dil_spec(d, ts, C) for d in dd) * 2,
                out_shape=(_sds((S, C), F32), _sds((S, C), BF16)) + tuple(_sds((d, S // d, C), F32) for d in dd)
                + tuple(_sds((d, S // d, C), BF16) for d in dd),
                scratch_shapes=[pltpu.VMEM((ts, LANES), F32)] * 2, name=name)(dout, out)
    dd_by_d, do_by_d = {1: res[0]}, {1: res[1]}
    dd_by_d.update({d: r.reshape(S, C) for d, r in zip(dd, res[2:2 + nd])})
    do_by_d.update({d: r.reshape(S, C) for d, r in zip(dd, res[2 + nd:])})
    return [dd_by_d[d] for d in dils], [do_by_d[d] for d in dils]


def _attn_bwd(qkv, do, lse, dd, bias, *, name, nb, G):
    S = qkv.shape[0]
    B, HD = ATTN_BLOCK, HEAD_DIM
    GR = G * B
    ng = S // GR
    nblk = S // B

    def body(q_ref, k_ref, kh_ref, v_ref, vh_ref, do_ref, lse_ref, dd_ref, qn_ref, don_ref, lsen_ref, ddn_ref,
             bias_ref, dq_ref, dk_ref, dv_ref, dbias_ref, kbuf, vbuf, dkp, dvp, dko_s, dvo_s):
        g = pl.program_id(1)
        halo_ok = (g * G) % nb != 0
        next_ok = jnp.logical_and(((g + 1) * G) % nb != 0, g < ng - 1)
        kbuf[0:B, :] = kh_ref[...]
        kbuf[B:B + GR, :] = k_ref[...]
        vbuf[0:B, :] = vh_ref[...]
        vbuf[B:B + GR, :] = v_ref[...]
        col = lax.broadcasted_iota(jnp.int32, (B, 2 * B), 1)

        def tn_dot(a, b):
            return lax.dot_general(a, b, (((0,), (0,)), ((), ())), preferred_element_type=F32)

        @pl.when(g == 0)
        def _():
            dbias_ref[...] = jnp.zeros_like(dbias_ref)

        head0 = lax.broadcasted_iota(jnp.int32, (B, LANES), 1) < HD

        def nt_dot(a, b):
            return lax.dot_general(a, b, (((1,), (1,)), ((), ())), preferred_element_type=F32)

        def one_head(x, hh):
            zero = jnp.zeros_like(x)
            return jnp.where(head0, x, zero) if hh == 0 else jnp.where(head0, zero, x)

        def pick(per_head):
            return jnp.where(head0, per_head[0], per_head[1])

        for bi in range(G):
            r0 = bi * B
            q2 = q_ref[r0:r0 + B, :]
            do2 = do_ref[r0:r0 + B, :]
            kk = kbuf[r0:r0 + 2 * B, :]
            vv = vbuf[r0:r0 + 2 * B, :]
            dq, dko, dvo, dkq, dvq = [], [], [], [], []
            for hh in range(2):
                s = nt_dot(one_head(q2, hh), kk) + bias_ref[hh]
                if bi == 0:
                    s = jnp.where(jnp.logical_and(jnp.logical_not(halo_ok), col < B), NEG, s)
                p = jnp.exp(s - lse_ref[r0:r0 + B, hh * HD:hh * HD + 1])
                dp = nt_dot(one_head(do2, hh), vv)
                ds = p * (dp - dd_ref[r0:r0 + B, hh * HD:hh * HD + 1])
                dbias_ref[hh] += ds
                ds_bf, p_bf = ds.astype(BF16), p.astype(BF16)
                dq.append(jnp.dot(ds_bf, kk, preferred_element_type=F32))
                dko.append(tn_dot(ds_bf[:, B:], q2))
                dvo.append(tn_dot(p_bf[:, B:], do2))
                if bi > 0:
                    dkq.append(tn_dot(ds_bf[:, :B], q2))
                    dvq.append(tn_dot(p_bf[:, :B], do2))
            dq_ref[r0:r0 + B, :] = pick(dq).astype(BF16)
            dko_s[r0:r0 + B, :] = pick(dko)
            dvo_s[r0:r0 + B, :] = pick(dvo)
            if bi > 0:
                dkp[r0 - B:r0, :] = pick(dkq)
                dvp[r0 - B:r0, :] = pick(dvq)

        @pl.when(next_ok)
        def _():
            qn = qn_ref[...]
            don = don_ref[...]
            kl = kbuf[GR:GR + B, :]
            vl = vbuf[GR:GR + B, :]
            dkq, dvq = [], []
            for hh in range(2):
                s = nt_dot(one_head(qn, hh), kl) + bias_ref[hh, :, 0:B]
                p = jnp.exp(s - lsen_ref[:, hh * HD:hh * HD + 1])
                dp = nt_dot(one_head(don, hh), vl)
                ds = p * (dp - ddn_ref[:, hh * HD:hh * HD + 1])
                dkq.append(tn_dot(ds.astype(BF16), qn))
                dvq.append(tn_dot(p.astype(BF16), don))
            dkp[GR - B:GR, :] = pick(dkq)
            dvp[GR - B:GR, :] = pick(dvq)

        @pl.when(jnp.logical_not(next_ok))
        def _():
            dkp[GR - B:GR, :] = jnp.zeros((B, LANES), F32)
            dvp[GR - B:GR, :] = jnp.zeros((B, LANES), F32)

        dk_ref[...] = (dko_s[...] + dkp[...]).astype(BF16)
        dv_ref[...] = (dvo_s[...] + dvp[...]).astype(BF16)

    halo = lambda off: (lambda hp, g: (jnp.maximum(g * G - 1, 0), off + hp))
    main = lambda off: (lambda hp, g: (g, off + hp))
    nxt = lambda off: (lambda hp, g: (jnp.minimum((g + 1) * G, nblk - 1), off + hp))
    big, small = (lambda m: pl.BlockSpec((GR, LANES), m)), (lambda m: pl.BlockSpec((B, LANES), m))
    return _call(body, grid=(4, ng),
                 in_specs=[big(main(0)), big(main(4)), small(halo(4)), big(main(8)), small(halo(8)),
                           big(main(0)), big(main(0)), big(main(0)),
                           small(nxt(0)), small(nxt(0)), small(nxt(0)), small(nxt(0)),
                           pl.BlockSpec((2, B, 2 * B), lambda hp, g: (hp, 0, 0))],
                 out_specs=(big(main(0)), big(main(0)), big(main(0)),
                            pl.BlockSpec((2, B, 2 * B), lambda hp, g: (hp, 0, 0))),
                 out_shape=(_sds((S, ATTN_CH), BF16),) * 3 + (_sds((ATTN_HEADS, B, 2 * B), F32),),
                 scratch_shapes=[pltpu.VMEM((B + GR, LANES), BF16), pltpu.VMEM((B + GR, LANES), BF16)]
                 + [pltpu.VMEM((GR, LANES), F32)] * 4, name=name)(
        qkv, qkv, qkv, qkv, qkv, do, lse, dd, qkv, do, lse, dd, bias)


def _attn_combine(dq_list, dk_list, dv_list, dils, *, name, ts):
    S, C = dq_list[0].shape
    P = len(dq_list)
    scale = HEAD_DIM ** -0.5
    assert dils[0] == 1

    def body(*refs):
        out_refs, acc = refs[3 * P:3 * P + 3], refs[3 * P + 3]
        for part in range(3):
            for j in range(C // LANES):
                cols = slice(j * LANES, (j + 1) * LANES)
                acc[...] = refs[part * P][:, cols].astype(F32)
                for r, d in zip(refs[part * P + 1:(part + 1) * P], dils[1:]):
                    _undilate(r, acc, cols, d, ts, accumulate=True)
                tot = acc[...]
                if part == 0:
                    tot = tot * scale
                out_refs[part][:, cols] = tot.astype(BF16)

    return _call(body, grid=(S // ts,), in_specs=[_dil_spec(d, ts, C) for d in dils] * 3,
                 out_specs=(_dil_spec(1, ts, C),) * 3, out_shape=(_sds((S, C), BF16),) * 3,
                 scratch_shapes=[pltpu.VMEM((ts, LANES), F32)], name=name)(
        *[_dil_view(a, d) for lst in (dq_list, dk_list, dv_list) for a, d in zip(lst, dils)])


def _bias_tables(table, bucket_flat, *, name):
    P, _, K = bucket_flat.shape
    H = table.shape[1]
    KC = 4096

    def body(t_ref, bk_ref, out_ref):
        row = lax.broadcasted_iota(jnp.int32, (N_BUCKETS, KC), 0)
        for c in range(K // KC):
            bk = bk_ref[0, :, c * KC:(c + 1) * KC]
            onehot = (row == bk).astype(F32)
            vals = jnp.dot(t_ref[...], onehot, preferred_element_type=F32, precision=lax.Precision.HIGHEST)
            out_ref[0, :, c * KC:(c + 1) * KC] = jnp.where(bk >= 0, vals, NEG)

    return _call(body, grid=(P,),
                 in_specs=[pl.BlockSpec((H, N_BUCKETS), lambda p: (0, 0)), pl.BlockSpec((1, 1, K), lambda p: (p, 0, 0))],
                 out_specs=pl.BlockSpec((1, H, K), lambda p: (p, 0, 0)), out_shape=_sds((P, H, K), F32),
                 name=name)(table.T, bucket_flat)


def _bias_grad(dbias_flat, bucket_flat, *, name):
    P, H, K = dbias_flat.shape
    KC = 4096

    def body(db_ref, bk_ref, out_ref):
        p = pl.program_id(0)
        acc = jnp.zeros((N_BUCKETS, H), F32)
        row = lax.broadcasted_iota(jnp.int32, (N_BUCKETS, KC), 0)
        for c in range(K // KC):
            onehot = (row == bk_ref[0, :, c * KC:(c + 1) * KC]).astype(F32)
            acc = acc + lax.dot_general(onehot, db_ref[0, :, c * KC:(c + 1) * KC], (((1,), (1,)), ((), ())),
                                        preferred_element_type=F32, precision=lax.Precision.HIGHEST)

        @pl.when(p == 0)
        def _():
            out_ref[...] = acc

        @pl.when(p > 0)
        def _():
            out_ref[...] += acc

    return _call(body, grid=(P,),
                 in_specs=[pl.BlockSpec((1, H, K), lambda p: (p, 0, 0)), pl.BlockSpec((1, 1, K), lambda p: (p, 0, 0))],
                 out_specs=pl.BlockSpec((N_BUCKETS, H), lambda p: (0, 0)), out_shape=_sds((N_BUCKETS, H), F32),
                 name=name)(dbias_flat, bucket_flat)


def _allgather(blocks, *, name):
    n = len(blocks)

    def body(*refs):
        x_refs, out_refs = refs[:n], refs[n:2 * n]
        send_sems, recv_sems, local_sems = refs[2 * n:]
        x, y, c = lax.axis_index("x"), lax.axis_index("y"), lax.axis_index("c")
        me, sibling = (x, y, c), (x, y, 1 - c)
        chips = [(1 - x, y), (x, 1 - y), (1 - x, 1 - y)]

        def copy(i, k, blk, to, own=False):
            slot = out_refs[i].at[4 * blk[0] + 2 * blk[1] + blk[2]]
            return pltpu.make_async_remote_copy(src_ref=x_refs[i] if own else slot, dst_ref=slot,
                                                send_sem=send_sems.at[7 * i + k], recv_sem=recv_sems.at[7 * i + k],
                                                device_id=to, device_id_type=MESH)

        mine = [pltpu.make_async_copy(x_refs[i], out_refs[i].at[4 * x + 2 * y + c], local_sems.at[i])
                for i in range(n)]
        for cp in mine:
            cp.start()
        first = []
        for i in range(n):
            first.append(copy(i, 0, me, sibling, own=True))
            first += [copy(i, 1 + j, me, (*chip, c), own=True) for j, chip in enumerate(chips)]
        for cp in first:
            cp.start()
        passed = []
        for j, chip in enumerate(chips):
            for i in range(n):
                copy(i, 1 + j, (*chip, c), me).wait_recv()
                fwd = copy(i, 4 + j, (*chip, c), sibling)
                fwd.start()
                passed.append(fwd)
        for i in range(n):
            copy(i, 0, sibling, me).wait_recv()
            for j, chip in enumerate(chips):
                copy(i, 4 + j, (*chip, 1 - c), me).wait_recv()
        for cp in first + passed:
            cp.wait_send()
        for cp in mine:
            cp.wait()

    any_spec = pl.BlockSpec(memory_space=pl.ANY)
    return pl.pallas_call(body, out_shape=tuple(_sds((N_DEV,) + b.shape, b.dtype) for b in blocks),
                          in_specs=[any_spec] * n, out_specs=(any_spec,) * n,
                          scratch_shapes=[pltpu.SemaphoreType.DMA((7 * n,)), pltpu.SemaphoreType.DMA((7 * n,)),
                                          pltpu.SemaphoreType.DMA((n,))], name=name)(*blocks)


def _exchange(sends, *, name):
    n = len(sends)

    def body(*refs):
        send_refs, recv_refs = refs[:n], refs[n:2 * n]
        send_sems, recv_sems, local_sems = refs[2 * n:]
        x, y, c = lax.axis_index("x"), lax.axis_index("y"), lax.axis_index("c")
        me = 4 * x + 2 * y + c
        mine = [pltpu.make_async_copy(send_refs[i].at[me], recv_refs[i].at[me], local_sems.at[i]) for i in range(n)]
        for cp in mine:
            cp.start()
        copies = []
        for k in range(1, N_DEV):
            px = 1 - x if k & 4 else x
            py = 1 - y if k & 2 else y
            pc = 1 - c if k & 1 else c
            for i in range(n):
                cp = pltpu.make_async_remote_copy(src_ref=send_refs[i].at[4 * px + 2 * py + pc],
                                                  dst_ref=recv_refs[i].at[me],
                                                  send_sem=send_sems.at[7 * i + k - 1],
                                                  recv_sem=recv_sems.at[7 * i + k - 1],
                                                  device_id=(px, py, pc), device_id_type=MESH)
                cp.start()
                copies.append(cp)
        for cp in copies:
            cp.wait_recv()
        for cp in copies:
            cp.wait_send()
        for cp in mine:
            cp.wait()

    any_spec = pl.BlockSpec(memory_space=pl.ANY)
    return pl.pallas_call(body, out_shape=tuple(_sds(s.shape, s.dtype) for s in sends), in_specs=[any_spec] * n,
                          out_specs=(any_spec,) * n,
                          scratch_shapes=[pltpu.SemaphoreType.DMA((7 * n,)), pltpu.SemaphoreType.DMA((7 * n,)),
                                          pltpu.SemaphoreType.DMA((n,))], name=name)(*sends)


def _adamw(w, m, v, g_parts, *, name, tr):
    R, W = w.shape
    bc1 = 1.0 - ADAM_B1 ** ADAM_STEP
    bc2 = 1.0 - ADAM_B2 ** ADAM_STEP

    def body(w_ref, m_ref, v_ref, g_ref, go_ref, d_ref, mo_ref, vo_ref):
        g = g_ref[0].astype(F32)
        for i in range(1, N_DEV):
            g = g + g_ref[i].astype(F32)
        mn = ADAM_B1 * m_ref[...] + (1.0 - ADAM_B1) * g
        vn = ADAM_B2 * v_ref[...] + (1.0 - ADAM_B2) * (g * g)
        m_hat = mn / bc1
        v_hat = vn / bc2
        go_ref[...] = g
        d_ref[...] = -ADAM_LR * (m_hat / (jnp.sqrt(v_hat) + ADAM_EPS) + ADAM_WD * w_ref[...])
        mo_ref[...] = mn
        vo_ref[...] = vn

    row = pl.BlockSpec((tr, W), lambda i: (i, 0))
    return _call(body, grid=(R // tr,), in_specs=[row, row, row, pl.BlockSpec((N_DEV, tr, W), lambda i: (0, i, 0))],
                 out_specs=(row,) * 4, out_shape=(_sds((R, W), F32),) * 4, name=name)(w, m, v, g_parts)


def _round_up(n, k):
    return -(-n // k) * k


def _pack(arrs, width, row_mult):
    pieces, offs, r = [], [], 0
    for a in arrs:
        n = a.size
        rows = _round_up(n, width) // width
        flat = a.reshape(-1)
        if rows * width != n:
            flat = jnp.pad(flat, (0, rows * width - n))
        pieces.append(flat.reshape(rows, width))
        offs.append((r, rows, n))
        r += rows
    total = _round_up(r, row_mult)
    if total != r:
        pieces.append(jnp.zeros((total - r, width), pieces[0].dtype))
    return jnp.concatenate(pieces, axis=0), offs


def _unpack(pack, offs, shapes):
    out = []
    for (r, rows, n), shp in zip(offs, shapes):
        out.append(pack[r:r + rows].reshape(-1)[:n].reshape(shp))
    return out


def _gather_axis(full8, axis):
    moved = jnp.moveaxis(full8, 0, axis)
    shp = list(moved.shape)
    shp[axis:axis + 2] = [shp[axis] * shp[axis + 1]]
    return moved.reshape(shp)


def _split_axis(full, axis):
    shp = list(full.shape)
    shp[axis:axis + 1] = [N_DEV, shp[axis] // N_DEV]
    return jnp.moveaxis(full.reshape(shp), axis, 0)


def kernel(x, w_in, b_in, conv_dw_w, conv_dw_b, conv_ln_g, conv_ln_b, rel_bias_table, gmlp_ln_g, gmlp_ln_b, gmlp_w_s, gmlp_b_s, w_out, b_out, ln1_g, ln1_b, ffn_w_up, ffn_b_up, ffn_conv_w, ffn_conv_b, ffn_w_down, ffn_b_down, ln2_g, ln2_b, loss_target, m_w_in, m_b_in, m_conv_dw_w, m_conv_dw_b, m_conv_ln_g, m_conv_ln_b, m_rel_bias_table, m_gmlp_ln_g, m_gmlp_ln_b, m_gmlp_w_s, m_gmlp_b_s, m_w_out, m_b_out, m_ln1_g, m_ln1_b, m_ffn_w_up, m_ffn_b_up, m_ffn_conv_w, m_ffn_conv_b, m_ffn_w_down, m_ffn_b_down, m_ln2_g, m_ln2_b, v_w_in, v_b_in, v_conv_dw_w, v_conv_dw_b, v_conv_ln_g, v_conv_ln_b, v_rel_bias_table, v_gmlp_ln_g, v_gmlp_ln_b, v_gmlp_w_s, v_gmlp_b_s, v_w_out, v_b_out, v_ln1_g, v_ln1_b, v_ffn_w_up, v_ffn_b_up, v_ffn_conv_w, v_ffn_conv_b, v_ffn_w_down, v_ffn_b_down, v_ln2_g, v_ln2_b):
    W = dict(w_in=w_in, b_in=b_in, conv_dw_w=conv_dw_w, conv_dw_b=conv_dw_b, conv_ln_g=conv_ln_g,
             conv_ln_b=conv_ln_b, rel_bias_table=rel_bias_table, gmlp_ln_g=gmlp_ln_g, gmlp_ln_b=gmlp_ln_b,
             gmlp_w_s=gmlp_w_s, gmlp_b_s=gmlp_b_s, w_out=w_out, b_out=b_out, ln1_g=ln1_g, ln1_b=ln1_b,
             ffn_w_up=ffn_w_up, ffn_b_up=ffn_b_up, ffn_conv_w=ffn_conv_w, ffn_conv_b=ffn_conv_b,
             ffn_w_down=ffn_w_down, ffn_b_down=ffn_b_down, ln2_g=ln2_g, ln2_b=ln2_b)
    Mo = dict(w_in=m_w_in, b_in=m_b_in, conv_dw_w=m_conv_dw_w, conv_dw_b=m_conv_dw_b, conv_ln_g=m_conv_ln_g,
              conv_ln_b=m_conv_ln_b, rel_bias_table=m_rel_bias_table, gmlp_ln_g=m_gmlp_ln_g, gmlp_ln_b=m_gmlp_ln_b,
              gmlp_w_s=m_gmlp_w_s, gmlp_b_s=m_gmlp_b_s, w_out=m_w_out, b_out=m_b_out, ln1_g=m_ln1_g, ln1_b=m_ln1_b,
              ffn_w_up=m_ffn_w_up, ffn_b_up=m_ffn_b_up, ffn_conv_w=m_ffn_conv_w, ffn_conv_b=m_ffn_conv_b,
              ffn_w_down=m_ffn_w_down, ffn_b_down=m_ffn_b_down, ln2_g=m_ln2_g, ln2_b=m_ln2_b)
    Vo = dict(w_in=v_w_in, b_in=v_b_in, conv_dw_w=v_conv_dw_w, conv_dw_b=v_conv_dw_b, conv_ln_g=v_conv_ln_g,
              conv_ln_b=v_conv_ln_b, rel_bias_table=v_rel_bias_table, gmlp_ln_g=v_gmlp_ln_g, gmlp_ln_b=v_gmlp_ln_b,
              gmlp_w_s=v_gmlp_w_s, gmlp_b_s=v_gmlp_b_s, w_out=v_w_out, b_out=v_b_out, ln1_g=v_ln1_g, ln1_b=v_ln1_b,
              ffn_w_up=v_ffn_w_up, ffn_b_up=v_ffn_b_up, ffn_conv_w=v_ffn_conv_w, ffn_conv_b=v_ffn_conv_b,
              ffn_w_down=v_ffn_w_down, ffn_b_down=v_ffn_b_down, ln2_g=v_ln2_g, ln2_b=v_ln2_b)

    xs = x[0]
    target = loss_target[0]
    S, D = xs.shape
    F2 = ffn_b_up.shape[1]
    F = F2 // 2
    ts = min(512, S)
    G = _attn_group(S)
    tc = F // 2 if (F // 2) % LANES == 0 else F

    mat_names = SHARDED[:4]
    payload = [W[n].astype(BF16) if n in mat_names else W[n] for n in SHARDED]
    wall = _allgather(payload, name="weight_allgather")
    full = {n: _gather_axis(parts, SHARD_AXIS[n]) for n, parts in zip(SHARDED, wall)}

    tables = [_pattern_tables(w, d) for (w, d) in PATTERNS]
    bucket_flat = jnp.asarray(np.stack([np.where(v, b, -1).reshape(1, -1) for (b, v) in tables]).astype(np.int32))
    bias_all = _bias_tables(rel_bias_table, bucket_flat, name="bias_tables")
    biases = [bias_all[p].reshape(ATTN_HEADS, ATTN_BLOCK, 2 * ATTN_BLOCK) for p in range(len(PATTERNS))]
    nbs = [(S // d) // ATTN_BLOCK for (_, d) in PATTERNS]
    dils = [d for (_, d) in PATTERNS]
    scale = HEAD_DIM ** -0.5

    saved = []
    cur = xs
    for l in range(DEPTH):
        Win, Wout, Wup, Wdown = full['w_in'][l], full['w_out'][l], full['ffn_w_up'][l], full['ffn_w_down'][l]
        qcols = slice(2 * CONV_CH, 2 * CONV_CH + ATTN_CH)
        Win_s = Win.at[:, qcols].multiply(scale)
        b_in_s = b_in[l].at[qcols].multiply(scale)
        h_a, qkv, h_c = _mm([cur], [Win_s], bias=b_in_s, tm=ts, name="in_proj",
                            splits=((2 * CONV_CH, F32, 1.0), (3 * ATTN_CH, F32, 1.0), (2 * GMLP_CH, F32, 1.0)))
        conv_out, hc = _conv_fwd(h_a, full['conv_dw_w'][l], conv_dw_b[l], conv_ln_g[l], conv_ln_b[l],
                                 name="conv_fwd", ts=ts)
        qkv_d = _dilate_qkv(qkv, name="dilate_qkv", ts=ts)
        o_ps, lse_ps = [], []
        for p, d in enumerate(dils):
            o_p, lse_p = _attn_fwd(qkv_d[p], biases[p], name=f"attn_fwd_d{d}", nb=nbs[p], G=G)
            o_ps.append(o_p)
            lse_ps.append(lse_p)
        attn_out, lse = _attn_merge(o_ps, lse_ps, dils, name="attn_merge", ts=ts)
        w_tril = jnp.tril(gmlp_w_s[l]).astype(BF16)
        bs_rows = jnp.repeat(gmlp_b_s[l].T, GMLP_GROUP_DIM, axis=1)
        gm_out = _gmlp_fwd(h_c, gmlp_ln_g[l], gmlp_ln_b[l], w_tril, bs_rows, name="gmlp_fwd", ts=ts)
        x1, xhat1, rstd1 = _mm([conv_out, attn_out, gm_out],
                               [Wout[:CONV_CH], Wout[CONV_CH:CONV_CH + ATTN_CH], Wout[CONV_CH + ATTN_CH:]],
                               bias=b_out[l], resid=cur, resid_scale=ALPHA, ln=(ln1_g[l], ln1_b[l]), tm=ts,
                               name="out_proj_ln")
        hu = _mm([x1], [Wup], bias=ffn_b_up[l], tm=ts, tn=F, name="ffn_up")
        act, act_dg, act_dv = _ffn_act_fwd(hu, full['ffn_conv_w'][l], ffn_conv_b[l], name="ffn_act_fwd",
                                           ts=min(256, S), tc=tc)
        x2, xhat2, rstd2 = _mm([act], [Wdown], bias=ffn_b_down[l], resid=x1, resid_scale=ALPHA,
                               ln=(ln2_g[l], ln2_b[l]), tm=ts, name="ffn_down_ln")
        saved.append(dict(x0=cur, h_a=h_a, h_c=h_c, qkv_d=qkv_d, hc=hc, conv_out=conv_out, attn_out=attn_out,
                          lse=lse, gm_out=gm_out, w_tril=w_tril, bs_rows=bs_rows, x1=x1, xhat1=xhat1, rstd1=rstd1,
                          hu=hu, act=act, act_dg=act_dg, act_dv=act_dv, xhat2=xhat2, rstd2=rstd2))
        cur = x2

    grads = {n: [None] * DEPTH for n in WEIGHTS if n != 'rel_bias_table'}
    drel = None
    dx = None
    loss_part = None
    tk = min(1024, S)
    for l in reversed(range(DEPTH)):
        sv = saved[l]
        Win, Wout, Wup, Wdown = full['w_in'][l], full['w_out'][l], full['ffn_w_up'][l], full['ffn_w_down'][l]
        if dx is None:
            dz2, dg2, db2, dzs2, loss_part = _ln_bwd(sv['xhat2'], sv['rstd2'], ln2_g[l], b=ln2_b[l], target=target,
                                                     name="ln2_bwd_loss", ts=ts)
        else:
            dz2, dg2, db2, dzs2 = _ln_bwd(sv['xhat2'], sv['rstd2'], ln2_g[l], dy=dx, name="ln_bwd", ts=ts)
        grads['ln2_g'][l], grads['ln2_b'][l], grads['ffn_b_down'][l] = dg2[0], db2[0], dzs2[0]
        grads['ffn_w_down'][l] = _mm_tn(sv['act'], dz2, tm=F // 2 if (F // 2) % LANES == 0 else F, tn=D, tk=tk,
                                        name="dw_down")
        dact = _mm([dz2], [Wdown.T], tm=ts, name="dact")
        dhg, dhv, dwg, dwv, dbg, dbv, dug, duv = _ffn_act_bwd(sv['hu'], dact, sv['act_dg'], sv['act_dv'],
                                                              full['ffn_conv_w'][l], name="ffn_act_bwd",
                                                              ts=min(256, S), tc=tc)
        grads['ffn_conv_w'][l] = jnp.concatenate([dwg, dwv], axis=1)
        grads['ffn_conv_b'][l] = jnp.concatenate([dbg, dbv], axis=1)[0]
        grads['ffn_b_up'][l] = jnp.concatenate([dug, duv], axis=1)[0]
        grads['ffn_w_up'][l] = jnp.concatenate(
            [_mm_tn(sv['x1'], dhg, tm=D, tn=tc, tk=tk, name="dw_up"),
             _mm_tn(sv['x1'], dhv, tm=D, tn=tc, tk=tk, name="dw_up")], axis=1)
        WupT = Wup.T
        dx1 = _mm([dhg, dhv], [WupT[:F], WupT[F:]], resid=dz2, resid_scale=ALPHA, tm=ts, name="dx1")
        dz1, dg1, db1, dzs1 = _ln_bwd(sv['xhat1'], sv['rstd1'], ln1_g[l], dy=dx1, name="ln_bwd", ts=ts)
        grads['ln1_g'][l], grads['ln1_b'][l], grads['b_out'][l] = dg1[0], db1[0], dzs1[0]
        grads['w_out'][l] = jnp.concatenate(
            [_mm_tn(sv['conv_out'], dz1, tm=CONV_CH, tn=D, tk=tk, name="dw_out_conv"),
             _mm_tn(sv['attn_out'], dz1, tm=ATTN_CH, tn=D, tk=tk, name="dw_out_attn"),
             _mm_tn(sv['gm_out'], dz1, tm=GMLP_CH, tn=D, tk=tk, name="dw_out_conv")], axis=0)
        dc_conv, dc_attn, dc_gm = _mm([dz1], [Wout.T], tm=ts, name="dcat",
                                      splits=((CONV_CH, F32, 1.0), (ATTN_CH, F32, 1.0), (GMLP_CH, F32, 1.0)))
        dh_a, ddw, ddwb, dclg, dclb = _conv_bwd(sv['h_a'], sv['hc'], dc_conv, full['conv_dw_w'][l], conv_ln_g[l],
                                                conv_ln_b[l], name="conv_bwd", ts=ts)
        grads['conv_dw_w'][l], grads['conv_dw_b'][l] = ddw[:CONV_WIDTH], ddwb[0]
        grads['conv_ln_g'][l], grads['conv_ln_b'][l] = dclg[0], dclb[0]
        dd_d, do_d = _attn_prep(dc_attn, sv['attn_out'], dils, name="attn_prep", ts=ts)
        dqs, dks, dvs, dbs = [], [], [], []
        for p, d in enumerate(dils):
            dq, dk, dv, dbias = _attn_bwd(sv['qkv_d'][p], do_d[p], sv['lse'][p], dd_d[p], biases[p],
                                          name=f"attn_bwd_d{d}", nb=nbs[p], G=G)
            dqs.append(dq)
            dks.append(dk)
            dvs.append(dv)
            dbs.append(dbias.reshape(1, ATTN_HEADS, -1))
        dqkv = _attn_combine(dqs, dks, dvs, dils, name="attn_combine", ts=ts)
        dr = _bias_grad(jnp.concatenate(dbs, axis=0), bucket_flat, name="bias_grad")
        drel = dr if drel is None else drel + dr
        w_tril_t = jnp.swapaxes(sv['w_tril'], 1, 2)
        dh_c, dws, dbs_acc, dglg, dglb = _gmlp_bwd(sv['h_c'], dc_gm, gmlp_ln_g[l], gmlp_ln_b[l], sv['w_tril'],
                                                   w_tril_t, sv['bs_rows'], name="gmlp_bwd", ts=ts)
        grads['gmlp_w_s'][l] = jnp.tril(dws)
        grads['gmlp_b_s'][l] = dbs_acc[:, :GMLP_GROUPS].T
        grads['gmlp_ln_g'][l], grads['gmlp_ln_b'][l] = dglg[0], dglb[0]
        dw_in = _mm_tn_shared(sv['x0'], [dh_a, *dqkv, dh_c], tk=tk, name="dw_in")
        grads['w_in'][l] = jnp.concatenate([w for w, _ in dw_in], axis=1)
        grads['b_in'][l] = jnp.concatenate([c for _, c in dw_in], axis=1)[0]
        WinT = Win.T
        edges = [0, 2 * CONV_CH] + [2 * CONV_CH + k * ATTN_CH for k in (1, 2, 3)] + [WinT.shape[0]]
        dx = _mm([dh_a, *dqkv, dh_c], [WinT[a:b] for a, b in zip(edges[:-1], edges[1:])], resid=dz1,
                 resid_scale=ALPHA, tm=ts, name="dx0")

    gfull = {n: jnp.stack(v) for n, v in grads.items()}
    gfull['rel_bias_table'] = drel

    sends = []
    for n in SHARDED:
        parts = _split_axis(gfull[n], SHARD_AXIS[n])
        sends.append(parts.reshape(N_DEV, -1, parts.shape[-1]).astype(BF16))
    recvs = _exchange(sends, name="grad_exchange")
    shard_out = [[], [], [], []]
    for n, recv in zip(SHARDED, recvs):
        shp = W[n].shape
        rows = recv.shape[1]
        tr = rows // 4 if rows % 64 == 0 else rows
        outs = _adamw(*[src[n].reshape(rows, shp[-1]) for src in (W, Mo, Vo)], recv, name=f"adamw_{n}", tr=tr)
        for kind in range(4):
            shard_out[kind].append(outs[kind].reshape(shp))

    gsmall, soffs = _pack([gfull[n] for n in SMALL], LANES, 8)
    gall = _allgather([gsmall], name="small_grad_allgather")[0]
    spacks = [_pack([src[n] for n in SMALL], LANES, 8)[0] for src in (W, Mo, Vo)]
    souts = _adamw(spacks[0], spacks[1], spacks[2], gall, name="adamw_small", tr=gsmall.shape[0])
    small_out = [_unpack(o, soffs, [W[n].shape for n in SMALL]) for o in souts]

    loss = lax.psum(loss_part[0, 0], ("x", "y", "c"))
    by_kind = []
    for kind in range(4):
        d = dict(zip(SHARDED, shard_out[kind]))
        d.update(zip(SMALL, small_out[kind]))
        by_kind.append([d[n] for n in WEIGHTS])
    return (loss, dx[None], *by_kind[0], *by_kind[1], *by_kind[2], *by_kind[3])
```

```python
import math

import numpy as np
import jax
import jax.numpy as jnp
from jax import lax
from jax.experimental import pallas as pl
from jax.experimental.pallas import tpu as pltpu

F32 = jnp.float32
BF16 = jnp.bfloat16

DEPTH = 2
HEAD_DIM = 64
CONV_CH = 256
CONV_WIDTH = 31
ATTN_HEADS = 8
ATTN_CH = ATTN_HEADS * HEAD_DIM
PATTERNS = ((128, 1), (512, 4), (2048, 16))
ATTN_BLOCK = 128
N_BUCKETS = 32
MAX_DISTANCE = 2048
GMLP_CH = 256
GMLP_GROUPS = 4
GMLP_GROUP_DIM = GMLP_CH // GMLP_GROUPS
CHUNK = 128
FFN_CONV_WIDTH = 3
LN_EPS = 1e-5
ALPHA = (2.0 * DEPTH) ** 0.25
ADAM_LR = 0.001
ADAM_B1 = 0.9
ADAM_B2 = 0.999
ADAM_EPS = 1e-08
ADAM_WD = 0.01
ADAM_STEP = 10
NEG = -1e30
N_DEV = 8
LANES = 128
CONV_HALO = 32
FFN_HALO = 8
FFN_ROWS = 16
FFN_STRIP = 256
MESH = pl.DeviceIdType.MESH

WEIGHTS = ['w_in', 'b_in', 'conv_dw_w', 'conv_dw_b', 'conv_ln_g', 'conv_ln_b', 'rel_bias_table', 'gmlp_ln_g',
           'gmlp_ln_b', 'gmlp_w_s', 'gmlp_b_s', 'w_out', 'b_out', 'ln1_g', 'ln1_b', 'ffn_w_up', 'ffn_b_up',
           'ffn_conv_w', 'ffn_conv_b', 'ffn_w_down', 'ffn_b_down', 'ln2_g', 'ln2_b']
SHARDED = ['w_in', 'w_out', 'ffn_w_up', 'ffn_w_down', 'conv_dw_w', 'ffn_conv_w']
SHARD_AXIS = {'w_in': 2, 'w_out': 1, 'ffn_w_up': 2, 'ffn_w_down': 1, 'conv_dw_w': 2, 'ffn_conv_w': 2}
SMALL = [n for n in WEIGHTS if n not in SHARDED]


def _call(body, *, grid=(), vmem_mb=48, **kw):
    params = pltpu.CompilerParams(dimension_semantics=("arbitrary",) * len(grid), vmem_limit_bytes=vmem_mb << 20)
    return pl.pallas_call(body, grid=grid, compiler_params=params, **kw)


def _sds(shape, dtype):
    return jax.ShapeDtypeStruct(shape, dtype)


def _ln_rows(z):
    mu = jnp.mean(z, axis=-1, keepdims=True)
    zc = z - mu
    var = jnp.mean(zc * zc, axis=-1, keepdims=True)
    rstd = lax.rsqrt(var + LN_EPS)
    return zc * rstd, rstd


def _ln_bwd_rows(dxhat, xhat, rstd):
    m1 = jnp.mean(dxhat, axis=-1, keepdims=True)
    m2 = jnp.mean(dxhat * xhat, axis=-1, keepdims=True)
    return rstd * (dxhat - m1 - xhat * m2)


def _colsum(v):
    return jnp.sum(v, axis=0, keepdims=True)


def _mm(a_list, w_list, *, name, tm, tn=None, bias=None, resid=None, resid_scale=1.0, ln=None, splits=None,
        out_dtype=F32):
    na = len(a_list)
    M = a_list[0].shape[0]
    N = w_list[0].shape[1]
    tn = N if tn is None else tn
    assert M % tm == 0 and N % tn == 0
    assert ln is None or tn == N
    assert splits is None or tn == N

    def body(*refs):
        a_refs, w_refs = refs[:na], refs[na:2 * na]
        pos = 2 * na
        acc = None
        for a_ref, w_ref in zip(a_refs, w_refs):
            t = jnp.dot(a_ref[...].astype(BF16), w_ref[...], preferred_element_type=F32)
            acc = t if acc is None else acc + t
        if bias is not None:
            acc = acc + refs[pos][...]
            pos += 1
        if resid is not None:
            acc = resid_scale * refs[pos][...] + acc
            pos += 1
        if ln is not None:
            g_ref, b_ref = refs[pos], refs[pos + 1]
            y_ref, xhat_ref, rstd_ref = refs[pos + 2], refs[pos + 3], refs[pos + 4]
            xhat, rstd = _ln_rows(acc)
            y_ref[...] = xhat * g_ref[...] + b_ref[...]
            xhat_ref[...] = xhat
            rstd_ref[...] = rstd
        elif splits is not None:
            c0 = 0
            for o_ref, (width, dtype, scale) in zip(refs[pos:], splits):
                part = acc[:, c0:c0 + width]
                if scale != 1.0:
                    part = part * scale
                o_ref[...] = part.astype(dtype)
                c0 += width
        else:
            refs[pos][...] = acc.astype(out_dtype)

    in_specs = [pl.BlockSpec((tm, a.shape[1]), lambda j, i: (i, 0)) for a in a_list]
    in_specs += [pl.BlockSpec((w.shape[0], tn), lambda j, i: (0, j)) for w in w_list]
    args = list(a_list) + list(w_list)
    if bias is not None:
        in_specs.append(pl.BlockSpec((1, tn), lambda j, i: (0, j)))
        args.append(bias.reshape(1, N))
    if resid is not None:
        in_specs.append(pl.BlockSpec((tm, tn), lambda j, i: (i, j)))
        args.append(resid)
    if ln is not None:
        in_specs += [pl.BlockSpec((1, N), lambda j, i: (0, 0))] * 2
        args += [ln[0].reshape(1, N), ln[1].reshape(1, N)]
        out_shape = (_sds((M, N), F32), _sds((M, N), F32), _sds((M, 1), F32))
        out_specs = (pl.BlockSpec((tm, N), lambda j, i: (i, 0)), pl.BlockSpec((tm, N), lambda j, i: (i, 0)),
                     pl.BlockSpec((tm, 1), lambda j, i: (i, 0)))
    elif splits is not None:
        out_shape = tuple(_sds((M, w), d) for (w, d, _) in splits)
        out_specs = tuple(pl.BlockSpec((tm, w), lambda j, i: (i, 0)) for (w, _, _) in splits)
    else:
        out_shape = _sds((M, N), out_dtype)
        out_specs = pl.BlockSpec((tm, tn), lambda j, i: (i, j))
    return _call(body, grid=(N // tn, M // tm), in_specs=in_specs, out_specs=out_specs, out_shape=out_shape,
                 name=name, vmem_mb=56)(*args)


def _mm_tn(a, dy, *, name, tm, tn, tk, colsum=False):
    S, Ka = a.shape
    N = dy.shape[1]
    assert S % tk == 0 and Ka % tm == 0 and N % tn == 0

    def body(a_ref, dy_ref, out_ref, *cs):
        i, k = pl.program_id(1), pl.program_id(2)
        dyb = dy_ref[...]
        part = lax.dot_general(a_ref[...].astype(BF16), dyb.astype(BF16), (((0,), (0,)), ((), ())),
                               preferred_element_type=F32)

        @pl.when(k == 0)
        def _():
            out_ref[...] = part

        @pl.when(k > 0)
        def _():
            out_ref[...] += part

        if colsum:
            cs_ref = cs[0]
            s = _colsum(dyb.astype(F32))

            @pl.when((i == 0) & (k == 0))
            def _():
                cs_ref[...] = s

            @pl.when((i == 0) & (k > 0))
            def _():
                cs_ref[...] += s

    out_shape = [_sds((Ka, N), F32)]
    out_specs = [pl.BlockSpec((tm, tn), lambda j, i, k: (i, j))]
    if colsum:
        out_shape.append(_sds((1, N), F32))
        out_specs.append(pl.BlockSpec((1, tn), lambda j, i, k: (0, j)))
    res = _call(body, grid=(N // tn, Ka // tm, S // tk),
                in_specs=[pl.BlockSpec((tk, tm), lambda j, i, k: (k, i)), pl.BlockSpec((tk, tn), lambda j, i, k: (k, j))],
                out_specs=tuple(out_specs), out_shape=tuple(out_shape), name=name, vmem_mb=56)(a, dy)
    return res if colsum else res[0]


def _mm_tn_shared(a, dys, *, name, tk):
    S, Ka = a.shape
    n = len(dys)
    assert S % tk == 0

    def body(*refs):
        a_ref, dy_refs, outs = refs[0], refs[1:1 + n], refs[1 + n:]
        k = pl.program_id(0)
        a_bf = a_ref[...].astype(BF16)
        for i, dy_ref in enumerate(dy_refs):
            dyb = dy_ref[...]
            part = lax.dot_general(a_bf, dyb.astype(BF16), (((0,), (0,)), ((), ())), preferred_element_type=F32)
            s = _colsum(dyb.astype(F32))
            w_ref, c_ref = outs[2 * i], outs[2 * i + 1]

            @pl.when(k == 0)
            def _():
                w_ref[...] = part
                c_ref[...] = s

            @pl.when(k > 0)
            def _():
                w_ref[...] += part
                c_ref[...] += s

    in_specs = [pl.BlockSpec((tk, Ka), lambda k: (k, 0))]
    in_specs += [pl.BlockSpec((tk, dy.shape[1]), lambda k: (k, 0)) for dy in dys]
    out_specs, out_shape = [], []
    for dy in dys:
        N = dy.shape[1]
        out_specs += [pl.BlockSpec((Ka, N), lambda k: (0, 0)), pl.BlockSpec((1, N), lambda k: (0, 0))]
        out_shape += [_sds((Ka, N), F32), _sds((1, N), F32)]
    res = _call(body, grid=(S // tk,), in_specs=in_specs, out_specs=tuple(out_specs), out_shape=tuple(out_shape),
                name=name, vmem_mb=56)(a, *dys)
    return [(res[2 * i], res[2 * i + 1]) for i in range(n)]


def _ln_bwd(xhat, rstd, g, *, name, ts, dy=None, b=None, target=None):
    S, D = xhat.shape
    from_loss = target is not None

    def body(*refs):
        if from_loss:
            xhat_ref, rstd_ref, g_ref, b_ref, t_ref, dz_ref, dg_ref, db_ref, dzs_ref, loss_ref = refs
        else:
            xhat_ref, rstd_ref, g_ref, dy_ref, dz_ref, dg_ref, db_ref, dzs_ref = refs
        i = pl.program_id(0)
        xh = xhat_ref[...]
        gg = g_ref[...]
        if from_loss:
            err = xh * gg + b_ref[...] - t_ref[...]
            dyv = err * (1.0 / D)
            lsum = (0.5 / D) * jnp.sum(err * err, axis=(0, 1), keepdims=True)
        else:
            dyv = dy_ref[...]
        dz = _ln_bwd_rows(dyv * gg, xh, rstd_ref[...])
        dz_ref[...] = dz
        parts = [(dg_ref, _colsum(dyv * xh)), (db_ref, _colsum(dyv)), (dzs_ref, _colsum(dz))]
        if from_loss:
            parts.append((loss_ref, lsum))

        @pl.when(i == 0)
        def _():
            for r, v in parts:
                r[...] = v

        @pl.when(i > 0)
        def _():
            for r, v in parts:
                r[...] += v

    row = pl.BlockSpec((ts, D), lambda i: (i, 0))
    vec = pl.BlockSpec((1, D), lambda i: (0, 0))
    in_specs = [row, pl.BlockSpec((ts, 1), lambda i: (i, 0)), vec]
    args = [xhat, rstd, g.reshape(1, D)]
    if from_loss:
        in_specs += [vec, row]
        args += [b.reshape(1, D), target]
    else:
        in_specs += [row]
        args += [dy]
    out_shape = [_sds((S, D), F32), _sds((1, D), F32), _sds((1, D), F32), _sds((1, D), F32)]
    out_specs = [row, vec, vec, vec]
    if from_loss:
        out_shape.append(_sds((1, 1), F32))
        out_specs.append(pl.BlockSpec((1, 1), lambda i: (0, 0)))
    return _call(body, grid=(S // ts,), in_specs=in_specs, out_specs=tuple(out_specs), out_shape=tuple(out_shape),
                 name=name)(*args)


def _glu(v):
    return v[:, :CONV_CH] * jax.nn.sigmoid(v[:, CONV_CH:])


def _copies_moved_back(buf, sh, rows):
    for b in range(1, 8):
        sh[b - 1, 8:rows, :] = buf[pl.ds(8 - b, rows - 8), :]


def _copies_moved_ahead(buf, sh, rows):
    for b in range(1, 8):
        sh[b - 1, 0:rows - 8, :] = buf[pl.ds(b, rows - 8), :]


def _rows_back(buf, sh, start, s, n):
    a, b = divmod(s, 8)
    return buf[pl.ds(start - 8 * a, n), :] if b == 0 else sh[b - 1, pl.ds(start - 8 * a, n), :]


def _rows_ahead(buf, sh, start, s, n):
    a, b = divmod(s, 8)
    return buf[pl.ds(start + 8 * a, n), :] if b == 0 else sh[b - 1, pl.ds(start + 8 * a, n), :]


def _conv_fwd(h_a, dw_w, dw_b, ln_g, ln_b, *, name, ts):
    S = h_a.shape[0]
    C, K, HB = CONV_CH, CONV_WIDTH, CONV_HALO
    RC = 128

    def body(h_ref, halo_ref, w_ref, b_ref, g_ref, bb_ref, out_ref, hc_ref, gbuf, gsh):
        i = pl.program_id(0)
        gbuf[0:HB, :] = jnp.where(i > 0, _glu(halo_ref[...]), 0.0)
        gbuf[HB:HB + ts, :] = _glu(h_ref[...])
        _copies_moved_back(gbuf, gsh, HB + ts)
        for r0 in range(0, ts, RC):
            acc = jnp.zeros((RC, C), F32) + b_ref[...]
            for k in range(K):
                acc = acc + w_ref[k:k + 1, :] * _rows_back(gbuf, gsh, r0 + HB, K - 1 - k, RC)
            hc_ref[r0:r0 + RC, :] = acc
            xhat, _ = _ln_rows(acc)
            hn = xhat * g_ref[...] + bb_ref[...]
            out_ref[r0:r0 + RC, :] = (hn * jax.nn.sigmoid(hn)).astype(BF16)

    nb = ts // HB
    vec = pl.BlockSpec((1, C), lambda i: (0, 0))
    return _call(body, grid=(S // ts,),
                 in_specs=[pl.BlockSpec((ts, 2 * C), lambda i: (i, 0)),
                           pl.BlockSpec((HB, 2 * C), lambda i: (jnp.maximum(i * nb - 1, 0), 0)),
                           pl.BlockSpec((K, C), lambda i: (0, 0)), vec, vec, vec],
                 out_specs=(pl.BlockSpec((ts, C), lambda i: (i, 0)), pl.BlockSpec((ts, C), lambda i: (i, 0))),
                 out_shape=(_sds((S, C), BF16), _sds((S, C), F32)),
                 scratch_shapes=[pltpu.VMEM((HB + ts, C), F32), pltpu.VMEM((7, HB + ts, C), F32)], name=name)(
        h_a, h_a, dw_w, dw_b.reshape(1, C), ln_g.reshape(1, C), ln_b.reshape(1, C))


def _conv_bwd(h_a, hc, dout, dw_w, ln_g, ln_b, *, name, ts):
    S = h_a.shape[0]
    C, K, HB = CONV_CH, CONV_WIDTH, CONV_HALO
    RC = 128
    n = S // ts

    def dconv_out(hc_v, do_v, g_ref, bb_ref):
        xhat, rstd = _ln_rows(hc_v)
        hn = xhat * g_ref[...] + bb_ref[...]
        sg = jax.nn.sigmoid(hn)
        dhn = do_v * (sg * (1.0 + hn * (1.0 - sg)))
        return _ln_bwd_rows(dhn * g_ref[...], xhat, rstd), dhn, xhat

    def body(h_ref, hprev_ref, hc_ref, hcnext_ref, do_ref, donext_ref, w_ref, g_ref, bb_ref,
             dh_ref, dw_ref, dwb_ref, dg_ref, db_ref, gbuf, dbuf, gsh, dsh):
        i = pl.program_id(0)
        hv = h_ref[...]
        gbuf[0:HB, :] = jnp.where(i > 0, _glu(hprev_ref[...]), 0.0)
        gbuf[HB:HB + ts, :] = _glu(hv)
        dhc, dhn, xhat = dconv_out(hc_ref[...], do_ref[...], g_ref, bb_ref)
        dhc_next, _, _ = dconv_out(hcnext_ref[...], donext_ref[...], g_ref, bb_ref)
        dbuf[0:ts, :] = dhc
        dbuf[ts:ts + HB, :] = jnp.where(i < n - 1, dhc_next, 0.0)
        _copies_moved_back(gbuf, gsh, HB + ts)
        _copies_moved_ahead(dbuf, dsh, ts + HB)
        dw_rows = []
        for k in range(K):
            acc_k = jnp.zeros((1, C), F32)
            for r0 in range(0, ts, RC):
                acc_k = acc_k + _colsum(dbuf[r0:r0 + RC, :] * _rows_back(gbuf, gsh, r0 + HB, K - 1 - k, RC))
            dw_rows.append(acc_k)
        dw_rows.append(jnp.zeros((1, C), F32))
        dw_tile = jnp.concatenate(dw_rows, axis=0)
        for r0 in range(0, ts, RC):
            acc = jnp.zeros((RC, C), F32)
            for k in range(K):
                acc = acc + w_ref[k:k + 1, :] * _rows_ahead(dbuf, dsh, r0, K - 1 - k, RC)
            a = hv[r0:r0 + RC, :C]
            sg = jax.nn.sigmoid(hv[r0:r0 + RC, C:])
            dh_ref[r0:r0 + RC, :C] = (acc * sg).astype(BF16)
            dh_ref[r0:r0 + RC, C:] = (acc * a * sg * (1.0 - sg)).astype(BF16)
        parts = [(dw_ref, dw_tile), (dwb_ref, _colsum(dhc)), (dg_ref, _colsum(dhn * xhat)), (db_ref, _colsum(dhn))]

        @pl.when(i == 0)
        def _():
            for r, v in parts:
                r[...] = v

        @pl.when(i > 0)
        def _():
            for r, v in parts:
                r[...] += v

    nb = ts // HB
    last = S // HB - 1
    vec = pl.BlockSpec((1, C), lambda i: (0, 0))
    nxt = lambda i: (jnp.minimum((i + 1) * nb, last), 0)
    return _call(body, grid=(n,),
                 in_specs=[pl.BlockSpec((ts, 2 * C), lambda i: (i, 0)),
                           pl.BlockSpec((HB, 2 * C), lambda i: (jnp.maximum(i * nb - 1, 0), 0)),
                           pl.BlockSpec((ts, C), lambda i: (i, 0)), pl.BlockSpec((HB, C), nxt),
                           pl.BlockSpec((ts, C), lambda i: (i, 0)), pl.BlockSpec((HB, C), nxt),
                           pl.BlockSpec((K, C), lambda i: (0, 0)), vec, vec],
                 out_specs=(pl.BlockSpec((ts, 2 * C), lambda i: (i, 0)), pl.BlockSpec((K + 1, C), lambda i: (0, 0)),
                            vec, vec, vec),
                 out_shape=(_sds((S, 2 * C), BF16), _sds((K + 1, C), F32), _sds((1, C), F32), _sds((1, C), F32),
                            _sds((1, C), F32)),
                 scratch_shapes=[pltpu.VMEM((HB + ts, C), F32), pltpu.VMEM((ts + HB, C), F32),
                                 pltpu.VMEM((7, HB + ts, C), F32), pltpu.VMEM((7, ts + HB, C), F32)], name=name)(
        h_a, h_a, hc, hc, dout, dout, dw_w, ln_g.reshape(1, C), ln_b.reshape(1, C))


def _gmlp_mix(vn_bf, w_ref, mix_buf, ts):
    for ch in range(ts // CHUNK):
        for g in range(GMLP_GROUPS):
            vg = vn_bf[ch * CHUNK:(ch + 1) * CHUNK, g * GMLP_GROUP_DIM:(g + 1) * GMLP_GROUP_DIM]
            mix_buf[ch * CHUNK:(ch + 1) * CHUNK, g * GMLP_GROUP_DIM:(g + 1) * GMLP_GROUP_DIM] = jnp.dot(
                w_ref[g], vg, preferred_element_type=F32)


def _gmlp_fwd(h_c, ln_g, ln_b, w_tril, bs_rows, *, name, ts):
    S = h_c.shape[0]
    C = GMLP_CH

    def body(h_ref, g_ref, b_ref, w_ref, bs_ref, out_ref, mix_buf):
        hv = h_ref[...]
        xhat, _ = _ln_rows(hv[:, C:])
        vn = (xhat * g_ref[...] + b_ref[...]).astype(BF16)
        _gmlp_mix(vn, w_ref, mix_buf, ts)
        for ch in range(ts // CHUNK):
            rows = slice(ch * CHUNK, (ch + 1) * CHUNK)
            out_ref[rows, :] = (hv[rows, :C] * (mix_buf[rows, :] + bs_ref[...])).astype(BF16)

    vec = pl.BlockSpec((1, C), lambda i: (0, 0))
    return _call(body, grid=(S // ts,),
                 in_specs=[pl.BlockSpec((ts, 2 * C), lambda i: (i, 0)), vec, vec,
                           pl.BlockSpec((GMLP_GROUPS, CHUNK, CHUNK), lambda i: (0, 0, 0)),
                           pl.BlockSpec((CHUNK, C), lambda i: (0, 0))],
                 out_specs=pl.BlockSpec((ts, C), lambda i: (i, 0)), out_shape=_sds((S, C), BF16),
                 scratch_shapes=[pltpu.VMEM((ts, C), F32)], name=name)(
        h_c, ln_g.reshape(1, C), ln_b.reshape(1, C), w_tril, bs_rows)


def _gmlp_bwd(h_c, dout, ln_g, ln_b, w_tril, w_tril_t, bs_rows, *, name, ts):
    S = h_c.shape[0]
    C, G, GD = GMLP_CH, GMLP_GROUPS, GMLP_GROUP_DIM

    def body(h_ref, do_ref, g_ref, b_ref, w_ref, wt_ref, bs_ref, dh_ref, dw_ref, dbs_ref, dg_ref, db_ref,
             mix_buf, dvn_buf):
        i = pl.program_id(0)
        hv = h_ref[...]
        u = hv[:, :C]
        xhat, rstd = _ln_rows(hv[:, C:])
        vn = (xhat * g_ref[...] + b_ref[...]).astype(BF16)
        _gmlp_mix(vn, w_ref, mix_buf, ts)
        do = do_ref[...]
        dmixed = do * u
        dm_bf = dmixed.astype(BF16)
        lane = lax.broadcasted_iota(jnp.int32, (CHUNK, LANES), 1)
        dbs = jnp.zeros((CHUNK, LANES), F32)
        dws = [jnp.zeros((CHUNK, CHUNK), F32) for _ in range(G)]
        for ch in range(ts // CHUNK):
            rows = slice(ch * CHUNK, (ch + 1) * CHUNK)
            dh_ref[rows, :C] = (do[rows, :] * (mix_buf[rows, :] + bs_ref[...])).astype(BF16)
            for g in range(G):
                cols = slice(g * GD, (g + 1) * GD)
                dmg = dm_bf[rows, cols]
                dvn_buf[rows, cols] = jnp.dot(wt_ref[g], dmg, preferred_element_type=F32)
                dws[g] = dws[g] + lax.dot_general(dmg, vn[rows, cols], (((1,), (1,)), ((), ())),
                                                  preferred_element_type=F32)
                rs = jnp.sum(dmixed[rows, cols], axis=1, keepdims=True)
                dbs = dbs + jnp.where(lane == g, rs, 0.0)
        dvn = dvn_buf[...]
        dh_ref[:, C:] = _ln_bwd_rows(dvn * g_ref[...], xhat, rstd).astype(BF16)
        dgv, dbv = _colsum(dvn * xhat), _colsum(dvn)

        @pl.when(i == 0)
        def _():
            for g in range(G):
                dw_ref[g] = dws[g]
            dbs_ref[...] = dbs
            dg_ref[...] = dgv
            db_ref[...] = dbv

        @pl.when(i > 0)
        def _():
            for g in range(G):
                dw_ref[g] += dws[g]
            dbs_ref[...] += dbs
            dg_ref[...] += dgv
            db_ref[...] += dbv

    vec = pl.BlockSpec((1, C), lambda i: (0, 0))
    wspec = pl.BlockSpec((G, CHUNK, CHUNK), lambda i: (0, 0, 0))
    return _call(body, grid=(S // ts,),
                 in_specs=[pl.BlockSpec((ts, 2 * C), lambda i: (i, 0)),
                           pl.BlockSpec((ts, C), lambda i: (i, 0)), vec, vec, wspec, wspec,
                           pl.BlockSpec((CHUNK, C), lambda i: (0, 0))],
                 out_specs=(pl.BlockSpec((ts, 2 * C), lambda i: (i, 0)), wspec,
                            pl.BlockSpec((CHUNK, LANES), lambda i: (0, 0)), vec, vec),
                 out_shape=(_sds((S, 2 * C), BF16), _sds((G, CHUNK, CHUNK), F32), _sds((CHUNK, LANES), F32),
                            _sds((1, C), F32), _sds((1, C), F32)),
                 scratch_shapes=[pltpu.VMEM((ts, C), F32), pltpu.VMEM((ts, C), F32)], name=name)(
        h_c, dout, ln_g.reshape(1, C), ln_b.reshape(1, C), w_tril, w_tril_t, bs_rows)


def _ffn_act_fwd(hu, cw, cb, *, name, ts, tc):
    S, F2 = hu.shape
    F = F2 // 2
    nj = F // tc
    HB = FFN_HALO
    nb = ts // HB

    def body(g_ref, v_ref, gh_ref, vh_ref, wg_ref, wv_ref, bg_ref, bv_ref, act_ref, dg_ref, dv_ref):
        i = pl.program_id(0)
        row = lax.broadcasted_iota(jnp.int32, (HB, tc), 0)

        def conv_chunk(x_ref, w_ref, b_ref, c, carry):
            cur = x_ref[c * HB:(c + 1) * HB, :]
            r1, r2 = pltpu.roll(cur, 1, 0), pltpu.roll(cur, 2, 0)
            x1 = jnp.where(row < 1, carry[0], r1)
            x2 = jnp.where(row < 2, carry[1], r2)
            out = (w_ref[0:1, :] * x2 + w_ref[1:2, :] * x1 + w_ref[2:3, :] * cur) + b_ref[...]
            return out, (r1, r2)

        def first_carry(h_ref):
            prev = jnp.where(i > 0, h_ref[...], 0.0)
            return pltpu.roll(prev, 1, 0), pltpu.roll(prev, 2, 0)

        cg, cv = first_carry(gh_ref), first_carry(vh_ref)
        for r0 in range(0, ts, FFN_ROWS):
            acts, dgs, dvs = [], [], []
            for c in range(r0 // HB, (r0 + FFN_ROWS) // HB):
                gc, cg = conv_chunk(g_ref, wg_ref, bg_ref, c, cg)
                vc, cv = conv_chunk(v_ref, wv_ref, bv_ref, c, cv)
                sg = jax.nn.sigmoid(gc)
                silu = gc * sg
                acts.append(silu * vc)
                dgs.append(vc * (sg * (1.0 + gc * (1.0 - sg))))
                dvs.append(silu)
            rows = slice(r0, r0 + FFN_ROWS)
            act_ref[rows, :] = jnp.concatenate(acts, axis=0).astype(BF16)
            dg_ref[rows, :] = jnp.concatenate(dgs, axis=0).astype(BF16)
            dv_ref[rows, :] = jnp.concatenate(dvs, axis=0).astype(BF16)

    prev = lambda off: (lambda i, j: (jnp.maximum(i * nb - 1, 0), j + off))
    out = pl.BlockSpec((ts, tc), lambda i, j: (i, j))
    return _call(body, grid=(S // ts, nj),
                 in_specs=[pl.BlockSpec((ts, tc), lambda i, j: (i, j)), pl.BlockSpec((ts, tc), lambda i, j: (i, j + nj)),
                           pl.BlockSpec((HB, tc), prev(0)), pl.BlockSpec((HB, tc), prev(nj)),
                           pl.BlockSpec((3, tc), lambda i, j: (0, j)), pl.BlockSpec((3, tc), lambda i, j: (0, j + nj)),
                           pl.BlockSpec((1, tc), lambda i, j: (0, j)), pl.BlockSpec((1, tc), lambda i, j: (0, j + nj))],
                 out_specs=(out, out, out), out_shape=(_sds((S, F), BF16),) * 3, name=name)(
        hu, hu, hu, hu, cw, cw, cb.reshape(1, F2), cb.reshape(1, F2))


def _ffn_act_bwd(hu, dact, dact_dg, dact_dv, cw, *, name, ts, tc):
    S, F2 = hu.shape
    F = F2 // 2
    nj = F // tc
    HB = FFN_HALO
    HB16 = 16
    n = S // ts

    def body(g_ref, v_ref, lg_ref, lv_ref, lgn_ref, lvn_ref, da_ref, dan_ref, wg_ref, wv_ref,
             dhg_ref, dhv_ref, dwg_ref, dwv_ref, dbg_ref, dbv_ref, dug_ref, duv_ref, accg, accv):
        i = pl.program_id(1)
        RC = FFN_ROWS
        npairs = ts // RC
        last_tile = i == n - 1

        @pl.when(i == 0)
        def _():
            accg[...] = jnp.zeros_like(accg)
            accv[...] = jnp.zeros_like(accv)

        for l_ref, ln_ref, h_ref, w_ref, dh_ref, acc in ((lg_ref, lgn_ref, g_ref, wg_ref, dhg_ref, accg),
                                                          (lv_ref, lvn_ref, v_ref, wv_ref, dhv_ref, accv)):
            for c0 in range(0, tc, FFN_STRIP):
                cols = slice(c0, min(c0 + FFN_STRIP, tc))
                sw = cols.stop - cols.start
                row = lax.broadcasted_iota(jnp.int32, (HB, sw), 0)
                w0, w1, w2 = w_ref[0:1, cols], w_ref[1:2, cols], w_ref[2:3, cols]

                def d_rows(p):
                    rows = slice(p * RC, (p + 1) * RC)
                    return da_ref[rows, cols] * l_ref[rows, cols].astype(F32)

                after = jnp.where(last_tile, 0.0, dan_ref[:, cols]) * ln_ref[:, cols].astype(F32)[0:HB, :]
                pair = d_rows(0)
                cur = pair[0:HB, :]
                c7, c6 = pltpu.roll(cur, HB - 1, 0), pltpu.roll(cur, HB - 2, 0)
                sums = [None] * 5
                for p in range(npairs):
                    nxt_pair = d_rows(p + 1) if p + 1 < npairs else None
                    dhus = []
                    for half, nxt_c in enumerate((pair[HB:RC, :], after if nxt_pair is None else nxt_pair[0:HB, :])):
                        c = 2 * p + half
                        n7, n6 = pltpu.roll(nxt_c, HB - 1, 0), pltpu.roll(nxt_c, HB - 2, 0)
                        taps = [jnp.where(row >= HB - 2, n6, c6), jnp.where(row >= HB - 1, n7, c7), cur]
                        dhu = w2 * taps[2] + w1 * taps[1] + w0 * taps[0]
                        dhus.append(dhu)
                        h = h_ref[c * HB:(c + 1) * HB, cols]
                        parts = [h * taps[0], h * taps[1], h * taps[2], taps[2], dhu]
                        sums = [q if s is None else s + q for s, q in zip(sums, parts)]
                        cur, c7, c6 = nxt_c, n7, n6
                    dh_ref[p * RC:(p + 1) * RC, cols] = jnp.concatenate(dhus, axis=0).astype(BF16)
                    pair = nxt_pair
                for k in range(5):
                    acc[8 * k:8 * k + 8, cols] += sums[k]

        @pl.when(i == n - 1)
        def _():
            for acc, dw_ref, db_ref, du_ref in ((accg, dwg_ref, dbg_ref, dug_ref), (accv, dwv_ref, dbv_ref, duv_ref)):
                for k in range(3):
                    dw_ref[k:k + 1, :] = _colsum(acc[8 * k:8 * k + 8, :])
                db_ref[...] = _colsum(acc[24:32, :])
                du_ref[...] = _colsum(acc[32:40, :])

    nxt = lambda hb: (lambda j, i: (jnp.minimum((i + 1) * (ts // hb), S // hb - 1), j))
    tile = lambda off: (lambda j, i: (i, j + off))
    vec = lambda rows: pl.BlockSpec((rows, tc), lambda j, i: (0, j))
    return _call(body, grid=(nj, n),
                 in_specs=[pl.BlockSpec((ts, tc), tile(0)), pl.BlockSpec((ts, tc), tile(nj)),
                           pl.BlockSpec((ts, tc), tile(0)), pl.BlockSpec((ts, tc), tile(0)),
                           pl.BlockSpec((HB16, tc), nxt(HB16)), pl.BlockSpec((HB16, tc), nxt(HB16)),
                           pl.BlockSpec((ts, tc), tile(0)), pl.BlockSpec((HB, tc), nxt(HB)),
                           pl.BlockSpec((3, tc), lambda j, i: (0, j)), pl.BlockSpec((3, tc), lambda j, i: (0, j + nj))],
                 out_specs=(pl.BlockSpec((ts, tc), tile(0)), pl.BlockSpec((ts, tc), tile(0)),
                            vec(3), vec(3), vec(1), vec(1), vec(1), vec(1)),
                 out_shape=(_sds((S, F), BF16), _sds((S, F), BF16), _sds((3, F), F32), _sds((3, F), F32),
                            _sds((1, F), F32), _sds((1, F), F32), _sds((1, F), F32), _sds((1, F), F32)),
                 scratch_shapes=[pltpu.VMEM((40, tc), F32), pltpu.VMEM((40, tc), F32)], name=name)(
        hu, hu, dact_dg, dact_dv, dact_dg, dact_dv, dact, dact, cw, cw)


def _t5_bucket(dist):
    max_exact = N_BUCKETS // 2
    d = np.maximum(dist, 1).astype(np.float64)
    large = max_exact + (np.log(d / max_exact) / math.log(MAX_DISTANCE / max_exact)
                         * (N_BUCKETS - max_exact)).astype(np.int32)
    large = np.minimum(large, N_BUCKETS - 1)
    return np.where(dist < max_exact, dist, large).astype(np.int32)


def _pattern_tables(window, dilation):
    qi = np.arange(ATTN_BLOCK)[:, None]
    kj = np.arange(2 * ATTN_BLOCK)[None, :]
    dist = qi + ATTN_BLOCK - kj
    valid = (dist >= 0) & (dist <= window // dilation)
    bucket = _t5_bucket(np.clip(dist, 0, None) * dilation)
    return bucket, valid


def _dilate_qkv(qkv, *, name, ts):
    S, C = qkv.shape
    dils = [d for (_, d) in PATTERNS if d > 1]

    def body(x_ref, nat_ref, *rest):
        outs, tmp = rest[:-1], rest[-1]
        nat_ref[...] = x_ref[...].astype(BF16)
        for j in range(C // LANES):
            cols = slice(j * LANES, (j + 1) * LANES)
            tmp[...] = x_ref[:, cols]
            for d, o_ref in zip(dils, outs):
                _dilate(tmp, o_ref, cols, d, ts, BF16)

    out_shape = (_sds((S, C), BF16),) + tuple(_sds((d, S // d, C), BF16) for d in dils)
    out_specs = (_dil_spec(1, ts, C),) + tuple(_dil_spec(d, ts, C) for d in dils)
    res = _call(body, grid=(S // ts,), in_specs=[_dil_spec(1, ts, C)], out_specs=out_specs, out_shape=out_shape,
                scratch_shapes=[pltpu.VMEM((ts, LANES), F32)], name=name)(qkv)
    return [res[0]] + [r.reshape(S, C) for r in res[1:]]


def _attn_group(S):
    nb_min = (S // PATTERNS[-1][1]) // ATTN_BLOCK
    return math.gcd(8, nb_min)


def _attn_fwd(qkv, bias, *, name, nb, G):
    S = qkv.shape[0]
    B, HD = ATTN_BLOCK, HEAD_DIM
    GR = G * B
    ng = S // GR

    def body(q_ref, k_ref, kh_ref, v_ref, vh_ref, bias_ref, o_ref, lse_ref, kbuf, vbuf):
        g = pl.program_id(1)
        halo_ok = (g * G) % nb != 0
        kbuf[0:B, :] = kh_ref[...]
        kbuf[B:B + GR, :] = k_ref[...]
        vbuf[0:B, :] = vh_ref[...]
        vbuf[B:B + GR, :] = v_ref[...]
        col = lax.broadcasted_iota(jnp.int32, (B, 2 * B), 1)
        head0 = lax.broadcasted_iota(jnp.int32, (B, LANES), 1) < HD

        for bi in range(G):
            r0 = bi * B
            q2 = q_ref[r0:r0 + B, :]
            kk = kbuf[r0:r0 + 2 * B, :]
            vv = vbuf[r0:r0 + 2 * B, :]
            zero = jnp.zeros_like(q2)
            os_, ls_, lses = [], [], []
            for hh in range(2):
                qh = jnp.where(head0, q2, zero) if hh == 0 else jnp.where(head0, zero, q2)
                s = lax.dot_general(qh, kk, (((1,), (1,)), ((), ())), preferred_element_type=F32)
                s = s + bias_ref[hh]
                if bi == 0:
                    s = jnp.where(jnp.logical_and(jnp.logical_not(halo_ok), col < B), NEG, s)
                m = jnp.max(s, axis=1, keepdims=True)
                p = jnp.exp(s - m)
                l = jnp.sum(p, axis=1, keepdims=True)
                os_.append(jnp.dot(p.astype(BF16), vv, preferred_element_type=F32))
                ls_.append(l)
                lses.append(m + jnp.log(l))
            o_ref[r0:r0 + B, :] = (jnp.where(head0, os_[0], os_[1]) / jnp.where(head0, ls_[0], ls_[1])).astype(BF16)
            lse_ref[r0:r0 + B, :] = jnp.where(head0, lses[0], lses[1])

    halo = lambda off: (lambda hp, g: (jnp.maximum(g * G - 1, 0), off + hp))
    main = lambda off: (lambda hp, g: (g, off + hp))
    return _call(body, grid=(4, ng),
                 in_specs=[pl.BlockSpec((GR, LANES), main(0)), pl.BlockSpec((GR, LANES), main(4)),
                           pl.BlockSpec((B, LANES), halo(4)), pl.BlockSpec((GR, LANES), main(8)),
                           pl.BlockSpec((B, LANES), halo(8)), pl.BlockSpec((2, B, 2 * B), lambda hp, g: (hp, 0, 0))],
                 out_specs=(pl.BlockSpec((GR, LANES), main(0)), pl.BlockSpec((GR, LANES), main(0))),
                 out_shape=(_sds((S, ATTN_CH), BF16), _sds((S, ATTN_CH), F32)),
                 scratch_shapes=[pltpu.VMEM((B + GR, LANES), BF16), pltpu.VMEM((B + GR, LANES), BF16)], name=name)(
        qkv, qkv, qkv, qkv, qkv, bias)


def _dil_spec(d, ts, C):
    if d == 1:
        return pl.BlockSpec((ts, C), lambda i: (i, 0))
    return pl.BlockSpec((d, ts // d, C), lambda i: (0, i, 0))


def _dil_view(a, d):
    return a if d == 1 else a.reshape(d, a.shape[0] // d, a.shape[1])


def _dilate(nat_tmp, dst_ref, cols, d, ts, dtype=F32):
    for r in range(d):
        dst_ref[r, :, cols] = nat_tmp[pl.ds(r, ts // d, stride=d), :].astype(dtype)


def _undilate(src_ref, nat_tmp, cols, d, ts, accumulate=False):
    for r in range(d):
        rows = pl.ds(r, ts // d, stride=d)
        if accumulate:
            nat_tmp[rows, :] = nat_tmp[rows, :] + src_ref[r, :, cols].astype(F32)
        else:
            nat_tmp[rows, :] = src_ref[r, :, cols].astype(F32)


def _attn_merge(o_list, lse_list, dils, *, name, ts):
    S, C = o_list[0].shape
    P = len(o_list)
    nd = sum(1 for d in dils if d > 1)

    def body(*refs):
        o_refs, l_refs = refs[:P], refs[P:2 * P]
        out_ref, lse_ref = refs[2 * P], refs[2 * P + 1]
        lse_d_refs = refs[2 * P + 2:2 * P + 2 + nd]
        scratch = list(refs[2 * P + 2 + nd:])
        tmp = scratch.pop()
        for j in range(C // LANES):
            cols = slice(j * LANES, (j + 1) * LANES)
            os_, ls, free = [], [], list(scratch)
            for o_ref, l_ref, d in zip(o_refs, l_refs, dils):
                if d > 1:
                    so, sl = free.pop(0), free.pop(0)
                    _undilate(o_ref, so, cols, d, ts)
                    _undilate(l_ref, sl, cols, d, ts)
                    os_.append(so[...])
                    ls.append(sl[...])
                else:
                    os_.append(o_ref[:, cols].astype(F32))
                    ls.append(l_ref[:, cols])
            m = ls[0]
            for l in ls[1:]:
                m = jnp.maximum(m, l)
            ws = [jnp.exp(l - m) for l in ls]
            den = ws[0]
            for w in ws[1:]:
                den = den + w
            num = ws[0] * os_[0]
            for w, o in zip(ws[1:], os_[1:]):
                num = num + w * o
            out_ref[:, cols] = num / den
            tmp[...] = m + jnp.log(den)
            lse_ref[:, cols] = tmp[...]
            for l_out, d in zip(lse_d_refs, [d for d in dils if d > 1]):
                _dilate(tmp, l_out, cols, d, ts)

    row = _dil_spec(1, ts, C)
    dd = [d for d in dils if d > 1]
    res = _call(body, grid=(S // ts,), in_specs=[_dil_spec(d, ts, C) for d in dils] * 2,
                out_specs=(row, row) + tuple(_dil_spec(d, ts, C) for d in dd),
                out_shape=(_sds((S, C), F32), _sds((S, C), F32)) + tuple(_sds((d, S // d, C), F32) for d in dd),
                scratch_shapes=[pltpu.VMEM((ts, LANES), F32)] * (2 * nd + 1), name=name)(
        *[_dil_view(o, d) for o, d in zip(o_list, dils)], *[_dil_view(l, d) for l, d in zip(lse_list, dils)])
    lse_by_d = {1: res[1]}
    lse_by_d.update({d: r.reshape(S, C) for d, r in zip(dd, res[2:])})
    return res[0], [lse_by_d[d] for d in dils]


def _attn_prep(dout, out, dils, *, name, ts):
    S, C = out.shape
    HD = HEAD_DIM
    dd = [d for d in dils if d > 1]
    nd = len(dd)

    def body(do_ref, o_ref, d_ref, dob_ref, *rest):
        outs, tmp_d, tmp_o = rest[:2 * nd], rest[2 * nd], rest[2 * nd + 1]
        dob_ref[...] = do_ref[...].astype(BF16)
        for j in range(C // LANES):
            cols = slice(j * LANES, (j + 1) * LANES)
            do = do_ref[:, cols]
            prod = do * o_ref[:, cols]
            tmp_o[...] = do
            for h in range(LANES // HD):
                cs = slice(h * HD, (h + 1) * HD)
                tmp_d[:, cs] = jnp.broadcast_to(jnp.sum(prod[:, cs], axis=1, keepdims=True), (ts, HD))
            d_ref[:, cols] = tmp_d[...]
            for d, dd_out, do_out in zip(dd, outs[:nd], outs[nd:]):
                _dilate(tmp_d, dd_out, cols, d, ts)
                _dilate(tmp_o, do_out, cols, d, ts, BF16)

    row = _dil_spec(1, ts, C)
    res = _call(body, grid=(S // ts,), in_specs=[row, row],
                out_specs=(row, row) + tuple(_dil_spec(d, ts, C) for d in dd) * 2,
                out_shape=(_sds((S, C), F32), _sds((S, C), BF16)) + tuple(_sds((d, S // d, C), F32) for d in dd)
                + tuple(_sds((d, S // d, C), BF16) for d in dd),
                scratch_shapes=[pltpu.VMEM((ts, LANES), F32)] * 2, name=name)(dout, out)
    dd_by_d, do_by_d = {1: res[0]}, {1: res[1]}
    dd_by_d.update({d: r.reshape(S, C) for d, r in zip(dd, res[2:2 + nd])})
    do_by_d.update({d: r.reshape(S, C) for d, r in zip(dd, res[2 + nd:])})
    return [dd_by_d[d] for d in dils], [do_by_d[d] for d in dils]


def _attn_bwd(qkv, do, lse, dd, bias, *, name, nb, G):
    S = qkv.shape[0]
    B, HD = ATTN_BLOCK, HEAD_DIM
    GR = G * B
    ng = S // GR
    nblk = S // B

    def body(q_ref, k_ref, kh_ref, v_ref, vh_ref, do_ref, lse_ref, dd_ref, qn_ref, don_ref, lsen_ref, ddn_ref,
             bias_ref, dq_ref, dk_ref, dv_ref, dbias_ref, kbuf, vbuf, qbuf, dobuf, ds_st, p_st):
        g = pl.program_id(1)
        halo_ok = (g * G) % nb != 0
        next_ok = jnp.logical_and(((g + 1) * G) % nb != 0, g < ng - 1)
        kbuf[0:B, :] = kh_ref[...]
        kbuf[B:B + GR, :] = k_ref[...]
        vbuf[0:B, :] = vh_ref[...]
        vbuf[B:B + GR, :] = v_ref[...]
        col = lax.broadcasted_iota(jnp.int32, (B, 2 * B), 1)

        def tn_dot(a, b):
            return lax.dot_general(a, b, (((0,), (0,)), ((), ())), preferred_element_type=F32)

        @pl.when(g == 0)
        def _():
            dbias_ref[...] = jnp.zeros_like(dbias_ref)

        head0 = lax.broadcasted_iota(jnp.int32, (B, LANES), 1) < HD

        def nt_dot(a, b):
            return lax.dot_general(a, b, (((1,), (1,)), ((), ())), preferred_element_type=F32)

        def one_head(x, hh):
            zero = jnp.zeros_like(x)
            return jnp.where(head0, x, zero) if hh == 0 else jnp.where(head0, zero, x)

        def pick(per_head):
            return jnp.where(head0, per_head[0], per_head[1])

        for bi in range(G):
            r0 = bi * B
            q2 = q_ref[r0:r0 + B, :]
            do2 = do_ref[r0:r0 + B, :]
            kk = kbuf[r0:r0 + 2 * B, :]
            vv = vbuf[r0:r0 + 2 * B, :]
            dq = []
            for hh in range(2):
                s = nt_dot(one_head(q2, hh), kk) + bias_ref[hh]
                if bi == 0:
                    s = jnp.where(jnp.logical_and(jnp.logical_not(halo_ok), col < B), NEG, s)
                p = jnp.exp(s - lse_ref[r0:r0 + B, hh * HD:hh * HD + 1])
                dp = nt_dot(one_head(do2, hh), vv)
                ds = p * (dp - dd_ref[r0:r0 + B, hh * HD:hh * HD + 1])
                dbias_ref[hh] += ds
                ds_bf, p_bf = ds.astype(BF16), p.astype(BF16)
                dq.append(jnp.dot(ds_bf, kk, preferred_element_type=F32))
                ds_st[hh, 2 * r0:2 * r0 + B, :] = ds_bf[:, B:]
                p_st[hh, 2 * r0:2 * r0 + B, :] = p_bf[:, B:]
                if bi > 0:
                    ds_st[hh, 2 * r0 - B:2 * r0, :] = ds_bf[:, :B]
                    p_st[hh, 2 * r0 - B:2 * r0, :] = p_bf[:, :B]
            dq_ref[r0:r0 + B, :] = pick(dq).astype(BF16)

        @pl.when(next_ok)
        def _():
            qn = qn_ref[...]
            don = don_ref[...]
            kl = kbuf[GR:GR + B, :]
            vl = vbuf[GR:GR + B, :]
            for hh in range(2):
                s = nt_dot(one_head(qn, hh), kl) + bias_ref[hh, :, 0:B]
                p = jnp.exp(s - lsen_ref[:, hh * HD:hh * HD + 1])
                dp = nt_dot(one_head(don, hh), vl)
                ds = p * (dp - ddn_ref[:, hh * HD:hh * HD + 1])
                ds_st[hh, 2 * GR - B:2 * GR, :] = ds.astype(BF16)
                p_st[hh, 2 * GR - B:2 * GR, :] = p.astype(BF16)

        @pl.when(jnp.logical_not(next_ok))
        def _():
            for hh in range(2):
                ds_st[hh, 2 * GR - B:2 * GR, :] = jnp.zeros((B, B), BF16)
                p_st[hh, 2 * GR - B:2 * GR, :] = jnp.zeros((B, B), BF16)

        qbuf[0:GR, :] = q_ref[...]
        qbuf[GR:GR + B, :] = qn_ref[...]
        dobuf[0:GR, :] = do_ref[...]
        dobuf[GR:GR + B, :] = don_ref[...]
        for j in range(G):
            r0 = j * B
            q_pair = qbuf[r0:r0 + 2 * B, :]
            do_pair = dobuf[r0:r0 + 2 * B, :]
            dk_ref[r0:r0 + B, :] = pick([tn_dot(ds_st[hh, 2 * r0:2 * r0 + 2 * B, :], q_pair)
                                         for hh in range(2)]).astype(BF16)
            dv_ref[r0:r0 + B, :] = pick([tn_dot(p_st[hh, 2 * r0:2 * r0 + 2 * B, :], do_pair)
                                         for hh in range(2)]).astype(BF16)

    halo = lambda off: (lambda hp, g: (jnp.maximum(g * G - 1, 0), off + hp))
    main = lambda off: (lambda hp, g: (g, off + hp))
    nxt = lambda off: (lambda hp, g: (jnp.minimum((g + 1) * G, nblk - 1), off + hp))
    big, small = (lambda m: pl.BlockSpec((GR, LANES), m)), (lambda m: pl.BlockSpec((B, LANES), m))
    return _call(body, grid=(4, ng),
                 in_specs=[big(main(0)), big(main(4)), small(halo(4)), big(main(8)), small(halo(8)),
                           big(main(0)), big(main(0)), big(main(0)),
                           small(nxt(0)), small(nxt(0)), small(nxt(0)), small(nxt(0)),
                           pl.BlockSpec((2, B, 2 * B), lambda hp, g: (hp, 0, 0))],
                 out_specs=(big(main(0)), big(main(0)), big(main(0)),
                            pl.BlockSpec((2, B, 2 * B), lambda hp, g: (hp, 0, 0))),
                 out_shape=(_sds((S, ATTN_CH), BF16),) * 3 + (_sds((ATTN_HEADS, B, 2 * B), F32),),
                 scratch_shapes=[pltpu.VMEM((B + GR, LANES), BF16)] * 4 + [pltpu.VMEM((2, 2 * GR, B), BF16)] * 2,
                 name=name)(
        qkv, qkv, qkv, qkv, qkv, do, lse, dd, qkv, do, lse, dd, bias)


def _attn_combine(dq_list, dk_list, dv_list, dils, *, name, ts):
    S, C = dq_list[0].shape
    P = len(dq_list)
    scale = HEAD_DIM ** -0.5
    assert dils[0] == 1

    def body(*refs):
        out_refs, acc = refs[3 * P:3 * P + 3], refs[3 * P + 3]
        for part in range(3):
            for j in range(C // LANES):
                cols = slice(j * LANES, (j + 1) * LANES)
                acc[...] = refs[part * P][:, cols].astype(F32)
                for r, d in zip(refs[part * P + 1:(part + 1) * P], dils[1:]):
                    _undilate(r, acc, cols, d, ts, accumulate=True)
                tot = acc[...]
                if part == 0:
                    tot = tot * scale
                out_refs[part][:, cols] = tot.astype(BF16)

    return _call(body, grid=(S // ts,), in_specs=[_dil_spec(d, ts, C) for d in dils] * 3,
                 out_specs=(_dil_spec(1, ts, C),) * 3, out_shape=(_sds((S, C), BF16),) * 3,
                 scratch_shapes=[pltpu.VMEM((ts, LANES), F32)], name=name)(
        *[_dil_view(a, d) for lst in (dq_list, dk_list, dv_list) for a, d in zip(lst, dils)])


def _bias_tables(table, bucket_flat, *, name):
    P, _, K = bucket_flat.shape
    H = table.shape[1]
    KC = 4096

    def body(t_ref, bk_ref, out_ref):
        row = lax.broadcasted_iota(jnp.int32, (N_BUCKETS, KC), 0)
        for c in range(K // KC):
            bk = bk_ref[0, :, c * KC:(c + 1) * KC]
            onehot = (row == bk).astype(F32)
            vals = jnp.dot(t_ref[...], onehot, preferred_element_type=F32, precision=lax.Precision.HIGHEST)
            out_ref[0, :, c * KC:(c + 1) * KC] = jnp.where(bk >= 0, vals, NEG)

    return _call(body, grid=(P,),
                 in_specs=[pl.BlockSpec((H, N_BUCKETS), lambda p: (0, 0)), pl.BlockSpec((1, 1, K), lambda p: (p, 0, 0))],
                 out_specs=pl.BlockSpec((1, H, K), lambda p: (p, 0, 0)), out_shape=_sds((P, H, K), F32),
                 name=name)(table.T, bucket_flat)


def _bias_grad(dbias_flat, bucket_flat, *, name):
    P, H, K = dbias_flat.shape
    KC = 4096

    def body(db_ref, bk_ref, out_ref):
        p = pl.program_id(0)
        acc = jnp.zeros((N_BUCKETS, H), F32)
        row = lax.broadcasted_iota(jnp.int32, (N_BUCKETS, KC), 0)
        for c in range(K // KC):
            onehot = (row == bk_ref[0, :, c * KC:(c + 1) * KC]).astype(F32)
            acc = acc + lax.dot_general(onehot, db_ref[0, :, c * KC:(c + 1) * KC], (((1,), (1,)), ((), ())),
                                        preferred_element_type=F32, precision=lax.Precision.HIGHEST)

        @pl.when(p == 0)
        def _():
            out_ref[...] = acc

        @pl.when(p > 0)
        def _():
            out_ref[...] += acc

    return _call(body, grid=(P,),
                 in_specs=[pl.BlockSpec((1, H, K), lambda p: (p, 0, 0)), pl.BlockSpec((1, 1, K), lambda p: (p, 0, 0))],
                 out_specs=pl.BlockSpec((N_BUCKETS, H), lambda p: (0, 0)), out_shape=_sds((N_BUCKETS, H), F32),
                 name=name)(dbias_flat, bucket_flat)


def _allgather(blocks, *, name):
    n = len(blocks)

    def body(*refs):
        x_refs, out_refs = refs[:n], refs[n:2 * n]
        send_sems, recv_sems, local_sems = refs[2 * n:]
        x, y, c = lax.axis_index("x"), lax.axis_index("y"), lax.axis_index("c")
        me, sibling = (x, y, c), (x, y, 1 - c)
        chips = [(1 - x, y), (x, 1 - y), (1 - x, 1 - y)]

        def copy(i, k, blk, to, own=False):
            slot = out_refs[i].at[4 * blk[0] + 2 * blk[1] + blk[2]]
            return pltpu.make_async_remote_copy(src_ref=x_refs[i] if own else slot, dst_ref=slot,
                                                send_sem=send_sems.at[7 * i + k], recv_sem=recv_sems.at[7 * i + k],
                                                device_id=to, device_id_type=MESH)

        mine = [pltpu.make_async_copy(x_refs[i], out_refs[i].at[4 * x + 2 * y + c], local_sems.at[i])
                for i in range(n)]
        for cp in mine:
            cp.start()
        first = []
        for i in range(n):
            first.append(copy(i, 0, me, sibling, own=True))
            first += [copy(i, 1 + j, me, (*chip, c), own=True) for j, chip in enumerate(chips)]
        for cp in first:
            cp.start()
        passed = []
        for j, chip in enumerate(chips):
            for i in range(n):
                copy(i, 1 + j, (*chip, c), me).wait_recv()
                fwd = copy(i, 4 + j, (*chip, c), sibling)
                fwd.start()
                passed.append(fwd)
        for i in range(n):
            copy(i, 0, sibling, me).wait_recv()
            for j, chip in enumerate(chips):
                copy(i, 4 + j, (*chip, 1 - c), me).wait_recv()
        for cp in first + passed:
            cp.wait_send()
        for cp in mine:
            cp.wait()

    any_spec = pl.BlockSpec(memory_space=pl.ANY)
    return pl.pallas_call(body, out_shape=tuple(_sds((N_DEV,) + b.shape, b.dtype) for b in blocks),
                          in_specs=[any_spec] * n, out_specs=(any_spec,) * n,
                          scratch_shapes=[pltpu.SemaphoreType.DMA((7 * n,)), pltpu.SemaphoreType.DMA((7 * n,)),
                                          pltpu.SemaphoreType.DMA((n,))], name=name)(*blocks)


def _exchange(sends, *, name):
    n = len(sends)

    def body(*refs):
        send_refs, recv_refs = refs[:n], refs[n:2 * n]
        send_sems, recv_sems, local_sems = refs[2 * n:]
        x, y, c = lax.axis_index("x"), lax.axis_index("y"), lax.axis_index("c")
        me = 4 * x + 2 * y + c
        mine = [pltpu.make_async_copy(send_refs[i].at[me], recv_refs[i].at[me], local_sems.at[i]) for i in range(n)]
        for cp in mine:
            cp.start()
        copies = []
        for k in range(1, N_DEV):
            px = 1 - x if k & 4 else x
            py = 1 - y if k & 2 else y
            pc = 1 - c if k & 1 else c
            for i in range(n):
                cp = pltpu.make_async_remote_copy(src_ref=send_refs[i].at[4 * px + 2 * py + pc],
                                                  dst_ref=recv_refs[i].at[me],
                                                  send_sem=send_sems.at[7 * i + k - 1],
                                                  recv_sem=recv_sems.at[7 * i + k - 1],
                                                  device_id=(px, py, pc), device_id_type=MESH)
                cp.start()
                copies.append(cp)
        for cp in copies:
            cp.wait_recv()
        for cp in copies:
            cp.wait_send()
        for cp in mine:
            cp.wait()

    any_spec = pl.BlockSpec(memory_space=pl.ANY)
    return pl.pallas_call(body, out_shape=tuple(_sds(s.shape, s.dtype) for s in sends), in_specs=[any_spec] * n,
                          out_specs=(any_spec,) * n,
                          scratch_shapes=[pltpu.SemaphoreType.DMA((7 * n,)), pltpu.SemaphoreType.DMA((7 * n,)),
                                          pltpu.SemaphoreType.DMA((n,))], name=name)(*sends)


def _adamw(w, m, v, g_parts, *, name, tr):
    R, W = w.shape
    bc1 = 1.0 - ADAM_B1 ** ADAM_STEP
    bc2 = 1.0 - ADAM_B2 ** ADAM_STEP

    def body(w_ref, m_ref, v_ref, g_ref, go_ref, d_ref, mo_ref, vo_ref):
        g = g_ref[0].astype(F32)
        for i in range(1, N_DEV):
            g = g + g_ref[i].astype(F32)
        mn = ADAM_B1 * m_ref[...] + (1.0 - ADAM_B1) * g
        vn = ADAM_B2 * v_ref[...] + (1.0 - ADAM_B2) * (g * g)
        m_hat = mn / bc1
        v_hat = vn / bc2
        go_ref[...] = g
        d_ref[...] = -ADAM_LR * (m_hat / (jnp.sqrt(v_hat) + ADAM_EPS) + ADAM_WD * w_ref[...])
        mo_ref[...] = mn
        vo_ref[...] = vn

    row = pl.BlockSpec((tr, W), lambda i: (i, 0))
    return _call(body, grid=(R // tr,), in_specs=[row, row, row, pl.BlockSpec((N_DEV, tr, W), lambda i: (0, i, 0))],
                 out_specs=(row,) * 4, out_shape=(_sds((R, W), F32),) * 4, name=name)(w, m, v, g_parts)


def _round_up(n, k):
    return -(-n // k) * k


def _pack(arrs, width, row_mult):
    pieces, offs, r = [], [], 0
    for a in arrs:
        n = a.size
        rows = _round_up(n, width) // width
        flat = a.reshape(-1)
        if rows * width != n:
            flat = jnp.pad(flat, (0, rows * width - n))
        pieces.append(flat.reshape(rows, width))
        offs.append((r, rows, n))
        r += rows
    total = _round_up(r, row_mult)
    if total != r:
        pieces.append(jnp.zeros((total - r, width), pieces[0].dtype))
    return jnp.concatenate(pieces, axis=0), offs


def _unpack(pack, offs, shapes):
    out = []
    for (r, rows, n), shp in zip(offs, shapes):
        out.append(pack[r:r + rows].reshape(-1)[:n].reshape(shp))
    return out


def _gather_axis(full8, axis):
    moved = jnp.moveaxis(full8, 0, axis)
    shp = list(moved.shape)
    shp[axis:axis + 2] = [shp[axis] * shp[axis + 1]]
    return moved.reshape(shp)


def _split_axis(full, axis):
    shp = list(full.shape)
    shp[axis:axis + 1] = [N_DEV, shp[axis] // N_DEV]
    return jnp.moveaxis(full.reshape(shp), axis, 0)


def kernel(x, w_in, b_in, conv_dw_w, conv_dw_b, conv_ln_g, conv_ln_b, rel_bias_table, gmlp_ln_g, gmlp_ln_b, gmlp_w_s, gmlp_b_s, w_out, b_out, ln1_g, ln1_b, ffn_w_up, ffn_b_up, ffn_conv_w, ffn_conv_b, ffn_w_down, ffn_b_down, ln2_g, ln2_b, loss_target, m_w_in, m_b_in, m_conv_dw_w, m_conv_dw_b, m_conv_ln_g, m_conv_ln_b, m_rel_bias_table, m_gmlp_ln_g, m_gmlp_ln_b, m_gmlp_w_s, m_gmlp_b_s, m_w_out, m_b_out, m_ln1_g, m_ln1_b, m_ffn_w_up, m_ffn_b_up, m_ffn_conv_w, m_ffn_conv_b, m_ffn_w_down, m_ffn_b_down, m_ln2_g, m_ln2_b, v_w_in, v_b_in, v_conv_dw_w, v_conv_dw_b, v_conv_ln_g, v_conv_ln_b, v_rel_bias_table, v_gmlp_ln_g, v_gmlp_ln_b, v_gmlp_w_s, v_gmlp_b_s, v_w_out, v_b_out, v_ln1_g, v_ln1_b, v_ffn_w_up, v_ffn_b_up, v_ffn_conv_w, v_ffn_conv_b, v_ffn_w_down, v_ffn_b_down, v_ln2_g, v_ln2_b):
    W = dict(w_in=w_in, b_in=b_in, conv_dw_w=conv_dw_w, conv_dw_b=conv_dw_b, conv_ln_g=conv_ln_g,
             conv_ln_b=conv_ln_b, rel_bias_table=rel_bias_table, gmlp_ln_g=gmlp_ln_g, gmlp_ln_b=gmlp_ln_b,
             gmlp_w_s=gmlp_w_s, gmlp_b_s=gmlp_b_s, w_out=w_out, b_out=b_out, ln1_g=ln1_g, ln1_b=ln1_b,
             ffn_w_up=ffn_w_up, ffn_b_up=ffn_b_up, ffn_conv_w=ffn_conv_w, ffn_conv_b=ffn_conv_b,
             ffn_w_down=ffn_w_down, ffn_b_down=ffn_b_down, ln2_g=ln2_g, ln2_b=ln2_b)
    Mo = dict(w_in=m_w_in, b_in=m_b_in, conv_dw_w=m_conv_dw_w, conv_dw_b=m_conv_dw_b, conv_ln_g=m_conv_ln_g,
              conv_ln_b=m_conv_ln_b, rel_bias_table=m_rel_bias_table, gmlp_ln_g=m_gmlp_ln_g, gmlp_ln_b=m_gmlp_ln_b,
              gmlp_w_s=m_gmlp_w_s, gmlp_b_s=m_gmlp_b_s, w_out=m_w_out, b_out=m_b_out, ln1_g=m_ln1_g, ln1_b=m_ln1_b,
              ffn_w_up=m_ffn_w_up, ffn_b_up=m_ffn_b_up, ffn_conv_w=m_ffn_conv_w, ffn_conv_b=m_ffn_conv_b,
              ffn_w_down=m_ffn_w_down, ffn_b_down=m_ffn_b_down, ln2_g=m_ln2_g, ln2_b=m_ln2_b)
    Vo = dict(w_in=v_w_in, b_in=v_b_in, conv_dw_w=v_conv_dw_w, conv_dw_b=v_conv_dw_b, conv_ln_g=v_conv_ln_g,
              conv_ln_b=v_conv_ln_b, rel_bias_table=v_rel_bias_table, gmlp_ln_g=v_gmlp_ln_g, gmlp_ln_b=v_gmlp_ln_b,
              gmlp_w_s=v_gmlp_w_s, gmlp_b_s=v_gmlp_b_s, w_out=v_w_out, b_out=v_b_out, ln1_g=v_ln1_g, ln1_b=v_ln1_b,
              ffn_w_up=v_ffn_w_up, ffn_b_up=v_ffn_b_up, ffn_conv_w=v_ffn_conv_w, ffn_conv_b=v_ffn_conv_b,
              ffn_w_down=v_ffn_w_down, ffn_b_down=v_ffn_b_down, ln2_g=v_ln2_g, ln2_b=v_ln2_b)

    xs = x[0]
    target = loss_target[0]
    S, D = xs.shape
    F2 = ffn_b_up.shape[1]
    F = F2 // 2
    ts = min(512, S)
    G = _attn_group(S)
    tc = F // 2 if (F // 2) % LANES == 0 else F

    mat_names = SHARDED[:4]
    payload = [W[n].astype(BF16) if n in mat_names else W[n] for n in SHARDED]
    wall = _allgather(payload, name="weight_allgather")
    full = {n: _gather_axis(parts, SHARD_AXIS[n]) for n, parts in zip(SHARDED, wall)}

    tables = [_pattern_tables(w, d) for (w, d) in PATTERNS]
    bucket_flat = jnp.asarray(np.stack([np.where(v, b, -1).reshape(1, -1) for (b, v) in tables]).astype(np.int32))
    bias_all = _bias_tables(rel_bias_table, bucket_flat, name="bias_tables")
    biases = [bias_all[p].reshape(ATTN_HEADS, ATTN_BLOCK, 2 * ATTN_BLOCK) for p in range(len(PATTERNS))]
    nbs = [(S // d) // ATTN_BLOCK for (_, d) in PATTERNS]
    dils = [d for (_, d) in PATTERNS]
    scale = HEAD_DIM ** -0.5

    saved = []
    cur = xs
    for l in range(DEPTH):
        Win, Wout, Wup, Wdown = full['w_in'][l], full['w_out'][l], full['ffn_w_up'][l], full['ffn_w_down'][l]
        qcols = slice(2 * CONV_CH, 2 * CONV_CH + ATTN_CH)
        Win_s = Win.at[:, qcols].multiply(scale)
        b_in_s = b_in[l].at[qcols].multiply(scale)
        h_a, qkv, h_c = _mm([cur], [Win_s], bias=b_in_s, tm=ts, name="in_proj",
                            splits=((2 * CONV_CH, F32, 1.0), (3 * ATTN_CH, F32, 1.0), (2 * GMLP_CH, F32, 1.0)))
        conv_out, hc = _conv_fwd(h_a, full['conv_dw_w'][l], conv_dw_b[l], conv_ln_g[l], conv_ln_b[l],
                                 name="conv_fwd", ts=ts)
        qkv_d = _dilate_qkv(qkv, name="dilate_qkv", ts=ts)
        o_ps, lse_ps = [], []
        for p, d in enumerate(dils):
            o_p, lse_p = _attn_fwd(qkv_d[p], biases[p], name=f"attn_fwd_d{d}", nb=nbs[p], G=G)
            o_ps.append(o_p)
            lse_ps.append(lse_p)
        attn_out, lse = _attn_merge(o_ps, lse_ps, dils, name="attn_merge", ts=ts)
        w_tril = jnp.tril(gmlp_w_s[l]).astype(BF16)
        bs_rows = jnp.repeat(gmlp_b_s[l].T, GMLP_GROUP_DIM, axis=1)
        gm_out = _gmlp_fwd(h_c, gmlp_ln_g[l], gmlp_ln_b[l], w_tril, bs_rows, name="gmlp_fwd", ts=ts)
        x1, xhat1, rstd1 = _mm([conv_out, attn_out, gm_out],
                               [Wout[:CONV_CH], Wout[CONV_CH:CONV_CH + ATTN_CH], Wout[CONV_CH + ATTN_CH:]],
                               bias=b_out[l], resid=cur, resid_scale=ALPHA, ln=(ln1_g[l], ln1_b[l]), tm=ts,
                               name="out_proj_ln")
        hu = _mm([x1], [Wup], bias=ffn_b_up[l], tm=ts, tn=F, name="ffn_up")
        act, act_dg, act_dv = _ffn_act_fwd(hu, full['ffn_conv_w'][l], ffn_conv_b[l], name="ffn_act_fwd",
                                           ts=min(256, S), tc=tc)
        x2, xhat2, rstd2 = _mm([act], [Wdown], bias=ffn_b_down[l], resid=x1, resid_scale=ALPHA,
                               ln=(ln2_g[l], ln2_b[l]), tm=ts, name="ffn_down_ln")
        saved.append(dict(x0=cur, h_a=h_a, h_c=h_c, qkv_d=qkv_d, hc=hc, conv_out=conv_out, attn_out=attn_out,
                          lse=lse, gm_out=gm_out, w_tril=w_tril, bs_rows=bs_rows, x1=x1, xhat1=xhat1, rstd1=rstd1,
                          hu=hu, act=act, act_dg=act_dg, act_dv=act_dv, xhat2=xhat2, rstd2=rstd2))
        cur = x2

    grads = {n: [None] * DEPTH for n in WEIGHTS if n != 'rel_bias_table'}
    drel = None
    dx = None
    loss_part = None
    tk = min(1024, S)
    for l in reversed(range(DEPTH)):
        sv = saved[l]
        Win, Wout, Wup, Wdown = full['w_in'][l], full['w_out'][l], full['ffn_w_up'][l], full['ffn_w_down'][l]
        if dx is None:
            dz2, dg2, db2, dzs2, loss_part = _ln_bwd(sv['xhat2'], sv['rstd2'], ln2_g[l], b=ln2_b[l], target=target,
                                                     name="ln2_bwd_loss", ts=ts)
        else:
            dz2, dg2, db2, dzs2 = _ln_bwd(sv['xhat2'], sv['rstd2'], ln2_g[l], dy=dx, name="ln_bwd", ts=ts)
        grads['ln2_g'][l], grads['ln2_b'][l], grads['ffn_b_down'][l] = dg2[0], db2[0], dzs2[0]
        grads['ffn_w_down'][l] = _mm_tn(sv['act'], dz2, tm=F // 2 if (F // 2) % LANES == 0 else F, tn=D, tk=tk,
                                        name="dw_down")
        dact = _mm([dz2], [Wdown.T], tm=ts, name="dact")
        dhg, dhv, dwg, dwv, dbg, dbv, dug, duv = _ffn_act_bwd(sv['hu'], dact, sv['act_dg'], sv['act_dv'],
                                                              full['ffn_conv_w'][l], name="ffn_act_bwd",
                                                              ts=min(256, S), tc=tc)
        grads['ffn_conv_w'][l] = jnp.concatenate([dwg, dwv], axis=1)
        grads['ffn_conv_b'][l] = jnp.concatenate([dbg, dbv], axis=1)[0]
        grads['ffn_b_up'][l] = jnp.concatenate([dug, duv], axis=1)[0]
        grads['ffn_w_up'][l] = jnp.concatenate(
            [_mm_tn(sv['x1'], dhg, tm=D, tn=tc, tk=tk, name="dw_up"),
             _mm_tn(sv['x1'], dhv, tm=D, tn=tc, tk=tk, name="dw_up")], axis=1)
        WupT = Wup.T
        dx1 = _mm([dhg, dhv], [WupT[:F], WupT[F:]], resid=dz2, resid_scale=ALPHA, tm=ts, name="dx1")
        dz1, dg1, db1, dzs1 = _ln_bwd(sv['xhat1'], sv['rstd1'], ln1_g[l], dy=dx1, name="ln_bwd", ts=ts)
        grads['ln1_g'][l], grads['ln1_b'][l], grads['b_out'][l] = dg1[0], db1[0], dzs1[0]
        grads['w_out'][l] = jnp.concatenate(
            [_mm_tn(sv['conv_out'], dz1, tm=CONV_CH, tn=D, tk=tk, name="dw_out_conv"),
             _mm_tn(sv['attn_out'], dz1, tm=ATTN_CH, tn=D, tk=tk, name="dw_out_attn"),
             _mm_tn(sv['gm_out'], dz1, tm=GMLP_CH, tn=D, tk=tk, name="dw_out_conv")], axis=0)
        dc_conv, dc_attn, dc_gm = _mm([dz1], [Wout.T], tm=ts, name="dcat",
                                      splits=((CONV_CH, F32, 1.0), (ATTN_CH, F32, 1.0), (GMLP_CH, F32, 1.0)))
        dh_a, ddw, ddwb, dclg, dclb = _conv_bwd(sv['h_a'], sv['hc'], dc_conv, full['conv_dw_w'][l], conv_ln_g[l],
                                                conv_ln_b[l], name="conv_bwd", ts=ts)
        grads['conv_dw_w'][l], grads['conv_dw_b'][l] = ddw[:CONV_WIDTH], ddwb[0]
        grads['conv_ln_g'][l], grads['conv_ln_b'][l] = dclg[0], dclb[0]
        dd_d, do_d = _attn_prep(dc_attn, sv['attn_out'], dils, name="attn_prep", ts=ts)
        dqs, dks, dvs, dbs = [], [], [], []
        for p, d in enumerate(dils):
            dq, dk, dv, dbias = _attn_bwd(sv['qkv_d'][p], do_d[p], sv['lse'][p], dd_d[p], biases[p],
                                          name=f"attn_bwd_d{d}", nb=nbs[p], G=G)
            dqs.append(dq)
            dks.append(dk)
            dvs.append(dv)
            dbs.append(dbias.reshape(1, ATTN_HEADS, -1))
        dqkv = _attn_combine(dqs, dks, dvs, dils, name="attn_combine", ts=ts)
        dr = _bias_grad(jnp.concatenate(dbs, axis=0), bucket_flat, name="bias_grad")
        drel = dr if drel is None else drel + dr
        w_tril_t = jnp.swapaxes(sv['w_tril'], 1, 2)
        dh_c, dws, dbs_acc, dglg, dglb = _gmlp_bwd(sv['h_c'], dc_gm, gmlp_ln_g[l], gmlp_ln_b[l], sv['w_tril'],
                                                   w_tril_t, sv['bs_rows'], name="gmlp_bwd", ts=ts)
        grads['gmlp_w_s'][l] = jnp.tril(dws)
        grads['gmlp_b_s'][l] = dbs_acc[:, :GMLP_GROUPS].T
        grads['gmlp_ln_g'][l], grads['gmlp_ln_b'][l] = dglg[0], dglb[0]
        dw_in = _mm_tn_shared(sv['x0'], [dh_a, *dqkv, dh_c], tk=tk, name="dw_in")
        grads['w_in'][l] = jnp.concatenate([w for w, _ in dw_in], axis=1)
        grads['b_in'][l] = jnp.concatenate([c for _, c in dw_in], axis=1)[0]
        WinT = Win.T
        edges = [0, 2 * CONV_CH] + [2 * CONV_CH + k * ATTN_CH for k in (1, 2, 3)] + [WinT.shape[0]]
        dx = _mm([dh_a, *dqkv, dh_c], [WinT[a:b] for a, b in zip(edges[:-1], edges[1:])], resid=dz1,
                 resid_scale=ALPHA, tm=ts, name="dx0")

    gfull = {n: jnp.stack(v) for n, v in grads.items()}
    gfull['rel_bias_table'] = drel

    sends = []
    for n in SHARDED:
        parts = _split_axis(gfull[n], SHARD_AXIS[n])
        sends.append(parts.reshape(N_DEV, -1, parts.shape[-1]).astype(BF16))
    recvs = _exchange(sends, name="grad_exchange")
    shard_out = [[], [], [], []]
    for n, recv in zip(SHARDED, recvs):
        shp = W[n].shape
        rows = recv.shape[1]
        tr = rows // 4 if rows % 64 == 0 else rows
        outs = _adamw(*[src[n].reshape(rows, shp[-1]) for src in (W, Mo, Vo)], recv, name=f"adamw_{n}", tr=tr)
        for kind in range(4):
            shard_out[kind].append(outs[kind].reshape(shp))

    gsmall, soffs = _pack([gfull[n] for n in SMALL], LANES, 8)
    gall = _allgather([gsmall], name="small_grad_allgather")[0]
    spacks = [_pack([src[n] for n in SMALL], LANES, 8)[0] for src in (W, Mo, Vo)]
    souts = _adamw(spacks[0], spacks[1], spacks[2], gall, name="adamw_small", tr=gsmall.shape[0])
    small_out = [_unpack(o, soffs, [W[n].shape for n in SMALL]) for o in souts]

    loss = lax.psum(loss_part[0, 0], ("x", "y", "c"))
    by_kind = []
    for kind in range(4):
        d = dict(zip(SHARDED, shard_out[kind]))
        d.update(zip(SMALL, small_out[kind]))
        by_kind.append([d[n] for n in WEIGHTS])
    return (loss, dx[None], *by_kind[0], *by_kind[1], *by_kind[2], *by_kind[3])
```

```python
import math

import numpy as np
import jax
import jax.numpy as jnp
from jax import lax
from jax.experimental import pallas as pl
from jax.experimental.pallas import tpu as pltpu

F32 = jnp.float32
BF16 = jnp.bfloat16

DEPTH = 2
HEAD_DIM = 64
CONV_CH = 256
CONV_WIDTH = 31
ATTN_HEADS = 8
ATTN_CH = ATTN_HEADS * HEAD_DIM
PATTERNS = ((128, 1), (512, 4), (2048, 16))
ATTN_BLOCK = 128
N_BUCKETS = 32
MAX_DISTANCE = 2048
GMLP_CH = 256
GMLP_GROUPS = 4
GMLP_GROUP_DIM = GMLP_CH // GMLP_GROUPS
CHUNK = 128
FFN_CONV_WIDTH = 3
LN_EPS = 1e-5
ALPHA = (2.0 * DEPTH) ** 0.25
ADAM_LR = 0.001
ADAM_B1 = 0.9
ADAM_B2 = 0.999
ADAM_EPS = 1e-08
ADAM_WD = 0.01
ADAM_STEP = 10
NEG = -1e30
N_DEV = 8
LANES = 128
CONV_HALO = 32
FFN_HALO = 8
FFN_ROWS = 16
FFN_STRIP = 256
MESH = pl.DeviceIdType.MESH

WEIGHTS = ['w_in', 'b_in', 'conv_dw_w', 'conv_dw_b', 'conv_ln_g', 'conv_ln_b', 'rel_bias_table', 'gmlp_ln_g',
           'gmlp_ln_b', 'gmlp_w_s', 'gmlp_b_s', 'w_out', 'b_out', 'ln1_g', 'ln1_b', 'ffn_w_up', 'ffn_b_up',
           'ffn_conv_w', 'ffn_conv_b', 'ffn_w_down', 'ffn_b_down', 'ln2_g', 'ln2_b']
SHARDED = ['w_in', 'w_out', 'ffn_w_up', 'ffn_w_down', 'conv_dw_w', 'ffn_conv_w']
SHARD_AXIS = {'w_in': 2, 'w_out': 1, 'ffn_w_up': 2, 'ffn_w_down': 1, 'conv_dw_w': 2, 'ffn_conv_w': 2}
SMALL = [n for n in WEIGHTS if n not in SHARDED]


def _call(body, *, grid=(), vmem_mb=48, **kw):
    params = pltpu.CompilerParams(dimension_semantics=("arbitrary",) * len(grid), vmem_limit_bytes=vmem_mb << 20)
    return pl.pallas_call(body, grid=grid, compiler_params=params, **kw)


def _sds(shape, dtype):
    return jax.ShapeDtypeStruct(shape, dtype)


def _ln_rows(z):
    mu = jnp.mean(z, axis=-1, keepdims=True)
    zc = z - mu
    var = jnp.mean(zc * zc, axis=-1, keepdims=True)
    rstd = lax.rsqrt(var + LN_EPS)
    return zc * rstd, rstd


def _ln_bwd_rows(dxhat, xhat, rstd):
    m1 = jnp.mean(dxhat, axis=-1, keepdims=True)
    m2 = jnp.mean(dxhat * xhat, axis=-1, keepdims=True)
    return rstd * (dxhat - m1 - xhat * m2)


def _colsum(v):
    return jnp.sum(v, axis=0, keepdims=True)


def _mm(a_list, w_list, *, name, tm, tn=None, bias=None, resid=None, resid_scale=1.0, ln=None, splits=None,
        out_dtype=F32):
    na = len(a_list)
    M = a_list[0].shape[0]
    N = w_list[0].shape[1]
    tn = N if tn is None else tn
    assert M % tm == 0 and N % tn == 0
    assert ln is None or tn == N
    assert splits is None or tn == N

    def body(*refs):
        a_refs, w_refs = refs[:na], refs[na:2 * na]
        pos = 2 * na
        acc = None
        for a_ref, w_ref in zip(a_refs, w_refs):
            t = jnp.dot(a_ref[...].astype(BF16), w_ref[...], preferred_element_type=F32)
            acc = t if acc is None else acc + t
        if bias is not None:
            acc = acc + refs[pos][...]
            pos += 1
        if resid is not None:
            acc = resid_scale * refs[pos][...] + acc
            pos += 1
        if ln is not None:
            g_ref, b_ref = refs[pos], refs[pos + 1]
            y_ref, xhat_ref, rstd_ref = refs[pos + 2], refs[pos + 3], refs[pos + 4]
            xhat, rstd = _ln_rows(acc)
            y_ref[...] = xhat * g_ref[...] + b_ref[...]
            xhat_ref[...] = xhat
            rstd_ref[...] = rstd
        elif splits is not None:
            c0 = 0
            for o_ref, (width, dtype, scale) in zip(refs[pos:], splits):
                part = acc[:, c0:c0 + width]
                if scale != 1.0:
                    part = part * scale
                o_ref[...] = part.astype(dtype)
                c0 += width
        else:
            refs[pos][...] = acc.astype(out_dtype)

    in_specs = [pl.BlockSpec((tm, a.shape[1]), lambda j, i: (i, 0)) for a in a_list]
    in_specs += [pl.BlockSpec((w.shape[0], tn), lambda j, i: (0, j)) for w in w_list]
    args = list(a_list) + list(w_list)
    if bias is not None:
        in_specs.append(pl.BlockSpec((1, tn), lambda j, i: (0, j)))
        args.append(bias.reshape(1, N))
    if resid is not None:
        in_specs.append(pl.BlockSpec((tm, tn), lambda j, i: (i, j)))
        args.append(resid)
    if ln is not None:
        in_specs += [pl.BlockSpec((1, N), lambda j, i: (0, 0))] * 2
        args += [ln[0].reshape(1, N), ln[1].reshape(1, N)]
        out_shape = (_sds((M, N), F32), _sds((M, N), F32), _sds((M, 1), F32))
        out_specs = (pl.BlockSpec((tm, N), lambda j, i: (i, 0)), pl.BlockSpec((tm, N), lambda j, i: (i, 0)),
                     pl.BlockSpec((tm, 1), lambda j, i: (i, 0)))
    elif splits is not None:
        out_shape = tuple(_sds((M, w), d) for (w, d, _) in splits)
        out_specs = tuple(pl.BlockSpec((tm, w), lambda j, i: (i, 0)) for (w, _, _) in splits)
    else:
        out_shape = _sds((M, N), out_dtype)
        out_specs = pl.BlockSpec((tm, tn), lambda j, i: (i, j))
    return _call(body, grid=(N // tn, M // tm), in_specs=in_specs, out_specs=out_specs, out_shape=out_shape,
                 name=name, vmem_mb=56)(*args)


def _mm_tn(a, dy, *, name, tm, tn, tk, colsum=False):
    S, Ka = a.shape
    N = dy.shape[1]
    assert S % tk == 0 and Ka % tm == 0 and N % tn == 0

    def body(a_ref, dy_ref, out_ref, *cs):
        i, k = pl.program_id(1), pl.program_id(2)
        dyb = dy_ref[...]
        part = lax.dot_general(a_ref[...].astype(BF16), dyb.astype(BF16), (((0,), (0,)), ((), ())),
                               preferred_element_type=F32)

        @pl.when(k == 0)
        def _():
            out_ref[...] = part

        @pl.when(k > 0)
        def _():
            out_ref[...] += part

        if colsum:
            cs_ref = cs[0]
            s = _colsum(dyb.astype(F32))

            @pl.when((i == 0) & (k == 0))
            def _():
                cs_ref[...] = s

            @pl.when((i == 0) & (k > 0))
            def _():
                cs_ref[...] += s

    out_shape = [_sds((Ka, N), F32)]
    out_specs = [pl.BlockSpec((tm, tn), lambda j, i, k: (i, j))]
    if colsum:
        out_shape.append(_sds((1, N), F32))
        out_specs.append(pl.BlockSpec((1, tn), lambda j, i, k: (0, j)))
    res = _call(body, grid=(N // tn, Ka // tm, S // tk),
                in_specs=[pl.BlockSpec((tk, tm), lambda j, i, k: (k, i)), pl.BlockSpec((tk, tn), lambda j, i, k: (k, j))],
                out_specs=tuple(out_specs), out_shape=tuple(out_shape), name=name, vmem_mb=56)(a, dy)
    return res if colsum else res[0]


def _mm_tn_shared(a, dys, *, name, tk):
    S, Ka = a.shape
    n = len(dys)
    assert S % tk == 0

    def body(*refs):
        a_ref, dy_refs, outs = refs[0], refs[1:1 + n], refs[1 + n:]
        k = pl.program_id(0)
        a_bf = a_ref[...].astype(BF16)
        for i, dy_ref in enumerate(dy_refs):
            dyb = dy_ref[...]
            part = lax.dot_general(a_bf, dyb.astype(BF16), (((0,), (0,)), ((), ())), preferred_element_type=F32)
            s = _colsum(dyb.astype(F32))
            w_ref, c_ref = outs[2 * i], outs[2 * i + 1]

            @pl.when(k == 0)
            def _():
                w_ref[...] = part
                c_ref[...] = s

            @pl.when(k > 0)
            def _():
                w_ref[...] += part
                c_ref[...] += s

    in_specs = [pl.BlockSpec((tk, Ka), lambda k: (k, 0))]
    in_specs += [pl.BlockSpec((tk, dy.shape[1]), lambda k: (k, 0)) for dy in dys]
    out_specs, out_shape = [], []
    for dy in dys:
        N = dy.shape[1]
        out_specs += [pl.BlockSpec((Ka, N), lambda k: (0, 0)), pl.BlockSpec((1, N), lambda k: (0, 0))]
        out_shape += [_sds((Ka, N), F32), _sds((1, N), F32)]
    res = _call(body, grid=(S // tk,), in_specs=in_specs, out_specs=tuple(out_specs), out_shape=tuple(out_shape),
                name=name, vmem_mb=56)(a, *dys)
    return [(res[2 * i], res[2 * i + 1]) for i in range(n)]


def _ln_bwd(xhat, rstd, g, *, name, ts, dy=None, b=None, target=None):
    S, D = xhat.shape
    from_loss = target is not None

    def body(*refs):
        if from_loss:
            xhat_ref, rstd_ref, g_ref, b_ref, t_ref, dz_ref, dg_ref, db_ref, dzs_ref, loss_ref = refs
        else:
            xhat_ref, rstd_ref, g_ref, dy_ref, dz_ref, dg_ref, db_ref, dzs_ref = refs
        i = pl.program_id(0)
        xh = xhat_ref[...]
        gg = g_ref[...]
        if from_loss:
            err = xh * gg + b_ref[...] - t_ref[...]
            dyv = err * (1.0 / D)
            lsum = (0.5 / D) * jnp.sum(err * err, axis=(0, 1), keepdims=True)
        else:
            dyv = dy_ref[...]
        dz = _ln_bwd_rows(dyv * gg, xh, rstd_ref[...])
        dz_ref[...] = dz
        parts = [(dg_ref, _colsum(dyv * xh)), (db_ref, _colsum(dyv)), (dzs_ref, _colsum(dz))]
        if from_loss:
            parts.append((loss_ref, lsum))

        @pl.when(i == 0)
        def _():
            for r, v in parts:
                r[...] = v

        @pl.when(i > 0)
        def _():
            for r, v in parts:
                r[...] += v

    row = pl.BlockSpec((ts, D), lambda i: (i, 0))
    vec = pl.BlockSpec((1, D), lambda i: (0, 0))
    in_specs = [row, pl.BlockSpec((ts, 1), lambda i: (i, 0)), vec]
    args = [xhat, rstd, g.reshape(1, D)]
    if from_loss:
        in_specs += [vec, row]
        args += [b.reshape(1, D), target]
    else:
        in_specs += [row]
        args += [dy]
    out_shape = [_sds((S, D), F32), _sds((1, D), F32), _sds((1, D), F32), _sds((1, D), F32)]
    out_specs = [row, vec, vec, vec]
    if from_loss:
        out_shape.append(_sds((1, 1), F32))
        out_specs.append(pl.BlockSpec((1, 1), lambda i: (0, 0)))
    return _call(body, grid=(S // ts,), in_specs=in_specs, out_specs=tuple(out_specs), out_shape=tuple(out_shape),
                 name=name)(*args)


def _glu(v):
    return v[:, :CONV_CH] * jax.nn.sigmoid(v[:, CONV_CH:])


def _copies_moved_back(buf, sh, rows):
    for b in range(1, 8):
        sh[b - 1, 8:rows, :] = buf[pl.ds(8 - b, rows - 8), :]


def _copies_moved_ahead(buf, sh, rows):
    for b in range(1, 8):
        sh[b - 1, 0:rows - 8, :] = buf[pl.ds(b, rows - 8), :]


def _rows_back(buf, sh, start, s, n):
    a, b = divmod(s, 8)
    return buf[pl.ds(start - 8 * a, n), :] if b == 0 else sh[b - 1, pl.ds(start - 8 * a, n), :]


def _rows_ahead(buf, sh, start, s, n):
    a, b = divmod(s, 8)
    return buf[pl.ds(start + 8 * a, n), :] if b == 0 else sh[b - 1, pl.ds(start + 8 * a, n), :]


def _conv_fwd(h_a, dw_w, dw_b, ln_g, ln_b, *, name, ts):
    S = h_a.shape[0]
    C, K, HB = CONV_CH, CONV_WIDTH, CONV_HALO
    RC = 128

    def body(h_ref, halo_ref, w_ref, b_ref, g_ref, bb_ref, out_ref, hc_ref, gbuf, gsh):
        i = pl.program_id(0)
        gbuf[0:HB, :] = jnp.where(i > 0, _glu(halo_ref[...]), 0.0)
        gbuf[HB:HB + ts, :] = _glu(h_ref[...])
        _copies_moved_back(gbuf, gsh, HB + ts)
        for r0 in range(0, ts, RC):
            acc = jnp.zeros((RC, C), F32) + b_ref[...]
            for k in range(K):
                acc = acc + w_ref[k:k + 1, :] * _rows_back(gbuf, gsh, r0 + HB, K - 1 - k, RC)
            hc_ref[r0:r0 + RC, :] = acc
            xhat, _ = _ln_rows(acc)
            hn = xhat * g_ref[...] + bb_ref[...]
            out_ref[r0:r0 + RC, :] = (hn * jax.nn.sigmoid(hn)).astype(BF16)

    nb = ts // HB
    vec = pl.BlockSpec((1, C), lambda i: (0, 0))
    return _call(body, grid=(S // ts,),
                 in_specs=[pl.BlockSpec((ts, 2 * C), lambda i: (i, 0)),
                           pl.BlockSpec((HB, 2 * C), lambda i: (jnp.maximum(i * nb - 1, 0), 0)),
                           pl.BlockSpec((K, C), lambda i: (0, 0)), vec, vec, vec],
                 out_specs=(pl.BlockSpec((ts, C), lambda i: (i, 0)), pl.BlockSpec((ts, C), lambda i: (i, 0))),
                 out_shape=(_sds((S, C), BF16), _sds((S, C), F32)),
                 scratch_shapes=[pltpu.VMEM((HB + ts, C), F32), pltpu.VMEM((7, HB + ts, C), F32)], name=name)(
        h_a, h_a, dw_w, dw_b.reshape(1, C), ln_g.reshape(1, C), ln_b.reshape(1, C))


def _conv_bwd(h_a, hc, dout, dw_w, ln_g, ln_b, *, name, ts):
    S = h_a.shape[0]
    C, K, HB = CONV_CH, CONV_WIDTH, CONV_HALO
    RC = 128
    n = S // ts

    def dconv_out(hc_v, do_v, g_ref, bb_ref):
        xhat, rstd = _ln_rows(hc_v)
        hn = xhat * g_ref[...] + bb_ref[...]
        sg = jax.nn.sigmoid(hn)
        dhn = do_v * (sg * (1.0 + hn * (1.0 - sg)))
        return _ln_bwd_rows(dhn * g_ref[...], xhat, rstd), dhn, xhat

    def body(h_ref, hprev_ref, hc_ref, hcnext_ref, do_ref, donext_ref, w_ref, g_ref, bb_ref,
             dh_ref, dw_ref, dwb_ref, dg_ref, db_ref, gbuf, dbuf, gsh, dsh):
        i = pl.program_id(0)
        hv = h_ref[...]
        gbuf[0:HB, :] = jnp.where(i > 0, _glu(hprev_ref[...]), 0.0)
        gbuf[HB:HB + ts, :] = _glu(hv)
        dhc, dhn, xhat = dconv_out(hc_ref[...], do_ref[...], g_ref, bb_ref)
        dhc_next, _, _ = dconv_out(hcnext_ref[...], donext_ref[...], g_ref, bb_ref)
        dbuf[0:ts, :] = dhc
        dbuf[ts:ts + HB, :] = jnp.where(i < n - 1, dhc_next, 0.0)
        _copies_moved_back(gbuf, gsh, HB + ts)
        _copies_moved_ahead(dbuf, dsh, ts + HB)
        dw_rows = []
        for k in range(K):
            acc_k = jnp.zeros((1, C), F32)
            for r0 in range(0, ts, RC):
                acc_k = acc_k + _colsum(dbuf[r0:r0 + RC, :] * _rows_back(gbuf, gsh, r0 + HB, K - 1 - k, RC))
            dw_rows.append(acc_k)
        dw_rows.append(jnp.zeros((1, C), F32))
        dw_tile = jnp.concatenate(dw_rows, axis=0)
        for r0 in range(0, ts, RC):
            acc = jnp.zeros((RC, C), F32)
            for k in range(K):
                acc = acc + w_ref[k:k + 1, :] * _rows_ahead(dbuf, dsh, r0, K - 1 - k, RC)
            a = hv[r0:r0 + RC, :C]
            sg = jax.nn.sigmoid(hv[r0:r0 + RC, C:])
            dh_ref[r0:r0 + RC, :C] = (acc * sg).astype(BF16)
            dh_ref[r0:r0 + RC, C:] = (acc * a * sg * (1.0 - sg)).astype(BF16)
        parts = [(dw_ref, dw_tile), (dwb_ref, _colsum(dhc)), (dg_ref, _colsum(dhn * xhat)), (db_ref, _colsum(dhn))]

        @pl.when(i == 0)
        def _():
            for r, v in parts:
                r[...] = v

        @pl.when(i > 0)
        def _():
            for r, v in parts:
                r[...] += v

    nb = ts // HB
    last = S // HB - 1
    vec = pl.BlockSpec((1, C), lambda i: (0, 0))
    nxt = lambda i: (jnp.minimum((i + 1) * nb, last), 0)
    return _call(body, grid=(n,),
                 in_specs=[pl.BlockSpec((ts, 2 * C), lambda i: (i, 0)),
                           pl.BlockSpec((HB, 2 * C), lambda i: (jnp.maximum(i * nb - 1, 0), 0)),
                           pl.BlockSpec((ts, C), lambda i: (i, 0)), pl.BlockSpec((HB, C), nxt),
                           pl.BlockSpec((ts, C), lambda i: (i, 0)), pl.BlockSpec((HB, C), nxt),
                           pl.BlockSpec((K, C), lambda i: (0, 0)), vec, vec],
                 out_specs=(pl.BlockSpec((ts, 2 * C), lambda i: (i, 0)), pl.BlockSpec((K + 1, C), lambda i: (0, 0)),
                            vec, vec, vec),
                 out_shape=(_sds((S, 2 * C), BF16), _sds((K + 1, C), F32), _sds((1, C), F32), _sds((1, C), F32),
                            _sds((1, C), F32)),
                 scratch_shapes=[pltpu.VMEM((HB + ts, C), F32), pltpu.VMEM((ts + HB, C), F32),
                                 pltpu.VMEM((7, HB + ts, C), F32), pltpu.VMEM((7, ts + HB, C), F32)], name=name)(
        h_a, h_a, hc, hc, dout, dout, dw_w, ln_g.reshape(1, C), ln_b.reshape(1, C))


def _gmlp_mix(vn_bf, w_ref, mix_buf, ts):
    for ch in range(ts // CHUNK):
        for g in range(GMLP_GROUPS):
            vg = vn_bf[ch * CHUNK:(ch + 1) * CHUNK, g * GMLP_GROUP_DIM:(g + 1) * GMLP_GROUP_DIM]
            mix_buf[ch * CHUNK:(ch + 1) * CHUNK, g * GMLP_GROUP_DIM:(g + 1) * GMLP_GROUP_DIM] = jnp.dot(
                w_ref[g], vg, preferred_element_type=F32)


def _gmlp_fwd(h_c, ln_g, ln_b, w_tril, bs_rows, *, name, ts):
    S = h_c.shape[0]
    C = GMLP_CH

    def body(h_ref, g_ref, b_ref, w_ref, bs_ref, out_ref, mix_buf):
        hv = h_ref[...]
        xhat, _ = _ln_rows(hv[:, C:])
        vn = (xhat * g_ref[...] + b_ref[...]).astype(BF16)
        _gmlp_mix(vn, w_ref, mix_buf, ts)
        for ch in range(ts // CHUNK):
            rows = slice(ch * CHUNK, (ch + 1) * CHUNK)
            out_ref[rows, :] = (hv[rows, :C] * (mix_buf[rows, :] + bs_ref[...])).astype(BF16)

    vec = pl.BlockSpec((1, C), lambda i: (0, 0))
    return _call(body, grid=(S // ts,),
                 in_specs=[pl.BlockSpec((ts, 2 * C), lambda i: (i, 0)), vec, vec,
                           pl.BlockSpec((GMLP_GROUPS, CHUNK, CHUNK), lambda i: (0, 0, 0)),
                           pl.BlockSpec((CHUNK, C), lambda i: (0, 0))],
                 out_specs=pl.BlockSpec((ts, C), lambda i: (i, 0)), out_shape=_sds((S, C), BF16),
                 scratch_shapes=[pltpu.VMEM((ts, C), F32)], name=name)(
        h_c, ln_g.reshape(1, C), ln_b.reshape(1, C), w_tril, bs_rows)


def _gmlp_bwd(h_c, dout, ln_g, ln_b, w_tril, w_tril_t, bs_rows, *, name, ts):
    S = h_c.shape[0]
    C, G, GD = GMLP_CH, GMLP_GROUPS, GMLP_GROUP_DIM

    def body(h_ref, do_ref, g_ref, b_ref, w_ref, wt_ref, bs_ref, dh_ref, dw_ref, dbs_ref, dg_ref, db_ref,
             mix_buf, dvn_buf):
        i = pl.program_id(0)
        hv = h_ref[...]
        u = hv[:, :C]
        xhat, rstd = _ln_rows(hv[:, C:])
        vn = (xhat * g_ref[...] + b_ref[...]).astype(BF16)
        _gmlp_mix(vn, w_ref, mix_buf, ts)
        do = do_ref[...]
        dmixed = do * u
        dm_bf = dmixed.astype(BF16)
        lane = lax.broadcasted_iota(jnp.int32, (CHUNK, LANES), 1)
        dbs = jnp.zeros((CHUNK, LANES), F32)
        dws = [jnp.zeros((CHUNK, CHUNK), F32) for _ in range(G)]
        for ch in range(ts // CHUNK):
            rows = slice(ch * CHUNK, (ch + 1) * CHUNK)
            dh_ref[rows, :C] = (do[rows, :] * (mix_buf[rows, :] + bs_ref[...])).astype(BF16)
            for g in range(G):
                cols = slice(g * GD, (g + 1) * GD)
                dmg = dm_bf[rows, cols]
                dvn_buf[rows, cols] = jnp.dot(wt_ref[g], dmg, preferred_element_type=F32)
                dws[g] = dws[g] + lax.dot_general(dmg, vn[rows, cols], (((1,), (1,)), ((), ())),
                                                  preferred_element_type=F32)
                rs = jnp.sum(dmixed[rows, cols], axis=1, keepdims=True)
                dbs = dbs + jnp.where(lane == g, rs, 0.0)
        dvn = dvn_buf[...]
        dh_ref[:, C:] = _ln_bwd_rows(dvn * g_ref[...], xhat, rstd).astype(BF16)
        dgv, dbv = _colsum(dvn * xhat), _colsum(dvn)

        @pl.when(i == 0)
        def _():
            for g in range(G):
                dw_ref[g] = dws[g]
            dbs_ref[...] = dbs
            dg_ref[...] = dgv
            db_ref[...] = dbv

        @pl.when(i > 0)
        def _():
            for g in range(G):
                dw_ref[g] += dws[g]
            dbs_ref[...] += dbs
            dg_ref[...] += dgv
            db_ref[...] += dbv

    vec = pl.BlockSpec((1, C), lambda i: (0, 0))
    wspec = pl.BlockSpec((G, CHUNK, CHUNK), lambda i: (0, 0, 0))
    return _call(body, grid=(S // ts,),
                 in_specs=[pl.BlockSpec((ts, 2 * C), lambda i: (i, 0)),
                           pl.BlockSpec((ts, C), lambda i: (i, 0)), vec, vec, wspec, wspec,
                           pl.BlockSpec((CHUNK, C), lambda i: (0, 0))],
                 out_specs=(pl.BlockSpec((ts, 2 * C), lambda i: (i, 0)), wspec,
                            pl.BlockSpec((CHUNK, LANES), lambda i: (0, 0)), vec, vec),
                 out_shape=(_sds((S, 2 * C), BF16), _sds((G, CHUNK, CHUNK), F32), _sds((CHUNK, LANES), F32),
                            _sds((1, C), F32), _sds((1, C), F32)),
                 scratch_shapes=[pltpu.VMEM((ts, C), F32), pltpu.VMEM((ts, C), F32)], name=name)(
        h_c, dout, ln_g.reshape(1, C), ln_b.reshape(1, C), w_tril, w_tril_t, bs_rows)


def _ffn_act_fwd(hu, cw, cb, *, name, ts, tc):
    S, F2 = hu.shape
    F = F2 // 2
    nj = F // tc
    HB = FFN_HALO
    nb = ts // HB

    def body(g_ref, v_ref, gh_ref, vh_ref, wg_ref, wv_ref, bg_ref, bv_ref, act_ref, dg_ref, dv_ref):
        i = pl.program_id(0)
        row = lax.broadcasted_iota(jnp.int32, (HB, tc), 0)

        def conv_chunk(x_ref, w_ref, b_ref, c, carry):
            cur = x_ref[c * HB:(c + 1) * HB, :]
            r1, r2 = pltpu.roll(cur, 1, 0), pltpu.roll(cur, 2, 0)
            x1 = jnp.where(row < 1, carry[0], r1)
            x2 = jnp.where(row < 2, carry[1], r2)
            out = (w_ref[0:1, :] * x2 + w_ref[1:2, :] * x1 + w_ref[2:3, :] * cur) + b_ref[...]
            return out, (r1, r2)

        def first_carry(h_ref):
            prev = jnp.where(i > 0, h_ref[...], 0.0)
            return pltpu.roll(prev, 1, 0), pltpu.roll(prev, 2, 0)

        cg, cv = first_carry(gh_ref), first_carry(vh_ref)
        for r0 in range(0, ts, FFN_ROWS):
            acts, dgs, dvs = [], [], []
            for c in range(r0 // HB, (r0 + FFN_ROWS) // HB):
                gc, cg = conv_chunk(g_ref, wg_ref, bg_ref, c, cg)
                vc, cv = conv_chunk(v_ref, wv_ref, bv_ref, c, cv)
                sg = jax.nn.sigmoid(gc)
                silu = gc * sg
                acts.append(silu * vc)
                dgs.append(vc * (sg * (1.0 + gc * (1.0 - sg))))
                dvs.append(silu)
            rows = slice(r0, r0 + FFN_ROWS)
            act_ref[rows, :] = jnp.concatenate(acts, axis=0).astype(BF16)
            dg_ref[rows, :] = jnp.concatenate(dgs, axis=0).astype(BF16)
            dv_ref[rows, :] = jnp.concatenate(dvs, axis=0).astype(BF16)

    prev = lambda off: (lambda i, j: (jnp.maximum(i * nb - 1, 0), j + off))
    out = pl.BlockSpec((ts, tc), lambda i, j: (i, j))
    return _call(body, grid=(S // ts, nj),
                 in_specs=[pl.BlockSpec((ts, tc), lambda i, j: (i, j)), pl.BlockSpec((ts, tc), lambda i, j: (i, j + nj)),
                           pl.BlockSpec((HB, tc), prev(0)), pl.BlockSpec((HB, tc), prev(nj)),
                           pl.BlockSpec((3, tc), lambda i, j: (0, j)), pl.BlockSpec((3, tc), lambda i, j: (0, j + nj)),
                           pl.BlockSpec((1, tc), lambda i, j: (0, j)), pl.BlockSpec((1, tc), lambda i, j: (0, j + nj))],
                 out_specs=(out, out, out), out_shape=(_sds((S, F), BF16),) * 3, name=name)(
        hu, hu, hu, hu, cw, cw, cb.reshape(1, F2), cb.reshape(1, F2))


def _ffn_act_bwd(hu, dact, dact_dg, dact_dv, cw, *, name, ts, tc):
    S, F2 = hu.shape
    F = F2 // 2
    nj = F // tc
    HB = FFN_HALO
    HB16 = 16
    n = S // ts

    def body(g_ref, v_ref, lg_ref, lv_ref, lgn_ref, lvn_ref, da_ref, dan_ref, wg_ref, wv_ref,
             dhg_ref, dhv_ref, dwg_ref, dwv_ref, dbg_ref, dbv_ref, dug_ref, duv_ref, accg, accv):
        i = pl.program_id(1)
        RC = FFN_ROWS
        npairs = ts // RC
        last_tile = i == n - 1

        @pl.when(i == 0)
        def _():
            accg[...] = jnp.zeros_like(accg)
            accv[...] = jnp.zeros_like(accv)

        for l_ref, ln_ref, h_ref, w_ref, dh_ref, acc in ((lg_ref, lgn_ref, g_ref, wg_ref, dhg_ref, accg),
                                                          (lv_ref, lvn_ref, v_ref, wv_ref, dhv_ref, accv)):
            for c0 in range(0, tc, FFN_STRIP):
                cols = slice(c0, min(c0 + FFN_STRIP, tc))
                sw = cols.stop - cols.start
                row = lax.broadcasted_iota(jnp.int32, (HB, sw), 0)
                w0, w1, w2 = w_ref[0:1, cols], w_ref[1:2, cols], w_ref[2:3, cols]

                def d_rows(p):
                    rows = slice(p * RC, (p + 1) * RC)
                    return da_ref[rows, cols] * l_ref[rows, cols].astype(F32)

                after = jnp.where(last_tile, 0.0, dan_ref[:, cols]) * ln_ref[:, cols].astype(F32)[0:HB, :]
                pair = d_rows(0)
                cur = pair[0:HB, :]
                c7, c6 = pltpu.roll(cur, HB - 1, 0), pltpu.roll(cur, HB - 2, 0)
                sums = [None] * 5
                for p in range(npairs):
                    nxt_pair = d_rows(p + 1) if p + 1 < npairs else None
                    dhus = []
                    for half, nxt_c in enumerate((pair[HB:RC, :], after if nxt_pair is None else nxt_pair[0:HB, :])):
                        c = 2 * p + half
                        n7, n6 = pltpu.roll(nxt_c, HB - 1, 0), pltpu.roll(nxt_c, HB - 2, 0)
                        taps = [jnp.where(row >= HB - 2, n6, c6), jnp.where(row >= HB - 1, n7, c7), cur]
                        dhu = w2 * taps[2] + w1 * taps[1] + w0 * taps[0]
                        dhus.append(dhu)
                        h = h_ref[c * HB:(c + 1) * HB, cols]
                        parts = [h * taps[0], h * taps[1], h * taps[2], taps[2], dhu]
                        sums = [q if s is None else s + q for s, q in zip(sums, parts)]
                        cur, c7, c6 = nxt_c, n7, n6
                    dh_ref[p * RC:(p + 1) * RC, cols] = jnp.concatenate(dhus, axis=0).astype(BF16)
                    pair = nxt_pair
                for k in range(5):
                    acc[8 * k:8 * k + 8, cols] += sums[k]

        @pl.when(i == n - 1)
        def _():
            for acc, dw_ref, db_ref, du_ref in ((accg, dwg_ref, dbg_ref, dug_ref), (accv, dwv_ref, dbv_ref, duv_ref)):
                for k in range(3):
                    dw_ref[k:k + 1, :] = _colsum(acc[8 * k:8 * k + 8, :])
                db_ref[...] = _colsum(acc[24:32, :])
                du_ref[...] = _colsum(acc[32:40, :])

    nxt = lambda hb: (lambda j, i: (jnp.minimum((i + 1) * (ts // hb), S // hb - 1), j))
    tile = lambda off: (lambda j, i: (i, j + off))
    vec = lambda rows: pl.BlockSpec((rows, tc), lambda j, i: (0, j))
    return _call(body, grid=(nj, n),
                 in_specs=[pl.BlockSpec((ts, tc), tile(0)), pl.BlockSpec((ts, tc), tile(nj)),
                           pl.BlockSpec((ts, tc), tile(0)), pl.BlockSpec((ts, tc), tile(0)),
                           pl.BlockSpec((HB16, tc), nxt(HB16)), pl.BlockSpec((HB16, tc), nxt(HB16)),
                           pl.BlockSpec((ts, tc), tile(0)), pl.BlockSpec((HB, tc), nxt(HB)),
                           pl.BlockSpec((3, tc), lambda j, i: (0, j)), pl.BlockSpec((3, tc), lambda j, i: (0, j + nj))],
                 out_specs=(pl.BlockSpec((ts, tc), tile(0)), pl.BlockSpec((ts, tc), tile(0)),
                            vec(3), vec(3), vec(1), vec(1), vec(1), vec(1)),
                 out_shape=(_sds((S, F), BF16), _sds((S, F), BF16), _sds((3, F), F32), _sds((3, F), F32),
                            _sds((1, F), F32), _sds((1, F), F32), _sds((1, F), F32), _sds((1, F), F32)),
                 scratch_shapes=[pltpu.VMEM((40, tc), F32), pltpu.VMEM((40, tc), F32)], name=name)(
        hu, hu, dact_dg, dact_dv, dact_dg, dact_dv, dact, dact, cw, cw)


def _t5_bucket(dist):
    max_exact = N_BUCKETS // 2
    d = np.maximum(dist, 1).astype(np.float64)
    large = max_exact + (np.log(d / max_exact) / math.log(MAX_DISTANCE / max_exact)
                         * (N_BUCKETS - max_exact)).astype(np.int32)
    large = np.minimum(large, N_BUCKETS - 1)
    return np.where(dist < max_exact, dist, large).astype(np.int32)


def _pattern_tables(window, dilation):
    qi = np.arange(ATTN_BLOCK)[:, None]
    kj = np.arange(2 * ATTN_BLOCK)[None, :]
    dist = qi + ATTN_BLOCK - kj
    valid = (dist >= 0) & (dist <= window // dilation)
    bucket = _t5_bucket(np.clip(dist, 0, None) * dilation)
    return bucket, valid


def _dilate_qkv(qkv, *, name, ts):
    S, C = qkv.shape
    dils = [d for (_, d) in PATTERNS if d > 1]

    def body(x_ref, nat_ref, *rest):
        outs, tmp = rest[:-1], rest[-1]
        nat_ref[...] = x_ref[...].astype(BF16)
        for j in range(C // LANES):
            cols = slice(j * LANES, (j + 1) * LANES)
            tmp[...] = x_ref[:, cols]
            for d, o_ref in zip(dils, outs):
                _dilate(tmp, o_ref, cols, d, ts, BF16)

    out_shape = (_sds((S, C), BF16),) + tuple(_sds((d, S // d, C), BF16) for d in dils)
    out_specs = (_dil_spec(1, ts, C),) + tuple(_dil_spec(d, ts, C) for d in dils)
    res = _call(body, grid=(S // ts,), in_specs=[_dil_spec(1, ts, C)], out_specs=out_specs, out_shape=out_shape,
                scratch_shapes=[pltpu.VMEM((ts, LANES), F32)], name=name)(qkv)
    return [res[0]] + [r.reshape(S, C) for r in res[1:]]


def _attn_group(S):
    nb_min = (S // PATTERNS[-1][1]) // ATTN_BLOCK
    return math.gcd(8, nb_min)


def _with_prev_block(ref, halo_ref, bi, B):
    if bi == 0:
        return jnp.concatenate([halo_ref[...], ref[0:B, :]], axis=0)
    return ref[(bi - 1) * B:(bi + 1) * B, :]


def _stack_heads(x, head0):
    zero = jnp.zeros_like(x)
    return jnp.concatenate([jnp.where(head0, x, zero), jnp.where(head0, zero, x)], axis=0)


def _attn_fwd(qkv, bias, *, name, nb, G):
    S = qkv.shape[0]
    B, HD = ATTN_BLOCK, HEAD_DIM
    GR = G * B
    ng = S // GR

    def body(q_ref, k_ref, kh_ref, v_ref, vh_ref, bias_ref, o_ref, lse_ref):
        g = pl.program_id(1)
        halo_ok = (g * G) % nb != 0
        col = lax.broadcasted_iota(jnp.int32, (B, 2 * B), 1)
        head0 = lax.broadcasted_iota(jnp.int32, (B, LANES), 1) < HD

        for bi in range(G):
            r0 = bi * B
            q2 = q_ref[r0:r0 + B, :]
            kk = _with_prev_block(k_ref, kh_ref, bi, B)
            vv = _with_prev_block(v_ref, vh_ref, bi, B)
            s_st = lax.dot_general(_stack_heads(q2, head0), kk, (((1,), (1,)), ((), ())), preferred_element_type=F32)
            ps, ls_, lses = [], [], []
            for hh in range(2):
                s = s_st[hh * B:(hh + 1) * B, :] + bias_ref[hh]
                if bi == 0:
                    s = jnp.where(jnp.logical_and(jnp.logical_not(halo_ok), col < B), NEG, s)
                m = jnp.max(s, axis=1, keepdims=True)
                p = jnp.exp(s - m)
                l = jnp.sum(p, axis=1, keepdims=True)
                ps.append(p.astype(BF16))
                ls_.append(l)
                lses.append(m + jnp.log(l))
            o_st = jnp.dot(jnp.concatenate(ps, axis=0), vv, preferred_element_type=F32)
            o_ref[r0:r0 + B, :] = (jnp.where(head0, o_st[0:B, :], o_st[B:2 * B, :])
                                   / jnp.where(head0, ls_[0], ls_[1])).astype(BF16)
            lse_ref[r0:r0 + B, :] = jnp.where(head0, lses[0], lses[1])

    halo = lambda off: (lambda hp, g: (jnp.maximum(g * G - 1, 0), off + hp))
    main = lambda off: (lambda hp, g: (g, off + hp))
    return _call(body, grid=(4, ng),
                 in_specs=[pl.BlockSpec((GR, LANES), main(0)), pl.BlockSpec((GR, LANES), main(4)),
                           pl.BlockSpec((B, LANES), halo(4)), pl.BlockSpec((GR, LANES), main(8)),
                           pl.BlockSpec((B, LANES), halo(8)), pl.BlockSpec((2, B, 2 * B), lambda hp, g: (hp, 0, 0))],
                 out_specs=(pl.BlockSpec((GR, LANES), main(0)), pl.BlockSpec((GR, LANES), main(0))),
                 out_shape=(_sds((S, ATTN_CH), BF16), _sds((S, ATTN_CH), F32)), name=name)(
        qkv, qkv, qkv, qkv, qkv, bias)


def _dil_spec(d, ts, C):
    if d == 1:
        return pl.BlockSpec((ts, C), lambda i: (i, 0))
    return pl.BlockSpec((d, ts // d, C), lambda i: (0, i, 0))


def _dil_view(a, d):
    return a if d == 1 else a.reshape(d, a.shape[0] // d, a.shape[1])


def _dilate(nat_tmp, dst_ref, cols, d, ts, dtype=F32):
    for r in range(d):
        dst_ref[r, :, cols] = nat_tmp[pl.ds(r, ts // d, stride=d), :].astype(dtype)


def _undilate(src_ref, nat_tmp, cols, d, ts, accumulate=False):
    for r in range(d):
        rows = pl.ds(r, ts // d, stride=d)
        if accumulate:
            nat_tmp[rows, :] = nat_tmp[rows, :] + src_ref[r, :, cols].astype(F32)
        else:
            nat_tmp[rows, :] = src_ref[r, :, cols].astype(F32)


def _attn_merge(o_list, lse_list, dils, *, name, ts):
    S, C = o_list[0].shape
    P = len(o_list)
    nd = sum(1 for d in dils if d > 1)

    def body(*refs):
        o_refs, l_refs = refs[:P], refs[P:2 * P]
        out_ref, lse_ref = refs[2 * P], refs[2 * P + 1]
        lse_d_refs = refs[2 * P + 2:2 * P + 2 + nd]
        scratch = list(refs[2 * P + 2 + nd:])
        tmp = scratch.pop()
        for j in range(C // LANES):
            cols = slice(j * LANES, (j + 1) * LANES)
            os_, ls, free = [], [], list(scratch)
            for o_ref, l_ref, d in zip(o_refs, l_refs, dils):
                if d > 1:
                    so, sl = free.pop(0), free.pop(0)
                    _undilate(o_ref, so, cols, d, ts)
                    _undilate(l_ref, sl, cols, d, ts)
                    os_.append(so[...])
                    ls.append(sl[...])
                else:
                    os_.append(o_ref[:, cols].astype(F32))
                    ls.append(l_ref[:, cols])
            m = ls[0]
            for l in ls[1:]:
                m = jnp.maximum(m, l)
            ws = [jnp.exp(l - m) for l in ls]
            den = ws[0]
            for w in ws[1:]:
                den = den + w
            num = ws[0] * os_[0]
            for w, o in zip(ws[1:], os_[1:]):
                num = num + w * o
            out_ref[:, cols] = num / den
            tmp[...] = m + jnp.log(den)
            lse_ref[:, cols] = tmp[...]
            for l_out, d in zip(lse_d_refs, [d for d in dils if d > 1]):
                _dilate(tmp, l_out, cols, d, ts)

    row = _dil_spec(1, ts, C)
    dd = [d for d in dils if d > 1]
    res = _call(body, grid=(S // ts,), in_specs=[_dil_spec(d, ts, C) for d in dils] * 2,
                out_specs=(row, row) + tuple(_dil_spec(d, ts, C) for d in dd),
                out_shape=(_sds((S, C), F32), _sds((S, C), F32)) + tuple(_sds((d, S // d, C), F32) for d in dd),
                scratch_shapes=[pltpu.VMEM((ts, LANES), F32)] * (2 * nd + 1), name=name)(
        *[_dil_view(o, d) for o, d in zip(o_list, dils)], *[_dil_view(l, d) for l, d in zip(lse_list, dils)])
    lse_by_d = {1: res[1]}
    lse_by_d.update({d: r.reshape(S, C) for d, r in zip(dd, res[2:])})
    return res[0], [lse_by_d[d] for d in dils]


def _attn_prep(dout, out, dils, *, name, ts):
    S, C = out.shape
    HD = HEAD_DIM
    dd = [d for d in dils if d > 1]
    nd = len(dd)

    def body(do_ref, o_ref, d_ref, dob_ref, *rest):
        outs, tmp_d, tmp_o = rest[:2 * nd], rest[2 * nd], rest[2 * nd + 1]
        dob_ref[...] = do_ref[...].astype(BF16)
        for j in range(C // LANES):
            cols = slice(j * LANES, (j + 1) * LANES)
            do = do_ref[:, cols]
            prod = do * o_ref[:, cols]
            tmp_o[...] = do
            for h in range(LANES // HD):
                cs = slice(h * HD, (h + 1) * HD)
                tmp_d[:, cs] = jnp.broadcast_to(jnp.sum(prod[:, cs], axis=1, keepdims=True), (ts, HD))
            d_ref[:, cols] = tmp_d[...]
            for d, dd_out, do_out in zip(dd, outs[:nd], outs[nd:]):
                _dilate(tmp_d, dd_out, cols, d, ts)
                _dilate(tmp_o, do_out, cols, d, ts, BF16)

    row = _dil_spec(1, ts, C)
    res = _call(body, grid=(S // ts,), in_specs=[row, row],
                out_specs=(row, row) + tuple(_dil_spec(d, ts, C) for d in dd) * 2,
                out_shape=(_sds((S, C), F32), _sds((S, C), BF16)) + tuple(_sds((d, S // d, C), F32) for d in dd)
                + tuple(_sds((d, S // d, C), BF16) for d in dd),
                scratch_shapes=[pltpu.VMEM((ts, LANES), F32)] * 2, name=name)(dout, out)
    dd_by_d, do_by_d = {1: res[0]}, {1: res[1]}
    dd_by_d.update({d: r.reshape(S, C) for d, r in zip(dd, res[2:2 + nd])})
    do_by_d.update({d: r.reshape(S, C) for d, r in zip(dd, res[2 + nd:])})
    return [dd_by_d[d] for d in dils], [do_by_d[d] for d in dils]


def _attn_bwd(qkv, do, lse, dd, bias, *, name, nb, G):
    S = qkv.shape[0]
    B, HD = ATTN_BLOCK, HEAD_DIM
    GR = G * B
    ng = S // GR
    nblk = S // B

    def body(q_ref, k_ref, kh_ref, v_ref, vh_ref, do_ref, lse_ref, dd_ref, qn_ref, don_ref, lsen_ref, ddn_ref,
             bias_ref, dq_ref, dk_ref, dv_ref, dbias_ref, ds_st, p_st):
        g = pl.program_id(1)
        halo_ok = (g * G) % nb != 0
        next_ok = jnp.logical_and(((g + 1) * G) % nb != 0, g < ng - 1)
        col = lax.broadcasted_iota(jnp.int32, (B, 2 * B), 1)

        def tn_dot(a, b):
            return lax.dot_general(a, b, (((0,), (0,)), ((), ())), preferred_element_type=F32)

        @pl.when(g == 0)
        def _():
            dbias_ref[...] = jnp.zeros_like(dbias_ref)

        head0 = lax.broadcasted_iota(jnp.int32, (B, LANES), 1) < HD

        def nt_dot(a, b):
            return lax.dot_general(a, b, (((1,), (1,)), ((), ())), preferred_element_type=F32)

        def one_head(x, hh):
            zero = jnp.zeros_like(x)
            return jnp.where(head0, x, zero) if hh == 0 else jnp.where(head0, zero, x)

        def pick(per_head):
            return jnp.where(head0, per_head[0], per_head[1])

        for bi in range(G):
            r0 = bi * B
            q2 = q_ref[r0:r0 + B, :]
            do2 = do_ref[r0:r0 + B, :]
            kk = _with_prev_block(k_ref, kh_ref, bi, B)
            vv = _with_prev_block(v_ref, vh_ref, bi, B)
            dq = []
            for hh in range(2):
                s = nt_dot(one_head(q2, hh), kk) + bias_ref[hh]
                if bi == 0:
                    s = jnp.where(jnp.logical_and(jnp.logical_not(halo_ok), col < B), NEG, s)
                p = jnp.exp(s - lse_ref[r0:r0 + B, hh * HD:hh * HD + 1])
                dp = nt_dot(one_head(do2, hh), vv)
                ds = p * (dp - dd_ref[r0:r0 + B, hh * HD:hh * HD + 1])
                dbias_ref[hh] += ds
                ds_bf, p_bf = ds.astype(BF16), p.astype(BF16)
                dq.append(jnp.dot(ds_bf, kk, preferred_element_type=F32))
                ds_st[hh, 2 * r0:2 * r0 + B, :] = ds_bf[:, B:]
                p_st[hh, 2 * r0:2 * r0 + B, :] = p_bf[:, B:]
                if bi > 0:
                    ds_st[hh, 2 * r0 - B:2 * r0, :] = ds_bf[:, :B]
                    p_st[hh, 2 * r0 - B:2 * r0, :] = p_bf[:, :B]
            dq_ref[r0:r0 + B, :] = pick(dq).astype(BF16)

        @pl.when(next_ok)
        def _():
            qn = qn_ref[...]
            don = don_ref[...]
            kl = k_ref[GR - B:GR, :]
            vl = v_ref[GR - B:GR, :]
            for hh in range(2):
                s = nt_dot(one_head(qn, hh), kl) + bias_ref[hh, :, 0:B]
                p = jnp.exp(s - lsen_ref[:, hh * HD:hh * HD + 1])
                dp = nt_dot(one_head(don, hh), vl)
                ds = p * (dp - ddn_ref[:, hh * HD:hh * HD + 1])
                ds_st[hh, 2 * GR - B:2 * GR, :] = ds.astype(BF16)
                p_st[hh, 2 * GR - B:2 * GR, :] = p.astype(BF16)

        @pl.when(jnp.logical_not(next_ok))
        def _():
            for hh in range(2):
                ds_st[hh, 2 * GR - B:2 * GR, :] = jnp.zeros((B, B), BF16)
                p_st[hh, 2 * GR - B:2 * GR, :] = jnp.zeros((B, B), BF16)

        for j in range(G):
            r0 = j * B
            if j < G - 1:
                q_pair, do_pair = q_ref[r0:r0 + 2 * B, :], do_ref[r0:r0 + 2 * B, :]
            else:
                q_pair = jnp.concatenate([q_ref[r0:r0 + B, :], qn_ref[...]], axis=0)
                do_pair = jnp.concatenate([do_ref[r0:r0 + B, :], don_ref[...]], axis=0)
            dk_ref[r0:r0 + B, :] = pick([tn_dot(ds_st[hh, 2 * r0:2 * r0 + 2 * B, :], q_pair)
                                         for hh in range(2)]).astype(BF16)
            dv_ref[r0:r0 + B, :] = pick([tn_dot(p_st[hh, 2 * r0:2 * r0 + 2 * B, :], do_pair)
                                         for hh in range(2)]).astype(BF16)

    halo = lambda off: (lambda hp, g: (jnp.maximum(g * G - 1, 0), off + hp))
    main = lambda off: (lambda hp, g: (g, off + hp))
    nxt = lambda off: (lambda hp, g: (jnp.minimum((g + 1) * G, nblk - 1), off + hp))
    big, small = (lambda m: pl.BlockSpec((GR, LANES), m)), (lambda m: pl.BlockSpec((B, LANES), m))
    return _call(body, grid=(4, ng),
                 in_specs=[big(main(0)), big(main(4)), small(halo(4)), big(main(8)), small(halo(8)),
                           big(main(0)), big(main(0)), big(main(0)),
                           small(nxt(0)), small(nxt(0)), small(nxt(0)), small(nxt(0)),
                           pl.BlockSpec((2, B, 2 * B), lambda hp, g: (hp, 0, 0))],
                 out_specs=(big(main(0)), big(main(0)), big(main(0)),
                            pl.BlockSpec((2, B, 2 * B), lambda hp, g: (hp, 0, 0))),
                 out_shape=(_sds((S, ATTN_CH), BF16),) * 3 + (_sds((ATTN_HEADS, B, 2 * B), F32),),
                 scratch_shapes=[pltpu.VMEM((2, 2 * GR, B), BF16)] * 2, name=name)(
        qkv, qkv, qkv, qkv, qkv, do, lse, dd, qkv, do, lse, dd, bias)


def _attn_combine(dq_list, dk_list, dv_list, dils, *, name, ts):
    S, C = dq_list[0].shape
    P = len(dq_list)
    scale = HEAD_DIM ** -0.5
    assert dils[0] == 1

    def body(*refs):
        out_refs, acc = refs[3 * P:3 * P + 3], refs[3 * P + 3]
        for part in range(3):
            for j in range(C // LANES):
                cols = slice(j * LANES, (j + 1) * LANES)
                acc[...] = refs[part * P][:, cols].astype(F32)
                for r, d in zip(refs[part * P + 1:(part + 1) * P], dils[1:]):
                    _undilate(r, acc, cols, d, ts, accumulate=True)
                tot = acc[...]
                if part == 0:
                    tot = tot * scale
                out_refs[part][:, cols] = tot.astype(BF16)

    return _call(body, grid=(S // ts,), in_specs=[_dil_spec(d, ts, C) for d in dils] * 3,
                 out_specs=(_dil_spec(1, ts, C),) * 3, out_shape=(_sds((S, C), BF16),) * 3,
                 scratch_shapes=[pltpu.VMEM((ts, LANES), F32)], name=name)(
        *[_dil_view(a, d) for lst in (dq_list, dk_list, dv_list) for a, d in zip(lst, dils)])


def _bias_tables(table, bucket_flat, *, name):
    P, _, K = bucket_flat.shape
    H = table.shape[1]
    KC = 4096

    def body(t_ref, bk_ref, out_ref):
        row = lax.broadcasted_iota(jnp.int32, (N_BUCKETS, KC), 0)
        for c in range(K // KC):
            bk = bk_ref[0, :, c * KC:(c + 1) * KC]
            onehot = (row == bk).astype(F32)
            vals = jnp.dot(t_ref[...], onehot, preferred_element_type=F32, precision=lax.Precision.HIGHEST)
            out_ref[0, :, c * KC:(c + 1) * KC] = jnp.where(bk >= 0, vals, NEG)

    return _call(body, grid=(P,),
                 in_specs=[pl.BlockSpec((H, N_BUCKETS), lambda p: (0, 0)), pl.BlockSpec((1, 1, K), lambda p: (p, 0, 0))],
                 out_specs=pl.BlockSpec((1, H, K), lambda p: (p, 0, 0)), out_shape=_sds((P, H, K), F32),
                 name=name)(table.T, bucket_flat)


def _bias_grad(dbias_flat, bucket_flat, *, name):
    P, H, K = dbias_flat.shape
    KC = 4096

    def body(db_ref, bk_ref, out_ref):
        p = pl.program_id(0)
        acc = jnp.zeros((N_BUCKETS, H), F32)
        row = lax.broadcasted_iota(jnp.int32, (N_BUCKETS, KC), 0)
        for c in range(K // KC):
            onehot = (row == bk_ref[0, :, c * KC:(c + 1) * KC]).astype(F32)
            acc = acc + lax.dot_general(onehot, db_ref[0, :, c * KC:(c + 1) * KC], (((1,), (1,)), ((), ())),
                                        preferred_element_type=F32, precision=lax.Precision.HIGHEST)

        @pl.when(p == 0)
        def _():
            out_ref[...] = acc

        @pl.when(p > 0)
        def _():
            out_ref[...] += acc

    return _call(body, grid=(P,),
                 in_specs=[pl.BlockSpec((1, H, K), lambda p: (p, 0, 0)), pl.BlockSpec((1, 1, K), lambda p: (p, 0, 0))],
                 out_specs=pl.BlockSpec((N_BUCKETS, H), lambda p: (0, 0)), out_shape=_sds((N_BUCKETS, H), F32),
                 name=name)(dbias_flat, bucket_flat)


def _allgather(blocks, *, name):
    n = len(blocks)

    def body(*refs):
        x_refs, out_refs = refs[:n], refs[n:2 * n]
        send_sems, recv_sems, local_sems = refs[2 * n:]
        x, y, c = lax.axis_index("x"), lax.axis_index("y"), lax.axis_index("c")
        me, sibling = (x, y, c), (x, y, 1 - c)
        chips = [(1 - x, y), (x, 1 - y), (1 - x, 1 - y)]

        def copy(i, k, blk, to, own=False):
            slot = out_refs[i].at[4 * blk[0] + 2 * blk[1] + blk[2]]
            return pltpu.make_async_remote_copy(src_ref=x_refs[i] if own else slot, dst_ref=slot,
                                                send_sem=send_sems.at[7 * i + k], recv_sem=recv_sems.at[7 * i + k],
                                                device_id=to, device_id_type=MESH)

        mine = [pltpu.make_async_copy(x_refs[i], out_refs[i].at[4 * x + 2 * y + c], local_sems.at[i])
                for i in range(n)]
        for cp in mine:
            cp.start()
        first = []
        for i in range(n):
            first.append(copy(i, 0, me, sibling, own=True))
            first += [copy(i, 1 + j, me, (*chip, c), own=True) for j, chip in enumerate(chips)]
        for cp in first:
            cp.start()
        passed = []
        for j, chip in enumerate(chips):
            for i in range(n):
                copy(i, 1 + j, (*chip, c), me).wait_recv()
                fwd = copy(i, 4 + j, (*chip, c), sibling)
                fwd.start()
                passed.append(fwd)
        for i in range(n):
            copy(i, 0, sibling, me).wait_recv()
            for j, chip in enumerate(chips):
                copy(i, 4 + j, (*chip, 1 - c), me).wait_recv()
        for cp in first + passed:
            cp.wait_send()
        for cp in mine:
            cp.wait()

    any_spec = pl.BlockSpec(memory_space=pl.ANY)
    return pl.pallas_call(body, out_shape=tuple(_sds((N_DEV,) + b.shape, b.dtype) for b in blocks),
                          in_specs=[any_spec] * n, out_specs=(any_spec,) * n,
                          scratch_shapes=[pltpu.SemaphoreType.DMA((7 * n,)), pltpu.SemaphoreType.DMA((7 * n,)),
                                          pltpu.SemaphoreType.DMA((n,))], name=name)(*blocks)


def _exchange(sends, *, name):
    n = len(sends)

    def body(*refs):
        send_refs, recv_refs = refs[:n], refs[n:2 * n]
        send_sems, recv_sems, local_sems = refs[2 * n:]
        x, y, c = lax.axis_index("x"), lax.axis_index("y"), lax.axis_index("c")
        me = 4 * x + 2 * y + c
        mine = [pltpu.make_async_copy(send_refs[i].at[me], recv_refs[i].at[me], local_sems.at[i]) for i in range(n)]
        for cp in mine:
            cp.start()
        copies = []
        for k in range(1, N_DEV):
            px = 1 - x if k & 4 else x
            py = 1 - y if k & 2 else y
            pc = 1 - c if k & 1 else c
            for i in range(n):
                cp = pltpu.make_async_remote_copy(src_ref=send_refs[i].at[4 * px + 2 * py + pc],
                                                  dst_ref=recv_refs[i].at[me],
                                                  send_sem=send_sems.at[7 * i + k - 1],
                                                  recv_sem=recv_sems.at[7 * i + k - 1],
                                                  device_id=(px, py, pc), device_id_type=MESH)
                cp.start()
                copies.append(cp)
        for cp in copies:
            cp.wait_recv()
        for cp in copies:
            cp.wait_send()
        for cp in mine:
            cp.wait()

    any_spec = pl.BlockSpec(memory_space=pl.ANY)
    return pl.pallas_call(body, out_shape=tuple(_sds(s.shape, s.dtype) for s in sends), in_specs=[any_spec] * n,
                          out_specs=(any_spec,) * n,
                          scratch_shapes=[pltpu.SemaphoreType.DMA((7 * n,)), pltpu.SemaphoreType.DMA((7 * n,)),
                                          pltpu.SemaphoreType.DMA((n,))], name=name)(*sends)


def _adamw(w, m, v, g_parts, *, name, tr):
    R, W = w.shape
    bc1 = 1.0 - ADAM_B1 ** ADAM_STEP
    bc2 = 1.0 - ADAM_B2 ** ADAM_STEP

    def body(w_ref, m_ref, v_ref, g_ref, go_ref, d_ref, mo_ref, vo_ref):
        g = g_ref[0].astype(F32)
        for i in range(1, N_DEV):
            g = g + g_ref[i].astype(F32)
        mn = ADAM_B1 * m_ref[...] + (1.0 - ADAM_B1) * g
        vn = ADAM_B2 * v_ref[...] + (1.0 - ADAM_B2) * (g * g)
        m_hat = mn / bc1
        v_hat = vn / bc2
        go_ref[...] = g
        d_ref[...] = -ADAM_LR * (m_hat / (jnp.sqrt(v_hat) + ADAM_EPS) + ADAM_WD * w_ref[...])
        mo_ref[...] = mn
        vo_ref[...] = vn

    row = pl.BlockSpec((tr, W), lambda i: (i, 0))
    return _call(body, grid=(R // tr,), in_specs=[row, row, row, pl.BlockSpec((N_DEV, tr, W), lambda i: (0, i, 0))],
                 out_specs=(row,) * 4, out_shape=(_sds((R, W), F32),) * 4, name=name)(w, m, v, g_parts)


def _round_up(n, k):
    return -(-n // k) * k


def _pack(arrs, width, row_mult):
    pieces, offs, r = [], [], 0
    for a in arrs:
        n = a.size
        rows = _round_up(n, width) // width
        flat = a.reshape(-1)
        if rows * width != n:
            flat = jnp.pad(flat, (0, rows * width - n))
        pieces.append(flat.reshape(rows, width))
        offs.append((r, rows, n))
        r += rows
    total = _round_up(r, row_mult)
    if total != r:
        pieces.append(jnp.zeros((total - r, width), pieces[0].dtype))
    return jnp.concatenate(pieces, axis=0), offs


def _unpack(pack, offs, shapes):
    out = []
    for (r, rows, n), shp in zip(offs, shapes):
        out.append(pack[r:r + rows].reshape(-1)[:n].reshape(shp))
    return out


def _gather_axis(full8, axis):
    moved = jnp.moveaxis(full8, 0, axis)
    shp = list(moved.shape)
    shp[axis:axis + 2] = [shp[axis] * shp[axis + 1]]
    return moved.reshape(shp)


def _split_axis(full, axis):
    shp = list(full.shape)
    shp[axis:axis + 1] = [N_DEV, shp[axis] // N_DEV]
    return jnp.moveaxis(full.reshape(shp), axis, 0)


def kernel(x, w_in, b_in, conv_dw_w, conv_dw_b, conv_ln_g, conv_ln_b, rel_bias_table, gmlp_ln_g, gmlp_ln_b, gmlp_w_s, gmlp_b_s, w_out, b_out, ln1_g, ln1_b, ffn_w_up, ffn_b_up, ffn_conv_w, ffn_conv_b, ffn_w_down, ffn_b_down, ln2_g, ln2_b, loss_target, m_w_in, m_b_in, m_conv_dw_w, m_conv_dw_b, m_conv_ln_g, m_conv_ln_b, m_rel_bias_table, m_gmlp_ln_g, m_gmlp_ln_b, m_gmlp_w_s, m_gmlp_b_s, m_w_out, m_b_out, m_ln1_g, m_ln1_b, m_ffn_w_up, m_ffn_b_up, m_ffn_conv_w, m_ffn_conv_b, m_ffn_w_down, m_ffn_b_down, m_ln2_g, m_ln2_b, v_w_in, v_b_in, v_conv_dw_w, v_conv_dw_b, v_conv_ln_g, v_conv_ln_b, v_rel_bias_table, v_gmlp_ln_g, v_gmlp_ln_b, v_gmlp_w_s, v_gmlp_b_s, v_w_out, v_b_out, v_ln1_g, v_ln1_b, v_ffn_w_up, v_ffn_b_up, v_ffn_conv_w, v_ffn_conv_b, v_ffn_w_down, v_ffn_b_down, v_ln2_g, v_ln2_b):
    W = dict(w_in=w_in, b_in=b_in, conv_dw_w=conv_dw_w, conv_dw_b=conv_dw_b, conv_ln_g=conv_ln_g,
             conv_ln_b=conv_ln_b, rel_bias_table=rel_bias_table, gmlp_ln_g=gmlp_ln_g, gmlp_ln_b=gmlp_ln_b,
             gmlp_w_s=gmlp_w_s, gmlp_b_s=gmlp_b_s, w_out=w_out, b_out=b_out, ln1_g=ln1_g, ln1_b=ln1_b,
             ffn_w_up=ffn_w_up, ffn_b_up=ffn_b_up, ffn_conv_w=ffn_conv_w, ffn_conv_b=ffn_conv_b,
             ffn_w_down=ffn_w_down, ffn_b_down=ffn_b_down, ln2_g=ln2_g, ln2_b=ln2_b)
    Mo = dict(w_in=m_w_in, b_in=m_b_in, conv_dw_w=m_conv_dw_w, conv_dw_b=m_conv_dw_b, conv_ln_g=m_conv_ln_g,
              conv_ln_b=m_conv_ln_b, rel_bias_table=m_rel_bias_table, gmlp_ln_g=m_gmlp_ln_g, gmlp_ln_b=m_gmlp_ln_b,
              gmlp_w_s=m_gmlp_w_s, gmlp_b_s=m_gmlp_b_s, w_out=m_w_out, b_out=m_b_out, ln1_g=m_ln1_g, ln1_b=m_ln1_b,
              ffn_w_up=m_ffn_w_up, ffn_b_up=m_ffn_b_up, ffn_conv_w=m_ffn_conv_w, ffn_conv_b=m_ffn_conv_b,
              ffn_w_down=m_ffn_w_down, ffn_b_down=m_ffn_b_down, ln2_g=m_ln2_g, ln2_b=m_ln2_b)
    Vo = dict(w_in=v_w_in, b_in=v_b_in, conv_dw_w=v_conv_dw_w, conv_dw_b=v_conv_dw_b, conv_ln_g=v_conv_ln_g,
              conv_ln_b=v_conv_ln_b, rel_bias_table=v_rel_bias_table, gmlp_ln_g=v_gmlp_ln_g, gmlp_ln_b=v_gmlp_ln_b,
              gmlp_w_s=v_gmlp_w_s, gmlp_b_s=v_gmlp_b_s, w_out=v_w_out, b_out=v_b_out, ln1_g=v_ln1_g, ln1_b=v_ln1_b,
              ffn_w_up=v_ffn_w_up, ffn_b_up=v_ffn_b_up, ffn_conv_w=v_ffn_conv_w, ffn_conv_b=v_ffn_conv_b,
              ffn_w_down=v_ffn_w_down, ffn_b_down=v_ffn_b_down, ln2_g=v_ln2_g, ln2_b=v_ln2_b)

    xs = x[0]
    target = loss_target[0]
    S, D = xs.shape
    F2 = ffn_b_up.shape[1]
    F = F2 // 2
    ts = min(512, S)
    G = _attn_group(S)
    tc = F // 2 if (F // 2) % LANES == 0 else F

    mat_names = SHARDED[:4]
    payload = [W[n].astype(BF16) if n in mat_names else W[n] for n in SHARDED]
    wall = _allgather(payload, name="weight_allgather")
    full = {n: _gather_axis(parts, SHARD_AXIS[n]) for n, parts in zip(SHARDED, wall)}

    tables = [_pattern_tables(w, d) for (w, d) in PATTERNS]
    bucket_flat = jnp.asarray(np.stack([np.where(v, b, -1).reshape(1, -1) for (b, v) in tables]).astype(np.int32))
    bias_all = _bias_tables(rel_bias_table, bucket_flat, name="bias_tables")
    biases = [bias_all[p].reshape(ATTN_HEADS, ATTN_BLOCK, 2 * ATTN_BLOCK) for p in range(len(PATTERNS))]
    nbs = [(S // d) // ATTN_BLOCK for (_, d) in PATTERNS]
    dils = [d for (_, d) in PATTERNS]
    scale = HEAD_DIM ** -0.5

    saved = []
    cur = xs
    for l in range(DEPTH):
        Win, Wout, Wup, Wdown = full['w_in'][l], full['w_out'][l], full['ffn_w_up'][l], full['ffn_w_down'][l]
        qcols = slice(2 * CONV_CH, 2 * CONV_CH + ATTN_CH)
        Win_s = Win.at[:, qcols].multiply(scale)
        b_in_s = b_in[l].at[qcols].multiply(scale)
        h_a, qkv, h_c = _mm([cur], [Win_s], bias=b_in_s, tm=ts, name="in_proj",
                            splits=((2 * CONV_CH, F32, 1.0), (3 * ATTN_CH, F32, 1.0), (2 * GMLP_CH, F32, 1.0)))
        conv_out, hc = _conv_fwd(h_a, full['conv_dw_w'][l], conv_dw_b[l], conv_ln_g[l], conv_ln_b[l],
                                 name="conv_fwd", ts=ts)
        qkv_d = _dilate_qkv(qkv, name="dilate_qkv", ts=ts)
        o_ps, lse_ps = [], []
        for p, d in enumerate(dils):
            o_p, lse_p = _attn_fwd(qkv_d[p], biases[p], name=f"attn_fwd_d{d}", nb=nbs[p], G=G)
            o_ps.append(o_p)
            lse_ps.append(lse_p)
        attn_out, lse = _attn_merge(o_ps, lse_ps, dils, name="attn_merge", ts=ts)
        w_tril = jnp.tril(gmlp_w_s[l]).astype(BF16)
        bs_rows = jnp.repeat(gmlp_b_s[l].T, GMLP_GROUP_DIM, axis=1)
        gm_out = _gmlp_fwd(h_c, gmlp_ln_g[l], gmlp_ln_b[l], w_tril, bs_rows, name="gmlp_fwd", ts=ts)
        x1, xhat1, rstd1 = _mm([conv_out, attn_out, gm_out],
                               [Wout[:CONV_CH], Wout[CONV_CH:CONV_CH + ATTN_CH], Wout[CONV_CH + ATTN_CH:]],
                               bias=b_out[l], resid=cur, resid_scale=ALPHA, ln=(ln1_g[l], ln1_b[l]), tm=ts,
                               name="out_proj_ln")
        hu = _mm([x1], [Wup], bias=ffn_b_up[l], tm=ts, tn=F, name="ffn_up")
        act, act_dg, act_dv = _ffn_act_fwd(hu, full['ffn_conv_w'][l], ffn_conv_b[l], name="ffn_act_fwd",
                                           ts=min(256, S), tc=tc)
        x2, xhat2, rstd2 = _mm([act], [Wdown], bias=ffn_b_down[l], resid=x1, resid_scale=ALPHA,
                               ln=(ln2_g[l], ln2_b[l]), tm=ts, name="ffn_down_ln")
        saved.append(dict(x0=cur, h_a=h_a, h_c=h_c, qkv_d=qkv_d, hc=hc, conv_out=conv_out, attn_out=attn_out,
                          lse=lse, gm_out=gm_out, w_tril=w_tril, bs_rows=bs_rows, x1=x1, xhat1=xhat1, rstd1=rstd1,
                          hu=hu, act=act, act_dg=act_dg, act_dv=act_dv, xhat2=xhat2, rstd2=rstd2))
        cur = x2

    grads = {n: [None] * DEPTH for n in WEIGHTS if n != 'rel_bias_table'}
    drel = None
    dx = None
    loss_part = None
    tk = min(1024, S)
    for l in reversed(range(DEPTH)):
        sv = saved[l]
        Win, Wout, Wup, Wdown = full['w_in'][l], full['w_out'][l], full['ffn_w_up'][l], full['ffn_w_down'][l]
        if dx is None:
            dz2, dg2, db2, dzs2, loss_part = _ln_bwd(sv['xhat2'], sv['rstd2'], ln2_g[l], b=ln2_b[l], target=target,
                                                     name="ln2_bwd_loss", ts=ts)
        else:
            dz2, dg2, db2, dzs2 = _ln_bwd(sv['xhat2'], sv['rstd2'], ln2_g[l], dy=dx, name="ln_bwd", ts=ts)
        grads['ln2_g'][l], grads['ln2_b'][l], grads['ffn_b_down'][l] = dg2[0], db2[0], dzs2[0]
        grads['ffn_w_down'][l] = _mm_tn(sv['act'], dz2, tm=F // 2 if (F // 2) % LANES == 0 else F, tn=D, tk=tk,
                                        name="dw_down")
        dact = _mm([dz2], [Wdown.T], tm=ts, name="dact")
        dhg, dhv, dwg, dwv, dbg, dbv, dug, duv = _ffn_act_bwd(sv['hu'], dact, sv['act_dg'], sv['act_dv'],
                                                              full['ffn_conv_w'][l], name="ffn_act_bwd",
                                                              ts=min(256, S), tc=tc)
        grads['ffn_conv_w'][l] = jnp.concatenate([dwg, dwv], axis=1)
        grads['ffn_conv_b'][l] = jnp.concatenate([dbg, dbv], axis=1)[0]
        grads['ffn_b_up'][l] = jnp.concatenate([dug, duv], axis=1)[0]
        grads['ffn_w_up'][l] = jnp.concatenate(
            [_mm_tn(sv['x1'], dhg, tm=D, tn=tc, tk=tk, name="dw_up"),
             _mm_tn(sv['x1'], dhv, tm=D, tn=tc, tk=tk, name="dw_up")], axis=1)
        WupT = Wup.T
        dx1 = _mm([dhg, dhv], [WupT[:F], WupT[F:]], resid=dz2, resid_scale=ALPHA, tm=ts, name="dx1")
        dz1, dg1, db1, dzs1 = _ln_bwd(sv['xhat1'], sv['rstd1'], ln1_g[l], dy=dx1, name="ln_bwd", ts=ts)
        grads['ln1_g'][l], grads['ln1_b'][l], grads['b_out'][l] = dg1[0], db1[0], dzs1[0]
        grads['w_out'][l] = jnp.concatenate(
            [_mm_tn(sv['conv_out'], dz1, tm=CONV_CH, tn=D, tk=tk, name="dw_out_conv"),
             _mm_tn(sv['attn_out'], dz1, tm=ATTN_CH, tn=D, tk=tk, name="dw_out_attn"),
             _mm_tn(sv['gm_out'], dz1, tm=GMLP_CH, tn=D, tk=tk, name="dw_out_conv")], axis=0)
        dc_conv, dc_attn, dc_gm = _mm([dz1], [Wout.T], tm=ts, name="dcat",
                                      splits=((CONV_CH, F32, 1.0), (ATTN_CH, F32, 1.0), (GMLP_CH, F32, 1.0)))
        dh_a, ddw, ddwb, dclg, dclb = _conv_bwd(sv['h_a'], sv['hc'], dc_conv, full['conv_dw_w'][l], conv_ln_g[l],
                                                conv_ln_b[l], name="conv_bwd", ts=ts)
        grads['conv_dw_w'][l], grads['conv_dw_b'][l] = ddw[:CONV_WIDTH], ddwb[0]
        grads['conv_ln_g'][l], grads['conv_ln_b'][l] = dclg[0], dclb[0]
        dd_d, do_d = _attn_prep(dc_attn, sv['attn_out'], dils, name="attn_prep", ts=ts)
        dqs, dks, dvs, dbs = [], [], [], []
        for p, d in enumerate(dils):
            dq, dk, dv, dbias = _attn_bwd(sv['qkv_d'][p], do_d[p], sv['lse'][p], dd_d[p], biases[p],
                                          name=f"attn_bwd_d{d}", nb=nbs[p], G=G)
            dqs.append(dq)
            dks.append(dk)
            dvs.append(dv)
            dbs.append(dbias.reshape(1, ATTN_HEADS, -1))
        dqkv = _attn_combine(dqs, dks, dvs, dils, name="attn_combine", ts=ts)
        dr = _bias_grad(jnp.concatenate(dbs, axis=0), bucket_flat, name="bias_grad")
        drel = dr if drel is None else drel + dr
        w_tril_t = jnp.swapaxes(sv['w_tril'], 1, 2)
        dh_c, dws, dbs_acc, dglg, dglb = _gmlp_bwd(sv['h_c'], dc_gm, gmlp_ln_g[l], gmlp_ln_b[l], sv['w_tril'],
                                                   w_tril_t, sv['bs_rows'], name="gmlp_bwd", ts=ts)
        grads['gmlp_w_s'][l] = jnp.tril(dws)
        grads['gmlp_b_s'][l] = dbs_acc[:, :GMLP_GROUPS].T
        grads['gmlp_ln_g'][l], grads['gmlp_ln_b'][l] = dglg[0], dglb[0]
        dw_in = _mm_tn_shared(sv['x0'], [dh_a, *dqkv, dh_c], tk=tk, name="dw_in")
        grads['w_in'][l] = jnp.concatenate([w for w, _ in dw_in], axis=1)
        grads['b_in'][l] = jnp.concatenate([c for _, c in dw_in], axis=1)[0]
        WinT = Win.T
        edges = [0, 2 * CONV_CH] + [2 * CONV_CH + k * ATTN_CH for k in (1, 2, 3)] + [WinT.shape[0]]
        dx = _mm([dh_a, *dqkv, dh_c], [WinT[a:b] for a, b in zip(edges[:-1], edges[1:])], resid=dz1,
                 resid_scale=ALPHA, tm=ts, name="dx0")

    gfull = {n: jnp.stack(v) for n, v in grads.items()}
    gfull['rel_bias_table'] = drel

    sends = []
    for n in SHARDED:
        parts = _split_axis(gfull[n], SHARD_AXIS[n])
        sends.append(parts.reshape(N_DEV, -1, parts.shape[-1]).astype(BF16))
    recvs = _exchange(sends, name="grad_exchange")
    shard_out = [[], [], [], []]
    for n, recv in zip(SHARDED, recvs):
        shp = W[n].shape
        rows = recv.shape[1]
        tr = rows // 4 if rows % 64 == 0 else rows
        outs = _adamw(*[src[n].reshape(rows, shp[-1]) for src in (W, Mo, Vo)], recv, name=f"adamw_{n}", tr=tr)
        for kind in range(4):
            shard_out[kind].append(outs[kind].reshape(shp))

    gsmall, soffs = _pack([gfull[n] for n in SMALL], LANES, 8)
    gall = _allgather([gsmall], name="small_grad_allgather")[0]
    spacks = [_pack([src[n] for n in SMALL], LANES, 8)[0] for src in (W, Mo, Vo)]
    souts = _adamw(spacks[0], spacks[1], spacks[2], gall, name="adamw_small", tr=gsmall.shape[0])
    small_out = [_unpack(o, soffs, [W[n].shape for n in SMALL]) for o in souts]

    loss = lax.psum(loss_part[0, 0], ("x", "y", "c"))
    by_kind = []
    for kind in range(4):
        d = dict(zip(SHARDED, shard_out[kind]))
        d.update(zip(SMALL, small_out[kind]))
        by_kind.append([d[n] for n in WEIGHTS])
    return (loss, dx[None], *by_kind[0], *by_kind[1], *by_kind[2], *by_kind[3])
```

```python
import math

import numpy as np
import jax
import jax.numpy as jnp
from jax import lax
from jax.experimental import pallas as pl
from jax.experimental.pallas import tpu as pltpu

F32 = jnp.float32
BF16 = jnp.bfloat16

DEPTH = 2
HEAD_DIM = 64
CONV_CH = 256
CONV_WIDTH = 31
ATTN_HEADS = 8
ATTN_CH = ATTN_HEADS * HEAD_DIM
PATTERNS = ((128, 1), (512, 4), (2048, 16))
ATTN_BLOCK = 128
N_BUCKETS = 32
MAX_DISTANCE = 2048
GMLP_CH = 256
GMLP_GROUPS = 4
GMLP_GROUP_DIM = GMLP_CH // GMLP_GROUPS
CHUNK = 128
FFN_CONV_WIDTH = 3
LN_EPS = 1e-5
ALPHA = (2.0 * DEPTH) ** 0.25
ADAM_LR = 0.001
ADAM_B1 = 0.9
ADAM_B2 = 0.999
ADAM_EPS = 1e-08
ADAM_WD = 0.01
ADAM_STEP = 10
NEG = -1e30
N_DEV = 8
LANES = 128
CONV_HALO = 32
FFN_HALO = 8
FFN_ROWS = 16
FFN_STRIP = 256
MESH = pl.DeviceIdType.MESH

WEIGHTS = ['w_in', 'b_in', 'conv_dw_w', 'conv_dw_b', 'conv_ln_g', 'conv_ln_b', 'rel_bias_table', 'gmlp_ln_g',
           'gmlp_ln_b', 'gmlp_w_s', 'gmlp_b_s', 'w_out', 'b_out', 'ln1_g', 'ln1_b', 'ffn_w_up', 'ffn_b_up',
           'ffn_conv_w', 'ffn_conv_b', 'ffn_w_down', 'ffn_b_down', 'ln2_g', 'ln2_b']
SHARDED = ['w_in', 'w_out', 'ffn_w_up', 'ffn_w_down', 'conv_dw_w', 'ffn_conv_w']
SHARD_AXIS = {'w_in': 2, 'w_out': 1, 'ffn_w_up': 2, 'ffn_w_down': 1, 'conv_dw_w': 2, 'ffn_conv_w': 2}
SMALL = [n for n in WEIGHTS if n not in SHARDED]


def _call(body, *, grid=(), vmem_mb=48, **kw):
    params = pltpu.CompilerParams(dimension_semantics=("arbitrary",) * len(grid), vmem_limit_bytes=vmem_mb << 20)
    return pl.pallas_call(body, grid=grid, compiler_params=params, **kw)


def _sds(shape, dtype):
    return jax.ShapeDtypeStruct(shape, dtype)


def _ln_rows(z):
    mu = jnp.mean(z, axis=-1, keepdims=True)
    zc = z - mu
    var = jnp.mean(zc * zc, axis=-1, keepdims=True)
    rstd = lax.rsqrt(var + LN_EPS)
    return zc * rstd, rstd


def _ln_bwd_rows(dxhat, xhat, rstd):
    m1 = jnp.mean(dxhat, axis=-1, keepdims=True)
    m2 = jnp.mean(dxhat * xhat, axis=-1, keepdims=True)
    return rstd * (dxhat - m1 - xhat * m2)


def _colsum(v):
    return jnp.sum(v, axis=0, keepdims=True)


def _mm(a_list, w_list, *, name, tm, tn=None, bias=None, resid=None, resid_scale=1.0, ln=None, splits=None,
        out_dtype=F32):
    na = len(a_list)
    M = a_list[0].shape[0]
    N = w_list[0].shape[1]
    tn = N if tn is None else tn
    assert M % tm == 0 and N % tn == 0
    assert ln is None or tn == N
    assert splits is None or tn == N

    def body(*refs):
        a_refs, w_refs = refs[:na], refs[na:2 * na]
        pos = 2 * na
        acc = None
        for a_ref, w_ref in zip(a_refs, w_refs):
            t = jnp.dot(a_ref[...].astype(BF16), w_ref[...], preferred_element_type=F32)
            acc = t if acc is None else acc + t
        if bias is not None:
            acc = acc + refs[pos][...]
            pos += 1
        if resid is not None:
            acc = resid_scale * refs[pos][...] + acc
            pos += 1
        if ln is not None:
            g_ref, b_ref = refs[pos], refs[pos + 1]
            y_ref, xhat_ref, rstd_ref = refs[pos + 2], refs[pos + 3], refs[pos + 4]
            xhat, rstd = _ln_rows(acc)
            y_ref[...] = xhat * g_ref[...] + b_ref[...]
            xhat_ref[...] = xhat
            rstd_ref[...] = rstd
        elif splits is not None:
            c0 = 0
            for o_ref, (width, dtype, scale) in zip(refs[pos:], splits):
                part = acc[:, c0:c0 + width]
                if scale != 1.0:
                    part = part * scale
                o_ref[...] = part.astype(dtype)
                c0 += width
        else:
            refs[pos][...] = acc.astype(out_dtype)

    in_specs = [pl.BlockSpec((tm, a.shape[1]), lambda j, i: (i, 0)) for a in a_list]
    in_specs += [pl.BlockSpec((w.shape[0], tn), lambda j, i: (0, j)) for w in w_list]
    args = list(a_list) + list(w_list)
    if bias is not None:
        in_specs.append(pl.BlockSpec((1, tn), lambda j, i: (0, j)))
        args.append(bias.reshape(1, N))
    if resid is not None:
        in_specs.append(pl.BlockSpec((tm, tn), lambda j, i: (i, j)))
        args.append(resid)
    if ln is not None:
        in_specs += [pl.BlockSpec((1, N), lambda j, i: (0, 0))] * 2
        args += [ln[0].reshape(1, N), ln[1].reshape(1, N)]
        out_shape = (_sds((M, N), F32), _sds((M, N), F32), _sds((M, 1), F32))
        out_specs = (pl.BlockSpec((tm, N), lambda j, i: (i, 0)), pl.BlockSpec((tm, N), lambda j, i: (i, 0)),
                     pl.BlockSpec((tm, 1), lambda j, i: (i, 0)))
    elif splits is not None:
        out_shape = tuple(_sds((M, w), d) for (w, d, _) in splits)
        out_specs = tuple(pl.BlockSpec((tm, w), lambda j, i: (i, 0)) for (w, _, _) in splits)
    else:
        out_shape = _sds((M, N), out_dtype)
        out_specs = pl.BlockSpec((tm, tn), lambda j, i: (i, j))
    return _call(body, grid=(N // tn, M // tm), in_specs=in_specs, out_specs=out_specs, out_shape=out_shape,
                 name=name, vmem_mb=56)(*args)


def _mm_tn(a, dy, *, name, tm, tn, tk, colsum=False):
    S, Ka = a.shape
    N = dy.shape[1]
    assert S % tk == 0 and Ka % tm == 0 and N % tn == 0

    def body(a_ref, dy_ref, out_ref, *cs):
        i, k = pl.program_id(1), pl.program_id(2)
        dyb = dy_ref[...]
        part = lax.dot_general(a_ref[...].astype(BF16), dyb.astype(BF16), (((0,), (0,)), ((), ())),
                               preferred_element_type=F32)

        @pl.when(k == 0)
        def _():
            out_ref[...] = part

        @pl.when(k > 0)
        def _():
            out_ref[...] += part

        if colsum:
            cs_ref = cs[0]
            s = _colsum(dyb.astype(F32))

            @pl.when((i == 0) & (k == 0))
            def _():
                cs_ref[...] = s

            @pl.when((i == 0) & (k > 0))
            def _():
                cs_ref[...] += s

    out_shape = [_sds((Ka, N), F32)]
    out_specs = [pl.BlockSpec((tm, tn), lambda j, i, k: (i, j))]
    if colsum:
        out_shape.append(_sds((1, N), F32))
        out_specs.append(pl.BlockSpec((1, tn), lambda j, i, k: (0, j)))
    res = _call(body, grid=(N // tn, Ka // tm, S // tk),
                in_specs=[pl.BlockSpec((tk, tm), lambda j, i, k: (k, i)), pl.BlockSpec((tk, tn), lambda j, i, k: (k, j))],
                out_specs=tuple(out_specs), out_shape=tuple(out_shape), name=name, vmem_mb=56)(a, dy)
    return res if colsum else res[0]


def _mm_tn_shared(a, dys, *, name, tk):
    S, Ka = a.shape
    n = len(dys)
    assert S % tk == 0

    def body(*refs):
        a_ref, dy_refs, outs = refs[0], refs[1:1 + n], refs[1 + n:]
        k = pl.program_id(0)
        a_bf = a_ref[...].astype(BF16)
        for i, dy_ref in enumerate(dy_refs):
            dyb = dy_ref[...]
            part = lax.dot_general(a_bf, dyb.astype(BF16), (((0,), (0,)), ((), ())), preferred_element_type=F32)
            s = _colsum(dyb.astype(F32))
            w_ref, c_ref = outs[2 * i], outs[2 * i + 1]

            @pl.when(k == 0)
            def _():
                w_ref[...] = part
                c_ref[...] = s

            @pl.when(k > 0)
            def _():
                w_ref[...] += part
                c_ref[...] += s

    in_specs = [pl.BlockSpec((tk, Ka), lambda k: (k, 0))]
    in_specs += [pl.BlockSpec((tk, dy.shape[1]), lambda k: (k, 0)) for dy in dys]
    out_specs, out_shape = [], []
    for dy in dys:
        N = dy.shape[1]
        out_specs += [pl.BlockSpec((Ka, N), lambda k: (0, 0)), pl.BlockSpec((1, N), lambda k: (0, 0))]
        out_shape += [_sds((Ka, N), F32), _sds((1, N), F32)]
    res = _call(body, grid=(S // tk,), in_specs=in_specs, out_specs=tuple(out_specs), out_shape=tuple(out_shape),
                name=name, vmem_mb=56)(a, *dys)
    return [(res[2 * i], res[2 * i + 1]) for i in range(n)]


def _ln_bwd(xhat, rstd, g, *, name, ts, dy=None, b=None, target=None):
    S, D = xhat.shape
    from_loss = target is not None

    def body(*refs):
        if from_loss:
            xhat_ref, rstd_ref, g_ref, b_ref, t_ref, dz_ref, dg_ref, db_ref, dzs_ref, loss_ref = refs
        else:
            xhat_ref, rstd_ref, g_ref, dy_ref, dz_ref, dg_ref, db_ref, dzs_ref = refs
        i = pl.program_id(0)
        xh = xhat_ref[...]
        gg = g_ref[...]
        if from_loss:
            err = xh * gg + b_ref[...] - t_ref[...]
            dyv = err * (1.0 / D)
            lsum = (0.5 / D) * jnp.sum(err * err, axis=(0, 1), keepdims=True)
        else:
            dyv = dy_ref[...]
        dz = _ln_bwd_rows(dyv * gg, xh, rstd_ref[...])
        dz_ref[...] = dz
        parts = [(dg_ref, _colsum(dyv * xh)), (db_ref, _colsum(dyv)), (dzs_ref, _colsum(dz))]
        if from_loss:
            parts.append((loss_ref, lsum))

        @pl.when(i == 0)
        def _():
            for r, v in parts:
                r[...] = v

        @pl.when(i > 0)
        def _():
            for r, v in parts:
                r[...] += v

    row = pl.BlockSpec((ts, D), lambda i: (i, 0))
    vec = pl.BlockSpec((1, D), lambda i: (0, 0))
    in_specs = [row, pl.BlockSpec((ts, 1), lambda i: (i, 0)), vec]
    args = [xhat, rstd, g.reshape(1, D)]
    if from_loss:
        in_specs += [vec, row]
        args += [b.reshape(1, D), target]
    else:
        in_specs += [row]
        args += [dy]
    out_shape = [_sds((S, D), F32), _sds((1, D), F32), _sds((1, D), F32), _sds((1, D), F32)]
    out_specs = [row, vec, vec, vec]
    if from_loss:
        out_shape.append(_sds((1, 1), F32))
        out_specs.append(pl.BlockSpec((1, 1), lambda i: (0, 0)))
    return _call(body, grid=(S // ts,), in_specs=in_specs, out_specs=tuple(out_specs), out_shape=tuple(out_shape),
                 name=name)(*args)


def _glu(v):
    return v[:, :CONV_CH] * jax.nn.sigmoid(v[:, CONV_CH:])


def _copies_moved_back(buf, sh, rows):
    for b in range(1, 8):
        sh[b - 1, 8:rows, :] = buf[pl.ds(8 - b, rows - 8), :]


def _copies_moved_ahead(buf, sh, rows):
    for b in range(1, 8):
        sh[b - 1, 0:rows - 8, :] = buf[pl.ds(b, rows - 8), :]


def _rows_back(buf, sh, start, s, n):
    a, b = divmod(s, 8)
    return buf[pl.ds(start - 8 * a, n), :] if b == 0 else sh[b - 1, pl.ds(start - 8 * a, n), :]


def _rows_ahead(buf, sh, start, s, n):
    a, b = divmod(s, 8)
    return buf[pl.ds(start + 8 * a, n), :] if b == 0 else sh[b - 1, pl.ds(start + 8 * a, n), :]


def _conv_fwd(h_a, dw_w, dw_b, ln_g, ln_b, *, name, ts):
    S = h_a.shape[0]
    C, K, HB = CONV_CH, CONV_WIDTH, CONV_HALO
    RC = 128

    def body(h_ref, halo_ref, w_ref, b_ref, g_ref, bb_ref, out_ref, hc_ref, gbuf, gsh):
        i = pl.program_id(0)
        gbuf[0:HB, :] = jnp.where(i > 0, _glu(halo_ref[...]), 0.0)
        gbuf[HB:HB + ts, :] = _glu(h_ref[...])
        _copies_moved_back(gbuf, gsh, HB + ts)
        for r0 in range(0, ts, RC):
            acc = jnp.zeros((RC, C), F32) + b_ref[...]
            for k in range(K):
                acc = acc + w_ref[k:k + 1, :] * _rows_back(gbuf, gsh, r0 + HB, K - 1 - k, RC)
            hc_ref[r0:r0 + RC, :] = acc
            xhat, _ = _ln_rows(acc)
            hn = xhat * g_ref[...] + bb_ref[...]
            out_ref[r0:r0 + RC, :] = (hn * jax.nn.sigmoid(hn)).astype(BF16)

    nb = ts // HB
    vec = pl.BlockSpec((1, C), lambda i: (0, 0))
    return _call(body, grid=(S // ts,),
                 in_specs=[pl.BlockSpec((ts, 2 * C), lambda i: (i, 0)),
                           pl.BlockSpec((HB, 2 * C), lambda i: (jnp.maximum(i * nb - 1, 0), 0)),
                           pl.BlockSpec((K, C), lambda i: (0, 0)), vec, vec, vec],
                 out_specs=(pl.BlockSpec((ts, C), lambda i: (i, 0)), pl.BlockSpec((ts, C), lambda i: (i, 0))),
                 out_shape=(_sds((S, C), BF16), _sds((S, C), F32)),
                 scratch_shapes=[pltpu.VMEM((HB + ts, C), F32), pltpu.VMEM((7, HB + ts, C), F32)], name=name)(
        h_a, h_a, dw_w, dw_b.reshape(1, C), ln_g.reshape(1, C), ln_b.reshape(1, C))


def _conv_bwd(h_a, hc, dout, dw_w, ln_g, ln_b, *, name, ts):
    S = h_a.shape[0]
    C, K, HB = CONV_CH, CONV_WIDTH, CONV_HALO
    RC = 128
    n = S // ts

    def dconv_out(hc_v, do_v, g_ref, bb_ref):
        xhat, rstd = _ln_rows(hc_v)
        hn = xhat * g_ref[...] + bb_ref[...]
        sg = jax.nn.sigmoid(hn)
        dhn = do_v * (sg * (1.0 + hn * (1.0 - sg)))
        return _ln_bwd_rows(dhn * g_ref[...], xhat, rstd), dhn, xhat

    def body(h_ref, hprev_ref, hc_ref, hcnext_ref, do_ref, donext_ref, w_ref, g_ref, bb_ref,
             dh_ref, dw_ref, dwb_ref, dg_ref, db_ref, gbuf, dbuf, gsh, dsh):
        i = pl.program_id(0)
        hv = h_ref[...]
        gbuf[0:HB, :] = jnp.where(i > 0, _glu(hprev_ref[...]), 0.0)
        gbuf[HB:HB + ts, :] = _glu(hv)
        dhc, dhn, xhat = dconv_out(hc_ref[...], do_ref[...], g_ref, bb_ref)
        dhc_next, _, _ = dconv_out(hcnext_ref[...], donext_ref[...], g_ref, bb_ref)
        dbuf[0:ts, :] = dhc
        dbuf[ts:ts + HB, :] = jnp.where(i < n - 1, dhc_next, 0.0)
        _copies_moved_back(gbuf, gsh, HB + ts)
        _copies_moved_ahead(dbuf, dsh, ts + HB)
        dw_rows = []
        for k in range(K):
            acc_k = jnp.zeros((1, C), F32)
            for r0 in range(0, ts, RC):
                acc_k = acc_k + _colsum(dbuf[r0:r0 + RC, :] * _rows_back(gbuf, gsh, r0 + HB, K - 1 - k, RC))
            dw_rows.append(acc_k)
        dw_rows.append(jnp.zeros((1, C), F32))
        dw_tile = jnp.concatenate(dw_rows, axis=0)
        for r0 in range(0, ts, RC):
            acc = jnp.zeros((RC, C), F32)
            for k in range(K):
                acc = acc + w_ref[k:k + 1, :] * _rows_ahead(dbuf, dsh, r0, K - 1 - k, RC)
            a = hv[r0:r0 + RC, :C]
            sg = jax.nn.sigmoid(hv[r0:r0 + RC, C:])
            dh_ref[r0:r0 + RC, :C] = (acc * sg).astype(BF16)
            dh_ref[r0:r0 + RC, C:] = (acc * a * sg * (1.0 - sg)).astype(BF16)
        parts = [(dw_ref, dw_tile), (dwb_ref, _colsum(dhc)), (dg_ref, _colsum(dhn * xhat)), (db_ref, _colsum(dhn))]

        @pl.when(i == 0)
        def _():
            for r, v in parts:
                r[...] = v

        @pl.when(i > 0)
        def _():
            for r, v in parts:
                r[...] += v

    nb = ts // HB
    last = S // HB - 1
    vec = pl.BlockSpec((1, C), lambda i: (0, 0))
    nxt = lambda i: (jnp.minimum((i + 1) * nb, last), 0)
    return _call(body, grid=(n,),
                 in_specs=[pl.BlockSpec((ts, 2 * C), lambda i: (i, 0)),
                           pl.BlockSpec((HB, 2 * C), lambda i: (jnp.maximum(i * nb - 1, 0), 0)),
                           pl.BlockSpec((ts, C), lambda i: (i, 0)), pl.BlockSpec((HB, C), nxt),
                           pl.BlockSpec((ts, C), lambda i: (i, 0)), pl.BlockSpec((HB, C), nxt),
                           pl.BlockSpec((K, C), lambda i: (0, 0)), vec, vec],
                 out_specs=(pl.BlockSpec((ts, 2 * C), lambda i: (i, 0)), pl.BlockSpec((K + 1, C), lambda i: (0, 0)),
                            vec, vec, vec),
                 out_shape=(_sds((S, 2 * C), BF16), _sds((K + 1, C), F32), _sds((1, C), F32), _sds((1, C), F32),
                            _sds((1, C), F32)),
                 scratch_shapes=[pltpu.VMEM((HB + ts, C), F32), pltpu.VMEM((ts + HB, C), F32),
                                 pltpu.VMEM((7, HB + ts, C), F32), pltpu.VMEM((7, ts + HB, C), F32)], name=name)(
        h_a, h_a, hc, hc, dout, dout, dw_w, ln_g.reshape(1, C), ln_b.reshape(1, C))


def _gmlp_mix(vn_bf, w_ref, mix_buf, ts):
    for ch in range(ts // CHUNK):
        for g in range(GMLP_GROUPS):
            vg = vn_bf[ch * CHUNK:(ch + 1) * CHUNK, g * GMLP_GROUP_DIM:(g + 1) * GMLP_GROUP_DIM]
            mix_buf[ch * CHUNK:(ch + 1) * CHUNK, g * GMLP_GROUP_DIM:(g + 1) * GMLP_GROUP_DIM] = jnp.dot(
                w_ref[g], vg, preferred_element_type=F32)


def _gmlp_fwd(h_c, ln_g, ln_b, w_tril, bs_rows, *, name, ts):
    S = h_c.shape[0]
    C = GMLP_CH

    def body(h_ref, g_ref, b_ref, w_ref, bs_ref, out_ref, mix_buf):
        hv = h_ref[...]
        xhat, _ = _ln_rows(hv[:, C:])
        vn = (xhat * g_ref[...] + b_ref[...]).astype(BF16)
        _gmlp_mix(vn, w_ref, mix_buf, ts)
        for ch in range(ts // CHUNK):
            rows = slice(ch * CHUNK, (ch + 1) * CHUNK)
            out_ref[rows, :] = (hv[rows, :C] * (mix_buf[rows, :] + bs_ref[...])).astype(BF16)

    vec = pl.BlockSpec((1, C), lambda i: (0, 0))
    return _call(body, grid=(S // ts,),
                 in_specs=[pl.BlockSpec((ts, 2 * C), lambda i: (i, 0)), vec, vec,
                           pl.BlockSpec((GMLP_GROUPS, CHUNK, CHUNK), lambda i: (0, 0, 0)),
                           pl.BlockSpec((CHUNK, C), lambda i: (0, 0))],
                 out_specs=pl.BlockSpec((ts, C), lambda i: (i, 0)), out_shape=_sds((S, C), BF16),
                 scratch_shapes=[pltpu.VMEM((ts, C), F32)], name=name)(
        h_c, ln_g.reshape(1, C), ln_b.reshape(1, C), w_tril, bs_rows)


def _gmlp_bwd(h_c, dout, ln_g, ln_b, w_tril, w_tril_t, bs_rows, *, name, ts):
    S = h_c.shape[0]
    C, G, GD = GMLP_CH, GMLP_GROUPS, GMLP_GROUP_DIM

    def body(h_ref, do_ref, g_ref, b_ref, w_ref, wt_ref, bs_ref, dh_ref, dw_ref, dbs_ref, dg_ref, db_ref,
             mix_buf, dvn_buf):
        i = pl.program_id(0)
        hv = h_ref[...]
        u = hv[:, :C]
        xhat, rstd = _ln_rows(hv[:, C:])
        vn = (xhat * g_ref[...] + b_ref[...]).astype(BF16)
        _gmlp_mix(vn, w_ref, mix_buf, ts)
        do = do_ref[...]
        dmixed = do * u
        dm_bf = dmixed.astype(BF16)
        lane = lax.broadcasted_iota(jnp.int32, (CHUNK, LANES), 1)
        dbs = jnp.zeros((CHUNK, LANES), F32)
        dws = [jnp.zeros((CHUNK, CHUNK), F32) for _ in range(G)]
        for ch in range(ts // CHUNK):
            rows = slice(ch * CHUNK, (ch + 1) * CHUNK)
            dh_ref[rows, :C] = (do[rows, :] * (mix_buf[rows, :] + bs_ref[...])).astype(BF16)
            for g in range(G):
                cols = slice(g * GD, (g + 1) * GD)
                dmg = dm_bf[rows, cols]
                dvn_buf[rows, cols] = jnp.dot(wt_ref[g], dmg, preferred_element_type=F32)
                dws[g] = dws[g] + lax.dot_general(dmg, vn[rows, cols], (((1,), (1,)), ((), ())),
                                                  preferred_element_type=F32)
                rs = jnp.sum(dmixed[rows, cols], axis=1, keepdims=True)
                dbs = dbs + jnp.where(lane == g, rs, 0.0)
        dvn = dvn_buf[...]
        dh_ref[:, C:] = _ln_bwd_rows(dvn * g_ref[...], xhat, rstd).astype(BF16)
        dgv, dbv = _colsum(dvn * xhat), _colsum(dvn)

        @pl.when(i == 0)
        def _():
            for g in range(G):
                dw_ref[g] = dws[g]
            dbs_ref[...] = dbs
            dg_ref[...] = dgv
            db_ref[...] = dbv

        @pl.when(i > 0)
        def _():
            for g in range(G):
                dw_ref[g] += dws[g]
            dbs_ref[...] += dbs
            dg_ref[...] += dgv
            db_ref[...] += dbv

    vec = pl.BlockSpec((1, C), lambda i: (0, 0))
    wspec = pl.BlockSpec((G, CHUNK, CHUNK), lambda i: (0, 0, 0))
    return _call(body, grid=(S // ts,),
                 in_specs=[pl.BlockSpec((ts, 2 * C), lambda i: (i, 0)),
                           pl.BlockSpec((ts, C), lambda i: (i, 0)), vec, vec, wspec, wspec,
                           pl.BlockSpec((CHUNK, C), lambda i: (0, 0))],
                 out_specs=(pl.BlockSpec((ts, 2 * C), lambda i: (i, 0)), wspec,
                            pl.BlockSpec((CHUNK, LANES), lambda i: (0, 0)), vec, vec),
                 out_shape=(_sds((S, 2 * C), BF16), _sds((G, CHUNK, CHUNK), F32), _sds((CHUNK, LANES), F32),
                            _sds((1, C), F32), _sds((1, C), F32)),
                 scratch_shapes=[pltpu.VMEM((ts, C), F32), pltpu.VMEM((ts, C), F32)], name=name)(
        h_c, dout, ln_g.reshape(1, C), ln_b.reshape(1, C), w_tril, w_tril_t, bs_rows)


def _ffn_act_fwd(hu, cw, cb, *, name, ts, tc):
    S, F2 = hu.shape
    F = F2 // 2
    nj = F // tc
    HB = FFN_HALO
    nb = ts // HB

    def body(g_ref, v_ref, gh_ref, vh_ref, wg_ref, wv_ref, bg_ref, bv_ref, act_ref, dg_ref, dv_ref):
        i = pl.program_id(0)
        row = lax.broadcasted_iota(jnp.int32, (HB, tc), 0)

        def conv_chunk(x_ref, w_ref, b_ref, c, carry):
            cur = x_ref[c * HB:(c + 1) * HB, :]
            r1, r2 = pltpu.roll(cur, 1, 0), pltpu.roll(cur, 2, 0)
            x1 = jnp.where(row < 1, carry[0], r1)
            x2 = jnp.where(row < 2, carry[1], r2)
            out = (w_ref[0:1, :] * x2 + w_ref[1:2, :] * x1 + w_ref[2:3, :] * cur) + b_ref[...]
            return out, (r1, r2)

        def first_carry(h_ref):
            prev = jnp.where(i > 0, h_ref[...], 0.0)
            return pltpu.roll(prev, 1, 0), pltpu.roll(prev, 2, 0)

        cg, cv = first_carry(gh_ref), first_carry(vh_ref)
        for r0 in range(0, ts, FFN_ROWS):
            acts, dgs, dvs = [], [], []
            for c in range(r0 // HB, (r0 + FFN_ROWS) // HB):
                gc, cg = conv_chunk(g_ref, wg_ref, bg_ref, c, cg)
                vc, cv = conv_chunk(v_ref, wv_ref, bv_ref, c, cv)
                sg = jax.nn.sigmoid(gc)
                silu = gc * sg
                acts.append(silu * vc)
                dgs.append(vc * (sg * (1.0 + gc * (1.0 - sg))))
                dvs.append(silu)
            rows = slice(r0, r0 + FFN_ROWS)
            act_ref[rows, :] = jnp.concatenate(acts, axis=0).astype(BF16)
            dg_ref[rows, :] = jnp.concatenate(dgs, axis=0).astype(BF16)
            dv_ref[rows, :] = jnp.concatenate(dvs, axis=0).astype(BF16)

    prev = lambda off: (lambda i, j: (jnp.maximum(i * nb - 1, 0), j + off))
    out = pl.BlockSpec((ts, tc), lambda i, j: (i, j))
    return _call(body, grid=(S // ts, nj),
                 in_specs=[pl.BlockSpec((ts, tc), lambda i, j: (i, j)), pl.BlockSpec((ts, tc), lambda i, j: (i, j + nj)),
                           pl.BlockSpec((HB, tc), prev(0)), pl.BlockSpec((HB, tc), prev(nj)),
                           pl.BlockSpec((3, tc), lambda i, j: (0, j)), pl.BlockSpec((3, tc), lambda i, j: (0, j + nj)),
                           pl.BlockSpec((1, tc), lambda i, j: (0, j)), pl.BlockSpec((1, tc), lambda i, j: (0, j + nj))],
                 out_specs=(out, out, out), out_shape=(_sds((S, F), BF16),) * 3, name=name)(
        hu, hu, hu, hu, cw, cw, cb.reshape(1, F2), cb.reshape(1, F2))


def _ffn_act_bwd(hu, dact, dact_dg, dact_dv, cw, *, name, ts, tc):
    S, F2 = hu.shape
    F = F2 // 2
    nj = F // tc
    HB = FFN_HALO
    HB16 = 16
    n = S // ts

    def body(g_ref, v_ref, lg_ref, lv_ref, lgn_ref, lvn_ref, da_ref, dan_ref, wg_ref, wv_ref,
             dhg_ref, dhv_ref, dwg_ref, dwv_ref, dbg_ref, dbv_ref, dug_ref, duv_ref, accg, accv):
        i = pl.program_id(1)
        RC = FFN_ROWS
        npairs = ts // RC
        last_tile = i == n - 1

        @pl.when(i == 0)
        def _():
            accg[...] = jnp.zeros_like(accg)
            accv[...] = jnp.zeros_like(accv)

        for l_ref, ln_ref, h_ref, w_ref, dh_ref, acc in ((lg_ref, lgn_ref, g_ref, wg_ref, dhg_ref, accg),
                                                          (lv_ref, lvn_ref, v_ref, wv_ref, dhv_ref, accv)):
            for c0 in range(0, tc, FFN_STRIP):
                cols = slice(c0, min(c0 + FFN_STRIP, tc))
                sw = cols.stop - cols.start
                row = lax.broadcasted_iota(jnp.int32, (HB, sw), 0)
                w0, w1, w2 = w_ref[0:1, cols], w_ref[1:2, cols], w_ref[2:3, cols]

                def d_rows(p):
                    rows = slice(p * RC, (p + 1) * RC)
                    return da_ref[rows, cols] * l_ref[rows, cols].astype(F32)

                after = jnp.where(last_tile, 0.0, dan_ref[:, cols]) * ln_ref[:, cols].astype(F32)[0:HB, :]
                pair = d_rows(0)
                cur = pair[0:HB, :]
                c7, c6 = pltpu.roll(cur, HB - 1, 0), pltpu.roll(cur, HB - 2, 0)
                sums = [None] * 5
                for p in range(npairs):
                    nxt_pair = d_rows(p + 1) if p + 1 < npairs else None
                    dhus = []
                    for half, nxt_c in enumerate((pair[HB:RC, :], after if nxt_pair is None else nxt_pair[0:HB, :])):
                        c = 2 * p + half
                        n7, n6 = pltpu.roll(nxt_c, HB - 1, 0), pltpu.roll(nxt_c, HB - 2, 0)
                        taps = [jnp.where(row >= HB - 2, n6, c6), jnp.where(row >= HB - 1, n7, c7), cur]
                        dhu = w2 * taps[2] + w1 * taps[1] + w0 * taps[0]
                        dhus.append(dhu)
                        h = h_ref[c * HB:(c + 1) * HB, cols]
                        parts = [h * taps[0], h * taps[1], h * taps[2], taps[2], dhu]
                        sums = [q if s is None else s + q for s, q in zip(sums, parts)]
                        cur, c7, c6 = nxt_c, n7, n6
                    dh_ref[p * RC:(p + 1) * RC, cols] = jnp.concatenate(dhus, axis=0).astype(BF16)
                    pair = nxt_pair
                for k in range(5):
                    acc[8 * k:8 * k + 8, cols] += sums[k]

        @pl.when(i == n - 1)
        def _():
            for acc, dw_ref, db_ref, du_ref in ((accg, dwg_ref, dbg_ref, dug_ref), (accv, dwv_ref, dbv_ref, duv_ref)):
                for k in range(3):
                    dw_ref[k:k + 1, :] = _colsum(acc[8 * k:8 * k + 8, :])
                db_ref[...] = _colsum(acc[24:32, :])
                du_ref[...] = _colsum(acc[32:40, :])

    nxt = lambda hb: (lambda j, i: (jnp.minimum((i + 1) * (ts // hb), S // hb - 1), j))
    tile = lambda off: (lambda j, i: (i, j + off))
    vec = lambda rows: pl.BlockSpec((rows, tc), lambda j, i: (0, j))
    return _call(body, grid=(nj, n),
                 in_specs=[pl.BlockSpec((ts, tc), tile(0)), pl.BlockSpec((ts, tc), tile(nj)),
                           pl.BlockSpec((ts, tc), tile(0)), pl.BlockSpec((ts, tc), tile(0)),
                           pl.BlockSpec((HB16, tc), nxt(HB16)), pl.BlockSpec((HB16, tc), nxt(HB16)),
                           pl.BlockSpec((ts, tc), tile(0)), pl.BlockSpec((HB, tc), nxt(HB)),
                           pl.BlockSpec((3, tc), lambda j, i: (0, j)), pl.BlockSpec((3, tc), lambda j, i: (0, j + nj))],
                 out_specs=(pl.BlockSpec((ts, tc), tile(0)), pl.BlockSpec((ts, tc), tile(0)),
                            vec(3), vec(3), vec(1), vec(1), vec(1), vec(1)),
                 out_shape=(_sds((S, F), BF16), _sds((S, F), BF16), _sds((3, F), F32), _sds((3, F), F32),
                            _sds((1, F), F32), _sds((1, F), F32), _sds((1, F), F32), _sds((1, F), F32)),
                 scratch_shapes=[pltpu.VMEM((40, tc), F32), pltpu.VMEM((40, tc), F32)], name=name)(
        hu, hu, dact_dg, dact_dv, dact_dg, dact_dv, dact, dact, cw, cw)


def _t5_bucket(dist):
    max_exact = N_BUCKETS // 2
    d = np.maximum(dist, 1).astype(np.float64)
    large = max_exact + (np.log(d / max_exact) / math.log(MAX_DISTANCE / max_exact)
                         * (N_BUCKETS - max_exact)).astype(np.int32)
    large = np.minimum(large, N_BUCKETS - 1)
    return np.where(dist < max_exact, dist, large).astype(np.int32)


def _pattern_tables(window, dilation):
    qi = np.arange(ATTN_BLOCK)[:, None]
    kj = np.arange(2 * ATTN_BLOCK)[None, :]
    dist = qi + ATTN_BLOCK - kj
    valid = (dist >= 0) & (dist <= window // dilation)
    bucket = _t5_bucket(np.clip(dist, 0, None) * dilation)
    return bucket, valid


def _dilate_qkv(qkv, *, name, ts):
    S, C = qkv.shape
    dils = [d for (_, d) in PATTERNS if d > 1]

    def body(x_ref, nat_ref, *rest):
        outs, tmp = rest[:-1], rest[-1]
        nat_ref[...] = x_ref[...].astype(BF16)
        for j in range(C // LANES):
            cols = slice(j * LANES, (j + 1) * LANES)
            tmp[...] = x_ref[:, cols]
            for d, o_ref in zip(dils, outs):
                _dilate(tmp, o_ref, cols, d, ts, BF16)

    out_shape = (_sds((S, C), BF16),) + tuple(_sds((d, S // d, C), BF16) for d in dils)
    out_specs = (_dil_spec(1, ts, C),) + tuple(_dil_spec(d, ts, C) for d in dils)
    res = _call(body, grid=(S // ts,), in_specs=[_dil_spec(1, ts, C)], out_specs=out_specs, out_shape=out_shape,
                scratch_shapes=[pltpu.VMEM((ts, LANES), F32)], name=name)(qkv)
    return [res[0]] + [r.reshape(S, C) for r in res[1:]]


def _attn_group(S):
    nb_min = (S // PATTERNS[-1][1]) // ATTN_BLOCK
    return math.gcd(8, nb_min)


def _with_prev_block(ref, halo_ref, bi, B):
    if bi == 0:
        return jnp.concatenate([halo_ref[...], ref[0:B, :]], axis=0)
    return ref[(bi - 1) * B:(bi + 1) * B, :]


def _stack_heads(x, head0):
    zero = jnp.zeros_like(x)
    return jnp.concatenate([jnp.where(head0, x, zero), jnp.where(head0, zero, x)], axis=0)


def _attn_fwd(qkv, bias, *, name, nb, G):
    S = qkv.shape[0]
    B, HD = ATTN_BLOCK, HEAD_DIM
    GR = G * B
    ng = S // GR

    def body(q_ref, k_ref, kh_ref, v_ref, vh_ref, bias_ref, o_ref, lse_ref):
        g = pl.program_id(1)
        halo_ok = (g * G) % nb != 0
        col = lax.broadcasted_iota(jnp.int32, (B, 2 * B), 1)
        head0 = lax.broadcasted_iota(jnp.int32, (B, LANES), 1) < HD

        for bi in range(G):
            r0 = bi * B
            q2 = q_ref[r0:r0 + B, :]
            kk = _with_prev_block(k_ref, kh_ref, bi, B)
            vv = _with_prev_block(v_ref, vh_ref, bi, B)
            s_st = lax.dot_general(_stack_heads(q2, head0), kk, (((1,), (1,)), ((), ())), preferred_element_type=F32)
            ps, ls_, lses = [], [], []
            for hh in range(2):
                s = s_st[hh * B:(hh + 1) * B, :] + bias_ref[hh]
                if bi == 0:
                    s = jnp.where(jnp.logical_and(jnp.logical_not(halo_ok), col < B), NEG, s)
                m = jnp.max(s, axis=1, keepdims=True)
                p = jnp.exp(s - m)
                l = jnp.sum(p, axis=1, keepdims=True)
                ps.append(p.astype(BF16))
                ls_.append(l)
                lses.append(m + jnp.log(l))
            o_st = jnp.dot(jnp.concatenate(ps, axis=0), vv, preferred_element_type=F32)
            o_ref[r0:r0 + B, :] = (jnp.where(head0, o_st[0:B, :], o_st[B:2 * B, :])
                                   / jnp.where(head0, ls_[0], ls_[1])).astype(BF16)
            lse_ref[r0:r0 + B, :] = jnp.where(head0, lses[0], lses[1])

    halo = lambda off: (lambda hp, g: (jnp.maximum(g * G - 1, 0), off + hp))
    main = lambda off: (lambda hp, g: (g, off + hp))
    return _call(body, grid=(4, ng),
                 in_specs=[pl.BlockSpec((GR, LANES), main(0)), pl.BlockSpec((GR, LANES), main(4)),
                           pl.BlockSpec((B, LANES), halo(4)), pl.BlockSpec((GR, LANES), main(8)),
                           pl.BlockSpec((B, LANES), halo(8)), pl.BlockSpec((2, B, 2 * B), lambda hp, g: (hp, 0, 0))],
                 out_specs=(pl.BlockSpec((GR, LANES), main(0)), pl.BlockSpec((GR, LANES), main(0))),
                 out_shape=(_sds((S, ATTN_CH), BF16), _sds((S, ATTN_CH), F32)), name=name)(
        qkv, qkv, qkv, qkv, qkv, bias)


def _dil_spec(d, ts, C):
    if d == 1:
        return pl.BlockSpec((ts, C), lambda i: (i, 0))
    return pl.BlockSpec((d, ts // d, C), lambda i: (0, i, 0))


def _dil_view(a, d):
    return a if d == 1 else a.reshape(d, a.shape[0] // d, a.shape[1])


def _dilate(nat_tmp, dst_ref, cols, d, ts, dtype=F32):
    for r in range(d):
        dst_ref[r, :, cols] = nat_tmp[pl.ds(r, ts // d, stride=d), :].astype(dtype)


def _undilate(src_ref, nat_tmp, cols, d, ts, accumulate=False):
    for r in range(d):
        rows = pl.ds(r, ts // d, stride=d)
        if accumulate:
            nat_tmp[rows, :] = nat_tmp[rows, :] + src_ref[r, :, cols].astype(F32)
        else:
            nat_tmp[rows, :] = src_ref[r, :, cols].astype(F32)


def _attn_merge(o_list, lse_list, dils, *, name, ts):
    S, C = o_list[0].shape
    P = len(o_list)
    nd = sum(1 for d in dils if d > 1)

    def body(*refs):
        o_refs, l_refs = refs[:P], refs[P:2 * P]
        out_ref, lse_ref = refs[2 * P], refs[2 * P + 1]
        lse_d_refs = refs[2 * P + 2:2 * P + 2 + nd]
        scratch = list(refs[2 * P + 2 + nd:])
        tmp = scratch.pop()
        for j in range(C // LANES):
            cols = slice(j * LANES, (j + 1) * LANES)
            os_, ls, free = [], [], list(scratch)
            for o_ref, l_ref, d in zip(o_refs, l_refs, dils):
                if d > 1:
                    so, sl = free.pop(0), free.pop(0)
                    _undilate(o_ref, so, cols, d, ts)
                    _undilate(l_ref, sl, cols, d, ts)
                    os_.append(so[...])
                    ls.append(sl[...])
                else:
                    os_.append(o_ref[:, cols].astype(F32))
                    ls.append(l_ref[:, cols])
            m = ls[0]
            for l in ls[1:]:
                m = jnp.maximum(m, l)
            ws = [jnp.exp(l - m) for l in ls]
            den = ws[0]
            for w in ws[1:]:
                den = den + w
            num = ws[0] * os_[0]
            for w, o in zip(ws[1:], os_[1:]):
                num = num + w * o
            out_ref[:, cols] = num / den
            tmp[...] = m + jnp.log(den)
            lse_ref[:, cols] = tmp[...]
            for l_out, d in zip(lse_d_refs, [d for d in dils if d > 1]):
                _dilate(tmp, l_out, cols, d, ts)

    row = _dil_spec(1, ts, C)
    dd = [d for d in dils if d > 1]
    res = _call(body, grid=(S // ts,), in_specs=[_dil_spec(d, ts, C) for d in dils] * 2,
                out_specs=(row, row) + tuple(_dil_spec(d, ts, C) for d in dd),
                out_shape=(_sds((S, C), F32), _sds((S, C), F32)) + tuple(_sds((d, S // d, C), F32) for d in dd),
                scratch_shapes=[pltpu.VMEM((ts, LANES), F32)] * (2 * nd + 1), name=name)(
        *[_dil_view(o, d) for o, d in zip(o_list, dils)], *[_dil_view(l, d) for l, d in zip(lse_list, dils)])
    lse_by_d = {1: res[1]}
    lse_by_d.update({d: r.reshape(S, C) for d, r in zip(dd, res[2:])})
    return res[0], [lse_by_d[d] for d in dils]


def _attn_prep(dout, out, dils, *, name, ts):
    S, C = out.shape
    HD = HEAD_DIM
    dd = [d for d in dils if d > 1]
    nd = len(dd)

    def body(do_ref, o_ref, d_ref, dob_ref, *rest):
        outs, tmp_d, tmp_o = rest[:2 * nd], rest[2 * nd], rest[2 * nd + 1]
        dob_ref[...] = do_ref[...].astype(BF16)
        for j in range(C // LANES):
            cols = slice(j * LANES, (j + 1) * LANES)
            do = do_ref[:, cols]
            prod = do * o_ref[:, cols]
            tmp_o[...] = do
            for h in range(LANES // HD):
                cs = slice(h * HD, (h + 1) * HD)
                tmp_d[:, cs] = jnp.broadcast_to(jnp.sum(prod[:, cs], axis=1, keepdims=True), (ts, HD))
            d_ref[:, cols] = tmp_d[...]
            for d, dd_out, do_out in zip(dd, outs[:nd], outs[nd:]):
                _dilate(tmp_d, dd_out, cols, d, ts)
                _dilate(tmp_o, do_out, cols, d, ts, BF16)

    row = _dil_spec(1, ts, C)
    res = _call(body, grid=(S // ts,), in_specs=[row, row],
                out_specs=(row, row) + tuple(_dil_spec(d, ts, C) for d in dd) * 2,
                out_shape=(_sds((S, C), F32), _sds((S, C), BF16)) + tuple(_sds((d, S // d, C), F32) for d in dd)
                + tuple(_sds((d, S // d, C), BF16) for d in dd),
                scratch_shapes=[pltpu.VMEM((ts, LANES), F32)] * 2, name=name)(dout, out)
    dd_by_d, do_by_d = {1: res[0]}, {1: res[1]}
    dd_by_d.update({d: r.reshape(S, C) for d, r in zip(dd, res[2:2 + nd])})
    do_by_d.update({d: r.reshape(S, C) for d, r in zip(dd, res[2 + nd:])})
    return [dd_by_d[d] for d in dils], [do_by_d[d] for d in dils]


def _attn_bwd(qkv, do, lse, dd, bias, *, name, nb, G):
    S = qkv.shape[0]
    B, HD = ATTN_BLOCK, HEAD_DIM
    GR = G * B
    ng = S // GR
    nblk = S // B

    def body(q_ref, k_ref, kh_ref, v_ref, vh_ref, do_ref, lse_ref, dd_ref, qn_ref, don_ref, lsen_ref, ddn_ref,
             bias_ref, dq_ref, dk_ref, dv_ref, dbias_ref, ds_st, p_st):
        g = pl.program_id(1)
        halo_ok = (g * G) % nb != 0
        next_ok = jnp.logical_and(((g + 1) * G) % nb != 0, g < ng - 1)
        col = lax.broadcasted_iota(jnp.int32, (B, 2 * B), 1)

        def tn_dot(a, b):
            return lax.dot_general(a, b, (((0,), (0,)), ((), ())), preferred_element_type=F32)

        @pl.when(g == 0)
        def _():
            dbias_ref[...] = jnp.zeros_like(dbias_ref)

        head0 = lax.broadcasted_iota(jnp.int32, (B, LANES), 1) < HD

        def nt_dot(a, b):
            return lax.dot_general(a, b, (((1,), (1,)), ((), ())), preferred_element_type=F32)

        def one_head(x, hh):
            zero = jnp.zeros_like(x)
            return jnp.where(head0, x, zero) if hh == 0 else jnp.where(head0, zero, x)

        def pick(per_head):
            return jnp.where(head0, per_head[0], per_head[1])

        for bi in range(G):
            r0 = bi * B
            q2 = q_ref[r0:r0 + B, :]
            do2 = do_ref[r0:r0 + B, :]
            kk = _with_prev_block(k_ref, kh_ref, bi, B)
            vv = _with_prev_block(v_ref, vh_ref, bi, B)
            dq = []
            for hh in range(2):
                s = nt_dot(one_head(q2, hh), kk) + bias_ref[hh]
                if bi == 0:
                    s = jnp.where(jnp.logical_and(jnp.logical_not(halo_ok), col < B), NEG, s)
                p = jnp.exp(s - lse_ref[r0:r0 + B, hh * HD:hh * HD + 1])
                dp = nt_dot(one_head(do2, hh), vv)
                ds = p * (dp - dd_ref[r0:r0 + B, hh * HD:hh * HD + 1])
                dbias_ref[hh] += ds
                ds_bf, p_bf = ds.astype(BF16), p.astype(BF16)
                dq.append(jnp.dot(ds_bf, kk, preferred_element_type=F32))
                ds_st[hh, 2 * r0:2 * r0 + B, :] = ds_bf[:, B:]
                p_st[hh, 2 * r0:2 * r0 + B, :] = p_bf[:, B:]
                if bi > 0:
                    ds_st[hh, 2 * r0 - B:2 * r0, :] = ds_bf[:, :B]
                    p_st[hh, 2 * r0 - B:2 * r0, :] = p_bf[:, :B]
            dq_ref[r0:r0 + B, :] = pick(dq).astype(BF16)

        @pl.when(next_ok)
        def _():
            qn = qn_ref[...]
            don = don_ref[...]
            kl = k_ref[GR - B:GR, :]
            vl = v_ref[GR - B:GR, :]
            for hh in range(2):
                s = nt_dot(one_head(qn, hh), kl) + bias_ref[hh, :, 0:B]
                p = jnp.exp(s - lsen_ref[:, hh * HD:hh * HD + 1])
                dp = nt_dot(one_head(don, hh), vl)
                ds = p * (dp - ddn_ref[:, hh * HD:hh * HD + 1])
                ds_st[hh, 2 * GR - B:2 * GR, :] = ds.astype(BF16)
                p_st[hh, 2 * GR - B:2 * GR, :] = p.astype(BF16)

        @pl.when(jnp.logical_not(next_ok))
        def _():
            for hh in range(2):
                ds_st[hh, 2 * GR - B:2 * GR, :] = jnp.zeros((B, B), BF16)
                p_st[hh, 2 * GR - B:2 * GR, :] = jnp.zeros((B, B), BF16)

        for j in range(G):
            r0 = j * B
            if j < G - 1:
                q_pair, do_pair = q_ref[r0:r0 + 2 * B, :], do_ref[r0:r0 + 2 * B, :]
            else:
                q_pair = jnp.concatenate([q_ref[r0:r0 + B, :], qn_ref[...]], axis=0)
                do_pair = jnp.concatenate([do_ref[r0:r0 + B, :], don_ref[...]], axis=0)
            dk_ref[r0:r0 + B, :] = pick([tn_dot(ds_st[hh, 2 * r0:2 * r0 + 2 * B, :], q_pair)
                                         for hh in range(2)]).astype(BF16)
            dv_ref[r0:r0 + B, :] = pick([tn_dot(p_st[hh, 2 * r0:2 * r0 + 2 * B, :], do_pair)
                                         for hh in range(2)]).astype(BF16)

    halo = lambda off: (lambda hp, g: (jnp.maximum(g * G - 1, 0), off + hp))
    main = lambda off: (lambda hp, g: (g, off + hp))
    nxt = lambda off: (lambda hp, g: (jnp.minimum((g + 1) * G, nblk - 1), off + hp))
    big, small = (lambda m: pl.BlockSpec((GR, LANES), m)), (lambda m: pl.BlockSpec((B, LANES), m))
    return _call(body, grid=(4, ng),
                 in_specs=[big(main(0)), big(main(4)), small(halo(4)), big(main(8)), small(halo(8)),
                           big(main(0)), big(main(0)), big(main(0)),
                           small(nxt(0)), small(nxt(0)), small(nxt(0)), small(nxt(0)),
                           pl.BlockSpec((2, B, 2 * B), lambda hp, g: (hp, 0, 0))],
                 out_specs=(big(main(0)), big(main(0)), big(main(0)),
                            pl.BlockSpec((2, B, 2 * B), lambda hp, g: (hp, 0, 0))),
                 out_shape=(_sds((S, ATTN_CH), BF16),) * 3 + (_sds((ATTN_HEADS, B, 2 * B), F32),),
                 scratch_shapes=[pltpu.VMEM((2, 2 * GR, B), BF16)] * 2, name=name)(
        qkv, qkv, qkv, qkv, qkv, do, lse, dd, qkv, do, lse, dd, bias)


def _attn_combine(dq_list, dk_list, dv_list, dils, *, name, ts):
    S, C = dq_list[0].shape
    P = len(dq_list)
    scale = HEAD_DIM ** -0.5
    assert dils[0] == 1

    def body(*refs):
        out_refs, acc = refs[3 * P:3 * P + 3], refs[3 * P + 3]
        for part in range(3):
            for j in range(C // LANES):
                cols = slice(j * LANES, (j + 1) * LANES)
                acc[...] = refs[part * P][:, cols].astype(F32)
                for r, d in zip(refs[part * P + 1:(part + 1) * P], dils[1:]):
                    _undilate(r, acc, cols, d, ts, accumulate=True)
                tot = acc[...]
                if part == 0:
                    tot = tot * scale
                out_refs[part][:, cols] = tot.astype(BF16)

    return _call(body, grid=(S // ts,), in_specs=[_dil_spec(d, ts, C) for d in dils] * 3,
                 out_specs=(_dil_spec(1, ts, C),) * 3, out_shape=(_sds((S, C), BF16),) * 3,
                 scratch_shapes=[pltpu.VMEM((ts, LANES), F32)], name=name)(
        *[_dil_view(a, d) for lst in (dq_list, dk_list, dv_list) for a, d in zip(lst, dils)])


def _bias_tables(table, bucket_flat, *, name):
    P, _, K = bucket_flat.shape
    H = table.shape[1]
    KC = 4096

    def body(t_ref, bk_ref, out_ref):
        row = lax.broadcasted_iota(jnp.int32, (N_BUCKETS, KC), 0)
        for c in range(K // KC):
            bk = bk_ref[0, :, c * KC:(c + 1) * KC]
            onehot = (row == bk).astype(F32)
            vals = jnp.dot(t_ref[...], onehot, preferred_element_type=F32, precision=lax.Precision.HIGHEST)
            out_ref[0, :, c * KC:(c + 1) * KC] = jnp.where(bk >= 0, vals, NEG)

    return _call(body, grid=(P,),
                 in_specs=[pl.BlockSpec((H, N_BUCKETS), lambda p: (0, 0)), pl.BlockSpec((1, 1, K), lambda p: (p, 0, 0))],
                 out_specs=pl.BlockSpec((1, H, K), lambda p: (p, 0, 0)), out_shape=_sds((P, H, K), F32),
                 name=name)(table.T, bucket_flat)


def _bias_grad(dbias_flat, bucket_flat, *, name):
    P, H, K = dbias_flat.shape
    KC = 4096

    def body(db_ref, bk_ref, out_ref):
        p = pl.program_id(0)
        acc = jnp.zeros((N_BUCKETS, H), F32)
        row = lax.broadcasted_iota(jnp.int32, (N_BUCKETS, KC), 0)
        for c in range(K // KC):
            onehot = (row == bk_ref[0, :, c * KC:(c + 1) * KC]).astype(F32)
            acc = acc + lax.dot_general(onehot, db_ref[0, :, c * KC:(c + 1) * KC], (((1,), (1,)), ((), ())),
                                        preferred_element_type=F32, precision=lax.Precision.HIGHEST)

        @pl.when(p == 0)
        def _():
            out_ref[...] = acc

        @pl.when(p > 0)
        def _():
            out_ref[...] += acc

    return _call(body, grid=(P,),
                 in_specs=[pl.BlockSpec((1, H, K), lambda p: (p, 0, 0)), pl.BlockSpec((1, 1, K), lambda p: (p, 0, 0))],
                 out_specs=pl.BlockSpec((N_BUCKETS, H), lambda p: (0, 0)), out_shape=_sds((N_BUCKETS, H), F32),
                 name=name)(dbias_flat, bucket_flat)


def _allgather(blocks, *, name):
    n = len(blocks)

    def body(*refs):
        x_refs, out_refs = refs[:n], refs[n:2 * n]
        send_sems, recv_sems, local_sems = refs[2 * n:]
        x, y, c = lax.axis_index("x"), lax.axis_index("y"), lax.axis_index("c")
        me, sibling = (x, y, c), (x, y, 1 - c)
        chips = [(1 - x, y), (x, 1 - y), (1 - x, 1 - y)]

        def copy(i, k, blk, to, own=False):
            slot = out_refs[i].at[4 * blk[0] + 2 * blk[1] + blk[2]]
            return pltpu.make_async_remote_copy(src_ref=x_refs[i] if own else slot, dst_ref=slot,
                                                send_sem=send_sems.at[7 * i + k], recv_sem=recv_sems.at[7 * i + k],
                                                device_id=to, device_id_type=MESH)

        mine = [pltpu.make_async_copy(x_refs[i], out_refs[i].at[4 * x + 2 * y + c], local_sems.at[i])
                for i in range(n)]
        for cp in mine:
            cp.start()
        first = []
        for i in range(n):
            first.append(copy(i, 0, me, sibling, own=True))
            first += [copy(i, 1 + j, me, (*chip, c), own=True) for j, chip in enumerate(chips)]
        for cp in first:
            cp.start()
        passed = []
        for j, chip in enumerate(chips):
            for i in range(n):
                copy(i, 1 + j, (*chip, c), me).wait_recv()
                fwd = copy(i, 4 + j, (*chip, c), sibling)
                fwd.start()
                passed.append(fwd)
        for i in range(n):
            copy(i, 0, sibling, me).wait_recv()
            for j, chip in enumerate(chips):
                copy(i, 4 + j, (*chip, 1 - c), me).wait_recv()
        for cp in first + passed:
            cp.wait_send()
        for cp in mine:
            cp.wait()

    any_spec = pl.BlockSpec(memory_space=pl.ANY)
    return pl.pallas_call(body, out_shape=tuple(_sds((N_DEV,) + b.shape, b.dtype) for b in blocks),
                          in_specs=[any_spec] * n, out_specs=(any_spec,) * n,
                          scratch_shapes=[pltpu.SemaphoreType.DMA((7 * n,)), pltpu.SemaphoreType.DMA((7 * n,)),
                                          pltpu.SemaphoreType.DMA((n,))], name=name)(*blocks)


def _exchange(sends, *, name):
    n = len(sends)

    def body(*refs):
        send_refs, recv_refs = refs[:n], refs[n:2 * n]
        send_sems, recv_sems, local_sems = refs[2 * n:]
        x, y, c = lax.axis_index("x"), lax.axis_index("y"), lax.axis_index("c")
        me = 4 * x + 2 * y + c
        mine = [pltpu.make_async_copy(send_refs[i].at[me], recv_refs[i].at[me], local_sems.at[i]) for i in range(n)]
        for cp in mine:
            cp.start()
        copies = []
        for k in range(1, N_DEV):
            px = 1 - x if k & 4 else x
            py = 1 - y if k & 2 else y
            pc = 1 - c if k & 1 else c
            for i in range(n):
                cp = pltpu.make_async_remote_copy(src_ref=send_refs[i].at[4 * px + 2 * py + pc],
                                                  dst_ref=recv_refs[i].at[me],
                                                  send_sem=send_sems.at[7 * i + k - 1],
                                                  recv_sem=recv_sems.at[7 * i + k - 1],
                                                  device_id=(px, py, pc), device_id_type=MESH)
                cp.start()
                copies.append(cp)
        for cp in copies:
            cp.wait_recv()
        for cp in copies:
            cp.wait_send()
        for cp in mine:
            cp.wait()

    any_spec = pl.BlockSpec(memory_space=pl.ANY)
    return pl.pallas_call(body, out_shape=tuple(_sds(s.shape, s.dtype) for s in sends), in_specs=[any_spec] * n,
                          out_specs=(any_spec,) * n,
                          scratch_shapes=[pltpu.SemaphoreType.DMA((7 * n,)), pltpu.SemaphoreType.DMA((7 * n,)),
                                          pltpu.SemaphoreType.DMA((n,))], name=name)(*sends)


def _adamw(w, m, v, g_parts, *, name, tr):
    R, W = w.shape
    bc1 = 1.0 - ADAM_B1 ** ADAM_STEP
    bc2 = 1.0 - ADAM_B2 ** ADAM_STEP

    def body(w_ref, m_ref, v_ref, g_ref, go_ref, d_ref, mo_ref, vo_ref):
        g = g_ref[0].astype(F32)
        for i in range(1, N_DEV):
            g = g + g_ref[i].astype(F32)
        mn = ADAM_B1 * m_ref[...] + (1.0 - ADAM_B1) * g
        vn = ADAM_B2 * v_ref[...] + (1.0 - ADAM_B2) * (g * g)
        m_hat = mn / bc1
        v_hat = vn / bc2
        go_ref[...] = g
        d_ref[...] = -ADAM_LR * (m_hat / (jnp.sqrt(v_hat) + ADAM_EPS) + ADAM_WD * w_ref[...])
        mo_ref[...] = mn
        vo_ref[...] = vn

    row = pl.BlockSpec((tr, W), lambda i: (i, 0))
    return _call(body, grid=(R // tr,), in_specs=[row, row, row, pl.BlockSpec((N_DEV, tr, W), lambda i: (0, i, 0))],
                 out_specs=(row,) * 4, out_shape=(_sds((R, W), F32),) * 4, name=name)(w, m, v, g_parts)


def _round_up(n, k):
    return -(-n // k) * k


def _pack(arrs, width, row_mult):
    pieces, offs, r = [], [], 0
    for a in arrs:
        n = a.size
        rows = _round_up(n, width) // width
        flat = a.reshape(-1)
        if rows * width != n:
            flat = jnp.pad(flat, (0, rows * width - n))
        pieces.append(flat.reshape(rows, width))
        offs.append((r, rows, n))
        r += rows
    total = _round_up(r, row_mult)
    if total != r:
        pieces.append(jnp.zeros((total - r, width), pieces[0].dtype))
    return jnp.concatenate(pieces, axis=0), offs


def _unpack(pack, offs, shapes):
    out = []
    for (r, rows, n), shp in zip(offs, shapes):
        out.append(pack[r:r + rows].reshape(-1)[:n].reshape(shp))
    return out


def _gather_axis(full8, axis):
    moved = jnp.moveaxis(full8, 0, axis)
    shp = list(moved.shape)
    shp[axis:axis + 2] = [shp[axis] * shp[axis + 1]]
    return moved.reshape(shp)


def _split_axis(full, axis):
    shp = list(full.shape)
    shp[axis:axis + 1] = [N_DEV, shp[axis] // N_DEV]
    return jnp.moveaxis(full.reshape(shp), axis, 0)


def kernel(x, w_in, b_in, conv_dw_w, conv_dw_b, conv_ln_g, conv_ln_b, rel_bias_table, gmlp_ln_g, gmlp_ln_b, gmlp_w_s, gmlp_b_s, w_out, b_out, ln1_g, ln1_b, ffn_w_up, ffn_b_up, ffn_conv_w, ffn_conv_b, ffn_w_down, ffn_b_down, ln2_g, ln2_b, loss_target, m_w_in, m_b_in, m_conv_dw_w, m_conv_dw_b, m_conv_ln_g, m_conv_ln_b, m_rel_bias_table, m_gmlp_ln_g, m_gmlp_ln_b, m_gmlp_w_s, m_gmlp_b_s, m_w_out, m_b_out, m_ln1_g, m_ln1_b, m_ffn_w_up, m_ffn_b_up, m_ffn_conv_w, m_ffn_conv_b, m_ffn_w_down, m_ffn_b_down, m_ln2_g, m_ln2_b, v_w_in, v_b_in, v_conv_dw_w, v_conv_dw_b, v_conv_ln_g, v_conv_ln_b, v_rel_bias_table, v_gmlp_ln_g, v_gmlp_ln_b, v_gmlp_w_s, v_gmlp_b_s, v_w_out, v_b_out, v_ln1_g, v_ln1_b, v_ffn_w_up, v_ffn_b_up, v_ffn_conv_w, v_ffn_conv_b, v_ffn_w_down, v_ffn_b_down, v_ln2_g, v_ln2_b):
    W = dict(w_in=w_in, b_in=b_in, conv_dw_w=conv_dw_w, conv_dw_b=conv_dw_b, conv_ln_g=conv_ln_g,
             conv_ln_b=conv_ln_b, rel_bias_table=rel_bias_table, gmlp_ln_g=gmlp_ln_g, gmlp_ln_b=gmlp_ln_b,
             gmlp_w_s=gmlp_w_s, gmlp_b_s=gmlp_b_s, w_out=w_out, b_out=b_out, ln1_g=ln1_g, ln1_b=ln1_b,
             ffn_w_up=ffn_w_up, ffn_b_up=ffn_b_up, ffn_conv_w=ffn_conv_w, ffn_conv_b=ffn_conv_b,
             ffn_w_down=ffn_w_down, ffn_b_down=ffn_b_down, ln2_g=ln2_g, ln2_b=ln2_b)
    Mo = dict(w_in=m_w_in, b_in=m_b_in, conv_dw_w=m_conv_dw_w, conv_dw_b=m_conv_dw_b, conv_ln_g=m_conv_ln_g,
              conv_ln_b=m_conv_ln_b, rel_bias_table=m_rel_bias_table, gmlp_ln_g=m_gmlp_ln_g, gmlp_ln_b=m_gmlp_ln_b,
              gmlp_w_s=m_gmlp_w_s, gmlp_b_s=m_gmlp_b_s, w_out=m_w_out, b_out=m_b_out, ln1_g=m_ln1_g, ln1_b=m_ln1_b,
              ffn_w_up=m_ffn_w_up, ffn_b_up=m_ffn_b_up, ffn_conv_w=m_ffn_conv_w, ffn_conv_b=m_ffn_conv_b,
              ffn_w_down=m_ffn_w_down, ffn_b_down=m_ffn_b_down, ln2_g=m_ln2_g, ln2_b=m_ln2_b)
    Vo = dict(w_in=v_w_in, b_in=v_b_in, conv_dw_w=v_conv_dw_w, conv_dw_b=v_conv_dw_b, conv_ln_g=v_conv_ln_g,
              conv_ln_b=v_conv_ln_b, rel_bias_table=v_rel_bias_table, gmlp_ln_g=v_gmlp_ln_g, gmlp_ln_b=v_gmlp_ln_b,
              gmlp_w_s=v_gmlp_w_s, gmlp_b_s=v_gmlp_b_s, w_out=v_w_out, b_out=v_b_out, ln1_g=v_ln1_g, ln1_b=v_ln1_b,
              ffn_w_up=v_ffn_w_up, ffn_b_up=v_ffn_b_up, ffn_conv_w=v_ffn_conv_w, ffn_conv_b=v_ffn_conv_b,
              ffn_w_down=v_ffn_w_down, ffn_b_down=v_ffn_b_down, ln2_g=v_ln2_g, ln2_b=v_ln2_b)

    xs = x[0]
    target = loss_target[0]
    S, D = xs.shape
    F2 = ffn_b_up.shape[1]
    F = F2 // 2
    ts = min(512, S)
    G = _attn_group(S)
    tc = F // 2 if (F // 2) % LANES == 0 else F

    mat_names = SHARDED[:4]
    payload = [W[n].astype(BF16) if n in mat_names else W[n] for n in SHARDED]
    wall = _allgather(payload, name="weight_allgather")
    full = {n: _gather_axis(parts, SHARD_AXIS[n]) for n, parts in zip(SHARDED, wall)}

    tables = [_pattern_tables(w, d) for (w, d) in PATTERNS]
    bucket_flat = jnp.asarray(np.stack([np.where(v, b, -1).reshape(1, -1) for (b, v) in tables]).astype(np.int32))
    bias_all = _bias_tables(rel_bias_table, bucket_flat, name="bias_tables")
    biases = [bias_all[p].reshape(ATTN_HEADS, ATTN_BLOCK, 2 * ATTN_BLOCK) for p in range(len(PATTERNS))]
    nbs = [(S // d) // ATTN_BLOCK for (_, d) in PATTERNS]
    dils = [d for (_, d) in PATTERNS]
    scale = HEAD_DIM ** -0.5

    saved = []
    cur = xs
    for l in range(DEPTH):
        Win, Wout, Wup, Wdown = full['w_in'][l], full['w_out'][l], full['ffn_w_up'][l], full['ffn_w_down'][l]
        qcols = slice(2 * CONV_CH, 2 * CONV_CH + ATTN_CH)
        Win_s = Win.at[:, qcols].multiply(scale)
        b_in_s = b_in[l].at[qcols].multiply(scale)
        h_a, qkv, h_c = _mm([cur], [Win_s], bias=b_in_s, tm=ts, name="in_proj",
                            splits=((2 * CONV_CH, F32, 1.0), (3 * ATTN_CH, F32, 1.0), (2 * GMLP_CH, F32, 1.0)))
        conv_out, hc = _conv_fwd(h_a, full['conv_dw_w'][l], conv_dw_b[l], conv_ln_g[l], conv_ln_b[l],
                                 name="conv_fwd", ts=ts)
        qkv_d = _dilate_qkv(qkv, name="dilate_qkv", ts=ts)
        o_ps, lse_ps = [], []
        for p, d in enumerate(dils):
            o_p, lse_p = _attn_fwd(qkv_d[p], biases[p], name=f"attn_fwd_d{d}", nb=nbs[p], G=G)
            o_ps.append(o_p)
            lse_ps.append(lse_p)
        attn_out, lse = _attn_merge(o_ps, lse_ps, dils, name="attn_merge", ts=ts)
        w_tril = jnp.tril(gmlp_w_s[l]).astype(BF16)
        bs_rows = jnp.repeat(gmlp_b_s[l].T, GMLP_GROUP_DIM, axis=1)
        gm_out = _gmlp_fwd(h_c, gmlp_ln_g[l], gmlp_ln_b[l], w_tril, bs_rows, name="gmlp_fwd", ts=ts)
        x1, xhat1, rstd1 = _mm([conv_out, attn_out, gm_out],
                               [Wout[:CONV_CH], Wout[CONV_CH:CONV_CH + ATTN_CH], Wout[CONV_CH + ATTN_CH:]],
                               bias=b_out[l], resid=cur, resid_scale=ALPHA, ln=(ln1_g[l], ln1_b[l]), tm=ts,
                               name="out_proj_ln")
        hu = _mm([x1], [Wup], bias=ffn_b_up[l], tm=ts, tn=F, name="ffn_up")
        act, act_dg, act_dv = _ffn_act_fwd(hu, full['ffn_conv_w'][l], ffn_conv_b[l], name="ffn_act_fwd",
                                           ts=ts, tc=tc)
        x2, xhat2, rstd2 = _mm([act], [Wdown], bias=ffn_b_down[l], resid=x1, resid_scale=ALPHA,
                               ln=(ln2_g[l], ln2_b[l]), tm=ts, name="ffn_down_ln")
        saved.append(dict(x0=cur, h_a=h_a, h_c=h_c, qkv_d=qkv_d, hc=hc, conv_out=conv_out, attn_out=attn_out,
                          lse=lse, gm_out=gm_out, w_tril=w_tril, bs_rows=bs_rows, x1=x1, xhat1=xhat1, rstd1=rstd1,
                          hu=hu, act=act, act_dg=act_dg, act_dv=act_dv, xhat2=xhat2, rstd2=rstd2))
        cur = x2

    grads = {n: [None] * DEPTH for n in WEIGHTS if n != 'rel_bias_table'}
    drel = None
    dx = None
    loss_part = None
    tk = min(1024, S)
    for l in reversed(range(DEPTH)):
        sv = saved[l]
        Win, Wout, Wup, Wdown = full['w_in'][l], full['w_out'][l], full['ffn_w_up'][l], full['ffn_w_down'][l]
        if dx is None:
            dz2, dg2, db2, dzs2, loss_part = _ln_bwd(sv['xhat2'], sv['rstd2'], ln2_g[l], b=ln2_b[l], target=target,
                                                     name="ln2_bwd_loss", ts=ts)
        else:
            dz2, dg2, db2, dzs2 = _ln_bwd(sv['xhat2'], sv['rstd2'], ln2_g[l], dy=dx, name="ln_bwd", ts=ts)
        grads['ln2_g'][l], grads['ln2_b'][l], grads['ffn_b_down'][l] = dg2[0], db2[0], dzs2[0]
        grads['ffn_w_down'][l] = _mm_tn(sv['act'], dz2, tm=F // 2 if (F // 2) % LANES == 0 else F, tn=D, tk=tk,
                                        name="dw_down")
        dact = _mm([dz2], [Wdown.T], tm=ts, name="dact")
        dhg, dhv, dwg, dwv, dbg, dbv, dug, duv = _ffn_act_bwd(sv['hu'], dact, sv['act_dg'], sv['act_dv'],
                                                              full['ffn_conv_w'][l], name="ffn_act_bwd",
                                                              ts=ts, tc=tc)
        grads['ffn_conv_w'][l] = jnp.concatenate([dwg, dwv], axis=1)
        grads['ffn_conv_b'][l] = jnp.concatenate([dbg, dbv], axis=1)[0]
        grads['ffn_b_up'][l] = jnp.concatenate([dug, duv], axis=1)[0]
        grads['ffn_w_up'][l] = jnp.concatenate(
            [_mm_tn(sv['x1'], dhg, tm=D, tn=tc, tk=tk, name="dw_up"),
             _mm_tn(sv['x1'], dhv, tm=D, tn=tc, tk=tk, name="dw_up")], axis=1)
        WupT = Wup.T
        dx1 = _mm([dhg, dhv], [WupT[:F], WupT[F:]], resid=dz2, resid_scale=ALPHA, tm=ts, name="dx1")
        dz1, dg1, db1, dzs1 = _ln_bwd(sv['xhat1'], sv['rstd1'], ln1_g[l], dy=dx1, name="ln_bwd", ts=ts)
        grads['ln1_g'][l], grads['ln1_b'][l], grads['b_out'][l] = dg1[0], db1[0], dzs1[0]
        grads['w_out'][l] = jnp.concatenate(
            [_mm_tn(sv['conv_out'], dz1, tm=CONV_CH, tn=D, tk=tk, name="dw_out_conv"),
             _mm_tn(sv['attn_out'], dz1, tm=ATTN_CH, tn=D, tk=tk, name="dw_out_attn"),
             _mm_tn(sv['gm_out'], dz1, tm=GMLP_CH, tn=D, tk=tk, name="dw_out_conv")], axis=0)
        dc_conv, dc_attn, dc_gm = _mm([dz1], [Wout.T], tm=ts, name="dcat",
                                      splits=((CONV_CH, F32, 1.0), (ATTN_CH, F32, 1.0), (GMLP_CH, F32, 1.0)))
        dh_a, ddw, ddwb, dclg, dclb = _conv_bwd(sv['h_a'], sv['hc'], dc_conv, full['conv_dw_w'][l], conv_ln_g[l],
                                                conv_ln_b[l], name="conv_bwd", ts=ts)
        grads['conv_dw_w'][l], grads['conv_dw_b'][l] = ddw[:CONV_WIDTH], ddwb[0]
        grads['conv_ln_g'][l], grads['conv_ln_b'][l] = dclg[0], dclb[0]
        dd_d, do_d = _attn_prep(dc_attn, sv['attn_out'], dils, name="attn_prep", ts=ts)
        dqs, dks, dvs, dbs = [], [], [], []
        for p, d in enumerate(dils):
            dq, dk, dv, dbias = _attn_bwd(sv['qkv_d'][p], do_d[p], sv['lse'][p], dd_d[p], biases[p],
                                          name=f"attn_bwd_d{d}", nb=nbs[p], G=G)
            dqs.append(dq)
            dks.append(dk)
            dvs.append(dv)
            dbs.append(dbias.reshape(1, ATTN_HEADS, -1))
        dqkv = _attn_combine(dqs, dks, dvs, dils, name="attn_combine", ts=ts)
        dr = _bias_grad(jnp.concatenate(dbs, axis=0), bucket_flat, name="bias_grad")
        drel = dr if drel is None else drel + dr
        w_tril_t = jnp.swapaxes(sv['w_tril'], 1, 2)
        dh_c, dws, dbs_acc, dglg, dglb = _gmlp_bwd(sv['h_c'], dc_gm, gmlp_ln_g[l], gmlp_ln_b[l], sv['w_tril'],
                                                   w_tril_t, sv['bs_rows'], name="gmlp_bwd", ts=ts)
        grads['gmlp_w_s'][l] = jnp.tril(dws)
        grads['gmlp_b_s'][l] = dbs_acc[:, :GMLP_GROUPS].T
        grads['gmlp_ln_g'][l], grads['gmlp_ln_b'][l] = dglg[0], dglb[0]
        dw_in = _mm_tn_shared(sv['x0'], [dh_a, *dqkv, dh_c], tk=tk, name="dw_in")
        grads['w_in'][l] = jnp.concatenate([w for w, _ in dw_in], axis=1)
        grads['b_in'][l] = jnp.concatenate([c for _, c in dw_in], axis=1)[0]
        WinT = Win.T
        edges = [0, 2 * CONV_CH] + [2 * CONV_CH + k * ATTN_CH for k in (1, 2, 3)] + [WinT.shape[0]]
        dx = _mm([dh_a, *dqkv, dh_c], [WinT[a:b] for a, b in zip(edges[:-1], edges[1:])], resid=dz1,
                 resid_scale=ALPHA, tm=ts, name="dx0")

    gfull = {n: jnp.stack(v) for n, v in grads.items()}
    gfull['rel_bias_table'] = drel

    sends = []
    for n in SHARDED:
        parts = _split_axis(gfull[n], SHARD_AXIS[n])
        sends.append(parts.reshape(N_DEV, -1, parts.shape[-1]).astype(BF16))
    recvs = _exchange(sends, name="grad_exchange")
    shard_out = [[], [], [], []]
    for n, recv in zip(SHARDED, recvs):
        shp = W[n].shape
        rows = recv.shape[1]
        tr = rows // 4 if rows % 64 == 0 else rows
        outs = _adamw(*[src[n].reshape(rows, shp[-1]) for src in (W, Mo, Vo)], recv, name=f"adamw_{n}", tr=tr)
        for kind in range(4):
            shard_out[kind].append(outs[kind].reshape(shp))

    gsmall, soffs = _pack([gfull[n] for n in SMALL], LANES, 8)
    gall = _allgather([gsmall], name="small_grad_allgather")[0]
    spacks = [_pack([src[n] for n in SMALL], LANES, 8)[0] for src in (W, Mo, Vo)]
    souts = _adamw(spacks[0], spacks[1], spacks[2], gall, name="adamw_small", tr=gsmall.shape[0])
    small_out = [_unpack(o, soffs, [W[n].shape for n in SMALL]) for o in souts]

    loss = lax.psum(loss_part[0, 0], ("x", "y", "c"))
    by_kind = []
    for kind in range(4):
        d = dict(zip(SHARDED, shard_out[kind]))
        d.update(zip(SMALL, small_out[kind]))
        by_kind.append([d[n] for n in WEIGHTS])
    return (loss, dx[None], *by_kind[0], *by_kind[1], *by_kind[2], *by_kind[3])
```

```python
import math

import numpy as np
import jax
import jax.numpy as jnp
from jax import lax
from jax.experimental import pallas as pl
from jax.experimental.pallas import tpu as pltpu

F32 = jnp.float32
BF16 = jnp.bfloat16

DEPTH = 2
HEAD_DIM = 64
CONV_CH = 256
CONV_WIDTH = 31
ATTN_HEADS = 8
ATTN_CH = ATTN_HEADS * HEAD_DIM
PATTERNS = ((128, 1), (512, 4), (2048, 16))
ATTN_BLOCK = 128
N_BUCKETS = 32
MAX_DISTANCE = 2048
GMLP_CH = 256
GMLP_GROUPS = 4
GMLP_GROUP_DIM = GMLP_CH // GMLP_GROUPS
CHUNK = 128
FFN_CONV_WIDTH = 3
LN_EPS = 1e-5
ALPHA = (2.0 * DEPTH) ** 0.25
ADAM_LR = 0.001
ADAM_B1 = 0.9
ADAM_B2 = 0.999
ADAM_EPS = 1e-08
ADAM_WD = 0.01
ADAM_STEP = 10
NEG = -1e30
N_DEV = 8
LANES = 128
CONV_HALO = 32
FFN_HALO = 8
FFN_ROWS = 16
FFN_STRIP = 256
MESH = pl.DeviceIdType.MESH

WEIGHTS = ['w_in', 'b_in', 'conv_dw_w', 'conv_dw_b', 'conv_ln_g', 'conv_ln_b', 'rel_bias_table', 'gmlp_ln_g',
           'gmlp_ln_b', 'gmlp_w_s', 'gmlp_b_s', 'w_out', 'b_out', 'ln1_g', 'ln1_b', 'ffn_w_up', 'ffn_b_up',
           'ffn_conv_w', 'ffn_conv_b', 'ffn_w_down', 'ffn_b_down', 'ln2_g', 'ln2_b']
SHARDED = ['w_in', 'w_out', 'ffn_w_up', 'ffn_w_down', 'conv_dw_w', 'ffn_conv_w']
SHARD_AXIS = {'w_in': 2, 'w_out': 1, 'ffn_w_up': 2, 'ffn_w_down': 1, 'conv_dw_w': 2, 'ffn_conv_w': 2}
SMALL = [n for n in WEIGHTS if n not in SHARDED]


def _call(body, *, grid=(), vmem_mb=48, **kw):
    params = pltpu.CompilerParams(dimension_semantics=("arbitrary",) * len(grid), vmem_limit_bytes=vmem_mb << 20)
    return pl.pallas_call(body, grid=grid, compiler_params=params, **kw)


def _sds(shape, dtype):
    return jax.ShapeDtypeStruct(shape, dtype)


def _ln_rows(z):
    mu = jnp.mean(z, axis=-1, keepdims=True)
    zc = z - mu
    var = jnp.mean(zc * zc, axis=-1, keepdims=True)
    rstd = lax.rsqrt(var + LN_EPS)
    return zc * rstd, rstd


def _ln_bwd_rows(dxhat, xhat, rstd):
    m1 = jnp.mean(dxhat, axis=-1, keepdims=True)
    m2 = jnp.mean(dxhat * xhat, axis=-1, keepdims=True)
    return rstd * (dxhat - m1 - xhat * m2)


def _colsum(v):
    return jnp.sum(v, axis=0, keepdims=True)


def _mm(a_list, w_list, *, name, tm, tn=None, bias=None, resid=None, resid_scale=1.0, ln=None, splits=None,
        out_dtype=F32):
    na = len(a_list)
    M = a_list[0].shape[0]
    N = w_list[0].shape[1]
    tn = N if tn is None else tn
    assert M % tm == 0 and N % tn == 0
    assert ln is None or tn == N
    assert splits is None or tn == N

    def body(*refs):
        a_refs, w_refs = refs[:na], refs[na:2 * na]
        pos = 2 * na
        acc = None
        for a_ref, w_ref in zip(a_refs, w_refs):
            t = jnp.dot(a_ref[...].astype(BF16), w_ref[...], preferred_element_type=F32)
            acc = t if acc is None else acc + t
        if bias is not None:
            acc = acc + refs[pos][...]
            pos += 1
        if resid is not None:
            acc = resid_scale * refs[pos][...] + acc
            pos += 1
        if ln is not None:
            g_ref, b_ref = refs[pos], refs[pos + 1]
            y_ref, xhat_ref, rstd_ref = refs[pos + 2], refs[pos + 3], refs[pos + 4]
            xhat, rstd = _ln_rows(acc)
            y_ref[...] = xhat * g_ref[...] + b_ref[...]
            xhat_ref[...] = xhat
            rstd_ref[...] = rstd
        elif splits is not None:
            c0 = 0
            for o_ref, (width, dtype, scale) in zip(refs[pos:], splits):
                part = acc[:, c0:c0 + width]
                if scale != 1.0:
                    part = part * scale
                o_ref[...] = part.astype(dtype)
                c0 += width
        else:
            refs[pos][...] = acc.astype(out_dtype)

    in_specs = [pl.BlockSpec((tm, a.shape[1]), lambda j, i: (i, 0)) for a in a_list]
    in_specs += [pl.BlockSpec((w.shape[0], tn), lambda j, i: (0, j)) for w in w_list]
    args = list(a_list) + list(w_list)
    if bias is not None:
        in_specs.append(pl.BlockSpec((1, tn), lambda j, i: (0, j)))
        args.append(bias.reshape(1, N))
    if resid is not None:
        in_specs.append(pl.BlockSpec((tm, tn), lambda j, i: (i, j)))
        args.append(resid)
    if ln is not None:
        in_specs += [pl.BlockSpec((1, N), lambda j, i: (0, 0))] * 2
        args += [ln[0].reshape(1, N), ln[1].reshape(1, N)]
        out_shape = (_sds((M, N), F32), _sds((M, N), F32), _sds((M, 1), F32))
        out_specs = (pl.BlockSpec((tm, N), lambda j, i: (i, 0)), pl.BlockSpec((tm, N), lambda j, i: (i, 0)),
                     pl.BlockSpec((tm, 1), lambda j, i: (i, 0)))
    elif splits is not None:
        out_shape = tuple(_sds((M, w), d) for (w, d, _) in splits)
        out_specs = tuple(pl.BlockSpec((tm, w), lambda j, i: (i, 0)) for (w, _, _) in splits)
    else:
        out_shape = _sds((M, N), out_dtype)
        out_specs = pl.BlockSpec((tm, tn), lambda j, i: (i, j))
    return _call(body, grid=(N // tn, M // tm), in_specs=in_specs, out_specs=out_specs, out_shape=out_shape,
                 name=name, vmem_mb=56)(*args)


def _mm_tn(a, dy, *, name, tm, tn, tk, colsum=False):
    S, Ka = a.shape
    N = dy.shape[1]
    assert S % tk == 0 and Ka % tm == 0 and N % tn == 0

    def body(a_ref, dy_ref, out_ref, *cs):
        i, k = pl.program_id(1), pl.program_id(2)
        dyb = dy_ref[...]
        part = lax.dot_general(a_ref[...].astype(BF16), dyb.astype(BF16), (((0,), (0,)), ((), ())),
                               preferred_element_type=F32)

        @pl.when(k == 0)
        def _():
            out_ref[...] = part

        @pl.when(k > 0)
        def _():
            out_ref[...] += part

        if colsum:
            cs_ref = cs[0]
            s = _colsum(dyb.astype(F32))

            @pl.when((i == 0) & (k == 0))
            def _():
                cs_ref[...] = s

            @pl.when((i == 0) & (k > 0))
            def _():
                cs_ref[...] += s

    out_shape = [_sds((Ka, N), F32)]
    out_specs = [pl.BlockSpec((tm, tn), lambda j, i, k: (i, j))]
    if colsum:
        out_shape.append(_sds((1, N), F32))
        out_specs.append(pl.BlockSpec((1, tn), lambda j, i, k: (0, j)))
    res = _call(body, grid=(N // tn, Ka // tm, S // tk),
                in_specs=[pl.BlockSpec((tk, tm), lambda j, i, k: (k, i)), pl.BlockSpec((tk, tn), lambda j, i, k: (k, j))],
                out_specs=tuple(out_specs), out_shape=tuple(out_shape), name=name, vmem_mb=56)(a, dy)
    return res if colsum else res[0]


def _mm_tn_shared(a, dys, *, name, tk):
    S, Ka = a.shape
    n = len(dys)
    assert S % tk == 0

    def body(*refs):
        a_ref, dy_refs, outs = refs[0], refs[1:1 + n], refs[1 + n:]
        k = pl.program_id(0)
        a_bf = a_ref[...].astype(BF16)
        for i, dy_ref in enumerate(dy_refs):
            dyb = dy_ref[...]
            part = lax.dot_general(a_bf, dyb.astype(BF16), (((0,), (0,)), ((), ())), preferred_element_type=F32)
            s = _colsum(dyb.astype(F32))
            w_ref, c_ref = outs[2 * i], outs[2 * i + 1]

            @pl.when(k == 0)
            def _():
                w_ref[...] = part
                c_ref[...] = s

            @pl.when(k > 0)
            def _():
                w_ref[...] += part
                c_ref[...] += s

    in_specs = [pl.BlockSpec((tk, Ka), lambda k: (k, 0))]
    in_specs += [pl.BlockSpec((tk, dy.shape[1]), lambda k: (k, 0)) for dy in dys]
    out_specs, out_shape = [], []
    for dy in dys:
        N = dy.shape[1]
        out_specs += [pl.BlockSpec((Ka, N), lambda k: (0, 0)), pl.BlockSpec((1, N), lambda k: (0, 0))]
        out_shape += [_sds((Ka, N), F32), _sds((1, N), F32)]
    res = _call(body, grid=(S // tk,), in_specs=in_specs, out_specs=tuple(out_specs), out_shape=tuple(out_shape),
                name=name, vmem_mb=56)(a, *dys)
    return [(res[2 * i], res[2 * i + 1]) for i in range(n)]


def _ln_bwd(xhat, rstd, g, *, name, ts, dy=None, b=None, target=None):
    S, D = xhat.shape
    from_loss = target is not None

    def body(*refs):
        if from_loss:
            xhat_ref, rstd_ref, g_ref, b_ref, t_ref, dz_ref, dg_ref, db_ref, dzs_ref, loss_ref = refs
        else:
            xhat_ref, rstd_ref, g_ref, dy_ref, dz_ref, dg_ref, db_ref, dzs_ref = refs
        i = pl.program_id(0)
        xh = xhat_ref[...]
        gg = g_ref[...]
        if from_loss:
            err = xh * gg + b_ref[...] - t_ref[...]
            dyv = err * (1.0 / D)
            lsum = (0.5 / D) * jnp.sum(err * err, axis=(0, 1), keepdims=True)
        else:
            dyv = dy_ref[...]
        dz = _ln_bwd_rows(dyv * gg, xh, rstd_ref[...])
        dz_ref[...] = dz
        parts = [(dg_ref, _colsum(dyv * xh)), (db_ref, _colsum(dyv)), (dzs_ref, _colsum(dz))]
        if from_loss:
            parts.append((loss_ref, lsum))

        @pl.when(i == 0)
        def _():
            for r, v in parts:
                r[...] = v

        @pl.when(i > 0)
        def _():
            for r, v in parts:
                r[...] += v

    row = pl.BlockSpec((ts, D), lambda i: (i, 0))
    vec = pl.BlockSpec((1, D), lambda i: (0, 0))
    in_specs = [row, pl.BlockSpec((ts, 1), lambda i: (i, 0)), vec]
    args = [xhat, rstd, g.reshape(1, D)]
    if from_loss:
        in_specs += [vec, row]
        args += [b.reshape(1, D), target]
    else:
        in_specs += [row]
        args += [dy]
    out_shape = [_sds((S, D), F32), _sds((1, D), F32), _sds((1, D), F32), _sds((1, D), F32)]
    out_specs = [row, vec, vec, vec]
    if from_loss:
        out_shape.append(_sds((1, 1), F32))
        out_specs.append(pl.BlockSpec((1, 1), lambda i: (0, 0)))
    return _call(body, grid=(S // ts,), in_specs=in_specs, out_specs=tuple(out_specs), out_shape=tuple(out_shape),
                 name=name)(*args)


def _glu(v):
    return v[:, :CONV_CH] * jax.nn.sigmoid(v[:, CONV_CH:])


def _copies_moved_back(buf, sh, rows):
    for b in range(1, 8):
        sh[b - 1, 8:rows, :] = buf[pl.ds(8 - b, rows - 8), :]


def _copies_moved_ahead(buf, sh, rows):
    for b in range(1, 8):
        sh[b - 1, 0:rows - 8, :] = buf[pl.ds(b, rows - 8), :]


def _rows_back(buf, sh, start, s, n):
    a, b = divmod(s, 8)
    return buf[pl.ds(start - 8 * a, n), :] if b == 0 else sh[b - 1, pl.ds(start - 8 * a, n), :]


def _rows_ahead(buf, sh, start, s, n):
    a, b = divmod(s, 8)
    return buf[pl.ds(start + 8 * a, n), :] if b == 0 else sh[b - 1, pl.ds(start + 8 * a, n), :]


def _conv_fwd(h_a, dw_w, dw_b, ln_g, ln_b, *, name, ts):
    S = h_a.shape[0]
    C, K, HB = CONV_CH, CONV_WIDTH, CONV_HALO
    RC = 128

    def body(h_ref, halo_ref, w_ref, b_ref, g_ref, bb_ref, out_ref, hc_ref, gbuf, gsh):
        i = pl.program_id(0)
        gbuf[0:HB, :] = jnp.where(i > 0, _glu(halo_ref[...]), 0.0)
        gbuf[HB:HB + ts, :] = _glu(h_ref[...])
        _copies_moved_back(gbuf, gsh, HB + ts)
        for r0 in range(0, ts, RC):
            acc = jnp.zeros((RC, C), F32) + b_ref[...]
            for k in range(K):
                acc = acc + w_ref[k:k + 1, :] * _rows_back(gbuf, gsh, r0 + HB, K - 1 - k, RC)
            hc_ref[r0:r0 + RC, :] = acc
            xhat, _ = _ln_rows(acc)
            hn = xhat * g_ref[...] + bb_ref[...]
            out_ref[r0:r0 + RC, :] = (hn * jax.nn.sigmoid(hn)).astype(BF16)

    nb = ts // HB
    vec = pl.BlockSpec((1, C), lambda i: (0, 0))
    return _call(body, grid=(S // ts,),
                 in_specs=[pl.BlockSpec((ts, 2 * C), lambda i: (i, 0)),
                           pl.BlockSpec((HB, 2 * C), lambda i: (jnp.maximum(i * nb - 1, 0), 0)),
                           pl.BlockSpec((K, C), lambda i: (0, 0)), vec, vec, vec],
                 out_specs=(pl.BlockSpec((ts, C), lambda i: (i, 0)), pl.BlockSpec((ts, C), lambda i: (i, 0))),
                 out_shape=(_sds((S, C), BF16), _sds((S, C), F32)),
                 scratch_shapes=[pltpu.VMEM((HB + ts, C), F32), pltpu.VMEM((7, HB + ts, C), F32)], name=name)(
        h_a, h_a, dw_w, dw_b.reshape(1, C), ln_g.reshape(1, C), ln_b.reshape(1, C))


def _conv_bwd(h_a, hc, dout, dw_w, ln_g, ln_b, *, name, ts):
    S = h_a.shape[0]
    C, K, HB = CONV_CH, CONV_WIDTH, CONV_HALO
    RC = 128
    n = S // ts

    def dconv_out(hc_v, do_v, g_ref, bb_ref):
        xhat, rstd = _ln_rows(hc_v)
        hn = xhat * g_ref[...] + bb_ref[...]
        sg = jax.nn.sigmoid(hn)
        dhn = do_v * (sg * (1.0 + hn * (1.0 - sg)))
        return _ln_bwd_rows(dhn * g_ref[...], xhat, rstd), dhn, xhat

    def body(h_ref, hprev_ref, hc_ref, hcnext_ref, do_ref, donext_ref, w_ref, g_ref, bb_ref,
             dh_ref, dw_ref, dwb_ref, dg_ref, db_ref, gbuf, dbuf, gsh, dsh):
        i = pl.program_id(0)
        hv = h_ref[...]
        gbuf[0:HB, :] = jnp.where(i > 0, _glu(hprev_ref[...]), 0.0)
        gbuf[HB:HB + ts, :] = _glu(hv)
        dhc, dhn, xhat = dconv_out(hc_ref[...], do_ref[...], g_ref, bb_ref)
        dhc_next, _, _ = dconv_out(hcnext_ref[...], donext_ref[...], g_ref, bb_ref)
        dbuf[0:ts, :] = dhc
        dbuf[ts:ts + HB, :] = jnp.where(i < n - 1, dhc_next, 0.0)
        _copies_moved_back(gbuf, gsh, HB + ts)
        _copies_moved_ahead(dbuf, dsh, ts + HB)
        dw_rows = []
        for k in range(K):
            acc_k = jnp.zeros((1, C), F32)
            for r0 in range(0, ts, RC):
                acc_k = acc_k + _colsum(dbuf[r0:r0 + RC, :] * _rows_back(gbuf, gsh, r0 + HB, K - 1 - k, RC))
            dw_rows.append(acc_k)
        dw_rows.append(jnp.zeros((1, C), F32))
        dw_tile = jnp.concatenate(dw_rows, axis=0)
        for r0 in range(0, ts, RC):
            acc = jnp.zeros((RC, C), F32)
            for k in range(K):
                acc = acc + w_ref[k:k + 1, :] * _rows_ahead(dbuf, dsh, r0, K - 1 - k, RC)
            a = hv[r0:r0 + RC, :C]
            sg = jax.nn.sigmoid(hv[r0:r0 + RC, C:])
            dh_ref[r0:r0 + RC, :C] = (acc * sg).astype(BF16)
            dh_ref[r0:r0 + RC, C:] = (acc * a * sg * (1.0 - sg)).astype(BF16)
        parts = [(dw_ref, dw_tile), (dwb_ref, _colsum(dhc)), (dg_ref, _colsum(dhn * xhat)), (db_ref, _colsum(dhn))]

        @pl.when(i == 0)
        def _():
            for r, v in parts:
                r[...] = v

        @pl.when(i > 0)
        def _():
            for r, v in parts:
                r[...] += v

    nb = ts // HB
    last = S // HB - 1
    vec = pl.BlockSpec((1, C), lambda i: (0, 0))
    nxt = lambda i: (jnp.minimum((i + 1) * nb, last), 0)
    return _call(body, grid=(n,),
                 in_specs=[pl.BlockSpec((ts, 2 * C), lambda i: (i, 0)),
                           pl.BlockSpec((HB, 2 * C), lambda i: (jnp.maximum(i * nb - 1, 0), 0)),
                           pl.BlockSpec((ts, C), lambda i: (i, 0)), pl.BlockSpec((HB, C), nxt),
                           pl.BlockSpec((ts, C), lambda i: (i, 0)), pl.BlockSpec((HB, C), nxt),
                           pl.BlockSpec((K, C), lambda i: (0, 0)), vec, vec],
                 out_specs=(pl.BlockSpec((ts, 2 * C), lambda i: (i, 0)), pl.BlockSpec((K + 1, C), lambda i: (0, 0)),
                            vec, vec, vec),
                 out_shape=(_sds((S, 2 * C), BF16), _sds((K + 1, C), F32), _sds((1, C), F32), _sds((1, C), F32),
                            _sds((1, C), F32)),
                 scratch_shapes=[pltpu.VMEM((HB + ts, C), F32), pltpu.VMEM((ts + HB, C), F32),
                                 pltpu.VMEM((7, HB + ts, C), F32), pltpu.VMEM((7, ts + HB, C), F32)], name=name)(
        h_a, h_a, hc, hc, dout, dout, dw_w, ln_g.reshape(1, C), ln_b.reshape(1, C))


def _gmlp_mix(vn_bf, w_ref, mix_buf, ts):
    for ch in range(ts // CHUNK):
        for g in range(GMLP_GROUPS):
            vg = vn_bf[ch * CHUNK:(ch + 1) * CHUNK, g * GMLP_GROUP_DIM:(g + 1) * GMLP_GROUP_DIM]
            mix_buf[ch * CHUNK:(ch + 1) * CHUNK, g * GMLP_GROUP_DIM:(g + 1) * GMLP_GROUP_DIM] = jnp.dot(
                w_ref[g], vg, preferred_element_type=F32)


def _gmlp_fwd(h_c, ln_g, ln_b, w_tril, bs_rows, *, name, ts):
    S = h_c.shape[0]
    C = GMLP_CH

    def body(h_ref, g_ref, b_ref, w_ref, bs_ref, out_ref, mix_buf):
        hv = h_ref[...]
        xhat, _ = _ln_rows(hv[:, C:])
        vn = (xhat * g_ref[...] + b_ref[...]).astype(BF16)
        _gmlp_mix(vn, w_ref, mix_buf, ts)
        for ch in range(ts // CHUNK):
            rows = slice(ch * CHUNK, (ch + 1) * CHUNK)
            out_ref[rows, :] = (hv[rows, :C] * (mix_buf[rows, :] + bs_ref[...])).astype(BF16)

    vec = pl.BlockSpec((1, C), lambda i: (0, 0))
    return _call(body, grid=(S // ts,),
                 in_specs=[pl.BlockSpec((ts, 2 * C), lambda i: (i, 0)), vec, vec,
                           pl.BlockSpec((GMLP_GROUPS, CHUNK, CHUNK), lambda i: (0, 0, 0)),
                           pl.BlockSpec((CHUNK, C), lambda i: (0, 0))],
                 out_specs=pl.BlockSpec((ts, C), lambda i: (i, 0)), out_shape=_sds((S, C), BF16),
                 scratch_shapes=[pltpu.VMEM((ts, C), F32)], name=name)(
        h_c, ln_g.reshape(1, C), ln_b.reshape(1, C), w_tril, bs_rows)


def _gmlp_bwd(h_c, dout, ln_g, ln_b, w_tril, w_tril_t, bs_rows, *, name, ts):
    S = h_c.shape[0]
    C, G, GD = GMLP_CH, GMLP_GROUPS, GMLP_GROUP_DIM

    def body(h_ref, do_ref, g_ref, b_ref, w_ref, wt_ref, bs_ref, dh_ref, dw_ref, dbs_ref, dg_ref, db_ref,
             mix_buf, dvn_buf):
        i = pl.program_id(0)
        hv = h_ref[...]
        u = hv[:, :C]
        xhat, rstd = _ln_rows(hv[:, C:])
        vn = (xhat * g_ref[...] + b_ref[...]).astype(BF16)
        _gmlp_mix(vn, w_ref, mix_buf, ts)
        do = do_ref[...]
        dmixed = do * u
        dm_bf = dmixed.astype(BF16)
        lane = lax.broadcasted_iota(jnp.int32, (CHUNK, LANES), 1)
        dbs = jnp.zeros((CHUNK, LANES), F32)
        dws = [jnp.zeros((CHUNK, CHUNK), F32) for _ in range(G)]
        for ch in range(ts // CHUNK):
            rows = slice(ch * CHUNK, (ch + 1) * CHUNK)
            dh_ref[rows, :C] = (do[rows, :] * (mix_buf[rows, :] + bs_ref[...])).astype(BF16)
            for g in range(G):
                cols = slice(g * GD, (g + 1) * GD)
                dmg = dm_bf[rows, cols]
                dvn_buf[rows, cols] = jnp.dot(wt_ref[g], dmg, preferred_element_type=F32)
                dws[g] = dws[g] + lax.dot_general(dmg, vn[rows, cols], (((1,), (1,)), ((), ())),
                                                  preferred_element_type=F32)
                rs = jnp.sum(dmixed[rows, cols], axis=1, keepdims=True)
                dbs = dbs + jnp.where(lane == g, rs, 0.0)
        dvn = dvn_buf[...]
        dh_ref[:, C:] = _ln_bwd_rows(dvn * g_ref[...], xhat, rstd).astype(BF16)
        dgv, dbv = _colsum(dvn * xhat), _colsum(dvn)

        @pl.when(i == 0)
        def _():
            for g in range(G):
                dw_ref[g] = dws[g]
            dbs_ref[...] = dbs
            dg_ref[...] = dgv
            db_ref[...] = dbv

        @pl.when(i > 0)
        def _():
            for g in range(G):
                dw_ref[g] += dws[g]
            dbs_ref[...] += dbs
            dg_ref[...] += dgv
            db_ref[...] += dbv

    vec = pl.BlockSpec((1, C), lambda i: (0, 0))
    wspec = pl.BlockSpec((G, CHUNK, CHUNK), lambda i: (0, 0, 0))
    return _call(body, grid=(S // ts,),
                 in_specs=[pl.BlockSpec((ts, 2 * C), lambda i: (i, 0)),
                           pl.BlockSpec((ts, C), lambda i: (i, 0)), vec, vec, wspec, wspec,
                           pl.BlockSpec((CHUNK, C), lambda i: (0, 0))],
                 out_specs=(pl.BlockSpec((ts, 2 * C), lambda i: (i, 0)), wspec,
                            pl.BlockSpec((CHUNK, LANES), lambda i: (0, 0)), vec, vec),
                 out_shape=(_sds((S, 2 * C), BF16), _sds((G, CHUNK, CHUNK), F32), _sds((CHUNK, LANES), F32),
                            _sds((1, C), F32), _sds((1, C), F32)),
                 scratch_shapes=[pltpu.VMEM((ts, C), F32), pltpu.VMEM((ts, C), F32)], name=name)(
        h_c, dout, ln_g.reshape(1, C), ln_b.reshape(1, C), w_tril, w_tril_t, bs_rows)


def _ffn_act_fwd(hu, cw, cb, *, name, ts, tc):
    S, F2 = hu.shape
    F = F2 // 2
    nj = F // tc
    HB = FFN_HALO
    nb = ts // HB

    def body(g_ref, v_ref, gh_ref, vh_ref, wg_ref, wv_ref, bg_ref, bv_ref, act_ref, dg_ref, dv_ref):
        i = pl.program_id(0)
        row = lax.broadcasted_iota(jnp.int32, (HB, tc), 0)

        def conv_chunk(x_ref, w_ref, b_ref, c, carry):
            cur = x_ref[c * HB:(c + 1) * HB, :]
            r1, r2 = pltpu.roll(cur, 1, 0), pltpu.roll(cur, 2, 0)
            x1 = jnp.where(row < 1, carry[0], r1)
            x2 = jnp.where(row < 2, carry[1], r2)
            out = (w_ref[0:1, :] * x2 + w_ref[1:2, :] * x1 + w_ref[2:3, :] * cur) + b_ref[...]
            return out, (r1, r2)

        def first_carry(h_ref):
            prev = jnp.where(i > 0, h_ref[...], 0.0)
            return pltpu.roll(prev, 1, 0), pltpu.roll(prev, 2, 0)

        cg, cv = first_carry(gh_ref), first_carry(vh_ref)
        for r0 in range(0, ts, FFN_ROWS):
            acts, dgs, dvs = [], [], []
            for c in range(r0 // HB, (r0 + FFN_ROWS) // HB):
                gc, cg = conv_chunk(g_ref, wg_ref, bg_ref, c, cg)
                vc, cv = conv_chunk(v_ref, wv_ref, bv_ref, c, cv)
                sg = jax.nn.sigmoid(gc)
                silu = gc * sg
                acts.append(silu * vc)
                dgs.append(vc * (sg * (1.0 + gc * (1.0 - sg))))
                dvs.append(silu)
            rows = slice(r0, r0 + FFN_ROWS)
            act_ref[rows, :] = jnp.concatenate(acts, axis=0).astype(BF16)
            dg_ref[rows, :] = jnp.concatenate(dgs, axis=0).astype(BF16)
            dv_ref[rows, :] = jnp.concatenate(dvs, axis=0).astype(BF16)

    prev = lambda off: (lambda i, j: (jnp.maximum(i * nb - 1, 0), j + off))
    out = pl.BlockSpec((ts, tc), lambda i, j: (i, j))
    return _call(body, grid=(S // ts, nj),
                 in_specs=[pl.BlockSpec((ts, tc), lambda i, j: (i, j)), pl.BlockSpec((ts, tc), lambda i, j: (i, j + nj)),
                           pl.BlockSpec((HB, tc), prev(0)), pl.BlockSpec((HB, tc), prev(nj)),
                           pl.BlockSpec((3, tc), lambda i, j: (0, j)), pl.BlockSpec((3, tc), lambda i, j: (0, j + nj)),
                           pl.BlockSpec((1, tc), lambda i, j: (0, j)), pl.BlockSpec((1, tc), lambda i, j: (0, j + nj))],
                 out_specs=(out, out, out), out_shape=(_sds((S, F), BF16),) * 3, name=name)(
        hu, hu, hu, hu, cw, cw, cb.reshape(1, F2), cb.reshape(1, F2))


def _ffn_act_bwd(hu, dact, dact_dg, dact_dv, cw, *, name, ts, tc):
    S, F2 = hu.shape
    F = F2 // 2
    nj = F // tc
    HB = FFN_HALO
    HB16 = 16
    n = S // ts

    def body(g_ref, v_ref, lg_ref, lv_ref, lgn_ref, lvn_ref, da_ref, dan_ref, wg_ref, wv_ref,
             dhg_ref, dhv_ref, dwg_ref, dwv_ref, dbg_ref, dbv_ref, dug_ref, duv_ref, accg, accv):
        i = pl.program_id(1)
        RC = FFN_ROWS
        npairs = ts // RC
        last_tile = i == n - 1

        @pl.when(i == 0)
        def _():
            accg[...] = jnp.zeros_like(accg)
            accv[...] = jnp.zeros_like(accv)

        for l_ref, ln_ref, h_ref, w_ref, dh_ref, acc in ((lg_ref, lgn_ref, g_ref, wg_ref, dhg_ref, accg),
                                                          (lv_ref, lvn_ref, v_ref, wv_ref, dhv_ref, accv)):
            for c0 in range(0, tc, FFN_STRIP):
                cols = slice(c0, min(c0 + FFN_STRIP, tc))
                sw = cols.stop - cols.start
                row = lax.broadcasted_iota(jnp.int32, (HB, sw), 0)
                w0, w1, w2 = w_ref[0:1, cols], w_ref[1:2, cols], w_ref[2:3, cols]

                def d_rows(p):
                    rows = slice(p * RC, (p + 1) * RC)
                    return da_ref[rows, cols] * l_ref[rows, cols].astype(F32)

                after = jnp.where(last_tile, 0.0, dan_ref[:, cols]) * ln_ref[:, cols].astype(F32)[0:HB, :]
                pair = d_rows(0)
                cur = pair[0:HB, :]
                c7, c6 = pltpu.roll(cur, HB - 1, 0), pltpu.roll(cur, HB - 2, 0)
                sums = [None] * 5
                for p in range(npairs):
                    nxt_pair = d_rows(p + 1) if p + 1 < npairs else None
                    dhus = []
                    for half, nxt_c in enumerate((pair[HB:RC, :], after if nxt_pair is None else nxt_pair[0:HB, :])):
                        c = 2 * p + half
                        n7, n6 = pltpu.roll(nxt_c, HB - 1, 0), pltpu.roll(nxt_c, HB - 2, 0)
                        taps = [jnp.where(row >= HB - 2, n6, c6), jnp.where(row >= HB - 1, n7, c7), cur]
                        dhu = w2 * taps[2] + w1 * taps[1] + w0 * taps[0]
                        dhus.append(dhu)
                        h = h_ref[c * HB:(c + 1) * HB, cols]
                        parts = [h * taps[0], h * taps[1], h * taps[2], taps[2], dhu]
                        sums = [q if s is None else s + q for s, q in zip(sums, parts)]
                        cur, c7, c6 = nxt_c, n7, n6
                    dh_ref[p * RC:(p + 1) * RC, cols] = jnp.concatenate(dhus, axis=0).astype(BF16)
                    pair = nxt_pair
                for k in range(5):
                    acc[8 * k:8 * k + 8, cols] += sums[k]

        @pl.when(i == n - 1)
        def _():
            for acc, dw_ref, db_ref, du_ref in ((accg, dwg_ref, dbg_ref, dug_ref), (accv, dwv_ref, dbv_ref, duv_ref)):
                for k in range(3):
                    dw_ref[k:k + 1, :] = _colsum(acc[8 * k:8 * k + 8, :])
                db_ref[...] = _colsum(acc[24:32, :])
                du_ref[...] = _colsum(acc[32:40, :])

    nxt = lambda hb: (lambda j, i: (jnp.minimum((i + 1) * (ts // hb), S // hb - 1), j))
    tile = lambda off: (lambda j, i: (i, j + off))
    vec = lambda rows: pl.BlockSpec((rows, tc), lambda j, i: (0, j))
    return _call(body, grid=(nj, n),
                 in_specs=[pl.BlockSpec((ts, tc), tile(0)), pl.BlockSpec((ts, tc), tile(nj)),
                           pl.BlockSpec((ts, tc), tile(0)), pl.BlockSpec((ts, tc), tile(0)),
                           pl.BlockSpec((HB16, tc), nxt(HB16)), pl.BlockSpec((HB16, tc), nxt(HB16)),
                           pl.BlockSpec((ts, tc), tile(0)), pl.BlockSpec((HB, tc), nxt(HB)),
                           pl.BlockSpec((3, tc), lambda j, i: (0, j)), pl.BlockSpec((3, tc), lambda j, i: (0, j + nj))],
                 out_specs=(pl.BlockSpec((ts, tc), tile(0)), pl.BlockSpec((ts, tc), tile(0)),
                            vec(3), vec(3), vec(1), vec(1), vec(1), vec(1)),
                 out_shape=(_sds((S, F), BF16), _sds((S, F), BF16), _sds((3, F), F32), _sds((3, F), F32),
                            _sds((1, F), F32), _sds((1, F), F32), _sds((1, F), F32), _sds((1, F), F32)),
                 scratch_shapes=[pltpu.VMEM((40, tc), F32), pltpu.VMEM((40, tc), F32)], name=name)(
        hu, hu, dact_dg, dact_dv, dact_dg, dact_dv, dact, dact, cw, cw)


def _t5_bucket(dist):
    max_exact = N_BUCKETS // 2
    d = np.maximum(dist, 1).astype(np.float64)
    large = max_exact + (np.log(d / max_exact) / math.log(MAX_DISTANCE / max_exact)
                         * (N_BUCKETS - max_exact)).astype(np.int32)
    large = np.minimum(large, N_BUCKETS - 1)
    return np.where(dist < max_exact, dist, large).astype(np.int32)


def _pattern_tables(window, dilation):
    qi = np.arange(ATTN_BLOCK)[:, None]
    kj = np.arange(2 * ATTN_BLOCK)[None, :]
    dist = qi + ATTN_BLOCK - kj
    valid = (dist >= 0) & (dist <= window // dilation)
    bucket = _t5_bucket(np.clip(dist, 0, None) * dilation)
    return bucket, valid


def _dilate_qkv(qkv, *, name, ts):
    S, C = qkv.shape
    dils = [d for (_, d) in PATTERNS if d > 1]

    def body(x_ref, nat_ref, *rest):
        outs, tmp = rest[:-1], rest[-1]
        nat_ref[...] = x_ref[...].astype(BF16)
        for j in range(C // LANES):
            cols = slice(j * LANES, (j + 1) * LANES)
            tmp[...] = x_ref[:, cols]
            for d, o_ref in zip(dils, outs):
                _dilate(tmp, o_ref, cols, d, ts, BF16)

    out_shape = (_sds((S, C), BF16),) + tuple(_sds((d, S // d, C), BF16) for d in dils)
    out_specs = (_dil_spec(1, ts, C),) + tuple(_dil_spec(d, ts, C) for d in dils)
    res = _call(body, grid=(S // ts,), in_specs=[_dil_spec(1, ts, C)], out_specs=out_specs, out_shape=out_shape,
                scratch_shapes=[pltpu.VMEM((ts, LANES), F32)], name=name)(qkv)
    return [res[0]] + [r.reshape(S, C) for r in res[1:]]


def _attn_group(S):
    nb_min = (S // PATTERNS[-1][1]) // ATTN_BLOCK
    return math.gcd(8, nb_min)


def _with_prev_block(ref, halo_ref, bi, B):
    if bi == 0:
        return jnp.concatenate([halo_ref[...], ref[0:B, :]], axis=0)
    return ref[(bi - 1) * B:(bi + 1) * B, :]


def _stack_heads(x, head0):
    zero = jnp.zeros_like(x)
    return jnp.concatenate([jnp.where(head0, x, zero), jnp.where(head0, zero, x)], axis=0)


def _attn_fwd(qkv, bias, *, name, nb, G):
    S = qkv.shape[0]
    B, HD = ATTN_BLOCK, HEAD_DIM
    GR = G * B
    ng = S // GR

    def body(q_ref, k_ref, kh_ref, v_ref, vh_ref, bias_ref, o_ref, lse_ref):
        g = pl.program_id(1)
        halo_ok = (g * G) % nb != 0
        col = lax.broadcasted_iota(jnp.int32, (B, 2 * B), 1)
        head0 = lax.broadcasted_iota(jnp.int32, (B, LANES), 1) < HD

        for bi in range(G):
            r0 = bi * B
            q2 = q_ref[r0:r0 + B, :]
            kk = _with_prev_block(k_ref, kh_ref, bi, B)
            vv = _with_prev_block(v_ref, vh_ref, bi, B)
            s_st = lax.dot_general(_stack_heads(q2, head0), kk, (((1,), (1,)), ((), ())), preferred_element_type=F32)
            ps, ls_, lses = [], [], []
            for hh in range(2):
                s = s_st[hh * B:(hh + 1) * B, :] + bias_ref[hh]
                if bi == 0:
                    s = jnp.where(jnp.logical_and(jnp.logical_not(halo_ok), col < B), NEG, s)
                m = jnp.max(s, axis=1, keepdims=True)
                p = jnp.exp(s - m)
                l = jnp.sum(p, axis=1, keepdims=True)
                ps.append(p.astype(BF16))
                ls_.append(l)
                lses.append(m + jnp.log(l))
            o_st = jnp.dot(jnp.concatenate(ps, axis=0), vv, preferred_element_type=F32)
            o_ref[r0:r0 + B, :] = (jnp.where(head0, o_st[0:B, :], o_st[B:2 * B, :])
                                   / jnp.where(head0, ls_[0], ls_[1])).astype(BF16)
            lse_ref[r0:r0 + B, :] = jnp.where(head0, lses[0], lses[1])

    halo = lambda off: (lambda hp, g: (jnp.maximum(g * G - 1, 0), off + hp))
    main = lambda off: (lambda hp, g: (g, off + hp))
    return _call(body, grid=(4, ng),
                 in_specs=[pl.BlockSpec((GR, LANES), main(0)), pl.BlockSpec((GR, LANES), main(4)),
                           pl.BlockSpec((B, LANES), halo(4)), pl.BlockSpec((GR, LANES), main(8)),
                           pl.BlockSpec((B, LANES), halo(8)), pl.BlockSpec((2, B, 2 * B), lambda hp, g: (hp, 0, 0))],
                 out_specs=(pl.BlockSpec((GR, LANES), main(0)), pl.BlockSpec((GR, LANES), main(0))),
                 out_shape=(_sds((S, ATTN_CH), BF16), _sds((S, ATTN_CH), F32)), name=name)(
        qkv, qkv, qkv, qkv, qkv, bias)


def _dil_spec(d, ts, C):
    if d == 1:
        return pl.BlockSpec((ts, C), lambda i: (i, 0))
    return pl.BlockSpec((d, ts // d, C), lambda i: (0, i, 0))


def _dil_view(a, d):
    return a if d == 1 else a.reshape(d, a.shape[0] // d, a.shape[1])


def _dilate(nat_tmp, dst_ref, cols, d, ts, dtype=F32):
    for r in range(d):
        dst_ref[r, :, cols] = nat_tmp[pl.ds(r, ts // d, stride=d), :].astype(dtype)


def _undilate(src_ref, nat_tmp, cols, d, ts, accumulate=False):
    for r in range(d):
        rows = pl.ds(r, ts // d, stride=d)
        if accumulate:
            nat_tmp[rows, :] = nat_tmp[rows, :] + src_ref[r, :, cols].astype(F32)
        else:
            nat_tmp[rows, :] = src_ref[r, :, cols].astype(F32)


def _attn_merge(o_list, lse_list, dils, *, name, ts):
    S, C = o_list[0].shape
    P = len(o_list)
    nd = sum(1 for d in dils if d > 1)

    def body(*refs):
        o_refs, l_refs = refs[:P], refs[P:2 * P]
        out_ref, lse_ref = refs[2 * P], refs[2 * P + 1]
        lse_d_refs = refs[2 * P + 2:2 * P + 2 + nd]
        scratch = list(refs[2 * P + 2 + nd:])
        tmp = scratch.pop()
        for j in range(C // LANES):
            cols = slice(j * LANES, (j + 1) * LANES)
            os_, ls, free = [], [], list(scratch)
            for o_ref, l_ref, d in zip(o_refs, l_refs, dils):
                if d > 1:
                    so, sl = free.pop(0), free.pop(0)
                    _undilate(o_ref, so, cols, d, ts)
                    _undilate(l_ref, sl, cols, d, ts)
                    os_.append(so[...])
                    ls.append(sl[...])
                else:
                    os_.append(o_ref[:, cols].astype(F32))
                    ls.append(l_ref[:, cols])
            m = ls[0]
            for l in ls[1:]:
                m = jnp.maximum(m, l)
            ws = [jnp.exp(l - m) for l in ls]
            den = ws[0]
            for w in ws[1:]:
                den = den + w
            num = ws[0] * os_[0]
            for w, o in zip(ws[1:], os_[1:]):
                num = num + w * o
            out_ref[:, cols] = num / den
            tmp[...] = m + jnp.log(den)
            lse_ref[:, cols] = tmp[...]
            for l_out, d in zip(lse_d_refs, [d for d in dils if d > 1]):
                _dilate(tmp, l_out, cols, d, ts)

    row = _dil_spec(1, ts, C)
    dd = [d for d in dils if d > 1]
    res = _call(body, grid=(S // ts,), in_specs=[_dil_spec(d, ts, C) for d in dils] * 2,
                out_specs=(row, row) + tuple(_dil_spec(d, ts, C) for d in dd),
                out_shape=(_sds((S, C), F32), _sds((S, C), F32)) + tuple(_sds((d, S // d, C), F32) for d in dd),
                scratch_shapes=[pltpu.VMEM((ts, LANES), F32)] * (2 * nd + 1), name=name)(
        *[_dil_view(o, d) for o, d in zip(o_list, dils)], *[_dil_view(l, d) for l, d in zip(lse_list, dils)])
    lse_by_d = {1: res[1]}
    lse_by_d.update({d: r.reshape(S, C) for d, r in zip(dd, res[2:])})
    return res[0], [lse_by_d[d] for d in dils]


def _attn_prep(dout, out, dils, *, name, ts):
    S, C = out.shape
    HD = HEAD_DIM
    dd = [d for d in dils if d > 1]
    nd = len(dd)

    def body(do_ref, o_ref, d_ref, dob_ref, *rest):
        outs, tmp_d, tmp_o = rest[:2 * nd], rest[2 * nd], rest[2 * nd + 1]
        dob_ref[...] = do_ref[...].astype(BF16)
        for j in range(C // LANES):
            cols = slice(j * LANES, (j + 1) * LANES)
            do = do_ref[:, cols]
            prod = do * o_ref[:, cols]
            tmp_o[...] = do
            for h in range(LANES // HD):
                cs = slice(h * HD, (h + 1) * HD)
                tmp_d[:, cs] = jnp.broadcast_to(jnp.sum(prod[:, cs], axis=1, keepdims=True), (ts, HD))
            d_ref[:, cols] = tmp_d[...]
            for d, dd_out, do_out in zip(dd, outs[:nd], outs[nd:]):
                _dilate(tmp_d, dd_out, cols, d, ts)
                _dilate(tmp_o, do_out, cols, d, ts, BF16)

    row = _dil_spec(1, ts, C)
    res = _call(body, grid=(S // ts,), in_specs=[row, row],
                out_specs=(row, row) + tuple(_dil_spec(d, ts, C) for d in dd) * 2,
                out_shape=(_sds((S, C), F32), _sds((S, C), BF16)) + tuple(_sds((d, S // d, C), F32) for d in dd)
                + tuple(_sds((d, S // d, C), BF16) for d in dd),
                scratch_shapes=[pltpu.VMEM((ts, LANES), F32)] * 2, name=name)(dout, out)
    dd_by_d, do_by_d = {1: res[0]}, {1: res[1]}
    dd_by_d.update({d: r.reshape(S, C) for d, r in zip(dd, res[2:2 + nd])})
    do_by_d.update({d: r.reshape(S, C) for d, r in zip(dd, res[2 + nd:])})
    return [dd_by_d[d] for d in dils], [do_by_d[d] for d in dils]


def _attn_bwd(qkv, do, lse, dd, bias, *, name, nb, G):
    S = qkv.shape[0]
    B, HD = ATTN_BLOCK, HEAD_DIM
    GR = G * B
    ng = S // GR
    nblk = S // B

    def body(q_ref, k_ref, kh_ref, v_ref, vh_ref, do_ref, lse_ref, dd_ref, qn_ref, don_ref, lsen_ref, ddn_ref,
             bias_ref, dq_ref, dk_ref, dv_ref, dbias_ref, ds_st, p_st):
        g = pl.program_id(1)
        halo_ok = (g * G) % nb != 0
        next_ok = jnp.logical_and(((g + 1) * G) % nb != 0, g < ng - 1)
        col = lax.broadcasted_iota(jnp.int32, (B, 2 * B), 1)

        def tn_dot(a, b):
            return lax.dot_general(a, b, (((0,), (0,)), ((), ())), preferred_element_type=F32)

        @pl.when(g == 0)
        def _():
            dbias_ref[...] = jnp.zeros_like(dbias_ref)

        head0 = lax.broadcasted_iota(jnp.int32, (B, LANES), 1) < HD

        def nt_dot(a, b):
            return lax.dot_general(a, b, (((1,), (1,)), ((), ())), preferred_element_type=F32)

        def one_head(x, hh):
            zero = jnp.zeros_like(x)
            return jnp.where(head0, x, zero) if hh == 0 else jnp.where(head0, zero, x)

        def pick(per_head):
            return jnp.where(head0, per_head[0], per_head[1])

        for bi in range(G):
            r0 = bi * B
            q2 = q_ref[r0:r0 + B, :]
            do2 = do_ref[r0:r0 + B, :]
            kk = _with_prev_block(k_ref, kh_ref, bi, B)
            vv = _with_prev_block(v_ref, vh_ref, bi, B)
            dq = []
            for hh in range(2):
                s = nt_dot(one_head(q2, hh), kk) + bias_ref[hh]
                if bi == 0:
                    s = jnp.where(jnp.logical_and(jnp.logical_not(halo_ok), col < B), NEG, s)
                p = jnp.exp(s - lse_ref[r0:r0 + B, hh * HD:hh * HD + 1])
                dp = nt_dot(one_head(do2, hh), vv)
                ds = p * (dp - dd_ref[r0:r0 + B, hh * HD:hh * HD + 1])
                dbias_ref[hh] += ds
                ds_bf, p_bf = ds.astype(BF16), p.astype(BF16)
                dq.append(jnp.dot(ds_bf, kk, preferred_element_type=F32))
                ds_st[hh, 2 * r0:2 * r0 + B, :] = ds_bf[:, B:]
                p_st[hh, 2 * r0:2 * r0 + B, :] = p_bf[:, B:]
                if bi > 0:
                    ds_st[hh, 2 * r0 - B:2 * r0, :] = ds_bf[:, :B]
                    p_st[hh, 2 * r0 - B:2 * r0, :] = p_bf[:, :B]
            dq_ref[r0:r0 + B, :] = pick(dq).astype(BF16)

        @pl.when(next_ok)
        def _():
            qn = qn_ref[...]
            don = don_ref[...]
            kl = k_ref[GR - B:GR, :]
            vl = v_ref[GR - B:GR, :]
            for hh in range(2):
                s = nt_dot(one_head(qn, hh), kl) + bias_ref[hh, :, 0:B]
                p = jnp.exp(s - lsen_ref[:, hh * HD:hh * HD + 1])
                dp = nt_dot(one_head(don, hh), vl)
                ds = p * (dp - ddn_ref[:, hh * HD:hh * HD + 1])
                ds_st[hh, 2 * GR - B:2 * GR, :] = ds.astype(BF16)
                p_st[hh, 2 * GR - B:2 * GR, :] = p.astype(BF16)

        @pl.when(jnp.logical_not(next_ok))
        def _():
            for hh in range(2):
                ds_st[hh, 2 * GR - B:2 * GR, :] = jnp.zeros((B, B), BF16)
                p_st[hh, 2 * GR - B:2 * GR, :] = jnp.zeros((B, B), BF16)

        for j in range(G):
            r0 = j * B
            if j < G - 1:
                q_pair, do_pair = q_ref[r0:r0 + 2 * B, :], do_ref[r0:r0 + 2 * B, :]
            else:
                q_pair = jnp.concatenate([q_ref[r0:r0 + B, :], qn_ref[...]], axis=0)
                do_pair = jnp.concatenate([do_ref[r0:r0 + B, :], don_ref[...]], axis=0)
            dk_ref[r0:r0 + B, :] = pick([tn_dot(ds_st[hh, 2 * r0:2 * r0 + 2 * B, :], q_pair)
                                         for hh in range(2)]).astype(BF16)
            dv_ref[r0:r0 + B, :] = pick([tn_dot(p_st[hh, 2 * r0:2 * r0 + 2 * B, :], do_pair)
                                         for hh in range(2)]).astype(BF16)

    halo = lambda off: (lambda hp, g: (jnp.maximum(g * G - 1, 0), off + hp))
    main = lambda off: (lambda hp, g: (g, off + hp))
    nxt = lambda off: (lambda hp, g: (jnp.minimum((g + 1) * G, nblk - 1), off + hp))
    big, small = (lambda m: pl.BlockSpec((GR, LANES), m)), (lambda m: pl.BlockSpec((B, LANES), m))
    return _call(body, grid=(4, ng),
                 in_specs=[big(main(0)), big(main(4)), small(halo(4)), big(main(8)), small(halo(8)),
                           big(main(0)), big(main(0)), big(main(0)),
                           small(nxt(0)), small(nxt(0)), small(nxt(0)), small(nxt(0)),
                           pl.BlockSpec((2, B, 2 * B), lambda hp, g: (hp, 0, 0))],
                 out_specs=(big(main(0)), big(main(0)), big(main(0)),
                            pl.BlockSpec((2, B, 2 * B), lambda hp, g: (hp, 0, 0))),
                 out_shape=(_sds((S, ATTN_CH), BF16),) * 3 + (_sds((ATTN_HEADS, B, 2 * B), F32),),
                 scratch_shapes=[pltpu.VMEM((2, 2 * GR, B), BF16)] * 2, name=name)(
        qkv, qkv, qkv, qkv, qkv, do, lse, dd, qkv, do, lse, dd, bias)


def _attn_combine(dq_list, dk_list, dv_list, dils, *, name, ts):
    S, C = dq_list[0].shape
    P = len(dq_list)
    scale = HEAD_DIM ** -0.5
    assert dils[0] == 1

    def body(*refs):
        out_refs, acc = refs[3 * P:3 * P + 3], refs[3 * P + 3]
        for part in range(3):
            for j in range(C // LANES):
                cols = slice(j * LANES, (j + 1) * LANES)
                acc[...] = refs[part * P][:, cols].astype(F32)
                for r, d in zip(refs[part * P + 1:(part + 1) * P], dils[1:]):
                    _undilate(r, acc, cols, d, ts, accumulate=True)
                tot = acc[...]
                if part == 0:
                    tot = tot * scale
                out_refs[part][:, cols] = tot.astype(BF16)

    return _call(body, grid=(S // ts,), in_specs=[_dil_spec(d, ts, C) for d in dils] * 3,
                 out_specs=(_dil_spec(1, ts, C),) * 3, out_shape=(_sds((S, C), BF16),) * 3,
                 scratch_shapes=[pltpu.VMEM((ts, LANES), F32)], name=name)(
        *[_dil_view(a, d) for lst in (dq_list, dk_list, dv_list) for a, d in zip(lst, dils)])


def _bias_tables(table, bucket_flat, *, name):
    P, _, K = bucket_flat.shape
    H = table.shape[1]
    KC = 4096

    def body(t_ref, bk_ref, out_ref):
        row = lax.broadcasted_iota(jnp.int32, (N_BUCKETS, KC), 0)
        for c in range(K // KC):
            bk = bk_ref[0, :, c * KC:(c + 1) * KC]
            onehot = (row == bk).astype(F32)
            vals = jnp.dot(t_ref[...], onehot, preferred_element_type=F32, precision=lax.Precision.HIGHEST)
            out_ref[0, :, c * KC:(c + 1) * KC] = jnp.where(bk >= 0, vals, NEG)

    return _call(body, grid=(P,),
                 in_specs=[pl.BlockSpec((H, N_BUCKETS), lambda p: (0, 0)), pl.BlockSpec((1, 1, K), lambda p: (p, 0, 0))],
                 out_specs=pl.BlockSpec((1, H, K), lambda p: (p, 0, 0)), out_shape=_sds((P, H, K), F32),
                 name=name)(table.T, bucket_flat)


def _bias_grad(dbias_flat, bucket_flat, *, name):
    P, H, K = dbias_flat.shape
    KC = 4096

    def body(db_ref, bk_ref, out_ref):
        p = pl.program_id(0)
        acc = jnp.zeros((N_BUCKETS, H), F32)
        row = lax.broadcasted_iota(jnp.int32, (N_BUCKETS, KC), 0)
        for c in range(K // KC):
            onehot = (row == bk_ref[0, :, c * KC:(c + 1) * KC]).astype(F32)
            acc = acc + lax.dot_general(onehot, db_ref[0, :, c * KC:(c + 1) * KC], (((1,), (1,)), ((), ())),
                                        preferred_element_type=F32, precision=lax.Precision.HIGHEST)

        @pl.when(p == 0)
        def _():
            out_ref[...] = acc

        @pl.when(p > 0)
        def _():
            out_ref[...] += acc

    return _call(body, grid=(P,),
                 in_specs=[pl.BlockSpec((1, H, K), lambda p: (p, 0, 0)), pl.BlockSpec((1, 1, K), lambda p: (p, 0, 0))],
                 out_specs=pl.BlockSpec((N_BUCKETS, H), lambda p: (0, 0)), out_shape=_sds((N_BUCKETS, H), F32),
                 name=name)(dbias_flat, bucket_flat)


def _allgather(blocks, *, name):
    n = len(blocks)

    def body(*refs):
        x_refs, out_refs = refs[:n], refs[n:2 * n]
        send_sems, recv_sems, local_sems = refs[2 * n:]
        x, y, c = lax.axis_index("x"), lax.axis_index("y"), lax.axis_index("c")
        me, sibling = (x, y, c), (x, y, 1 - c)
        chips = [(1 - x, y), (x, 1 - y), (1 - x, 1 - y)]

        def copy(i, k, blk, to, own=False):
            slot = out_refs[i].at[4 * blk[0] + 2 * blk[1] + blk[2]]
            return pltpu.make_async_remote_copy(src_ref=x_refs[i] if own else slot, dst_ref=slot,
                                                send_sem=send_sems.at[7 * i + k], recv_sem=recv_sems.at[7 * i + k],
                                                device_id=to, device_id_type=MESH)

        mine = [pltpu.make_async_copy(x_refs[i], out_refs[i].at[4 * x + 2 * y + c], local_sems.at[i])
                for i in range(n)]
        for cp in mine:
            cp.start()
        first = []
        for i in range(n):
            first.append(copy(i, 0, me, sibling, own=True))
            first += [copy(i, 1 + j, me, (*chip, c), own=True) for j, chip in enumerate(chips)]
        for cp in first:
            cp.start()
        passed = []
        for j, chip in enumerate(chips):
            for i in range(n):
                copy(i, 1 + j, (*chip, c), me).wait_recv()
                fwd = copy(i, 4 + j, (*chip, c), sibling)
                fwd.start()
                passed.append(fwd)
        for i in range(n):
            copy(i, 0, sibling, me).wait_recv()
            for j, chip in enumerate(chips):
                copy(i, 4 + j, (*chip, 1 - c), me).wait_recv()
        for cp in first + passed:
            cp.wait_send()
        for cp in mine:
            cp.wait()

    any_spec = pl.BlockSpec(memory_space=pl.ANY)
    return pl.pallas_call(body, out_shape=tuple(_sds((N_DEV,) + b.shape, b.dtype) for b in blocks),
                          in_specs=[any_spec] * n, out_specs=(any_spec,) * n,
                          scratch_shapes=[pltpu.SemaphoreType.DMA((7 * n,)), pltpu.SemaphoreType.DMA((7 * n,)),
                                          pltpu.SemaphoreType.DMA((n,))], name=name)(*blocks)


def _exchange(sends, *, name):
    n = len(sends)

    def body(*refs):
        send_refs, recv_refs = refs[:n], refs[n:2 * n]
        send_sems, recv_sems, local_sems = refs[2 * n:]
        x, y, c = lax.axis_index("x"), lax.axis_index("y"), lax.axis_index("c")
        me = 4 * x + 2 * y + c
        mine = [pltpu.make_async_copy(send_refs[i].at[me], recv_refs[i].at[me], local_sems.at[i]) for i in range(n)]
        for cp in mine:
            cp.start()
        copies = []
        for k in range(1, N_DEV):
            px = 1 - x if k & 4 else x
            py = 1 - y if k & 2 else y
            pc = 1 - c if k & 1 else c
            for i in range(n):
                cp = pltpu.make_async_remote_copy(src_ref=send_refs[i].at[4 * px + 2 * py + pc],
                                                  dst_ref=recv_refs[i].at[me],
                                                  send_sem=send_sems.at[7 * i + k - 1],
                                                  recv_sem=recv_sems.at[7 * i + k - 1],
                                                  device_id=(px, py, pc), device_id_type=MESH)
                cp.start()
                copies.append(cp)
        for cp in copies:
            cp.wait_recv()
        for cp in copies:
            cp.wait_send()
        for cp in mine:
            cp.wait()

    any_spec = pl.BlockSpec(memory_space=pl.ANY)
    return pl.pallas_call(body, out_shape=tuple(_sds(s.shape, s.dtype) for s in sends), in_specs=[any_spec] * n,
                          out_specs=(any_spec,) * n,
                          scratch_shapes=[pltpu.SemaphoreType.DMA((7 * n,)), pltpu.SemaphoreType.DMA((7 * n,)),
                                          pltpu.SemaphoreType.DMA((n,))], name=name)(*sends)


def _adamw(w, m, v, g_parts, *, name, tr):
    R, W = w.shape
    bc1 = 1.0 - ADAM_B1 ** ADAM_STEP
    bc2 = 1.0 - ADAM_B2 ** ADAM_STEP

    def body(w_ref, m_ref, v_ref, g_ref, go_ref, d_ref, mo_ref, vo_ref):
        g = g_ref[0].astype(F32)
        for i in range(1, N_DEV):
            g = g + g_ref[i].astype(F32)
        mn = ADAM_B1 * m_ref[...] + (1.0 - ADAM_B1) * g
        vn = ADAM_B2 * v_ref[...] + (1.0 - ADAM_B2) * (g * g)
        m_hat = mn / bc1
        v_hat = vn / bc2
        go_ref[...] = g
        d_ref[...] = -ADAM_LR * (m_hat / (jnp.sqrt(v_hat) + ADAM_EPS) + ADAM_WD * w_ref[...])
        mo_ref[...] = mn
        vo_ref[...] = vn

    row = pl.BlockSpec((tr, W), lambda i: (i, 0))
    return _call(body, grid=(R // tr,), in_specs=[row, row, row, pl.BlockSpec((N_DEV, tr, W), lambda i: (0, i, 0))],
                 out_specs=(row,) * 4, out_shape=(_sds((R, W), F32),) * 4, name=name)(w, m, v, g_parts)


def _round_up(n, k):
    return -(-n // k) * k


def _pack(arrs, width, row_mult):
    pieces, offs, r = [], [], 0
    for a in arrs:
        n = a.size
        rows = _round_up(n, width) // width
        flat = a.reshape(-1)
        if rows * width != n:
            flat = jnp.pad(flat, (0, rows * width - n))
        pieces.append(flat.reshape(rows, width))
        offs.append((r, rows, n))
        r += rows
    total = _round_up(r, row_mult)
    if total != r:
        pieces.append(jnp.zeros((total - r, width), pieces[0].dtype))
    return jnp.concatenate(pieces, axis=0), offs


def _unpack(pack, offs, shapes):
    out = []
    for (r, rows, n), shp in zip(offs, shapes):
        out.append(pack[r:r + rows].reshape(-1)[:n].reshape(shp))
    return out


def _gather_axis(full8, axis):
    moved = jnp.moveaxis(full8, 0, axis)
    shp = list(moved.shape)
    shp[axis:axis + 2] = [shp[axis] * shp[axis + 1]]
    return moved.reshape(shp)


def _split_axis(full, axis):
    shp = list(full.shape)
    shp[axis:axis + 1] = [N_DEV, shp[axis] // N_DEV]
    return jnp.moveaxis(full.reshape(shp), axis, 0)


def kernel(x, w_in, b_in, conv_dw_w, conv_dw_b, conv_ln_g, conv_ln_b, rel_bias_table, gmlp_ln_g, gmlp_ln_b, gmlp_w_s, gmlp_b_s, w_out, b_out, ln1_g, ln1_b, ffn_w_up, ffn_b_up, ffn_conv_w, ffn_conv_b, ffn_w_down, ffn_b_down, ln2_g, ln2_b, loss_target, m_w_in, m_b_in, m_conv_dw_w, m_conv_dw_b, m_conv_ln_g, m_conv_ln_b, m_rel_bias_table, m_gmlp_ln_g, m_gmlp_ln_b, m_gmlp_w_s, m_gmlp_b_s, m_w_out, m_b_out, m_ln1_g, m_ln1_b, m_ffn_w_up, m_ffn_b_up, m_ffn_conv_w, m_ffn_conv_b, m_ffn_w_down, m_ffn_b_down, m_ln2_g, m_ln2_b, v_w_in, v_b_in, v_conv_dw_w, v_conv_dw_b, v_conv_ln_g, v_conv_ln_b, v_rel_bias_table, v_gmlp_ln_g, v_gmlp_ln_b, v_gmlp_w_s, v_gmlp_b_s, v_w_out, v_b_out, v_ln1_g, v_ln1_b, v_ffn_w_up, v_ffn_b_up, v_ffn_conv_w, v_ffn_conv_b, v_ffn_w_down, v_ffn_b_down, v_ln2_g, v_ln2_b):
    W = dict(w_in=w_in, b_in=b_in, conv_dw_w=conv_dw_w, conv_dw_b=conv_dw_b, conv_ln_g=conv_ln_g,
             conv_ln_b=conv_ln_b, rel_bias_table=rel_bias_table, gmlp_ln_g=gmlp_ln_g, gmlp_ln_b=gmlp_ln_b,
             gmlp_w_s=gmlp_w_s, gmlp_b_s=gmlp_b_s, w_out=w_out, b_out=b_out, ln1_g=ln1_g, ln1_b=ln1_b,
             ffn_w_up=ffn_w_up, ffn_b_up=ffn_b_up, ffn_conv_w=ffn_conv_w, ffn_conv_b=ffn_conv_b,
             ffn_w_down=ffn_w_down, ffn_b_down=ffn_b_down, ln2_g=ln2_g, ln2_b=ln2_b)
    Mo = dict(w_in=m_w_in, b_in=m_b_in, conv_dw_w=m_conv_dw_w, conv_dw_b=m_conv_dw_b, conv_ln_g=m_conv_ln_g,
              conv_ln_b=m_conv_ln_b, rel_bias_table=m_rel_bias_table, gmlp_ln_g=m_gmlp_ln_g, gmlp_ln_b=m_gmlp_ln_b,
              gmlp_w_s=m_gmlp_w_s, gmlp_b_s=m_gmlp_b_s, w_out=m_w_out, b_out=m_b_out, ln1_g=m_ln1_g, ln1_b=m_ln1_b,
              ffn_w_up=m_ffn_w_up, ffn_b_up=m_ffn_b_up, ffn_conv_w=m_ffn_conv_w, ffn_conv_b=m_ffn_conv_b,
              ffn_w_down=m_ffn_w_down, ffn_b_down=m_ffn_b_down, ln2_g=m_ln2_g, ln2_b=m_ln2_b)
    Vo = dict(w_in=v_w_in, b_in=v_b_in, conv_dw_w=v_conv_dw_w, conv_dw_b=v_conv_dw_b, conv_ln_g=v_conv_ln_g,
              conv_ln_b=v_conv_ln_b, rel_bias_table=v_rel_bias_table, gmlp_ln_g=v_gmlp_ln_g, gmlp_ln_b=v_gmlp_ln_b,
              gmlp_w_s=v_gmlp_w_s, gmlp_b_s=v_gmlp_b_s, w_out=v_w_out, b_out=v_b_out, ln1_g=v_ln1_g, ln1_b=v_ln1_b,
              ffn_w_up=v_ffn_w_up, ffn_b_up=v_ffn_b_up, ffn_conv_w=v_ffn_conv_w, ffn_conv_b=v_ffn_conv_b,
              ffn_w_down=v_ffn_w_down, ffn_b_down=v_ffn_b_down, ln2_g=v_ln2_g, ln2_b=v_ln2_b)

    xs = x[0]
    target = loss_target[0]
    S, D = xs.shape
    F2 = ffn_b_up.shape[1]
    F = F2 // 2
    ts = min(512, S)
    ts_dil = min(1024, S)
    G = _attn_group(S)
    tc = F // 2 if (F // 2) % LANES == 0 else F

    mat_names = SHARDED[:4]
    payload = [W[n].astype(BF16) if n in mat_names else W[n] for n in SHARDED]
    wall = _allgather(payload, name="weight_allgather")
    full = {n: _gather_axis(parts, SHARD_AXIS[n]) for n, parts in zip(SHARDED, wall)}

    tables = [_pattern_tables(w, d) for (w, d) in PATTERNS]
    bucket_flat = jnp.asarray(np.stack([np.where(v, b, -1).reshape(1, -1) for (b, v) in tables]).astype(np.int32))
    bias_all = _bias_tables(rel_bias_table, bucket_flat, name="bias_tables")
    biases = [bias_all[p].reshape(ATTN_HEADS, ATTN_BLOCK, 2 * ATTN_BLOCK) for p in range(len(PATTERNS))]
    nbs = [(S // d) // ATTN_BLOCK for (_, d) in PATTERNS]
    dils = [d for (_, d) in PATTERNS]
    scale = HEAD_DIM ** -0.5

    saved = []
    cur = xs
    for l in range(DEPTH):
        Win, Wout, Wup, Wdown = full['w_in'][l], full['w_out'][l], full['ffn_w_up'][l], full['ffn_w_down'][l]
        qcols = slice(2 * CONV_CH, 2 * CONV_CH + ATTN_CH)
        Win_s = Win.at[:, qcols].multiply(scale)
        b_in_s = b_in[l].at[qcols].multiply(scale)
        h_a, qkv, h_c = _mm([cur], [Win_s], bias=b_in_s, tm=ts, name="in_proj",
                            splits=((2 * CONV_CH, F32, 1.0), (3 * ATTN_CH, F32, 1.0), (2 * GMLP_CH, F32, 1.0)))
        conv_out, hc = _conv_fwd(h_a, full['conv_dw_w'][l], conv_dw_b[l], conv_ln_g[l], conv_ln_b[l],
                                 name="conv_fwd", ts=ts)
        qkv_d = _dilate_qkv(qkv, name="dilate_qkv", ts=ts_dil)
        o_ps, lse_ps = [], []
        for p, d in enumerate(dils):
            o_p, lse_p = _attn_fwd(qkv_d[p], biases[p], name=f"attn_fwd_d{d}", nb=nbs[p], G=G)
            o_ps.append(o_p)
            lse_ps.append(lse_p)
        attn_out, lse = _attn_merge(o_ps, lse_ps, dils, name="attn_merge", ts=ts_dil)
        w_tril = jnp.tril(gmlp_w_s[l]).astype(BF16)
        bs_rows = jnp.repeat(gmlp_b_s[l].T, GMLP_GROUP_DIM, axis=1)
        gm_out = _gmlp_fwd(h_c, gmlp_ln_g[l], gmlp_ln_b[l], w_tril, bs_rows, name="gmlp_fwd", ts=ts)
        x1, xhat1, rstd1 = _mm([conv_out, attn_out, gm_out],
                               [Wout[:CONV_CH], Wout[CONV_CH:CONV_CH + ATTN_CH], Wout[CONV_CH + ATTN_CH:]],
                               bias=b_out[l], resid=cur, resid_scale=ALPHA, ln=(ln1_g[l], ln1_b[l]), tm=ts,
                               name="out_proj_ln")
        hu = _mm([x1], [Wup], bias=ffn_b_up[l], tm=ts, tn=F, name="ffn_up")
        act, act_dg, act_dv = _ffn_act_fwd(hu, full['ffn_conv_w'][l], ffn_conv_b[l], name="ffn_act_fwd",
                                           ts=ts, tc=tc)
        x2, xhat2, rstd2 = _mm([act], [Wdown], bias=ffn_b_down[l], resid=x1, resid_scale=ALPHA,
                               ln=(ln2_g[l], ln2_b[l]), tm=ts, name="ffn_down_ln")
        saved.append(dict(x0=cur, h_a=h_a, h_c=h_c, qkv_d=qkv_d, hc=hc, conv_out=conv_out, attn_out=attn_out,
                          lse=lse, gm_out=gm_out, w_tril=w_tril, bs_rows=bs_rows, x1=x1, xhat1=xhat1, rstd1=rstd1,
                          hu=hu, act=act, act_dg=act_dg, act_dv=act_dv, xhat2=xhat2, rstd2=rstd2))
        cur = x2

    grads = {n: [None] * DEPTH for n in WEIGHTS if n != 'rel_bias_table'}
    drel = None
    dx = None
    loss_part = None
    tk = min(1024, S)
    for l in reversed(range(DEPTH)):
        sv = saved[l]
        Win, Wout, Wup, Wdown = full['w_in'][l], full['w_out'][l], full['ffn_w_up'][l], full['ffn_w_down'][l]
        if dx is None:
            dz2, dg2, db2, dzs2, loss_part = _ln_bwd(sv['xhat2'], sv['rstd2'], ln2_g[l], b=ln2_b[l], target=target,
                                                     name="ln2_bwd_loss", ts=ts)
        else:
            dz2, dg2, db2, dzs2 = _ln_bwd(sv['xhat2'], sv['rstd2'], ln2_g[l], dy=dx, name="ln_bwd", ts=ts)
        grads['ln2_g'][l], grads['ln2_b'][l], grads['ffn_b_down'][l] = dg2[0], db2[0], dzs2[0]
        grads['ffn_w_down'][l] = _mm_tn(sv['act'], dz2, tm=F // 2 if (F // 2) % LANES == 0 else F, tn=D, tk=tk,
                                        name="dw_down")
        dact = _mm([dz2], [Wdown.T], tm=ts, name="dact")
        dhg, dhv, dwg, dwv, dbg, dbv, dug, duv = _ffn_act_bwd(sv['hu'], dact, sv['act_dg'], sv['act_dv'],
                                                              full['ffn_conv_w'][l], name="ffn_act_bwd",
                                                              ts=ts, tc=tc)
        grads['ffn_conv_w'][l] = jnp.concatenate([dwg, dwv], axis=1)
        grads['ffn_conv_b'][l] = jnp.concatenate([dbg, dbv], axis=1)[0]
        grads['ffn_b_up'][l] = jnp.concatenate([dug, duv], axis=1)[0]
        grads['ffn_w_up'][l] = jnp.concatenate(
            [_mm_tn(sv['x1'], dhg, tm=D, tn=tc, tk=tk, name="dw_up"),
             _mm_tn(sv['x1'], dhv, tm=D, tn=tc, tk=tk, name="dw_up")], axis=1)
        WupT = Wup.T
        dx1 = _mm([dhg, dhv], [WupT[:F], WupT[F:]], resid=dz2, resid_scale=ALPHA, tm=ts, name="dx1")
        dz1, dg1, db1, dzs1 = _ln_bwd(sv['xhat1'], sv['rstd1'], ln1_g[l], dy=dx1, name="ln_bwd", ts=ts)
        grads['ln1_g'][l], grads['ln1_b'][l], grads['b_out'][l] = dg1[0], db1[0], dzs1[0]
        grads['w_out'][l] = jnp.concatenate(
            [_mm_tn(sv['conv_out'], dz1, tm=CONV_CH, tn=D, tk=tk, name="dw_out_conv"),
             _mm_tn(sv['attn_out'], dz1, tm=ATTN_CH, tn=D, tk=tk, name="dw_out_attn"),
             _mm_tn(sv['gm_out'], dz1, tm=GMLP_CH, tn=D, tk=tk, name="dw_out_conv")], axis=0)
        dc_conv, dc_attn, dc_gm = _mm([dz1], [Wout.T], tm=ts, name="dcat",
                                      splits=((CONV_CH, F32, 1.0), (ATTN_CH, F32, 1.0), (GMLP_CH, F32, 1.0)))
        dh_a, ddw, ddwb, dclg, dclb = _conv_bwd(sv['h_a'], sv['hc'], dc_conv, full['conv_dw_w'][l], conv_ln_g[l],
                                                conv_ln_b[l], name="conv_bwd", ts=ts)
        grads['conv_dw_w'][l], grads['conv_dw_b'][l] = ddw[:CONV_WIDTH], ddwb[0]
        grads['conv_ln_g'][l], grads['conv_ln_b'][l] = dclg[0], dclb[0]
        dd_d, do_d = _attn_prep(dc_attn, sv['attn_out'], dils, name="attn_prep", ts=ts_dil)
        dqs, dks, dvs, dbs = [], [], [], []
        for p, d in enumerate(dils):
            dq, dk, dv, dbias = _attn_bwd(sv['qkv_d'][p], do_d[p], sv['lse'][p], dd_d[p], biases[p],
                                          name=f"attn_bwd_d{d}", nb=nbs[p], G=G)
            dqs.append(dq)
            dks.append(dk)
            dvs.append(dv)
            dbs.append(dbias.reshape(1, ATTN_HEADS, -1))
        dqkv = _attn_combine(dqs, dks, dvs, dils, name="attn_combine", ts=ts_dil)
        dr = _bias_grad(jnp.concatenate(dbs, axis=0), bucket_flat, name="bias_grad")
        drel = dr if drel is None else drel + dr
        w_tril_t = jnp.swapaxes(sv['w_tril'], 1, 2)
        dh_c, dws, dbs_acc, dglg, dglb = _gmlp_bwd(sv['h_c'], dc_gm, gmlp_ln_g[l], gmlp_ln_b[l], sv['w_tril'],
                                                   w_tril_t, sv['bs_rows'], name="gmlp_bwd", ts=ts)
        grads['gmlp_w_s'][l] = jnp.tril(dws)
        grads['gmlp_b_s'][l] = dbs_acc[:, :GMLP_GROUPS].T
        grads['gmlp_ln_g'][l], grads['gmlp_ln_b'][l] = dglg[0], dglb[0]
        dw_in = _mm_tn_shared(sv['x0'], [dh_a, *dqkv, dh_c], tk=tk, name="dw_in")
        grads['w_in'][l] = jnp.concatenate([w for w, _ in dw_in], axis=1)
        grads['b_in'][l] = jnp.concatenate([c for _, c in dw_in], axis=1)[0]
        WinT = Win.T
        edges = [0, 2 * CONV_CH] + [2 * CONV_CH + k * ATTN_CH for k in (1, 2, 3)] + [WinT.shape[0]]
        dx = _mm([dh_a, *dqkv, dh_c], [WinT[a:b] for a, b in zip(edges[:-1], edges[1:])], resid=dz1,
                 resid_scale=ALPHA, tm=ts, name="dx0")

    gfull = {n: jnp.stack(v) for n, v in grads.items()}
    gfull['rel_bias_table'] = drel

    sends = []
    for n in SHARDED:
        parts = _split_axis(gfull[n], SHARD_AXIS[n])
        sends.append(parts.reshape(N_DEV, -1, parts.shape[-1]).astype(BF16))
    recvs = _exchange(sends, name="grad_exchange")
    shard_out = [[], [], [], []]
    for n, recv in zip(SHARDED, recvs):
        shp = W[n].shape
        rows = recv.shape[1]
        tr = rows // 4 if rows % 64 == 0 else rows
        outs = _adamw(*[src[n].reshape(rows, shp[-1]) for src in (W, Mo, Vo)], recv, name=f"adamw_{n}", tr=tr)
        for kind in range(4):
            shard_out[kind].append(outs[kind].reshape(shp))

    gsmall, soffs = _pack([gfull[n] for n in SMALL], LANES, 8)
    gall = _allgather([gsmall], name="small_grad_allgather")[0]
    spacks = [_pack([src[n] for n in SMALL], LANES, 8)[0] for src in (W, Mo, Vo)]
    souts = _adamw(spacks[0], spacks[1], spacks[2], gall, name="adamw_small", tr=gsmall.shape[0])
    small_out = [_unpack(o, soffs, [W[n].shape for n in SMALL]) for o in souts]

    loss = lax.psum(loss_part[0, 0], ("x", "y", "c"))
    by_kind = []
    for kind in range(4):
        d = dict(zip(SHARDED, shard_out[kind]))
        d.update(zip(SMALL, small_out[kind]))
        by_kind.append([d[n] for n in WEIGHTS])
    return (loss, dx[None], *by_kind[0], *by_kind[1], *by_kind[2], *by_kind[3])
```
